```python
import jax
import jax.numpy as jnp
from jax import lax
import numpy as np

D_MODEL = 2048
BATCH = 2
SEQ = 8192
DEPTH = 1

HEAD_DIM = 128
GRID_W = 64
NA_HEADS = 8
NA_KH = 8
NA_KW = 16
NA_QC = 16
NA_KC = NA_QC + NA_KW
DIL_GROUPS = ((128, 1), (512, 4), (2048, 16))
DIL_HEADS_PER_GROUP = 4
DIL_HEADS = DIL_HEADS_PER_GROUP * len(DIL_GROUPS)
DIL_BLOCK = 128
N_GROUPS = 8
EXPERTS_PER_GROUP = 8
N_EXPERTS = N_GROUPS * EXPERTS_PER_GROUP
TOP_K = 2
D_EXPERT = D_MODEL // 4
MOE_BLOCK = 128
LN_EPS = 1e-5
DN_ALPHA = (2 * DEPTH) ** 0.25
DN_BETA = (8 * DEPTH) ** -0.25
NA_WIDTH = NA_HEADS * HEAD_DIM
DIL_WIDTH = DIL_HEADS * HEAD_DIM
DIL_OUT_WIDTH = DIL_HEADS_PER_GROUP * HEAD_DIM
IN_WIDTHS = (NA_WIDTH, NA_WIDTH, NA_WIDTH, DIL_WIDTH, DIL_WIDTH, DIL_WIDTH, D_MODEL, D_MODEL)

kernel_name = 'hybrid_natten_dilated_hmoe_deepnorm'


def layer_norm(x, g, b):
    xf = x.astype(jnp.float32)
    mu = jnp.mean(xf, axis=-1, keepdims=True)
    var = jnp.mean(jnp.square(xf - mu), axis=-1, keepdims=True)
    y = (xf - mu) * lax.rsqrt(var + LN_EPS)
    return (y * g.astype(jnp.float32) + b.astype(jnp.float32)).astype(x.dtype)


def alibi_slopes(n):
    return np.array([2.0 ** (-8.0 * (i + 1) / n) for i in range(n)], dtype=np.float32)


def neighborhood_attention(q, k, v, rpb):
    B, H, S, hd = q.shape
    rows = S // GRID_W
    kh = min(NA_KH, rows)
    nc = GRID_W // NA_QC
    K = kh * NA_KC
    chunk_c0 = np.arange(nc) * NA_QC
    kc_start = np.clip(chunk_c0 - NA_KW // 2, 0, GRID_W - NA_KC)
    key_cols = kc_start[:, None] + np.arange(NA_KC)[None, :]
    qcol = chunk_c0[:, None] + np.arange(NA_QC)[None, :]
    qcs = np.clip(qcol - NA_KW // 2, 0, GRID_W - NA_KW)
    kc_slot = np.tile(key_cols, (1, kh))
    valid = jnp.asarray((kc_slot[:, None, :] >= qcs[:, :, None]) &
                        (kc_slot[:, None, :] < qcs[:, :, None] + NA_KW))
    dc = jnp.asarray(np.clip(kc_slot[:, None, :] - qcol[:, :, None] + NA_KW - 1,
                             0, 2 * NA_KW - 2), jnp.int32)
    ri_slot = jnp.asarray(np.repeat(np.arange(kh), NA_KC), jnp.int32)
    qg = q.reshape(B, H, rows, nc, NA_QC, hd)
    kg = k.reshape(B, H, rows, GRID_W, hd)
    vg = v.reshape(B, H, rows, GRID_W, hd)
    scale = HEAD_DIM ** -0.5

    def row_block(r):
        start = jnp.clip(r - kh // 2, 0, rows - kh)
        qr = lax.dynamic_index_in_dim(qg, r, axis=2, keepdims=False)
        kr = lax.dynamic_slice_in_dim(kg, start, kh, axis=2)[:, :, :, key_cols]
        vr = lax.dynamic_slice_in_dim(vg, start, kh, axis=2)[:, :, :, key_cols]
        kb = kr.transpose(0, 1, 3, 2, 4, 5).reshape(B, H, nc, K, hd)
        vb = vr.transpose(0, 1, 3, 2, 4, 5).reshape(B, H, nc, K, hd)
        dr = start - r + ri_slot + (NA_KH - 1)
        bias = rpb[:, dr[None, None, :], dc].astype(jnp.float32)
        s = jnp.einsum('bhcqe,bhcke->bhcqk', qr, kb,
                       preferred_element_type=jnp.float32) * scale + bias
        s = jnp.where(valid, s, -jnp.inf)
        p = jax.nn.softmax(s, axis=-1).astype(v.dtype)
        o = jnp.einsum('bhcqk,bhcke->bhcqe', p, vb)
        return o.reshape(B, H, GRID_W, hd)

    out = lax.map(row_block, jnp.arange(rows, dtype=jnp.int32))
    return out.transpose(1, 2, 0, 3, 4).reshape(B, H, S, hd)


def dilated_window_attention(q, k, v, slopes, dilation, half_w):
    B, G, S, hd = q.shape
    L = S // dilation
    blk = min(DIL_BLOCK, L)
    nb = -(-L // blk)
    Lp = nb * blk
    K = blk + 2 * half_w

    def split(t):
        return t.reshape(B, G, L, dilation, hd).transpose(0, 1, 3, 2, 4)

    qs = jnp.pad(split(q), ((0, 0), (0, 0), (0, 0), (0, Lp - L), (0, 0)))
    qs = qs.reshape(B, G, dilation, nb, blk, hd)
    pad_kv = ((0, 0), (0, 0), (0, 0), (half_w, Lp - L + half_w), (0, 0))
    kidx = np.arange(nb)[:, None] * blk + np.arange(K)[None, :]
    kb = jnp.pad(split(k), pad_kv)[:, :, :, kidx]
    vb = jnp.pad(split(v), pad_kv)[:, :, :, kidx]
    qpos = np.arange(nb)[:, None] * blk + np.arange(blk)[None, :]
    kpos = kidx - half_w
    rel = kpos[:, None, :] - qpos[:, :, None]
    in_seq = (kpos[:, None, :] >= 0) & (kpos[:, None, :] < L)
    valid = jnp.asarray((np.abs(rel) <= half_w) & (in_seq | (rel == 0)))
    dist = jnp.asarray(np.abs(rel) * dilation, jnp.float32)
    alibi = -jnp.asarray(slopes, jnp.float32)[:, None, None, None, None] * dist
    s = jnp.einsum('bgdnqe,bgdnke->bgdnqk', qs, kb,
                   preferred_element_type=jnp.float32) * HEAD_DIM ** -0.5 + alibi
    s = jnp.where(valid, s, -jnp.inf)
    lse = jax.nn.logsumexp(s, axis=-1)
    p = jnp.exp(s - lse[..., None]).astype(v.dtype)
    o = jnp.einsum('bgdnqk,bgdnke->bgdnqe', p, vb)
    o = o.reshape(B, G, dilation, Lp, hd)[:, :, :, :L].transpose(0, 1, 3, 2, 4).reshape(B, G, S, hd)
    lse = lse.reshape(B, G, dilation, Lp)[..., :L].transpose(0, 1, 3, 2).reshape(B, G, S)
    return o, lse


def dilated_mixture(q, k, v):
    slopes = alibi_slopes(DIL_HEADS)
    G = DIL_HEADS_PER_GROUP
    outs, lses = [], []
    for g, (window, dil) in enumerate(DIL_GROUPS):
        sl = slice(g * G, (g + 1) * G)
        o, lse = dilated_window_attention(q[:, sl], k[:, sl], v[:, sl], slopes[sl],
                                          dil, (window // 2) // dil)
        outs.append(o)
        lses.append(lse)
    w = jax.nn.softmax(jnp.stack(lses), axis=0)
    o = jnp.sum(w[..., None] * jnp.stack(outs).astype(jnp.float32), axis=0)
    return o.astype(q.dtype)


def token_mixer(h, w_in, b_in, rpb, w_proj_a, w_proj_b, w_o, b_o):
    B, S, D = h.shape
    z = jnp.einsum('bsd,de->bse', h, w_in) + b_in
    splits = np.cumsum(IN_WIDTHS)[:-1].tolist()
    qa, ka, va, qb, kb, vb, ga, gb = jnp.split(z, splits, axis=-1)

    def heads(t, n):
        return t.reshape(B, S, n, HEAD_DIM).transpose(0, 2, 1, 3)

    ya = neighborhood_attention(heads(qa, NA_HEADS), heads(ka, NA_HEADS), heads(va, NA_HEADS), rpb)
    ya = ya.transpose(0, 2, 1, 3).reshape(B, S, NA_WIDTH) @ w_proj_a
    yb = dilated_mixture(heads(qb, DIL_HEADS), heads(kb, DIL_HEADS), heads(vb, DIL_HEADS))
    yb = yb.transpose(0, 2, 1, 3).reshape(B, S, DIL_OUT_WIDTH) @ w_proj_b
    m = jax.nn.sigmoid(ga) * ya + jax.nn.sigmoid(gb) * yb
    return m @ w_o + b_o


def hierarchical_moe(h, w_rg, b_rg, w_re, b_re, w_gate, w_up, w_down):
    B, S, D = h.shape
    N = B * S
    xf = h.reshape(N, D)
    g_logits = (xf @ w_rg + b_rg).astype(jnp.float32)
    g_sel = jnp.argmax(g_logits, axis=-1).astype(jnp.int32)
    g_prob = jnp.take_along_axis(jax.nn.softmax(g_logits, axis=-1), g_sel[:, None], axis=-1)
    e_logits = (xf @ w_re + b_re).astype(jnp.float32).reshape(N, N_GROUPS, EXPERTS_PER_GROUP)
    e_logits = jnp.take_along_axis(e_logits, g_sel[:, None, None], axis=1)[:, 0]
    top_val, top_idx = lax.top_k(e_logits, TOP_K)
    gate = g_prob * jax.nn.softmax(top_val, axis=-1)
    expert_id = (g_sel[:, None] * EXPERTS_PER_GROUP + top_idx).astype(jnp.int32)

    A = N * TOP_K
    e_flat = expert_id.reshape(A)
    w_flat = gate.reshape(A).astype(h.dtype)
    tok = jnp.repeat(jnp.arange(N, dtype=jnp.int32), TOP_K)
    order = jnp.argsort(e_flat)
    e_s, tok_s, w_s = e_flat[order], tok[order], w_flat[order]
    counts = jnp.bincount(e_flat, length=N_EXPERTS).astype(jnp.int32)
    starts = jnp.cumsum(counts) - counts
    pcounts = (counts + MOE_BLOCK - 1) // MOE_BLOCK * MOE_BLOCK
    pends = jnp.cumsum(pcounts)
    pstarts = pends - pcounts
    dest = pstarts[e_s] + (jnp.arange(A, dtype=jnp.int32) - starts[e_s])
    nb = -(-A // MOE_BLOCK) + N_EXPERTS
    P = nb * MOE_BLOCK
    slot_tok = jnp.full((P,), N, jnp.int32).at[dest].set(tok_s)
    slot_w = jnp.zeros((P,), h.dtype).at[dest].set(w_s)
    block_e = jnp.minimum(jnp.searchsorted(pends, jnp.arange(nb, dtype=jnp.int32) * MOE_BLOCK,
                                           side='right'), N_EXPERTS - 1).astype(jnp.int32)
    xpad = jnp.concatenate([xf, jnp.zeros((1, D), xf.dtype)], axis=0)
    xs = xpad[slot_tok].reshape(nb, MOE_BLOCK, D)

    def expert_block(args):
        xb, e = args
        hid = jax.nn.silu(xb @ w_gate[e]) * (xb @ w_up[e])
        return hid @ w_down[e]

    ys = lax.map(expert_block, (xs, block_e)).reshape(P, D) * slot_w[:, None]
    y = jax.ops.segment_sum(ys, slot_tok, num_segments=N + 1)[:N]
    return y.reshape(B, S, D)


def setup_inputs(seed: int = 0) -> dict:
    key = jax.random.key(seed)
    ks = jax.random.split(key, 24)

    def nrm(k, shape, scale):
        return jax.random.normal(k, shape, jnp.float32) * scale

    col_scale = np.concatenate([np.full((w,), DN_BETA if i in (2, 5) else 1.0, np.float32)
                                for i, w in enumerate(IN_WIDTHS)])
    d_in = int(sum(IN_WIDTHS))
    return {
        'x': nrm(ks[0], (BATCH, SEQ, D_MODEL), 1.0),
        'ln0_g': 1.0 + nrm(ks[1], (D_MODEL,), 0.02),
        'ln0_b': nrm(ks[2], (D_MODEL,), 0.02),
        'w_in': nrm(ks[3], (DEPTH, D_MODEL, d_in), D_MODEL ** -0.5) * jnp.asarray(col_scale),
        'b_in': nrm(ks[4], (DEPTH, d_in), 0.02),
        'rpb': nrm(ks[5], (DEPTH, NA_HEADS, 2 * NA_KH - 1, 2 * NA_KW - 1), 0.1),
        'w_proj_a': nrm(ks[6], (DEPTH, NA_WIDTH, D_MODEL), NA_WIDTH ** -0.5 * DN_BETA),
        'w_proj_b': nrm(ks[7], (DEPTH, DIL_OUT_WIDTH, D_MODEL), DIL_OUT_WIDTH ** -0.5 * DN_BETA),
        'w_o': nrm(ks[8], (DEPTH, D_MODEL, D_MODEL), D_MODEL ** -0.5 * DN_BETA),
        'b_o': nrm(ks[9], (DEPTH, D_MODEL), 0.02),
        'ln1_g': 1.0 + nrm(ks[10], (DEPTH, D_MODEL), 0.02),
        'ln1_b': nrm(ks[11], (DEPTH, D_MODEL), 0.02),
        'w_router_group': nrm(ks[12], (DEPTH, D_MODEL, N_GROUPS), D_MODEL ** -0.5),
        'b_router_group': nrm(ks[13], (DEPTH, N_GROUPS), 0.01),
        'w_router_expert': nrm(ks[14], (DEPTH, D_MODEL, N_EXPERTS), D_MODEL ** -0.5),
        'b_router_expert': nrm(ks[15], (DEPTH, N_EXPERTS), 0.01),
        'w_gate': nrm(ks[16], (DEPTH, N_EXPERTS, D_MODEL, D_EXPERT), D_MODEL ** -0.5 * DN_BETA),
        'w_up': nrm(ks[17], (DEPTH, N_EXPERTS, D_MODEL, D_EXPERT), D_MODEL ** -0.5 * DN_BETA),
        'w_down': nrm(ks[18], (DEPTH, N_EXPERTS, D_EXPERT, D_MODEL), D_EXPERT ** -0.5 * DN_BETA),
        'ln2_g': 1.0 + nrm(ks[19], (DEPTH, D_MODEL), 0.02),
        'ln2_b': nrm(ks[20], (DEPTH, D_MODEL), 0.02),
    }


def reference(x, ln0_g, ln0_b, w_in, b_in, rpb, w_proj_a, w_proj_b, w_o, b_o, ln1_g, ln1_b,
              w_router_group, b_router_group, w_router_expert, b_router_expert,
              w_gate, w_up, w_down, ln2_g, ln2_b):
    h = layer_norm(x, ln0_g, ln0_b)
    for l in range(DEPTH):
        mix = token_mixer(h, w_in[l], b_in[l], rpb[l], w_proj_a[l], w_proj_b[l], w_o[l], b_o[l])
        h = layer_norm(DN_ALPHA * h + mix, ln1_g[l], ln1_b[l])
        ffn = hierarchical_moe(h, w_router_group[l], b_router_group[l], w_router_expert[l],
                               b_router_expert[l], w_gate[l], w_up[l], w_down[l])
        h = layer_norm(DN_ALPHA * h + ffn, ln2_g[l], ln2_b[l])
    return h
```

```python
import functools

import numpy as np
import jax
import jax.numpy as jnp
from jax import lax
from jax.experimental import pallas as pl
from jax.experimental.pallas import tpu as pltpu

F32 = jnp.float32
BF16 = jnp.bfloat16

D_MODEL = 2048
HEAD_DIM = 128
GRID_W = 64
NA_HEADS = 8
NA_KH = 8
NA_KW = 16
DIL_GROUPS = ((128, 1), (512, 4), (2048, 16))
DIL_HEADS_PER_GROUP = 4
DIL_HEADS = DIL_HEADS_PER_GROUP * len(DIL_GROUPS)
N_GROUPS = 8
EXPERTS_PER_GROUP = 8
N_EXPERTS = N_GROUPS * EXPERTS_PER_GROUP
TOP_K = 2
D_EXPERT = D_MODEL // 4
LN_EPS = 1e-5
DEPTH = 1
DN_ALPHA = (2 * DEPTH) ** 0.25
NA_WIDTH = NA_HEADS * HEAD_DIM
DIL_WIDTH = DIL_HEADS * HEAD_DIM
DIL_OUT_WIDTH = DIL_HEADS_PER_GROUP * HEAD_DIM
QKV_WIDTH = 3 * NA_WIDTH + 3 * DIL_WIDTH
QKV_HEADS = QKV_WIDTH // HEAD_DIM
LANES = 128
CHUNKS = D_MODEL // LANES
MASK_VALUE = -1e30

QA_H, KA_H, VA_H = 0, NA_HEADS, 2 * NA_HEADS
QB_H = 3 * NA_HEADS
KB_H = QB_H + DIL_HEADS
VB_H = KB_H + DIL_HEADS

LN_TM = 512
INPROJ_TM, INPROJ_TN = 1024, 512
NA_QROWS = 4
NA_KROWS = 12
DIL_TQ = 256
DIL_HALF = 64
DIL_TK = DIL_TQ + 2 * DIL_HALF
MIX_TM, MIX_TN = 512, 512
OUT_TM = 256
ROUTER_PAD = 128
MOE_TB = 128
FIN_TM = 128
VMEM_LIMIT = 56 * 1024 * 1024


def _params(sem, limit=VMEM_LIMIT):
    return pltpu.CompilerParams(dimension_semantics=sem, vmem_limit_bytes=limit)


def _layer_norm(x, g, b):
    mu = jnp.mean(x, axis=-1, keepdims=True)
    xc = x - mu
    var = jnp.mean(xc * xc, axis=-1, keepdims=True)
    return xc * lax.rsqrt(var + LN_EPS) * g + b


def _store_row_chunks(ref, val):
    for c in range(CHUNKS):
        ref[:, c, :] = val[:, c * LANES:(c + 1) * LANES]


def _load_row_chunks(ref):
    return jnp.concatenate([ref[:, c, :] for c in range(CHUNKS)], axis=-1)


def _ln0_kernel(x_ref, g_ref, b_ref, h_ref, hb_ref):
    y = _layer_norm(x_ref[...], g_ref[...], b_ref[...])
    h_ref[...] = y
    hb_ref[...] = y.astype(BF16)


def _ln0(x, g, b):
    n, d = x.shape
    row = pl.BlockSpec((LN_TM, d), lambda i: (i, 0))
    vec = pl.BlockSpec((1, d), lambda i: (0, 0))
    return pl.pallas_call(
        _ln0_kernel,
        grid=(n // LN_TM,),
        in_specs=[row, vec, vec],
        out_specs=[row, row],
        out_shape=[jax.ShapeDtypeStruct((n, d), F32), jax.ShapeDtypeStruct((n, d), BF16)],
        compiler_params=_params(("parallel",)),
        name="ln0",
    )(x, g.reshape(1, d), b.reshape(1, d))


def _inproj_kernel(hb_ref, w_ref, b_ref, o_ref):
    acc = jnp.dot(hb_ref[...], w_ref[...], preferred_element_type=F32) + b_ref[...]
    for c in range(INPROJ_TN // LANES):
        o_ref[c] = acc[:, c * LANES:(c + 1) * LANES].astype(BF16)


def _inproj(hb, w, b):
    n, d = hb.shape
    width = w.shape[1]
    return pl.pallas_call(
        _inproj_kernel,
        grid=(n // INPROJ_TM, width // INPROJ_TN),
        in_specs=[pl.BlockSpec((INPROJ_TM, d), lambda i, j: (i, 0)),
                  pl.BlockSpec((d, INPROJ_TN), lambda i, j: (0, j)),
                  pl.BlockSpec((1, INPROJ_TN), lambda i, j: (0, j))],
        out_specs=pl.BlockSpec((INPROJ_TN // LANES, INPROJ_TM, LANES), lambda i, j: (j, i, 0)),
        out_shape=jax.ShapeDtypeStruct((width // LANES, n, LANES), BF16),
        compiler_params=_params(("parallel", "arbitrary")),
        name="inproj",
    )(hb, w, b.reshape(1, width))


def _na_bias_tables(rpb, rows):
    i = np.arange(NA_QROWS)[:, None, None, None]
    qc = np.arange(GRID_W)[None, :, None, None]
    j = np.arange(NA_KROWS)[None, None, :, None]
    kc = np.arange(GRID_W)[None, None, None, :]
    idx, valid = [], []
    for r0, ks in ((0, 0), (2 * NA_QROWS, 2 * NA_QROWS - NA_KH // 2), (rows - NA_QROWS, rows - NA_KROWS)):
        r, krow = r0 + i, ks + j
        start = np.clip(r - NA_KH // 2, 0, rows - NA_KH)
        v_row = (krow >= start) & (krow < start + NA_KH)
        dr = np.clip(krow - r + NA_KH - 1, 0, 2 * NA_KH - 2)
        qcs = np.clip(qc - NA_KW // 2, 0, GRID_W - NA_KW)
        v_col = (kc >= qcs) & (kc < qcs + NA_KW)
        dc = np.clip(kc - qc + NA_KW - 1, 0, 2 * NA_KW - 2)
        shape = (NA_QROWS, GRID_W, NA_KROWS, GRID_W)
        idx.append(np.broadcast_to(dr * (2 * NA_KW - 1) + dc, shape).reshape(NA_QROWS * GRID_W, -1))
        valid.append(np.broadcast_to(v_row & v_col, shape).reshape(NA_QROWS * GRID_W, -1))
    idx = jnp.asarray(np.stack(idx), jnp.int32)
    valid = jnp.asarray(np.stack(valid))
    flat = rpb.reshape(rpb.shape[0], -1).astype(F32)
    vals = jnp.take(flat, idx, axis=1)
    return jnp.where(valid[None], vals, MASK_VALUE).transpose(1, 0, 2, 3)


def _na_kernel(q_ref, k_ref, v_ref, tab_ref, o_ref, *, rows):
    i = pl.program_id(2)
    last = pl.num_programs(2) - 1
    ks = jnp.clip(i * NA_QROWS - NA_KH // 2, 0, rows - NA_KROWS)
    off = pl.multiple_of(ks * GRID_W, GRID_W)
    kw = k_ref[0, 0, pl.ds(off, NA_KROWS * GRID_W), :]
    vw = v_ref[0, 0, pl.ds(off, NA_KROWS * GRID_W), :]
    variant = jnp.where(i == 0, 0, jnp.where(i == last, 2, 1))
    s = lax.dot_general(q_ref[0, 0], kw, (((1,), (1,)), ((), ())), preferred_element_type=F32)
    s = s + tab_ref[variant, 0]
    m = jnp.max(s, axis=-1, keepdims=True)
    p = jnp.exp(s - m)
    l = jnp.sum(p, axis=-1, keepdims=True)
    o = jnp.dot(p.astype(BF16), vw, preferred_element_type=F32) / l
    o_ref[0, 0] = o.astype(BF16)


def _neighborhood_attention(zq4, tabs):
    _, bsz, s, _ = zq4.shape
    rows = s // GRID_W
    assert rows % NA_QROWS == 0 and rows >= NA_KROWS + NA_QROWS
    tq = NA_QROWS * GRID_W
    tk = NA_KROWS * GRID_W
    return pl.pallas_call(
        functools.partial(_na_kernel, rows=rows),
        grid=(bsz, NA_HEADS, rows // NA_QROWS),
        in_specs=[pl.BlockSpec((1, 1, tq, HEAD_DIM), lambda b, h, i: (QA_H + h, b, i, 0)),
                  pl.BlockSpec((1, 1, s, HEAD_DIM), lambda b, h, i: (KA_H + h, b, 0, 0)),
                  pl.BlockSpec((1, 1, s, HEAD_DIM), lambda b, h, i: (VA_H + h, b, 0, 0)),
                  pl.BlockSpec((3, 1, tq, tk), lambda b, h, i: (0, h, 0, 0))],
        out_specs=pl.BlockSpec((1, 1, tq, HEAD_DIM), lambda b, h, i: (h, b, i, 0)),
        out_shape=jax.ShapeDtypeStruct((NA_HEADS, bsz, s, HEAD_DIM), BF16),
        compiler_params=_params(("parallel", "parallel", "arbitrary")),
        name="na_attn",
    )(zq4, zq4, zq4, tabs)


def _alibi_slopes(n):
    return np.array([2.0 ** (-8.0 * (i + 1) / n) for i in range(n)], dtype=np.float32)


def _dil_kernel(slope_ref, q_ref, k_ref, v_ref, o_ref, lse_ref, *, dil, length):
    n = pl.program_id(3)
    ks = jnp.clip(n * DIL_TQ - DIL_HALF, 0, length - DIL_TK)
    off = pl.multiple_of(ks, DIL_HALF)
    kw = k_ref[0, 0, pl.ds(off, DIL_TK), :]
    vw = v_ref[0, 0, pl.ds(off, DIL_TK), :]
    s = lax.dot_general(q_ref[0, 0], kw, (((1,), (1,)), ((), ())), preferred_element_type=F32)
    qi = lax.broadcasted_iota(jnp.int32, (DIL_TQ, DIL_TK), 0)
    kj = lax.broadcasted_iota(jnp.int32, (DIL_TQ, DIL_TK), 1)
    dist = jnp.abs(kj - qi + (ks - n * DIL_TQ))
    penalty = slope_ref[pl.program_id(1)] * float(dil)
    s = jnp.where(dist <= DIL_HALF, s - penalty * dist.astype(F32), MASK_VALUE)
    m = jnp.max(s, axis=-1, keepdims=True)
    p = jnp.exp(s - m)
    l = jnp.sum(p, axis=-1, keepdims=True)
    o = jnp.dot(p.astype(BF16), vw, preferred_element_type=F32) / l
    o_ref[0, 0] = o.astype(BF16)
    lse_ref[0, 0] = jnp.broadcast_to(m + jnp.log(l), (DIL_TQ, LANES))


def _dilated_group(zq, bsz, s, g, slopes):
    window, dil = DIL_GROUPS[g]
    assert (window // 2) // dil == DIL_HALF
    length = s // dil
    assert length % DIL_TQ == 0 and length >= DIL_TK
    gh = DIL_HEADS_PER_GROUP
    view = zq.reshape(QKV_HEADS, bsz, length, dil * HEAD_DIM)
    qspec = lambda base: pl.BlockSpec((1, 1, DIL_TQ, HEAD_DIM),
                                      lambda b, h, d, n: (base + g * gh + h, b, n, d))
    kvspec = lambda base: pl.BlockSpec((1, 1, length, HEAD_DIM),
                                       lambda b, h, d, n: (base + g * gh + h, b, 0, d))
    ospec = pl.BlockSpec((1, 1, DIL_TQ, HEAD_DIM), lambda b, h, d, n: (h, b, n, d))
    o, lse = pl.pallas_call(
        functools.partial(_dil_kernel, dil=dil, length=length),
        grid=(bsz, gh, dil, length // DIL_TQ),
        in_specs=[pl.BlockSpec(memory_space=pltpu.SMEM), qspec(QB_H), kvspec(KB_H), kvspec(VB_H)],
        out_specs=[ospec, ospec],
        out_shape=[jax.ShapeDtypeStruct((gh, bsz, length, dil * HEAD_DIM), BF16),
                   jax.ShapeDtypeStruct((gh, bsz, length, dil * HEAD_DIM), F32)],
        compiler_params=_params(("parallel", "parallel", "parallel", "arbitrary")),
        name=f"dil_attn_{dil}",
    )(jnp.asarray(slopes[g * gh:(g + 1) * gh]), view, view, view)
    n = bsz * s
    return o.reshape(gh, n, HEAD_DIM), lse.reshape(gh, n, HEAD_DIM)


def _mix_kernel(na_ref, o1_ref, o2_ref, o3_ref, l1_ref, l2_ref, l3_ref, hb_ref,
                wpa_ref, wpb_ref, wga_ref, wgb_ref, bga_ref, bgb_ref, m_ref, na_s, dil_s):
    @pl.when(pl.program_id(1) == 0)
    def _():
        for h in range(NA_HEADS):
            na_s[:, h * HEAD_DIM:(h + 1) * HEAD_DIM] = na_ref[h]
        for h in range(DIL_HEADS_PER_GROUP):
            ls = [l1_ref[h], l2_ref[h], l3_ref[h]]
            os_ = [o1_ref[h], o2_ref[h], o3_ref[h]]
            mx = jnp.maximum(jnp.maximum(ls[0], ls[1]), ls[2])
            es = [jnp.exp(l - mx) for l in ls]
            den = es[0] + es[1] + es[2]
            acc = (es[0] * os_[0].astype(F32) + es[1] * os_[1].astype(F32)
                   + es[2] * os_[2].astype(F32)) / den
            dil_s[:, h * HEAD_DIM:(h + 1) * HEAD_DIM] = acc.astype(BF16)

    hb = hb_ref[...]
    ya = jnp.dot(na_s[...], wpa_ref[...], preferred_element_type=F32)
    yb = jnp.dot(dil_s[...], wpb_ref[...], preferred_element_type=F32)
    ga = jnp.dot(hb, wga_ref[...], preferred_element_type=F32) + bga_ref[...]
    gb = jnp.dot(hb, wgb_ref[...], preferred_element_type=F32) + bgb_ref[...]
    m_ref[...] = (jax.nn.sigmoid(ga) * ya + jax.nn.sigmoid(gb) * yb).astype(BF16)


def _mix(na, dil_o, dil_lse, hb, wpa, wpb, wga, wgb, bga, bgb):
    n, d = hb.shape
    tm, tn = MIX_TM, MIX_TN
    heads = lambda nh: pl.BlockSpec((nh, tm, HEAD_DIM), lambda i, j: (0, i, 0))
    col = lambda k: pl.BlockSpec((k, tn), lambda i, j: (0, j))
    gh = DIL_HEADS_PER_GROUP
    return pl.pallas_call(
        _mix_kernel,
        grid=(n // tm, d // tn),
        in_specs=[heads(NA_HEADS), heads(gh), heads(gh), heads(gh), heads(gh), heads(gh), heads(gh),
                  pl.BlockSpec((tm, d), lambda i, j: (i, 0)),
                  col(NA_WIDTH), col(DIL_OUT_WIDTH), col(d), col(d), col(1), col(1)],
        out_specs=pl.BlockSpec((tm, tn), lambda i, j: (i, j)),
        out_shape=jax.ShapeDtypeStruct((n, d), BF16),
        scratch_shapes=[pltpu.VMEM((tm, NA_WIDTH), BF16), pltpu.VMEM((tm, DIL_OUT_WIDTH), BF16)],
        compiler_params=_params(("parallel", "arbitrary")),
        name="mix",
    )(na, *dil_o, *dil_lse, hb, wpa, wpb, wga, wgb, bga.reshape(1, d), bgb.reshape(1, d))


def _outproj_kernel(m_ref, h_ref, wo_ref, bo_ref, g_ref, b_ref, wrh_ref, wrl_ref, br_ref,
                    h1_ref, lg_ref):
    mix = jnp.dot(m_ref[...], wo_ref[...], preferred_element_type=F32) + bo_ref[...]
    h1 = _layer_norm(DN_ALPHA * h_ref[...] + mix, g_ref[...], b_ref[...])
    _store_row_chunks(h1_ref, h1)
    hi = h1.astype(BF16)
    lo = (h1 - hi.astype(F32)).astype(BF16)
    lg = (jnp.dot(hi, wrh_ref[...], preferred_element_type=F32)
          + jnp.dot(lo, wrh_ref[...], preferred_element_type=F32)
          + jnp.dot(hi, wrl_ref[...], preferred_element_type=F32))
    lg_ref[...] = lg + br_ref[...]


def _outproj(m, h, wo, bo, g, b, wr_hi, wr_lo, br):
    n, d = h.shape
    tm = OUT_TM
    row = lambda w: pl.BlockSpec((tm, w), lambda i: (i, 0))
    full = lambda r, c: pl.BlockSpec((r, c), lambda i: (0, 0))
    return pl.pallas_call(
        _outproj_kernel,
        grid=(n // tm,),
        in_specs=[row(d), row(d), full(d, d), full(1, d), full(1, d), full(1, d),
                  full(d, ROUTER_PAD), full(d, ROUTER_PAD), full(1, ROUTER_PAD)],
        out_specs=[pl.BlockSpec((tm, CHUNKS, LANES), lambda i: (i, 0, 0)), row(ROUTER_PAD)],
        out_shape=[jax.ShapeDtypeStruct((n, CHUNKS, LANES), F32),
                   jax.ShapeDtypeStruct((n, ROUTER_PAD), F32)],
        compiler_params=_params(("parallel",)),
        name="outproj",
    )(m, h, wo, bo.reshape(1, d), g.reshape(1, d), b.reshape(1, d), wr_hi, wr_lo, br)


def _route(logits, n):
    g_logits = logits[:, :N_GROUPS]
    g_sel = jnp.argmax(g_logits, axis=-1).astype(jnp.int32)
    g_prob = jnp.take_along_axis(jax.nn.softmax(g_logits, axis=-1), g_sel[:, None], axis=-1)
    e_logits = logits[:, N_GROUPS:N_GROUPS + N_EXPERTS].reshape(n, N_GROUPS, EXPERTS_PER_GROUP)
    e_logits = jnp.take_along_axis(e_logits, g_sel[:, None, None], axis=1)[:, 0]
    top_val, top_idx = lax.top_k(e_logits, TOP_K)
    gate = g_prob * jax.nn.softmax(top_val, axis=-1)
    expert_id = (g_sel[:, None] * EXPERTS_PER_GROUP + top_idx).astype(jnp.int32)

    a = n * TOP_K
    e_flat = expert_id.reshape(a)
    w_flat = gate.reshape(a)
    onehot = (e_flat[:, None] == jnp.arange(N_EXPERTS, dtype=jnp.int32)[None, :]).astype(jnp.int32)
    csum = jnp.cumsum(onehot, axis=0)
    rank = jnp.take_along_axis(csum, e_flat[:, None], axis=1)[:, 0] - 1
    counts = csum[-1]
    pcounts = (counts + MOE_TB - 1) // MOE_TB * MOE_TB
    pends = jnp.cumsum(pcounts)
    pstarts = pends - pcounts
    dest = (pstarts[e_flat] + rank).astype(jnp.int32)
    nb = a // MOE_TB + N_EXPERTS
    slots = nb * MOE_TB
    tok = jnp.arange(a, dtype=jnp.int32) // TOP_K
    slot_tok = jnp.zeros((slots,), jnp.int32).at[dest].set(tok)
    slot_w = jnp.zeros((slots,), F32).at[dest].set(w_flat)
    block_e = jnp.minimum(jnp.searchsorted(pends, jnp.arange(nb, dtype=jnp.int32) * MOE_TB,
                                           side='right'), N_EXPERTS - 1).astype(jnp.int32)
    n_active = (pends[-1] // MOE_TB).astype(jnp.int32).reshape(1)
    return dest, slot_tok, slot_w, block_e, n_active


def _gather_rows(idx_ref, base, count, src_hbm, buf, sem):
    def copy(t):
        return pltpu.make_async_copy(src_hbm.at[idx_ref[base + t]],
                                     buf.at[pl.ds(t * CHUNKS, CHUNKS), :], sem)

    def start(t, c):
        copy(t).start()
        return c

    def wait(t, c):
        copy(t).wait()
        return c

    lax.fori_loop(0, count, start, 0)
    lax.fori_loop(0, count, wait, 0)


def _rows_to_tile(buf, count):
    return jnp.concatenate([buf[pl.ds(c, count, stride=CHUNKS), :] for c in range(CHUNKS)], axis=-1)


def _expert_kernel(be_ref, nact_ref, tok_ref, h1_hbm, sw_ref, wg_ref, wu_ref, wd_ref, ys_ref,
                   xbuf, wg_s, wu_s, wd_s, sem):
    i = pl.program_id(0)

    @pl.when(i < nact_ref[0])
    def _():
        prev = be_ref[jnp.maximum(i - 1, 0)]

        @pl.when((i == 0) | (be_ref[i] != prev))
        def _():
            wg_s[...] = wg_ref[0].astype(BF16)
            wu_s[...] = wu_ref[0].astype(BF16)
            wd_s[...] = wd_ref[0].astype(BF16)

        _gather_rows(tok_ref, i * MOE_TB, MOE_TB, h1_hbm, xbuf, sem)
        xb = _rows_to_tile(xbuf, MOE_TB).astype(BF16)
        gate = jnp.dot(xb, wg_s[...], preferred_element_type=F32)
        up = jnp.dot(xb, wu_s[...], preferred_element_type=F32)
        hid = (jax.nn.silu(gate) * up).astype(BF16)
        y = jnp.dot(hid, wd_s[...], preferred_element_type=F32)
        _store_row_chunks(ys_ref, y * sw_ref[...])

    @pl.when(i >= nact_ref[0])
    def _():
        ys_ref[...] = jnp.zeros(ys_ref.shape, F32)


def _experts(h1, slot_tok, slot_w, block_e, n_active, w_gate, w_up, w_down):
    n, d = h1.shape[0], D_MODEL
    slots = slot_tok.shape[0]
    nb = slots // MOE_TB
    grid_spec = pltpu.PrefetchScalarGridSpec(
        num_scalar_prefetch=3,
        grid=(nb,),
        in_specs=[pl.BlockSpec(memory_space=pl.ANY),
                  pl.BlockSpec((MOE_TB, 1), lambda i, be, na, tk: (i, 0)),
                  pl.BlockSpec((1, d, D_EXPERT), lambda i, be, na, tk: (be[i], 0, 0)),
                  pl.BlockSpec((1, d, D_EXPERT), lambda i, be, na, tk: (be[i], 0, 0)),
                  pl.BlockSpec((1, D_EXPERT, d), lambda i, be, na, tk: (be[i], 0, 0))],
        out_specs=pl.BlockSpec((MOE_TB, CHUNKS, LANES), lambda i, be, na, tk: (i, 0, 0)),
        scratch_shapes=[pltpu.VMEM((MOE_TB * CHUNKS, LANES), F32),
                        pltpu.VMEM((d, D_EXPERT), BF16), pltpu.VMEM((d, D_EXPERT), BF16),
                        pltpu.VMEM((D_EXPERT, d), BF16), pltpu.SemaphoreType.DMA(())])
    return pl.pallas_call(
        _expert_kernel,
        grid_spec=grid_spec,
        out_shape=jax.ShapeDtypeStruct((slots, CHUNKS, LANES), F32),
        compiler_params=_params(("arbitrary",)),
        name="experts",
    )(block_e, n_active, slot_tok, h1, slot_w.reshape(slots, 1), w_gate, w_up, w_down)


def _final_kernel(dest_ref, ys_hbm, h1_ref, g_ref, b_ref, o_ref, buf, sem):
    i = pl.program_id(0)
    _gather_rows(dest_ref, i * FIN_TM * TOP_K, FIN_TM * TOP_K, ys_hbm, buf, sem)
    parts = []
    for c in range(CHUNKS):
        y0 = buf[pl.ds(c, FIN_TM, stride=TOP_K * CHUNKS), :]
        y1 = buf[pl.ds(CHUNKS + c, FIN_TM, stride=TOP_K * CHUNKS), :]
        parts.append(y0 + y1)
    ffn = jnp.concatenate(parts, axis=-1)
    o_ref[...] = _layer_norm(DN_ALPHA * _load_row_chunks(h1_ref) + ffn, g_ref[...], b_ref[...])


def _final(dest, ys, h1, g, b):
    n, d = h1.shape[0], D_MODEL
    grid_spec = pltpu.PrefetchScalarGridSpec(
        num_scalar_prefetch=1,
        grid=(n // FIN_TM,),
        in_specs=[pl.BlockSpec(memory_space=pl.ANY),
                  pl.BlockSpec((FIN_TM, CHUNKS, LANES), lambda i, ds: (i, 0, 0)),
                  pl.BlockSpec((1, d), lambda i, ds: (0, 0)),
                  pl.BlockSpec((1, d), lambda i, ds: (0, 0))],
        out_specs=pl.BlockSpec((FIN_TM, d), lambda i, ds: (i, 0)),
        scratch_shapes=[pltpu.VMEM((FIN_TM * TOP_K * CHUNKS, LANES), F32),
                        pltpu.SemaphoreType.DMA(())])
    return pl.pallas_call(
        _final_kernel,
        grid_spec=grid_spec,
        out_shape=jax.ShapeDtypeStruct((n, d), F32),
        compiler_params=_params(("arbitrary",)),
        name="final",
    )(dest, ys, h1, g.reshape(1, d), b.reshape(1, d))


def kernel(x, ln0_g, ln0_b, w_in, b_in, rpb, w_proj_a, w_proj_b, w_o, b_o, ln1_g, ln1_b,
           w_router_group, b_router_group, w_router_expert, b_router_expert,
           w_gate, w_up, w_down, ln2_g, ln2_b):
    bsz, s, d = x.shape
    n = bsz * s
    assert d == D_MODEL and w_in.shape[0] == DEPTH
    scale = HEAD_DIM ** -0.5

    col_scale = np.ones((QKV_WIDTH,), np.float32)
    col_scale[:NA_WIDTH] = scale
    col_scale[3 * NA_WIDTH:3 * NA_WIDTH + DIL_WIDTH] = scale
    w_qkv = (w_in[0][:, :QKV_WIDTH] * col_scale).astype(BF16)
    b_qkv = b_in[0][:QKV_WIDTH] * col_scale
    w_ga = w_in[0][:, QKV_WIDTH:QKV_WIDTH + d].astype(BF16)
    w_gb = w_in[0][:, QKV_WIDTH + d:].astype(BF16)
    b_ga = b_in[0][QKV_WIDTH:QKV_WIDTH + d]
    b_gb = b_in[0][QKV_WIDTH + d:]
    w_r = jnp.concatenate([w_router_group[0], w_router_expert[0]], axis=1)
    w_r = jnp.pad(w_r, ((0, 0), (0, ROUTER_PAD - w_r.shape[1])))
    w_r_hi = w_r.astype(BF16)
    w_r_lo = (w_r - w_r_hi.astype(F32)).astype(BF16)
    b_r = jnp.pad(jnp.concatenate([b_router_group[0], b_router_expert[0]]),
                  (0, ROUTER_PAD - N_GROUPS - N_EXPERTS)).reshape(1, ROUTER_PAD)
    tabs = _na_bias_tables(rpb[0], s // GRID_W)
    slopes = _alibi_slopes(DIL_HEADS)

    h, hb = _ln0(x.reshape(n, d), ln0_g, ln0_b)
    zq = _inproj(hb, w_qkv, b_qkv)
    na = _neighborhood_attention(zq.reshape(QKV_HEADS, bsz, s, HEAD_DIM), tabs)
    na = na.reshape(NA_HEADS, n, HEAD_DIM)
    dil = [_dilated_group(zq, bsz, s, g, slopes) for g in range(len(DIL_GROUPS))]
    m = _mix(na, [o for o, _ in dil], [l for _, l in dil], hb,
             w_proj_a[0].astype(BF16), w_proj_b[0].astype(BF16), w_ga, w_gb, b_ga, b_gb)
    h1, logits = _outproj(m, h, w_o[0].astype(BF16), b_o[0], ln1_g[0], ln1_b[0], w_r_hi, w_r_lo, b_r)
    dest, slot_tok, slot_w, block_e, n_active = _route(logits, n)
    ys = _experts(h1, slot_tok, slot_w, block_e, n_active, w_gate[0], w_up[0], w_down[0])
    out = _final(dest, ys, h1, ln2_g[0], ln2_b[0])
    return out.reshape(bsz, s, d)
```

```python
import functools

import numpy as np
import jax
import jax.numpy as jnp
from jax import lax
from jax.experimental import pallas as pl
from jax.experimental.pallas import tpu as pltpu

F32 = jnp.float32
BF16 = jnp.bfloat16

D_MODEL = 2048
HEAD_DIM = 128
GRID_W = 64
NA_HEADS = 8
NA_KH = 8
NA_KW = 16
DIL_GROUPS = ((128, 1), (512, 4), (2048, 16))
DIL_HEADS_PER_GROUP = 4
DIL_HEADS = DIL_HEADS_PER_GROUP * len(DIL_GROUPS)
N_GROUPS = 8
EXPERTS_PER_GROUP = 8
N_EXPERTS = N_GROUPS * EXPERTS_PER_GROUP
TOP_K = 2
D_EXPERT = D_MODEL // 4
LN_EPS = 1e-5
DEPTH = 1
DN_ALPHA = (2 * DEPTH) ** 0.25
NA_WIDTH = NA_HEADS * HEAD_DIM
DIL_WIDTH = DIL_HEADS * HEAD_DIM
DIL_OUT_WIDTH = DIL_HEADS_PER_GROUP * HEAD_DIM
QKV_WIDTH = 3 * NA_WIDTH + 3 * DIL_WIDTH
QKV_HEADS = QKV_WIDTH // HEAD_DIM
LANES = 128
CHUNKS = D_MODEL // LANES
MASK_VALUE = -1e30

QA_H, KA_H, VA_H = 0, NA_HEADS, 2 * NA_HEADS
DIL0_H = 3 * NA_HEADS

LN_TM = 512
INPROJ_TM, INPROJ_TN = 1024, 512
NA_QROWS = 4
NA_KROWS = 12
DIL_TQ = 256
DIL_HALF = 64
DIL_TK = DIL_TQ + 2 * DIL_HALF
MIX_TM, MIX_TN = 512, 512
OUT_TM = 256
ROUTER_PAD = 128
MOE_TB = 128
FIN_TM = 128
VMEM_LIMIT = 56 * 1024 * 1024


def _params(sem, limit=VMEM_LIMIT):
    return pltpu.CompilerParams(dimension_semantics=sem, vmem_limit_bytes=limit)


def _layer_norm(x, g, b):
    mu = jnp.mean(x, axis=-1, keepdims=True)
    xc = x - mu
    var = jnp.mean(xc * xc, axis=-1, keepdims=True)
    return xc * lax.rsqrt(var + LN_EPS) * g + b


def _store_row_chunks(ref, val):
    for c in range(CHUNKS):
        ref[:, c, :] = val[:, c * LANES:(c + 1) * LANES]


def _load_row_chunks(ref):
    return jnp.concatenate([ref[:, c, :] for c in range(CHUNKS)], axis=-1)


def _ln0_kernel(x_ref, g_ref, b_ref, h_ref, hb_ref):
    y = _layer_norm(x_ref[...], g_ref[...], b_ref[...])
    h_ref[...] = y
    hb_ref[...] = y.astype(BF16)


def _ln0(x, g, b):
    n, d = x.shape
    row = pl.BlockSpec((LN_TM, d), lambda i: (i, 0))
    vec = pl.BlockSpec((1, d), lambda i: (0, 0))
    return pl.pallas_call(
        _ln0_kernel,
        grid=(n // LN_TM,),
        in_specs=[row, vec, vec],
        out_specs=[row, row],
        out_shape=[jax.ShapeDtypeStruct((n, d), F32), jax.ShapeDtypeStruct((n, d), BF16)],
        compiler_params=_params(("parallel",)),
        name="ln0",
    )(x, g.reshape(1, d), b.reshape(1, d))


def _inproj_kernel(hb_ref, w_ref, b_ref, o_ref, acc_s, *, dil):
    acc = jnp.dot(hb_ref[...], w_ref[...], preferred_element_type=F32) + b_ref[...]
    if dil == 1:
        for c in range(INPROJ_TN // LANES):
            o_ref[c, 0, 0] = acc[:, c * LANES:(c + 1) * LANES].astype(BF16)
    else:
        for c in range(INPROJ_TN // LANES):
            acc_s[c] = acc[:, c * LANES:(c + 1) * LANES]
        for c in range(INPROJ_TN // LANES):
            for r in range(dil):
                o_ref[c, 0, r] = acc_s[c, pl.ds(r, INPROJ_TM // dil, stride=dil), :].astype(BF16)


def _inproj(hb, w, b, bsz, s, dil):
    n, d = hb.shape
    width = w.shape[1]
    tiles = s // INPROJ_TM
    assert INPROJ_TM % (16 * dil) == 0 and s % INPROJ_TM == 0 and width % INPROJ_TN == 0
    return pl.pallas_call(
        functools.partial(_inproj_kernel, dil=dil),
        grid=(n // INPROJ_TM, width // INPROJ_TN),
        in_specs=[pl.BlockSpec((INPROJ_TM, d), lambda i, j: (i, 0)),
                  pl.BlockSpec((d, INPROJ_TN), lambda i, j: (0, j)),
                  pl.BlockSpec((1, INPROJ_TN), lambda i, j: (0, j))],
        out_specs=pl.BlockSpec((INPROJ_TN // LANES, 1, dil, INPROJ_TM // dil, LANES),
                               lambda i, j: (j, i // tiles, 0, i % tiles, 0)),
        out_shape=jax.ShapeDtypeStruct((width // LANES, bsz, dil, s // dil, LANES), BF16),
        scratch_shapes=[pltpu.VMEM((INPROJ_TN // LANES, INPROJ_TM, LANES), F32)],
        compiler_params=_params(("parallel", "arbitrary")),
        name=f"inproj_{dil}",
    )(hb, w, b.reshape(1, width))


def _na_bias_tables(rpb, rows):
    heads, n_dr, n_dc = rpb.shape
    qc = np.arange(GRID_W)[:, None]
    kc = np.arange(GRID_W)[None, :]
    qcs = np.clip(qc - NA_KW // 2, 0, GRID_W - NA_KW)
    v_col = (kc >= qcs) & (kc < qcs + NA_KW)
    dc = np.clip(kc - qc + NA_KW - 1, 0, n_dc - 1)
    onehot = (dc[None] == np.arange(n_dc)[:, None, None]) & v_col[None]
    toep = jnp.einsum('hrd,dqk->hrqk', rpb.astype(F32), jnp.asarray(onehot, F32),
                      precision=lax.Precision.HIGHEST)
    toep = jnp.where(jnp.asarray(v_col), toep, MASK_VALUE)
    masked = jnp.full((heads, 1, GRID_W, GRID_W), MASK_VALUE, F32)
    blocks = jnp.concatenate([toep, masked], axis=1)
    i = np.arange(NA_QROWS)[:, None]
    j = np.arange(NA_KROWS)[None, :]
    sel = []
    for r0, ks in ((0, 0), (2 * NA_QROWS, 2 * NA_QROWS - NA_KH // 2), (rows - NA_QROWS, rows - NA_KROWS)):
        r, krow = r0 + i, ks + j
        start = np.clip(r - NA_KH // 2, 0, rows - NA_KH)
        v_row = (krow >= start) & (krow < start + NA_KH)
        sel.append(np.where(v_row, krow - r + NA_KH - 1, n_dr))
    sel = np.stack(sel).reshape(-1)
    tab = jnp.take(blocks, jnp.asarray(sel, jnp.int32), axis=1)
    tab = tab.reshape(heads, 3, NA_QROWS, NA_KROWS, GRID_W, GRID_W).transpose(1, 0, 2, 4, 3, 5)
    return tab.reshape(3, heads, NA_QROWS * GRID_W, NA_KROWS * GRID_W)


def _na_kernel(q_ref, k_ref, v_ref, tab_ref, o_ref, *, rows):
    i = pl.program_id(2)
    last = pl.num_programs(2) - 1
    ks = jnp.clip(i * NA_QROWS - NA_KH // 2, 0, rows - NA_KROWS)
    off = pl.multiple_of(ks * GRID_W, GRID_W)
    kw = k_ref[0, 0, pl.ds(off, NA_KROWS * GRID_W), :]
    vw = v_ref[0, 0, pl.ds(off, NA_KROWS * GRID_W), :]
    variant = jnp.where(i == 0, 0, jnp.where(i == last, 2, 1))
    s = lax.dot_general(q_ref[0, 0], kw, (((1,), (1,)), ((), ())), preferred_element_type=F32)
    s = s + tab_ref[variant, 0]
    m = jnp.max(s, axis=-1, keepdims=True)
    p = jnp.exp(s - m)
    l = jnp.sum(p, axis=-1, keepdims=True)
    o = jnp.dot(p.astype(BF16), vw, preferred_element_type=F32) / l
    o_ref[0, 0] = o.astype(BF16)


def _neighborhood_attention(zq4, tabs):
    _, bsz, s, _ = zq4.shape
    rows = s // GRID_W
    assert rows % NA_QROWS == 0 and rows >= NA_KROWS + NA_QROWS
    tq = NA_QROWS * GRID_W
    tk = NA_KROWS * GRID_W
    return pl.pallas_call(
        functools.partial(_na_kernel, rows=rows),
        grid=(bsz, NA_HEADS, rows // NA_QROWS),
        in_specs=[pl.BlockSpec((1, 1, tq, HEAD_DIM), lambda b, h, i: (QA_H + h, b, i, 0)),
                  pl.BlockSpec((1, 1, s, HEAD_DIM), lambda b, h, i: (KA_H + h, b, 0, 0)),
                  pl.BlockSpec((1, 1, s, HEAD_DIM), lambda b, h, i: (VA_H + h, b, 0, 0)),
                  pl.BlockSpec((3, 1, tq, tk), lambda b, h, i: (0, h, 0, 0))],
        out_specs=pl.BlockSpec((1, 1, tq, HEAD_DIM), lambda b, h, i: (h, b, i, 0)),
        out_shape=jax.ShapeDtypeStruct((NA_HEADS, bsz, s, HEAD_DIM), BF16),
        compiler_params=_params(("parallel", "parallel", "arbitrary")),
        name="na_attn",
    )(zq4, zq4, zq4, tabs)


def _alibi_slopes(n):
    return np.array([2.0 ** (-8.0 * (i + 1) / n) for i in range(n)], dtype=np.float32)


def _dil_kernel(slope_ref, q_ref, k_ref, v_ref, o_ref, lse_ref, *, dil, length):
    n = pl.program_id(3)
    ks = jnp.clip(n * DIL_TQ - DIL_HALF, 0, length - DIL_TK)
    off = pl.multiple_of(ks, DIL_HALF)
    kw = k_ref[0, 0, 0, pl.ds(off, DIL_TK), :]
    vw = v_ref[0, 0, 0, pl.ds(off, DIL_TK), :]
    s = lax.dot_general(q_ref[0, 0, 0], kw, (((1,), (1,)), ((), ())), preferred_element_type=F32)
    qi = lax.broadcasted_iota(jnp.int32, (DIL_TQ, DIL_TK), 0)
    kj = lax.broadcasted_iota(jnp.int32, (DIL_TQ, DIL_TK), 1)
    dist = jnp.abs(kj - qi + (ks - n * DIL_TQ))
    penalty = slope_ref[pl.program_id(1)] * float(dil)
    s = jnp.where(dist <= DIL_HALF, s - penalty * dist.astype(F32), MASK_VALUE)
    m = jnp.max(s, axis=-1, keepdims=True)
    p = jnp.exp(s - m)
    l = jnp.sum(p, axis=-1, keepdims=True)
    o = jnp.dot(p.astype(BF16), vw, preferred_element_type=F32) / l
    o_ref[0, 0, 0] = o.astype(BF16)
    lse_ref[0, 0, 0] = jnp.broadcast_to(m + jnp.log(l), (DIL_TQ, LANES))


def _dilated_group(zq, base, g, slopes):
    window, dil = DIL_GROUPS[g]
    assert (window // 2) // dil == DIL_HALF
    _, bsz, _, length, _ = zq.shape
    assert zq.shape[2] == dil and length % DIL_TQ == 0 and length >= DIL_TK
    gh = DIL_HEADS_PER_GROUP
    qspec = pl.BlockSpec((1, 1, 1, DIL_TQ, HEAD_DIM), lambda b, h, r, n: (base + h, b, r, n, 0))
    kvspec = lambda off: pl.BlockSpec((1, 1, 1, length, HEAD_DIM),
                                      lambda b, h, r, n: (base + off + h, b, r, 0, 0))
    ospec = pl.BlockSpec((1, 1, 1, DIL_TQ, HEAD_DIM), lambda b, h, r, n: (h, b, r, n, 0))
    return pl.pallas_call(
        functools.partial(_dil_kernel, dil=dil, length=length),
        grid=(bsz, gh, dil, length // DIL_TQ),
        in_specs=[pl.BlockSpec(memory_space=pltpu.SMEM), qspec, kvspec(gh), kvspec(2 * gh)],
        out_specs=[ospec, ospec],
        out_shape=[jax.ShapeDtypeStruct((gh, bsz, dil, length, HEAD_DIM), BF16),
                   jax.ShapeDtypeStruct((gh, bsz, dil, length, HEAD_DIM), F32)],
        compiler_params=_params(("parallel", "parallel", "parallel", "arbitrary")),
        name=f"dil_attn_{dil}",
    )(jnp.asarray(slopes[g * gh:(g + 1) * gh]), zq, zq, zq)


def _mix_kernel(na_ref, o1_ref, o2_ref, o3_ref, l1_ref, l2_ref, l3_ref, hb_ref,
                wpa_ref, wpb_ref, wga_ref, wgb_ref, bga_ref, bgb_ref, m_ref, na_s, dil_s, o_s, l_s):
    @pl.when(pl.program_id(1) == 0)
    def _():
        for h in range(NA_HEADS):
            na_s[:, h * HEAD_DIM:(h + 1) * HEAD_DIM] = na_ref[h]
        for g, (o_ref, l_ref) in enumerate(((o1_ref, l1_ref), (o2_ref, l2_ref), (o3_ref, l3_ref))):
            dil = DIL_GROUPS[g][1]
            for h in range(DIL_HEADS_PER_GROUP):
                for r in range(dil):
                    rows = slice(None) if dil == 1 else pl.ds(r, MIX_TM // dil, stride=dil)
                    o_s[g, h, rows, :] = o_ref[h, 0, r].astype(F32)
                    l_s[g, h, rows, :] = l_ref[h, 0, r]
        for h in range(DIL_HEADS_PER_GROUP):
            ls = [l_s[g, h] for g in range(len(DIL_GROUPS))]
            mx = jnp.maximum(jnp.maximum(ls[0], ls[1]), ls[2])
            es = [jnp.exp(l - mx) for l in ls]
            den = es[0] + es[1] + es[2]
            acc = (es[0] * o_s[0, h] + es[1] * o_s[1, h] + es[2] * o_s[2, h]) / den
            dil_s[:, h * HEAD_DIM:(h + 1) * HEAD_DIM] = acc.astype(BF16)

    hb = hb_ref[...]
    ya = jnp.dot(na_s[...], wpa_ref[...], preferred_element_type=F32)
    yb = jnp.dot(dil_s[...], wpb_ref[...], preferred_element_type=F32)
    ga = jnp.dot(hb, wga_ref[...], preferred_element_type=F32) + bga_ref[...]
    gb = jnp.dot(hb, wgb_ref[...], preferred_element_type=F32) + bgb_ref[...]
    m_ref[...] = (jax.nn.sigmoid(ga) * ya + jax.nn.sigmoid(gb) * yb).astype(BF16)


def _mix(na, dil_o, dil_lse, hb, wpa, wpb, wga, wgb, bga, bgb, s):
    n, d = hb.shape
    tm, tn = MIX_TM, MIX_TN
    tiles = s // tm
    gh = DIL_HEADS_PER_GROUP
    ngroups = len(DIL_GROUPS)
    assert all(tm % (16 * dil) == 0 for _, dil in DIL_GROUPS) and s % tm == 0
    col = lambda k: pl.BlockSpec((k, tn), lambda i, j: (0, j))
    grp = lambda dil: pl.BlockSpec((gh, 1, dil, tm // dil, HEAD_DIM),
                                   lambda i, j: (0, i // tiles, 0, i % tiles, 0))
    groups = [grp(dil) for _, dil in DIL_GROUPS]
    return pl.pallas_call(
        _mix_kernel,
        grid=(n // tm, d // tn),
        in_specs=[pl.BlockSpec((NA_HEADS, tm, HEAD_DIM), lambda i, j: (0, i, 0)), *groups, *groups,
                  pl.BlockSpec((tm, d), lambda i, j: (i, 0)),
                  col(NA_WIDTH), col(DIL_OUT_WIDTH), col(d), col(d), col(1), col(1)],
        out_specs=pl.BlockSpec((tm, tn), lambda i, j: (i, j)),
        out_shape=jax.ShapeDtypeStruct((n, d), BF16),
        scratch_shapes=[pltpu.VMEM((tm, NA_WIDTH), BF16), pltpu.VMEM((tm, DIL_OUT_WIDTH), BF16),
                        pltpu.VMEM((ngroups, gh, tm, HEAD_DIM), F32),
                        pltpu.VMEM((ngroups, gh, tm, HEAD_DIM), F32)],
        compiler_params=_params(("parallel", "arbitrary")),
        name="mix",
    )(na, *dil_o, *dil_lse, hb, wpa, wpb, wga, wgb, bga.reshape(1, d), bgb.reshape(1, d))


def _outproj_kernel(m_ref, h_ref, wo_ref, bo_ref, g_ref, b_ref, wrh_ref, wrl_ref, br_ref,
                    h1_ref, lg_ref):
    mix = jnp.dot(m_ref[...], wo_ref[...], preferred_element_type=F32) + bo_ref[...]
    h1 = _layer_norm(DN_ALPHA * h_ref[...] + mix, g_ref[...], b_ref[...])
    _store_row_chunks(h1_ref, h1)
    hi = h1.astype(BF16)
    lo = (h1 - hi.astype(F32)).astype(BF16)
    lg = (jnp.dot(hi, wrh_ref[...], preferred_element_type=F32)
          + jnp.dot(lo, wrh_ref[...], preferred_element_type=F32)
          + jnp.dot(hi, wrl_ref[...], preferred_element_type=F32))
    lg_ref[...] = lg + br_ref[...]


def _outproj(m, h, wo, bo, g, b, wr_hi, wr_lo, br):
    n, d = h.shape
    tm = OUT_TM
    row = lambda w: pl.BlockSpec((tm, w), lambda i: (i, 0))
    full = lambda r, c: pl.BlockSpec((r, c), lambda i: (0, 0))
    return pl.pallas_call(
        _outproj_kernel,
        grid=(n // tm,),
        in_specs=[row(d), row(d), full(d, d), full(1, d), full(1, d), full(1, d),
                  full(d, ROUTER_PAD), full(d, ROUTER_PAD), full(1, ROUTER_PAD)],
        out_specs=[pl.BlockSpec((tm, CHUNKS, LANES), lambda i: (i, 0, 0)), row(ROUTER_PAD)],
        out_shape=[jax.ShapeDtypeStruct((n, CHUNKS, LANES), F32),
                   jax.ShapeDtypeStruct((n, ROUTER_PAD), F32)],
        compiler_params=_params(("parallel",)),
        name="outproj",
    )(m, h, wo, bo.reshape(1, d), g.reshape(1, d), b.reshape(1, d), wr_hi, wr_lo, br)


def _route(logits, n):
    g_logits = logits[:, :N_GROUPS]
    g_sel = jnp.argmax(g_logits, axis=-1).astype(jnp.int32)
    g_prob = jnp.take_along_axis(jax.nn.softmax(g_logits, axis=-1), g_sel[:, None], axis=-1)
    e_logits = logits[:, N_GROUPS:N_GROUPS + N_EXPERTS].reshape(n, N_GROUPS, EXPERTS_PER_GROUP)
    e_logits = jnp.take_along_axis(e_logits, g_sel[:, None, None], axis=1)[:, 0]
    top_val, top_idx = lax.top_k(e_logits, TOP_K)
    gate = g_prob * jax.nn.softmax(top_val, axis=-1)
    expert_id = (g_sel[:, None] * EXPERTS_PER_GROUP + top_idx).astype(jnp.int32)

    a = n * TOP_K
    e_flat = expert_id.reshape(a)
    onehot = (e_flat[:, None] == jnp.arange(N_EXPERTS, dtype=jnp.int32)[None, :]).astype(jnp.int32)
    csum = jnp.cumsum(onehot, axis=0)
    rank = jnp.take_along_axis(csum, e_flat[:, None], axis=1)[:, 0] - 1
    counts = csum[-1]
    pcounts = (counts + MOE_TB - 1) // MOE_TB * MOE_TB
    pends = jnp.cumsum(pcounts)
    pstarts = pends - pcounts
    dest = (pstarts[e_flat] + rank).astype(jnp.int32)
    nb = a // MOE_TB + N_EXPERTS
    slots = nb * MOE_TB
    tok = jnp.arange(a, dtype=jnp.int32) // TOP_K
    slot_tok = jnp.zeros((slots,), jnp.int32).at[dest].set(tok)
    block_e = jnp.minimum(jnp.searchsorted(pends, jnp.arange(nb, dtype=jnp.int32) * MOE_TB,
                                           side='right'), N_EXPERTS - 1).astype(jnp.int32)
    n_active = (pends[-1] // MOE_TB).astype(jnp.int32).reshape(1)
    return dest, gate, slot_tok, block_e, n_active


def _gather_rows(idx_ref, base, count, src_hbm, buf, sem):
    def copy(t):
        return pltpu.make_async_copy(src_hbm.at[idx_ref[base + t]],
                                     buf.at[pl.ds(t * CHUNKS, CHUNKS), :], sem)

    def start(t, c):
        copy(t).start()
        return c

    def wait(t, c):
        copy(t).wait()
        return c

    lax.fori_loop(0, count, start, 0)
    lax.fori_loop(0, count, wait, 0)


def _rows_to_tile(buf, count):
    return jnp.concatenate([buf[pl.ds(c, count, stride=CHUNKS), :] for c in range(CHUNKS)], axis=-1)


def _expert_kernel(be_ref, nact_ref, tok_ref, h1_hbm, wg_ref, wu_ref, wd_ref, ys_ref,
                   xbuf, wg_s, wu_s, wd_s, sem):
    i = pl.program_id(0)

    @pl.when(i < nact_ref[0])
    def _():
        prev = be_ref[jnp.maximum(i - 1, 0)]

        @pl.when((i == 0) | (be_ref[i] != prev))
        def _():
            wg_s[...] = wg_ref[0].astype(BF16)
            wu_s[...] = wu_ref[0].astype(BF16)
            wd_s[...] = wd_ref[0].astype(BF16)

        _gather_rows(tok_ref, i * MOE_TB, MOE_TB, h1_hbm, xbuf, sem)
        xb = _rows_to_tile(xbuf, MOE_TB).astype(BF16)
        gate = jnp.dot(xb, wg_s[...], preferred_element_type=F32)
        up = jnp.dot(xb, wu_s[...], preferred_element_type=F32)
        hid = (jax.nn.silu(gate) * up).astype(BF16)
        y = jnp.dot(hid, wd_s[...], preferred_element_type=F32)
        _store_row_chunks(ys_ref, y)

    @pl.when(i >= nact_ref[0])
    def _():
        ys_ref[...] = jnp.zeros(ys_ref.shape, F32)


def _experts(h1, slot_tok, block_e, n_active, w_gate, w_up, w_down):
    n, d = h1.shape[0], D_MODEL
    slots = slot_tok.shape[0]
    nb = slots // MOE_TB
    grid_spec = pltpu.PrefetchScalarGridSpec(
        num_scalar_prefetch=3,
        grid=(nb,),
        in_specs=[pl.BlockSpec(memory_space=pl.ANY),
                  pl.BlockSpec((1, d, D_EXPERT), lambda i, be, na, tk: (be[i], 0, 0)),
                  pl.BlockSpec((1, d, D_EXPERT), lambda i, be, na, tk: (be[i], 0, 0)),
                  pl.BlockSpec((1, D_EXPERT, d), lambda i, be, na, tk: (be[i], 0, 0))],
        out_specs=pl.BlockSpec((MOE_TB, CHUNKS, LANES), lambda i, be, na, tk: (i, 0, 0)),
        scratch_shapes=[pltpu.VMEM((MOE_TB * CHUNKS, LANES), F32),
                        pltpu.VMEM((d, D_EXPERT), BF16), pltpu.VMEM((d, D_EXPERT), BF16),
                        pltpu.VMEM((D_EXPERT, d), BF16), pltpu.SemaphoreType.DMA(())])
    return pl.pallas_call(
        _expert_kernel,
        grid_spec=grid_spec,
        out_shape=jax.ShapeDtypeStruct((slots, CHUNKS, LANES), F32),
        compiler_params=_params(("arbitrary",)),
        name="experts",
    )(block_e, n_active, slot_tok, h1, w_gate, w_up, w_down)


def _final_kernel(dest_ref, ys_hbm, h1_ref, gate_ref, g_ref, b_ref, o_ref, buf, sem):
    i = pl.program_id(0)
    _gather_rows(dest_ref, i * FIN_TM * TOP_K, FIN_TM * TOP_K, ys_hbm, buf, sem)
    gate = gate_ref[...]
    w0 = gate[:, 0:1]
    w1 = gate[:, 1:2]
    parts = []
    for c in range(CHUNKS):
        y0 = buf[pl.ds(c, FIN_TM, stride=TOP_K * CHUNKS), :]
        y1 = buf[pl.ds(CHUNKS + c, FIN_TM, stride=TOP_K * CHUNKS), :]
        parts.append(y0 * w0 + y1 * w1)
    ffn = jnp.concatenate(parts, axis=-1)
    o_ref[...] = _layer_norm(DN_ALPHA * _load_row_chunks(h1_ref) + ffn, g_ref[...], b_ref[...])


def _final(dest, ys, h1, gate, g, b):
    n, d = h1.shape[0], D_MODEL
    grid_spec = pltpu.PrefetchScalarGridSpec(
        num_scalar_prefetch=1,
        grid=(n // FIN_TM,),
        in_specs=[pl.BlockSpec(memory_space=pl.ANY),
                  pl.BlockSpec((FIN_TM, CHUNKS, LANES), lambda i, ds: (i, 0, 0)),
                  pl.BlockSpec((FIN_TM, TOP_K), lambda i, ds: (i, 0)),
                  pl.BlockSpec((1, d), lambda i, ds: (0, 0)),
                  pl.BlockSpec((1, d), lambda i, ds: (0, 0))],
        out_specs=pl.BlockSpec((FIN_TM, d), lambda i, ds: (i, 0)),
        scratch_shapes=[pltpu.VMEM((FIN_TM * TOP_K * CHUNKS, LANES), F32),
                        pltpu.SemaphoreType.DMA(())])
    return pl.pallas_call(
        _final_kernel,
        grid_spec=grid_spec,
        out_shape=jax.ShapeDtypeStruct((n, d), F32),
        compiler_params=_params(("arbitrary",)),
        name="final",
    )(dest, ys, h1, gate, g.reshape(1, d), b.reshape(1, d))


def kernel(x, ln0_g, ln0_b, w_in, b_in, rpb, w_proj_a, w_proj_b, w_o, b_o, ln1_g, ln1_b,
           w_router_group, b_router_group, w_router_expert, b_router_expert,
           w_gate, w_up, w_down, ln2_g, ln2_b):
    bsz, s, d = x.shape
    n = bsz * s
    assert d == D_MODEL and w_in.shape[0] == DEPTH
    scale = HEAD_DIM ** -0.5

    col_scale = np.ones((QKV_WIDTH,), np.float32)
    col_scale[:NA_WIDTH] = scale
    col_scale[3 * NA_WIDTH:3 * NA_WIDTH + DIL_WIDTH] = scale
    w_qkv = w_in[0][:, :QKV_WIDTH] * col_scale
    b_qkv = b_in[0][:QKV_WIDTH] * col_scale

    def group_cols(t, g):
        gw = DIL_OUT_WIDTH
        return [t[..., 3 * NA_WIDTH + p * DIL_WIDTH + g * gw:3 * NA_WIDTH + p * DIL_WIDTH + (g + 1) * gw]
                for p in range(3)]

    w_proj = [jnp.concatenate([w_qkv[:, :3 * NA_WIDTH]] + group_cols(w_qkv, 0), axis=1).astype(BF16)]
    b_proj = [jnp.concatenate([b_qkv[:3 * NA_WIDTH]] + group_cols(b_qkv, 0))]
    for g in range(1, len(DIL_GROUPS)):
        w_proj.append(jnp.concatenate(group_cols(w_qkv, g), axis=1).astype(BF16))
        b_proj.append(jnp.concatenate(group_cols(b_qkv, g)))
    w_ga = w_in[0][:, QKV_WIDTH:QKV_WIDTH + d].astype(BF16)
    w_gb = w_in[0][:, QKV_WIDTH + d:].astype(BF16)
    b_ga = b_in[0][QKV_WIDTH:QKV_WIDTH + d]
    b_gb = b_in[0][QKV_WIDTH + d:]
    w_r = jnp.concatenate([w_router_group[0], w_router_expert[0]], axis=1)
    w_r = jnp.pad(w_r, ((0, 0), (0, ROUTER_PAD - w_r.shape[1])))
    w_r_hi = w_r.astype(BF16)
    w_r_lo = (w_r - w_r_hi.astype(F32)).astype(BF16)
    b_r = jnp.pad(jnp.concatenate([b_router_group[0], b_router_expert[0]]),
                  (0, ROUTER_PAD - N_GROUPS - N_EXPERTS)).reshape(1, ROUTER_PAD)
    tabs = _na_bias_tables(rpb[0], s // GRID_W)
    slopes = _alibi_slopes(DIL_HEADS)

    h, hb = _ln0(x.reshape(n, d), ln0_g, ln0_b)
    zq = [_inproj(hb, w_proj[g], b_proj[g], bsz, s, DIL_GROUPS[g][1]) for g in range(len(DIL_GROUPS))]
    na = _neighborhood_attention(zq[0].reshape(-1, bsz, s, HEAD_DIM), tabs)
    na = na.reshape(NA_HEADS, n, HEAD_DIM)
    dil = [_dilated_group(zq[g], DIL0_H if g == 0 else 0, g, slopes) for g in range(len(DIL_GROUPS))]
    m = _mix(na, [o for o, _ in dil], [l for _, l in dil], hb,
             w_proj_a[0].astype(BF16), w_proj_b[0].astype(BF16), w_ga, w_gb, b_ga, b_gb, s)
    h1, logits = _outproj(m, h, w_o[0].astype(BF16), b_o[0], ln1_g[0], ln1_b[0], w_r_hi, w_r_lo, b_r)
    dest, gate, slot_tok, block_e, n_active = _route(logits, n)
    ys = _experts(h1, slot_tok, block_e, n_active, w_gate[0], w_up[0], w_down[0])
    out = _final(dest, ys, h1, gate, ln2_g[0], ln2_b[0])
    return out.reshape(bsz, s, d)
```

```python
import functools

import numpy as np
import jax
import jax.numpy as jnp
from jax import lax
from jax.experimental import pallas as pl
from jax.experimental.pallas import tpu as pltpu

F32 = jnp.float32
BF16 = jnp.bfloat16

D_MODEL = 2048
HEAD_DIM = 128
GRID_W = 64
NA_HEADS = 8
NA_KH = 8
NA_KW = 16
DIL_GROUPS = ((128, 1), (512, 4), (2048, 16))
DIL_HEADS_PER_GROUP = 4
DIL_HEADS = DIL_HEADS_PER_GROUP * len(DIL_GROUPS)
N_GROUPS = 8
EXPERTS_PER_GROUP = 8
N_EXPERTS = N_GROUPS * EXPERTS_PER_GROUP
TOP_K = 2
D_EXPERT = D_MODEL // 4
LN_EPS = 1e-5
DEPTH = 1
DN_ALPHA = (2 * DEPTH) ** 0.25
NA_WIDTH = NA_HEADS * HEAD_DIM
DIL_WIDTH = DIL_HEADS * HEAD_DIM
DIL_OUT_WIDTH = DIL_HEADS_PER_GROUP * HEAD_DIM
QKV_WIDTH = 3 * NA_WIDTH + 3 * DIL_WIDTH
QKV_HEADS = QKV_WIDTH // HEAD_DIM
LANES = 128
CHUNKS = D_MODEL // LANES
MASK_VALUE = -1e30

QA_H, KA_H, VA_H = 0, NA_HEADS, 2 * NA_HEADS
DIL0_H = 3 * NA_HEADS

LN_TM = 512
INPROJ_TM, INPROJ_TN = 1024, 512
NA_QROWS = 4
NA_KROWS = 12
DIL_TQ = 256
DIL_HALF = 64
DIL_TK = DIL_TQ + 2 * DIL_HALF
MIX_TM, MIX_TN = 512, 512
OUT_TM = 256
ROUTER_PAD = 128
ROUTE_TM = 512
MOE_TB = 128
FIN_TM = 128
VMEM_LIMIT = 56 * 1024 * 1024


def _params(sem, limit=VMEM_LIMIT):
    return pltpu.CompilerParams(dimension_semantics=sem, vmem_limit_bytes=limit)


def _layer_norm(x, g, b):
    mu = jnp.mean(x, axis=-1, keepdims=True)
    xc = x - mu
    var = jnp.mean(xc * xc, axis=-1, keepdims=True)
    return xc * lax.rsqrt(var + LN_EPS) * g + b


def _store_row_chunks(ref, val):
    for c in range(CHUNKS):
        ref[:, c, :] = val[:, c * LANES:(c + 1) * LANES]


def _load_row_chunks(ref):
    return jnp.concatenate([ref[:, c, :] for c in range(CHUNKS)], axis=-1)


def _ln0_kernel(x_ref, g_ref, b_ref, h_ref, hb_ref):
    y = _layer_norm(x_ref[...], g_ref[...], b_ref[...])
    h_ref[...] = y
    hb_ref[...] = y.astype(BF16)


def _ln0(x, g, b):
    n, d = x.shape
    row = pl.BlockSpec((LN_TM, d), lambda i: (i, 0))
    vec = pl.BlockSpec((1, d), lambda i: (0, 0))
    return pl.pallas_call(
        _ln0_kernel,
        grid=(n // LN_TM,),
        in_specs=[row, vec, vec],
        out_specs=[row, row],
        out_shape=[jax.ShapeDtypeStruct((n, d), F32), jax.ShapeDtypeStruct((n, d), BF16)],
        compiler_params=_params(("parallel",)),
        name="ln0",
    )(x, g.reshape(1, d), b.reshape(1, d))


def _inproj_kernel(hb_ref, w_ref, b_ref, o_ref, acc_s, *, dil):
    acc = jnp.dot(hb_ref[...], w_ref[...], preferred_element_type=F32) + b_ref[...]
    if dil == 1:
        for c in range(INPROJ_TN // LANES):
            o_ref[c, 0, 0] = acc[:, c * LANES:(c + 1) * LANES].astype(BF16)
    else:
        for c in range(INPROJ_TN // LANES):
            acc_s[c] = acc[:, c * LANES:(c + 1) * LANES]
        for c in range(INPROJ_TN // LANES):
            for r in range(dil):
                o_ref[c, 0, r] = acc_s[c, pl.ds(r, INPROJ_TM // dil, stride=dil), :].astype(BF16)


def _inproj(hb, w, b, bsz, s, dil):
    n, d = hb.shape
    width = w.shape[1]
    tiles = s // INPROJ_TM
    assert INPROJ_TM % (16 * dil) == 0 and s % INPROJ_TM == 0 and width % INPROJ_TN == 0
    return pl.pallas_call(
        functools.partial(_inproj_kernel, dil=dil),
        grid=(n // INPROJ_TM, width // INPROJ_TN),
        in_specs=[pl.BlockSpec((INPROJ_TM, d), lambda i, j: (i, 0)),
                  pl.BlockSpec((d, INPROJ_TN), lambda i, j: (0, j)),
                  pl.BlockSpec((1, INPROJ_TN), lambda i, j: (0, j))],
        out_specs=pl.BlockSpec((INPROJ_TN // LANES, 1, dil, INPROJ_TM // dil, LANES),
                               lambda i, j: (j, i // tiles, 0, i % tiles, 0)),
        out_shape=jax.ShapeDtypeStruct((width // LANES, bsz, dil, s // dil, LANES), BF16),
        scratch_shapes=[pltpu.VMEM((INPROJ_TN // LANES, INPROJ_TM, LANES), F32)],
        compiler_params=_params(("parallel", "arbitrary")),
        name=f"inproj_{dil}",
    )(hb, w, b.reshape(1, width))


def _na_bias_tables(rpb, rows):
    heads, n_dr, n_dc = rpb.shape
    qc = np.arange(GRID_W)[:, None]
    kc = np.arange(GRID_W)[None, :]
    qcs = np.clip(qc - NA_KW // 2, 0, GRID_W - NA_KW)
    v_col = (kc >= qcs) & (kc < qcs + NA_KW)
    dc = np.clip(kc - qc + NA_KW - 1, 0, n_dc - 1)
    onehot = (dc[None] == np.arange(n_dc)[:, None, None]) & v_col[None]
    toep = jnp.einsum('hrd,dqk->hrqk', rpb.astype(F32), jnp.asarray(onehot, F32),
                      precision=lax.Precision.HIGHEST)
    toep = jnp.where(jnp.asarray(v_col), toep, MASK_VALUE)
    masked = jnp.full((heads, 1, GRID_W, GRID_W), MASK_VALUE, F32)
    blocks = jnp.concatenate([toep, masked], axis=1)
    i = np.arange(NA_QROWS)[:, None]
    j = np.arange(NA_KROWS)[None, :]
    sel = []
    for r0, ks in ((0, 0), (2 * NA_QROWS, 2 * NA_QROWS - NA_KH // 2), (rows - NA_QROWS, rows - NA_KROWS)):
        r, krow = r0 + i, ks + j
        start = np.clip(r - NA_KH // 2, 0, rows - NA_KH)
        v_row = (krow >= start) & (krow < start + NA_KH)
        sel.append(np.where(v_row, krow - r + NA_KH - 1, n_dr))
    sel = np.stack(sel).reshape(-1)
    tab = jnp.take(blocks, jnp.asarray(sel, jnp.int32), axis=1)
    tab = tab.reshape(heads, 3, NA_QROWS, NA_KROWS, GRID_W, GRID_W).transpose(1, 0, 2, 4, 3, 5)
    return tab.reshape(3, heads, NA_QROWS * GRID_W, NA_KROWS * GRID_W)


def _na_kernel(q_ref, k_ref, v_ref, tab_ref, o_ref, *, rows):
    i = pl.program_id(2)
    last = pl.num_programs(2) - 1
    ks = jnp.clip(i * NA_QROWS - NA_KH // 2, 0, rows - NA_KROWS)
    off = pl.multiple_of(ks * GRID_W, GRID_W)
    kw = k_ref[0, 0, pl.ds(off, NA_KROWS * GRID_W), :]
    vw = v_ref[0, 0, pl.ds(off, NA_KROWS * GRID_W), :]
    variant = jnp.where(i == 0, 0, jnp.where(i == last, 2, 1))
    s = lax.dot_general(q_ref[0, 0], kw, (((1,), (1,)), ((), ())), preferred_element_type=F32)
    s = s + tab_ref[variant, 0]
    m = jnp.max(s, axis=-1, keepdims=True)
    p = jnp.exp(s - m)
    l = jnp.sum(p, axis=-1, keepdims=True)
    o = jnp.dot(p.astype(BF16), vw, preferred_element_type=F32) / l
    o_ref[0, 0] = o.astype(BF16)


def _neighborhood_attention(zq4, tabs):
    _, bsz, s, _ = zq4.shape
    rows = s // GRID_W
    assert rows % NA_QROWS == 0 and rows >= NA_KROWS + NA_QROWS
    tq = NA_QROWS * GRID_W
    tk = NA_KROWS * GRID_W
    return pl.pallas_call(
        functools.partial(_na_kernel, rows=rows),
        grid=(bsz, NA_HEADS, rows // NA_QROWS),
        in_specs=[pl.BlockSpec((1, 1, tq, HEAD_DIM), lambda b, h, i: (QA_H + h, b, i, 0)),
                  pl.BlockSpec((1, 1, s, HEAD_DIM), lambda b, h, i: (KA_H + h, b, 0, 0)),
                  pl.BlockSpec((1, 1, s, HEAD_DIM), lambda b, h, i: (VA_H + h, b, 0, 0)),
                  pl.BlockSpec((3, 1, tq, tk), lambda b, h, i: (0, h, 0, 0))],
        out_specs=pl.BlockSpec((1, 1, tq, HEAD_DIM), lambda b, h, i: (h, b, i, 0)),
        out_shape=jax.ShapeDtypeStruct((NA_HEADS, bsz, s, HEAD_DIM), BF16),
        compiler_params=_params(("parallel", "parallel", "arbitrary")),
        name="na_attn",
    )(zq4, zq4, zq4, tabs)


def _alibi_slopes(n):
    return np.array([2.0 ** (-8.0 * (i + 1) / n) for i in range(n)], dtype=np.float32)


def _dil_kernel(slope_ref, q_ref, k_ref, v_ref, o_ref, lse_ref, *, dil, length):
    n = pl.program_id(3)
    ks = jnp.clip(n * DIL_TQ - DIL_HALF, 0, length - DIL_TK)
    off = pl.multiple_of(ks, DIL_HALF)
    kw = k_ref[0, 0, 0, pl.ds(off, DIL_TK), :]
    vw = v_ref[0, 0, 0, pl.ds(off, DIL_TK), :]
    s = lax.dot_general(q_ref[0, 0, 0], kw, (((1,), (1,)), ((), ())), preferred_element_type=F32)
    qi = lax.broadcasted_iota(jnp.int32, (DIL_TQ, DIL_TK), 0)
    kj = lax.broadcasted_iota(jnp.int32, (DIL_TQ, DIL_TK), 1)
    dist = jnp.abs(kj - qi + (ks - n * DIL_TQ))
    penalty = slope_ref[pl.program_id(1)] * float(dil)
    s = jnp.where(dist <= DIL_HALF, s - penalty * dist.astype(F32), MASK_VALUE)
    m = jnp.max(s, axis=-1, keepdims=True)
    p = jnp.exp(s - m)
    l = jnp.sum(p, axis=-1, keepdims=True)
    o = jnp.dot(p.astype(BF16), vw, preferred_element_type=F32) / l
    o_ref[0, 0, 0] = o.astype(BF16)
    lse_ref[0, 0, 0] = jnp.broadcast_to(m + jnp.log(l), (DIL_TQ, LANES))


def _dilated_group(zq, base, g, slopes):
    window, dil = DIL_GROUPS[g]
    assert (window // 2) // dil == DIL_HALF
    _, bsz, _, length, _ = zq.shape
    assert zq.shape[2] == dil and length % DIL_TQ == 0 and length >= DIL_TK
    gh = DIL_HEADS_PER_GROUP
    qspec = pl.BlockSpec((1, 1, 1, DIL_TQ, HEAD_DIM), lambda b, h, r, n: (base + h, b, r, n, 0))
    kvspec = lambda off: pl.BlockSpec((1, 1, 1, length, HEAD_DIM),
                                      lambda b, h, r, n: (base + off + h, b, r, 0, 0))
    ospec = pl.BlockSpec((1, 1, 1, DIL_TQ, HEAD_DIM), lambda b, h, r, n: (h, b, r, n, 0))
    return pl.pallas_call(
        functools.partial(_dil_kernel, dil=dil, length=length),
        grid=(bsz, gh, dil, length // DIL_TQ),
        in_specs=[pl.BlockSpec(memory_space=pltpu.SMEM), qspec, kvspec(gh), kvspec(2 * gh)],
        out_specs=[ospec, ospec],
        out_shape=[jax.ShapeDtypeStruct((gh, bsz, dil, length, HEAD_DIM), BF16),
                   jax.ShapeDtypeStruct((gh, bsz, dil, length, HEAD_DIM), F32)],
        compiler_params=_params(("parallel", "parallel", "parallel", "arbitrary")),
        name=f"dil_attn_{dil}",
    )(jnp.asarray(slopes[g * gh:(g + 1) * gh]), zq, zq, zq)


def _mix_kernel(na_ref, o1_ref, o2_ref, o3_ref, l1_ref, l2_ref, l3_ref, hb_ref,
                wpa_ref, wpb_ref, wga_ref, wgb_ref, bga_ref, bgb_ref, m_ref, na_s, dil_s, o_s, l_s):
    @pl.when(pl.program_id(1) == 0)
    def _():
        for h in range(NA_HEADS):
            na_s[:, h * HEAD_DIM:(h + 1) * HEAD_DIM] = na_ref[h]
        for g, (o_ref, l_ref) in enumerate(((o1_ref, l1_ref), (o2_ref, l2_ref), (o3_ref, l3_ref))):
            dil = DIL_GROUPS[g][1]
            for h in range(DIL_HEADS_PER_GROUP):
                for r in range(dil):
                    rows = slice(None) if dil == 1 else pl.ds(r, MIX_TM // dil, stride=dil)
                    o_s[g, h, rows, :] = o_ref[h, 0, r].astype(F32)
                    l_s[g, h, rows, :] = l_ref[h, 0, r]
        for h in range(DIL_HEADS_PER_GROUP):
            ls = [l_s[g, h] for g in range(len(DIL_GROUPS))]
            mx = jnp.maximum(jnp.maximum(ls[0], ls[1]), ls[2])
            es = [jnp.exp(l - mx) for l in ls]
            den = es[0] + es[1] + es[2]
            acc = (es[0] * o_s[0, h] + es[1] * o_s[1, h] + es[2] * o_s[2, h]) / den
            dil_s[:, h * HEAD_DIM:(h + 1) * HEAD_DIM] = acc.astype(BF16)

    hb = hb_ref[...]
    ya = jnp.dot(na_s[...], wpa_ref[...], preferred_element_type=F32)
    yb = jnp.dot(dil_s[...], wpb_ref[...], preferred_element_type=F32)
    ga = jnp.dot(hb, wga_ref[...], preferred_element_type=F32) + bga_ref[...]
    gb = jnp.dot(hb, wgb_ref[...], preferred_element_type=F32) + bgb_ref[...]
    m_ref[...] = (jax.nn.sigmoid(ga) * ya + jax.nn.sigmoid(gb) * yb).astype(BF16)


def _mix(na, dil_o, dil_lse, hb, wpa, wpb, wga, wgb, bga, bgb, s):
    n, d = hb.shape
    tm, tn = MIX_TM, MIX_TN
    tiles = s // tm
    gh = DIL_HEADS_PER_GROUP
    ngroups = len(DIL_GROUPS)
    assert all(tm % (16 * dil) == 0 for _, dil in DIL_GROUPS) and s % tm == 0
    col = lambda k: pl.BlockSpec((k, tn), lambda i, j: (0, j))
    grp = lambda dil: pl.BlockSpec((gh, 1, dil, tm // dil, HEAD_DIM),
                                   lambda i, j: (0, i // tiles, 0, i % tiles, 0))
    groups = [grp(dil) for _, dil in DIL_GROUPS]
    return pl.pallas_call(
        _mix_kernel,
        grid=(n // tm, d // tn),
        in_specs=[pl.BlockSpec((NA_HEADS, tm, HEAD_DIM), lambda i, j: (0, i, 0)), *groups, *groups,
                  pl.BlockSpec((tm, d), lambda i, j: (i, 0)),
                  col(NA_WIDTH), col(DIL_OUT_WIDTH), col(d), col(d), col(1), col(1)],
        out_specs=pl.BlockSpec((tm, tn), lambda i, j: (i, j)),
        out_shape=jax.ShapeDtypeStruct((n, d), BF16),
        scratch_shapes=[pltpu.VMEM((tm, NA_WIDTH), BF16), pltpu.VMEM((tm, DIL_OUT_WIDTH), BF16),
                        pltpu.VMEM((ngroups, gh, tm, HEAD_DIM), F32),
                        pltpu.VMEM((ngroups, gh, tm, HEAD_DIM), F32)],
        compiler_params=_params(("parallel", "arbitrary")),
        name="mix",
    )(na, *dil_o, *dil_lse, hb, wpa, wpb, wga, wgb, bga.reshape(1, d), bgb.reshape(1, d))


def _outproj_kernel(m_ref, h_ref, wo_ref, bo_ref, g_ref, b_ref, wrh_ref, wrl_ref, br_ref,
                    h1_ref, lg_ref):
    mix = jnp.dot(m_ref[...], wo_ref[...], preferred_element_type=F32) + bo_ref[...]
    h1 = _layer_norm(DN_ALPHA * h_ref[...] + mix, g_ref[...], b_ref[...])
    _store_row_chunks(h1_ref, h1)
    hi = h1.astype(BF16)
    lo = (h1 - hi.astype(F32)).astype(BF16)
    lg = (jnp.dot(hi, wrh_ref[...], preferred_element_type=F32)
          + jnp.dot(lo, wrh_ref[...], preferred_element_type=F32)
          + jnp.dot(hi, wrl_ref[...], preferred_element_type=F32))
    lg_ref[...] = lg + br_ref[...]


def _outproj(m, h, wo, bo, g, b, wr_hi, wr_lo, br):
    n, d = h.shape
    tm = OUT_TM
    row = lambda w: pl.BlockSpec((tm, w), lambda i: (i, 0))
    full = lambda r, c: pl.BlockSpec((r, c), lambda i: (0, 0))
    return pl.pallas_call(
        _outproj_kernel,
        grid=(n // tm,),
        in_specs=[row(d), row(d), full(d, d), full(1, d), full(1, d), full(1, d),
                  full(d, ROUTER_PAD), full(d, ROUTER_PAD), full(1, ROUTER_PAD)],
        out_specs=[pl.BlockSpec((tm, CHUNKS, LANES), lambda i: (i, 0, 0)), row(ROUTER_PAD)],
        out_shape=[jax.ShapeDtypeStruct((n, CHUNKS, LANES), F32),
                   jax.ShapeDtypeStruct((n, ROUTER_PAD), F32)],
        compiler_params=_params(("parallel",)),
        name="outproj",
    )(m, h, wo, bo.reshape(1, d), g.reshape(1, d), b.reshape(1, d), wr_hi, wr_lo, br)


def _first_argmax(vals, lane_f):
    top = jnp.max(vals, axis=-1, keepdims=True)
    idx = jnp.min(jnp.where(vals == top, lane_f, float(LANES)), axis=-1, keepdims=True)
    return top, idx


def _route_kernel(lg_ref, info_ref, gate_ref, cnt_ref, base_s):
    @pl.when(pl.program_id(0) == 0)
    def _():
        base_s[...] = jnp.zeros(base_s.shape, F32)

    lg = lg_ref[...]
    lane = lax.broadcasted_iota(jnp.int32, lg.shape, 1)
    lane_f = lane.astype(F32)
    g_mask = lane < N_GROUPS
    g_top, g_sel = _first_argmax(jnp.where(g_mask, lg, MASK_VALUE), lane_f)
    g_prob = 1.0 / jnp.sum(jnp.where(g_mask, jnp.exp(lg - g_top), 0.0), axis=-1, keepdims=True)
    first = N_GROUPS + g_sel * EXPERTS_PER_GROUP
    e_mask = (lane_f >= first) & (lane_f < first + EXPERTS_PER_GROUP)
    el = jnp.where(e_mask, lg, MASK_VALUE)
    v0, i0 = _first_argmax(el, lane_f)
    v1, i1 = _first_argmax(jnp.where(lane_f == i0, MASK_VALUE, el), lane_f)
    e1 = jnp.exp(v1 - v0)
    w0 = g_prob / (1.0 + e1)
    w1 = g_prob * e1 / (1.0 + e1)

    tm = lg.shape[0]
    tri = (lax.broadcasted_iota(jnp.int32, (tm, tm), 1)
           < lax.broadcasted_iota(jnp.int32, (tm, tm), 0)).astype(BF16)
    ranks = []
    for idx in (i0, i1):
        onehot = lane_f == idx - N_GROUPS
        before = jnp.dot(tri, onehot.astype(BF16), preferred_element_type=F32) + base_s[...]
        ranks.append(jnp.sum(jnp.where(onehot, before, 0.0), axis=-1, keepdims=True))
        base_s[...] = base_s[...] + jnp.sum(onehot.astype(F32), axis=0, keepdims=True)

    info = jnp.where(lane == 0, i0 - N_GROUPS,
                     jnp.where(lane == 1, i1 - N_GROUPS,
                               jnp.where(lane == 2, ranks[0], jnp.where(lane == 3, ranks[1], 0.0))))
    info_ref[...] = info.astype(jnp.int32)
    gate_ref[...] = jnp.where(lane == 0, w0, jnp.where(lane == 1, w1, 0.0))
    cnt_ref[...] = base_s[...]


def _route(logits, n):
    tm = ROUTE_TM
    row = pl.BlockSpec((tm, ROUTER_PAD), lambda i: (i, 0))
    one = pl.BlockSpec((1, ROUTER_PAD), lambda i: (0, 0))
    info, gate, counts = pl.pallas_call(
        _route_kernel,
        grid=(n // tm,),
        in_specs=[row],
        out_specs=[row, row, one],
        out_shape=[jax.ShapeDtypeStruct((n, ROUTER_PAD), jnp.int32),
                   jax.ShapeDtypeStruct((n, ROUTER_PAD), F32),
                   jax.ShapeDtypeStruct((1, ROUTER_PAD), F32)],
        scratch_shapes=[pltpu.VMEM((1, ROUTER_PAD), F32)],
        compiler_params=_params(("arbitrary",)),
        name="route",
    )(logits)
    gate = gate[:, :TOP_K]
    a = n * TOP_K
    e_flat = info[:, :TOP_K].reshape(a)
    rank = info[:, TOP_K:2 * TOP_K].reshape(a)
    counts = counts[0, :N_EXPERTS].astype(jnp.int32)
    pcounts = (counts + MOE_TB - 1) // MOE_TB * MOE_TB
    pends = jnp.cumsum(pcounts)
    pstarts = pends - pcounts
    dest = (pstarts[e_flat] + rank).astype(jnp.int32)
    nb = a // MOE_TB + N_EXPERTS
    slots = nb * MOE_TB
    tok = jnp.arange(a, dtype=jnp.int32) // TOP_K
    slot_tok = jnp.zeros((slots,), jnp.int32).at[dest].set(tok)
    block_e = jnp.minimum(jnp.searchsorted(pends, jnp.arange(nb, dtype=jnp.int32) * MOE_TB,
                                           side='right'), N_EXPERTS - 1).astype(jnp.int32)
    n_active = (pends[-1] // MOE_TB).astype(jnp.int32).reshape(1)
    return dest, gate, slot_tok, block_e, n_active


def _gather_rows(idx_ref, base, count, src_hbm, buf, sem):
    def copy(t):
        return pltpu.make_async_copy(src_hbm.at[idx_ref[base + t]],
                                     buf.at[pl.ds(t * CHUNKS, CHUNKS), :], sem)

    def start(t, c):
        copy(t).start()
        return c

    def wait(t, c):
        copy(t).wait()
        return c

    lax.fori_loop(0, count, start, 0)
    lax.fori_loop(0, count, wait, 0)


def _rows_to_tile(buf, count):
    return jnp.concatenate([buf[pl.ds(c, count, stride=CHUNKS), :] for c in range(CHUNKS)], axis=-1)


def _expert_kernel(be_ref, nact_ref, tok_ref, h1_hbm, wg_ref, wu_ref, wd_ref, ys_ref,
                   xbuf, wg_s, wu_s, wd_s, sem):
    i = pl.program_id(0)

    @pl.when(i < nact_ref[0])
    def _():
        prev = be_ref[jnp.maximum(i - 1, 0)]

        @pl.when((i == 0) | (be_ref[i] != prev))
        def _():
            wg_s[...] = wg_ref[0].astype(BF16)
            wu_s[...] = wu_ref[0].astype(BF16)
            wd_s[...] = wd_ref[0].astype(BF16)

        _gather_rows(tok_ref, i * MOE_TB, MOE_TB, h1_hbm, xbuf, sem)
        xb = _rows_to_tile(xbuf, MOE_TB).astype(BF16)
        gate = jnp.dot(xb, wg_s[...], preferred_element_type=F32)
        up = jnp.dot(xb, wu_s[...], preferred_element_type=F32)
        hid = (jax.nn.silu(gate) * up).astype(BF16)
        y = jnp.dot(hid, wd_s[...], preferred_element_type=F32)
        _store_row_chunks(ys_ref, y)

    @pl.when(i >= nact_ref[0])
    def _():
        ys_ref[...] = jnp.zeros(ys_ref.shape, F32)


def _experts(h1, slot_tok, block_e, n_active, w_gate, w_up, w_down):
    n, d = h1.shape[0], D_MODEL
    slots = slot_tok.shape[0]
    nb = slots // MOE_TB
    grid_spec = pltpu.PrefetchScalarGridSpec(
        num_scalar_prefetch=3,
        grid=(nb,),
        in_specs=[pl.BlockSpec(memory_space=pl.ANY),
                  pl.BlockSpec((1, d, D_EXPERT), lambda i, be, na, tk: (be[i], 0, 0)),
                  pl.BlockSpec((1, d, D_EXPERT), lambda i, be, na, tk: (be[i], 0, 0)),
                  pl.BlockSpec((1, D_EXPERT, d), lambda i, be, na, tk: (be[i], 0, 0))],
        out_specs=pl.BlockSpec((MOE_TB, CHUNKS, LANES), lambda i, be, na, tk: (i, 0, 0)),
        scratch_shapes=[pltpu.VMEM((MOE_TB * CHUNKS, LANES), F32),
                        pltpu.VMEM((d, D_EXPERT), BF16), pltpu.VMEM((d, D_EXPERT), BF16),
                        pltpu.VMEM((D_EXPERT, d), BF16), pltpu.SemaphoreType.DMA(())])
    return pl.pallas_call(
        _expert_kernel,
        grid_spec=grid_spec,
        out_shape=jax.ShapeDtypeStruct((slots, CHUNKS, LANES), F32),
        compiler_params=_params(("arbitrary",)),
        name="experts",
    )(block_e, n_active, slot_tok, h1, w_gate, w_up, w_down)


def _final_kernel(dest_ref, ys_hbm, h1_ref, gate_ref, g_ref, b_ref, o_ref, buf, sem):
    i = pl.program_id(0)
    _gather_rows(dest_ref, i * FIN_TM * TOP_K, FIN_TM * TOP_K, ys_hbm, buf, sem)
    gate = gate_ref[...]
    w0 = gate[:, 0:1]
    w1 = gate[:, 1:2]
    parts = []
    for c in range(CHUNKS):
        y0 = buf[pl.ds(c, FIN_TM, stride=TOP_K * CHUNKS), :]
        y1 = buf[pl.ds(CHUNKS + c, FIN_TM, stride=TOP_K * CHUNKS), :]
        parts.append(y0 * w0 + y1 * w1)
    ffn = jnp.concatenate(parts, axis=-1)
    o_ref[...] = _layer_norm(DN_ALPHA * _load_row_chunks(h1_ref) + ffn, g_ref[...], b_ref[...])


def _final(dest, ys, h1, gate, g, b):
    n, d = h1.shape[0], D_MODEL
    grid_spec = pltpu.PrefetchScalarGridSpec(
        num_scalar_prefetch=1,
        grid=(n // FIN_TM,),
        in_specs=[pl.BlockSpec(memory_space=pl.ANY),
                  pl.BlockSpec((FIN_TM, CHUNKS, LANES), lambda i, ds: (i, 0, 0)),
                  pl.BlockSpec((FIN_TM, TOP_K), lambda i, ds: (i, 0)),
                  pl.BlockSpec((1, d), lambda i, ds: (0, 0)),
                  pl.BlockSpec((1, d), lambda i, ds: (0, 0))],
        out_specs=pl.BlockSpec((FIN_TM, d), lambda i, ds: (i, 0)),
        scratch_shapes=[pltpu.VMEM((FIN_TM * TOP_K * CHUNKS, LANES), F32),
                        pltpu.SemaphoreType.DMA(())])
    return pl.pallas_call(
        _final_kernel,
        grid_spec=grid_spec,
        out_shape=jax.ShapeDtypeStruct((n, d), F32),
        compiler_params=_params(("arbitrary",)),
        name="final",
    )(dest, ys, h1, gate, g.reshape(1, d), b.reshape(1, d))


def kernel(x, ln0_g, ln0_b, w_in, b_in, rpb, w_proj_a, w_proj_b, w_o, b_o, ln1_g, ln1_b,
           w_router_group, b_router_group, w_router_expert, b_router_expert,
           w_gate, w_up, w_down, ln2_g, ln2_b):
    bsz, s, d = x.shape
    n = bsz * s
    assert d == D_MODEL and w_in.shape[0] == DEPTH
    scale = HEAD_DIM ** -0.5

    col_scale = np.ones((QKV_WIDTH,), np.float32)
    col_scale[:NA_WIDTH] = scale
    col_scale[3 * NA_WIDTH:3 * NA_WIDTH + DIL_WIDTH] = scale
    w_qkv = w_in[0][:, :QKV_WIDTH] * col_scale
    b_qkv = b_in[0][:QKV_WIDTH] * col_scale

    def group_cols(t, g):
        gw = DIL_OUT_WIDTH
        return [t[..., 3 * NA_WIDTH + p * DIL_WIDTH + g * gw:3 * NA_WIDTH + p * DIL_WIDTH + (g + 1) * gw]
                for p in range(3)]

    w_proj = [jnp.concatenate([w_qkv[:, :3 * NA_WIDTH]] + group_cols(w_qkv, 0), axis=1).astype(BF16)]
    b_proj = [jnp.concatenate([b_qkv[:3 * NA_WIDTH]] + group_cols(b_qkv, 0))]
    for g in range(1, len(DIL_GROUPS)):
        w_proj.append(jnp.concatenate(group_cols(w_qkv, g), axis=1).astype(BF16))
        b_proj.append(jnp.concatenate(group_cols(b_qkv, g)))
    w_ga = w_in[0][:, QKV_WIDTH:QKV_WIDTH + d].astype(BF16)
    w_gb = w_in[0][:, QKV_WIDTH + d:].astype(BF16)
    b_ga = b_in[0][QKV_WIDTH:QKV_WIDTH + d]
    b_gb = b_in[0][QKV_WIDTH + d:]
    w_r = jnp.concatenate([w_router_group[0], w_router_expert[0]], axis=1)
    w_r = jnp.pad(w_r, ((0, 0), (0, ROUTER_PAD - w_r.shape[1])))
    w_r_hi = w_r.astype(BF16)
    w_r_lo = (w_r - w_r_hi.astype(F32)).astype(BF16)
    b_r = jnp.pad(jnp.concatenate([b_router_group[0], b_router_expert[0]]),
                  (0, ROUTER_PAD - N_GROUPS - N_EXPERTS)).reshape(1, ROUTER_PAD)
    tabs = _na_bias_tables(rpb[0], s // GRID_W)
    slopes = _alibi_slopes(DIL_HEADS)

    h, hb = _ln0(x.reshape(n, d), ln0_g, ln0_b)
    zq = [_inproj(hb, w_proj[g], b_proj[g], bsz, s, DIL_GROUPS[g][1]) for g in range(len(DIL_GROUPS))]
    na = _neighborhood_attention(zq[0].reshape(-1, bsz, s, HEAD_DIM), tabs)
    na = na.reshape(NA_HEADS, n, HEAD_DIM)
    dil = [_dilated_group(zq[g], DIL0_H if g == 0 else 0, g, slopes) for g in range(len(DIL_GROUPS))]
    m = _mix(na, [o for o, _ in dil], [l for _, l in dil], hb,
             w_proj_a[0].astype(BF16), w_proj_b[0].astype(BF16), w_ga, w_gb, b_ga, b_gb, s)
    h1, logits = _outproj(m, h, w_o[0].astype(BF16), b_o[0], ln1_g[0], ln1_b[0], w_r_hi, w_r_lo, b_r)
    dest, gate, slot_tok, block_e, n_active = _route(logits, n)
    ys = _experts(h1, slot_tok, block_e, n_active, w_gate[0], w_up[0], w_down[0])
    out = _final(dest, ys, h1, gate, ln2_g[0], ln2_b[0])
    return out.reshape(bsz, s, d)
```

```python
import functools

import numpy as np
import jax
import jax.numpy as jnp
from jax import lax
from jax.experimental import pallas as pl
from jax.experimental.pallas import tpu as pltpu

F32 = jnp.float32
BF16 = jnp.bfloat16

D_MODEL = 2048
HEAD_DIM = 128
GRID_W = 64
NA_HEADS = 8
NA_KH = 8
NA_KW = 16
DIL_GROUPS = ((128, 1), (512, 4), (2048, 16))
DIL_HEADS_PER_GROUP = 4
DIL_HEADS = DIL_HEADS_PER_GROUP * len(DIL_GROUPS)
N_GROUPS = 8
EXPERTS_PER_GROUP = 8
N_EXPERTS = N_GROUPS * EXPERTS_PER_GROUP
TOP_K = 2
D_EXPERT = D_MODEL // 4
LN_EPS = 1e-5
DEPTH = 1
DN_ALPHA = (2 * DEPTH) ** 0.25
NA_WIDTH = NA_HEADS * HEAD_DIM
DIL_WIDTH = DIL_HEADS * HEAD_DIM
DIL_OUT_WIDTH = DIL_HEADS_PER_GROUP * HEAD_DIM
QKV_WIDTH = 3 * NA_WIDTH + 3 * DIL_WIDTH
QKV_HEADS = QKV_WIDTH // HEAD_DIM
LANES = 128
CHUNKS = D_MODEL // LANES
MASK_VALUE = -1e30

QA_H, KA_H, VA_H = 0, NA_HEADS, 2 * NA_HEADS
DIL0_H = 3 * NA_HEADS

LN_TM = 512
INPROJ_TM, INPROJ_TN = 1024, 512
NA_QROWS = 4
NA_KROWS = 12
DIL_TQ = 256
DIL_HALF = 64
DIL_TK = DIL_TQ + 2 * DIL_HALF
MIX_TM, MIX_TN = 512, 512
OUT_TM = 256
ROUTER_PAD = 128
ROUTE_TM = 512
MOE_TB = 128
FIN_TM = 128
VMEM_LIMIT = 56 * 1024 * 1024


def _params(sem, limit=VMEM_LIMIT):
    return pltpu.CompilerParams(dimension_semantics=sem, vmem_limit_bytes=limit)


def _layer_norm(x, g, b):
    mu = jnp.mean(x, axis=-1, keepdims=True)
    xc = x - mu
    var = jnp.mean(xc * xc, axis=-1, keepdims=True)
    return xc * lax.rsqrt(var + LN_EPS) * g + b


def _to_chunks(ref, val):
    for c in range(CHUNKS):
        ref[c] = val[:, c * LANES:(c + 1) * LANES]


def _from_chunks(ref):
    return jnp.concatenate([ref[c] for c in range(CHUNKS)], axis=-1)


def _tile_copies(hbm, row0, rows, vmem, sem, to_hbm):
    copies = []
    for c in range(CHUNKS):
        h, v = hbm.at[pl.ds(row0, rows), c], vmem.at[c]
        copies.append(pltpu.make_async_copy(v, h, sem) if to_hbm else pltpu.make_async_copy(h, v, sem))
    return copies


def _start_row_gather(idx_ref, base, step, count, src_hbm, dst, sem):
    def start(t, c):
        pltpu.make_async_copy(src_hbm.at[idx_ref[base + t * step]], dst.at[:, t], sem).start()
        return c

    lax.fori_loop(0, count, start, 0, unroll=4)


def _wait_row_gather(dst, sem):
    pltpu.make_async_copy(dst, dst, sem).wait()


def _ln0_kernel(x_ref, g_ref, b_ref, h_ref, hb_ref):
    y = _layer_norm(x_ref[...], g_ref[...], b_ref[...])
    h_ref[...] = y
    hb_ref[...] = y.astype(BF16)


def _ln0(x, g, b):
    n, d = x.shape
    row = pl.BlockSpec((LN_TM, d), lambda i: (i, 0))
    vec = pl.BlockSpec((1, d), lambda i: (0, 0))
    return pl.pallas_call(
        _ln0_kernel,
        grid=(n // LN_TM,),
        in_specs=[row, vec, vec],
        out_specs=[row, row],
        out_shape=[jax.ShapeDtypeStruct((n, d), F32), jax.ShapeDtypeStruct((n, d), BF16)],
        compiler_params=_params(("parallel",)),
        name="ln0",
    )(x, g.reshape(1, d), b.reshape(1, d))


def _inproj_kernel(hb_ref, w_ref, b_ref, o_ref, acc_s, *, dil):
    acc = jnp.dot(hb_ref[...], w_ref[...], preferred_element_type=F32) + b_ref[...]
    if dil == 1:
        for c in range(INPROJ_TN // LANES):
            o_ref[c, 0, 0] = acc[:, c * LANES:(c + 1) * LANES].astype(BF16)
    else:
        for c in range(INPROJ_TN // LANES):
            acc_s[c] = acc[:, c * LANES:(c + 1) * LANES]
        for c in range(INPROJ_TN // LANES):
            for r in range(dil):
                o_ref[c, 0, r] = acc_s[c, pl.ds(r, INPROJ_TM // dil, stride=dil), :].astype(BF16)


def _inproj(hb, w, b, bsz, s, dil):
    n, d = hb.shape
    width = w.shape[1]
    tiles = s // INPROJ_TM
    assert INPROJ_TM % (16 * dil) == 0 and s % INPROJ_TM == 0 and width % INPROJ_TN == 0
    return pl.pallas_call(
        functools.partial(_inproj_kernel, dil=dil),
        grid=(n // INPROJ_TM, width // INPROJ_TN),
        in_specs=[pl.BlockSpec((INPROJ_TM, d), lambda i, j: (i, 0)),
                  pl.BlockSpec((d, INPROJ_TN), lambda i, j: (0, j)),
                  pl.BlockSpec((1, INPROJ_TN), lambda i, j: (0, j))],
        out_specs=pl.BlockSpec((INPROJ_TN // LANES, 1, dil, INPROJ_TM // dil, LANES),
                               lambda i, j: (j, i // tiles, 0, i % tiles, 0)),
        out_shape=jax.ShapeDtypeStruct((width // LANES, bsz, dil, s // dil, LANES), BF16),
        scratch_shapes=[pltpu.VMEM((INPROJ_TN // LANES, INPROJ_TM, LANES), F32)],
        compiler_params=_params(("parallel", "arbitrary")),
        name=f"inproj_{dil}",
    )(hb, w, b.reshape(1, width))


def _na_bias_tables(rpb, rows):
    heads, n_dr, n_dc = rpb.shape
    qc = np.arange(GRID_W)[:, None]
    kc = np.arange(GRID_W)[None, :]
    qcs = np.clip(qc - NA_KW // 2, 0, GRID_W - NA_KW)
    v_col = (kc >= qcs) & (kc < qcs + NA_KW)
    dc = np.clip(kc - qc + NA_KW - 1, 0, n_dc - 1)
    onehot = (dc[None] == np.arange(n_dc)[:, None, None]) & v_col[None]
    toep = jnp.einsum('hrd,dqk->hrqk', rpb.astype(F32), jnp.asarray(onehot, F32),
                      precision=lax.Precision.HIGHEST)
    toep = jnp.where(jnp.asarray(v_col), toep, MASK_VALUE)
    masked = jnp.full((heads, 1, GRID_W, GRID_W), MASK_VALUE, F32)
    blocks = jnp.concatenate([toep, masked], axis=1)
    i = np.arange(NA_QROWS)[:, None]
    j = np.arange(NA_KROWS)[None, :]
    sel = []
    for r0, ks in ((0, 0), (2 * NA_QROWS, 2 * NA_QROWS - NA_KH // 2), (rows - NA_QROWS, rows - NA_KROWS)):
        r, krow = r0 + i, ks + j
        start = np.clip(r - NA_KH // 2, 0, rows - NA_KH)
        v_row = (krow >= start) & (krow < start + NA_KH)
        sel.append(np.where(v_row, krow - r + NA_KH - 1, n_dr))
    sel = np.stack(sel).reshape(-1)
    tab = jnp.take(blocks, jnp.asarray(sel, jnp.int32), axis=1)
    tab = tab.reshape(heads, 3, NA_QROWS, NA_KROWS, GRID_W, GRID_W).transpose(1, 0, 2, 4, 3, 5)
    return tab.reshape(3, heads, NA_QROWS * GRID_W, NA_KROWS * GRID_W)


def _na_kernel(q_ref, k_ref, v_ref, tab_ref, o_ref, *, rows):
    i = pl.program_id(2)
    last = pl.num_programs(2) - 1
    ks = jnp.clip(i * NA_QROWS - NA_KH // 2, 0, rows - NA_KROWS)
    off = pl.multiple_of(ks * GRID_W, GRID_W)
    kw = k_ref[0, 0, pl.ds(off, NA_KROWS * GRID_W), :]
    vw = v_ref[0, 0, pl.ds(off, NA_KROWS * GRID_W), :]
    variant = jnp.where(i == 0, 0, jnp.where(i == last, 2, 1))
    s = lax.dot_general(q_ref[0, 0], kw, (((1,), (1,)), ((), ())), preferred_element_type=F32)
    s = s + tab_ref[variant, 0]
    m = jnp.max(s, axis=-1, keepdims=True)
    p = jnp.exp(s - m)
    l = jnp.sum(p, axis=-1, keepdims=True)
    o = jnp.dot(p.astype(BF16), vw, preferred_element_type=F32) / l
    o_ref[0, 0] = o.astype(BF16)


def _neighborhood_attention(zq4, tabs):
    _, bsz, s, _ = zq4.shape
    rows = s // GRID_W
    assert rows % NA_QROWS == 0 and rows >= NA_KROWS + NA_QROWS
    tq = NA_QROWS * GRID_W
    tk = NA_KROWS * GRID_W
    return pl.pallas_call(
        functools.partial(_na_kernel, rows=rows),
        grid=(bsz, NA_HEADS, rows // NA_QROWS),
        in_specs=[pl.BlockSpec((1, 1, tq, HEAD_DIM), lambda b, h, i: (QA_H + h, b, i, 0)),
                  pl.BlockSpec((1, 1, s, HEAD_DIM), lambda b, h, i: (KA_H + h, b, 0, 0)),
                  pl.BlockSpec((1, 1, s, HEAD_DIM), lambda b, h, i: (VA_H + h, b, 0, 0)),
                  pl.BlockSpec((3, 1, tq, tk), lambda b, h, i: (0, h, 0, 0))],
        out_specs=pl.BlockSpec((1, 1, tq, HEAD_DIM), lambda b, h, i: (h, b, i, 0)),
        out_shape=jax.ShapeDtypeStruct((NA_HEADS, bsz, s, HEAD_DIM), BF16),
        compiler_params=_params(("parallel", "parallel", "arbitrary")),
        name="na_attn",
    )(zq4, zq4, zq4, tabs)


def _alibi_slopes(n):
    return np.array([2.0 ** (-8.0 * (i + 1) / n) for i in range(n)], dtype=np.float32)


def _dil_kernel(slope_ref, q_ref, k_ref, v_ref, o_ref, lse_ref, *, dil, length):
    n = pl.program_id(3)
    ks = jnp.clip(n * DIL_TQ - DIL_HALF, 0, length - DIL_TK)
    off = pl.multiple_of(ks, DIL_HALF)
    kw = k_ref[0, 0, 0, pl.ds(off, DIL_TK), :]
    vw = v_ref[0, 0, 0, pl.ds(off, DIL_TK), :]
    s = lax.dot_general(q_ref[0, 0, 0], kw, (((1,), (1,)), ((), ())), preferred_element_type=F32)
    qi = lax.broadcasted_iota(jnp.int32, (DIL_TQ, DIL_TK), 0)
    kj = lax.broadcasted_iota(jnp.int32, (DIL_TQ, DIL_TK), 1)
    dist = jnp.abs(kj - qi + (ks - n * DIL_TQ))
    penalty = slope_ref[pl.program_id(1)] * float(dil)
    s = jnp.where(dist <= DIL_HALF, s - penalty * dist.astype(F32), MASK_VALUE)
    m = jnp.max(s, axis=-1, keepdims=True)
    p = jnp.exp(s - m)
    l = jnp.sum(p, axis=-1, keepdims=True)
    o = jnp.dot(p.astype(BF16), vw, preferred_element_type=F32) / l
    o_ref[0, 0, 0] = o.astype(BF16)
    lse_ref[0, 0, 0] = jnp.broadcast_to(m + jnp.log(l), (DIL_TQ, LANES))


def _dilated_group(zq, base, g, slopes):
    window, dil = DIL_GROUPS[g]
    assert (window // 2) // dil == DIL_HALF
    _, bsz, _, length, _ = zq.shape
    assert zq.shape[2] == dil and length % DIL_TQ == 0 and length >= DIL_TK
    gh = DIL_HEADS_PER_GROUP
    qspec = pl.BlockSpec((1, 1, 1, DIL_TQ, HEAD_DIM), lambda b, h, r, n: (base + h, b, r, n, 0))
    kvspec = lambda off: pl.BlockSpec((1, 1, 1, length, HEAD_DIM),
                                      lambda b, h, r, n: (base + off + h, b, r, 0, 0))
    ospec = pl.BlockSpec((1, 1, 1, DIL_TQ, HEAD_DIM), lambda b, h, r, n: (h, b, r, n, 0))
    return pl.pallas_call(
        functools.partial(_dil_kernel, dil=dil, length=length),
        grid=(bsz, gh, dil, length // DIL_TQ),
        in_specs=[pl.BlockSpec(memory_space=pltpu.SMEM), qspec, kvspec(gh), kvspec(2 * gh)],
        out_specs=[ospec, ospec],
        out_shape=[jax.ShapeDtypeStruct((gh, bsz, dil, length, HEAD_DIM), BF16),
                   jax.ShapeDtypeStruct((gh, bsz, dil, length, HEAD_DIM), F32)],
        compiler_params=_params(("parallel", "parallel", "parallel", "arbitrary")),
        name=f"dil_attn_{dil}",
    )(jnp.asarray(slopes[g * gh:(g + 1) * gh]), zq, zq, zq)


def _mix_kernel(na_ref, o1_ref, o2_ref, o3_ref, l1_ref, l2_ref, l3_ref, hb_ref,
                wpa_ref, wpb_ref, wga_ref, wgb_ref, bga_ref, bgb_ref, m_ref, na_s, dil_s, o_s, l_s):
    @pl.when(pl.program_id(1) == 0)
    def _():
        for h in range(NA_HEADS):
            na_s[:, h * HEAD_DIM:(h + 1) * HEAD_DIM] = na_ref[h]
        for g, (o_ref, l_ref) in enumerate(((o1_ref, l1_ref), (o2_ref, l2_ref), (o3_ref, l3_ref))):
            dil = DIL_GROUPS[g][1]
            for h in range(DIL_HEADS_PER_GROUP):
                for r in range(dil):
                    rows = slice(None) if dil == 1 else pl.ds(r, MIX_TM // dil, stride=dil)
                    o_s[g, h, rows, :] = o_ref[h, 0, r].astype(F32)
                    l_s[g, h, rows, :] = l_ref[h, 0, r]
        for h in range(DIL_HEADS_PER_GROUP):
            ls = [l_s[g, h] for g in range(len(DIL_GROUPS))]
            mx = jnp.maximum(jnp.maximum(ls[0], ls[1]), ls[2])
            es = [jnp.exp(l - mx) for l in ls]
            den = es[0] + es[1] + es[2]
            acc = (es[0] * o_s[0, h] + es[1] * o_s[1, h] + es[2] * o_s[2, h]) / den
            dil_s[:, h * HEAD_DIM:(h + 1) * HEAD_DIM] = acc.astype(BF16)

    hb = hb_ref[...]
    ya = jnp.dot(na_s[...], wpa_ref[...], preferred_element_type=F32)
    yb = jnp.dot(dil_s[...], wpb_ref[...], preferred_element_type=F32)
    ga = jnp.dot(hb, wga_ref[...], preferred_element_type=F32) + bga_ref[...]
    gb = jnp.dot(hb, wgb_ref[...], preferred_element_type=F32) + bgb_ref[...]
    m_ref[...] = (jax.nn.sigmoid(ga) * ya + jax.nn.sigmoid(gb) * yb).astype(BF16)


def _mix(na, dil_o, dil_lse, hb, wpa, wpb, wga, wgb, bga, bgb, s):
    n, d = hb.shape
    tm, tn = MIX_TM, MIX_TN
    tiles = s // tm
    gh = DIL_HEADS_PER_GROUP
    ngroups = len(DIL_GROUPS)
    assert all(tm % (16 * dil) == 0 for _, dil in DIL_GROUPS) and s % tm == 0
    col = lambda k: pl.BlockSpec((k, tn), lambda i, j: (0, j))
    grp = lambda dil: pl.BlockSpec((gh, 1, dil, tm // dil, HEAD_DIM),
                                   lambda i, j: (0, i // tiles, 0, i % tiles, 0))
    groups = [grp(dil) for _, dil in DIL_GROUPS]
    return pl.pallas_call(
        _mix_kernel,
        grid=(n // tm, d // tn),
        in_specs=[pl.BlockSpec((NA_HEADS, tm, HEAD_DIM), lambda i, j: (0, i, 0)), *groups, *groups,
                  pl.BlockSpec((tm, d), lambda i, j: (i, 0)),
                  col(NA_WIDTH), col(DIL_OUT_WIDTH), col(d), col(d), col(1), col(1)],
        out_specs=pl.BlockSpec((tm, tn), lambda i, j: (i, j)),
        out_shape=jax.ShapeDtypeStruct((n, d), BF16),
        scratch_shapes=[pltpu.VMEM((tm, NA_WIDTH), BF16), pltpu.VMEM((tm, DIL_OUT_WIDTH), BF16),
                        pltpu.VMEM((ngroups, gh, tm, HEAD_DIM), F32),
                        pltpu.VMEM((ngroups, gh, tm, HEAD_DIM), F32)],
        compiler_params=_params(("parallel", "arbitrary")),
        name="mix",
    )(na, *dil_o, *dil_lse, hb, wpa, wpb, wga, wgb, bga.reshape(1, d), bgb.reshape(1, d))


def _outproj_kernel(m_ref, h_ref, wo_ref, bo_ref, g_ref, b_ref, wrh_ref, wrl_ref, br_ref,
                    h1_hbm, lg_ref, hbuf, sem):
    i = pl.program_id(0)
    slot = i % 2

    def out_copies(step, sl):
        return _tile_copies(h1_hbm, step * OUT_TM, OUT_TM, hbuf.at[sl], sem.at[sl], True)

    mix = jnp.dot(m_ref[...], wo_ref[...], preferred_element_type=F32) + bo_ref[...]
    h1 = _layer_norm(DN_ALPHA * h_ref[...] + mix, g_ref[...], b_ref[...])

    @pl.when(i >= 2)
    def _():
        for cp in out_copies(i - 2, slot):
            cp.wait()

    _to_chunks(hbuf.at[slot], h1)
    for cp in out_copies(i, slot):
        cp.start()

    @pl.when(i == pl.num_programs(0) - 1)
    def _():
        for cp in out_copies(i, slot):
            cp.wait()

        @pl.when(i >= 1)
        def _():
            for cp in out_copies(i - 1, 1 - slot):
                cp.wait()

    hi = h1.astype(BF16)
    lo = (h1 - hi.astype(F32)).astype(BF16)
    lg = (jnp.dot(hi, wrh_ref[...], preferred_element_type=F32)
          + jnp.dot(lo, wrh_ref[...], preferred_element_type=F32)
          + jnp.dot(hi, wrl_ref[...], preferred_element_type=F32))
    lg_ref[...] = lg + br_ref[...]


def _outproj(m, h, wo, bo, g, b, wr_hi, wr_lo, br):
    n, d = h.shape
    tm = OUT_TM
    row = lambda w: pl.BlockSpec((tm, w), lambda i: (i, 0))
    full = lambda r, c: pl.BlockSpec((r, c), lambda i: (0, 0))
    return pl.pallas_call(
        _outproj_kernel,
        grid=(n // tm,),
        in_specs=[row(d), row(d), full(d, d), full(1, d), full(1, d), full(1, d),
                  full(d, ROUTER_PAD), full(d, ROUTER_PAD), full(1, ROUTER_PAD)],
        out_specs=[pl.BlockSpec(memory_space=pl.ANY), row(ROUTER_PAD)],
        out_shape=[jax.ShapeDtypeStruct((n, CHUNKS, LANES), F32),
                   jax.ShapeDtypeStruct((n, ROUTER_PAD), F32)],
        scratch_shapes=[pltpu.VMEM((2, CHUNKS, tm, LANES), F32), pltpu.SemaphoreType.DMA((2,))],
        compiler_params=_params(("arbitrary",)),
        name="outproj",
    )(m, h, wo, bo.reshape(1, d), g.reshape(1, d), b.reshape(1, d), wr_hi, wr_lo, br)


def _first_argmax(vals, lane_f):
    top = jnp.max(vals, axis=-1, keepdims=True)
    idx = jnp.min(jnp.where(vals == top, lane_f, float(LANES)), axis=-1, keepdims=True)
    return top, idx


def _route_kernel(lg_ref, info_ref, gate_ref, cnt_ref, base_s):
    @pl.when(pl.program_id(0) == 0)
    def _():
        base_s[...] = jnp.zeros(base_s.shape, F32)

    lg = lg_ref[...]
    lane = lax.broadcasted_iota(jnp.int32, lg.shape, 1)
    lane_f = lane.astype(F32)
    g_mask = lane < N_GROUPS
    g_top, g_sel = _first_argmax(jnp.where(g_mask, lg, MASK_VALUE), lane_f)
    g_prob = 1.0 / jnp.sum(jnp.where(g_mask, jnp.exp(lg - g_top), 0.0), axis=-1, keepdims=True)
    first = N_GROUPS + g_sel * EXPERTS_PER_GROUP
    e_mask = (lane_f >= first) & (lane_f < first + EXPERTS_PER_GROUP)
    el = jnp.where(e_mask, lg, MASK_VALUE)
    v0, i0 = _first_argmax(el, lane_f)
    v1, i1 = _first_argmax(jnp.where(lane_f == i0, MASK_VALUE, el), lane_f)
    e1 = jnp.exp(v1 - v0)
    w0 = g_prob / (1.0 + e1)
    w1 = g_prob * e1 / (1.0 + e1)

    tm = lg.shape[0]
    tri = (lax.broadcasted_iota(jnp.int32, (tm, tm), 1)
           < lax.broadcasted_iota(jnp.int32, (tm, tm), 0)).astype(BF16)
    ranks = []
    for idx in (i0, i1):
        onehot = lane_f == idx - N_GROUPS
        before = jnp.dot(tri, onehot.astype(BF16), preferred_element_type=F32) + base_s[...]
        ranks.append(jnp.sum(jnp.where(onehot, before, 0.0), axis=-1, keepdims=True))
        base_s[...] = base_s[...] + jnp.sum(onehot.astype(F32), axis=0, keepdims=True)

    info = jnp.where(lane == 0, i0 - N_GROUPS,
                     jnp.where(lane == 1, i1 - N_GROUPS,
                               jnp.where(lane == 2, ranks[0], jnp.where(lane == 3, ranks[1], 0.0))))
    info_ref[...] = info.astype(jnp.int32)
    gate_ref[...] = jnp.where(lane == 0, w0, jnp.where(lane == 1, w1, 0.0))
    cnt_ref[...] = base_s[...]


def _route(logits, n):
    tm = ROUTE_TM
    row = pl.BlockSpec((tm, ROUTER_PAD), lambda i: (i, 0))
    one = pl.BlockSpec((1, ROUTER_PAD), lambda i: (0, 0))
    info, gate, counts = pl.pallas_call(
        _route_kernel,
        grid=(n // tm,),
        in_specs=[row],
        out_specs=[row, row, one],
        out_shape=[jax.ShapeDtypeStruct((n, ROUTER_PAD), jnp.int32),
                   jax.ShapeDtypeStruct((n, ROUTER_PAD), F32),
                   jax.ShapeDtypeStruct((1, ROUTER_PAD), F32)],
        scratch_shapes=[pltpu.VMEM((1, ROUTER_PAD), F32)],
        compiler_params=_params(("arbitrary",)),
        name="route",
    )(logits)
    gate = gate[:, :TOP_K]
    a = n * TOP_K
    e_flat = info[:, :TOP_K].reshape(a)
    rank = info[:, TOP_K:2 * TOP_K].reshape(a)
    counts = counts[0, :N_EXPERTS].astype(jnp.int32)
    pcounts = (counts + MOE_TB - 1) // MOE_TB * MOE_TB
    pends = jnp.cumsum(pcounts)
    pstarts = pends - pcounts
    dest = (pstarts[e_flat] + rank).astype(jnp.int32)
    nb = a // MOE_TB + N_EXPERTS
    slots = nb * MOE_TB
    tok = jnp.arange(a, dtype=jnp.int32) // TOP_K
    slot_tok = jnp.zeros((slots,), jnp.int32).at[dest].set(tok)
    block_start = jnp.arange(nb, dtype=jnp.int32) * MOE_TB
    block_e = jnp.minimum(jnp.sum((pends[None, :] <= block_start[:, None]).astype(jnp.int32), axis=1),
                          N_EXPERTS - 1)
    n_active = (pends[-1] // MOE_TB).astype(jnp.int32).reshape(1)
    return dest, gate, slot_tok, block_e, n_active


def _expert_kernel(be_ref, nact_ref, tok_ref, h1_hbm, wg_ref, wu_ref, wd_ref, ys_hbm,
                   xbuf, ybuf, wg_s, wu_s, wd_s, gsem, osem):
    i = pl.program_id(0)
    slot = i % 2
    nact = nact_ref[0]

    def gather(step, sl):
        _start_row_gather(tok_ref, step * MOE_TB, 1, MOE_TB, h1_hbm, xbuf.at[sl], gsem.at[sl])

    def out_copies(step, sl):
        return _tile_copies(ys_hbm, step * MOE_TB, MOE_TB, ybuf.at[sl], osem.at[sl], True)

    @pl.when((i == 0) & (nact > 0))
    def _():
        gather(0, 0)

    @pl.when(i + 1 < nact)
    def _():
        gather(i + 1, 1 - slot)

    @pl.when(i >= 2)
    def _():
        for cp in out_copies(i - 2, slot):
            cp.wait()

    @pl.when(i < nact)
    def _():
        prev = be_ref[jnp.maximum(i - 1, 0)]

        @pl.when((i == 0) | (be_ref[i] != prev))
        def _():
            wg_s[...] = wg_ref[0].astype(BF16)
            wu_s[...] = wu_ref[0].astype(BF16)
            wd_s[...] = wd_ref[0].astype(BF16)

        _wait_row_gather(xbuf.at[slot], gsem.at[slot])
        xb = _from_chunks(xbuf.at[slot]).astype(BF16)
        gate = jnp.dot(xb, wg_s[...], preferred_element_type=F32)
        up = jnp.dot(xb, wu_s[...], preferred_element_type=F32)
        hid = (jax.nn.silu(gate) * up).astype(BF16)
        _to_chunks(ybuf.at[slot], jnp.dot(hid, wd_s[...], preferred_element_type=F32))

    @pl.when(i >= nact)
    def _():
        ybuf[slot] = jnp.zeros(ybuf.shape[1:], F32)

    for cp in out_copies(i, slot):
        cp.start()

    @pl.when(i == pl.num_programs(0) - 1)
    def _():
        for cp in out_copies(i, slot):
            cp.wait()

        @pl.when(i >= 1)
        def _():
            for cp in out_copies(i - 1, 1 - slot):
                cp.wait()


def _experts(h1, slot_tok, block_e, n_active, w_gate, w_up, w_down):
    d = D_MODEL
    slots = slot_tok.shape[0]
    nb = slots // MOE_TB
    grid_spec = pltpu.PrefetchScalarGridSpec(
        num_scalar_prefetch=3,
        grid=(nb,),
        in_specs=[pl.BlockSpec(memory_space=pl.ANY),
                  pl.BlockSpec((1, d, D_EXPERT), lambda i, be, na, tk: (be[i], 0, 0)),
                  pl.BlockSpec((1, d, D_EXPERT), lambda i, be, na, tk: (be[i], 0, 0)),
                  pl.BlockSpec((1, D_EXPERT, d), lambda i, be, na, tk: (be[i], 0, 0))],
        out_specs=pl.BlockSpec(memory_space=pl.ANY),
        scratch_shapes=[pltpu.VMEM((2, CHUNKS, MOE_TB, LANES), F32),
                        pltpu.VMEM((2, CHUNKS, MOE_TB, LANES), F32),
                        pltpu.VMEM((d, D_EXPERT), BF16), pltpu.VMEM((d, D_EXPERT), BF16),
                        pltpu.VMEM((D_EXPERT, d), BF16),
                        pltpu.SemaphoreType.DMA((2,)), pltpu.SemaphoreType.DMA((2,))])
    return pl.pallas_call(
        _expert_kernel,
        grid_spec=grid_spec,
        out_shape=jax.ShapeDtypeStruct((slots, CHUNKS, LANES), F32),
        compiler_params=_params(("arbitrary",)),
        name="experts",
    )(block_e, n_active, slot_tok, h1, w_gate, w_up, w_down)


def _final_kernel(dest_ref, ys_hbm, h1_hbm, gate_ref, g_ref, b_ref, o_ref, ybuf, hbuf, gsem, hsem):
    i = pl.program_id(0)
    slot = i % 2

    def h1_copies(step, sl):
        return _tile_copies(h1_hbm, step * FIN_TM, FIN_TM, hbuf.at[sl], hsem.at[sl], False)

    def fetch(step, sl):
        for k in range(TOP_K):
            _start_row_gather(dest_ref, step * FIN_TM * TOP_K + k, TOP_K, FIN_TM, ys_hbm,
                              ybuf.at[sl, k], gsem.at[sl])
        for cp in h1_copies(step, sl):
            cp.start()

    @pl.when(i == 0)
    def _():
        fetch(0, 0)

    @pl.when(i + 1 < pl.num_programs(0))
    def _():
        fetch(i + 1, 1 - slot)

    _wait_row_gather(ybuf.at[slot], gsem.at[slot])
    for cp in h1_copies(i, slot):
        cp.wait()
    gate = gate_ref[...]
    ffn = (_from_chunks(ybuf.at[slot, 0]) * gate[:, 0:1]
           + _from_chunks(ybuf.at[slot, 1]) * gate[:, 1:2])
    o_ref[...] = _layer_norm(DN_ALPHA * _from_chunks(hbuf.at[slot]) + ffn, g_ref[...], b_ref[...])


def _final(dest, ys, h1, gate, g, b):
    n, d = h1.shape[0], D_MODEL
    grid_spec = pltpu.PrefetchScalarGridSpec(
        num_scalar_prefetch=1,
        grid=(n // FIN_TM,),
        in_specs=[pl.BlockSpec(memory_space=pl.ANY),
                  pl.BlockSpec(memory_space=pl.ANY),
                  pl.BlockSpec((FIN_TM, TOP_K), lambda i, ds: (i, 0)),
                  pl.BlockSpec((1, d), lambda i, ds: (0, 0)),
                  pl.BlockSpec((1, d), lambda i, ds: (0, 0))],
        out_specs=pl.BlockSpec((FIN_TM, d), lambda i, ds: (i, 0)),
        scratch_shapes=[pltpu.VMEM((2, TOP_K, CHUNKS, FIN_TM, LANES), F32),
                        pltpu.VMEM((2, CHUNKS, FIN_TM, LANES), F32),
                        pltpu.SemaphoreType.DMA((2,)), pltpu.SemaphoreType.DMA((2,))])
    return pl.pallas_call(
        _final_kernel,
        grid_spec=grid_spec,
        out_shape=jax.ShapeDtypeStruct((n, d), F32),
        compiler_params=_params(("arbitrary",)),
        name="final",
    )(dest, ys, h1, gate, g.reshape(1, d), b.reshape(1, d))


def kernel(x, ln0_g, ln0_b, w_in, b_in, rpb, w_proj_a, w_proj_b, w_o, b_o, ln1_g, ln1_b,
           w_router_group, b_router_group, w_router_expert, b_router_expert,
           w_gate, w_up, w_down, ln2_g, ln2_b):
    bsz, s, d = x.shape
    n = bsz * s
    assert d == D_MODEL and w_in.shape[0] == DEPTH
    scale = HEAD_DIM ** -0.5

    col_scale = np.ones((QKV_WIDTH,), np.float32)
    col_scale[:NA_WIDTH] = scale
    col_scale[3 * NA_WIDTH:3 * NA_WIDTH + DIL_WIDTH] = scale
    w_qkv = w_in[0][:, :QKV_WIDTH] * col_scale
    b_qkv = b_in[0][:QKV_WIDTH] * col_scale

    def group_cols(t, g):
        gw = DIL_OUT_WIDTH
        return [t[..., 3 * NA_WIDTH + p * DIL_WIDTH + g * gw:3 * NA_WIDTH + p * DIL_WIDTH + (g + 1) * gw]
                for p in range(3)]

    w_proj = [jnp.concatenate([w_qkv[:, :3 * NA_WIDTH]] + group_cols(w_qkv, 0), axis=1).astype(BF16)]
    b_proj = [jnp.concatenate([b_qkv[:3 * NA_WIDTH]] + group_cols(b_qkv, 0))]
    for g in range(1, len(DIL_GROUPS)):
        w_proj.append(jnp.concatenate(group_cols(w_qkv, g), axis=1).astype(BF16))
        b_proj.append(jnp.concatenate(group_cols(b_qkv, g)))
    w_ga = w_in[0][:, QKV_WIDTH:QKV_WIDTH + d].astype(BF16)
    w_gb = w_in[0][:, QKV_WIDTH + d:].astype(BF16)
    b_ga = b_in[0][QKV_WIDTH:QKV_WIDTH + d]
    b_gb = b_in[0][QKV_WIDTH + d:]
    w_r = jnp.concatenate([w_router_group[0], w_router_expert[0]], axis=1)
    w_r = jnp.pad(w_r, ((0, 0), (0, ROUTER_PAD - w_r.shape[1])))
    w_r_hi = w_r.astype(BF16)
    w_r_lo = (w_r - w_r_hi.astype(F32)).astype(BF16)
    b_r = jnp.pad(jnp.concatenate([b_router_group[0], b_router_expert[0]]),
                  (0, ROUTER_PAD - N_GROUPS - N_EXPERTS)).reshape(1, ROUTER_PAD)
    tabs = _na_bias_tables(rpb[0], s // GRID_W)
    slopes = _alibi_slopes(DIL_HEADS)

    h, hb = _ln0(x.reshape(n, d), ln0_g, ln0_b)
    zq = [_inproj(hb, w_proj[g], b_proj[g], bsz, s, DIL_GROUPS[g][1]) for g in range(len(DIL_GROUPS))]
    na = _neighborhood_attention(zq[0].reshape(-1, bsz, s, HEAD_DIM), tabs)
    na = na.reshape(NA_HEADS, n, HEAD_DIM)
    dil = [_dilated_group(zq[g], DIL0_H if g == 0 else 0, g, slopes) for g in range(len(DIL_GROUPS))]
    m = _mix(na, [o for o, _ in dil], [l for _, l in dil], hb,
             w_proj_a[0].astype(BF16), w_proj_b[0].astype(BF16), w_ga, w_gb, b_ga, b_gb, s)
    h1, logits = _outproj(m, h, w_o[0].astype(BF16), b_o[0], ln1_g[0], ln1_b[0], w_r_hi, w_r_lo, b_r)
    dest, gate, slot_tok, block_e, n_active = _route(logits, n)
    ys = _experts(h1, slot_tok, block_e, n_active, w_gate[0], w_up[0], w_down[0])
    out = _final(dest, ys, h1, gate, ln2_g[0], ln2_b[0])
    return out.reshape(bsz, s, d)
```

```python
import functools

import numpy as np
import jax
import jax.numpy as jnp
from jax import lax
from jax.experimental import pallas as pl
from jax.experimental.pallas import tpu as pltpu

F32 = jnp.float32
BF16 = jnp.bfloat16

D_MODEL = 2048
HEAD_DIM = 128
GRID_W = 64
NA_HEADS = 8
NA_KH = 8
NA_KW = 16
DIL_GROUPS = ((128, 1), (512, 4), (2048, 16))
DIL_HEADS_PER_GROUP = 4
DIL_HEADS = DIL_HEADS_PER_GROUP * len(DIL_GROUPS)
N_GROUPS = 8
EXPERTS_PER_GROUP = 8
N_EXPERTS = N_GROUPS * EXPERTS_PER_GROUP
TOP_K = 2
D_EXPERT = D_MODEL // 4
LN_EPS = 1e-5
DEPTH = 1
DN_ALPHA = (2 * DEPTH) ** 0.25
NA_WIDTH = NA_HEADS * HEAD_DIM
DIL_WIDTH = DIL_HEADS * HEAD_DIM
DIL_OUT_WIDTH = DIL_HEADS_PER_GROUP * HEAD_DIM
QKV_WIDTH = 3 * NA_WIDTH + 3 * DIL_WIDTH
QKV_HEADS = QKV_WIDTH // HEAD_DIM
LANES = 128
CHUNKS = D_MODEL // LANES
MASK_VALUE = -1e30

QA_H, KA_H, VA_H = 0, NA_HEADS, 2 * NA_HEADS
DIL0_H = 3 * NA_HEADS

LN_TM = 512
INPROJ_TM, INPROJ_TN = 1024, 512
NA_QROWS = 4
NA_KROWS = 12
DIL_TQ = 256
DIL_HALF = 64
DIL_TK = DIL_TQ + 2 * DIL_HALF
MIX_TM, MIX_TN = 512, 512
OUT_TM = 256
ROUTER_PAD = 128
ROUTE_TM = 512
MOE_TB = 256
FIN_TM = 128
VMEM_LIMIT = 56 * 1024 * 1024


def _params(sem, limit=VMEM_LIMIT):
    return pltpu.CompilerParams(dimension_semantics=sem, vmem_limit_bytes=limit)


def _layer_norm(x, g, b):
    mu = jnp.mean(x, axis=-1, keepdims=True)
    xc = x - mu
    var = jnp.mean(xc * xc, axis=-1, keepdims=True)
    return xc * lax.rsqrt(var + LN_EPS) * g + b


def _to_chunks(ref, val):
    for c in range(CHUNKS):
        ref[c] = val[:, c * LANES:(c + 1) * LANES]


def _from_chunks(ref):
    return jnp.concatenate([ref[c] for c in range(CHUNKS)], axis=-1)


def _tile_copies(hbm, row0, rows, vmem, sem, to_hbm):
    copies = []
    for c in range(CHUNKS):
        h, v = hbm.at[pl.ds(row0, rows), c], vmem.at[c]
        copies.append(pltpu.make_async_copy(v, h, sem) if to_hbm else pltpu.make_async_copy(h, v, sem))
    return copies


def _start_row_gather(idx_ref, base, step, count, src_hbm, dst, sem):
    def start(t, c):
        pltpu.make_async_copy(src_hbm.at[idx_ref[base + t * step]], dst.at[:, t], sem).start()
        return c

    lax.fori_loop(0, count, start, 0, unroll=4)


def _wait_row_gather(dst, sem):
    pltpu.make_async_copy(dst, dst, sem).wait()


def _ln0_kernel(x_ref, g_ref, b_ref, h_ref, hb_ref):
    y = _layer_norm(x_ref[...], g_ref[...], b_ref[...])
    h_ref[...] = y
    hb_ref[...] = y.astype(BF16)


def _ln0(x, g, b):
    n, d = x.shape
    row = pl.BlockSpec((LN_TM, d), lambda i: (i, 0))
    vec = pl.BlockSpec((1, d), lambda i: (0, 0))
    return pl.pallas_call(
        _ln0_kernel,
        grid=(n // LN_TM,),
        in_specs=[row, vec, vec],
        out_specs=[row, row],
        out_shape=[jax.ShapeDtypeStruct((n, d), F32), jax.ShapeDtypeStruct((n, d), BF16)],
        compiler_params=_params(("parallel",)),
        name="ln0",
    )(x, g.reshape(1, d), b.reshape(1, d))


def _inproj_kernel(hb_ref, w_ref, b_ref, o_ref, acc_s, *, dil):
    acc = jnp.dot(hb_ref[...], w_ref[...], preferred_element_type=F32) + b_ref[...]
    if dil == 1:
        for c in range(INPROJ_TN // LANES):
            o_ref[c, 0, 0] = acc[:, c * LANES:(c + 1) * LANES].astype(BF16)
    else:
        for c in range(INPROJ_TN // LANES):
            acc_s[c] = acc[:, c * LANES:(c + 1) * LANES]
        for c in range(INPROJ_TN // LANES):
            for r in range(dil):
                o_ref[c, 0, r] = acc_s[c, pl.ds(r, INPROJ_TM // dil, stride=dil), :].astype(BF16)


def _inproj(hb, w, b, bsz, s, dil):
    n, d = hb.shape
    width = w.shape[1]
    tiles = s // INPROJ_TM
    assert INPROJ_TM % (16 * dil) == 0 and s % INPROJ_TM == 0 and width % INPROJ_TN == 0
    return pl.pallas_call(
        functools.partial(_inproj_kernel, dil=dil),
        grid=(n // INPROJ_TM, width // INPROJ_TN),
        in_specs=[pl.BlockSpec((INPROJ_TM, d), lambda i, j: (i, 0)),
                  pl.BlockSpec((d, INPROJ_TN), lambda i, j: (0, j)),
                  pl.BlockSpec((1, INPROJ_TN), lambda i, j: (0, j))],
        out_specs=pl.BlockSpec((INPROJ_TN // LANES, 1, dil, INPROJ_TM // dil, LANES),
                               lambda i, j: (j, i // tiles, 0, i % tiles, 0)),
        out_shape=jax.ShapeDtypeStruct((width // LANES, bsz, dil, s // dil, LANES), BF16),
        scratch_shapes=[pltpu.VMEM((INPROJ_TN // LANES, INPROJ_TM, LANES), F32)],
        compiler_params=_params(("parallel", "arbitrary")),
        name=f"inproj_{dil}",
    )(hb, w, b.reshape(1, width))


def _na_bias_tables(rpb, rows):
    heads, n_dr, n_dc = rpb.shape
    qc = np.arange(GRID_W)[:, None]
    kc = np.arange(GRID_W)[None, :]
    qcs = np.clip(qc - NA_KW // 2, 0, GRID_W - NA_KW)
    v_col = (kc >= qcs) & (kc < qcs + NA_KW)
    dc = np.clip(kc - qc + NA_KW - 1, 0, n_dc - 1)
    onehot = (dc[None] == np.arange(n_dc)[:, None, None]) & v_col[None]
    toep = jnp.einsum('hrd,dqk->hrqk', rpb.astype(F32), jnp.asarray(onehot, F32),
                      precision=lax.Precision.HIGHEST)
    toep = jnp.where(jnp.asarray(v_col), toep, MASK_VALUE)
    masked = jnp.full((heads, 1, GRID_W, GRID_W), MASK_VALUE, F32)
    blocks = jnp.concatenate([toep, masked], axis=1)
    i = np.arange(NA_QROWS)[:, None]
    j = np.arange(NA_KROWS)[None, :]
    sel = []
    for r0, ks in ((0, 0), (2 * NA_QROWS, 2 * NA_QROWS - NA_KH // 2), (rows - NA_QROWS, rows - NA_KROWS)):
        r, krow = r0 + i, ks + j
        start = np.clip(r - NA_KH // 2, 0, rows - NA_KH)
        v_row = (krow >= start) & (krow < start + NA_KH)
        sel.append(np.where(v_row, krow - r + NA_KH - 1, n_dr))
    sel = np.stack(sel).reshape(-1)
    tab = jnp.take(blocks, jnp.asarray(sel, jnp.int32), axis=1)
    tab = tab.reshape(heads, 3, NA_QROWS, NA_KROWS, GRID_W, GRID_W).transpose(1, 0, 2, 4, 3, 5)
    return tab.reshape(3, heads, NA_QROWS * GRID_W, NA_KROWS * GRID_W)


def _na_kernel(q_ref, k_ref, v_ref, tab_ref, o_ref, *, rows):
    i = pl.program_id(2)
    last = pl.num_programs(2) - 1
    ks = jnp.clip(i * NA_QROWS - NA_KH // 2, 0, rows - NA_KROWS)
    off = pl.multiple_of(ks * GRID_W, GRID_W)
    kw = k_ref[0, 0, pl.ds(off, NA_KROWS * GRID_W), :]
    vw = v_ref[0, 0, pl.ds(off, NA_KROWS * GRID_W), :]
    variant = jnp.where(i == 0, 0, jnp.where(i == last, 2, 1))
    s = lax.dot_general(q_ref[0, 0], kw, (((1,), (1,)), ((), ())), preferred_element_type=F32)
    s = s + tab_ref[variant, 0]
    m = jnp.max(s, axis=-1, keepdims=True)
    p = jnp.exp(s - m)
    l = jnp.sum(p, axis=-1, keepdims=True)
    o = jnp.dot(p.astype(BF16), vw, preferred_element_type=F32) / l
    o_ref[0, 0] = o.astype(BF16)


def _neighborhood_attention(zq4, tabs):
    _, bsz, s, _ = zq4.shape
    rows = s // GRID_W
    assert rows % NA_QROWS == 0 and rows >= NA_KROWS + NA_QROWS
    tq = NA_QROWS * GRID_W
    tk = NA_KROWS * GRID_W
    return pl.pallas_call(
        functools.partial(_na_kernel, rows=rows),
        grid=(bsz, NA_HEADS, rows // NA_QROWS),
        in_specs=[pl.BlockSpec((1, 1, tq, HEAD_DIM), lambda b, h, i: (QA_H + h, b, i, 0)),
                  pl.BlockSpec((1, 1, s, HEAD_DIM), lambda b, h, i: (KA_H + h, b, 0, 0)),
                  pl.BlockSpec((1, 1, s, HEAD_DIM), lambda b, h, i: (VA_H + h, b, 0, 0)),
                  pl.BlockSpec((3, 1, tq, tk), lambda b, h, i: (0, h, 0, 0))],
        out_specs=pl.BlockSpec((1, 1, tq, HEAD_DIM), lambda b, h, i: (h, b, i, 0)),
        out_shape=jax.ShapeDtypeStruct((NA_HEADS, bsz, s, HEAD_DIM), BF16),
        compiler_params=_params(("parallel", "parallel", "arbitrary")),
        name="na_attn",
    )(zq4, zq4, zq4, tabs)


def _alibi_slopes(n):
    return np.array([2.0 ** (-8.0 * (i + 1) / n) for i in range(n)], dtype=np.float32)


def _dil_kernel(slope_ref, q_ref, k_ref, v_ref, o_ref, lse_ref, *, dil, length):
    n = pl.program_id(3)
    ks = jnp.clip(n * DIL_TQ - DIL_HALF, 0, length - DIL_TK)
    off = pl.multiple_of(ks, DIL_HALF)
    kw = k_ref[0, 0, 0, pl.ds(off, DIL_TK), :]
    vw = v_ref[0, 0, 0, pl.ds(off, DIL_TK), :]
    s = lax.dot_general(q_ref[0, 0, 0], kw, (((1,), (1,)), ((), ())), preferred_element_type=F32)
    qi = lax.broadcasted_iota(jnp.int32, (DIL_TQ, DIL_TK), 0)
    kj = lax.broadcasted_iota(jnp.int32, (DIL_TQ, DIL_TK), 1)
    dist = jnp.abs(kj - qi + (ks - n * DIL_TQ))
    penalty = slope_ref[pl.program_id(1)] * float(dil)
    s = jnp.where(dist <= DIL_HALF, s - penalty * dist.astype(F32), MASK_VALUE)
    m = jnp.max(s, axis=-1, keepdims=True)
    p = jnp.exp(s - m)
    l = jnp.sum(p, axis=-1, keepdims=True)
    o = jnp.dot(p.astype(BF16), vw, preferred_element_type=F32) / l
    o_ref[0, 0, 0] = o.astype(BF16)
    lse_ref[0, 0, 0] = jnp.broadcast_to(m + jnp.log(l), (DIL_TQ, LANES))


def _dilated_group(zq, base, g, slopes):
    window, dil = DIL_GROUPS[g]
    assert (window // 2) // dil == DIL_HALF
    _, bsz, _, length, _ = zq.shape
    assert zq.shape[2] == dil and length % DIL_TQ == 0 and length >= DIL_TK
    gh = DIL_HEADS_PER_GROUP
    qspec = pl.BlockSpec((1, 1, 1, DIL_TQ, HEAD_DIM), lambda b, h, r, n: (base + h, b, r, n, 0))
    kvspec = lambda off: pl.BlockSpec((1, 1, 1, length, HEAD_DIM),
                                      lambda b, h, r, n: (base + off + h, b, r, 0, 0))
    ospec = pl.BlockSpec((1, 1, 1, DIL_TQ, HEAD_DIM), lambda b, h, r, n: (h, b, r, n, 0))
    return pl.pallas_call(
        functools.partial(_dil_kernel, dil=dil, length=length),
        grid=(bsz, gh, dil, length // DIL_TQ),
        in_specs=[pl.BlockSpec(memory_space=pltpu.SMEM), qspec, kvspec(gh), kvspec(2 * gh)],
        out_specs=[ospec, ospec],
        out_shape=[jax.ShapeDtypeStruct((gh, bsz, dil, length, HEAD_DIM), BF16),
                   jax.ShapeDtypeStruct((gh, bsz, dil, length, HEAD_DIM), F32)],
        compiler_params=_params(("parallel", "parallel", "parallel", "arbitrary")),
        name=f"dil_attn_{dil}",
    )(jnp.asarray(slopes[g * gh:(g + 1) * gh]), zq, zq, zq)


def _mix_kernel(na_ref, o1_ref, o2_ref, o3_ref, l1_ref, l2_ref, l3_ref, hb_ref,
                wpa_ref, wpb_ref, wga_ref, wgb_ref, bga_ref, bgb_ref, m_ref, na_s, dil_s, o_s, l_s):
    @pl.when(pl.program_id(1) == 0)
    def _():
        for h in range(NA_HEADS):
            na_s[:, h * HEAD_DIM:(h + 1) * HEAD_DIM] = na_ref[h]
        for g, (o_ref, l_ref) in enumerate(((o1_ref, l1_ref), (o2_ref, l2_ref), (o3_ref, l3_ref))):
            dil = DIL_GROUPS[g][1]
            for h in range(DIL_HEADS_PER_GROUP):
                for r in range(dil):
                    rows = slice(None) if dil == 1 else pl.ds(r, MIX_TM // dil, stride=dil)
                    o_s[g, h, rows, :] = o_ref[h, 0, r].astype(F32)
                    l_s[g, h, rows, :] = l_ref[h, 0, r]
        for h in range(DIL_HEADS_PER_GROUP):
            ls = [l_s[g, h] for g in range(len(DIL_GROUPS))]
            mx = jnp.maximum(jnp.maximum(ls[0], ls[1]), ls[2])
            es = [jnp.exp(l - mx) for l in ls]
            den = es[0] + es[1] + es[2]
            acc = (es[0] * o_s[0, h] + es[1] * o_s[1, h] + es[2] * o_s[2, h]) / den
            dil_s[:, h * HEAD_DIM:(h + 1) * HEAD_DIM] = acc.astype(BF16)

    hb = hb_ref[...]
    ya = jnp.dot(na_s[...], wpa_ref[...], preferred_element_type=F32)
    yb = jnp.dot(dil_s[...], wpb_ref[...], preferred_element_type=F32)
    ga = jnp.dot(hb, wga_ref[...], preferred_element_type=F32) + bga_ref[...]
    gb = jnp.dot(hb, wgb_ref[...], preferred_element_type=F32) + bgb_ref[...]
    m_ref[...] = (jax.nn.sigmoid(ga) * ya + jax.nn.sigmoid(gb) * yb).astype(BF16)


def _mix(na, dil_o, dil_lse, hb, wpa, wpb, wga, wgb, bga, bgb, s):
    n, d = hb.shape
    tm, tn = MIX_TM, MIX_TN
    tiles = s // tm
    gh = DIL_HEADS_PER_GROUP
    ngroups = len(DIL_GROUPS)
    assert all(tm % (16 * dil) == 0 for _, dil in DIL_GROUPS) and s % tm == 0
    col = lambda k: pl.BlockSpec((k, tn), lambda i, j: (0, j))
    grp = lambda dil: pl.BlockSpec((gh, 1, dil, tm // dil, HEAD_DIM),
                                   lambda i, j: (0, i // tiles, 0, i % tiles, 0))
    groups = [grp(dil) for _, dil in DIL_GROUPS]
    return pl.pallas_call(
        _mix_kernel,
        grid=(n // tm, d // tn),
        in_specs=[pl.BlockSpec((NA_HEADS, tm, HEAD_DIM), lambda i, j: (0, i, 0)), *groups, *groups,
                  pl.BlockSpec((tm, d), lambda i, j: (i, 0)),
                  col(NA_WIDTH), col(DIL_OUT_WIDTH), col(d), col(d), col(1), col(1)],
        out_specs=pl.BlockSpec((tm, tn), lambda i, j: (i, j)),
        out_shape=jax.ShapeDtypeStruct((n, d), BF16),
        scratch_shapes=[pltpu.VMEM((tm, NA_WIDTH), BF16), pltpu.VMEM((tm, DIL_OUT_WIDTH), BF16),
                        pltpu.VMEM((ngroups, gh, tm, HEAD_DIM), F32),
                        pltpu.VMEM((ngroups, gh, tm, HEAD_DIM), F32)],
        compiler_params=_params(("parallel", "arbitrary")),
        name="mix",
    )(na, *dil_o, *dil_lse, hb, wpa, wpb, wga, wgb, bga.reshape(1, d), bgb.reshape(1, d))


def _outproj_kernel(m_ref, h_ref, wo_ref, bo_ref, g_ref, b_ref, wrh_ref, wrl_ref, br_ref,
                    h1_hbm, lg_ref, hbuf, sem):
    i = pl.program_id(0)
    slot = i % 2

    def out_copies(step, sl):
        return _tile_copies(h1_hbm, step * OUT_TM, OUT_TM, hbuf.at[sl], sem.at[sl], True)

    mix = jnp.dot(m_ref[...], wo_ref[...], preferred_element_type=F32) + bo_ref[...]
    h1 = _layer_norm(DN_ALPHA * h_ref[...] + mix, g_ref[...], b_ref[...])

    @pl.when(i >= 2)
    def _():
        for cp in out_copies(i - 2, slot):
            cp.wait()

    _to_chunks(hbuf.at[slot], h1)
    for cp in out_copies(i, slot):
        cp.start()

    @pl.when(i == pl.num_programs(0) - 1)
    def _():
        for cp in out_copies(i, slot):
            cp.wait()

        @pl.when(i >= 1)
        def _():
            for cp in out_copies(i - 1, 1 - slot):
                cp.wait()

    hi = h1.astype(BF16)
    lo = (h1 - hi.astype(F32)).astype(BF16)
    lg = (jnp.dot(hi, wrh_ref[...], preferred_element_type=F32)
          + jnp.dot(lo, wrh_ref[...], preferred_element_type=F32)
          + jnp.dot(hi, wrl_ref[...], preferred_element_type=F32))
    lg_ref[...] = lg + br_ref[...]


def _outproj(m, h, wo, bo, g, b, wr_hi, wr_lo, br):
    n, d = h.shape
    tm = OUT_TM
    row = lambda w: pl.BlockSpec((tm, w), lambda i: (i, 0))
    full = lambda r, c: pl.BlockSpec((r, c), lambda i: (0, 0))
    return pl.pallas_call(
        _outproj_kernel,
        grid=(n // tm,),
        in_specs=[row(d), row(d), full(d, d), full(1, d), full(1, d), full(1, d),
                  full(d, ROUTER_PAD), full(d, ROUTER_PAD), full(1, ROUTER_PAD)],
        out_specs=[pl.BlockSpec(memory_space=pl.ANY), row(ROUTER_PAD)],
        out_shape=[jax.ShapeDtypeStruct((n, CHUNKS, LANES), F32),
                   jax.ShapeDtypeStruct((n, ROUTER_PAD), F32)],
        scratch_shapes=[pltpu.VMEM((2, CHUNKS, tm, LANES), F32), pltpu.SemaphoreType.DMA((2,))],
        compiler_params=_params(("arbitrary",)),
        name="outproj",
    )(m, h, wo, bo.reshape(1, d), g.reshape(1, d), b.reshape(1, d), wr_hi, wr_lo, br)


def _first_argmax(vals, lane_f):
    top = jnp.max(vals, axis=-1, keepdims=True)
    idx = jnp.min(jnp.where(vals == top, lane_f, float(LANES)), axis=-1, keepdims=True)
    return top, idx


def _route_kernel(lg_ref, info_ref, gate_ref, cnt_ref, base_s):
    @pl.when(pl.program_id(0) == 0)
    def _():
        base_s[...] = jnp.zeros(base_s.shape, F32)

    lg = lg_ref[...]
    lane = lax.broadcasted_iota(jnp.int32, lg.shape, 1)
    lane_f = lane.astype(F32)
    g_mask = lane < N_GROUPS
    g_top, g_sel = _first_argmax(jnp.where(g_mask, lg, MASK_VALUE), lane_f)
    g_prob = 1.0 / jnp.sum(jnp.where(g_mask, jnp.exp(lg - g_top), 0.0), axis=-1, keepdims=True)
    first = N_GROUPS + g_sel * EXPERTS_PER_GROUP
    e_mask = (lane_f >= first) & (lane_f < first + EXPERTS_PER_GROUP)
    el = jnp.where(e_mask, lg, MASK_VALUE)
    v0, i0 = _first_argmax(el, lane_f)
    v1, i1 = _first_argmax(jnp.where(lane_f == i0, MASK_VALUE, el), lane_f)
    e1 = jnp.exp(v1 - v0)
    w0 = g_prob / (1.0 + e1)
    w1 = g_prob * e1 / (1.0 + e1)

    tm = lg.shape[0]
    tri = (lax.broadcasted_iota(jnp.int32, (tm, tm), 1)
           < lax.broadcasted_iota(jnp.int32, (tm, tm), 0)).astype(BF16)
    ranks = []
    for idx in (i0, i1):
        onehot = lane_f == idx - N_GROUPS
        before = jnp.dot(tri, onehot.astype(BF16), preferred_element_type=F32) + base_s[...]
        ranks.append(jnp.sum(jnp.where(onehot, before, 0.0), axis=-1, keepdims=True))
        base_s[...] = base_s[...] + jnp.sum(onehot.astype(F32), axis=0, keepdims=True)

    info = jnp.where(lane == 0, i0 - N_GROUPS,
                     jnp.where(lane == 1, i1 - N_GROUPS,
                               jnp.where(lane == 2, ranks[0], jnp.where(lane == 3, ranks[1], 0.0))))
    info_ref[...] = info.astype(jnp.int32)
    gate_ref[...] = jnp.where(lane == 0, w0, jnp.where(lane == 1, w1, 0.0))
    cnt_ref[...] = base_s[...]


def _route(logits, n):
    tm = ROUTE_TM
    row = pl.BlockSpec((tm, ROUTER_PAD), lambda i: (i, 0))
    one = pl.BlockSpec((1, ROUTER_PAD), lambda i: (0, 0))
    info, gate, counts = pl.pallas_call(
        _route_kernel,
        grid=(n // tm,),
        in_specs=[row],
        out_specs=[row, row, one],
        out_shape=[jax.ShapeDtypeStruct((n, ROUTER_PAD), jnp.int32),
                   jax.ShapeDtypeStruct((n, ROUTER_PAD), F32),
                   jax.ShapeDtypeStruct((1, ROUTER_PAD), F32)],
        scratch_shapes=[pltpu.VMEM((1, ROUTER_PAD), F32)],
        compiler_params=_params(("arbitrary",)),
        name="route",
    )(logits)
    gate = gate[:, :TOP_K]
    a = n * TOP_K
    e_flat = info[:, :TOP_K].reshape(a)
    rank = info[:, TOP_K:2 * TOP_K].reshape(a)
    counts = counts[0, :N_EXPERTS].astype(jnp.int32)
    pcounts = (counts + MOE_TB - 1) // MOE_TB * MOE_TB
    pends = jnp.cumsum(pcounts)
    pstarts = pends - pcounts
    dest = (pstarts[e_flat] + rank).astype(jnp.int32)
    nb = a // MOE_TB + N_EXPERTS
    slots = nb * MOE_TB
    tok = jnp.arange(a, dtype=jnp.int32) // TOP_K
    slot_tok = jnp.zeros((slots,), jnp.int32).at[dest].set(tok)
    block_start = jnp.arange(nb, dtype=jnp.int32) * MOE_TB
    block_e = jnp.minimum(jnp.sum((pends[None, :] <= block_start[:, None]).astype(jnp.int32), axis=1),
                          N_EXPERTS - 1)
    n_active = (pends[-1] // MOE_TB).astype(jnp.int32).reshape(1)
    return dest, gate, slot_tok, block_e, n_active


def _expert_kernel(be_ref, nact_ref, tok_ref, h1_hbm, wg_ref, wu_ref, wd_ref, ys_hbm,
                   xbuf, ybuf, wg_s, wu_s, wd_s, gsem, osem):
    i = pl.program_id(0)
    slot = i % 2
    nact = nact_ref[0]

    def gather(step, sl):
        _start_row_gather(tok_ref, step * MOE_TB, 1, MOE_TB, h1_hbm, xbuf.at[sl], gsem.at[sl])

    def out_copies(step, sl):
        return _tile_copies(ys_hbm, step * MOE_TB, MOE_TB, ybuf.at[sl], osem.at[sl], True)

    @pl.when((i == 0) & (nact > 0))
    def _():
        gather(0, 0)

    @pl.when(i + 1 < nact)
    def _():
        gather(i + 1, 1 - slot)

    @pl.when(i >= 2)
    def _():
        for cp in out_copies(i - 2, slot):
            cp.wait()

    @pl.when(i < nact)
    def _():
        prev = be_ref[jnp.maximum(i - 1, 0)]

        @pl.when((i == 0) | (be_ref[i] != prev))
        def _():
            wg_s[...] = wg_ref[0].astype(BF16)
            wu_s[...] = wu_ref[0].astype(BF16)
            wd_s[...] = wd_ref[0].astype(BF16)

        _wait_row_gather(xbuf.at[slot], gsem.at[slot])
        xb = _from_chunks(xbuf.at[slot]).astype(BF16)
        gate = jnp.dot(xb, wg_s[...], preferred_element_type=F32)
        up = jnp.dot(xb, wu_s[...], preferred_element_type=F32)
        hid = (jax.nn.silu(gate) * up).astype(BF16)
        _to_chunks(ybuf.at[slot], jnp.dot(hid, wd_s[...], preferred_element_type=F32))

    @pl.when(i >= nact)
    def _():
        ybuf[slot] = jnp.zeros(ybuf.shape[1:], F32)

    for cp in out_copies(i, slot):
        cp.start()

    @pl.when(i == pl.num_programs(0) - 1)
    def _():
        for cp in out_copies(i, slot):
            cp.wait()

        @pl.when(i >= 1)
        def _():
            for cp in out_copies(i - 1, 1 - slot):
                cp.wait()


def _experts(h1, slot_tok, block_e, n_active, w_gate, w_up, w_down):
    d = D_MODEL
    slots = slot_tok.shape[0]
    nb = slots // MOE_TB
    grid_spec = pltpu.PrefetchScalarGridSpec(
        num_scalar_prefetch=3,
        grid=(nb,),
        in_specs=[pl.BlockSpec(memory_space=pl.ANY),
                  pl.BlockSpec((1, d, D_EXPERT), lambda i, be, na, tk: (be[i], 0, 0)),
                  pl.BlockSpec((1, d, D_EXPERT), lambda i, be, na, tk: (be[i], 0, 0)),
                  pl.BlockSpec((1, D_EXPERT, d), lambda i, be, na, tk: (be[i], 0, 0))],
        out_specs=pl.BlockSpec(memory_space=pl.ANY),
        scratch_shapes=[pltpu.VMEM((2, CHUNKS, MOE_TB, LANES), F32),
                        pltpu.VMEM((2, CHUNKS, MOE_TB, LANES), F32),
                        pltpu.VMEM((d, D_EXPERT), BF16), pltpu.VMEM((d, D_EXPERT), BF16),
                        pltpu.VMEM((D_EXPERT, d), BF16),
                        pltpu.SemaphoreType.DMA((2,)), pltpu.SemaphoreType.DMA((2,))])
    return pl.pallas_call(
        _expert_kernel,
        grid_spec=grid_spec,
        out_shape=jax.ShapeDtypeStruct((slots, CHUNKS, LANES), F32),
        compiler_params=_params(("arbitrary",)),
        name="experts",
    )(block_e, n_active, slot_tok, h1, w_gate, w_up, w_down)


def _final_kernel(dest_ref, ys_hbm, h1_hbm, gate_ref, g_ref, b_ref, o_ref, ybuf, hbuf, gsem, hsem):
    i = pl.program_id(0)
    slot = i % 2

    def h1_copies(step, sl):
        return _tile_copies(h1_hbm, step * FIN_TM, FIN_TM, hbuf.at[sl], hsem.at[sl], False)

    def fetch(step, sl):
        for k in range(TOP_K):
            _start_row_gather(dest_ref, step * FIN_TM * TOP_K + k, TOP_K, FIN_TM, ys_hbm,
                              ybuf.at[sl, k], gsem.at[sl])
        for cp in h1_copies(step, sl):
            cp.start()

    @pl.when(i == 0)
    def _():
        fetch(0, 0)

    @pl.when(i + 1 < pl.num_programs(0))
    def _():
        fetch(i + 1, 1 - slot)

    _wait_row_gather(ybuf.at[slot], gsem.at[slot])
    for cp in h1_copies(i, slot):
        cp.wait()
    gate = gate_ref[...]
    ffn = (_from_chunks(ybuf.at[slot, 0]) * gate[:, 0:1]
           + _from_chunks(ybuf.at[slot, 1]) * gate[:, 1:2])
    o_ref[...] = _layer_norm(DN_ALPHA * _from_chunks(hbuf.at[slot]) + ffn, g_ref[...], b_ref[...])


def _final(dest, ys, h1, gate, g, b):
    n, d = h1.shape[0], D_MODEL
    grid_spec = pltpu.PrefetchScalarGridSpec(
        num_scalar_prefetch=1,
        grid=(n // FIN_TM,),
        in_specs=[pl.BlockSpec(memory_space=pl.ANY),
                  pl.BlockSpec(memory_space=pl.ANY),
                  pl.BlockSpec((FIN_TM, TOP_K), lambda i, ds: (i, 0)),
                  pl.BlockSpec((1, d), lambda i, ds: (0, 0)),
                  pl.BlockSpec((1, d), lambda i, ds: (0, 0))],
        out_specs=pl.BlockSpec((FIN_TM, d), lambda i, ds: (i, 0)),
        scratch_shapes=[pltpu.VMEM((2, TOP_K, CHUNKS, FIN_TM, LANES), F32),
                        pltpu.VMEM((2, CHUNKS, FIN_TM, LANES), F32),
                        pltpu.SemaphoreType.DMA((2,)), pltpu.SemaphoreType.DMA((2,))])
    return pl.pallas_call(
        _final_kernel,
        grid_spec=grid_spec,
        out_shape=jax.ShapeDtypeStruct((n, d), F32),
        compiler_params=_params(("arbitrary",)),
        name="final",
    )(dest, ys, h1, gate, g.reshape(1, d), b.reshape(1, d))


def kernel(x, ln0_g, ln0_b, w_in, b_in, rpb, w_proj_a, w_proj_b, w_o, b_o, ln1_g, ln1_b,
           w_router_group, b_router_group, w_router_expert, b_router_expert,
           w_gate, w_up, w_down, ln2_g, ln2_b):
    bsz, s, d = x.shape
    n = bsz * s
    assert d == D_MODEL and w_in.shape[0] == DEPTH
    scale = HEAD_DIM ** -0.5

    col_scale = np.ones((QKV_WIDTH,), np.float32)
    col_scale[:NA_WIDTH] = scale
    col_scale[3 * NA_WIDTH:3 * NA_WIDTH + DIL_WIDTH] = scale
    w_qkv = w_in[0][:, :QKV_WIDTH] * col_scale
    b_qkv = b_in[0][:QKV_WIDTH] * col_scale

    def group_cols(t, g):
        gw = DIL_OUT_WIDTH
        return [t[..., 3 * NA_WIDTH + p * DIL_WIDTH + g * gw:3 * NA_WIDTH + p * DIL_WIDTH + (g + 1) * gw]
                for p in range(3)]

    w_proj = [jnp.concatenate([w_qkv[:, :3 * NA_WIDTH]] + group_cols(w_qkv, 0), axis=1).astype(BF16)]
    b_proj = [jnp.concatenate([b_qkv[:3 * NA_WIDTH]] + group_cols(b_qkv, 0))]
    for g in range(1, len(DIL_GROUPS)):
        w_proj.append(jnp.concatenate(group_cols(w_qkv, g), axis=1).astype(BF16))
        b_proj.append(jnp.concatenate(group_cols(b_qkv, g)))
    w_ga = w_in[0][:, QKV_WIDTH:QKV_WIDTH + d].astype(BF16)
    w_gb = w_in[0][:, QKV_WIDTH + d:].astype(BF16)
    b_ga = b_in[0][QKV_WIDTH:QKV_WIDTH + d]
    b_gb = b_in[0][QKV_WIDTH + d:]
    w_r = jnp.concatenate([w_router_group[0], w_router_expert[0]], axis=1)
    w_r = jnp.pad(w_r, ((0, 0), (0, ROUTER_PAD - w_r.shape[1])))
    w_r_hi = w_r.astype(BF16)
    w_r_lo = (w_r - w_r_hi.astype(F32)).astype(BF16)
    b_r = jnp.pad(jnp.concatenate([b_router_group[0], b_router_expert[0]]),
                  (0, ROUTER_PAD - N_GROUPS - N_EXPERTS)).reshape(1, ROUTER_PAD)
    tabs = _na_bias_tables(rpb[0], s // GRID_W)
    slopes = _alibi_slopes(DIL_HEADS)

    h, hb = _ln0(x.reshape(n, d), ln0_g, ln0_b)
    zq = [_inproj(hb, w_proj[g], b_proj[g], bsz, s, DIL_GROUPS[g][1]) for g in range(len(DIL_GROUPS))]
    na = _neighborhood_attention(zq[0].reshape(-1, bsz, s, HEAD_DIM), tabs)
    na = na.reshape(NA_HEADS, n, HEAD_DIM)
    dil = [_dilated_group(zq[g], DIL0_H if g == 0 else 0, g, slopes) for g in range(len(DIL_GROUPS))]
    m = _mix(na, [o for o, _ in dil], [l for _, l in dil], hb,
             w_proj_a[0].astype(BF16), w_proj_b[0].astype(BF16), w_ga, w_gb, b_ga, b_gb, s)
    h1, logits = _outproj(m, h, w_o[0].astype(BF16), b_o[0], ln1_g[0], ln1_b[0], w_r_hi, w_r_lo, b_r)
    dest, gate, slot_tok, block_e, n_active = _route(logits, n)
    ys = _experts(h1, slot_tok, block_e, n_active, w_gate[0], w_up[0], w_down[0])
    out = _final(dest, ys, h1, gate, ln2_g[0], ln2_b[0])
    return out.reshape(bsz, s, d)
```

```python
import functools

import numpy as np
import jax
import jax.numpy as jnp
from jax import lax
from jax.experimental import pallas as pl
from jax.experimental.pallas import tpu as pltpu

F32 = jnp.float32
BF16 = jnp.bfloat16

D_MODEL = 2048
HEAD_DIM = 128
GRID_W = 64
NA_HEADS = 8
NA_KH = 8
NA_KW = 16
DIL_GROUPS = ((128, 1), (512, 4), (2048, 16))
DIL_HEADS_PER_GROUP = 4
DIL_HEADS = DIL_HEADS_PER_GROUP * len(DIL_GROUPS)
N_GROUPS = 8
EXPERTS_PER_GROUP = 8
N_EXPERTS = N_GROUPS * EXPERTS_PER_GROUP
TOP_K = 2
D_EXPERT = D_MODEL // 4
LN_EPS = 1e-5
DEPTH = 1
DN_ALPHA = (2 * DEPTH) ** 0.25
NA_WIDTH = NA_HEADS * HEAD_DIM
DIL_WIDTH = DIL_HEADS * HEAD_DIM
DIL_OUT_WIDTH = DIL_HEADS_PER_GROUP * HEAD_DIM
QKV_WIDTH = 3 * NA_WIDTH + 3 * DIL_WIDTH
QKV_HEADS = QKV_WIDTH // HEAD_DIM
LANES = 128
CHUNKS = D_MODEL // LANES
MASK_VALUE = -1e30

QA_H, KA_H, VA_H = 0, NA_HEADS, 2 * NA_HEADS
DIL0_H = 3 * NA_HEADS

LN_TM = 512
INPROJ_TM, INPROJ_TN = 1024, 512
NA_QROWS = 4
NA_KROWS = 12
NA_NSUB = 4
DIL_NSUB = 4
DIL_TQ = 256
DIL_HALF = 64
DIL_TK = DIL_TQ + 2 * DIL_HALF
MIX_TM, MIX_TN = 512, 512
OUT_TM = 256
ROUTER_PAD = 128
ROUTE_TM = 512
MOE_TB = 128
FIN_TM = 128
VMEM_LIMIT = 56 * 1024 * 1024


def _params(sem, limit=VMEM_LIMIT):
    return pltpu.CompilerParams(dimension_semantics=sem, vmem_limit_bytes=limit)


def _layer_norm(x, g, b):
    mu = jnp.mean(x, axis=-1, keepdims=True)
    xc = x - mu
    var = jnp.mean(xc * xc, axis=-1, keepdims=True)
    return xc * lax.rsqrt(var + LN_EPS) * g + b


def _to_chunks(ref, val):
    for c in range(CHUNKS):
        ref[c] = val[:, c * LANES:(c + 1) * LANES]


def _from_chunks(ref):
    return jnp.concatenate([ref[c] for c in range(CHUNKS)], axis=-1)


def _tile_copies(hbm, row0, rows, vmem, sem, to_hbm):
    copies = []
    for c in range(CHUNKS):
        h, v = hbm.at[pl.ds(row0, rows), c], vmem.at[c]
        copies.append(pltpu.make_async_copy(v, h, sem) if to_hbm else pltpu.make_async_copy(h, v, sem))
    return copies


def _start_row_gather(idx_ref, base, step, count, src_hbm, dst, sem):
    def start(t, c):
        pltpu.make_async_copy(src_hbm.at[idx_ref[base + t * step]], dst.at[:, t], sem).start()
        return c

    lax.fori_loop(0, count, start, 0, unroll=4)


def _wait_row_gather(dst, sem):
    pltpu.make_async_copy(dst, dst, sem).wait()


def _ln0_kernel(x_ref, g_ref, b_ref, h_ref, hb_ref):
    y = _layer_norm(x_ref[...], g_ref[...], b_ref[...])
    h_ref[...] = y
    hb_ref[...] = y.astype(BF16)


def _ln0(x, g, b):
    n, d = x.shape
    row = pl.BlockSpec((LN_TM, d), lambda i: (i, 0))
    vec = pl.BlockSpec((1, d), lambda i: (0, 0))
    return pl.pallas_call(
        _ln0_kernel,
        grid=(n // LN_TM,),
        in_specs=[row, vec, vec],
        out_specs=[row, row],
        out_shape=[jax.ShapeDtypeStruct((n, d), F32), jax.ShapeDtypeStruct((n, d), BF16)],
        compiler_params=_params(("parallel",)),
        name="ln0",
    )(x, g.reshape(1, d), b.reshape(1, d))


def _inproj_kernel(hb_ref, w_ref, b_ref, o_ref, acc_s, *, dil):
    acc = jnp.dot(hb_ref[...], w_ref[...], preferred_element_type=F32) + b_ref[...]
    if dil == 1:
        for c in range(INPROJ_TN // LANES):
            o_ref[c, 0, 0] = acc[:, c * LANES:(c + 1) * LANES].astype(BF16)
    else:
        for c in range(INPROJ_TN // LANES):
            acc_s[c] = acc[:, c * LANES:(c + 1) * LANES]
        for c in range(INPROJ_TN // LANES):
            for r in range(dil):
                o_ref[c, 0, r] = acc_s[c, pl.ds(r, INPROJ_TM // dil, stride=dil), :].astype(BF16)


def _inproj(hb, w, b, bsz, s, dil, width, col_block):
    n, d = hb.shape
    tiles = s // INPROJ_TM
    assert INPROJ_TM % (16 * dil) == 0 and s % INPROJ_TM == 0 and width % INPROJ_TN == 0
    return pl.pallas_call(
        functools.partial(_inproj_kernel, dil=dil),
        grid=(n // INPROJ_TM, width // INPROJ_TN),
        in_specs=[pl.BlockSpec((INPROJ_TM, d), lambda i, j: (i, 0)),
                  pl.BlockSpec((d, INPROJ_TN), lambda i, j: (0, col_block(j))),
                  pl.BlockSpec((1, INPROJ_TN), lambda i, j: (0, col_block(j)))],
        out_specs=pl.BlockSpec((INPROJ_TN // LANES, 1, dil, INPROJ_TM // dil, LANES),
                               lambda i, j: (j, i // tiles, 0, i % tiles, 0)),
        out_shape=jax.ShapeDtypeStruct((width // LANES, bsz, dil, s // dil, LANES), BF16),
        scratch_shapes=[pltpu.VMEM((INPROJ_TN // LANES, INPROJ_TM, LANES), F32)],
        compiler_params=_params(("parallel", "arbitrary")),
        name=f"inproj_{dil}",
    )(hb, w, b)


def _na_bias_tables(rpb, rows):
    heads, n_dr, n_dc = rpb.shape
    qc = np.arange(GRID_W)[:, None]
    kc = np.arange(GRID_W)[None, :]
    qcs = np.clip(qc - NA_KW // 2, 0, GRID_W - NA_KW)
    v_col = (kc >= qcs) & (kc < qcs + NA_KW)
    dc = np.clip(kc - qc + NA_KW - 1, 0, n_dc - 1)
    onehot = (dc[None] == np.arange(n_dc)[:, None, None]) & v_col[None]
    toep = jnp.einsum('hrd,dqk->hrqk', rpb.astype(F32), jnp.asarray(onehot, F32),
                      precision=lax.Precision.HIGHEST)
    toep = jnp.where(jnp.asarray(v_col), toep, MASK_VALUE)
    masked = jnp.full((heads, 1, GRID_W, GRID_W), MASK_VALUE, F32)
    blocks = jnp.concatenate([toep, masked], axis=1)
    i = np.arange(NA_QROWS)[:, None]
    j = np.arange(NA_KROWS)[None, :]
    sel = []
    for r0, ks in ((0, 0), (2 * NA_QROWS, 2 * NA_QROWS - NA_KH // 2), (rows - NA_QROWS, rows - NA_KROWS)):
        r, krow = r0 + i, ks + j
        start = np.clip(r - NA_KH // 2, 0, rows - NA_KH)
        v_row = (krow >= start) & (krow < start + NA_KH)
        sel.append(np.where(v_row, krow - r + NA_KH - 1, n_dr))
    sel = np.stack(sel).reshape(-1)
    tab = jnp.take(blocks, jnp.asarray(sel, jnp.int32), axis=1)
    tab = tab.reshape(heads, 3, NA_QROWS, NA_KROWS, GRID_W, GRID_W).transpose(1, 0, 2, 4, 3, 5)
    return tab.reshape(3, heads, NA_QROWS * GRID_W, NA_KROWS * GRID_W)


def _softmax_attend(q, kw, vw, bias):
    s = lax.dot_general(q, kw, (((1,), (1,)), ((), ())), preferred_element_type=F32) + bias
    m = jnp.max(s, axis=-1, keepdims=True)
    p = jnp.exp(s - m)
    l = jnp.sum(p, axis=-1, keepdims=True)
    o = jnp.dot(p.astype(BF16), vw, preferred_element_type=F32) / l
    return o, m, l


def _edge_variant(blk, nblk):
    return jnp.where(blk == 0, 0, jnp.where(blk == nblk - 1, 2, 1))


def _na_kernel(q_ref, k_ref, v_ref, tab_ref, o_ref, *, rows):
    tq = NA_QROWS * GRID_W
    for u in range(NA_NSUB):
        blk = pl.program_id(2) * NA_NSUB + u
        ks = jnp.clip(blk * NA_QROWS - NA_KH // 2, 0, rows - NA_KROWS)
        off = pl.multiple_of(ks * GRID_W, GRID_W)
        kw = k_ref[0, 0, pl.ds(off, NA_KROWS * GRID_W), :]
        vw = v_ref[0, 0, pl.ds(off, NA_KROWS * GRID_W), :]
        bias = tab_ref[_edge_variant(blk, rows // NA_QROWS), 0]
        o, _, _ = _softmax_attend(q_ref[0, 0, u * tq:(u + 1) * tq, :], kw, vw, bias)
        o_ref[0, 0, u * tq:(u + 1) * tq, :] = o.astype(BF16)


def _neighborhood_attention(zq4, tabs):
    _, bsz, s, _ = zq4.shape
    rows = s // GRID_W
    assert rows % (NA_QROWS * NA_NSUB) == 0 and rows >= NA_KROWS + NA_QROWS
    tq = NA_QROWS * GRID_W * NA_NSUB
    return pl.pallas_call(
        functools.partial(_na_kernel, rows=rows),
        grid=(bsz, NA_HEADS, s // tq),
        in_specs=[pl.BlockSpec((1, 1, tq, HEAD_DIM), lambda b, h, i: (QA_H + h, b, i, 0)),
                  pl.BlockSpec((1, 1, s, HEAD_DIM), lambda b, h, i: (KA_H + h, b, 0, 0)),
                  pl.BlockSpec((1, 1, s, HEAD_DIM), lambda b, h, i: (VA_H + h, b, 0, 0)),
                  pl.BlockSpec((3, 1) + tabs.shape[2:], lambda b, h, i: (0, h, 0, 0))],
        out_specs=pl.BlockSpec((1, 1, tq, HEAD_DIM), lambda b, h, i: (h, b, i, 0)),
        out_shape=jax.ShapeDtypeStruct((NA_HEADS, bsz, s, HEAD_DIM), BF16),
        compiler_params=_params(("parallel", "parallel", "arbitrary")),
        name="na_attn",
    )(zq4, zq4, zq4, tabs)


def _alibi_slopes(n):
    return np.array([2.0 ** (-8.0 * (i + 1) / n) for i in range(n)], dtype=np.float32)


def _dil_bias_tables(slopes, dil):
    qi = np.arange(DIL_TQ)[:, None]
    kj = np.arange(DIL_TK)[None, :]
    dist = np.stack([np.abs(kj - qi + shift) for shift in (0, -DIL_HALF, -2 * DIL_HALF)])
    dist = jnp.asarray(dist, F32)[:, None]
    penalty = jnp.asarray(slopes, F32)[None, :, None, None] * (dist * float(dil))
    return jnp.where(dist <= DIL_HALF, -penalty, MASK_VALUE)


def _dil_kernel(q_ref, k_ref, v_ref, tab_ref, o_ref, lse_ref, *, length, nsub):
    for u in range(nsub):
        blk = pl.program_id(3) * nsub + u
        ks = jnp.clip(blk * DIL_TQ - DIL_HALF, 0, length - DIL_TK)
        off = pl.multiple_of(ks, DIL_HALF)
        kw = k_ref[0, 0, 0, pl.ds(off, DIL_TK), :]
        vw = v_ref[0, 0, 0, pl.ds(off, DIL_TK), :]
        bias = tab_ref[_edge_variant(blk, length // DIL_TQ), 0]
        rows = slice(u * DIL_TQ, (u + 1) * DIL_TQ)
        o, m, l = _softmax_attend(q_ref[0, 0, 0, rows, :], kw, vw, bias)
        o_ref[0, 0, 0, rows, :] = o.astype(BF16)
        lse_ref[0, 0, 0, rows, :] = jnp.broadcast_to(m + jnp.log(l), (DIL_TQ, LANES))


def _dilated_group(zq, base, g, slopes):
    window, dil = DIL_GROUPS[g]
    assert (window // 2) // dil == DIL_HALF
    _, bsz, _, length, _ = zq.shape
    nsub = min(DIL_NSUB, length // DIL_TQ)
    assert zq.shape[2] == dil and length % (DIL_TQ * nsub) == 0 and length >= DIL_TK
    gh = DIL_HEADS_PER_GROUP
    tabs = _dil_bias_tables(slopes[g * gh:(g + 1) * gh], dil)
    tq = DIL_TQ * nsub
    qspec = pl.BlockSpec((1, 1, 1, tq, HEAD_DIM), lambda b, h, r, n: (base + h, b, r, n, 0))
    kvspec = lambda off: pl.BlockSpec((1, 1, 1, length, HEAD_DIM),
                                      lambda b, h, r, n: (base + off + h, b, r, 0, 0))
    ospec = pl.BlockSpec((1, 1, 1, tq, HEAD_DIM), lambda b, h, r, n: (h, b, r, n, 0))
    return pl.pallas_call(
        functools.partial(_dil_kernel, length=length, nsub=nsub),
        grid=(bsz, gh, dil, length // tq),
        in_specs=[qspec, kvspec(gh), kvspec(2 * gh),
                  pl.BlockSpec((3, 1, DIL_TQ, DIL_TK), lambda b, h, r, n: (0, h, 0, 0))],
        out_specs=[ospec, ospec],
        out_shape=[jax.ShapeDtypeStruct((gh, bsz, dil, length, HEAD_DIM), BF16),
                   jax.ShapeDtypeStruct((gh, bsz, dil, length, HEAD_DIM), F32)],
        compiler_params=_params(("parallel", "parallel", "parallel", "arbitrary")),
        name=f"dil_attn_{dil}",
    )(zq, zq, zq, tabs)


def _mix_kernel(na_ref, o1_ref, o2_ref, o3_ref, l1_ref, l2_ref, l3_ref, hb_ref,
                wpa_ref, wpb_ref, wga_ref, wgb_ref, bga_ref, bgb_ref, m_ref, na_s, dil_s, o_s, l_s):
    @pl.when(pl.program_id(1) == 0)
    def _():
        for h in range(NA_HEADS):
            na_s[:, h * HEAD_DIM:(h + 1) * HEAD_DIM] = na_ref[h]
        for g, (o_ref, l_ref) in enumerate(((o1_ref, l1_ref), (o2_ref, l2_ref), (o3_ref, l3_ref))):
            dil = DIL_GROUPS[g][1]
            for h in range(DIL_HEADS_PER_GROUP):
                for r in range(dil):
                    rows = slice(None) if dil == 1 else pl.ds(r, MIX_TM // dil, stride=dil)
                    o_s[g, h, rows, :] = o_ref[h, 0, r].astype(F32)
                    l_s[g, h, rows, :] = l_ref[h, 0, r]
        for h in range(DIL_HEADS_PER_GROUP):
            ls = [l_s[g, h] for g in range(len(DIL_GROUPS))]
            mx = jnp.maximum(jnp.maximum(ls[0], ls[1]), ls[2])
            es = [jnp.exp(l - mx) for l in ls]
            den = es[0] + es[1] + es[2]
            acc = (es[0] * o_s[0, h] + es[1] * o_s[1, h] + es[2] * o_s[2, h]) / den
            dil_s[:, h * HEAD_DIM:(h + 1) * HEAD_DIM] = acc.astype(BF16)

    hb = hb_ref[...]
    ya = jnp.dot(na_s[...], wpa_ref[...], preferred_element_type=F32)
    yb = jnp.dot(dil_s[...], wpb_ref[...], preferred_element_type=F32)
    ga = jnp.dot(hb, wga_ref[...], preferred_element_type=F32) + bga_ref[...]
    gb = jnp.dot(hb, wgb_ref[...], preferred_element_type=F32) + bgb_ref[...]
    m_ref[...] = (jax.nn.sigmoid(ga) * ya + jax.nn.sigmoid(gb) * yb).astype(BF16)


def _mix(na, dil_o, dil_lse, hb, wpa, wpb, w_all, b_all, s):
    n, d = hb.shape
    tm, tn = MIX_TM, MIX_TN
    tiles = s // tm
    gh = DIL_HEADS_PER_GROUP
    ngroups = len(DIL_GROUPS)
    assert all(tm % (16 * dil) == 0 for _, dil in DIL_GROUPS) and s % tm == 0
    col = lambda k: pl.BlockSpec((k, tn), lambda i, j: (0, j))
    ga0, gb0 = QKV_WIDTH // tn, (QKV_WIDTH + d) // tn
    gate = lambda k, first: pl.BlockSpec((k, tn), lambda i, j: (0, first + j))
    grp = lambda dil: pl.BlockSpec((gh, 1, dil, tm // dil, HEAD_DIM),
                                   lambda i, j: (0, i // tiles, 0, i % tiles, 0))
    groups = [grp(dil) for _, dil in DIL_GROUPS]
    return pl.pallas_call(
        _mix_kernel,
        grid=(n // tm, d // tn),
        in_specs=[pl.BlockSpec((NA_HEADS, tm, HEAD_DIM), lambda i, j: (0, i, 0)), *groups, *groups,
                  pl.BlockSpec((tm, d), lambda i, j: (i, 0)),
                  col(NA_WIDTH), col(DIL_OUT_WIDTH),
                  gate(d, ga0), gate(d, gb0), gate(1, ga0), gate(1, gb0)],
        out_specs=pl.BlockSpec((tm, tn), lambda i, j: (i, j)),
        out_shape=jax.ShapeDtypeStruct((n, d), BF16),
        scratch_shapes=[pltpu.VMEM((tm, NA_WIDTH), BF16), pltpu.VMEM((tm, DIL_OUT_WIDTH), BF16),
                        pltpu.VMEM((ngroups, gh, tm, HEAD_DIM), F32),
                        pltpu.VMEM((ngroups, gh, tm, HEAD_DIM), F32)],
        compiler_params=_params(("parallel", "arbitrary")),
        name="mix",
    )(na, *dil_o, *dil_lse, hb, wpa, wpb, w_all, w_all, b_all, b_all)


def _outproj_kernel(m_ref, h_ref, wo_ref, bo_ref, g_ref, b_ref, wrh_ref, wrl_ref, br_ref,
                    h1_hbm, lg_ref, hbuf, sem):
    i = pl.program_id(0)
    slot = i % 2

    def out_copies(step, sl):
        return _tile_copies(h1_hbm, step * OUT_TM, OUT_TM, hbuf.at[sl], sem.at[sl], True)

    mix = jnp.dot(m_ref[...], wo_ref[...], preferred_element_type=F32) + bo_ref[...]
    h1 = _layer_norm(DN_ALPHA * h_ref[...] + mix, g_ref[...], b_ref[...])

    @pl.when(i >= 2)
    def _():
        for cp in out_copies(i - 2, slot):
            cp.wait()

    _to_chunks(hbuf.at[slot], h1)
    for cp in out_copies(i, slot):
        cp.start()

    @pl.when(i == pl.num_programs(0) - 1)
    def _():
        for cp in out_copies(i, slot):
            cp.wait()

        @pl.when(i >= 1)
        def _():
            for cp in out_copies(i - 1, 1 - slot):
                cp.wait()

    hi = h1.astype(BF16)
    lo = (h1 - hi.astype(F32)).astype(BF16)
    lg = (jnp.dot(hi, wrh_ref[...], preferred_element_type=F32)
          + jnp.dot(lo, wrh_ref[...], preferred_element_type=F32)
          + jnp.dot(hi, wrl_ref[...], preferred_element_type=F32))
    lg_ref[...] = lg + br_ref[...]


def _outproj(m, h, wo, bo, g, b, wr_hi, wr_lo, br):
    n, d = h.shape
    tm = OUT_TM
    row = lambda w: pl.BlockSpec((tm, w), lambda i: (i, 0))
    full = lambda r, c: pl.BlockSpec((r, c), lambda i: (0, 0))
    return pl.pallas_call(
        _outproj_kernel,
        grid=(n // tm,),
        in_specs=[row(d), row(d), full(d, d), full(1, d), full(1, d), full(1, d),
                  full(d, ROUTER_PAD), full(d, ROUTER_PAD), full(1, ROUTER_PAD)],
        out_specs=[pl.BlockSpec(memory_space=pl.ANY), row(ROUTER_PAD)],
        out_shape=[jax.ShapeDtypeStruct((n, CHUNKS, LANES), F32),
                   jax.ShapeDtypeStruct((n, ROUTER_PAD), F32)],
        scratch_shapes=[pltpu.VMEM((2, CHUNKS, tm, LANES), F32), pltpu.SemaphoreType.DMA((2,))],
        compiler_params=_params(("arbitrary",)),
        name="outproj",
    )(m, h, wo, bo.reshape(1, d), g.reshape(1, d), b.reshape(1, d), wr_hi, wr_lo, br)


def _first_argmax(vals, lane_f):
    top = jnp.max(vals, axis=-1, keepdims=True)
    idx = jnp.min(jnp.where(vals == top, lane_f, float(LANES)), axis=-1, keepdims=True)
    return top, idx


def _route_kernel(lg_ref, info_ref, gate_ref, cnt_ref, base_s):
    @pl.when(pl.program_id(0) == 0)
    def _():
        base_s[...] = jnp.zeros(base_s.shape, F32)

    lg = lg_ref[...]
    lane = lax.broadcasted_iota(jnp.int32, lg.shape, 1)
    lane_f = lane.astype(F32)
    g_mask = lane < N_GROUPS
    g_top, g_sel = _first_argmax(jnp.where(g_mask, lg, MASK_VALUE), lane_f)
    g_prob = 1.0 / jnp.sum(jnp.where(g_mask, jnp.exp(lg - g_top), 0.0), axis=-1, keepdims=True)
    first = N_GROUPS + g_sel * EXPERTS_PER_GROUP
    e_mask = (lane_f >= first) & (lane_f < first + EXPERTS_PER_GROUP)
    el = jnp.where(e_mask, lg, MASK_VALUE)
    v0, i0 = _first_argmax(el, lane_f)
    v1, i1 = _first_argmax(jnp.where(lane_f == i0, MASK_VALUE, el), lane_f)
    e1 = jnp.exp(v1 - v0)
    w0 = g_prob / (1.0 + e1)
    w1 = g_prob * e1 / (1.0 + e1)

    tm = lg.shape[0]
    tri = (lax.broadcasted_iota(jnp.int32, (tm, tm), 1)
           < lax.broadcasted_iota(jnp.int32, (tm, tm), 0)).astype(BF16)
    ranks = []
    for idx in (i0, i1):
        onehot = lane_f == idx - N_GROUPS
        before = jnp.dot(tri, onehot.astype(BF16), preferred_element_type=F32) + base_s[...]
        ranks.append(jnp.sum(jnp.where(onehot, before, 0.0), axis=-1, keepdims=True))
        base_s[...] = base_s[...] + jnp.sum(onehot.astype(F32), axis=0, keepdims=True)

    info = jnp.where(lane == 0, i0 - N_GROUPS,
                     jnp.where(lane == 1, i1 - N_GROUPS,
                               jnp.where(lane == 2, ranks[0], jnp.where(lane == 3, ranks[1], 0.0))))
    info_ref[...] = info.astype(jnp.int32)
    gate_ref[...] = jnp.where(lane == 0, w0, jnp.where(lane == 1, w1, 0.0))
    cnt_ref[...] = base_s[...]


def _route(logits, n):
    tm = ROUTE_TM
    row = pl.BlockSpec((tm, ROUTER_PAD), lambda i: (i, 0))
    one = pl.BlockSpec((1, ROUTER_PAD), lambda i: (0, 0))
    info, gate, counts = pl.pallas_call(
        _route_kernel,
        grid=(n // tm,),
        in_specs=[row],
        out_specs=[row, row, one],
        out_shape=[jax.ShapeDtypeStruct((n, ROUTER_PAD), jnp.int32),
                   jax.ShapeDtypeStruct((n, ROUTER_PAD), F32),
                   jax.ShapeDtypeStruct((1, ROUTER_PAD), F32)],
        scratch_shapes=[pltpu.VMEM((1, ROUTER_PAD), F32)],
        compiler_params=_params(("arbitrary",)),
        name="route",
    )(logits)
    gate = gate[:, :TOP_K]
    a = n * TOP_K
    e_flat = info[:, :TOP_K].reshape(a)
    rank = info[:, TOP_K:2 * TOP_K].reshape(a)
    counts = counts[0, :N_EXPERTS].astype(jnp.int32)
    pcounts = (counts + MOE_TB - 1) // MOE_TB * MOE_TB
    pends = jnp.cumsum(pcounts)
    pstarts = pends - pcounts
    dest = (pstarts[e_flat] + rank).astype(jnp.int32)
    nb = a // MOE_TB + N_EXPERTS
    slots = nb * MOE_TB
    tok = jnp.arange(a, dtype=jnp.int32) // TOP_K
    slot_tok = jnp.zeros((slots,), jnp.int32).at[dest].set(tok)
    block_start = jnp.arange(nb, dtype=jnp.int32) * MOE_TB
    block_e = jnp.minimum(jnp.sum((pends[None, :] <= block_start[:, None]).astype(jnp.int32), axis=1),
                          N_EXPERTS - 1)
    n_active = (pends[-1] // MOE_TB).astype(jnp.int32).reshape(1)
    return dest, gate, slot_tok, block_e, n_active


def _expert_kernel(be_ref, nact_ref, tok_ref, h1_hbm, wg_ref, wu_ref, wd_ref, ys_hbm,
                   xbuf, ybuf, wg_s, wu_s, wd_s, gsem, osem):
    i = pl.program_id(0)
    slot = i % 2
    nact = nact_ref[0]

    def gather(step, sl):
        _start_row_gather(tok_ref, step * MOE_TB, 1, MOE_TB, h1_hbm, xbuf.at[sl], gsem.at[sl])

    def out_copies(step, sl):
        return _tile_copies(ys_hbm, step * MOE_TB, MOE_TB, ybuf.at[sl], osem.at[sl], True)

    @pl.when((i == 0) & (nact > 0))
    def _():
        gather(0, 0)

    @pl.when(i + 1 < nact)
    def _():
        gather(i + 1, 1 - slot)

    @pl.when(i >= 2)
    def _():
        for cp in out_copies(i - 2, slot):
            cp.wait()

    @pl.when(i < nact)
    def _():
        prev = be_ref[jnp.maximum(i - 1, 0)]

        @pl.when((i == 0) | (be_ref[i] != prev))
        def _():
            wg_s[...] = wg_ref[0].astype(BF16)
            wu_s[...] = wu_ref[0].astype(BF16)
            wd_s[...] = wd_ref[0].astype(BF16)

        _wait_row_gather(xbuf.at[slot], gsem.at[slot])
        xb = _from_chunks(xbuf.at[slot]).astype(BF16)
        gate = jnp.dot(xb, wg_s[...], preferred_element_type=F32)
        up = jnp.dot(xb, wu_s[...], preferred_element_type=F32)
        hid = (jax.nn.silu(gate) * up).astype(BF16)
        _to_chunks(ybuf.at[slot], jnp.dot(hid, wd_s[...], preferred_element_type=F32))

    @pl.when(i >= nact)
    def _():
        ybuf[slot] = jnp.zeros(ybuf.shape[1:], F32)

    for cp in out_copies(i, slot):
        cp.start()

    @pl.when(i == pl.num_programs(0) - 1)
    def _():
        for cp in out_copies(i, slot):
            cp.wait()

        @pl.when(i >= 1)
        def _():
            for cp in out_copies(i - 1, 1 - slot):
                cp.wait()


def _experts(h1, slot_tok, block_e, n_active, w_gate, w_up, w_down):
    d = D_MODEL
    slots = slot_tok.shape[0]
    nb = slots // MOE_TB
    grid_spec = pltpu.PrefetchScalarGridSpec(
        num_scalar_prefetch=3,
        grid=(nb,),
        in_specs=[pl.BlockSpec(memory_space=pl.ANY),
                  pl.BlockSpec((1, d, D_EXPERT), lambda i, be, na, tk: (be[i], 0, 0)),
                  pl.BlockSpec((1, d, D_EXPERT), lambda i, be, na, tk: (be[i], 0, 0)),
                  pl.BlockSpec((1, D_EXPERT, d), lambda i, be, na, tk: (be[i], 0, 0))],
        out_specs=pl.BlockSpec(memory_space=pl.ANY),
        scratch_shapes=[pltpu.VMEM((2, CHUNKS, MOE_TB, LANES), F32),
                        pltpu.VMEM((2, CHUNKS, MOE_TB, LANES), F32),
                        pltpu.VMEM((d, D_EXPERT), BF16), pltpu.VMEM((d, D_EXPERT), BF16),
                        pltpu.VMEM((D_EXPERT, d), BF16),
                        pltpu.SemaphoreType.DMA((2,)), pltpu.SemaphoreType.DMA((2,))])
    return pl.pallas_call(
        _expert_kernel,
        grid_spec=grid_spec,
        out_shape=jax.ShapeDtypeStruct((slots, CHUNKS, LANES), F32),
        compiler_params=_params(("arbitrary",)),
        name="experts",
    )(block_e, n_active, slot_tok, h1, w_gate, w_up, w_down)


def _final_kernel(dest_ref, ys_hbm, h1_hbm, gate_ref, g_ref, b_ref, o_ref, ybuf, hbuf, gsem, hsem):
    i = pl.program_id(0)
    slot = i % 2

    def h1_copies(step, sl):
        return _tile_copies(h1_hbm, step * FIN_TM, FIN_TM, hbuf.at[sl], hsem.at[sl], False)

    def fetch(step, sl):
        for k in range(TOP_K):
            _start_row_gather(dest_ref, step * FIN_TM * TOP_K + k, TOP_K, FIN_TM, ys_hbm,
                              ybuf.at[sl, k], gsem.at[sl])
        for cp in h1_copies(step, sl):
            cp.start()

    @pl.when(i == 0)
    def _():
        fetch(0, 0)

    @pl.when(i + 1 < pl.num_programs(0))
    def _():
        fetch(i + 1, 1 - slot)

    _wait_row_gather(ybuf.at[slot], gsem.at[slot])
    for cp in h1_copies(i, slot):
        cp.wait()
    gate = gate_ref[...]
    ffn = (_from_chunks(ybuf.at[slot, 0]) * gate[:, 0:1]
           + _from_chunks(ybuf.at[slot, 1]) * gate[:, 1:2])
    o_ref[...] = _layer_norm(DN_ALPHA * _from_chunks(hbuf.at[slot]) + ffn, g_ref[...], b_ref[...])


def _final(dest, ys, h1, gate, g, b):
    n, d = h1.shape[0], D_MODEL
    grid_spec = pltpu.PrefetchScalarGridSpec(
        num_scalar_prefetch=1,
        grid=(n // FIN_TM,),
        in_specs=[pl.BlockSpec(memory_space=pl.ANY),
                  pl.BlockSpec(memory_space=pl.ANY),
                  pl.BlockSpec((FIN_TM, TOP_K), lambda i, ds: (i, 0)),
                  pl.BlockSpec((1, d), lambda i, ds: (0, 0)),
                  pl.BlockSpec((1, d), lambda i, ds: (0, 0))],
        out_specs=pl.BlockSpec((FIN_TM, d), lambda i, ds: (i, 0)),
        scratch_shapes=[pltpu.VMEM((2, TOP_K, CHUNKS, FIN_TM, LANES), F32),
                        pltpu.VMEM((2, CHUNKS, FIN_TM, LANES), F32),
                        pltpu.SemaphoreType.DMA((2,)), pltpu.SemaphoreType.DMA((2,))])
    return pl.pallas_call(
        _final_kernel,
        grid_spec=grid_spec,
        out_shape=jax.ShapeDtypeStruct((n, d), F32),
        compiler_params=_params(("arbitrary",)),
        name="final",
    )(dest, ys, h1, gate, g.reshape(1, d), b.reshape(1, d))


def kernel(x, ln0_g, ln0_b, w_in, b_in, rpb, w_proj_a, w_proj_b, w_o, b_o, ln1_g, ln1_b,
           w_router_group, b_router_group, w_router_expert, b_router_expert,
           w_gate, w_up, w_down, ln2_g, ln2_b):
    bsz, s, d = x.shape
    n = bsz * s
    assert d == D_MODEL and w_in.shape[0] == DEPTH
    scale = HEAD_DIM ** -0.5

    col_scale = np.ones((w_in.shape[2],), np.float32)
    col_scale[:NA_WIDTH] = scale
    col_scale[3 * NA_WIDTH:3 * NA_WIDTH + DIL_WIDTH] = scale
    w_all = (w_in[0] * col_scale).astype(BF16)
    b_all = (b_in[0] * col_scale).reshape(1, -1)
    assert DIL_OUT_WIDTH == INPROJ_TN
    na_blocks, ngroups = 3 * NA_WIDTH // INPROJ_TN, len(DIL_GROUPS)
    w_r = jnp.concatenate([w_router_group[0], w_router_expert[0]], axis=1)
    w_r = jnp.pad(w_r, ((0, 0), (0, ROUTER_PAD - w_r.shape[1])))
    w_r_hi = w_r.astype(BF16)
    w_r_lo = (w_r - w_r_hi.astype(F32)).astype(BF16)
    b_r = jnp.pad(jnp.concatenate([b_router_group[0], b_router_expert[0]]),
                  (0, ROUTER_PAD - N_GROUPS - N_EXPERTS)).reshape(1, ROUTER_PAD)
    tabs = _na_bias_tables(rpb[0], s // GRID_W)
    slopes = _alibi_slopes(DIL_HEADS)

    h, hb = _ln0(x.reshape(n, d), ln0_g, ln0_b)
    zq = [_inproj(hb, w_all, b_all, bsz, s, DIL_GROUPS[0][1], 3 * NA_WIDTH + 3 * DIL_OUT_WIDTH,
                  lambda j: jnp.where(j < na_blocks, j, na_blocks + (j - na_blocks) * ngroups))]
    for g in range(1, ngroups):
        zq.append(_inproj(hb, w_all, b_all, bsz, s, DIL_GROUPS[g][1], 3 * DIL_OUT_WIDTH,
                          lambda j, g=g: na_blocks + j * ngroups + g))
    na = _neighborhood_attention(zq[0].reshape(-1, bsz, s, HEAD_DIM), tabs)
    na = na.reshape(NA_HEADS, n, HEAD_DIM)
    dil = [_dilated_group(zq[g], DIL0_H if g == 0 else 0, g, slopes) for g in range(ngroups)]
    m = _mix(na, [o for o, _ in dil], [l for _, l in dil], hb,
             w_proj_a[0].astype(BF16), w_proj_b[0].astype(BF16), w_all, b_all, s)
    h1, logits = _outproj(m, h, w_o[0].astype(BF16), b_o[0], ln1_g[0], ln1_b[0], w_r_hi, w_r_lo, b_r)
    dest, gate, slot_tok, block_e, n_active = _route(logits, n)
    ys = _experts(h1, slot_tok, block_e, n_active, w_gate[0], w_up[0], w_down[0])
    out = _final(dest, ys, h1, gate, ln2_g[0], ln2_b[0])
    return out.reshape(bsz, s, d)
```

```python
import functools

import numpy as np
import jax
import jax.numpy as jnp
from jax import lax
from jax.experimental import pallas as pl
from jax.experimental.pallas import tpu as pltpu

F32 = jnp.float32
BF16 = jnp.bfloat16

D_MODEL = 2048
HEAD_DIM = 128
GRID_W = 64
NA_HEADS = 8
NA_KH = 8
NA_KW = 16
DIL_GROUPS = ((128, 1), (512, 4), (2048, 16))
DIL_HEADS_PER_GROUP = 4
DIL_HEADS = DIL_HEADS_PER_GROUP * len(DIL_GROUPS)
N_GROUPS = 8
EXPERTS_PER_GROUP = 8
N_EXPERTS = N_GROUPS * EXPERTS_PER_GROUP
TOP_K = 2
D_EXPERT = D_MODEL // 4
LN_EPS = 1e-5
DEPTH = 1
DN_ALPHA = (2 * DEPTH) ** 0.25
NA_WIDTH = NA_HEADS * HEAD_DIM
DIL_WIDTH = DIL_HEADS * HEAD_DIM
DIL_OUT_WIDTH = DIL_HEADS_PER_GROUP * HEAD_DIM
QKV_WIDTH = 3 * NA_WIDTH + 3 * DIL_WIDTH
QKV_HEADS = QKV_WIDTH // HEAD_DIM
LANES = 128
CHUNKS = D_MODEL // LANES
PACKED = CHUNKS // 2
MASK_VALUE = -1e30

QA_H, KA_H, VA_H = 0, NA_HEADS, 2 * NA_HEADS
DIL0_H = 3 * NA_HEADS

LN_TM = 512
INPROJ_TM, INPROJ_TN = 1024, 512
NA_QROWS = 4
NA_KROWS = 12
NA_NSUB = 4
DIL_NSUB = 4
DIL_TQ = 256
DIL_HALF = 64
DIL_TK = DIL_TQ + 2 * DIL_HALF
MIX_TM, MIX_TN = 512, 512
OUT_TM = 256
ROUTER_PAD = 128
ROUTE_TM = 512
MOE_TB = 128
FIN_TM = 128
VMEM_LIMIT = 56 * 1024 * 1024


def _params(sem, limit=VMEM_LIMIT):
    return pltpu.CompilerParams(dimension_semantics=sem, vmem_limit_bytes=limit)


def _layer_norm(x, g, b):
    mu = jnp.mean(x, axis=-1, keepdims=True)
    xc = x - mu
    var = jnp.mean(xc * xc, axis=-1, keepdims=True)
    return xc * lax.rsqrt(var + LN_EPS) * g + b


HIGH_HALF = np.uint32(0xFFFF0000)


def _bf16_bits(x):
    return lax.bitcast_convert_type(x.astype(BF16).astype(F32), jnp.uint32)


def _pack_rows(ref, val):
    for c in range(PACKED):
        lo = _bf16_bits(val[:, c * LANES:(c + 1) * LANES])
        hi = _bf16_bits(val[:, (c + PACKED) * LANES:(c + PACKED + 1) * LANES])
        ref[c] = (lo >> 16) | (hi & HIGH_HALF)


def _unpack_rows(ref):
    words = [ref[c] for c in range(PACKED)]
    lo = [lax.bitcast_convert_type(w << 16, F32) for w in words]
    hi = [lax.bitcast_convert_type(w & HIGH_HALF, F32) for w in words]
    return jnp.concatenate(lo + hi, axis=-1)


def _tile_copies(hbm, row0, rows, vmem, sem, to_hbm):
    copies = []
    for c in range(PACKED):
        h, v = hbm.at[pl.ds(row0, rows), c], vmem.at[c]
        copies.append(pltpu.make_async_copy(v, h, sem) if to_hbm else pltpu.make_async_copy(h, v, sem))
    return copies


def _start_row_gather(idx_ref, base, step, count, src_hbm, dst, sem):
    def start(t, c):
        pltpu.make_async_copy(src_hbm.at[idx_ref[base + t * step]], dst.at[:, t], sem).start()
        return c

    lax.fori_loop(0, count, start, 0, unroll=4)


def _wait_row_gather(dst, sem):
    pltpu.make_async_copy(dst, dst, sem).wait()


def _ln0_kernel(x_ref, g_ref, b_ref, h_ref, hb_ref):
    y = _layer_norm(x_ref[...], g_ref[...], b_ref[...])
    h_ref[...] = y
    hb_ref[...] = y.astype(BF16)


def _ln0(x, g, b):
    n, d = x.shape
    row = pl.BlockSpec((LN_TM, d), lambda i: (i, 0))
    vec = pl.BlockSpec((1, d), lambda i: (0, 0))
    return pl.pallas_call(
        _ln0_kernel,
        grid=(n // LN_TM,),
        in_specs=[row, vec, vec],
        out_specs=[row, row],
        out_shape=[jax.ShapeDtypeStruct((n, d), F32), jax.ShapeDtypeStruct((n, d), BF16)],
        compiler_params=_params(("parallel",)),
        name="ln0",
    )(x, g.reshape(1, d), b.reshape(1, d))


def _inproj_kernel(hb_ref, w_ref, b_ref, o_ref, acc_s, *, dil):
    acc = jnp.dot(hb_ref[...], w_ref[...], preferred_element_type=F32) + b_ref[...]
    if dil == 1:
        for c in range(INPROJ_TN // LANES):
            o_ref[c, 0, 0] = acc[:, c * LANES:(c + 1) * LANES].astype(BF16)
    else:
        for c in range(INPROJ_TN // LANES):
            acc_s[c] = acc[:, c * LANES:(c + 1) * LANES]
        for c in range(INPROJ_TN // LANES):
            for r in range(dil):
                o_ref[c, 0, r] = acc_s[c, pl.ds(r, INPROJ_TM // dil, stride=dil), :].astype(BF16)


def _inproj(hb, w, b, bsz, s, dil, width, col_block):
    n, d = hb.shape
    tiles = s // INPROJ_TM
    assert INPROJ_TM % (16 * dil) == 0 and s % INPROJ_TM == 0 and width % INPROJ_TN == 0
    return pl.pallas_call(
        functools.partial(_inproj_kernel, dil=dil),
        grid=(n // INPROJ_TM, width // INPROJ_TN),
        in_specs=[pl.BlockSpec((INPROJ_TM, d), lambda i, j: (i, 0)),
                  pl.BlockSpec((d, INPROJ_TN), lambda i, j: (0, col_block(j))),
                  pl.BlockSpec((1, INPROJ_TN), lambda i, j: (0, col_block(j)))],
        out_specs=pl.BlockSpec((INPROJ_TN // LANES, 1, dil, INPROJ_TM // dil, LANES),
                               lambda i, j: (j, i // tiles, 0, i % tiles, 0)),
        out_shape=jax.ShapeDtypeStruct((width // LANES, bsz, dil, s // dil, LANES), BF16),
        scratch_shapes=[pltpu.VMEM((INPROJ_TN // LANES, INPROJ_TM, LANES), F32)],
        compiler_params=_params(("parallel", "arbitrary")),
        name=f"inproj_{dil}",
    )(hb, w, b)


def _na_bias_tables(rpb, rows):
    heads, n_dr, n_dc = rpb.shape
    qc = np.arange(GRID_W)[:, None]
    kc = np.arange(GRID_W)[None, :]
    qcs = np.clip(qc - NA_KW // 2, 0, GRID_W - NA_KW)
    v_col = (kc >= qcs) & (kc < qcs + NA_KW)
    dc = np.clip(kc - qc + NA_KW - 1, 0, n_dc - 1)
    onehot = (dc[None] == np.arange(n_dc)[:, None, None]) & v_col[None]
    toep = jnp.einsum('hrd,dqk->hrqk', rpb.astype(F32), jnp.asarray(onehot, F32),
                      precision=lax.Precision.HIGHEST)
    toep = jnp.where(jnp.asarray(v_col), toep, MASK_VALUE)
    masked = jnp.full((heads, 1, GRID_W, GRID_W), MASK_VALUE, F32)
    blocks = jnp.concatenate([toep, masked], axis=1)
    i = np.arange(NA_QROWS)[:, None]
    j = np.arange(NA_KROWS)[None, :]
    sel = []
    for r0, ks in ((0, 0), (2 * NA_QROWS, 2 * NA_QROWS - NA_KH // 2), (rows - NA_QROWS, rows - NA_KROWS)):
        r, krow = r0 + i, ks + j
        start = np.clip(r - NA_KH // 2, 0, rows - NA_KH)
        v_row = (krow >= start) & (krow < start + NA_KH)
        sel.append(np.where(v_row, krow - r + NA_KH - 1, n_dr))
    sel = np.stack(sel).reshape(-1)
    tab = jnp.take(blocks, jnp.asarray(sel, jnp.int32), axis=1)
    tab = tab.reshape(heads, 3, NA_QROWS, NA_KROWS, GRID_W, GRID_W).transpose(1, 0, 2, 4, 3, 5)
    return tab.reshape(3, heads, NA_QROWS * GRID_W, NA_KROWS * GRID_W)


def _softmax_attend(q, kw, vw, bias):
    s = lax.dot_general(q, kw, (((1,), (1,)), ((), ())), preferred_element_type=F32) + bias
    m = jnp.max(s, axis=-1, keepdims=True)
    p = jnp.exp(s - m)
    l = jnp.sum(p, axis=-1, keepdims=True)
    o = jnp.dot(p.astype(BF16), vw, preferred_element_type=F32) / l
    return o, m, l


def _edge_variant(blk, nblk):
    return jnp.where(blk == 0, 0, jnp.where(blk == nblk - 1, 2, 1))


def _na_kernel(q_ref, k_ref, v_ref, tab_ref, o_ref, *, rows):
    tq = NA_QROWS * GRID_W
    for u in range(NA_NSUB):
        blk = pl.program_id(2) * NA_NSUB + u
        ks = jnp.clip(blk * NA_QROWS - NA_KH // 2, 0, rows - NA_KROWS)
        off = pl.multiple_of(ks * GRID_W, GRID_W)
        kw = k_ref[0, 0, pl.ds(off, NA_KROWS * GRID_W), :]
        vw = v_ref[0, 0, pl.ds(off, NA_KROWS * GRID_W), :]
        bias = tab_ref[_edge_variant(blk, rows // NA_QROWS), 0]
        o, _, _ = _softmax_attend(q_ref[0, 0, u * tq:(u + 1) * tq, :], kw, vw, bias)
        o_ref[0, 0, u * tq:(u + 1) * tq, :] = o.astype(BF16)


def _neighborhood_attention(zq4, tabs):
    _, bsz, s, _ = zq4.shape
    rows = s // GRID_W
    assert rows % (NA_QROWS * NA_NSUB) == 0 and rows >= NA_KROWS + NA_QROWS
    tq = NA_QROWS * GRID_W * NA_NSUB
    return pl.pallas_call(
        functools.partial(_na_kernel, rows=rows),
        grid=(bsz, NA_HEADS, s // tq),
        in_specs=[pl.BlockSpec((1, 1, tq, HEAD_DIM), lambda b, h, i: (QA_H + h, b, i, 0)),
                  pl.BlockSpec((1, 1, s, HEAD_DIM), lambda b, h, i: (KA_H + h, b, 0, 0)),
                  pl.BlockSpec((1, 1, s, HEAD_DIM), lambda b, h, i: (VA_H + h, b, 0, 0)),
                  pl.BlockSpec((3, 1) + tabs.shape[2:], lambda b, h, i: (0, h, 0, 0))],
        out_specs=pl.BlockSpec((1, 1, tq, HEAD_DIM), lambda b, h, i: (h, b, i, 0)),
        out_shape=jax.ShapeDtypeStruct((NA_HEADS, bsz, s, HEAD_DIM), BF16),
        compiler_params=_params(("parallel", "parallel", "arbitrary")),
        name="na_attn",
    )(zq4, zq4, zq4, tabs)


def _alibi_slopes(n):
    return np.array([2.0 ** (-8.0 * (i + 1) / n) for i in range(n)], dtype=np.float32)


def _dil_bias_tables(slopes, dil):
    qi = np.arange(DIL_TQ)[:, None]
    kj = np.arange(DIL_TK)[None, :]
    dist = np.stack([np.abs(kj - qi + shift) for shift in (0, -DIL_HALF, -2 * DIL_HALF)])
    dist = jnp.asarray(dist, F32)[:, None]
    penalty = jnp.asarray(slopes, F32)[None, :, None, None] * (dist * float(dil))
    return jnp.where(dist <= DIL_HALF, -penalty, MASK_VALUE)


def _dil_kernel(q_ref, k_ref, v_ref, tab_ref, o_ref, lse_ref, *, length, nsub):
    for u in range(nsub):
        blk = pl.program_id(3) * nsub + u
        ks = jnp.clip(blk * DIL_TQ - DIL_HALF, 0, length - DIL_TK)
        off = pl.multiple_of(ks, DIL_HALF)
        kw = k_ref[0, 0, 0, pl.ds(off, DIL_TK), :]
        vw = v_ref[0, 0, 0, pl.ds(off, DIL_TK), :]
        bias = tab_ref[_edge_variant(blk, length // DIL_TQ), 0]
        rows = slice(u * DIL_TQ, (u + 1) * DIL_TQ)
        o, m, l = _softmax_attend(q_ref[0, 0, 0, rows, :], kw, vw, bias)
        o_ref[0, 0, 0, rows, :] = o.astype(BF16)
        lse_ref[0, 0, 0, rows, :] = jnp.broadcast_to(m + jnp.log(l), (DIL_TQ, LANES))


def _dilated_group(zq, base, g, slopes):
    window, dil = DIL_GROUPS[g]
    assert (window // 2) // dil == DIL_HALF
    _, bsz, _, length, _ = zq.shape
    nsub = min(DIL_NSUB, length // DIL_TQ)
    assert zq.shape[2] == dil and length % (DIL_TQ * nsub) == 0 and length >= DIL_TK
    gh = DIL_HEADS_PER_GROUP
    tabs = _dil_bias_tables(slopes[g * gh:(g + 1) * gh], dil)
    tq = DIL_TQ * nsub
    qspec = pl.BlockSpec((1, 1, 1, tq, HEAD_DIM), lambda b, h, r, n: (base + h, b, r, n, 0))
    kvspec = lambda off: pl.BlockSpec((1, 1, 1, length, HEAD_DIM),
                                      lambda b, h, r, n: (base + off + h, b, r, 0, 0))
    ospec = pl.BlockSpec((1, 1, 1, tq, HEAD_DIM), lambda b, h, r, n: (h, b, r, n, 0))
    return pl.pallas_call(
        functools.partial(_dil_kernel, length=length, nsub=nsub),
        grid=(bsz, gh, dil, length // tq),
        in_specs=[qspec, kvspec(gh), kvspec(2 * gh),
                  pl.BlockSpec((3, 1, DIL_TQ, DIL_TK), lambda b, h, r, n: (0, h, 0, 0))],
        out_specs=[ospec, ospec],
        out_shape=[jax.ShapeDtypeStruct((gh, bsz, dil, length, HEAD_DIM), BF16),
                   jax.ShapeDtypeStruct((gh, bsz, dil, length, HEAD_DIM), F32)],
        compiler_params=_params(("parallel", "parallel", "parallel", "arbitrary")),
        name=f"dil_attn_{dil}",
    )(zq, zq, zq, tabs)


def _mix_kernel(na_ref, o1_ref, o2_ref, o3_ref, l1_ref, l2_ref, l3_ref, hb_ref,
                wpa_ref, wpb_ref, wga_ref, wgb_ref, bga_ref, bgb_ref, m_ref, na_s, dil_s, o_s, l_s):
    @pl.when(pl.program_id(1) == 0)
    def _():
        for h in range(NA_HEADS):
            na_s[:, h * HEAD_DIM:(h + 1) * HEAD_DIM] = na_ref[h]
        for g, (o_ref, l_ref) in enumerate(((o1_ref, l1_ref), (o2_ref, l2_ref), (o3_ref, l3_ref))):
            dil = DIL_GROUPS[g][1]
            for h in range(DIL_HEADS_PER_GROUP):
                for r in range(dil):
                    rows = slice(None) if dil == 1 else pl.ds(r, MIX_TM // dil, stride=dil)
                    o_s[g, h, rows, :] = o_ref[h, 0, r].astype(F32)
                    l_s[g, h, rows, :] = l_ref[h, 0, r]
        for h in range(DIL_HEADS_PER_GROUP):
            ls = [l_s[g, h] for g in range(len(DIL_GROUPS))]
            mx = jnp.maximum(jnp.maximum(ls[0], ls[1]), ls[2])
            es = [jnp.exp(l - mx) for l in ls]
            den = es[0] + es[1] + es[2]
            acc = (es[0] * o_s[0, h] + es[1] * o_s[1, h] + es[2] * o_s[2, h]) / den
            dil_s[:, h * HEAD_DIM:(h + 1) * HEAD_DIM] = acc.astype(BF16)

    hb = hb_ref[...]
    ya = jnp.dot(na_s[...], wpa_ref[...], preferred_element_type=F32)
    yb = jnp.dot(dil_s[...], wpb_ref[...], preferred_element_type=F32)
    ga = jnp.dot(hb, wga_ref[...], preferred_element_type=F32) + bga_ref[...]
    gb = jnp.dot(hb, wgb_ref[...], preferred_element_type=F32) + bgb_ref[...]
    m_ref[...] = (jax.nn.sigmoid(ga) * ya + jax.nn.sigmoid(gb) * yb).astype(BF16)


def _mix(na, dil_o, dil_lse, hb, wpa, wpb, w_all, b_all, s):
    n, d = hb.shape
    tm, tn = MIX_TM, MIX_TN
    tiles = s // tm
    gh = DIL_HEADS_PER_GROUP
    ngroups = len(DIL_GROUPS)
    assert all(tm % (16 * dil) == 0 for _, dil in DIL_GROUPS) and s % tm == 0
    col = lambda k: pl.BlockSpec((k, tn), lambda i, j: (0, j))
    ga0, gb0 = QKV_WIDTH // tn, (QKV_WIDTH + d) // tn
    gate = lambda k, first: pl.BlockSpec((k, tn), lambda i, j: (0, first + j))
    grp = lambda dil: pl.BlockSpec((gh, 1, dil, tm // dil, HEAD_DIM),
                                   lambda i, j: (0, i // tiles, 0, i % tiles, 0))
    groups = [grp(dil) for _, dil in DIL_GROUPS]
    return pl.pallas_call(
        _mix_kernel,
        grid=(n // tm, d // tn),
        in_specs=[pl.BlockSpec((NA_HEADS, tm, HEAD_DIM), lambda i, j: (0, i, 0)), *groups, *groups,
                  pl.BlockSpec((tm, d), lambda i, j: (i, 0)),
                  col(NA_WIDTH), col(DIL_OUT_WIDTH),
                  gate(d, ga0), gate(d, gb0), gate(1, ga0), gate(1, gb0)],
        out_specs=pl.BlockSpec((tm, tn), lambda i, j: (i, j)),
        out_shape=jax.ShapeDtypeStruct((n, d), BF16),
        scratch_shapes=[pltpu.VMEM((tm, NA_WIDTH), BF16), pltpu.VMEM((tm, DIL_OUT_WIDTH), BF16),
                        pltpu.VMEM((ngroups, gh, tm, HEAD_DIM), F32),
                        pltpu.VMEM((ngroups, gh, tm, HEAD_DIM), F32)],
        compiler_params=_params(("parallel", "arbitrary")),
        name="mix",
    )(na, *dil_o, *dil_lse, hb, wpa, wpb, w_all, w_all, b_all, b_all)


def _outproj_kernel(m_ref, h_ref, wo_ref, bo_ref, g_ref, b_ref, wrh_ref, wrl_ref, br_ref,
                    h1_ref, h1p_hbm, lg_ref, hbuf, sem):
    i = pl.program_id(0)
    slot = i % 2

    def out_copies(step, sl):
        return _tile_copies(h1p_hbm, step * OUT_TM, OUT_TM, hbuf.at[sl], sem.at[sl], True)

    mix = jnp.dot(m_ref[...], wo_ref[...], preferred_element_type=F32) + bo_ref[...]
    h1 = _layer_norm(DN_ALPHA * h_ref[...] + mix, g_ref[...], b_ref[...])
    h1_ref[...] = h1

    @pl.when(i >= 2)
    def _():
        for cp in out_copies(i - 2, slot):
            cp.wait()

    _pack_rows(hbuf.at[slot], h1)
    for cp in out_copies(i, slot):
        cp.start()

    @pl.when(i == pl.num_programs(0) - 1)
    def _():
        for cp in out_copies(i, slot):
            cp.wait()

        @pl.when(i >= 1)
        def _():
            for cp in out_copies(i - 1, 1 - slot):
                cp.wait()

    hi = h1.astype(BF16)
    lo = (h1 - hi.astype(F32)).astype(BF16)
    lg = (jnp.dot(hi, wrh_ref[...], preferred_element_type=F32)
          + jnp.dot(lo, wrh_ref[...], preferred_element_type=F32)
          + jnp.dot(hi, wrl_ref[...], preferred_element_type=F32))
    lg_ref[...] = lg + br_ref[...]


def _outproj(m, h, wo, bo, g, b, wr_hi, wr_lo, br):
    n, d = h.shape
    tm = OUT_TM
    row = lambda w: pl.BlockSpec((tm, w), lambda i: (i, 0))
    full = lambda r, c: pl.BlockSpec((r, c), lambda i: (0, 0))
    return pl.pallas_call(
        _outproj_kernel,
        grid=(n // tm,),
        in_specs=[row(d), row(d), full(d, d), full(1, d), full(1, d), full(1, d),
                  full(d, ROUTER_PAD), full(d, ROUTER_PAD), full(1, ROUTER_PAD)],
        out_specs=[row(d), pl.BlockSpec(memory_space=pl.ANY), row(ROUTER_PAD)],
        out_shape=[jax.ShapeDtypeStruct((n, d), F32),
                   jax.ShapeDtypeStruct((n, PACKED, LANES), jnp.uint32),
                   jax.ShapeDtypeStruct((n, ROUTER_PAD), F32)],
        scratch_shapes=[pltpu.VMEM((2, PACKED, tm, LANES), jnp.uint32), pltpu.SemaphoreType.DMA((2,))],
        compiler_params=_params(("arbitrary",)),
        name="outproj",
    )(m, h, wo, bo.reshape(1, d), g.reshape(1, d), b.reshape(1, d), wr_hi, wr_lo, br)


def _first_argmax(vals, lane_f):
    top = jnp.max(vals, axis=-1, keepdims=True)
    idx = jnp.min(jnp.where(vals == top, lane_f, float(LANES)), axis=-1, keepdims=True)
    return top, idx


def _route_kernel(lg_ref, info_ref, gate_ref, cnt_ref, base_s):
    @pl.when(pl.program_id(0) == 0)
    def _():
        base_s[...] = jnp.zeros(base_s.shape, F32)

    lg = lg_ref[...]
    lane = lax.broadcasted_iota(jnp.int32, lg.shape, 1)
    lane_f = lane.astype(F32)
    g_mask = lane < N_GROUPS
    g_top, g_sel = _first_argmax(jnp.where(g_mask, lg, MASK_VALUE), lane_f)
    g_prob = 1.0 / jnp.sum(jnp.where(g_mask, jnp.exp(lg - g_top), 0.0), axis=-1, keepdims=True)
    first = N_GROUPS + g_sel * EXPERTS_PER_GROUP
    e_mask = (lane_f >= first) & (lane_f < first + EXPERTS_PER_GROUP)
    el = jnp.where(e_mask, lg, MASK_VALUE)
    v0, i0 = _first_argmax(el, lane_f)
    v1, i1 = _first_argmax(jnp.where(lane_f == i0, MASK_VALUE, el), lane_f)
    e1 = jnp.exp(v1 - v0)
    w0 = g_prob / (1.0 + e1)
    w1 = g_prob * e1 / (1.0 + e1)

    tm = lg.shape[0]
    tri = (lax.broadcasted_iota(jnp.int32, (tm, tm), 1)
           < lax.broadcasted_iota(jnp.int32, (tm, tm), 0)).astype(BF16)
    ranks = []
    for idx in (i0, i1):
        onehot = lane_f == idx - N_GROUPS
        before = jnp.dot(tri, onehot.astype(BF16), preferred_element_type=F32) + base_s[...]
        ranks.append(jnp.sum(jnp.where(onehot, before, 0.0), axis=-1, keepdims=True))
        base_s[...] = base_s[...] + jnp.sum(onehot.astype(F32), axis=0, keepdims=True)

    info = jnp.where(lane == 0, i0 - N_GROUPS,
                     jnp.where(lane == 1, i1 - N_GROUPS,
                               jnp.where(lane == 2, ranks[0], jnp.where(lane == 3, ranks[1], 0.0))))
    info_ref[...] = info.astype(jnp.int32)
    gate_ref[...] = jnp.where(lane == 0, w0, jnp.where(lane == 1, w1, 0.0))
    cnt_ref[...] = base_s[...]


def _route(logits, n):
    tm = ROUTE_TM
    row = pl.BlockSpec((tm, ROUTER_PAD), lambda i: (i, 0))
    one = pl.BlockSpec((1, ROUTER_PAD), lambda i: (0, 0))
    info, gate, counts = pl.pallas_call(
        _route_kernel,
        grid=(n // tm,),
        in_specs=[row],
        out_specs=[row, row, one],
        out_shape=[jax.ShapeDtypeStruct((n, ROUTER_PAD), jnp.int32),
                   jax.ShapeDtypeStruct((n, ROUTER_PAD), F32),
                   jax.ShapeDtypeStruct((1, ROUTER_PAD), F32)],
        scratch_shapes=[pltpu.VMEM((1, ROUTER_PAD), F32)],
        compiler_params=_params(("arbitrary",)),
        name="route",
    )(logits)
    gate = gate[:, :TOP_K]
    a = n * TOP_K
    e_flat = info[:, :TOP_K].reshape(a)
    rank = info[:, TOP_K:2 * TOP_K].reshape(a)
    counts = counts[0, :N_EXPERTS].astype(jnp.int32)
    pcounts = (counts + MOE_TB - 1) // MOE_TB * MOE_TB
    pends = jnp.cumsum(pcounts)
    pstarts = pends - pcounts
    dest = (pstarts[e_flat] + rank).astype(jnp.int32)
    nb = a // MOE_TB + N_EXPERTS
    slots = nb * MOE_TB
    tok = jnp.arange(a, dtype=jnp.int32) // TOP_K
    slot_tok = jnp.zeros((slots,), jnp.int32).at[dest].set(tok)
    block_start = jnp.arange(nb, dtype=jnp.int32) * MOE_TB
    block_e = jnp.minimum(jnp.sum((pends[None, :] <= block_start[:, None]).astype(jnp.int32), axis=1),
                          N_EXPERTS - 1)
    n_active = (pends[-1] // MOE_TB).astype(jnp.int32).reshape(1)
    ids = jnp.arange(N_EXPERTS, dtype=jnp.int32)
    later = (counts[None, :] > 0) & (ids[None, :] > ids[:, None])
    nxt = jnp.min(jnp.where(later, ids[None, :], N_EXPERTS), axis=1)
    block_next = jnp.where(nxt == N_EXPERTS, -1, nxt)[block_e].astype(jnp.int32)
    return dest, gate, slot_tok, block_e, block_next, n_active


def _expert_kernel(be_ref, nx_ref, nact_ref, tok_ref, h1p_hbm, wg_hbm, wu_hbm, wd_hbm, ys_hbm,
                   xbuf, ybuf, wg_f, wu_f, wd_f, wg_s, wu_s, wd_s, gsem, osem, wsem):
    i = pl.program_id(0)
    slot = i % 2
    nact = nact_ref[0]

    def gather(step, sl):
        _start_row_gather(tok_ref, step * MOE_TB, 1, MOE_TB, h1p_hbm, xbuf.at[sl], gsem.at[sl])

    def out_copies(step, sl):
        return _tile_copies(ys_hbm, step * MOE_TB, MOE_TB, ybuf.at[sl], osem.at[sl], True)

    def weight_copies(e):
        return [pltpu.make_async_copy(src.at[e], dst, wsem)
                for src, dst in ((wg_hbm, wg_f), (wu_hbm, wu_f), (wd_hbm, wd_f))]

    @pl.when((i == 0) & (nact > 0))
    def _():
        for cp in weight_copies(be_ref[0]):
            cp.start()
        gather(0, 0)

    @pl.when(i + 1 < nact)
    def _():
        gather(i + 1, 1 - slot)

    @pl.when(i >= 2)
    def _():
        for cp in out_copies(i - 2, slot):
            cp.wait()

    @pl.when(i < nact)
    def _():
        expert = be_ref[i]

        @pl.when((i == 0) | (expert != be_ref[jnp.maximum(i - 1, 0)]))
        def _():
            for cp in weight_copies(expert):
                cp.wait()
            wg_s[...] = wg_f[...].astype(BF16)
            wu_s[...] = wu_f[...].astype(BF16)
            wd_s[...] = wd_f[...].astype(BF16)

            @pl.when(nx_ref[i] >= 0)
            def _():
                for cp in weight_copies(nx_ref[i]):
                    cp.start()

        _wait_row_gather(xbuf.at[slot], gsem.at[slot])
        xb = _unpack_rows(xbuf.at[slot]).astype(BF16)
        gate = jnp.dot(xb, wg_s[...], preferred_element_type=F32)
        up = jnp.dot(xb, wu_s[...], preferred_element_type=F32)
        hid = (jax.nn.silu(gate) * up).astype(BF16)
        _pack_rows(ybuf.at[slot], jnp.dot(hid, wd_s[...], preferred_element_type=F32))

    @pl.when(i >= nact)
    def _():
        ybuf[slot] = jnp.zeros(ybuf.shape[1:], jnp.uint32)

    for cp in out_copies(i, slot):
        cp.start()

    @pl.when(i == pl.num_programs(0) - 1)
    def _():
        for cp in out_copies(i, slot):
            cp.wait()

        @pl.when(i >= 1)
        def _():
            for cp in out_copies(i - 1, 1 - slot):
                cp.wait()


def _experts(h1p, slot_tok, block_e, block_next, n_active, w_gate, w_up, w_down):
    d = D_MODEL
    slots = slot_tok.shape[0]
    nb = slots // MOE_TB
    any_spec = pl.BlockSpec(memory_space=pl.ANY)
    grid_spec = pltpu.PrefetchScalarGridSpec(
        num_scalar_prefetch=4,
        grid=(nb,),
        in_specs=[any_spec, any_spec, any_spec, any_spec],
        out_specs=any_spec,
        scratch_shapes=[pltpu.VMEM((2, PACKED, MOE_TB, LANES), jnp.uint32),
                        pltpu.VMEM((2, PACKED, MOE_TB, LANES), jnp.uint32),
                        pltpu.VMEM((d, D_EXPERT), F32), pltpu.VMEM((d, D_EXPERT), F32),
                        pltpu.VMEM((D_EXPERT, d), F32),
                        pltpu.VMEM((d, D_EXPERT), BF16), pltpu.VMEM((d, D_EXPERT), BF16),
                        pltpu.VMEM((D_EXPERT, d), BF16),
                        pltpu.SemaphoreType.DMA((2,)), pltpu.SemaphoreType.DMA((2,)),
                        pltpu.SemaphoreType.DMA(())])
    return pl.pallas_call(
        _expert_kernel,
        grid_spec=grid_spec,
        out_shape=jax.ShapeDtypeStruct((slots, PACKED, LANES), jnp.uint32),
        compiler_params=_params(("arbitrary",)),
        name="experts",
    )(block_e, block_next, n_active, slot_tok, h1p, w_gate, w_up, w_down)


def _final_kernel(dest_ref, ys_hbm, h1_ref, gate_ref, g_ref, b_ref, o_ref, ybuf, gsem):
    i = pl.program_id(0)
    slot = i % 2

    def fetch(step, sl):
        for k in range(TOP_K):
            _start_row_gather(dest_ref, step * FIN_TM * TOP_K + k, TOP_K, FIN_TM, ys_hbm,
                              ybuf.at[sl, k], gsem.at[sl])

    @pl.when(i == 0)
    def _():
        fetch(0, 0)

    @pl.when(i + 1 < pl.num_programs(0))
    def _():
        fetch(i + 1, 1 - slot)

    _wait_row_gather(ybuf.at[slot], gsem.at[slot])
    gate = gate_ref[...]
    ffn = (_unpack_rows(ybuf.at[slot, 0]) * gate[:, 0:1]
           + _unpack_rows(ybuf.at[slot, 1]) * gate[:, 1:2])
    o_ref[...] = _layer_norm(DN_ALPHA * h1_ref[...] + ffn, g_ref[...], b_ref[...])


def _final(dest, ys, h1, gate, g, b):
    n, d = h1.shape
    grid_spec = pltpu.PrefetchScalarGridSpec(
        num_scalar_prefetch=1,
        grid=(n // FIN_TM,),
        in_specs=[pl.BlockSpec(memory_space=pl.ANY),
                  pl.BlockSpec((FIN_TM, d), lambda i, ds: (i, 0)),
                  pl.BlockSpec((FIN_TM, TOP_K), lambda i, ds: (i, 0)),
                  pl.BlockSpec((1, d), lambda i, ds: (0, 0)),
                  pl.BlockSpec((1, d), lambda i, ds: (0, 0))],
        out_specs=pl.BlockSpec((FIN_TM, d), lambda i, ds: (i, 0)),
        scratch_shapes=[pltpu.VMEM((2, TOP_K, PACKED, FIN_TM, LANES), jnp.uint32),
                        pltpu.SemaphoreType.DMA((2,))])
    return pl.pallas_call(
        _final_kernel,
        grid_spec=grid_spec,
        out_shape=jax.ShapeDtypeStruct((n, d), F32),
        compiler_params=_params(("arbitrary",)),
        name="final",
    )(dest, ys, h1, gate, g.reshape(1, d), b.reshape(1, d))


def kernel(x, ln0_g, ln0_b, w_in, b_in, rpb, w_proj_a, w_proj_b, w_o, b_o, ln1_g, ln1_b,
           w_router_group, b_router_group, w_router_expert, b_router_expert,
           w_gate, w_up, w_down, ln2_g, ln2_b):
    bsz, s, d = x.shape
    n = bsz * s
    assert d == D_MODEL and w_in.shape[0] == DEPTH
    scale = HEAD_DIM ** -0.5

    col_scale = np.ones((w_in.shape[2],), np.float32)
    col_scale[:NA_WIDTH] = scale
    col_scale[3 * NA_WIDTH:3 * NA_WIDTH + DIL_WIDTH] = scale
    w_all = (w_in[0] * col_scale).astype(BF16)
    b_all = (b_in[0] * col_scale).reshape(1, -1)
    assert DIL_OUT_WIDTH == INPROJ_TN
    na_blocks, ngroups = 3 * NA_WIDTH // INPROJ_TN, len(DIL_GROUPS)
    w_r = jnp.concatenate([w_router_group[0], w_router_expert[0]], axis=1)
    w_r = jnp.pad(w_r, ((0, 0), (0, ROUTER_PAD - w_r.shape[1])))
    w_r_hi = w_r.astype(BF16)
    w_r_lo = (w_r - w_r_hi.astype(F32)).astype(BF16)
    b_r = jnp.pad(jnp.concatenate([b_router_group[0], b_router_expert[0]]),
                  (0, ROUTER_PAD - N_GROUPS - N_EXPERTS)).reshape(1, ROUTER_PAD)
    tabs = _na_bias_tables(rpb[0], s // GRID_W)
    slopes = _alibi_slopes(DIL_HEADS)

    h, hb = _ln0(x.reshape(n, d), ln0_g, ln0_b)
    zq = [_inproj(hb, w_all, b_all, bsz, s, DIL_GROUPS[0][1], 3 * NA_WIDTH + 3 * DIL_OUT_WIDTH,
                  lambda j: jnp.where(j < na_blocks, j, na_blocks + (j - na_blocks) * ngroups))]
    for g in range(1, ngroups):
        zq.append(_inproj(hb, w_all, b_all, bsz, s, DIL_GROUPS[g][1], 3 * DIL_OUT_WIDTH,
                          lambda j, g=g: na_blocks + j * ngroups + g))
    na = _neighborhood_attention(zq[0].reshape(-1, bsz, s, HEAD_DIM), tabs)
    na = na.reshape(NA_HEADS, n, HEAD_DIM)
    dil = [_dilated_group(zq[g], DIL0_H if g == 0 else 0, g, slopes) for g in range(ngroups)]
    m = _mix(na, [o for o, _ in dil], [l for _, l in dil], hb,
             w_proj_a[0].astype(BF16), w_proj_b[0].astype(BF16), w_all, b_all, s)
    h1, h1p, logits = _outproj(m, h, w_o[0].astype(BF16), b_o[0], ln1_g[0], ln1_b[0],
                               w_r_hi, w_r_lo, b_r)
    dest, gate, slot_tok, block_e, block_next, n_active = _route(logits, n)
    ys = _experts(h1p, slot_tok, block_e, block_next, n_active, w_gate[0], w_up[0], w_down[0])
    out = _final(dest, ys, h1, gate, ln2_g[0], ln2_b[0])
    return out.reshape(bsz, s, d)
```

```python
import functools

import numpy as np
import jax
import jax.numpy as jnp
from jax import lax
from jax.experimental import pallas as pl
from jax.experimental.pallas import tpu as pltpu

F32 = jnp.float32
BF16 = jnp.bfloat16

D_MODEL = 2048
HEAD_DIM = 128
GRID_W = 64
NA_HEADS = 8
NA_KH = 8
NA_KW = 16
DIL_GROUPS = ((128, 1), (512, 4), (2048, 16))
DIL_HEADS_PER_GROUP = 4
DIL_HEADS = DIL_HEADS_PER_GROUP * len(DIL_GROUPS)
N_GROUPS = 8
EXPERTS_PER_GROUP = 8
N_EXPERTS = N_GROUPS * EXPERTS_PER_GROUP
TOP_K = 2
D_EXPERT = D_MODEL // 4
LN_EPS = 1e-5
DEPTH = 1
DN_ALPHA = (2 * DEPTH) ** 0.25
NA_WIDTH = NA_HEADS * HEAD_DIM
DIL_WIDTH = DIL_HEADS * HEAD_DIM
DIL_OUT_WIDTH = DIL_HEADS_PER_GROUP * HEAD_DIM
QKV_WIDTH = 3 * NA_WIDTH + 3 * DIL_WIDTH
QKV_HEADS = QKV_WIDTH // HEAD_DIM
LANES = 128
CHUNKS = D_MODEL // LANES
PACKED = CHUNKS // 2
MASK_VALUE = -1e30

QA_H, KA_H, VA_H = 0, NA_HEADS, 2 * NA_HEADS
DIL0_H = 3 * NA_HEADS

LN_TM = 512
INPROJ_TM, INPROJ_TN = 1024, 512
NA_QROWS = 4
NA_KROWS = 12
NA_NSUB = 4
DIL_NSUB = 4
DIL_TQ = 256
DIL_HALF = 64
DIL_TK = DIL_TQ + 2 * DIL_HALF
MIX_TM, MIX_TN = 512, 512
OUT_TM = 256
ROUTER_PAD = 128
ROUTE_TM = 512
MOE_TB = 128
FIN_TM = 128
WEIGHT_DMA_PRIORITY = 1
VMEM_LIMIT = 56 * 1024 * 1024


def _params(sem, limit=VMEM_LIMIT):
    return pltpu.CompilerParams(dimension_semantics=sem, vmem_limit_bytes=limit)


def _layer_norm(x, g, b):
    mu = jnp.mean(x, axis=-1, keepdims=True)
    xc = x - mu
    var = jnp.mean(xc * xc, axis=-1, keepdims=True)
    return xc * lax.rsqrt(var + LN_EPS) * g + b


HIGH_HALF = np.uint32(0xFFFF0000)


def _bf16_bits(x):
    return lax.bitcast_convert_type(x.astype(BF16).astype(F32), jnp.uint32)


def _pack_rows(ref, val):
    for c in range(PACKED):
        lo = _bf16_bits(val[:, c * LANES:(c + 1) * LANES])
        hi = _bf16_bits(val[:, (c + PACKED) * LANES:(c + PACKED + 1) * LANES])
        ref[c] = (lo >> 16) | (hi & HIGH_HALF)


def _unpack_rows(ref):
    words = [ref[c] for c in range(PACKED)]
    lo = [lax.bitcast_convert_type(w << 16, F32) for w in words]
    hi = [lax.bitcast_convert_type(w & HIGH_HALF, F32) for w in words]
    return jnp.concatenate(lo + hi, axis=-1)


def _tile_copies(hbm, row0, rows, vmem, sem, to_hbm):
    copies = []
    for c in range(PACKED):
        h, v = hbm.at[pl.ds(row0, rows), c], vmem.at[c]
        copies.append(pltpu.make_async_copy(v, h, sem) if to_hbm else pltpu.make_async_copy(h, v, sem))
    return copies


def _start_row_gather(idx_ref, base, step, count, src_hbm, dst, sem, both_threads):
    def start(j, c):
        for half in range(2):
            t = 2 * j + half
            pltpu.make_async_copy(src_hbm.at[idx_ref[base + t * step]], dst.at[:, t], sem).start(
                priority=half if both_threads else 0)
        return c

    lax.fori_loop(0, count // 2, start, 0, unroll=2)


def _wait_row_gather(dst, sem):
    pltpu.make_async_copy(dst, dst, sem).wait()


def _ln0_kernel(x_ref, g_ref, b_ref, h_ref, hb_ref):
    y = _layer_norm(x_ref[...], g_ref[...], b_ref[...])
    h_ref[...] = y
    hb_ref[...] = y.astype(BF16)


def _ln0(x, g, b):
    n, d = x.shape
    row = pl.BlockSpec((LN_TM, d), lambda i: (i, 0))
    vec = pl.BlockSpec((1, d), lambda i: (0, 0))
    return pl.pallas_call(
        _ln0_kernel,
        grid=(n // LN_TM,),
        in_specs=[row, vec, vec],
        out_specs=[row, row],
        out_shape=[jax.ShapeDtypeStruct((n, d), F32), jax.ShapeDtypeStruct((n, d), BF16)],
        compiler_params=_params(("parallel",)),
        name="ln0",
    )(x, g.reshape(1, d), b.reshape(1, d))


def _inproj_kernel(hb_ref, w_ref, b_ref, o_ref, acc_s, *, dil):
    acc = jnp.dot(hb_ref[...], w_ref[...], preferred_element_type=F32) + b_ref[...]
    if dil == 1:
        for c in range(INPROJ_TN // LANES):
            o_ref[c, 0, 0] = acc[:, c * LANES:(c + 1) * LANES].astype(BF16)
    else:
        for c in range(INPROJ_TN // LANES):
            acc_s[c] = acc[:, c * LANES:(c + 1) * LANES]
        for c in range(INPROJ_TN // LANES):
            for r in range(dil):
                o_ref[c, 0, r] = acc_s[c, pl.ds(r, INPROJ_TM // dil, stride=dil), :].astype(BF16)


def _inproj(hb, w, b, bsz, s, dil, width, col_block):
    n, d = hb.shape
    tiles = s // INPROJ_TM
    assert INPROJ_TM % (16 * dil) == 0 and s % INPROJ_TM == 0 and width % INPROJ_TN == 0
    return pl.pallas_call(
        functools.partial(_inproj_kernel, dil=dil),
        grid=(n // INPROJ_TM, width // INPROJ_TN),
        in_specs=[pl.BlockSpec((INPROJ_TM, d), lambda i, j: (i, 0)),
                  pl.BlockSpec((d, INPROJ_TN), lambda i, j: (0, col_block(j))),
                  pl.BlockSpec((1, INPROJ_TN), lambda i, j: (0, col_block(j)))],
        out_specs=pl.BlockSpec((INPROJ_TN // LANES, 1, dil, INPROJ_TM // dil, LANES),
                               lambda i, j: (j, i // tiles, 0, i % tiles, 0)),
        out_shape=jax.ShapeDtypeStruct((width // LANES, bsz, dil, s // dil, LANES), BF16),
        scratch_shapes=[pltpu.VMEM((INPROJ_TN // LANES, INPROJ_TM, LANES), F32)],
        compiler_params=_params(("parallel", "arbitrary")),
        name=f"inproj_{dil}",
    )(hb, w, b)


def _na_bias_tables(rpb, rows):
    heads, n_dr, n_dc = rpb.shape
    qc = np.arange(GRID_W)[:, None]
    kc = np.arange(GRID_W)[None, :]
    qcs = np.clip(qc - NA_KW // 2, 0, GRID_W - NA_KW)
    v_col = (kc >= qcs) & (kc < qcs + NA_KW)
    dc = np.clip(kc - qc + NA_KW - 1, 0, n_dc - 1)
    onehot = (dc[None] == np.arange(n_dc)[:, None, None]) & v_col[None]
    toep = jnp.einsum('hrd,dqk->hrqk', rpb.astype(F32), jnp.asarray(onehot, F32),
                      precision=lax.Precision.HIGHEST)
    toep = jnp.where(jnp.asarray(v_col), toep, MASK_VALUE)
    masked = jnp.full((heads, 1, GRID_W, GRID_W), MASK_VALUE, F32)
    blocks = jnp.concatenate([toep, masked], axis=1)
    i = np.arange(NA_QROWS)[:, None]
    j = np.arange(NA_KROWS)[None, :]
    sel = []
    for r0, ks in ((0, 0), (2 * NA_QROWS, 2 * NA_QROWS - NA_KH // 2), (rows - NA_QROWS, rows - NA_KROWS)):
        r, krow = r0 + i, ks + j
        start = np.clip(r - NA_KH // 2, 0, rows - NA_KH)
        v_row = (krow >= start) & (krow < start + NA_KH)
        sel.append(np.where(v_row, krow - r + NA_KH - 1, n_dr))
    sel = np.stack(sel).reshape(-1)
    tab = jnp.take(blocks, jnp.asarray(sel, jnp.int32), axis=1)
    tab = tab.reshape(heads, 3, NA_QROWS, NA_KROWS, GRID_W, GRID_W).transpose(1, 0, 2, 4, 3, 5)
    return tab.reshape(3, heads, NA_QROWS * GRID_W, NA_KROWS * GRID_W)


def _softmax_attend(q, kw, vw, bias):
    s = lax.dot_general(q, kw, (((1,), (1,)), ((), ())), preferred_element_type=F32) + bias
    m = jnp.max(s, axis=-1, keepdims=True)
    p = jnp.exp(s - m)
    l = jnp.sum(p, axis=-1, keepdims=True)
    o = jnp.dot(p.astype(BF16), vw, preferred_element_type=F32) / l
    return o, m, l


def _edge_variant(blk, nblk):
    return jnp.where(blk == 0, 0, jnp.where(blk == nblk - 1, 2, 1))


def _na_kernel(q_ref, k_ref, v_ref, tab_ref, o_ref, *, rows):
    tq = NA_QROWS * GRID_W
    for u in range(NA_NSUB):
        blk = pl.program_id(2) * NA_NSUB + u
        ks = jnp.clip(blk * NA_QROWS - NA_KH // 2, 0, rows - NA_KROWS)
        off = pl.multiple_of(ks * GRID_W, GRID_W)
        kw = k_ref[0, 0, pl.ds(off, NA_KROWS * GRID_W), :]
        vw = v_ref[0, 0, pl.ds(off, NA_KROWS * GRID_W), :]
        bias = tab_ref[_edge_variant(blk, rows // NA_QROWS), 0]
        o, _, _ = _softmax_attend(q_ref[0, 0, u * tq:(u + 1) * tq, :], kw, vw, bias)
        o_ref[0, 0, u * tq:(u + 1) * tq, :] = o.astype(BF16)


def _neighborhood_attention(zq4, tabs):
    _, bsz, s, _ = zq4.shape
    rows = s // GRID_W
    assert rows % (NA_QROWS * NA_NSUB) == 0 and rows >= NA_KROWS + NA_QROWS
    tq = NA_QROWS * GRID_W * NA_NSUB
    return pl.pallas_call(
        functools.partial(_na_kernel, rows=rows),
        grid=(bsz, NA_HEADS, s // tq),
        in_specs=[pl.BlockSpec((1, 1, tq, HEAD_DIM), lambda b, h, i: (QA_H + h, b, i, 0)),
                  pl.BlockSpec((1, 1, s, HEAD_DIM), lambda b, h, i: (KA_H + h, b, 0, 0)),
                  pl.BlockSpec((1, 1, s, HEAD_DIM), lambda b, h, i: (VA_H + h, b, 0, 0)),
                  pl.BlockSpec((3, 1) + tabs.shape[2:], lambda b, h, i: (0, h, 0, 0))],
        out_specs=pl.BlockSpec((1, 1, tq, HEAD_DIM), lambda b, h, i: (h, b, i, 0)),
        out_shape=jax.ShapeDtypeStruct((NA_HEADS, bsz, s, HEAD_DIM), BF16),
        compiler_params=_params(("parallel", "parallel", "arbitrary")),
        name="na_attn",
    )(zq4, zq4, zq4, tabs)


def _alibi_slopes(n):
    return np.array([2.0 ** (-8.0 * (i + 1) / n) for i in range(n)], dtype=np.float32)


def _dil_bias_tables(slopes, dil):
    qi = np.arange(DIL_TQ)[:, None]
    kj = np.arange(DIL_TK)[None, :]
    dist = np.stack([np.abs(kj - qi + shift) for shift in (0, -DIL_HALF, -2 * DIL_HALF)])
    dist = jnp.asarray(dist, F32)[:, None]
    penalty = jnp.asarray(slopes, F32)[None, :, None, None] * (dist * float(dil))
    return jnp.where(dist <= DIL_HALF, -penalty, MASK_VALUE)


def _dil_kernel(q_ref, k_ref, v_ref, tab_ref, o_ref, lse_ref, *, length, nsub):
    for u in range(nsub):
        blk = pl.program_id(3) * nsub + u
        ks = jnp.clip(blk * DIL_TQ - DIL_HALF, 0, length - DIL_TK)
        off = pl.multiple_of(ks, DIL_HALF)
        kw = k_ref[0, 0, 0, pl.ds(off, DIL_TK), :]
        vw = v_ref[0, 0, 0, pl.ds(off, DIL_TK), :]
        bias = tab_ref[_edge_variant(blk, length // DIL_TQ), 0]
        rows = slice(u * DIL_TQ, (u + 1) * DIL_TQ)
        o, m, l = _softmax_attend(q_ref[0, 0, 0, rows, :], kw, vw, bias)
        o_ref[0, 0, 0, rows, :] = o.astype(BF16)
        lse_ref[0, 0, 0, rows, :] = jnp.broadcast_to(m + jnp.log(l), (DIL_TQ, LANES))


def _dilated_group(zq, base, g, slopes):
    window, dil = DIL_GROUPS[g]
    assert (window // 2) // dil == DIL_HALF
    _, bsz, _, length, _ = zq.shape
    nsub = min(DIL_NSUB, length // DIL_TQ)
    assert zq.shape[2] == dil and length % (DIL_TQ * nsub) == 0 and length >= DIL_TK
    gh = DIL_HEADS_PER_GROUP
    tabs = _dil_bias_tables(slopes[g * gh:(g + 1) * gh], dil)
    tq = DIL_TQ * nsub
    qspec = pl.BlockSpec((1, 1, 1, tq, HEAD_DIM), lambda b, h, r, n: (base + h, b, r, n, 0))
    kvspec = lambda off: pl.BlockSpec((1, 1, 1, length, HEAD_DIM),
                                      lambda b, h, r, n: (base + off + h, b, r, 0, 0))
    ospec = pl.BlockSpec((1, 1, 1, tq, HEAD_DIM), lambda b, h, r, n: (h, b, r, n, 0))
    return pl.pallas_call(
        functools.partial(_dil_kernel, length=length, nsub=nsub),
        grid=(bsz, gh, dil, length // tq),
        in_specs=[qspec, kvspec(gh), kvspec(2 * gh),
                  pl.BlockSpec((3, 1, DIL_TQ, DIL_TK), lambda b, h, r, n: (0, h, 0, 0))],
        out_specs=[ospec, ospec],
        out_shape=[jax.ShapeDtypeStruct((gh, bsz, dil, length, HEAD_DIM), BF16),
                   jax.ShapeDtypeStruct((gh, bsz, dil, length, HEAD_DIM), F32)],
        compiler_params=_params(("parallel", "parallel", "parallel", "arbitrary")),
        name=f"dil_attn_{dil}",
    )(zq, zq, zq, tabs)


def _mix_kernel(na_ref, o1_ref, o2_ref, o3_ref, l1_ref, l2_ref, l3_ref, hb_ref,
                wpa_ref, wpb_ref, wga_ref, wgb_ref, bga_ref, bgb_ref, m_ref, na_s, dil_s, o_s, l_s):
    @pl.when(pl.program_id(1) == 0)
    def _():
        for h in range(NA_HEADS):
            na_s[:, h * HEAD_DIM:(h + 1) * HEAD_DIM] = na_ref[h]
        for g, (o_ref, l_ref) in enumerate(((o1_ref, l1_ref), (o2_ref, l2_ref), (o3_ref, l3_ref))):
            dil = DIL_GROUPS[g][1]
            for h in range(DIL_HEADS_PER_GROUP):
                for r in range(dil):
                    rows = slice(None) if dil == 1 else pl.ds(r, MIX_TM // dil, stride=dil)
                    o_s[g, h, rows, :] = o_ref[h, 0, r].astype(F32)
                    l_s[g, h, rows, :] = l_ref[h, 0, r]
        for h in range(DIL_HEADS_PER_GROUP):
            ls = [l_s[g, h] for g in range(len(DIL_GROUPS))]
            mx = jnp.maximum(jnp.maximum(ls[0], ls[1]), ls[2])
            es = [jnp.exp(l - mx) for l in ls]
            den = es[0] + es[1] + es[2]
            acc = (es[0] * o_s[0, h] + es[1] * o_s[1, h] + es[2] * o_s[2, h]) / den
            dil_s[:, h * HEAD_DIM:(h + 1) * HEAD_DIM] = acc.astype(BF16)

    hb = hb_ref[...]
    ya = jnp.dot(na_s[...], wpa_ref[...], preferred_element_type=F32)
    yb = jnp.dot(dil_s[...], wpb_ref[...], preferred_element_type=F32)
    ga = jnp.dot(hb, wga_ref[...], preferred_element_type=F32) + bga_ref[...]
    gb = jnp.dot(hb, wgb_ref[...], preferred_element_type=F32) + bgb_ref[...]
    m_ref[...] = (jax.nn.sigmoid(ga) * ya + jax.nn.sigmoid(gb) * yb).astype(BF16)


def _mix(na, dil_o, dil_lse, hb, wpa, wpb, w_all, b_all, s):
    n, d = hb.shape
    tm, tn = MIX_TM, MIX_TN
    tiles = s // tm
    gh = DIL_HEADS_PER_GROUP
    ngroups = len(DIL_GROUPS)
    assert all(tm % (16 * dil) == 0 for _, dil in DIL_GROUPS) and s % tm == 0
    col = lambda k: pl.BlockSpec((k, tn), lambda i, j: (0, j))
    ga0, gb0 = QKV_WIDTH // tn, (QKV_WIDTH + d) // tn
    gate = lambda k, first: pl.BlockSpec((k, tn), lambda i, j: (0, first + j))
    grp = lambda dil: pl.BlockSpec((gh, 1, dil, tm // dil, HEAD_DIM),
                                   lambda i, j: (0, i // tiles, 0, i % tiles, 0))
    groups = [grp(dil) for _, dil in DIL_GROUPS]
    return pl.pallas_call(
        _mix_kernel,
        grid=(n // tm, d // tn),
        in_specs=[pl.BlockSpec((NA_HEADS, tm, HEAD_DIM), lambda i, j: (0, i, 0)), *groups, *groups,
                  pl.BlockSpec((tm, d), lambda i, j: (i, 0)),
                  col(NA_WIDTH), col(DIL_OUT_WIDTH),
                  gate(d, ga0), gate(d, gb0), gate(1, ga0), gate(1, gb0)],
        out_specs=pl.BlockSpec((tm, tn), lambda i, j: (i, j)),
        out_shape=jax.ShapeDtypeStruct((n, d), BF16),
        scratch_shapes=[pltpu.VMEM((tm, NA_WIDTH), BF16), pltpu.VMEM((tm, DIL_OUT_WIDTH), BF16),
                        pltpu.VMEM((ngroups, gh, tm, HEAD_DIM), F32),
                        pltpu.VMEM((ngroups, gh, tm, HEAD_DIM), F32)],
        compiler_params=_params(("parallel", "arbitrary")),
        name="mix",
    )(na, *dil_o, *dil_lse, hb, wpa, wpb, w_all, w_all, b_all, b_all)


def _outproj_kernel(m_ref, h_ref, wo_ref, bo_ref, g_ref, b_ref, wrh_ref, wrl_ref, br_ref,
                    h1_ref, h1p_hbm, lg_ref, hbuf, sem):
    i = pl.program_id(0)
    slot = i % 2

    def out_copies(step, sl):
        return _tile_copies(h1p_hbm, step * OUT_TM, OUT_TM, hbuf.at[sl], sem.at[sl], True)

    mix = jnp.dot(m_ref[...], wo_ref[...], preferred_element_type=F32) + bo_ref[...]
    h1 = _layer_norm(DN_ALPHA * h_ref[...] + mix, g_ref[...], b_ref[...])
    h1_ref[...] = h1

    @pl.when(i >= 2)
    def _():
        for cp in out_copies(i - 2, slot):
            cp.wait()

    _pack_rows(hbuf.at[slot], h1)
    for cp in out_copies(i, slot):
        cp.start()

    @pl.when(i == pl.num_programs(0) - 1)
    def _():
        for cp in out_copies(i, slot):
            cp.wait()

        @pl.when(i >= 1)
        def _():
            for cp in out_copies(i - 1, 1 - slot):
                cp.wait()

    hi = h1.astype(BF16)
    lo = (h1 - hi.astype(F32)).astype(BF16)
    lg = (jnp.dot(hi, wrh_ref[...], preferred_element_type=F32)
          + jnp.dot(lo, wrh_ref[...], preferred_element_type=F32)
          + jnp.dot(hi, wrl_ref[...], preferred_element_type=F32))
    lg_ref[...] = lg + br_ref[...]


def _outproj(m, h, wo, bo, g, b, wr_hi, wr_lo, br):
    n, d = h.shape
    tm = OUT_TM
    row = lambda w: pl.BlockSpec((tm, w), lambda i: (i, 0))
    full = lambda r, c: pl.BlockSpec((r, c), lambda i: (0, 0))
    return pl.pallas_call(
        _outproj_kernel,
        grid=(n // tm,),
        in_specs=[row(d), row(d), full(d, d), full(1, d), full(1, d), full(1, d),
                  full(d, ROUTER_PAD), full(d, ROUTER_PAD), full(1, ROUTER_PAD)],
        out_specs=[row(d), pl.BlockSpec(memory_space=pl.ANY), row(ROUTER_PAD)],
        out_shape=[jax.ShapeDtypeStruct((n, d), F32),
                   jax.ShapeDtypeStruct((n, PACKED, LANES), jnp.uint32),
                   jax.ShapeDtypeStruct((n, ROUTER_PAD), F32)],
        scratch_shapes=[pltpu.VMEM((2, PACKED, tm, LANES), jnp.uint32), pltpu.SemaphoreType.DMA((2,))],
        compiler_params=_params(("arbitrary",)),
        name="outproj",
    )(m, h, wo, bo.reshape(1, d), g.reshape(1, d), b.reshape(1, d), wr_hi, wr_lo, br)


def _first_argmax(vals, lane_f):
    top = jnp.max(vals, axis=-1, keepdims=True)
    idx = jnp.min(jnp.where(vals == top, lane_f, float(LANES)), axis=-1, keepdims=True)
    return top, idx


def _route_kernel(lg_ref, info_ref, gate_ref, cnt_ref, base_s):
    @pl.when(pl.program_id(0) == 0)
    def _():
        base_s[...] = jnp.zeros(base_s.shape, F32)

    lg = lg_ref[...]
    lane = lax.broadcasted_iota(jnp.int32, lg.shape, 1)
    lane_f = lane.astype(F32)
    g_mask = lane < N_GROUPS
    g_top, g_sel = _first_argmax(jnp.where(g_mask, lg, MASK_VALUE), lane_f)
    g_prob = 1.0 / jnp.sum(jnp.where(g_mask, jnp.exp(lg - g_top), 0.0), axis=-1, keepdims=True)
    first = N_GROUPS + g_sel * EXPERTS_PER_GROUP
    e_mask = (lane_f >= first) & (lane_f < first + EXPERTS_PER_GROUP)
    el = jnp.where(e_mask, lg, MASK_VALUE)
    v0, i0 = _first_argmax(el, lane_f)
    v1, i1 = _first_argmax(jnp.where(lane_f == i0, MASK_VALUE, el), lane_f)
    e1 = jnp.exp(v1 - v0)
    w0 = g_prob / (1.0 + e1)
    w1 = g_prob * e1 / (1.0 + e1)

    tm = lg.shape[0]
    tri = (lax.broadcasted_iota(jnp.int32, (tm, tm), 1)
           < lax.broadcasted_iota(jnp.int32, (tm, tm), 0)).astype(BF16)
    ranks = []
    for idx in (i0, i1):
        onehot = lane_f == idx - N_GROUPS
        before = jnp.dot(tri, onehot.astype(BF16), preferred_element_type=F32) + base_s[...]
        ranks.append(jnp.sum(jnp.where(onehot, before, 0.0), axis=-1, keepdims=True))
        base_s[...] = base_s[...] + jnp.sum(onehot.astype(F32), axis=0, keepdims=True)

    info = jnp.where(lane == 0, i0 - N_GROUPS,
                     jnp.where(lane == 1, i1 - N_GROUPS,
                               jnp.where(lane == 2, ranks[0], jnp.where(lane == 3, ranks[1], 0.0))))
    info_ref[...] = info.astype(jnp.int32)
    gate_ref[...] = jnp.where(lane == 0, w0, jnp.where(lane == 1, w1, 0.0))
    cnt_ref[...] = base_s[...]


def _route(logits, n):
    tm = ROUTE_TM
    row = pl.BlockSpec((tm, ROUTER_PAD), lambda i: (i, 0))
    one = pl.BlockSpec((1, ROUTER_PAD), lambda i: (0, 0))
    info, gate, counts = pl.pallas_call(
        _route_kernel,
        grid=(n // tm,),
        in_specs=[row],
        out_specs=[row, row, one],
        out_shape=[jax.ShapeDtypeStruct((n, ROUTER_PAD), jnp.int32),
                   jax.ShapeDtypeStruct((n, ROUTER_PAD), F32),
                   jax.ShapeDtypeStruct((1, ROUTER_PAD), F32)],
        scratch_shapes=[pltpu.VMEM((1, ROUTER_PAD), F32)],
        compiler_params=_params(("arbitrary",)),
        name="route",
    )(logits)
    gate = gate[:, :TOP_K]
    a = n * TOP_K
    e_flat = info[:, :TOP_K].reshape(a)
    rank = info[:, TOP_K:2 * TOP_K].reshape(a)
    counts = counts[0, :N_EXPERTS].astype(jnp.int32)
    pcounts = (counts + MOE_TB - 1) // MOE_TB * MOE_TB
    pends = jnp.cumsum(pcounts)
    pstarts = pends - pcounts
    dest = (pstarts[e_flat] + rank).astype(jnp.int32)
    nb = a // MOE_TB + N_EXPERTS
    slots = nb * MOE_TB
    tok = jnp.arange(a, dtype=jnp.int32) // TOP_K
    slot_tok = jnp.zeros((slots,), jnp.int32).at[dest].set(tok)
    block_start = jnp.arange(nb, dtype=jnp.int32) * MOE_TB
    block_e = jnp.minimum(jnp.sum((pends[None, :] <= block_start[:, None]).astype(jnp.int32), axis=1),
                          N_EXPERTS - 1)
    n_active = (pends[-1] // MOE_TB).astype(jnp.int32).reshape(1)
    ids = jnp.arange(N_EXPERTS, dtype=jnp.int32)
    later = (counts[None, :] > 0) & (ids[None, :] > ids[:, None])
    nxt = jnp.min(jnp.where(later, ids[None, :], N_EXPERTS), axis=1)
    block_next = jnp.where(nxt == N_EXPERTS, -1, nxt)[block_e].astype(jnp.int32)
    return dest, gate, slot_tok, block_e, block_next, n_active


def _expert_kernel(be_ref, nx_ref, nact_ref, tok_ref, h1p_hbm, wg_hbm, wu_hbm, wd_hbm, ys_hbm,
                   xbuf, ybuf, wg_f, wu_f, wd_f, wg_s, wu_s, wd_s, gsem, osem, wsem):
    i = pl.program_id(0)
    slot = i % 2
    nact = nact_ref[0]

    def gather(step, sl):
        _start_row_gather(tok_ref, step * MOE_TB, 1, MOE_TB, h1p_hbm, xbuf.at[sl], gsem.at[sl], False)

    def out_copies(step, sl):
        return _tile_copies(ys_hbm, step * MOE_TB, MOE_TB, ybuf.at[sl], osem.at[sl], True)

    def weight_copies(e):
        return [pltpu.make_async_copy(src.at[e], dst, wsem)
                for src, dst in ((wg_hbm, wg_f), (wu_hbm, wu_f), (wd_hbm, wd_f))]

    @pl.when((i == 0) & (nact > 0))
    def _():
        for cp in weight_copies(be_ref[0]):
            cp.start(priority=WEIGHT_DMA_PRIORITY)
        gather(0, 0)

    @pl.when(i + 1 < nact)
    def _():
        gather(i + 1, 1 - slot)

    @pl.when(i >= 2)
    def _():
        for cp in out_copies(i - 2, slot):
            cp.wait()

    @pl.when(i < nact)
    def _():
        expert = be_ref[i]

        @pl.when((i == 0) | (expert != be_ref[jnp.maximum(i - 1, 0)]))
        def _():
            for cp in weight_copies(expert):
                cp.wait()
            wg_s[...] = wg_f[...].astype(BF16)
            wu_s[...] = wu_f[...].astype(BF16)
            wd_s[...] = wd_f[...].astype(BF16)

            @pl.when(nx_ref[i] >= 0)
            def _():
                for cp in weight_copies(nx_ref[i]):
                    cp.start(priority=WEIGHT_DMA_PRIORITY)

        _wait_row_gather(xbuf.at[slot], gsem.at[slot])
        xb = _unpack_rows(xbuf.at[slot]).astype(BF16)
        gate = jnp.dot(xb, wg_s[...], preferred_element_type=F32)
        up = jnp.dot(xb, wu_s[...], preferred_element_type=F32)
        hid = (jax.nn.silu(gate) * up).astype(BF16)
        _pack_rows(ybuf.at[slot], jnp.dot(hid, wd_s[...], preferred_element_type=F32))

    @pl.when(i >= nact)
    def _():
        ybuf[slot] = jnp.zeros(ybuf.shape[1:], jnp.uint32)

    for cp in out_copies(i, slot):
        cp.start()

    @pl.when(i == pl.num_programs(0) - 1)
    def _():
        for cp in out_copies(i, slot):
            cp.wait()

        @pl.when(i >= 1)
        def _():
            for cp in out_copies(i - 1, 1 - slot):
                cp.wait()


def _experts(h1p, slot_tok, block_e, block_next, n_active, w_gate, w_up, w_down):
    d = D_MODEL
    slots = slot_tok.shape[0]
    nb = slots // MOE_TB
    any_spec = pl.BlockSpec(memory_space=pl.ANY)
    grid_spec = pltpu.PrefetchScalarGridSpec(
        num_scalar_prefetch=4,
        grid=(nb,),
        in_specs=[any_spec, any_spec, any_spec, any_spec],
        out_specs=any_spec,
        scratch_shapes=[pltpu.VMEM((2, PACKED, MOE_TB, LANES), jnp.uint32),
                        pltpu.VMEM((2, PACKED, MOE_TB, LANES), jnp.uint32),
                        pltpu.VMEM((d, D_EXPERT), F32), pltpu.VMEM((d, D_EXPERT), F32),
                        pltpu.VMEM((D_EXPERT, d), F32),
                        pltpu.VMEM((d, D_EXPERT), BF16), pltpu.VMEM((d, D_EXPERT), BF16),
                        pltpu.VMEM((D_EXPERT, d), BF16),
                        pltpu.SemaphoreType.DMA((2,)), pltpu.SemaphoreType.DMA((2,)),
                        pltpu.SemaphoreType.DMA(())])
    return pl.pallas_call(
        _expert_kernel,
        grid_spec=grid_spec,
        out_shape=jax.ShapeDtypeStruct((slots, PACKED, LANES), jnp.uint32),
        compiler_params=_params(("arbitrary",)),
        name="experts",
    )(block_e, block_next, n_active, slot_tok, h1p, w_gate, w_up, w_down)


def _final_kernel(dest_ref, ys_hbm, h1_ref, gate_ref, g_ref, b_ref, o_ref, ybuf, gsem):
    i = pl.program_id(0)
    slot = i % 2

    def fetch(step, sl):
        for k in range(TOP_K):
            _start_row_gather(dest_ref, step * FIN_TM * TOP_K + k, TOP_K, FIN_TM, ys_hbm,
                              ybuf.at[sl, k], gsem.at[sl], True)

    @pl.when(i == 0)
    def _():
        fetch(0, 0)

    @pl.when(i + 1 < pl.num_programs(0))
    def _():
        fetch(i + 1, 1 - slot)

    _wait_row_gather(ybuf.at[slot], gsem.at[slot])
    gate = gate_ref[...]
    ffn = (_unpack_rows(ybuf.at[slot, 0]) * gate[:, 0:1]
           + _unpack_rows(ybuf.at[slot, 1]) * gate[:, 1:2])
    o_ref[...] = _layer_norm(DN_ALPHA * h1_ref[...] + ffn, g_ref[...], b_ref[...])


def _final(dest, ys, h1, gate, g, b):
    n, d = h1.shape
    grid_spec = pltpu.PrefetchScalarGridSpec(
        num_scalar_prefetch=1,
        grid=(n // FIN_TM,),
        in_specs=[pl.BlockSpec(memory_space=pl.ANY),
                  pl.BlockSpec((FIN_TM, d), lambda i, ds: (i, 0)),
                  pl.BlockSpec((FIN_TM, TOP_K), lambda i, ds: (i, 0)),
                  pl.BlockSpec((1, d), lambda i, ds: (0, 0)),
                  pl.BlockSpec((1, d), lambda i, ds: (0, 0))],
        out_specs=pl.BlockSpec((FIN_TM, d), lambda i, ds: (i, 0)),
        scratch_shapes=[pltpu.VMEM((2, TOP_K, PACKED, FIN_TM, LANES), jnp.uint32),
                        pltpu.SemaphoreType.DMA((2,))])
    return pl.pallas_call(
        _final_kernel,
        grid_spec=grid_spec,
        out_shape=jax.ShapeDtypeStruct((n, d), F32),
        compiler_params=_params(("arbitrary",)),
        name="final",
    )(dest, ys, h1, gate, g.reshape(1, d), b.reshape(1, d))


def kernel(x, ln0_g, ln0_b, w_in, b_in, rpb, w_proj_a, w_proj_b, w_o, b_o, ln1_g, ln1_b,
           w_router_group, b_router_group, w_router_expert, b_router_expert,
           w_gate, w_up, w_down, ln2_g, ln2_b):
    bsz, s, d = x.shape
    n = bsz * s
    assert d == D_MODEL and w_in.shape[0] == DEPTH
    scale = HEAD_DIM ** -0.5

    col_scale = np.ones((w_in.shape[2],), np.float32)
    col_scale[:NA_WIDTH] = scale
    col_scale[3 * NA_WIDTH:3 * NA_WIDTH + DIL_WIDTH] = scale
    w_all = (w_in[0] * col_scale).astype(BF16)
    b_all = (b_in[0] * col_scale).reshape(1, -1)
    assert DIL_OUT_WIDTH == INPROJ_TN
    na_blocks, ngroups = 3 * NA_WIDTH // INPROJ_TN, len(DIL_GROUPS)
    w_r = jnp.concatenate([w_router_group[0], w_router_expert[0]], axis=1)
    w_r = jnp.pad(w_r, ((0, 0), (0, ROUTER_PAD - w_r.shape[1])))
    w_r_hi = w_r.astype(BF16)
    w_r_lo = (w_r - w_r_hi.astype(F32)).astype(BF16)
    b_r = jnp.pad(jnp.concatenate([b_router_group[0], b_router_expert[0]]),
                  (0, ROUTER_PAD - N_GROUPS - N_EXPERTS)).reshape(1, ROUTER_PAD)
    tabs = _na_bias_tables(rpb[0], s // GRID_W)
    slopes = _alibi_slopes(DIL_HEADS)

    h, hb = _ln0(x.reshape(n, d), ln0_g, ln0_b)
    zq = [_inproj(hb, w_all, b_all, bsz, s, DIL_GROUPS[0][1], 3 * NA_WIDTH + 3 * DIL_OUT_WIDTH,
                  lambda j: jnp.where(j < na_blocks, j, na_blocks + (j - na_blocks) * ngroups))]
    for g in range(1, ngroups):
        zq.append(_inproj(hb, w_all, b_all, bsz, s, DIL_GROUPS[g][1], 3 * DIL_OUT_WIDTH,
                          lambda j, g=g: na_blocks + j * ngroups + g))
    na = _neighborhood_attention(zq[0].reshape(-1, bsz, s, HEAD_DIM), tabs)
    na = na.reshape(NA_HEADS, n, HEAD_DIM)
    dil = [_dilated_group(zq[g], DIL0_H if g == 0 else 0, g, slopes) for g in range(ngroups)]
    m = _mix(na, [o for o, _ in dil], [l for _, l in dil], hb,
             w_proj_a[0].astype(BF16), w_proj_b[0].astype(BF16), w_all, b_all, s)
    h1, h1p, logits = _outproj(m, h, w_o[0].astype(BF16), b_o[0], ln1_g[0], ln1_b[0],
                               w_r_hi, w_r_lo, b_r)
    dest, gate, slot_tok, block_e, block_next, n_active = _route(logits, n)
    ys = _experts(h1p, slot_tok, block_e, block_next, n_active, w_gate[0], w_up[0], w_down[0])
    out = _final(dest, ys, h1, gate, ln2_g[0], ln2_b[0])
    return out.reshape(bsz, s, d)
```

```python
import functools

import numpy as np
import jax
import jax.numpy as jnp
from jax import lax
from jax.experimental import pallas as pl
from jax.experimental.pallas import tpu as pltpu

F32 = jnp.float32
BF16 = jnp.bfloat16

D_MODEL = 2048
HEAD_DIM = 128
GRID_W = 64
NA_HEADS = 8
NA_KH = 8
NA_KW = 16
DIL_GROUPS = ((128, 1), (512, 4), (2048, 16))
DIL_HEADS_PER_GROUP = 4
DIL_HEADS = DIL_HEADS_PER_GROUP * len(DIL_GROUPS)
N_GROUPS = 8
EXPERTS_PER_GROUP = 8
N_EXPERTS = N_GROUPS * EXPERTS_PER_GROUP
TOP_K = 2
D_EXPERT = D_MODEL // 4
LN_EPS = 1e-5
DEPTH = 1
DN_ALPHA = (2 * DEPTH) ** 0.25
NA_WIDTH = NA_HEADS * HEAD_DIM
DIL_WIDTH = DIL_HEADS * HEAD_DIM
DIL_OUT_WIDTH = DIL_HEADS_PER_GROUP * HEAD_DIM
QKV_WIDTH = 3 * NA_WIDTH + 3 * DIL_WIDTH
QKV_HEADS = QKV_WIDTH // HEAD_DIM
LANES = 128
CHUNKS = D_MODEL // LANES
PACKED = CHUNKS // 2
MASK_VALUE = -1e30

QA_H, KA_H, VA_H = 0, NA_HEADS, 2 * NA_HEADS
DIL0_H = 3 * NA_HEADS

LN_TM = 512
INPROJ_TM, INPROJ_TN = 1024, 512
NA_QROWS = 4
NA_KROWS = 12
NA_NSUB = 4
DIL_NSUB = 4
DIL_TQ = 256
DIL_HALF = 64
DIL_TK = DIL_TQ + 2 * DIL_HALF
MIX_TM, MIX_TN = 512, 512
OUT_TM = 256
ROUTER_PAD = 128
ROUTE_TM = 512
MOE_TB = 128
FIN_TM = 128
WEIGHT_DMA_PRIORITY = 1
VMEM_LIMIT = 56 * 1024 * 1024


def _params(sem, limit=VMEM_LIMIT):
    return pltpu.CompilerParams(dimension_semantics=sem, vmem_limit_bytes=limit)


def _layer_norm(x, g, b):
    mu = jnp.mean(x, axis=-1, keepdims=True)
    xc = x - mu
    var = jnp.mean(xc * xc, axis=-1, keepdims=True)
    return xc * lax.rsqrt(var + LN_EPS) * g + b


HIGH_HALF = np.uint32(0xFFFF0000)


def _bf16_bits(x):
    return lax.bitcast_convert_type(x.astype(BF16).astype(F32), jnp.uint32)


def _pack_rows(ref, val):
    for c in range(PACKED):
        lo = _bf16_bits(val[:, c * LANES:(c + 1) * LANES])
        hi = _bf16_bits(val[:, (c + PACKED) * LANES:(c + PACKED + 1) * LANES])
        ref[c] = (lo >> 16) | (hi & HIGH_HALF)


def _unpack_rows(ref, row_major=False):
    words = [ref[:, c, :] if row_major else ref[c] for c in range(PACKED)]
    lo = [lax.bitcast_convert_type(w << 16, F32) for w in words]
    hi = [lax.bitcast_convert_type(w & HIGH_HALF, F32) for w in words]
    return jnp.concatenate(lo + hi, axis=-1)


def _tile_copies(hbm, row0, rows, vmem, sem, to_hbm):
    copies = []
    for c in range(PACKED):
        h, v = hbm.at[pl.ds(row0, rows), c], vmem.at[c]
        copies.append(pltpu.make_async_copy(v, h, sem) if to_hbm else pltpu.make_async_copy(h, v, sem))
    return copies


def _start_row_gather(idx_ref, base, step, count, src_hbm, dst, sem, both_threads):
    def start(j, c):
        for half in range(2):
            t = 2 * j + half
            pltpu.make_async_copy(src_hbm.at[idx_ref[base + t * step]], dst.at[t], sem).start(
                priority=half if both_threads else 0)
        return c

    lax.fori_loop(0, count // 2, start, 0, unroll=2)


def _wait_row_gather(dst, sem):
    pltpu.make_async_copy(dst, dst, sem).wait()


def _ln0_kernel(x_ref, g_ref, b_ref, h_ref, hb_ref):
    y = _layer_norm(x_ref[...], g_ref[...], b_ref[...])
    h_ref[...] = y
    hb_ref[...] = y.astype(BF16)


def _ln0(x, g, b):
    n, d = x.shape
    row = pl.BlockSpec((LN_TM, d), lambda i: (i, 0))
    vec = pl.BlockSpec((1, d), lambda i: (0, 0))
    return pl.pallas_call(
        _ln0_kernel,
        grid=(n // LN_TM,),
        in_specs=[row, vec, vec],
        out_specs=[row, row],
        out_shape=[jax.ShapeDtypeStruct((n, d), F32), jax.ShapeDtypeStruct((n, d), BF16)],
        compiler_params=_params(("parallel",)),
        name="ln0",
    )(x, g.reshape(1, d), b.reshape(1, d))


def _inproj_kernel(hb_ref, w_ref, b_ref, o_ref, acc_s, *, dil):
    acc = jnp.dot(hb_ref[...], w_ref[...], preferred_element_type=F32) + b_ref[...]
    if dil == 1:
        for c in range(INPROJ_TN // LANES):
            o_ref[c, 0, 0] = acc[:, c * LANES:(c + 1) * LANES].astype(BF16)
    else:
        for c in range(INPROJ_TN // LANES):
            acc_s[c] = acc[:, c * LANES:(c + 1) * LANES]
        for c in range(INPROJ_TN // LANES):
            for r in range(dil):
                o_ref[c, 0, r] = acc_s[c, pl.ds(r, INPROJ_TM // dil, stride=dil), :].astype(BF16)


def _inproj(hb, w, b, bsz, s, dil, width, col_block):
    n, d = hb.shape
    tiles = s // INPROJ_TM
    assert INPROJ_TM % (16 * dil) == 0 and s % INPROJ_TM == 0 and width % INPROJ_TN == 0
    return pl.pallas_call(
        functools.partial(_inproj_kernel, dil=dil),
        grid=(n // INPROJ_TM, width // INPROJ_TN),
        in_specs=[pl.BlockSpec((INPROJ_TM, d), lambda i, j: (i, 0)),
                  pl.BlockSpec((d, INPROJ_TN), lambda i, j: (0, col_block(j))),
                  pl.BlockSpec((1, INPROJ_TN), lambda i, j: (0, col_block(j)))],
        out_specs=pl.BlockSpec((INPROJ_TN // LANES, 1, dil, INPROJ_TM // dil, LANES),
                               lambda i, j: (j, i // tiles, 0, i % tiles, 0)),
        out_shape=jax.ShapeDtypeStruct((width // LANES, bsz, dil, s // dil, LANES), BF16),
        scratch_shapes=[pltpu.VMEM((INPROJ_TN // LANES, INPROJ_TM, LANES), F32)],
        compiler_params=_params(("parallel", "arbitrary")),
        name=f"inproj_{dil}",
    )(hb, w, b)


def _na_bias_tables(rpb, rows):
    heads, n_dr, n_dc = rpb.shape
    qc = np.arange(GRID_W)[:, None]
    kc = np.arange(GRID_W)[None, :]
    qcs = np.clip(qc - NA_KW // 2, 0, GRID_W - NA_KW)
    v_col = (kc >= qcs) & (kc < qcs + NA_KW)
    dc = np.clip(kc - qc + NA_KW - 1, 0, n_dc - 1)
    onehot = (dc[None] == np.arange(n_dc)[:, None, None]) & v_col[None]
    toep = jnp.einsum('hrd,dqk->hrqk', rpb.astype(F32), jnp.asarray(onehot, F32),
                      precision=lax.Precision.HIGHEST)
    toep = jnp.where(jnp.asarray(v_col), toep, MASK_VALUE)
    masked = jnp.full((heads, 1, GRID_W, GRID_W), MASK_VALUE, F32)
    blocks = jnp.concatenate([toep, masked], axis=1)
    i = np.arange(NA_QROWS)[:, None]
    j = np.arange(NA_KROWS)[None, :]
    sel = []
    for r0, ks in ((0, 0), (2 * NA_QROWS, 2 * NA_QROWS - NA_KH // 2), (rows - NA_QROWS, rows - NA_KROWS)):
        r, krow = r0 + i, ks + j
        start = np.clip(r - NA_KH // 2, 0, rows - NA_KH)
        v_row = (krow >= start) & (krow < start + NA_KH)
        sel.append(np.where(v_row, krow - r + NA_KH - 1, n_dr))
    sel = np.stack(sel).reshape(-1)
    tab = jnp.take(blocks, jnp.asarray(sel, jnp.int32), axis=1)
    tab = tab.reshape(heads, 3, NA_QROWS, NA_KROWS, GRID_W, GRID_W).transpose(1, 0, 2, 4, 3, 5)
    return tab.reshape(3, heads, NA_QROWS * GRID_W, NA_KROWS * GRID_W)


def _softmax_attend(q, kw, vw, bias):
    s = lax.dot_general(q, kw, (((1,), (1,)), ((), ())), preferred_element_type=F32) + bias
    m = jnp.max(s, axis=-1, keepdims=True)
    p = jnp.exp(s - m)
    l = jnp.sum(p, axis=-1, keepdims=True)
    o = jnp.dot(p.astype(BF16), vw, preferred_element_type=F32) / l
    return o, m, l


def _edge_variant(blk, nblk):
    return jnp.where(blk == 0, 0, jnp.where(blk == nblk - 1, 2, 1))


def _na_kernel(q_ref, k_ref, v_ref, tab_ref, o_ref, *, rows):
    tq = NA_QROWS * GRID_W
    for u in range(NA_NSUB):
        blk = pl.program_id(2) * NA_NSUB + u
        ks = jnp.clip(blk * NA_QROWS - NA_KH // 2, 0, rows - NA_KROWS)
        off = pl.multiple_of(ks * GRID_W, GRID_W)
        kw = k_ref[0, 0, pl.ds(off, NA_KROWS * GRID_W), :]
        vw = v_ref[0, 0, pl.ds(off, NA_KROWS * GRID_W), :]
        bias = tab_ref[_edge_variant(blk, rows // NA_QROWS), 0]
        o, _, _ = _softmax_attend(q_ref[0, 0, u * tq:(u + 1) * tq, :], kw, vw, bias)
        o_ref[0, 0, u * tq:(u + 1) * tq, :] = o.astype(BF16)


def _neighborhood_attention(zq4, tabs):
    _, bsz, s, _ = zq4.shape
    rows = s // GRID_W
    assert rows % (NA_QROWS * NA_NSUB) == 0 and rows >= NA_KROWS + NA_QROWS
    tq = NA_QROWS * GRID_W * NA_NSUB
    return pl.pallas_call(
        functools.partial(_na_kernel, rows=rows),
        grid=(bsz, NA_HEADS, s // tq),
        in_specs=[pl.BlockSpec((1, 1, tq, HEAD_DIM), lambda b, h, i: (QA_H + h, b, i, 0)),
                  pl.BlockSpec((1, 1, s, HEAD_DIM), lambda b, h, i: (KA_H + h, b, 0, 0)),
                  pl.BlockSpec((1, 1, s, HEAD_DIM), lambda b, h, i: (VA_H + h, b, 0, 0)),
                  pl.BlockSpec((3, 1) + tabs.shape[2:], lambda b, h, i: (0, h, 0, 0))],
        out_specs=pl.BlockSpec((1, 1, tq, HEAD_DIM), lambda b, h, i: (h, b, i, 0)),
        out_shape=jax.ShapeDtypeStruct((NA_HEADS, bsz, s, HEAD_DIM), BF16),
        compiler_params=_params(("parallel", "parallel", "arbitrary")),
        name="na_attn",
    )(zq4, zq4, zq4, tabs)


def _alibi_slopes(n):
    return np.array([2.0 ** (-8.0 * (i + 1) / n) for i in range(n)], dtype=np.float32)


def _dil_bias_tables(slopes, dil):
    qi = np.arange(DIL_TQ)[:, None]
    kj = np.arange(DIL_TK)[None, :]
    dist = np.stack([np.abs(kj - qi + shift) for shift in (0, -DIL_HALF, -2 * DIL_HALF)])
    dist = jnp.asarray(dist, F32)[:, None]
    penalty = jnp.asarray(slopes, F32)[None, :, None, None] * (dist * float(dil))
    return jnp.where(dist <= DIL_HALF, -penalty, MASK_VALUE)


def _dil_kernel(q_ref, k_ref, v_ref, tab_ref, o_ref, lse_ref, *, length, nsub):
    for u in range(nsub):
        blk = pl.program_id(3) * nsub + u
        ks = jnp.clip(blk * DIL_TQ - DIL_HALF, 0, length - DIL_TK)
        off = pl.multiple_of(ks, DIL_HALF)
        kw = k_ref[0, 0, 0, pl.ds(off, DIL_TK), :]
        vw = v_ref[0, 0, 0, pl.ds(off, DIL_TK), :]
        bias = tab_ref[_edge_variant(blk, length // DIL_TQ), 0]
        rows = slice(u * DIL_TQ, (u + 1) * DIL_TQ)
        o, m, l = _softmax_attend(q_ref[0, 0, 0, rows, :], kw, vw, bias)
        o_ref[0, 0, 0, rows, :] = o.astype(BF16)
        lse_ref[0, 0, 0, rows, :] = jnp.broadcast_to(m + jnp.log(l), (DIL_TQ, LANES))


def _dilated_group(zq, base, g, slopes):
    window, dil = DIL_GROUPS[g]
    assert (window // 2) // dil == DIL_HALF
    _, bsz, _, length, _ = zq.shape
    nsub = min(DIL_NSUB, length // DIL_TQ)
    assert zq.shape[2] == dil and length % (DIL_TQ * nsub) == 0 and length >= DIL_TK
    gh = DIL_HEADS_PER_GROUP
    tabs = _dil_bias_tables(slopes[g * gh:(g + 1) * gh], dil)
    tq = DIL_TQ * nsub
    qspec = pl.BlockSpec((1, 1, 1, tq, HEAD_DIM), lambda b, h, r, n: (base + h, b, r, n, 0))
    kvspec = lambda off: pl.BlockSpec((1, 1, 1, length, HEAD_DIM),
                                      lambda b, h, r, n: (base + off + h, b, r, 0, 0))
    ospec = pl.BlockSpec((1, 1, 1, tq, HEAD_DIM), lambda b, h, r, n: (h, b, r, n, 0))
    return pl.pallas_call(
        functools.partial(_dil_kernel, length=length, nsub=nsub),
        grid=(bsz, gh, dil, length // tq),
        in_specs=[qspec, kvspec(gh), kvspec(2 * gh),
                  pl.BlockSpec((3, 1, DIL_TQ, DIL_TK), lambda b, h, r, n: (0, h, 0, 0))],
        out_specs=[ospec, ospec],
        out_shape=[jax.ShapeDtypeStruct((gh, bsz, dil, length, HEAD_DIM), BF16),
                   jax.ShapeDtypeStruct((gh, bsz, dil, length, HEAD_DIM), F32)],
        compiler_params=_params(("parallel", "parallel", "parallel", "arbitrary")),
        name=f"dil_attn_{dil}",
    )(zq, zq, zq, tabs)


def _mix_kernel(na_ref, o1_ref, o2_ref, o3_ref, l1_ref, l2_ref, l3_ref, hb_ref,
                wpa_ref, wpb_ref, wga_ref, wgb_ref, bga_ref, bgb_ref, m_ref, na_s, dil_s, o_s, l_s):
    @pl.when(pl.program_id(1) == 0)
    def _():
        for h in range(NA_HEADS):
            na_s[:, h * HEAD_DIM:(h + 1) * HEAD_DIM] = na_ref[h]
        for g, (o_ref, l_ref) in enumerate(((o1_ref, l1_ref), (o2_ref, l2_ref), (o3_ref, l3_ref))):
            dil = DIL_GROUPS[g][1]
            for h in range(DIL_HEADS_PER_GROUP):
                for r in range(dil):
                    rows = slice(None) if dil == 1 else pl.ds(r, MIX_TM // dil, stride=dil)
                    o_s[g, h, rows, :] = o_ref[h, 0, r].astype(F32)
                    l_s[g, h, rows, :] = l_ref[h, 0, r]
        for h in range(DIL_HEADS_PER_GROUP):
            ls = [l_s[g, h] for g in range(len(DIL_GROUPS))]
            mx = jnp.maximum(jnp.maximum(ls[0], ls[1]), ls[2])
            es = [jnp.exp(l - mx) for l in ls]
            den = es[0] + es[1] + es[2]
            acc = (es[0] * o_s[0, h] + es[1] * o_s[1, h] + es[2] * o_s[2, h]) / den
            dil_s[:, h * HEAD_DIM:(h + 1) * HEAD_DIM] = acc.astype(BF16)

    hb = hb_ref[...]
    ya = jnp.dot(na_s[...], wpa_ref[...], preferred_element_type=F32)
    yb = jnp.dot(dil_s[...], wpb_ref[...], preferred_element_type=F32)
    ga = jnp.dot(hb, wga_ref[...], preferred_element_type=F32) + bga_ref[...]
    gb = jnp.dot(hb, wgb_ref[...], preferred_element_type=F32) + bgb_ref[...]
    m_ref[...] = (jax.nn.sigmoid(ga) * ya + jax.nn.sigmoid(gb) * yb).astype(BF16)


def _mix(na, dil_o, dil_lse, hb, wpa, wpb, w_all, b_all, s):
    n, d = hb.shape
    tm, tn = MIX_TM, MIX_TN
    tiles = s // tm
    gh = DIL_HEADS_PER_GROUP
    ngroups = len(DIL_GROUPS)
    assert all(tm % (16 * dil) == 0 for _, dil in DIL_GROUPS) and s % tm == 0
    col = lambda k: pl.BlockSpec((k, tn), lambda i, j: (0, j))
    ga0, gb0 = QKV_WIDTH // tn, (QKV_WIDTH + d) // tn
    gate = lambda k, first: pl.BlockSpec((k, tn), lambda i, j: (0, first + j))
    grp = lambda dil: pl.BlockSpec((gh, 1, dil, tm // dil, HEAD_DIM),
                                   lambda i, j: (0, i // tiles, 0, i % tiles, 0))
    groups = [grp(dil) for _, dil in DIL_GROUPS]
    return pl.pallas_call(
        _mix_kernel,
        grid=(n // tm, d // tn),
        in_specs=[pl.BlockSpec((NA_HEADS, tm, HEAD_DIM), lambda i, j: (0, i, 0)), *groups, *groups,
                  pl.BlockSpec((tm, d), lambda i, j: (i, 0)),
                  col(NA_WIDTH), col(DIL_OUT_WIDTH),
                  gate(d, ga0), gate(d, gb0), gate(1, ga0), gate(1, gb0)],
        out_specs=pl.BlockSpec((tm, tn), lambda i, j: (i, j)),
        out_shape=jax.ShapeDtypeStruct((n, d), BF16),
        scratch_shapes=[pltpu.VMEM((tm, NA_WIDTH), BF16), pltpu.VMEM((tm, DIL_OUT_WIDTH), BF16),
                        pltpu.VMEM((ngroups, gh, tm, HEAD_DIM), F32),
                        pltpu.VMEM((ngroups, gh, tm, HEAD_DIM), F32)],
        compiler_params=_params(("parallel", "arbitrary")),
        name="mix",
    )(na, *dil_o, *dil_lse, hb, wpa, wpb, w_all, w_all, b_all, b_all)


def _outproj_kernel(m_ref, h_ref, wo_ref, bo_ref, g_ref, b_ref, wrh_ref, wrl_ref, br_ref,
                    h1_ref, h1p_hbm, lg_ref, hbuf, sem):
    i = pl.program_id(0)
    slot = i % 2

    def out_copies(step, sl):
        return _tile_copies(h1p_hbm, step * OUT_TM, OUT_TM, hbuf.at[sl], sem.at[sl], True)

    mix = jnp.dot(m_ref[...], wo_ref[...], preferred_element_type=F32) + bo_ref[...]
    h1 = _layer_norm(DN_ALPHA * h_ref[...] + mix, g_ref[...], b_ref[...])
    h1_ref[...] = h1

    @pl.when(i >= 2)
    def _():
        for cp in out_copies(i - 2, slot):
            cp.wait()

    _pack_rows(hbuf.at[slot], h1)
    for cp in out_copies(i, slot):
        cp.start()

    @pl.when(i == pl.num_programs(0) - 1)
    def _():
        for cp in out_copies(i, slot):
            cp.wait()

        @pl.when(i >= 1)
        def _():
            for cp in out_copies(i - 1, 1 - slot):
                cp.wait()

    hi = h1.astype(BF16)
    lo = (h1 - hi.astype(F32)).astype(BF16)
    lg = (jnp.dot(hi, wrh_ref[...], preferred_element_type=F32)
          + jnp.dot(lo, wrh_ref[...], preferred_element_type=F32)
          + jnp.dot(hi, wrl_ref[...], preferred_element_type=F32))
    lg_ref[...] = lg + br_ref[...]


def _outproj(m, h, wo, bo, g, b, wr_hi, wr_lo, br):
    n, d = h.shape
    tm = OUT_TM
    row = lambda w: pl.BlockSpec((tm, w), lambda i: (i, 0))
    full = lambda r, c: pl.BlockSpec((r, c), lambda i: (0, 0))
    return pl.pallas_call(
        _outproj_kernel,
        grid=(n // tm,),
        in_specs=[row(d), row(d), full(d, d), full(1, d), full(1, d), full(1, d),
                  full(d, ROUTER_PAD), full(d, ROUTER_PAD), full(1, ROUTER_PAD)],
        out_specs=[row(d), pl.BlockSpec(memory_space=pl.ANY), row(ROUTER_PAD)],
        out_shape=[jax.ShapeDtypeStruct((n, d), F32),
                   jax.ShapeDtypeStruct((n, PACKED, LANES), jnp.uint32),
                   jax.ShapeDtypeStruct((n, ROUTER_PAD), F32)],
        scratch_shapes=[pltpu.VMEM((2, PACKED, tm, LANES), jnp.uint32), pltpu.SemaphoreType.DMA((2,))],
        compiler_params=_params(("arbitrary",)),
        name="outproj",
    )(m, h, wo, bo.reshape(1, d), g.reshape(1, d), b.reshape(1, d), wr_hi, wr_lo, br)


def _first_argmax(vals, lane_f):
    top = jnp.max(vals, axis=-1, keepdims=True)
    idx = jnp.min(jnp.where(vals == top, lane_f, float(LANES)), axis=-1, keepdims=True)
    return top, idx


def _route_kernel(lg_ref, info_ref, gate_ref, cnt_ref, base_s):
    @pl.when(pl.program_id(0) == 0)
    def _():
        base_s[...] = jnp.zeros(base_s.shape, F32)

    lg = lg_ref[...]
    lane = lax.broadcasted_iota(jnp.int32, lg.shape, 1)
    lane_f = lane.astype(F32)
    g_mask = lane < N_GROUPS
    g_top, g_sel = _first_argmax(jnp.where(g_mask, lg, MASK_VALUE), lane_f)
    g_prob = 1.0 / jnp.sum(jnp.where(g_mask, jnp.exp(lg - g_top), 0.0), axis=-1, keepdims=True)
    first = N_GROUPS + g_sel * EXPERTS_PER_GROUP
    e_mask = (lane_f >= first) & (lane_f < first + EXPERTS_PER_GROUP)
    el = jnp.where(e_mask, lg, MASK_VALUE)
    v0, i0 = _first_argmax(el, lane_f)
    v1, i1 = _first_argmax(jnp.where(lane_f == i0, MASK_VALUE, el), lane_f)
    e1 = jnp.exp(v1 - v0)
    w0 = g_prob / (1.0 + e1)
    w1 = g_prob * e1 / (1.0 + e1)

    tm = lg.shape[0]
    tri = (lax.broadcasted_iota(jnp.int32, (tm, tm), 1)
           < lax.broadcasted_iota(jnp.int32, (tm, tm), 0)).astype(BF16)
    ranks = []
    for idx in (i0, i1):
        onehot = lane_f == idx - N_GROUPS
        before = jnp.dot(tri, onehot.astype(BF16), preferred_element_type=F32) + base_s[...]
        ranks.append(jnp.sum(jnp.where(onehot, before, 0.0), axis=-1, keepdims=True))
        base_s[...] = base_s[...] + jnp.sum(onehot.astype(F32), axis=0, keepdims=True)

    info = jnp.where(lane == 0, i0 - N_GROUPS,
                     jnp.where(lane == 1, i1 - N_GROUPS,
                               jnp.where(lane == 2, ranks[0], jnp.where(lane == 3, ranks[1], 0.0))))
    info_ref[...] = info.astype(jnp.int32)
    gate_ref[...] = jnp.where(lane == 0, w0, jnp.where(lane == 1, w1, 0.0))
    cnt_ref[...] = base_s[...]


def _route(logits, n):
    tm = ROUTE_TM
    row = pl.BlockSpec((tm, ROUTER_PAD), lambda i: (i, 0))
    one = pl.BlockSpec((1, ROUTER_PAD), lambda i: (0, 0))
    info, gate, counts = pl.pallas_call(
        _route_kernel,
        grid=(n // tm,),
        in_specs=[row],
        out_specs=[row, row, one],
        out_shape=[jax.ShapeDtypeStruct((n, ROUTER_PAD), jnp.int32),
                   jax.ShapeDtypeStruct((n, ROUTER_PAD), F32),
                   jax.ShapeDtypeStruct((1, ROUTER_PAD), F32)],
        scratch_shapes=[pltpu.VMEM((1, ROUTER_PAD), F32)],
        compiler_params=_params(("arbitrary",)),
        name="route",
    )(logits)
    gate = gate[:, :TOP_K]
    a = n * TOP_K
    e_flat = info[:, :TOP_K].reshape(a)
    rank = info[:, TOP_K:2 * TOP_K].reshape(a)
    counts = counts[0, :N_EXPERTS].astype(jnp.int32)
    pcounts = (counts + MOE_TB - 1) // MOE_TB * MOE_TB
    pends = jnp.cumsum(pcounts)
    pstarts = pends - pcounts
    dest = (pstarts[e_flat] + rank).astype(jnp.int32)
    nb = a // MOE_TB + N_EXPERTS
    slots = nb * MOE_TB
    tok = jnp.arange(a, dtype=jnp.int32) // TOP_K
    slot_tok = jnp.zeros((slots,), jnp.int32).at[dest].set(tok)
    block_start = jnp.arange(nb, dtype=jnp.int32) * MOE_TB
    block_e = jnp.minimum(jnp.sum((pends[None, :] <= block_start[:, None]).astype(jnp.int32), axis=1),
                          N_EXPERTS - 1)
    n_active = (pends[-1] // MOE_TB).astype(jnp.int32).reshape(1)
    ids = jnp.arange(N_EXPERTS, dtype=jnp.int32)
    later = (counts[None, :] > 0) & (ids[None, :] > ids[:, None])
    nxt = jnp.min(jnp.where(later, ids[None, :], N_EXPERTS), axis=1)
    block_next = jnp.where(nxt == N_EXPERTS, -1, nxt)[block_e].astype(jnp.int32)
    return dest, gate, slot_tok, block_e, block_next, n_active


def _expert_kernel(be_ref, nx_ref, nact_ref, tok_ref, h1p_hbm, wg_hbm, wu_hbm, wd_hbm, ys_hbm,
                   xbuf, ybuf, wg_f, wu_f, wd_f, wg_s, wu_s, wd_s, gsem, osem, wsem):
    i = pl.program_id(0)
    slot = i % 2
    nact = nact_ref[0]

    def gather(step, sl):
        _start_row_gather(tok_ref, step * MOE_TB, 1, MOE_TB, h1p_hbm, xbuf.at[sl], gsem.at[sl], False)

    def out_copies(step, sl):
        return _tile_copies(ys_hbm, step * MOE_TB, MOE_TB, ybuf.at[sl], osem.at[sl], True)

    def weight_copies(e):
        return [pltpu.make_async_copy(src.at[e], dst, wsem)
                for src, dst in ((wg_hbm, wg_f), (wu_hbm, wu_f), (wd_hbm, wd_f))]

    @pl.when((i == 0) & (nact > 0))
    def _():
        for cp in weight_copies(be_ref[0]):
            cp.start(priority=WEIGHT_DMA_PRIORITY)
        gather(0, 0)

    @pl.when(i + 1 < nact)
    def _():
        gather(i + 1, 1 - slot)

    @pl.when(i >= 2)
    def _():
        for cp in out_copies(i - 2, slot):
            cp.wait()

    @pl.when(i < nact)
    def _():
        expert = be_ref[i]

        @pl.when((i == 0) | (expert != be_ref[jnp.maximum(i - 1, 0)]))
        def _():
            for cp in weight_copies(expert):
                cp.wait()
            wg_s[...] = wg_f[...].astype(BF16)
            wu_s[...] = wu_f[...].astype(BF16)
            wd_s[...] = wd_f[...].astype(BF16)

            @pl.when(nx_ref[i] >= 0)
            def _():
                for cp in weight_copies(nx_ref[i]):
                    cp.start(priority=WEIGHT_DMA_PRIORITY)

        _wait_row_gather(xbuf.at[slot], gsem.at[slot])
        xb = _unpack_rows(xbuf.at[slot], row_major=True).astype(BF16)
        gate = jnp.dot(xb, wg_s[...], preferred_element_type=F32)
        up = jnp.dot(xb, wu_s[...], preferred_element_type=F32)
        hid = (jax.nn.silu(gate) * up).astype(BF16)
        _pack_rows(ybuf.at[slot], jnp.dot(hid, wd_s[...], preferred_element_type=F32))

    @pl.when(i >= nact)
    def _():
        ybuf[slot] = jnp.zeros(ybuf.shape[1:], jnp.uint32)

    for cp in out_copies(i, slot):
        cp.start()

    @pl.when(i == pl.num_programs(0) - 1)
    def _():
        for cp in out_copies(i, slot):
            cp.wait()

        @pl.when(i >= 1)
        def _():
            for cp in out_copies(i - 1, 1 - slot):
                cp.wait()


def _experts(h1p, slot_tok, block_e, block_next, n_active, w_gate, w_up, w_down):
    d = D_MODEL
    slots = slot_tok.shape[0]
    nb = slots // MOE_TB
    any_spec = pl.BlockSpec(memory_space=pl.ANY)
    grid_spec = pltpu.PrefetchScalarGridSpec(
        num_scalar_prefetch=4,
        grid=(nb,),
        in_specs=[any_spec, any_spec, any_spec, any_spec],
        out_specs=any_spec,
        scratch_shapes=[pltpu.VMEM((2, MOE_TB, PACKED, LANES), jnp.uint32),
                        pltpu.VMEM((2, PACKED, MOE_TB, LANES), jnp.uint32),
                        pltpu.VMEM((d, D_EXPERT), F32), pltpu.VMEM((d, D_EXPERT), F32),
                        pltpu.VMEM((D_EXPERT, d), F32),
                        pltpu.VMEM((d, D_EXPERT), BF16), pltpu.VMEM((d, D_EXPERT), BF16),
                        pltpu.VMEM((D_EXPERT, d), BF16),
                        pltpu.SemaphoreType.DMA((2,)), pltpu.SemaphoreType.DMA((2,)),
                        pltpu.SemaphoreType.DMA(())])
    return pl.pallas_call(
        _expert_kernel,
        grid_spec=grid_spec,
        out_shape=jax.ShapeDtypeStruct((slots, PACKED, LANES), jnp.uint32),
        compiler_params=_params(("arbitrary",)),
        name="experts",
    )(block_e, block_next, n_active, slot_tok, h1p, w_gate, w_up, w_down)


def _final_kernel(dest_ref, ys_hbm, h1_ref, gate_ref, g_ref, b_ref, o_ref, ybuf, gsem):
    i = pl.program_id(0)
    slot = i % 2

    def fetch(step, sl):
        for k in range(TOP_K):
            _start_row_gather(dest_ref, step * FIN_TM * TOP_K + k, TOP_K, FIN_TM, ys_hbm,
                              ybuf.at[sl, k], gsem.at[sl], True)

    @pl.when(i == 0)
    def _():
        fetch(0, 0)

    @pl.when(i + 1 < pl.num_programs(0))
    def _():
        fetch(i + 1, 1 - slot)

    _wait_row_gather(ybuf.at[slot], gsem.at[slot])
    gate = gate_ref[...]
    ffn = (_unpack_rows(ybuf.at[slot, 0], row_major=True) * gate[:, 0:1]
           + _unpack_rows(ybuf.at[slot, 1], row_major=True) * gate[:, 1:2])
    o_ref[...] = _layer_norm(DN_ALPHA * h1_ref[...] + ffn, g_ref[...], b_ref[...])


def _final(dest, ys, h1, gate, g, b):
    n, d = h1.shape
    grid_spec = pltpu.PrefetchScalarGridSpec(
        num_scalar_prefetch=1,
        grid=(n // FIN_TM,),
        in_specs=[pl.BlockSpec(memory_space=pl.ANY),
                  pl.BlockSpec((FIN_TM, d), lambda i, ds: (i, 0)),
                  pl.BlockSpec((FIN_TM, TOP_K), lambda i, ds: (i, 0)),
                  pl.BlockSpec((1, d), lambda i, ds: (0, 0)),
                  pl.BlockSpec((1, d), lambda i, ds: (0, 0))],
        out_specs=pl.BlockSpec((FIN_TM, d), lambda i, ds: (i, 0)),
        scratch_shapes=[pltpu.VMEM((2, TOP_K, FIN_TM, PACKED, LANES), jnp.uint32),
                        pltpu.SemaphoreType.DMA((2,))])
    return pl.pallas_call(
        _final_kernel,
        grid_spec=grid_spec,
        out_shape=jax.ShapeDtypeStruct((n, d), F32),
        compiler_params=_params(("arbitrary",)),
        name="final",
    )(dest, ys, h1, gate, g.reshape(1, d), b.reshape(1, d))


def kernel(x, ln0_g, ln0_b, w_in, b_in, rpb, w_proj_a, w_proj_b, w_o, b_o, ln1_g, ln1_b,
           w_router_group, b_router_group, w_router_expert, b_router_expert,
           w_gate, w_up, w_down, ln2_g, ln2_b):
    bsz, s, d = x.shape
    n = bsz * s
    assert d == D_MODEL and w_in.shape[0] == DEPTH
    scale = HEAD_DIM ** -0.5

    col_scale = np.ones((w_in.shape[2],), np.float32)
    col_scale[:NA_WIDTH] = scale
    col_scale[3 * NA_WIDTH:3 * NA_WIDTH + DIL_WIDTH] = scale
    w_all = (w_in[0] * col_scale).astype(BF16)
    b_all = (b_in[0] * col_scale).reshape(1, -1)
    assert DIL_OUT_WIDTH == INPROJ_TN
    na_blocks, ngroups = 3 * NA_WIDTH // INPROJ_TN, len(DIL_GROUPS)
    w_r = jnp.concatenate([w_router_group[0], w_router_expert[0]], axis=1)
    w_r = jnp.pad(w_r, ((0, 0), (0, ROUTER_PAD - w_r.shape[1])))
    w_r_hi = w_r.astype(BF16)
    w_r_lo = (w_r - w_r_hi.astype(F32)).astype(BF16)
    b_r = jnp.pad(jnp.concatenate([b_router_group[0], b_router_expert[0]]),
                  (0, ROUTER_PAD - N_GROUPS - N_EXPERTS)).reshape(1, ROUTER_PAD)
    tabs = _na_bias_tables(rpb[0], s // GRID_W)
    slopes = _alibi_slopes(DIL_HEADS)

    h, hb = _ln0(x.reshape(n, d), ln0_g, ln0_b)
    zq = [_inproj(hb, w_all, b_all, bsz, s, DIL_GROUPS[0][1], 3 * NA_WIDTH + 3 * DIL_OUT_WIDTH,
                  lambda j: jnp.where(j < na_blocks, j, na_blocks + (j - na_blocks) * ngroups))]
    for g in range(1, ngroups):
        zq.append(_inproj(hb, w_all, b_all, bsz, s, DIL_GROUPS[g][1], 3 * DIL_OUT_WIDTH,
                          lambda j, g=g: na_blocks + j * ngroups + g))
    na = _neighborhood_attention(zq[0].reshape(-1, bsz, s, HEAD_DIM), tabs)
    na = na.reshape(NA_HEADS, n, HEAD_DIM)
    dil = [_dilated_group(zq[g], DIL0_H if g == 0 else 0, g, slopes) for g in range(ngroups)]
    m = _mix(na, [o for o, _ in dil], [l for _, l in dil], hb,
             w_proj_a[0].astype(BF16), w_proj_b[0].astype(BF16), w_all, b_all, s)
    h1, h1p, logits = _outproj(m, h, w_o[0].astype(BF16), b_o[0], ln1_g[0], ln1_b[0],
                               w_r_hi, w_r_lo, b_r)
    dest, gate, slot_tok, block_e, block_next, n_active = _route(logits, n)
    ys = _experts(h1p, slot_tok, block_e, block_next, n_active, w_gate[0], w_up[0], w_down[0])
    out = _final(dest, ys, h1, gate, ln2_g[0], ln2_b[0])
    return out.reshape(bsz, s, d)
```

```python
import functools

import numpy as np
import jax
import jax.numpy as jnp
from jax import lax
from jax.experimental import pallas as pl
from jax.experimental.pallas import tpu as pltpu

F32 = jnp.float32
BF16 = jnp.bfloat16

D_MODEL = 2048
HEAD_DIM = 128
GRID_W = 64
NA_HEADS = 8
NA_KH = 8
NA_KW = 16
DIL_GROUPS = ((128, 1), (512, 4), (2048, 16))
DIL_HEADS_PER_GROUP = 4
DIL_HEADS = DIL_HEADS_PER_GROUP * len(DIL_GROUPS)
N_GROUPS = 8
EXPERTS_PER_GROUP = 8
N_EXPERTS = N_GROUPS * EXPERTS_PER_GROUP
TOP_K = 2
D_EXPERT = D_MODEL // 4
LN_EPS = 1e-5
DEPTH = 1
DN_ALPHA = (2 * DEPTH) ** 0.25
NA_WIDTH = NA_HEADS * HEAD_DIM
DIL_WIDTH = DIL_HEADS * HEAD_DIM
DIL_OUT_WIDTH = DIL_HEADS_PER_GROUP * HEAD_DIM
QKV_WIDTH = 3 * NA_WIDTH + 3 * DIL_WIDTH
QKV_HEADS = QKV_WIDTH // HEAD_DIM
LANES = 128
CHUNKS = D_MODEL // LANES
PACKED = CHUNKS // 2
MASK_VALUE = -1e30

QA_H, KA_H, VA_H = 0, NA_HEADS, 2 * NA_HEADS
DIL0_H = 3 * NA_HEADS

LN_TM = 512
INPROJ_TM, INPROJ_TN = 2048, 512
NA_QROWS = 4
NA_KROWS = 12
NA_NSUB = 4
DIL_NSUB = 4
DIL_TQ = 256
DIL_HALF = 64
DIL_TK = DIL_TQ + 2 * DIL_HALF
MIX_TM, MIX_TN = 512, 512
OUT_TM = 512
ROUTER_PAD = 128
ROUTE_TM = 512
MOE_TB = 128
FIN_TM = 128
WEIGHT_DMA_PRIORITY = 1
VMEM_LIMIT = 56 * 1024 * 1024


def _params(sem, limit=VMEM_LIMIT):
    return pltpu.CompilerParams(dimension_semantics=sem, vmem_limit_bytes=limit)


def _layer_norm(x, g, b):
    mu = jnp.mean(x, axis=-1, keepdims=True)
    xc = x - mu
    var = jnp.mean(xc * xc, axis=-1, keepdims=True)
    return xc * lax.rsqrt(var + LN_EPS) * g + b


HIGH_HALF = np.uint32(0xFFFF0000)


def _bf16_bits(x):
    return lax.bitcast_convert_type(x.astype(BF16).astype(F32), jnp.uint32)


def _pack_rows(ref, val):
    for c in range(PACKED):
        lo = _bf16_bits(val[:, c * LANES:(c + 1) * LANES])
        hi = _bf16_bits(val[:, (c + PACKED) * LANES:(c + PACKED + 1) * LANES])
        ref[c] = (lo >> 16) | (hi & HIGH_HALF)


def _unpack_rows(ref):
    words = [ref[c] for c in range(PACKED)]
    lo = [lax.bitcast_convert_type(w << 16, F32) for w in words]
    hi = [lax.bitcast_convert_type(w & HIGH_HALF, F32) for w in words]
    return jnp.concatenate(lo + hi, axis=-1)


def _tile_copies(hbm, row0, rows, vmem, sem, to_hbm):
    copies = []
    for c in range(PACKED):
        h, v = hbm.at[pl.ds(row0, rows), c], vmem.at[c]
        copies.append(pltpu.make_async_copy(v, h, sem) if to_hbm else pltpu.make_async_copy(h, v, sem))
    return copies


def _start_row_gather(idx_ref, base, step, count, src_hbm, dst, sem, both_threads):
    def start(j, c):
        for half in range(2):
            t = 2 * j + half
            pltpu.make_async_copy(src_hbm.at[idx_ref[base + t * step]], dst.at[:, t], sem).start(
                priority=half if both_threads else 0)
        return c

    lax.fori_loop(0, count // 2, start, 0, unroll=2)


def _wait_row_gather(dst, sem):
    pltpu.make_async_copy(dst, dst, sem).wait()


def _ln0_kernel(x_ref, g_ref, b_ref, h_ref, hb_ref):
    y = _layer_norm(x_ref[...], g_ref[...], b_ref[...])
    h_ref[...] = y
    hb_ref[...] = y.astype(BF16)


def _ln0(x, g, b):
    n, d = x.shape
    row = pl.BlockSpec((LN_TM, d), lambda i: (i, 0))
    vec = pl.BlockSpec((1, d), lambda i: (0, 0))
    return pl.pallas_call(
        _ln0_kernel,
        grid=(n // LN_TM,),
        in_specs=[row, vec, vec],
        out_specs=[row, row],
        out_shape=[jax.ShapeDtypeStruct((n, d), F32), jax.ShapeDtypeStruct((n, d), BF16)],
        compiler_params=_params(("parallel",)),
        name="ln0",
    )(x, g.reshape(1, d), b.reshape(1, d))


def _inproj_kernel(hb_ref, w_ref, b_ref, o_ref, acc_s, *, dil):
    acc = jnp.dot(hb_ref[...], w_ref[...], preferred_element_type=F32) + b_ref[...]
    if dil == 1:
        for c in range(INPROJ_TN // LANES):
            o_ref[c, 0, 0] = acc[:, c * LANES:(c + 1) * LANES].astype(BF16)
    else:
        for c in range(INPROJ_TN // LANES):
            acc_s[c] = acc[:, c * LANES:(c + 1) * LANES]
        for c in range(INPROJ_TN // LANES):
            for r in range(dil):
                o_ref[c, 0, r] = acc_s[c, pl.ds(r, INPROJ_TM // dil, stride=dil), :].astype(BF16)


def _inproj(hb, w, b, bsz, s, dil, width, col_block):
    n, d = hb.shape
    tiles = s // INPROJ_TM
    assert INPROJ_TM % (16 * dil) == 0 and s % INPROJ_TM == 0 and width % INPROJ_TN == 0
    return pl.pallas_call(
        functools.partial(_inproj_kernel, dil=dil),
        grid=(n // INPROJ_TM, width // INPROJ_TN),
        in_specs=[pl.BlockSpec((INPROJ_TM, d), lambda i, j: (i, 0)),
                  pl.BlockSpec((d, INPROJ_TN), lambda i, j: (0, col_block(j))),
                  pl.BlockSpec((1, INPROJ_TN), lambda i, j: (0, col_block(j)))],
        out_specs=pl.BlockSpec((INPROJ_TN // LANES, 1, dil, INPROJ_TM // dil, LANES),
                               lambda i, j: (j, i // tiles, 0, i % tiles, 0)),
        out_shape=jax.ShapeDtypeStruct((width // LANES, bsz, dil, s // dil, LANES), BF16),
        scratch_shapes=[pltpu.VMEM((INPROJ_TN // LANES, INPROJ_TM, LANES), F32)],
        compiler_params=_params(("parallel", "arbitrary")),
        name=f"inproj_{dil}",
    )(hb, w, b)


def _na_bias_tables(rpb, rows):
    heads, n_dr, n_dc = rpb.shape
    qc = np.arange(GRID_W)[:, None]
    kc = np.arange(GRID_W)[None, :]
    qcs = np.clip(qc - NA_KW // 2, 0, GRID_W - NA_KW)
    v_col = (kc >= qcs) & (kc < qcs + NA_KW)
    dc = np.clip(kc - qc + NA_KW - 1, 0, n_dc - 1)
    onehot = (dc[None] == np.arange(n_dc)[:, None, None]) & v_col[None]
    toep = jnp.einsum('hrd,dqk->hrqk', rpb.astype(F32), jnp.asarray(onehot, F32),
                      precision=lax.Precision.HIGHEST)
    toep = jnp.where(jnp.asarray(v_col), toep, MASK_VALUE)
    masked = jnp.full((heads, 1, GRID_W, GRID_W), MASK_VALUE, F32)
    blocks = jnp.concatenate([toep, masked], axis=1)
    i = np.arange(NA_QROWS)[:, None]
    j = np.arange(NA_KROWS)[None, :]
    sel = []
    for r0, ks in ((0, 0), (2 * NA_QROWS, 2 * NA_QROWS - NA_KH // 2), (rows - NA_QROWS, rows - NA_KROWS)):
        r, krow = r0 + i, ks + j
        start = np.clip(r - NA_KH // 2, 0, rows - NA_KH)
        v_row = (krow >= start) & (krow < start + NA_KH)
        sel.append(np.where(v_row, krow - r + NA_KH - 1, n_dr))
    sel = np.stack(sel).reshape(-1)
    tab = jnp.take(blocks, jnp.asarray(sel, jnp.int32), axis=1)
    tab = tab.reshape(heads, 3, NA_QROWS, NA_KROWS, GRID_W, GRID_W).transpose(1, 0, 2, 4, 3, 5)
    return tab.reshape(3, heads, NA_QROWS * GRID_W, NA_KROWS * GRID_W)


def _softmax_attend(q, kw, vw, bias):
    s = lax.dot_general(q, kw, (((1,), (1,)), ((), ())), preferred_element_type=F32) + bias
    m = jnp.max(s, axis=-1, keepdims=True)
    p = jnp.exp(s - m)
    l = jnp.sum(p, axis=-1, keepdims=True)
    o = jnp.dot(p.astype(BF16), vw, preferred_element_type=F32) / l
    return o, m, l


def _edge_variant(blk, nblk):
    return jnp.where(blk == 0, 0, jnp.where(blk == nblk - 1, 2, 1))


def _na_kernel(q_ref, k_ref, v_ref, tab_ref, o_ref, *, rows):
    tq = NA_QROWS * GRID_W
    for u in range(NA_NSUB):
        blk = pl.program_id(2) * NA_NSUB + u
        ks = jnp.clip(blk * NA_QROWS - NA_KH // 2, 0, rows - NA_KROWS)
        off = pl.multiple_of(ks * GRID_W, GRID_W)
        kw = k_ref[0, 0, pl.ds(off, NA_KROWS * GRID_W), :]
        vw = v_ref[0, 0, pl.ds(off, NA_KROWS * GRID_W), :]
        bias = tab_ref[_edge_variant(blk, rows // NA_QROWS), 0]
        o, _, _ = _softmax_attend(q_ref[0, 0, u * tq:(u + 1) * tq, :], kw, vw, bias)
        o_ref[0, 0, u * tq:(u + 1) * tq, :] = o.astype(BF16)


def _neighborhood_attention(zq4, tabs):
    _, bsz, s, _ = zq4.shape
    rows = s // GRID_W
    assert rows % (NA_QROWS * NA_NSUB) == 0 and rows >= NA_KROWS + NA_QROWS
    tq = NA_QROWS * GRID_W * NA_NSUB
    return pl.pallas_call(
        functools.partial(_na_kernel, rows=rows),
        grid=(bsz, NA_HEADS, s // tq),
        in_specs=[pl.BlockSpec((1, 1, tq, HEAD_DIM), lambda b, h, i: (QA_H + h, b, i, 0)),
                  pl.BlockSpec((1, 1, s, HEAD_DIM), lambda b, h, i: (KA_H + h, b, 0, 0)),
                  pl.BlockSpec((1, 1, s, HEAD_DIM), lambda b, h, i: (VA_H + h, b, 0, 0)),
                  pl.BlockSpec((3, 1) + tabs.shape[2:], lambda b, h, i: (0, h, 0, 0))],
        out_specs=pl.BlockSpec((1, 1, tq, HEAD_DIM), lambda b, h, i: (h, b, i, 0)),
        out_shape=jax.ShapeDtypeStruct((NA_HEADS, bsz, s, HEAD_DIM), BF16),
        compiler_params=_params(("parallel", "parallel", "arbitrary")),
        name="na_attn",
    )(zq4, zq4, zq4, tabs)


def _alibi_slopes(n):
    return np.array([2.0 ** (-8.0 * (i + 1) / n) for i in range(n)], dtype=np.float32)


def _dil_bias_tables(slopes, dil):
    qi = np.arange(DIL_TQ)[:, None]
    kj = np.arange(DIL_TK)[None, :]
    dist = np.stack([np.abs(kj - qi + shift) for shift in (0, -DIL_HALF, -2 * DIL_HALF)])
    dist = jnp.asarray(dist, F32)[:, None]
    penalty = jnp.asarray(slopes, F32)[None, :, None, None] * (dist * float(dil))
    return jnp.where(dist <= DIL_HALF, -penalty, MASK_VALUE)


def _dil_kernel(q_ref, k_ref, v_ref, tab_ref, o_ref, lse_ref, *, length, nsub):
    for u in range(nsub):
        blk = pl.program_id(3) * nsub + u
        ks = jnp.clip(blk * DIL_TQ - DIL_HALF, 0, length - DIL_TK)
        off = pl.multiple_of(ks, DIL_HALF)
        kw = k_ref[0, 0, 0, pl.ds(off, DIL_TK), :]
        vw = v_ref[0, 0, 0, pl.ds(off, DIL_TK), :]
        bias = tab_ref[_edge_variant(blk, length // DIL_TQ), 0]
        rows = slice(u * DIL_TQ, (u + 1) * DIL_TQ)
        o, m, l = _softmax_attend(q_ref[0, 0, 0, rows, :], kw, vw, bias)
        o_ref[0, 0, 0, rows, :] = o.astype(BF16)
        lse_ref[0, 0, 0, rows, :] = jnp.broadcast_to(m + jnp.log(l), (DIL_TQ, LANES))


def _dilated_group(zq, base, g, slopes):
    window, dil = DIL_GROUPS[g]
    assert (window // 2) // dil == DIL_HALF
    _, bsz, _, length, _ = zq.shape
    nsub = min(DIL_NSUB, length // DIL_TQ)
    assert zq.shape[2] == dil and length % (DIL_TQ * nsub) == 0 and length >= DIL_TK
    gh = DIL_HEADS_PER_GROUP
    tabs = _dil_bias_tables(slopes[g * gh:(g + 1) * gh], dil)
    tq = DIL_TQ * nsub
    qspec = pl.BlockSpec((1, 1, 1, tq, HEAD_DIM), lambda b, h, r, n: (base + h, b, r, n, 0))
    kvspec = lambda off: pl.BlockSpec((1, 1, 1, length, HEAD_DIM),
                                      lambda b, h, r, n: (base + off + h, b, r, 0, 0))
    ospec = pl.BlockSpec((1, 1, 1, tq, HEAD_DIM), lambda b, h, r, n: (h, b, r, n, 0))
    return pl.pallas_call(
        functools.partial(_dil_kernel, length=length, nsub=nsub),
        grid=(bsz, gh, dil, length // tq),
        in_specs=[qspec, kvspec(gh), kvspec(2 * gh),
                  pl.BlockSpec((3, 1, DIL_TQ, DIL_TK), lambda b, h, r, n: (0, h, 0, 0))],
        out_specs=[ospec, ospec],
        out_shape=[jax.ShapeDtypeStruct((gh, bsz, dil, length, HEAD_DIM), BF16),
                   jax.ShapeDtypeStruct((gh, bsz, dil, length, HEAD_DIM), F32)],
        compiler_params=_params(("parallel", "parallel", "parallel", "arbitrary")),
        name=f"dil_attn_{dil}",
    )(zq, zq, zq, tabs)


def _mix_kernel(na_ref, o1_ref, o2_ref, o3_ref, l1_ref, l2_ref, l3_ref, hb_ref,
                wpa_ref, wpb_ref, wga_ref, wgb_ref, bga_ref, bgb_ref, m_ref, na_s, dil_s, o_s, l_s):
    @pl.when(pl.program_id(1) == 0)
    def _():
        for h in range(NA_HEADS):
            na_s[:, h * HEAD_DIM:(h + 1) * HEAD_DIM] = na_ref[h]
        for g, (o_ref, l_ref) in enumerate(((o1_ref, l1_ref), (o2_ref, l2_ref), (o3_ref, l3_ref))):
            dil = DIL_GROUPS[g][1]
            for h in range(DIL_HEADS_PER_GROUP):
                for r in range(dil):
                    rows = slice(None) if dil == 1 else pl.ds(r, MIX_TM // dil, stride=dil)
                    o_s[g, h, rows, :] = o_ref[h, 0, r].astype(F32)
                    l_s[g, h, rows, :] = l_ref[h, 0, r]
        for h in range(DIL_HEADS_PER_GROUP):
            ls = [l_s[g, h] for g in range(len(DIL_GROUPS))]
            mx = jnp.maximum(jnp.maximum(ls[0], ls[1]), ls[2])
            es = [jnp.exp(l - mx) for l in ls]
            den = es[0] + es[1] + es[2]
            acc = (es[0] * o_s[0, h] + es[1] * o_s[1, h] + es[2] * o_s[2, h]) / den
            dil_s[:, h * HEAD_DIM:(h + 1) * HEAD_DIM] = acc.astype(BF16)

    hb = hb_ref[...]
    ya = jnp.dot(na_s[...], wpa_ref[...], preferred_element_type=F32)
    yb = jnp.dot(dil_s[...], wpb_ref[...], preferred_element_type=F32)
    ga = jnp.dot(hb, wga_ref[...], preferred_element_type=F32) + bga_ref[...]
    gb = jnp.dot(hb, wgb_ref[...], preferred_element_type=F32) + bgb_ref[...]
    m_ref[...] = (jax.nn.sigmoid(ga) * ya + jax.nn.sigmoid(gb) * yb).astype(BF16)


def _mix(na, dil_o, dil_lse, hb, wpa, wpb, w_all, b_all, s):
    n, d = hb.shape
    tm, tn = MIX_TM, MIX_TN
    tiles = s // tm
    gh = DIL_HEADS_PER_GROUP
    ngroups = len(DIL_GROUPS)
    assert all(tm % (16 * dil) == 0 for _, dil in DIL_GROUPS) and s % tm == 0
    col = lambda k: pl.BlockSpec((k, tn), lambda i, j: (0, j))
    ga0, gb0 = QKV_WIDTH // tn, (QKV_WIDTH + d) // tn
    gate = lambda k, first: pl.BlockSpec((k, tn), lambda i, j: (0, first + j))
    grp = lambda dil: pl.BlockSpec((gh, 1, dil, tm // dil, HEAD_DIM),
                                   lambda i, j: (0, i // tiles, 0, i % tiles, 0))
    groups = [grp(dil) for _, dil in DIL_GROUPS]
    return pl.pallas_call(
        _mix_kernel,
        grid=(n // tm, d // tn),
        in_specs=[pl.BlockSpec((NA_HEADS, tm, HEAD_DIM), lambda i, j: (0, i, 0)), *groups, *groups,
                  pl.BlockSpec((tm, d), lambda i, j: (i, 0)),
                  col(NA_WIDTH), col(DIL_OUT_WIDTH),
                  gate(d, ga0), gate(d, gb0), gate(1, ga0), gate(1, gb0)],
        out_specs=pl.BlockSpec((tm, tn), lambda i, j: (i, j)),
        out_shape=jax.ShapeDtypeStruct((n, d), BF16),
        scratch_shapes=[pltpu.VMEM((tm, NA_WIDTH), BF16), pltpu.VMEM((tm, DIL_OUT_WIDTH), BF16),
                        pltpu.VMEM((ngroups, gh, tm, HEAD_DIM), F32),
                        pltpu.VMEM((ngroups, gh, tm, HEAD_DIM), F32)],
        compiler_params=_params(("parallel", "arbitrary")),
        name="mix",
    )(na, *dil_o, *dil_lse, hb, wpa, wpb, w_all, w_all, b_all, b_all)


def _outproj_kernel(m_ref, h_ref, wo_ref, bo_ref, g_ref, b_ref, wrh_ref, wrl_ref, br_ref,
                    h1_ref, h1p_hbm, lg_ref, hbuf, sem):
    i = pl.program_id(0)
    slot = i % 2

    def out_copies(step, sl):
        return _tile_copies(h1p_hbm, step * OUT_TM, OUT_TM, hbuf.at[sl], sem.at[sl], True)

    mix = jnp.dot(m_ref[...], wo_ref[...], preferred_element_type=F32) + bo_ref[...]
    h1 = _layer_norm(DN_ALPHA * h_ref[...] + mix, g_ref[...], b_ref[...])
    h1_ref[...] = h1

    @pl.when(i >= 2)
    def _():
        for cp in out_copies(i - 2, slot):
            cp.wait()

    _pack_rows(hbuf.at[slot], h1)
    for cp in out_copies(i, slot):
        cp.start()

    @pl.when(i == pl.num_programs(0) - 1)
    def _():
        for cp in out_copies(i, slot):
            cp.wait()

        @pl.when(i >= 1)
        def _():
            for cp in out_copies(i - 1, 1 - slot):
                cp.wait()

    hi = h1.astype(BF16)
    lo = (h1 - hi.astype(F32)).astype(BF16)
    lg = (jnp.dot(hi, wrh_ref[...], preferred_element_type=F32)
          + jnp.dot(lo, wrh_ref[...], preferred_element_type=F32)
          + jnp.dot(hi, wrl_ref[...], preferred_element_type=F32))
    lg_ref[...] = lg + br_ref[...]


def _outproj(m, h, wo, bo, g, b, wr_hi, wr_lo, br):
    n, d = h.shape
    tm = OUT_TM
    row = lambda w: pl.BlockSpec((tm, w), lambda i: (i, 0))
    full = lambda r, c: pl.BlockSpec((r, c), lambda i: (0, 0), pipeline_mode=pl.Buffered(1))
    return pl.pallas_call(
        _outproj_kernel,
        grid=(n // tm,),
        in_specs=[row(d), row(d), full(d, d), full(1, d), full(1, d), full(1, d),
                  full(d, ROUTER_PAD), full(d, ROUTER_PAD), full(1, ROUTER_PAD)],
        out_specs=[row(d), pl.BlockSpec(memory_space=pl.ANY), row(ROUTER_PAD)],
        out_shape=[jax.ShapeDtypeStruct((n, d), F32),
                   jax.ShapeDtypeStruct((n, PACKED, LANES), jnp.uint32),
                   jax.ShapeDtypeStruct((n, ROUTER_PAD), F32)],
        scratch_shapes=[pltpu.VMEM((2, PACKED, tm, LANES), jnp.uint32), pltpu.SemaphoreType.DMA((2,))],
        compiler_params=_params(("arbitrary",)),
        name="outproj",
    )(m, h, wo, bo.reshape(1, d), g.reshape(1, d), b.reshape(1, d), wr_hi, wr_lo, br)


def _first_argmax(vals, lane_f):
    top = jnp.max(vals, axis=-1, keepdims=True)
    idx = jnp.min(jnp.where(vals == top, lane_f, float(LANES)), axis=-1, keepdims=True)
    return top, idx


def _route_kernel(lg_ref, info_ref, gate_ref, cnt_ref, base_s):
    @pl.when(pl.program_id(0) == 0)
    def _():
        base_s[...] = jnp.zeros(base_s.shape, F32)

    lg = lg_ref[...]
    lane = lax.broadcasted_iota(jnp.int32, lg.shape, 1)
    lane_f = lane.astype(F32)
    g_mask = lane < N_GROUPS
    g_top, g_sel = _first_argmax(jnp.where(g_mask, lg, MASK_VALUE), lane_f)
    g_prob = 1.0 / jnp.sum(jnp.where(g_mask, jnp.exp(lg - g_top), 0.0), axis=-1, keepdims=True)
    first = N_GROUPS + g_sel * EXPERTS_PER_GROUP
    e_mask = (lane_f >= first) & (lane_f < first + EXPERTS_PER_GROUP)
    el = jnp.where(e_mask, lg, MASK_VALUE)
    v0, i0 = _first_argmax(el, lane_f)
    v1, i1 = _first_argmax(jnp.where(lane_f == i0, MASK_VALUE, el), lane_f)
    e1 = jnp.exp(v1 - v0)
    w0 = g_prob / (1.0 + e1)
    w1 = g_prob * e1 / (1.0 + e1)

    tm = lg.shape[0]
    tri = (lax.broadcasted_iota(jnp.int32, (tm, tm), 1)
           < lax.broadcasted_iota(jnp.int32, (tm, tm), 0)).astype(BF16)
    ranks = []
    for idx in (i0, i1):
        onehot = lane_f == idx - N_GROUPS
        before = jnp.dot(tri, onehot.astype(BF16), preferred_element_type=F32) + base_s[...]
        ranks.append(jnp.sum(jnp.where(onehot, before, 0.0), axis=-1, keepdims=True))
        base_s[...] = base_s[...] + jnp.sum(onehot.astype(F32), axis=0, keepdims=True)

    info = jnp.where(lane == 0, i0 - N_GROUPS,
                     jnp.where(lane == 1, i1 - N_GROUPS,
                               jnp.where(lane == 2, ranks[0], jnp.where(lane == 3, ranks[1], 0.0))))
    info_ref[...] = info.astype(jnp.int32)
    gate_ref[...] = jnp.where(lane == 0, w0, jnp.where(lane == 1, w1, 0.0))
    cnt_ref[...] = base_s[...]


def _route(logits, n):
    tm = ROUTE_TM
    row = pl.BlockSpec((tm, ROUTER_PAD), lambda i: (i, 0))
    one = pl.BlockSpec((1, ROUTER_PAD), lambda i: (0, 0))
    info, gate, counts = pl.pallas_call(
        _route_kernel,
        grid=(n // tm,),
        in_specs=[row],
        out_specs=[row, row, one],
        out_shape=[jax.ShapeDtypeStruct((n, ROUTER_PAD), jnp.int32),
                   jax.ShapeDtypeStruct((n, ROUTER_PAD), F32),
                   jax.ShapeDtypeStruct((1, ROUTER_PAD), F32)],
        scratch_shapes=[pltpu.VMEM((1, ROUTER_PAD), F32)],
        compiler_params=_params(("arbitrary",)),
        name="route",
    )(logits)
    gate = gate[:, :TOP_K]
    a = n * TOP_K
    e_flat = info[:, :TOP_K].reshape(a)
    rank = info[:, TOP_K:2 * TOP_K].reshape(a)
    counts = counts[0, :N_EXPERTS].astype(jnp.int32)
    pcounts = (counts + MOE_TB - 1) // MOE_TB * MOE_TB
    pends = jnp.cumsum(pcounts)
    pstarts = pends - pcounts
    dest = (pstarts[e_flat] + rank).astype(jnp.int32)
    nb = a // MOE_TB + N_EXPERTS
    slots = nb * MOE_TB
    tok = jnp.arange(a, dtype=jnp.int32) // TOP_K
    slot_tok = jnp.zeros((slots,), jnp.int32).at[dest].set(tok)
    block_start = jnp.arange(nb, dtype=jnp.int32) * MOE_TB
    block_e = jnp.minimum(jnp.sum((pends[None, :] <= block_start[:, None]).astype(jnp.int32), axis=1),
                          N_EXPERTS - 1)
    n_active = (pends[-1] // MOE_TB).astype(jnp.int32).reshape(1)
    ids = jnp.arange(N_EXPERTS, dtype=jnp.int32)
    later = (counts[None, :] > 0) & (ids[None, :] > ids[:, None])
    nxt = jnp.min(jnp.where(later, ids[None, :], N_EXPERTS), axis=1)
    block_next = jnp.where(nxt == N_EXPERTS, -1, nxt)[block_e].astype(jnp.int32)
    return dest, gate, slot_tok, block_e, block_next, n_active


def _expert_kernel(be_ref, nx_ref, nact_ref, tok_ref, h1p_hbm, wg_hbm, wu_hbm, wd_hbm, ys_hbm,
                   xbuf, ybuf, wg_f, wu_f, wd_f, wg_s, wu_s, wd_s, gsem, osem, wsem):
    i = pl.program_id(0)
    slot = i % 2
    nact = nact_ref[0]

    def gather(step, sl):
        _start_row_gather(tok_ref, step * MOE_TB, 1, MOE_TB, h1p_hbm, xbuf.at[sl], gsem.at[sl], False)

    def out_copies(step, sl):
        return _tile_copies(ys_hbm, step * MOE_TB, MOE_TB, ybuf.at[sl], osem.at[sl], True)

    def weight_copies(e):
        return [pltpu.make_async_copy(src.at[e], dst, wsem)
                for src, dst in ((wg_hbm, wg_f), (wu_hbm, wu_f), (wd_hbm, wd_f))]

    @pl.when((i == 0) & (nact > 0))
    def _():
        for cp in weight_copies(be_ref[0]):
            cp.start(priority=WEIGHT_DMA_PRIORITY)
        gather(0, 0)

    @pl.when(i + 1 < nact)
    def _():
        gather(i + 1, 1 - slot)

    @pl.when(i >= 2)
    def _():
        for cp in out_copies(i - 2, slot):
            cp.wait()

    @pl.when(i < nact)
    def _():
        expert = be_ref[i]

        @pl.when((i == 0) | (expert != be_ref[jnp.maximum(i - 1, 0)]))
        def _():
            for cp in weight_copies(expert):
                cp.wait()
            wg_s[...] = wg_f[...].astype(BF16)
            wu_s[...] = wu_f[...].astype(BF16)
            wd_s[...] = wd_f[...].astype(BF16)

            @pl.when(nx_ref[i] >= 0)
            def _():
                for cp in weight_copies(nx_ref[i]):
                    cp.start(priority=WEIGHT_DMA_PRIORITY)

        _wait_row_gather(xbuf.at[slot], gsem.at[slot])
        xb = _unpack_rows(xbuf.at[slot]).astype(BF16)
        gate = jnp.dot(xb, wg_s[...], preferred_element_type=F32)
        up = jnp.dot(xb, wu_s[...], preferred_element_type=F32)
        hid = (jax.nn.silu(gate) * up).astype(BF16)
        _pack_rows(ybuf.at[slot], jnp.dot(hid, wd_s[...], preferred_element_type=F32))

    @pl.when(i >= nact)
    def _():
        ybuf[slot] = jnp.zeros(ybuf.shape[1:], jnp.uint32)

    for cp in out_copies(i, slot):
        cp.start()

    @pl.when(i == pl.num_programs(0) - 1)
    def _():
        for cp in out_copies(i, slot):
            cp.wait()

        @pl.when(i >= 1)
        def _():
            for cp in out_copies(i - 1, 1 - slot):
                cp.wait()


def _experts(h1p, slot_tok, block_e, block_next, n_active, w_gate, w_up, w_down):
    d = D_MODEL
    slots = slot_tok.shape[0]
    nb = slots // MOE_TB
    any_spec = pl.BlockSpec(memory_space=pl.ANY)
    grid_spec = pltpu.PrefetchScalarGridSpec(
        num_scalar_prefetch=4,
        grid=(nb,),
        in_specs=[any_spec, any_spec, any_spec, any_spec],
        out_specs=any_spec,
        scratch_shapes=[pltpu.VMEM((2, PACKED, MOE_TB, LANES), jnp.uint32),
                        pltpu.VMEM((2, PACKED, MOE_TB, LANES), jnp.uint32),
                        pltpu.VMEM((d, D_EXPERT), F32), pltpu.VMEM((d, D_EXPERT), F32),
                        pltpu.VMEM((D_EXPERT, d), F32),
                        pltpu.VMEM((d, D_EXPERT), BF16), pltpu.VMEM((d, D_EXPERT), BF16),
                        pltpu.VMEM((D_EXPERT, d), BF16),
                        pltpu.SemaphoreType.DMA((2,)), pltpu.SemaphoreType.DMA((2,)),
                        pltpu.SemaphoreType.DMA(())])
    return pl.pallas_call(
        _expert_kernel,
        grid_spec=grid_spec,
        out_shape=jax.ShapeDtypeStruct((slots, PACKED, LANES), jnp.uint32),
        compiler_params=_params(("arbitrary",)),
        name="experts",
    )(block_e, block_next, n_active, slot_tok, h1p, w_gate, w_up, w_down)


def _final_kernel(dest_ref, ys_hbm, h1_ref, gate_ref, g_ref, b_ref, o_ref, ybuf, gsem):
    i = pl.program_id(0)
    slot = i % 2

    def fetch(step, sl):
        for k in range(TOP_K):
            _start_row_gather(dest_ref, step * FIN_TM * TOP_K + k, TOP_K, FIN_TM, ys_hbm,
                              ybuf.at[sl, k], gsem.at[sl], True)

    @pl.when(i == 0)
    def _():
        fetch(0, 0)

    @pl.when(i + 1 < pl.num_programs(0))
    def _():
        fetch(i + 1, 1 - slot)

    _wait_row_gather(ybuf.at[slot], gsem.at[slot])
    gate = gate_ref[...]
    ffn = (_unpack_rows(ybuf.at[slot, 0]) * gate[:, 0:1]
           + _unpack_rows(ybuf.at[slot, 1]) * gate[:, 1:2])
    o_ref[...] = _layer_norm(DN_ALPHA * h1_ref[...] + ffn, g_ref[...], b_ref[...])


def _final(dest, ys, h1, gate, g, b):
    n, d = h1.shape
    grid_spec = pltpu.PrefetchScalarGridSpec(
        num_scalar_prefetch=1,
        grid=(n // FIN_TM,),
        in_specs=[pl.BlockSpec(memory_space=pl.ANY),
                  pl.BlockSpec((FIN_TM, d), lambda i, ds: (i, 0)),
                  pl.BlockSpec((FIN_TM, TOP_K), lambda i, ds: (i, 0)),
                  pl.BlockSpec((1, d), lambda i, ds: (0, 0)),
                  pl.BlockSpec((1, d), lambda i, ds: (0, 0))],
        out_specs=pl.BlockSpec((FIN_TM, d), lambda i, ds: (i, 0)),
        scratch_shapes=[pltpu.VMEM((2, TOP_K, PACKED, FIN_TM, LANES), jnp.uint32),
                        pltpu.SemaphoreType.DMA((2,))])
    return pl.pallas_call(
        _final_kernel,
        grid_spec=grid_spec,
        out_shape=jax.ShapeDtypeStruct((n, d), F32),
        compiler_params=_params(("arbitrary",)),
        name="final",
    )(dest, ys, h1, gate, g.reshape(1, d), b.reshape(1, d))


def kernel(x, ln0_g, ln0_b, w_in, b_in, rpb, w_proj_a, w_proj_b, w_o, b_o, ln1_g, ln1_b,
           w_router_group, b_router_group, w_router_expert, b_router_expert,
           w_gate, w_up, w_down, ln2_g, ln2_b):
    bsz, s, d = x.shape
    n = bsz * s
    assert d == D_MODEL and w_in.shape[0] == DEPTH
    scale = HEAD_DIM ** -0.5

    col_scale = np.ones((w_in.shape[2],), np.float32)
    col_scale[:NA_WIDTH] = scale
    col_scale[3 * NA_WIDTH:3 * NA_WIDTH + DIL_WIDTH] = scale
    w_all = (w_in[0] * col_scale).astype(BF16)
    b_all = (b_in[0] * col_scale).reshape(1, -1)
    assert DIL_OUT_WIDTH == INPROJ_TN
    na_blocks, ngroups = 3 * NA_WIDTH // INPROJ_TN, len(DIL_GROUPS)
    w_r = jnp.concatenate([w_router_group[0], w_router_expert[0]], axis=1)
    w_r = jnp.pad(w_r, ((0, 0), (0, ROUTER_PAD - w_r.shape[1])))
    w_r_hi = w_r.astype(BF16)
    w_r_lo = (w_r - w_r_hi.astype(F32)).astype(BF16)
    b_r = jnp.pad(jnp.concatenate([b_router_group[0], b_router_expert[0]]),
                  (0, ROUTER_PAD - N_GROUPS - N_EXPERTS)).reshape(1, ROUTER_PAD)
    tabs = _na_bias_tables(rpb[0], s // GRID_W)
    slopes = _alibi_slopes(DIL_HEADS)

    h, hb = _ln0(x.reshape(n, d), ln0_g, ln0_b)
    zq = [_inproj(hb, w_all, b_all, bsz, s, DIL_GROUPS[0][1], 3 * NA_WIDTH + 3 * DIL_OUT_WIDTH,
                  lambda j: jnp.where(j < na_blocks, j, na_blocks + (j - na_blocks) * ngroups))]
    for g in range(1, ngroups):
        zq.append(_inproj(hb, w_all, b_all, bsz, s, DIL_GROUPS[g][1], 3 * DIL_OUT_WIDTH,
                          lambda j, g=g: na_blocks + j * ngroups + g))
    na = _neighborhood_attention(zq[0].reshape(-1, bsz, s, HEAD_DIM), tabs)
    na = na.reshape(NA_HEADS, n, HEAD_DIM)
    dil = [_dilated_group(zq[g], DIL0_H if g == 0 else 0, g, slopes) for g in range(ngroups)]
    m = _mix(na, [o for o, _ in dil], [l for _, l in dil], hb,
             w_proj_a[0].astype(BF16), w_proj_b[0].astype(BF16), w_all, b_all, s)
    h1, h1p, logits = _outproj(m, h, w_o[0].astype(BF16), b_o[0], ln1_g[0], ln1_b[0],
                               w_r_hi, w_r_lo, b_r)
    dest, gate, slot_tok, block_e, block_next, n_active = _route(logits, n)
    ys = _experts(h1p, slot_tok, block_e, block_next, n_active, w_gate[0], w_up[0], w_down[0])
    out = _final(dest, ys, h1, gate, ln2_g[0], ln2_b[0])
    return out.reshape(bsz, s, d)
```

```python
import functools

import numpy as np
import jax
import jax.numpy as jnp
from jax import lax
from jax.experimental import pallas as pl
from jax.experimental.pallas import tpu as pltpu

F32 = jnp.float32
BF16 = jnp.bfloat16

D_MODEL = 2048
HEAD_DIM = 128
GRID_W = 64
NA_HEADS = 8
NA_KH = 8
NA_KW = 16
DIL_GROUPS = ((128, 1), (512, 4), (2048, 16))
DIL_HEADS_PER_GROUP = 4
DIL_HEADS = DIL_HEADS_PER_GROUP * len(DIL_GROUPS)
N_GROUPS = 8
EXPERTS_PER_GROUP = 8
N_EXPERTS = N_GROUPS * EXPERTS_PER_GROUP
TOP_K = 2
D_EXPERT = D_MODEL // 4
LN_EPS = 1e-5
DEPTH = 1
DN_ALPHA = (2 * DEPTH) ** 0.25
NA_WIDTH = NA_HEADS * HEAD_DIM
DIL_WIDTH = DIL_HEADS * HEAD_DIM
DIL_OUT_WIDTH = DIL_HEADS_PER_GROUP * HEAD_DIM
QKV_WIDTH = 3 * NA_WIDTH + 3 * DIL_WIDTH
QKV_HEADS = QKV_WIDTH // HEAD_DIM
LANES = 128
CHUNKS = D_MODEL // LANES
PACKED = CHUNKS // 2
MASK_VALUE = -1e30

QA_H, KA_H, VA_H = 0, NA_HEADS, 2 * NA_HEADS
DIL0_H = 3 * NA_HEADS

LN_TM = 512
INPROJ_TM, INPROJ_TN = 2048, 512
NA_QROWS = 4
NA_KROWS = 12
NA_NSUB = 4
DIL_NSUB = 4
DIL_TQ = 256
DIL_HALF = 64
DIL_TK = DIL_TQ + 2 * DIL_HALF
MIX_TM, MIX_TN = 512, 512
OUT_TM = 512
ROUTER_PAD = 128
ROUTE_TM = 512
MOE_TB = 128
FIN_TM = 128
GATHER_SLOTS = 3
WEIGHT_DMA_PRIORITY = 1
VMEM_LIMIT = 56 * 1024 * 1024


def _params(sem, limit=VMEM_LIMIT):
    return pltpu.CompilerParams(dimension_semantics=sem, vmem_limit_bytes=limit)


def _layer_norm(x, g, b):
    mu = jnp.mean(x, axis=-1, keepdims=True)
    xc = x - mu
    var = jnp.mean(xc * xc, axis=-1, keepdims=True)
    return xc * lax.rsqrt(var + LN_EPS) * g + b


HIGH_HALF = np.uint32(0xFFFF0000)


def _bf16_bits(x):
    return lax.bitcast_convert_type(x.astype(BF16).astype(F32), jnp.uint32)


def _pack_rows(ref, val):
    for c in range(PACKED):
        lo = _bf16_bits(val[:, c * LANES:(c + 1) * LANES])
        hi = _bf16_bits(val[:, (c + PACKED) * LANES:(c + PACKED + 1) * LANES])
        ref[c] = (lo >> 16) | (hi & HIGH_HALF)


def _unpack_rows(ref):
    words = [ref[c] for c in range(PACKED)]
    lo = [lax.bitcast_convert_type(w << 16, F32) for w in words]
    hi = [lax.bitcast_convert_type(w & HIGH_HALF, F32) for w in words]
    return jnp.concatenate(lo + hi, axis=-1)


def _tile_copies(hbm, row0, rows, vmem, sem, to_hbm):
    copies = []
    for c in range(PACKED):
        h, v = hbm.at[pl.ds(row0, rows), c], vmem.at[c]
        copies.append(pltpu.make_async_copy(v, h, sem) if to_hbm else pltpu.make_async_copy(h, v, sem))
    return copies


def _start_row_gather(idx_ref, base, step, count, src_hbm, dst, sem, both_threads, inline=False):
    def start(j, c):
        for half in range(2):
            t = 2 * j + half
            pltpu.make_async_copy(src_hbm.at[idx_ref[base + t * step]], dst.at[:, t], sem).start(
                priority=half if both_threads else 0)
        return c

    if inline:
        for j in range(count // 2):
            start(j, 0)
    else:
        lax.fori_loop(0, count // 2, start, 0, unroll=2)


def _wait_row_gather(dst, sem):
    pltpu.make_async_copy(dst, dst, sem).wait()


def _ln0_kernel(x_ref, g_ref, b_ref, h_ref, hb_ref):
    y = _layer_norm(x_ref[...], g_ref[...], b_ref[...])
    h_ref[...] = y
    hb_ref[...] = y.astype(BF16)


def _ln0(x, g, b):
    n, d = x.shape
    row = pl.BlockSpec((LN_TM, d), lambda i: (i, 0))
    vec = pl.BlockSpec((1, d), lambda i: (0, 0))
    return pl.pallas_call(
        _ln0_kernel,
        grid=(n // LN_TM,),
        in_specs=[row, vec, vec],
        out_specs=[row, row],
        out_shape=[jax.ShapeDtypeStruct((n, d), F32), jax.ShapeDtypeStruct((n, d), BF16)],
        compiler_params=_params(("parallel",)),
        name="ln0",
    )(x, g.reshape(1, d), b.reshape(1, d))


def _inproj_kernel(hb_ref, w_ref, b_ref, o_ref, acc_s, *, dil):
    acc = jnp.dot(hb_ref[...], w_ref[...], preferred_element_type=F32) + b_ref[...]
    if dil == 1:
        for c in range(INPROJ_TN // LANES):
            o_ref[c, 0, 0] = acc[:, c * LANES:(c + 1) * LANES].astype(BF16)
    else:
        for c in range(INPROJ_TN // LANES):
            acc_s[c] = acc[:, c * LANES:(c + 1) * LANES]
        for c in range(INPROJ_TN // LANES):
            for r in range(dil):
                o_ref[c, 0, r] = acc_s[c, pl.ds(r, INPROJ_TM // dil, stride=dil), :].astype(BF16)


def _inproj(hb, w, b, bsz, s, dil, width, col_block):
    n, d = hb.shape
    tiles = s // INPROJ_TM
    assert INPROJ_TM % (16 * dil) == 0 and s % INPROJ_TM == 0 and width % INPROJ_TN == 0
    return pl.pallas_call(
        functools.partial(_inproj_kernel, dil=dil),
        grid=(n // INPROJ_TM, width // INPROJ_TN),
        in_specs=[pl.BlockSpec((INPROJ_TM, d), lambda i, j: (i, 0)),
                  pl.BlockSpec((d, INPROJ_TN), lambda i, j: (0, col_block(j))),
                  pl.BlockSpec((1, INPROJ_TN), lambda i, j: (0, col_block(j)))],
        out_specs=pl.BlockSpec((INPROJ_TN // LANES, 1, dil, INPROJ_TM // dil, LANES),
                               lambda i, j: (j, i // tiles, 0, i % tiles, 0)),
        out_shape=jax.ShapeDtypeStruct((width // LANES, bsz, dil, s // dil, LANES), BF16),
        scratch_shapes=[pltpu.VMEM((INPROJ_TN // LANES, INPROJ_TM, LANES), F32)],
        compiler_params=_params(("parallel", "arbitrary")),
        name=f"inproj_{dil}",
    )(hb, w, b)


def _na_bias_tables(rpb, rows):
    heads, n_dr, n_dc = rpb.shape
    qc = np.arange(GRID_W)[:, None]
    kc = np.arange(GRID_W)[None, :]
    qcs = np.clip(qc - NA_KW // 2, 0, GRID_W - NA_KW)
    v_col = (kc >= qcs) & (kc < qcs + NA_KW)
    dc = np.clip(kc - qc + NA_KW - 1, 0, n_dc - 1)
    onehot = (dc[None] == np.arange(n_dc)[:, None, None]) & v_col[None]
    toep = jnp.einsum('hrd,dqk->hrqk', rpb.astype(F32), jnp.asarray(onehot, F32),
                      precision=lax.Precision.HIGHEST)
    toep = jnp.where(jnp.asarray(v_col), toep, MASK_VALUE)
    masked = jnp.full((heads, 1, GRID_W, GRID_W), MASK_VALUE, F32)
    blocks = jnp.concatenate([toep, masked], axis=1)
    i = np.arange(NA_QROWS)[:, None]
    j = np.arange(NA_KROWS)[None, :]
    sel = []
    for r0, ks in ((0, 0), (2 * NA_QROWS, 2 * NA_QROWS - NA_KH // 2), (rows - NA_QROWS, rows - NA_KROWS)):
        r, krow = r0 + i, ks + j
        start = np.clip(r - NA_KH // 2, 0, rows - NA_KH)
        v_row = (krow >= start) & (krow < start + NA_KH)
        sel.append(np.where(v_row, krow - r + NA_KH - 1, n_dr))
    sel = np.stack(sel).reshape(-1)
    tab = jnp.take(blocks, jnp.asarray(sel, jnp.int32), axis=1)
    tab = tab.reshape(heads, 3, NA_QROWS, NA_KROWS, GRID_W, GRID_W).transpose(1, 0, 2, 4, 3, 5)
    return tab.reshape(3, heads, NA_QROWS * GRID_W, NA_KROWS * GRID_W)


def _softmax_attend(q, kw, vw, bias):
    s = lax.dot_general(q, kw, (((1,), (1,)), ((), ())), preferred_element_type=F32) + bias
    m = jnp.max(s, axis=-1, keepdims=True)
    p = jnp.exp(s - m)
    l = jnp.sum(p, axis=-1, keepdims=True)
    o = jnp.dot(p.astype(BF16), vw, preferred_element_type=F32) / l
    return o, m, l


def _edge_variant(blk, nblk):
    return jnp.where(blk == 0, 0, jnp.where(blk == nblk - 1, 2, 1))


def _na_kernel(q_ref, k_ref, v_ref, tab_ref, o_ref, *, rows):
    tq = NA_QROWS * GRID_W
    for u in range(NA_NSUB):
        blk = pl.program_id(2) * NA_NSUB + u
        ks = jnp.clip(blk * NA_QROWS - NA_KH // 2, 0, rows - NA_KROWS)
        off = pl.multiple_of(ks * GRID_W, GRID_W)
        kw = k_ref[0, 0, pl.ds(off, NA_KROWS * GRID_W), :]
        vw = v_ref[0, 0, pl.ds(off, NA_KROWS * GRID_W), :]
        bias = tab_ref[_edge_variant(blk, rows // NA_QROWS), 0]
        o, _, _ = _softmax_attend(q_ref[0, 0, u * tq:(u + 1) * tq, :], kw, vw, bias)
        o_ref[0, 0, u * tq:(u + 1) * tq, :] = o.astype(BF16)


def _neighborhood_attention(zq4, tabs):
    _, bsz, s, _ = zq4.shape
    rows = s // GRID_W
    assert rows % (NA_QROWS * NA_NSUB) == 0 and rows >= NA_KROWS + NA_QROWS
    tq = NA_QROWS * GRID_W * NA_NSUB
    return pl.pallas_call(
        functools.partial(_na_kernel, rows=rows),
        grid=(bsz, NA_HEADS, s // tq),
        in_specs=[pl.BlockSpec((1, 1, tq, HEAD_DIM), lambda b, h, i: (QA_H + h, b, i, 0)),
                  pl.BlockSpec((1, 1, s, HEAD_DIM), lambda b, h, i: (KA_H + h, b, 0, 0)),
                  pl.BlockSpec((1, 1, s, HEAD_DIM), lambda b, h, i: (VA_H + h, b, 0, 0)),
                  pl.BlockSpec((3, 1) + tabs.shape[2:], lambda b, h, i: (0, h, 0, 0))],
        out_specs=pl.BlockSpec((1, 1, tq, HEAD_DIM), lambda b, h, i: (h, b, i, 0)),
        out_shape=jax.ShapeDtypeStruct((NA_HEADS, bsz, s, HEAD_DIM), BF16),
        compiler_params=_params(("parallel", "parallel", "arbitrary")),
        name="na_attn",
    )(zq4, zq4, zq4, tabs)


def _alibi_slopes(n):
    return np.array([2.0 ** (-8.0 * (i + 1) / n) for i in range(n)], dtype=np.float32)


def _dil_bias_tables(slopes, dil):
    qi = np.arange(DIL_TQ)[:, None]
    kj = np.arange(DIL_TK)[None, :]
    dist = np.stack([np.abs(kj - qi + shift) for shift in (0, -DIL_HALF, -2 * DIL_HALF)])
    dist = jnp.asarray(dist, F32)[:, None]
    penalty = jnp.asarray(slopes, F32)[None, :, None, None] * (dist * float(dil))
    return jnp.where(dist <= DIL_HALF, -penalty, MASK_VALUE)


def _dil_kernel(q_ref, k_ref, v_ref, tab_ref, o_ref, lse_ref, *, length, nsub):
    for u in range(nsub):
        blk = pl.program_id(3) * nsub + u
        ks = jnp.clip(blk * DIL_TQ - DIL_HALF, 0, length - DIL_TK)
        off = pl.multiple_of(ks, DIL_HALF)
        kw = k_ref[0, 0, 0, pl.ds(off, DIL_TK), :]
        vw = v_ref[0, 0, 0, pl.ds(off, DIL_TK), :]
        bias = tab_ref[_edge_variant(blk, length // DIL_TQ), 0]
        rows = slice(u * DIL_TQ, (u + 1) * DIL_TQ)
        o, m, l = _softmax_attend(q_ref[0, 0, 0, rows, :], kw, vw, bias)
        o_ref[0, 0, 0, rows, :] = o.astype(BF16)
        lse_ref[0, 0, 0, rows, :] = jnp.broadcast_to(m + jnp.log(l), (DIL_TQ, LANES))


def _dilated_group(zq, base, g, slopes):
    window, dil = DIL_GROUPS[g]
    assert (window // 2) // dil == DIL_HALF
    _, bsz, _, length, _ = zq.shape
    nsub = min(DIL_NSUB, length // DIL_TQ)
    assert zq.shape[2] == dil and length % (DIL_TQ * nsub) == 0 and length >= DIL_TK
    gh = DIL_HEADS_PER_GROUP
    tabs = _dil_bias_tables(slopes[g * gh:(g + 1) * gh], dil)
    tq = DIL_TQ * nsub
    qspec = pl.BlockSpec((1, 1, 1, tq, HEAD_DIM), lambda b, h, r, n: (base + h, b, r, n, 0))
    kvspec = lambda off: pl.BlockSpec((1, 1, 1, length, HEAD_DIM),
                                      lambda b, h, r, n: (base + off + h, b, r, 0, 0))
    ospec = pl.BlockSpec((1, 1, 1, tq, HEAD_DIM), lambda b, h, r, n: (h, b, r, n, 0))
    return pl.pallas_call(
        functools.partial(_dil_kernel, length=length, nsub=nsub),
        grid=(bsz, gh, dil, length // tq),
        in_specs=[qspec, kvspec(gh), kvspec(2 * gh),
                  pl.BlockSpec((3, 1, DIL_TQ, DIL_TK), lambda b, h, r, n: (0, h, 0, 0))],
        out_specs=[ospec, ospec],
        out_shape=[jax.ShapeDtypeStruct((gh, bsz, dil, length, HEAD_DIM), BF16),
                   jax.ShapeDtypeStruct((gh, bsz, dil, length, HEAD_DIM), F32)],
        compiler_params=_params(("parallel", "parallel", "parallel", "arbitrary")),
        name=f"dil_attn_{dil}",
    )(zq, zq, zq, tabs)


def _mix_kernel(na_ref, o1_ref, o2_ref, o3_ref, l1_ref, l2_ref, l3_ref, hb_ref,
                wpa_ref, wpb_ref, wga_ref, wgb_ref, bga_ref, bgb_ref, m_ref, na_s, dil_s, o_s, l_s):
    @pl.when(pl.program_id(1) == 0)
    def _():
        for h in range(NA_HEADS):
            na_s[:, h * HEAD_DIM:(h + 1) * HEAD_DIM] = na_ref[h]
        for g, (o_ref, l_ref) in enumerate(((o1_ref, l1_ref), (o2_ref, l2_ref), (o3_ref, l3_ref))):
            dil = DIL_GROUPS[g][1]
            for h in range(DIL_HEADS_PER_GROUP):
                for r in range(dil):
                    rows = slice(None) if dil == 1 else pl.ds(r, MIX_TM // dil, stride=dil)
                    o_s[g, h, rows, :] = o_ref[h, 0, r].astype(F32)
                    l_s[g, h, rows, :] = l_ref[h, 0, r]
        for h in range(DIL_HEADS_PER_GROUP):
            ls = [l_s[g, h] for g in range(len(DIL_GROUPS))]
            mx = jnp.maximum(jnp.maximum(ls[0], ls[1]), ls[2])
            es = [jnp.exp(l - mx) for l in ls]
            den = es[0] + es[1] + es[2]
            acc = (es[0] * o_s[0, h] + es[1] * o_s[1, h] + es[2] * o_s[2, h]) / den
            dil_s[:, h * HEAD_DIM:(h + 1) * HEAD_DIM] = acc.astype(BF16)

    hb = hb_ref[...]
    ya = jnp.dot(na_s[...], wpa_ref[...], preferred_element_type=F32)
    yb = jnp.dot(dil_s[...], wpb_ref[...], preferred_element_type=F32)
    ga = jnp.dot(hb, wga_ref[...], preferred_element_type=F32) + bga_ref[...]
    gb = jnp.dot(hb, wgb_ref[...], preferred_element_type=F32) + bgb_ref[...]
    m_ref[...] = (jax.nn.sigmoid(ga) * ya + jax.nn.sigmoid(gb) * yb).astype(BF16)


def _mix(na, dil_o, dil_lse, hb, wpa, wpb, w_all, b_all, s):
    n, d = hb.shape
    tm, tn = MIX_TM, MIX_TN
    tiles = s // tm
    gh = DIL_HEADS_PER_GROUP
    ngroups = len(DIL_GROUPS)
    assert all(tm % (16 * dil) == 0 for _, dil in DIL_GROUPS) and s % tm == 0
    col = lambda k: pl.BlockSpec((k, tn), lambda i, j: (0, j))
    ga0, gb0 = QKV_WIDTH // tn, (QKV_WIDTH + d) // tn
    gate = lambda k, first: pl.BlockSpec((k, tn), lambda i, j: (0, first + j))
    grp = lambda dil: pl.BlockSpec((gh, 1, dil, tm // dil, HEAD_DIM),
                                   lambda i, j: (0, i // tiles, 0, i % tiles, 0))
    groups = [grp(dil) for _, dil in DIL_GROUPS]
    return pl.pallas_call(
        _mix_kernel,
        grid=(n // tm, d // tn),
        in_specs=[pl.BlockSpec((NA_HEADS, tm, HEAD_DIM), lambda i, j: (0, i, 0)), *groups, *groups,
                  pl.BlockSpec((tm, d), lambda i, j: (i, 0)),
                  col(NA_WIDTH), col(DIL_OUT_WIDTH),
                  gate(d, ga0), gate(d, gb0), gate(1, ga0), gate(1, gb0)],
        out_specs=pl.BlockSpec((tm, tn), lambda i, j: (i, j)),
        out_shape=jax.ShapeDtypeStruct((n, d), BF16),
        scratch_shapes=[pltpu.VMEM((tm, NA_WIDTH), BF16), pltpu.VMEM((tm, DIL_OUT_WIDTH), BF16),
                        pltpu.VMEM((ngroups, gh, tm, HEAD_DIM), F32),
                        pltpu.VMEM((ngroups, gh, tm, HEAD_DIM), F32)],
        compiler_params=_params(("parallel", "arbitrary")),
        name="mix",
    )(na, *dil_o, *dil_lse, hb, wpa, wpb, w_all, w_all, b_all, b_all)


def _outproj_kernel(m_ref, h_ref, wo_ref, bo_ref, g_ref, b_ref, wrh_ref, wrl_ref, br_ref,
                    h1_ref, h1p_hbm, lg_ref, hbuf, sem):
    i = pl.program_id(0)
    slot = i % 2

    def out_copies(step, sl):
        return _tile_copies(h1p_hbm, step * OUT_TM, OUT_TM, hbuf.at[sl], sem.at[sl], True)

    mix = jnp.dot(m_ref[...], wo_ref[...], preferred_element_type=F32) + bo_ref[...]
    h1 = _layer_norm(DN_ALPHA * h_ref[...] + mix, g_ref[...], b_ref[...])
    h1_ref[...] = h1

    @pl.when(i >= 2)
    def _():
        for cp in out_copies(i - 2, slot):
            cp.wait()

    _pack_rows(hbuf.at[slot], h1)
    for cp in out_copies(i, slot):
        cp.start()

    @pl.when(i == pl.num_programs(0) - 1)
    def _():
        for cp in out_copies(i, slot):
            cp.wait()

        @pl.when(i >= 1)
        def _():
            for cp in out_copies(i - 1, 1 - slot):
                cp.wait()

    hi = h1.astype(BF16)
    lo = (h1 - hi.astype(F32)).astype(BF16)
    lg = (jnp.dot(hi, wrh_ref[...], preferred_element_type=F32)
          + jnp.dot(lo, wrh_ref[...], preferred_element_type=F32)
          + jnp.dot(hi, wrl_ref[...], preferred_element_type=F32))
    lg_ref[...] = lg + br_ref[...]


def _outproj(m, h, wo, bo, g, b, wr_hi, wr_lo, br):
    n, d = h.shape
    tm = OUT_TM
    row = lambda w: pl.BlockSpec((tm, w), lambda i: (i, 0))
    full = lambda r, c: pl.BlockSpec((r, c), lambda i: (0, 0), pipeline_mode=pl.Buffered(1))
    return pl.pallas_call(
        _outproj_kernel,
        grid=(n // tm,),
        in_specs=[row(d), row(d), full(d, d), full(1, d), full(1, d), full(1, d),
                  full(d, ROUTER_PAD), full(d, ROUTER_PAD), full(1, ROUTER_PAD)],
        out_specs=[row(d), pl.BlockSpec(memory_space=pl.ANY), row(ROUTER_PAD)],
        out_shape=[jax.ShapeDtypeStruct((n, d), F32),
                   jax.ShapeDtypeStruct((n, PACKED, LANES), jnp.uint32),
                   jax.ShapeDtypeStruct((n, ROUTER_PAD), F32)],
        scratch_shapes=[pltpu.VMEM((2, PACKED, tm, LANES), jnp.uint32), pltpu.SemaphoreType.DMA((2,))],
        compiler_params=_params(("arbitrary",)),
        name="outproj",
    )(m, h, wo, bo.reshape(1, d), g.reshape(1, d), b.reshape(1, d), wr_hi, wr_lo, br)


def _first_argmax(vals, lane_f):
    top = jnp.max(vals, axis=-1, keepdims=True)
    idx = jnp.min(jnp.where(vals == top, lane_f, float(LANES)), axis=-1, keepdims=True)
    return top, idx


def _route_kernel(lg_ref, info_ref, gate_ref, cnt_ref, base_s):
    @pl.when(pl.program_id(0) == 0)
    def _():
        base_s[...] = jnp.zeros(base_s.shape, F32)

    lg = lg_ref[...]
    lane = lax.broadcasted_iota(jnp.int32, lg.shape, 1)
    lane_f = lane.astype(F32)
    g_mask = lane < N_GROUPS
    g_top, g_sel = _first_argmax(jnp.where(g_mask, lg, MASK_VALUE), lane_f)
    g_prob = 1.0 / jnp.sum(jnp.where(g_mask, jnp.exp(lg - g_top), 0.0), axis=-1, keepdims=True)
    first = N_GROUPS + g_sel * EXPERTS_PER_GROUP
    e_mask = (lane_f >= first) & (lane_f < first + EXPERTS_PER_GROUP)
    el = jnp.where(e_mask, lg, MASK_VALUE)
    v0, i0 = _first_argmax(el, lane_f)
    v1, i1 = _first_argmax(jnp.where(lane_f == i0, MASK_VALUE, el), lane_f)
    e1 = jnp.exp(v1 - v0)
    w0 = g_prob / (1.0 + e1)
    w1 = g_prob * e1 / (1.0 + e1)

    tm = lg.shape[0]
    tri = (lax.broadcasted_iota(jnp.int32, (tm, tm), 1)
           < lax.broadcasted_iota(jnp.int32, (tm, tm), 0)).astype(BF16)
    ranks = []
    for idx in (i0, i1):
        onehot = lane_f == idx - N_GROUPS
        before = jnp.dot(tri, onehot.astype(BF16), preferred_element_type=F32) + base_s[...]
        ranks.append(jnp.sum(jnp.where(onehot, before, 0.0), axis=-1, keepdims=True))
        base_s[...] = base_s[...] + jnp.sum(onehot.astype(F32), axis=0, keepdims=True)

    info = jnp.where(lane == 0, i0 - N_GROUPS,
                     jnp.where(lane == 1, i1 - N_GROUPS,
                               jnp.where(lane == 2, ranks[0], jnp.where(lane == 3, ranks[1], 0.0))))
    info_ref[...] = info.astype(jnp.int32)
    gate_ref[...] = jnp.where(lane == 0, w0, jnp.where(lane == 1, w1, 0.0))
    cnt_ref[...] = base_s[...]


def _route(logits, n):
    tm = ROUTE_TM
    row = pl.BlockSpec((tm, ROUTER_PAD), lambda i: (i, 0))
    one = pl.BlockSpec((1, ROUTER_PAD), lambda i: (0, 0))
    info, gate, counts = pl.pallas_call(
        _route_kernel,
        grid=(n // tm,),
        in_specs=[row],
        out_specs=[row, row, one],
        out_shape=[jax.ShapeDtypeStruct((n, ROUTER_PAD), jnp.int32),
                   jax.ShapeDtypeStruct((n, ROUTER_PAD), F32),
                   jax.ShapeDtypeStruct((1, ROUTER_PAD), F32)],
        scratch_shapes=[pltpu.VMEM((1, ROUTER_PAD), F32)],
        compiler_params=_params(("arbitrary",)),
        name="route",
    )(logits)
    gate = gate[:, :TOP_K]
    a = n * TOP_K
    e_flat = info[:, :TOP_K].reshape(a)
    rank = info[:, TOP_K:2 * TOP_K].reshape(a)
    counts = counts[0, :N_EXPERTS].astype(jnp.int32)
    pcounts = (counts + MOE_TB - 1) // MOE_TB * MOE_TB
    pends = jnp.cumsum(pcounts)
    pstarts = pends - pcounts
    dest = (pstarts[e_flat] + rank).astype(jnp.int32)
    nb = a // MOE_TB + N_EXPERTS
    slots = nb * MOE_TB
    tok = jnp.arange(a, dtype=jnp.int32) // TOP_K
    slot_tok = jnp.zeros((slots,), jnp.int32).at[dest].set(tok)
    block_start = jnp.arange(nb, dtype=jnp.int32) * MOE_TB
    block_e = jnp.minimum(jnp.sum((pends[None, :] <= block_start[:, None]).astype(jnp.int32), axis=1),
                          N_EXPERTS - 1)
    n_active = (pends[-1] // MOE_TB).astype(jnp.int32).reshape(1)
    ids = jnp.arange(N_EXPERTS, dtype=jnp.int32)
    later = (counts[None, :] > 0) & (ids[None, :] > ids[:, None])
    nxt = jnp.min(jnp.where(later, ids[None, :], N_EXPERTS), axis=1)
    block_next = jnp.where(nxt == N_EXPERTS, -1, nxt)[block_e].astype(jnp.int32)
    return dest, gate, slot_tok, block_e, block_next, n_active


def _expert_kernel(be_ref, nx_ref, nact_ref, tok_ref, h1p_hbm, wg_hbm, wu_hbm, wd_hbm, ys_hbm,
                   xbuf, ybuf, wg_f, wu_f, wd_f, wg_s, wu_s, wd_s, gsem, osem, wsem):
    i = pl.program_id(0)
    slot = i % 2
    xslot = i % GATHER_SLOTS
    nact = nact_ref[0]

    def gather(step, inline=False):
        sl = step % GATHER_SLOTS
        _start_row_gather(tok_ref, step * MOE_TB, 1, MOE_TB, h1p_hbm, xbuf.at[sl], gsem.at[sl], False,
                          inline)

    def out_copies(step, sl):
        return _tile_copies(ys_hbm, step * MOE_TB, MOE_TB, ybuf.at[sl], osem.at[sl], True)

    def weight_copies(e):
        return [pltpu.make_async_copy(src.at[e], dst, wsem)
                for src, dst in ((wg_hbm, wg_f), (wu_hbm, wu_f), (wd_hbm, wd_f))]

    @pl.when((i == 0) & (nact > 0))
    def _():
        for cp in weight_copies(be_ref[0]):
            cp.start(priority=WEIGHT_DMA_PRIORITY)
        gather(0)

    @pl.when((i == 0) & (nact > 1))
    def _():
        gather(1)

    @pl.when(i >= 2)
    def _():
        for cp in out_copies(i - 2, slot):
            cp.wait()

    @pl.when(i < nact)
    def _():
        expert = be_ref[i]

        @pl.when((i == 0) | (expert != be_ref[jnp.maximum(i - 1, 0)]))
        def _():
            for cp in weight_copies(expert):
                cp.wait()
            wg_s[...] = wg_f[...].astype(BF16)
            wu_s[...] = wu_f[...].astype(BF16)
            wd_s[...] = wd_f[...].astype(BF16)

            @pl.when(nx_ref[i] >= 0)
            def _():
                for cp in weight_copies(nx_ref[i]):
                    cp.start(priority=WEIGHT_DMA_PRIORITY)

        _wait_row_gather(xbuf.at[xslot], gsem.at[xslot])

        def block(prefetch):
            if prefetch:
                gather(i + 2, inline=True)
            xb = _unpack_rows(xbuf.at[xslot]).astype(BF16)
            gate = jnp.dot(xb, wg_s[...], preferred_element_type=F32)
            up = jnp.dot(xb, wu_s[...], preferred_element_type=F32)
            hid = (jax.nn.silu(gate) * up).astype(BF16)
            _pack_rows(ybuf.at[slot], jnp.dot(hid, wd_s[...], preferred_element_type=F32))

        pl.when(i + 2 < nact)(functools.partial(block, True))
        pl.when(i + 2 >= nact)(functools.partial(block, False))

    @pl.when(i >= nact)
    def _():
        ybuf[slot] = jnp.zeros(ybuf.shape[1:], jnp.uint32)

    for cp in out_copies(i, slot):
        cp.start()

    @pl.when(i == pl.num_programs(0) - 1)
    def _():
        for cp in out_copies(i, slot):
            cp.wait()

        @pl.when(i >= 1)
        def _():
            for cp in out_copies(i - 1, 1 - slot):
                cp.wait()


def _experts(h1p, slot_tok, block_e, block_next, n_active, w_gate, w_up, w_down):
    d = D_MODEL
    slots = slot_tok.shape[0]
    nb = slots // MOE_TB
    any_spec = pl.BlockSpec(memory_space=pl.ANY)
    grid_spec = pltpu.PrefetchScalarGridSpec(
        num_scalar_prefetch=4,
        grid=(nb,),
        in_specs=[any_spec, any_spec, any_spec, any_spec],
        out_specs=any_spec,
        scratch_shapes=[pltpu.VMEM((GATHER_SLOTS, PACKED, MOE_TB, LANES), jnp.uint32),
                        pltpu.VMEM((2, PACKED, MOE_TB, LANES), jnp.uint32),
                        pltpu.VMEM((d, D_EXPERT), F32), pltpu.VMEM((d, D_EXPERT), F32),
                        pltpu.VMEM((D_EXPERT, d), F32),
                        pltpu.VMEM((d, D_EXPERT), BF16), pltpu.VMEM((d, D_EXPERT), BF16),
                        pltpu.VMEM((D_EXPERT, d), BF16),
                        pltpu.SemaphoreType.DMA((GATHER_SLOTS,)), pltpu.SemaphoreType.DMA((2,)),
                        pltpu.SemaphoreType.DMA(())])
    return pl.pallas_call(
        _expert_kernel,
        grid_spec=grid_spec,
        out_shape=jax.ShapeDtypeStruct((slots, PACKED, LANES), jnp.uint32),
        compiler_params=_params(("arbitrary",)),
        name="experts",
    )(block_e, block_next, n_active, slot_tok, h1p, w_gate, w_up, w_down)


def _final_kernel(dest_ref, ys_hbm, h1_ref, gate_ref, g_ref, b_ref, o_ref, ybuf, gsem):
    i = pl.program_id(0)
    steps = pl.num_programs(0)
    slot = i % GATHER_SLOTS

    def fetch(step, inline=False):
        sl = step % GATHER_SLOTS
        for k in range(TOP_K):
            _start_row_gather(dest_ref, step * FIN_TM * TOP_K + k, TOP_K, FIN_TM, ys_hbm,
                              ybuf.at[sl, k], gsem.at[sl], True, inline)

    @pl.when(i == 0)
    def _():
        fetch(0)

    @pl.when((i == 0) & (steps > 1))
    def _():
        fetch(1)

    _wait_row_gather(ybuf.at[slot], gsem.at[slot])

    def combine(prefetch):
        if prefetch:
            fetch(i + 2, inline=True)
        gate = gate_ref[...]
        ffn = (_unpack_rows(ybuf.at[slot, 0]) * gate[:, 0:1]
               + _unpack_rows(ybuf.at[slot, 1]) * gate[:, 1:2])
        o_ref[...] = _layer_norm(DN_ALPHA * h1_ref[...] + ffn, g_ref[...], b_ref[...])

    pl.when(i + 2 < steps)(functools.partial(combine, True))
    pl.when(i + 2 >= steps)(functools.partial(combine, False))


def _final(dest, ys, h1, gate, g, b):
    n, d = h1.shape
    grid_spec = pltpu.PrefetchScalarGridSpec(
        num_scalar_prefetch=1,
        grid=(n // FIN_TM,),
        in_specs=[pl.BlockSpec(memory_space=pl.ANY),
                  pl.BlockSpec((FIN_TM, d), lambda i, ds: (i, 0)),
                  pl.BlockSpec((FIN_TM, TOP_K), lambda i, ds: (i, 0)),
                  pl.BlockSpec((1, d), lambda i, ds: (0, 0)),
                  pl.BlockSpec((1, d), lambda i, ds: (0, 0))],
        out_specs=pl.BlockSpec((FIN_TM, d), lambda i, ds: (i, 0)),
        scratch_shapes=[pltpu.VMEM((GATHER_SLOTS, TOP_K, PACKED, FIN_TM, LANES), jnp.uint32),
                        pltpu.SemaphoreType.DMA((GATHER_SLOTS,))])
    return pl.pallas_call(
        _final_kernel,
        grid_spec=grid_spec,
        out_shape=jax.ShapeDtypeStruct((n, d), F32),
        compiler_params=_params(("arbitrary",)),
        name="final",
    )(dest, ys, h1, gate, g.reshape(1, d), b.reshape(1, d))


def kernel(x, ln0_g, ln0_b, w_in, b_in, rpb, w_proj_a, w_proj_b, w_o, b_o, ln1_g, ln1_b,
           w_router_group, b_router_group, w_router_expert, b_router_expert,
           w_gate, w_up, w_down, ln2_g, ln2_b):
    bsz, s, d = x.shape
    n = bsz * s
    assert d == D_MODEL and w_in.shape[0] == DEPTH
    scale = HEAD_DIM ** -0.5

    col_scale = np.ones((w_in.shape[2],), np.float32)
    col_scale[:NA_WIDTH] = scale
    col_scale[3 * NA_WIDTH:3 * NA_WIDTH + DIL_WIDTH] = scale
    w_all = (w_in[0] * col_scale).astype(BF16)
    b_all = (b_in[0] * col_scale).reshape(1, -1)
    assert DIL_OUT_WIDTH == INPROJ_TN
    na_blocks, ngroups = 3 * NA_WIDTH // INPROJ_TN, len(DIL_GROUPS)
    w_r = jnp.concatenate([w_router_group[0], w_router_expert[0]], axis=1)
    w_r = jnp.pad(w_r, ((0, 0), (0, ROUTER_PAD - w_r.shape[1])))
    w_r_hi = w_r.astype(BF16)
    w_r_lo = (w_r - w_r_hi.astype(F32)).astype(BF16)
    b_r = jnp.pad(jnp.concatenate([b_router_group[0], b_router_expert[0]]),
                  (0, ROUTER_PAD - N_GROUPS - N_EXPERTS)).reshape(1, ROUTER_PAD)
    tabs = _na_bias_tables(rpb[0], s // GRID_W)
    slopes = _alibi_slopes(DIL_HEADS)

    h, hb = _ln0(x.reshape(n, d), ln0_g, ln0_b)
    zq = [_inproj(hb, w_all, b_all, bsz, s, DIL_GROUPS[0][1], 3 * NA_WIDTH + 3 * DIL_OUT_WIDTH,
                  lambda j: jnp.where(j < na_blocks, j, na_blocks + (j - na_blocks) * ngroups))]
    for g in range(1, ngroups):
        zq.append(_inproj(hb, w_all, b_all, bsz, s, DIL_GROUPS[g][1], 3 * DIL_OUT_WIDTH,
                          lambda j, g=g: na_blocks + j * ngroups + g))
    na = _neighborhood_attention(zq[0].reshape(-1, bsz, s, HEAD_DIM), tabs)
    na = na.reshape(NA_HEADS, n, HEAD_DIM)
    dil = [_dilated_group(zq[g], DIL0_H if g == 0 else 0, g, slopes) for g in range(ngroups)]
    m = _mix(na, [o for o, _ in dil], [l for _, l in dil], hb,
             w_proj_a[0].astype(BF16), w_proj_b[0].astype(BF16), w_all, b_all, s)
    h1, h1p, logits = _outproj(m, h, w_o[0].astype(BF16), b_o[0], ln1_g[0], ln1_b[0],
                               w_r_hi, w_r_lo, b_r)
    dest, gate, slot_tok, block_e, block_next, n_active = _route(logits, n)
    ys = _experts(h1p, slot_tok, block_e, block_next, n_active, w_gate[0], w_up[0], w_down[0])
    out = _final(dest, ys, h1, gate, ln2_g[0], ln2_b[0])
    return out.reshape(bsz, s, d)
```

```python
import functools

import numpy as np
import jax
import jax.numpy as jnp
from jax import lax
from jax.experimental import pallas as pl
from jax.experimental.pallas import tpu as pltpu

F32 = jnp.float32
BF16 = jnp.bfloat16

D_MODEL = 2048
HEAD_DIM = 128
GRID_W = 64
NA_HEADS = 8
NA_KH = 8
NA_KW = 16
DIL_GROUPS = ((128, 1), (512, 4), (2048, 16))
DIL_HEADS_PER_GROUP = 4
DIL_HEADS = DIL_HEADS_PER_GROUP * len(DIL_GROUPS)
N_GROUPS = 8
EXPERTS_PER_GROUP = 8
N_EXPERTS = N_GROUPS * EXPERTS_PER_GROUP
TOP_K = 2
D_EXPERT = D_MODEL // 4
LN_EPS = 1e-5
DEPTH = 1
DN_ALPHA = (2 * DEPTH) ** 0.25
NA_WIDTH = NA_HEADS * HEAD_DIM
DIL_WIDTH = DIL_HEADS * HEAD_DIM
DIL_OUT_WIDTH = DIL_HEADS_PER_GROUP * HEAD_DIM
QKV_WIDTH = 3 * NA_WIDTH + 3 * DIL_WIDTH
QKV_HEADS = QKV_WIDTH // HEAD_DIM
LANES = 128
CHUNKS = D_MODEL // LANES
PACKED = CHUNKS // 2
MASK_VALUE = -1e30

QA_H, KA_H, VA_H = 0, NA_HEADS, 2 * NA_HEADS
DIL0_H = 3 * NA_HEADS

LN_TM = 512
INPROJ_TM, INPROJ_TN = 2048, 512
NA_QROWS = 4
NA_KROWS = 12
NA_NSUB = 8
DIL_NSUB = 8
DIL_TQ = 256
DIL_HALF = 64
DIL_TK = DIL_TQ + 2 * DIL_HALF
MIX_TM, MIX_TN = 512, 512
OUT_TM = 512
ROUTER_PAD = 128
ROUTE_TM = 512
MOE_TB = 128
FIN_TM = 128
CAST_VREGS = 32
GATHER_SLOTS = 3
WEIGHT_DMA_PRIORITY = 1
VMEM_LIMIT = 56 * 1024 * 1024


def _params(sem, limit=VMEM_LIMIT):
    return pltpu.CompilerParams(dimension_semantics=sem, vmem_limit_bytes=limit)


def _layer_norm(x, g, b):
    mu = jnp.mean(x, axis=-1, keepdims=True)
    xc = x - mu
    var = jnp.mean(xc * xc, axis=-1, keepdims=True)
    return xc * lax.rsqrt(var + LN_EPS) * g + b


HIGH_HALF = np.uint32(0xFFFF0000)


def _bf16_bits(x):
    return lax.bitcast_convert_type(x.astype(BF16).astype(F32), jnp.uint32)


def _pack_rows(ref, val):
    for c in range(PACKED):
        lo = _bf16_bits(val[:, c * LANES:(c + 1) * LANES])
        hi = _bf16_bits(val[:, (c + PACKED) * LANES:(c + PACKED + 1) * LANES])
        ref[c] = (lo >> 16) | (hi & HIGH_HALF)


def _unpack_rows(ref):
    words = [ref[c] for c in range(PACKED)]
    lo = [lax.bitcast_convert_type(w << 16, F32) for w in words]
    hi = [lax.bitcast_convert_type(w & HIGH_HALF, F32) for w in words]
    return jnp.concatenate(lo + hi, axis=-1)


def _tile_copies(hbm, row0, rows, vmem, sem, to_hbm):
    copies = []
    for c in range(PACKED):
        h, v = hbm.at[pl.ds(row0, rows), c], vmem.at[c]
        copies.append(pltpu.make_async_copy(v, h, sem) if to_hbm else pltpu.make_async_copy(h, v, sem))
    return copies


def _start_row_gather(idx_ref, base, step, count, src_hbm, dst, sem, both_threads, inline=False):
    def start(j, c):
        for half in range(2):
            t = 2 * j + half
            pltpu.make_async_copy(src_hbm.at[idx_ref[base + t * step]], dst.at[:, t], sem).start(
                priority=half if both_threads else 0)
        return c

    if inline:
        for j in range(count // 2):
            start(j, 0)
    else:
        lax.fori_loop(0, count // 2, start, 0, unroll=2)


def _wait_row_gather(dst, sem):
    pltpu.make_async_copy(dst, dst, sem).wait()


def _ln0_kernel(x_ref, g_ref, b_ref, h_ref, hb_ref):
    y = _layer_norm(x_ref[...], g_ref[...], b_ref[...])
    h_ref[...] = y
    hb_ref[...] = y.astype(BF16)


def _ln0(x, g, b):
    n, d = x.shape
    row = pl.BlockSpec((LN_TM, d), lambda i: (i, 0))
    vec = pl.BlockSpec((1, d), lambda i: (0, 0))
    return pl.pallas_call(
        _ln0_kernel,
        grid=(n // LN_TM,),
        in_specs=[row, vec, vec],
        out_specs=[row, row],
        out_shape=[jax.ShapeDtypeStruct((n, d), F32), jax.ShapeDtypeStruct((n, d), BF16)],
        compiler_params=_params(("parallel",)),
        name="ln0",
    )(x, g.reshape(1, d), b.reshape(1, d))


def _inproj_kernel(hb_ref, w_ref, b_ref, o_ref, acc_s, *, dil):
    acc = jnp.dot(hb_ref[...], w_ref[...], preferred_element_type=F32) + b_ref[...]
    if dil == 1:
        for c in range(INPROJ_TN // LANES):
            o_ref[c, 0, 0] = acc[:, c * LANES:(c + 1) * LANES].astype(BF16)
    else:
        for c in range(INPROJ_TN // LANES):
            acc_s[c] = acc[:, c * LANES:(c + 1) * LANES]
        for c in range(INPROJ_TN // LANES):
            for r in range(dil):
                o_ref[c, 0, r] = acc_s[c, pl.ds(r, INPROJ_TM // dil, stride=dil), :].astype(BF16)


def _inproj(hb, w, b, bsz, s, dil, width, col_block):
    n, d = hb.shape
    tiles = s // INPROJ_TM
    assert INPROJ_TM % (16 * dil) == 0 and s % INPROJ_TM == 0 and width % INPROJ_TN == 0
    return pl.pallas_call(
        functools.partial(_inproj_kernel, dil=dil),
        grid=(n // INPROJ_TM, width // INPROJ_TN),
        in_specs=[pl.BlockSpec((INPROJ_TM, d), lambda i, j: (i, 0)),
                  pl.BlockSpec((d, INPROJ_TN), lambda i, j: (0, col_block(j))),
                  pl.BlockSpec((1, INPROJ_TN), lambda i, j: (0, col_block(j)))],
        out_specs=pl.BlockSpec((INPROJ_TN // LANES, 1, dil, INPROJ_TM // dil, LANES),
                               lambda i, j: (j, i // tiles, 0, i % tiles, 0)),
        out_shape=jax.ShapeDtypeStruct((width // LANES, bsz, dil, s // dil, LANES), BF16),
        scratch_shapes=[pltpu.VMEM((INPROJ_TN // LANES, INPROJ_TM, LANES), F32)],
        compiler_params=_params(("parallel", "arbitrary")),
        name=f"inproj_{dil}",
    )(hb, w, b)


def _na_bias_tables(rpb, rows):
    heads, n_dr, n_dc = rpb.shape
    qc = np.arange(GRID_W)[:, None]
    kc = np.arange(GRID_W)[None, :]
    qcs = np.clip(qc - NA_KW // 2, 0, GRID_W - NA_KW)
    v_col = (kc >= qcs) & (kc < qcs + NA_KW)
    dc = np.clip(kc - qc + NA_KW - 1, 0, n_dc - 1)
    onehot = (dc[None] == np.arange(n_dc)[:, None, None]) & v_col[None]
    toep = jnp.einsum('hrd,dqk->hrqk', rpb.astype(F32), jnp.asarray(onehot, F32),
                      precision=lax.Precision.HIGHEST)
    toep = jnp.where(jnp.asarray(v_col), toep, MASK_VALUE)
    masked = jnp.full((heads, 1, GRID_W, GRID_W), MASK_VALUE, F32)
    blocks = jnp.concatenate([toep, masked], axis=1)
    i = np.arange(NA_QROWS)[:, None]
    j = np.arange(NA_KROWS)[None, :]
    sel = []
    for r0, ks in ((0, 0), (2 * NA_QROWS, 2 * NA_QROWS - NA_KH // 2), (rows - NA_QROWS, rows - NA_KROWS)):
        r, krow = r0 + i, ks + j
        start = np.clip(r - NA_KH // 2, 0, rows - NA_KH)
        v_row = (krow >= start) & (krow < start + NA_KH)
        sel.append(np.where(v_row, krow - r + NA_KH - 1, n_dr))
    sel = np.stack(sel).reshape(-1)
    tab = jnp.take(blocks, jnp.asarray(sel, jnp.int32), axis=1)
    tab = tab.reshape(heads, 3, NA_QROWS, NA_KROWS, GRID_W, GRID_W).transpose(1, 0, 2, 4, 3, 5)
    return tab.reshape(3, heads, NA_QROWS * GRID_W, NA_KROWS * GRID_W)


def _attend_blocks(nblocks, scores, finish):
    scored, weighted = {}, {}
    for t in range(nblocks + 2):
        if t < nblocks:
            scored[t] = scores(t)
        if 0 <= t - 1 < nblocks:
            s, vw = scored.pop(t - 1)
            m = jnp.max(s, axis=-1, keepdims=True)
            p = jnp.exp(s - m)
            weighted[t - 1] = (p.astype(BF16), vw, m, jnp.sum(p, axis=-1, keepdims=True))
        if 0 <= t - 2 < nblocks:
            p, vw, m, l = weighted.pop(t - 2)
            finish(t - 2, jnp.dot(p, vw, preferred_element_type=F32) / l, m, l)


def _scores(q, kw, bias):
    return lax.dot_general(q, kw, (((1,), (1,)), ((), ())), preferred_element_type=F32) + bias


def _edge_variant(blk, nblk):
    return jnp.where(blk == 0, 0, jnp.where(blk == nblk - 1, 2, 1))


def _na_kernel(q_ref, k_ref, v_ref, tab_ref, o_ref, *, rows):
    tq = NA_QROWS * GRID_W

    def scores(u):
        blk = pl.program_id(2) * NA_NSUB + u
        ks = jnp.clip(blk * NA_QROWS - NA_KH // 2, 0, rows - NA_KROWS)
        off = pl.multiple_of(ks * GRID_W, GRID_W)
        kw = k_ref[0, 0, pl.ds(off, NA_KROWS * GRID_W), :]
        vw = v_ref[0, 0, pl.ds(off, NA_KROWS * GRID_W), :]
        bias = tab_ref[_edge_variant(blk, rows // NA_QROWS), 0]
        return _scores(q_ref[0, 0, u * tq:(u + 1) * tq, :], kw, bias), vw

    def finish(u, o, m, l):
        o_ref[0, 0, u * tq:(u + 1) * tq, :] = o.astype(BF16)

    _attend_blocks(NA_NSUB, scores, finish)


def _neighborhood_attention(zq4, tabs):
    _, bsz, s, _ = zq4.shape
    rows = s // GRID_W
    assert rows % (NA_QROWS * NA_NSUB) == 0 and rows >= NA_KROWS + NA_QROWS
    tq = NA_QROWS * GRID_W * NA_NSUB
    return pl.pallas_call(
        functools.partial(_na_kernel, rows=rows),
        grid=(bsz, NA_HEADS, s // tq),
        in_specs=[pl.BlockSpec((1, 1, tq, HEAD_DIM), lambda b, h, i: (QA_H + h, b, i, 0)),
                  pl.BlockSpec((1, 1, s, HEAD_DIM), lambda b, h, i: (KA_H + h, b, 0, 0)),
                  pl.BlockSpec((1, 1, s, HEAD_DIM), lambda b, h, i: (VA_H + h, b, 0, 0)),
                  pl.BlockSpec((3, 1) + tabs.shape[2:], lambda b, h, i: (0, h, 0, 0))],
        out_specs=pl.BlockSpec((1, 1, tq, HEAD_DIM), lambda b, h, i: (h, b, i, 0)),
        out_shape=jax.ShapeDtypeStruct((NA_HEADS, bsz, s, HEAD_DIM), BF16),
        compiler_params=_params(("parallel", "parallel", "arbitrary")),
        name="na_attn",
    )(zq4, zq4, zq4, tabs)


def _alibi_slopes(n):
    return np.array([2.0 ** (-8.0 * (i + 1) / n) for i in range(n)], dtype=np.float32)


def _dil_bias_tables(slopes, dil):
    qi = np.arange(DIL_TQ)[:, None]
    kj = np.arange(DIL_TK)[None, :]
    dist = np.stack([np.abs(kj - qi + shift) for shift in (0, -DIL_HALF, -2 * DIL_HALF)])
    dist = jnp.asarray(dist, F32)[:, None]
    penalty = jnp.asarray(slopes, F32)[None, :, None, None] * (dist * float(dil))
    return jnp.where(dist <= DIL_HALF, -penalty, MASK_VALUE)


def _dil_kernel(q_ref, k_ref, v_ref, tab_ref, o_ref, lse_ref, *, length, nsub):
    def scores(u):
        blk = pl.program_id(3) * nsub + u
        ks = jnp.clip(blk * DIL_TQ - DIL_HALF, 0, length - DIL_TK)
        off = pl.multiple_of(ks, DIL_HALF)
        kw = k_ref[0, 0, 0, pl.ds(off, DIL_TK), :]
        vw = v_ref[0, 0, 0, pl.ds(off, DIL_TK), :]
        bias = tab_ref[_edge_variant(blk, length // DIL_TQ), 0]
        return _scores(q_ref[0, 0, 0, u * DIL_TQ:(u + 1) * DIL_TQ, :], kw, bias), vw

    def finish(u, o, m, l):
        rows = slice(u * DIL_TQ, (u + 1) * DIL_TQ)
        o_ref[0, 0, 0, rows, :] = o.astype(BF16)
        lse_ref[0, 0, 0, rows, :] = jnp.broadcast_to(m + jnp.log(l), (DIL_TQ, LANES))

    _attend_blocks(nsub, scores, finish)


def _dilated_group(zq, base, g, slopes):
    window, dil = DIL_GROUPS[g]
    assert (window // 2) // dil == DIL_HALF
    _, bsz, _, length, _ = zq.shape
    nsub = min(DIL_NSUB, length // DIL_TQ)
    assert zq.shape[2] == dil and length % (DIL_TQ * nsub) == 0 and length >= DIL_TK
    gh = DIL_HEADS_PER_GROUP
    tabs = _dil_bias_tables(slopes[g * gh:(g + 1) * gh], dil)
    tq = DIL_TQ * nsub
    qspec = pl.BlockSpec((1, 1, 1, tq, HEAD_DIM), lambda b, h, r, n: (base + h, b, r, n, 0))
    kvspec = lambda off: pl.BlockSpec((1, 1, 1, length, HEAD_DIM),
                                      lambda b, h, r, n: (base + off + h, b, r, 0, 0))
    ospec = pl.BlockSpec((1, 1, 1, tq, HEAD_DIM), lambda b, h, r, n: (h, b, r, n, 0))
    return pl.pallas_call(
        functools.partial(_dil_kernel, length=length, nsub=nsub),
        grid=(bsz, gh, dil, length // tq),
        in_specs=[qspec, kvspec(gh), kvspec(2 * gh),
                  pl.BlockSpec((3, 1, DIL_TQ, DIL_TK), lambda b, h, r, n: (0, h, 0, 0))],
        out_specs=[ospec, ospec],
        out_shape=[jax.ShapeDtypeStruct((gh, bsz, dil, length, HEAD_DIM), BF16),
                   jax.ShapeDtypeStruct((gh, bsz, dil, length, HEAD_DIM), F32)],
        compiler_params=_params(("parallel", "parallel", "parallel", "arbitrary")),
        name=f"dil_attn_{dil}",
    )(zq, zq, zq, tabs)


def _mix_kernel(na_ref, o1_ref, o2_ref, o3_ref, l1_ref, l2_ref, l3_ref, hb_ref,
                wpa_ref, wpb_ref, wga_ref, wgb_ref, bga_ref, bgb_ref, m_ref, na_s, dil_s, o_s, l_s):
    @pl.when(pl.program_id(1) == 0)
    def _():
        for h in range(NA_HEADS):
            na_s[:, h * HEAD_DIM:(h + 1) * HEAD_DIM] = na_ref[h]
        for g, (o_ref, l_ref) in enumerate(((o1_ref, l1_ref), (o2_ref, l2_ref), (o3_ref, l3_ref))):
            dil = DIL_GROUPS[g][1]
            for h in range(DIL_HEADS_PER_GROUP):
                for r in range(dil):
                    rows = slice(None) if dil == 1 else pl.ds(r, MIX_TM // dil, stride=dil)
                    o_s[g, h, rows, :] = o_ref[h, 0, r].astype(F32)
                    l_s[g, h, rows, :] = l_ref[h, 0, r]
        for h in range(DIL_HEADS_PER_GROUP):
            ls = [l_s[g, h] for g in range(len(DIL_GROUPS))]
            mx = jnp.maximum(jnp.maximum(ls[0], ls[1]), ls[2])
            es = [jnp.exp(l - mx) for l in ls]
            den = es[0] + es[1] + es[2]
            acc = (es[0] * o_s[0, h] + es[1] * o_s[1, h] + es[2] * o_s[2, h]) / den
            dil_s[:, h * HEAD_DIM:(h + 1) * HEAD_DIM] = acc.astype(BF16)

    hb = hb_ref[...]
    ya = jnp.dot(na_s[...], wpa_ref[...], preferred_element_type=F32)
    yb = jnp.dot(dil_s[...], wpb_ref[...], preferred_element_type=F32)
    ga = jnp.dot(hb, wga_ref[...], preferred_element_type=F32) + bga_ref[...]
    gb = jnp.dot(hb, wgb_ref[...], preferred_element_type=F32) + bgb_ref[...]
    m_ref[...] = (jax.nn.sigmoid(ga) * ya + jax.nn.sigmoid(gb) * yb).astype(BF16)


def _mix(na, dil_o, dil_lse, hb, wpa, wpb, w_all, b_all, s):
    n, d = hb.shape
    tm, tn = MIX_TM, MIX_TN
    tiles = s // tm
    gh = DIL_HEADS_PER_GROUP
    ngroups = len(DIL_GROUPS)
    assert all(tm % (16 * dil) == 0 for _, dil in DIL_GROUPS) and s % tm == 0
    col = lambda k: pl.BlockSpec((k, tn), lambda i, j: (0, j))
    ga0, gb0 = QKV_WIDTH // tn, (QKV_WIDTH + d) // tn
    gate = lambda k, first: pl.BlockSpec((k, tn), lambda i, j: (0, first + j))
    grp = lambda dil: pl.BlockSpec((gh, 1, dil, tm // dil, HEAD_DIM),
                                   lambda i, j: (0, i // tiles, 0, i % tiles, 0))
    groups = [grp(dil) for _, dil in DIL_GROUPS]
    return pl.pallas_call(
        _mix_kernel,
        grid=(n // tm, d // tn),
        in_specs=[pl.BlockSpec((NA_HEADS, tm, HEAD_DIM), lambda i, j: (0, i, 0)), *groups, *groups,
                  pl.BlockSpec((tm, d), lambda i, j: (i, 0)),
                  col(NA_WIDTH), col(DIL_OUT_WIDTH),
                  gate(d, ga0), gate(d, gb0), gate(1, ga0), gate(1, gb0)],
        out_specs=pl.BlockSpec((tm, tn), lambda i, j: (i, j)),
        out_shape=jax.ShapeDtypeStruct((n, d), BF16),
        scratch_shapes=[pltpu.VMEM((tm, NA_WIDTH), BF16), pltpu.VMEM((tm, DIL_OUT_WIDTH), BF16),
                        pltpu.VMEM((ngroups, gh, tm, HEAD_DIM), F32),
                        pltpu.VMEM((ngroups, gh, tm, HEAD_DIM), F32)],
        compiler_params=_params(("parallel", "arbitrary")),
        name="mix",
    )(na, *dil_o, *dil_lse, hb, wpa, wpb, w_all, w_all, b_all, b_all)


def _outproj_kernel(m_ref, h_ref, wo_ref, bo_ref, g_ref, b_ref, wrh_ref, wrl_ref, br_ref,
                    h1_ref, h1p_hbm, lg_ref, hbuf, sem):
    i = pl.program_id(0)
    slot = i % 2

    def out_copies(step, sl):
        return _tile_copies(h1p_hbm, step * OUT_TM, OUT_TM, hbuf.at[sl], sem.at[sl], True)

    mix = jnp.dot(m_ref[...], wo_ref[...], preferred_element_type=F32) + bo_ref[...]
    h1 = _layer_norm(DN_ALPHA * h_ref[...] + mix, g_ref[...], b_ref[...])
    h1_ref[...] = h1

    @pl.when(i >= 2)
    def _():
        for cp in out_copies(i - 2, slot):
            cp.wait()

    _pack_rows(hbuf.at[slot], h1)
    for cp in out_copies(i, slot):
        cp.start()

    @pl.when(i == pl.num_programs(0) - 1)
    def _():
        for cp in out_copies(i, slot):
            cp.wait()

        @pl.when(i >= 1)
        def _():
            for cp in out_copies(i - 1, 1 - slot):
                cp.wait()

    hi = h1.astype(BF16)
    lo = (h1 - hi.astype(F32)).astype(BF16)
    lg = (jnp.dot(hi, wrh_ref[...], preferred_element_type=F32)
          + jnp.dot(lo, wrh_ref[...], preferred_element_type=F32)
          + jnp.dot(hi, wrl_ref[...], preferred_element_type=F32))
    lg_ref[...] = lg + br_ref[...]


def _outproj(m, h, wo, bo, g, b, wr_hi, wr_lo, br):
    n, d = h.shape
    tm = OUT_TM
    row = lambda w: pl.BlockSpec((tm, w), lambda i: (i, 0))
    full = lambda r, c: pl.BlockSpec((r, c), lambda i: (0, 0), pipeline_mode=pl.Buffered(1))
    return pl.pallas_call(
        _outproj_kernel,
        grid=(n // tm,),
        in_specs=[row(d), row(d), full(d, d), full(1, d), full(1, d), full(1, d),
                  full(d, ROUTER_PAD), full(d, ROUTER_PAD), full(1, ROUTER_PAD)],
        out_specs=[row(d), pl.BlockSpec(memory_space=pl.ANY), row(ROUTER_PAD)],
        out_shape=[jax.ShapeDtypeStruct((n, d), F32),
                   jax.ShapeDtypeStruct((n, PACKED, LANES), jnp.uint32),
                   jax.ShapeDtypeStruct((n, ROUTER_PAD), F32)],
        scratch_shapes=[pltpu.VMEM((2, PACKED, tm, LANES), jnp.uint32), pltpu.SemaphoreType.DMA((2,))],
        compiler_params=_params(("arbitrary",)),
        name="outproj",
    )(m, h, wo, bo.reshape(1, d), g.reshape(1, d), b.reshape(1, d), wr_hi, wr_lo, br)


def _first_argmax(vals, lane_f):
    top = jnp.max(vals, axis=-1, keepdims=True)
    idx = jnp.min(jnp.where(vals == top, lane_f, float(LANES)), axis=-1, keepdims=True)
    return top, idx


def _route_kernel(lg_ref, info_ref, gate_ref, cnt_ref, base_s):
    @pl.when(pl.program_id(0) == 0)
    def _():
        base_s[...] = jnp.zeros(base_s.shape, F32)

    lg = lg_ref[...]
    lane = lax.broadcasted_iota(jnp.int32, lg.shape, 1)
    lane_f = lane.astype(F32)
    g_mask = lane < N_GROUPS
    g_top, g_sel = _first_argmax(jnp.where(g_mask, lg, MASK_VALUE), lane_f)
    g_prob = 1.0 / jnp.sum(jnp.where(g_mask, jnp.exp(lg - g_top), 0.0), axis=-1, keepdims=True)
    first = N_GROUPS + g_sel * EXPERTS_PER_GROUP
    e_mask = (lane_f >= first) & (lane_f < first + EXPERTS_PER_GROUP)
    el = jnp.where(e_mask, lg, MASK_VALUE)
    v0, i0 = _first_argmax(el, lane_f)
    v1, i1 = _first_argmax(jnp.where(lane_f == i0, MASK_VALUE, el), lane_f)
    e1 = jnp.exp(v1 - v0)
    w0 = g_prob / (1.0 + e1)
    w1 = g_prob * e1 / (1.0 + e1)

    tm = lg.shape[0]
    tri = (lax.broadcasted_iota(jnp.int32, (tm, tm), 1)
           < lax.broadcasted_iota(jnp.int32, (tm, tm), 0)).astype(BF16)
    ranks = []
    for idx in (i0, i1):
        onehot = lane_f == idx - N_GROUPS
        before = jnp.dot(tri, onehot.astype(BF16), preferred_element_type=F32) + base_s[...]
        ranks.append(jnp.sum(jnp.where(onehot, before, 0.0), axis=-1, keepdims=True))
        base_s[...] = base_s[...] + jnp.sum(onehot.astype(F32), axis=0, keepdims=True)

    info = jnp.where(lane == 0, i0 - N_GROUPS,
                     jnp.where(lane == 1, i1 - N_GROUPS,
                               jnp.where(lane == 2, ranks[0], jnp.where(lane == 3, ranks[1], 0.0))))
    info_ref[...] = info.astype(jnp.int32)
    gate_ref[...] = jnp.where(lane == 0, w0, jnp.where(lane == 1, w1, 0.0))
    cnt_ref[...] = base_s[...]


def _route(logits, n):
    tm = ROUTE_TM
    row = pl.BlockSpec((tm, ROUTER_PAD), lambda i: (i, 0))
    one = pl.BlockSpec((1, ROUTER_PAD), lambda i: (0, 0))
    info, gate, counts = pl.pallas_call(
        _route_kernel,
        grid=(n // tm,),
        in_specs=[row],
        out_specs=[row, row, one],
        out_shape=[jax.ShapeDtypeStruct((n, ROUTER_PAD), jnp.int32),
                   jax.ShapeDtypeStruct((n, ROUTER_PAD), F32),
                   jax.ShapeDtypeStruct((1, ROUTER_PAD), F32)],
        scratch_shapes=[pltpu.VMEM((1, ROUTER_PAD), F32)],
        compiler_params=_params(("arbitrary",)),
        name="route",
    )(logits)
    gate = gate[:, :TOP_K]
    a = n * TOP_K
    e_flat = info[:, :TOP_K].reshape(a)
    rank = info[:, TOP_K:2 * TOP_K].reshape(a)
    counts = counts[0, :N_EXPERTS].astype(jnp.int32)
    pcounts = (counts + MOE_TB - 1) // MOE_TB * MOE_TB
    pends = jnp.cumsum(pcounts)
    pstarts = pends - pcounts
    dest = (pstarts[e_flat] + rank).astype(jnp.int32)
    nb = a // MOE_TB + N_EXPERTS
    slots = nb * MOE_TB
    tok = jnp.arange(a, dtype=jnp.int32) // TOP_K
    slot_tok = jnp.zeros((slots,), jnp.int32).at[dest].set(tok)
    block_start = jnp.arange(nb, dtype=jnp.int32) * MOE_TB
    block_e = jnp.minimum(jnp.sum((pends[None, :] <= block_start[:, None]).astype(jnp.int32), axis=1),
                          N_EXPERTS - 1)
    n_active = (pends[-1] // MOE_TB).astype(jnp.int32).reshape(1)
    ids = jnp.arange(N_EXPERTS, dtype=jnp.int32)
    later = (counts[None, :] > 0) & (ids[None, :] > ids[:, None])
    nxt = jnp.min(jnp.where(later, ids[None, :], N_EXPERTS), axis=1)
    block_next = jnp.where(nxt == N_EXPERTS, -1, nxt)[block_e].astype(jnp.int32)
    return dest, gate, slot_tok, block_e, block_next, n_active


def _cast_weights(src, dst):
    rows, cols = src.shape
    step = CAST_VREGS * 8 * LANES // cols

    def body(r, c):
        sl = pl.ds(pl.multiple_of(r * step, step), step)
        dst[sl, :] = src[sl, :].astype(BF16)
        return c

    lax.fori_loop(0, rows // step, body, 0, unroll=2)


def _expert_kernel(be_ref, nx_ref, nact_ref, tok_ref, h1p_hbm, wg_hbm, wu_hbm, wd_hbm, ys_hbm,
                   xbuf, ybuf, wg_f, wu_f, wd_f, wg_s, wu_s, wd_s, gsem, osem, wsem):
    i = pl.program_id(0)
    slot = i % 2
    xslot = i % GATHER_SLOTS
    nact = nact_ref[0]

    def gather(step, inline=False):
        sl = step % GATHER_SLOTS
        _start_row_gather(tok_ref, step * MOE_TB, 1, MOE_TB, h1p_hbm, xbuf.at[sl], gsem.at[sl], False,
                          inline)

    def out_copies(step, sl):
        return _tile_copies(ys_hbm, step * MOE_TB, MOE_TB, ybuf.at[sl], osem.at[sl], True)

    def weight_copies(e):
        return [pltpu.make_async_copy(src.at[e], dst, wsem)
                for src, dst in ((wg_hbm, wg_f), (wu_hbm, wu_f), (wd_hbm, wd_f))]

    @pl.when((i == 0) & (nact > 0))
    def _():
        for cp in weight_copies(be_ref[0]):
            cp.start(priority=WEIGHT_DMA_PRIORITY)
        gather(0)

    @pl.when((i == 0) & (nact > 1))
    def _():
        gather(1)

    @pl.when(i >= 2)
    def _():
        for cp in out_copies(i - 2, slot):
            cp.wait()

    @pl.when(i < nact)
    def _():
        expert = be_ref[i]

        @pl.when((i == 0) | (expert != be_ref[jnp.maximum(i - 1, 0)]))
        def _():
            for cp in weight_copies(expert):
                cp.wait()
            _cast_weights(wg_f, wg_s)
            _cast_weights(wu_f, wu_s)
            _cast_weights(wd_f, wd_s)

            @pl.when(nx_ref[i] >= 0)
            def _():
                for cp in weight_copies(nx_ref[i]):
                    cp.start(priority=WEIGHT_DMA_PRIORITY)

        _wait_row_gather(xbuf.at[xslot], gsem.at[xslot])

        def block(prefetch):
            if prefetch:
                gather(i + 2, inline=True)
            xb = _unpack_rows(xbuf.at[xslot]).astype(BF16)
            gate = jnp.dot(xb, wg_s[...], preferred_element_type=F32)
            up = jnp.dot(xb, wu_s[...], preferred_element_type=F32)
            hid = (jax.nn.silu(gate) * up).astype(BF16)
            _pack_rows(ybuf.at[slot], jnp.dot(hid, wd_s[...], preferred_element_type=F32))

        pl.when(i + 2 < nact)(functools.partial(block, True))
        pl.when(i + 2 >= nact)(functools.partial(block, False))

    @pl.when(i >= nact)
    def _():
        ybuf[slot] = jnp.zeros(ybuf.shape[1:], jnp.uint32)

    for cp in out_copies(i, slot):
        cp.start()

    @pl.when(i == pl.num_programs(0) - 1)
    def _():
        for cp in out_copies(i, slot):
            cp.wait()

        @pl.when(i >= 1)
        def _():
            for cp in out_copies(i - 1, 1 - slot):
                cp.wait()


def _experts(h1p, slot_tok, block_e, block_next, n_active, w_gate, w_up, w_down):
    d = D_MODEL
    slots = slot_tok.shape[0]
    nb = slots // MOE_TB
    any_spec = pl.BlockSpec(memory_space=pl.ANY)
    grid_spec = pltpu.PrefetchScalarGridSpec(
        num_scalar_prefetch=4,
        grid=(nb,),
        in_specs=[any_spec, any_spec, any_spec, any_spec],
        out_specs=any_spec,
        scratch_shapes=[pltpu.VMEM((GATHER_SLOTS, PACKED, MOE_TB, LANES), jnp.uint32),
                        pltpu.VMEM((2, PACKED, MOE_TB, LANES), jnp.uint32),
                        pltpu.VMEM((d, D_EXPERT), F32), pltpu.VMEM((d, D_EXPERT), F32),
                        pltpu.VMEM((D_EXPERT, d), F32),
                        pltpu.VMEM((d, D_EXPERT), BF16), pltpu.VMEM((d, D_EXPERT), BF16),
                        pltpu.VMEM((D_EXPERT, d), BF16),
                        pltpu.SemaphoreType.DMA((GATHER_SLOTS,)), pltpu.SemaphoreType.DMA((2,)),
                        pltpu.SemaphoreType.DMA(())])
    return pl.pallas_call(
        _expert_kernel,
        grid_spec=grid_spec,
        out_shape=jax.ShapeDtypeStruct((slots, PACKED, LANES), jnp.uint32),
        compiler_params=_params(("arbitrary",)),
        name="experts",
    )(block_e, block_next, n_active, slot_tok, h1p, w_gate, w_up, w_down)


def _final_kernel(dest_ref, ys_hbm, h1_ref, gate_ref, g_ref, b_ref, o_ref, ybuf, gsem):
    i = pl.program_id(0)
    steps = pl.num_programs(0)
    slot = i % GATHER_SLOTS

    def fetch(step, inline=False):
        sl = step % GATHER_SLOTS
        for k in range(TOP_K):
            _start_row_gather(dest_ref, step * FIN_TM * TOP_K + k, TOP_K, FIN_TM, ys_hbm,
                              ybuf.at[sl, k], gsem.at[sl], True, inline)

    @pl.when(i == 0)
    def _():
        fetch(0)

    @pl.when((i == 0) & (steps > 1))
    def _():
        fetch(1)

    _wait_row_gather(ybuf.at[slot], gsem.at[slot])

    def combine(prefetch):
        if prefetch:
            fetch(i + 2, inline=True)
        gate = gate_ref[...]
        ffn = (_unpack_rows(ybuf.at[slot, 0]) * gate[:, 0:1]
               + _unpack_rows(ybuf.at[slot, 1]) * gate[:, 1:2])
        o_ref[...] = _layer_norm(DN_ALPHA * h1_ref[...] + ffn, g_ref[...], b_ref[...])

    pl.when(i + 2 < steps)(functools.partial(combine, True))
    pl.when(i + 2 >= steps)(functools.partial(combine, False))


def _final(dest, ys, h1, gate, g, b):
    n, d = h1.shape
    grid_spec = pltpu.PrefetchScalarGridSpec(
        num_scalar_prefetch=1,
        grid=(n // FIN_TM,),
        in_specs=[pl.BlockSpec(memory_space=pl.ANY),
                  pl.BlockSpec((FIN_TM, d), lambda i, ds: (i, 0)),
                  pl.BlockSpec((FIN_TM, TOP_K), lambda i, ds: (i, 0)),
                  pl.BlockSpec((1, d), lambda i, ds: (0, 0)),
                  pl.BlockSpec((1, d), lambda i, ds: (0, 0))],
        out_specs=pl.BlockSpec((FIN_TM, d), lambda i, ds: (i, 0)),
        scratch_shapes=[pltpu.VMEM((GATHER_SLOTS, TOP_K, PACKED, FIN_TM, LANES), jnp.uint32),
                        pltpu.SemaphoreType.DMA((GATHER_SLOTS,))])
    return pl.pallas_call(
        _final_kernel,
        grid_spec=grid_spec,
        out_shape=jax.ShapeDtypeStruct((n, d), F32),
        compiler_params=_params(("arbitrary",)),
        name="final",
    )(dest, ys, h1, gate, g.reshape(1, d), b.reshape(1, d))


def kernel(x, ln0_g, ln0_b, w_in, b_in, rpb, w_proj_a, w_proj_b, w_o, b_o, ln1_g, ln1_b,
           w_router_group, b_router_group, w_router_expert, b_router_expert,
           w_gate, w_up, w_down, ln2_g, ln2_b):
    bsz, s, d = x.shape
    n = bsz * s
    assert d == D_MODEL and w_in.shape[0] == DEPTH
    scale = HEAD_DIM ** -0.5

    col_scale = np.ones((w_in.shape[2],), np.float32)
    col_scale[:NA_WIDTH] = scale
    col_scale[3 * NA_WIDTH:3 * NA_WIDTH + DIL_WIDTH] = scale
    w_all = (w_in[0] * col_scale).astype(BF16)
    b_all = (b_in[0] * col_scale).reshape(1, -1)
    assert DIL_OUT_WIDTH == INPROJ_TN
    na_blocks, ngroups = 3 * NA_WIDTH // INPROJ_TN, len(DIL_GROUPS)
    w_r = jnp.concatenate([w_router_group[0], w_router_expert[0]], axis=1)
    w_r = jnp.pad(w_r, ((0, 0), (0, ROUTER_PAD - w_r.shape[1])))
    w_r_hi = w_r.astype(BF16)
    w_r_lo = (w_r - w_r_hi.astype(F32)).astype(BF16)
    b_r = jnp.pad(jnp.concatenate([b_router_group[0], b_router_expert[0]]),
                  (0, ROUTER_PAD - N_GROUPS - N_EXPERTS)).reshape(1, ROUTER_PAD)
    tabs = _na_bias_tables(rpb[0], s // GRID_W)
    slopes = _alibi_slopes(DIL_HEADS)

    h, hb = _ln0(x.reshape(n, d), ln0_g, ln0_b)
    zq = [_inproj(hb, w_all, b_all, bsz, s, DIL_GROUPS[0][1], 3 * NA_WIDTH + 3 * DIL_OUT_WIDTH,
                  lambda j: jnp.where(j < na_blocks, j, na_blocks + (j - na_blocks) * ngroups))]
    for g in range(1, ngroups):
        zq.append(_inproj(hb, w_all, b_all, bsz, s, DIL_GROUPS[g][1], 3 * DIL_OUT_WIDTH,
                          lambda j, g=g: na_blocks + j * ngroups + g))
    na = _neighborhood_attention(zq[0].reshape(-1, bsz, s, HEAD_DIM), tabs)
    na = na.reshape(NA_HEADS, n, HEAD_DIM)
    dil = [_dilated_group(zq[g], DIL0_H if g == 0 else 0, g, slopes) for g in range(ngroups)]
    m = _mix(na, [o for o, _ in dil], [l for _, l in dil], hb,
             w_proj_a[0].astype(BF16), w_proj_b[0].astype(BF16), w_all, b_all, s)
    h1, h1p, logits = _outproj(m, h, w_o[0].astype(BF16), b_o[0], ln1_g[0], ln1_b[0],
                               w_r_hi, w_r_lo, b_r)
    dest, gate, slot_tok, block_e, block_next, n_active = _route(logits, n)
    ys = _experts(h1p, slot_tok, block_e, block_next, n_active, w_gate[0], w_up[0], w_down[0])
    out = _final(dest, ys, h1, gate, ln2_g[0], ln2_b[0])
    return out.reshape(bsz, s, d)
```

```python
import functools

import numpy as np
import jax
import jax.numpy as jnp
from jax import lax
from jax.experimental import pallas as pl
from jax.experimental.pallas import tpu as pltpu

F32 = jnp.float32
BF16 = jnp.bfloat16

D_MODEL = 2048
HEAD_DIM = 128
GRID_W = 64
NA_HEADS = 8
NA_KH = 8
NA_KW = 16
DIL_GROUPS = ((128, 1), (512, 4), (2048, 16))
DIL_HEADS_PER_GROUP = 4
DIL_HEADS = DIL_HEADS_PER_GROUP * len(DIL_GROUPS)
N_GROUPS = 8
EXPERTS_PER_GROUP = 8
N_EXPERTS = N_GROUPS * EXPERTS_PER_GROUP
TOP_K = 2
D_EXPERT = D_MODEL // 4
LN_EPS = 1e-5
DEPTH = 1
DN_ALPHA = (2 * DEPTH) ** 0.25
NA_WIDTH = NA_HEADS * HEAD_DIM
DIL_WIDTH = DIL_HEADS * HEAD_DIM
DIL_OUT_WIDTH = DIL_HEADS_PER_GROUP * HEAD_DIM
QKV_WIDTH = 3 * NA_WIDTH + 3 * DIL_WIDTH
QKV_HEADS = QKV_WIDTH // HEAD_DIM
LANES = 128
CHUNKS = D_MODEL // LANES
PACKED = CHUNKS // 2
MASK_VALUE = -1e30

QA_H, KA_H, VA_H = 0, NA_HEADS, 2 * NA_HEADS
DIL0_H = 3 * NA_HEADS

LN_TM = 512
INPROJ_TM, INPROJ_TN = 2048, 512
NA_QROWS = 4
NA_KROWS = 12
NA_NSUB = 8
DIL_NSUB = 8
DIL_TQ = 256
DIL_HALF = 64
DIL_TK = DIL_TQ + 2 * DIL_HALF
MIX_TM, MIX_TN = 512, 512
OUT_TM = 512
ROUTER_PAD = 128
ROUTE_TM = 512
MOE_TB = 128
FIN_TM = 128
CAST_VREGS = 32
GATHER_SLOTS = 3
WEIGHT_DMA_PRIORITY = 1
VMEM_LIMIT = 56 * 1024 * 1024


def _params(sem, limit=VMEM_LIMIT):
    return pltpu.CompilerParams(dimension_semantics=sem, vmem_limit_bytes=limit)


def _layer_norm(x, g, b):
    mu = jnp.mean(x, axis=-1, keepdims=True)
    xc = x - mu
    var = jnp.mean(xc * xc, axis=-1, keepdims=True)
    return xc * lax.rsqrt(var + LN_EPS) * g + b


HIGH_HALF = np.uint32(0xFFFF0000)


def _bf16_bits(x):
    return lax.bitcast_convert_type(x.astype(BF16).astype(F32), jnp.uint32)


def _pack_rows(ref, val):
    for c in range(PACKED):
        lo = _bf16_bits(val[:, c * LANES:(c + 1) * LANES])
        hi = _bf16_bits(val[:, (c + PACKED) * LANES:(c + PACKED + 1) * LANES])
        ref[c] = (lo >> 16) | (hi & HIGH_HALF)


def _unpack_rows(ref):
    words = [ref[c] for c in range(PACKED)]
    lo = [lax.bitcast_convert_type(w << 16, F32) for w in words]
    hi = [lax.bitcast_convert_type(w & HIGH_HALF, F32) for w in words]
    return jnp.concatenate(lo + hi, axis=-1)


def _tile_copies(hbm, row0, rows, vmem, sem, to_hbm):
    copies = []
    for c in range(PACKED):
        h, v = hbm.at[pl.ds(row0, rows), c], vmem.at[c]
        copies.append(pltpu.make_async_copy(v, h, sem) if to_hbm else pltpu.make_async_copy(h, v, sem))
    return copies


def _start_row_gather(idx_ref, base, step, count, src_hbm, dst, sem, both_threads, inline=False):
    def start(j, c):
        for half in range(2):
            t = 2 * j + half
            pltpu.make_async_copy(src_hbm.at[idx_ref[base + t * step]], dst.at[:, t], sem).start(
                priority=half if both_threads else 0)
        return c

    if inline:
        for j in range(count // 2):
            start(j, 0)
    else:
        lax.fori_loop(0, count // 2, start, 0, unroll=2)


def _wait_row_gather(dst, sem):
    pltpu.make_async_copy(dst, dst, sem).wait()


def _ln0_kernel(x_ref, g_ref, b_ref, h_ref, hb_ref):
    y = _layer_norm(x_ref[...], g_ref[...], b_ref[...])
    h_ref[...] = y
    hb_ref[...] = y.astype(BF16)


def _ln0(x, g, b):
    n, d = x.shape
    row = pl.BlockSpec((LN_TM, d), lambda i: (i, 0))
    vec = pl.BlockSpec((1, d), lambda i: (0, 0))
    return pl.pallas_call(
        _ln0_kernel,
        grid=(n // LN_TM,),
        in_specs=[row, vec, vec],
        out_specs=[row, row],
        out_shape=[jax.ShapeDtypeStruct((n, d), F32), jax.ShapeDtypeStruct((n, d), BF16)],
        compiler_params=_params(("parallel",)),
        name="ln0",
    )(x, g.reshape(1, d), b.reshape(1, d))


def _inproj_kernel(hb_ref, w_ref, b_ref, o_ref, acc_s, *, dil):
    acc = jnp.dot(hb_ref[...], w_ref[...], preferred_element_type=F32) + b_ref[...]
    if dil == 1:
        for c in range(INPROJ_TN // LANES):
            o_ref[c, 0, 0] = acc[:, c * LANES:(c + 1) * LANES].astype(BF16)
    else:
        for c in range(INPROJ_TN // LANES):
            acc_s[c] = acc[:, c * LANES:(c + 1) * LANES]
        for c in range(INPROJ_TN // LANES):
            for r in range(dil):
                o_ref[c, 0, r] = acc_s[c, pl.ds(r, INPROJ_TM // dil, stride=dil), :].astype(BF16)


def _inproj(hb, w, b, bsz, s, dil, width, col_block):
    n, d = hb.shape
    tiles = s // INPROJ_TM
    assert INPROJ_TM % (16 * dil) == 0 and s % INPROJ_TM == 0 and width % INPROJ_TN == 0
    return pl.pallas_call(
        functools.partial(_inproj_kernel, dil=dil),
        grid=(n // INPROJ_TM, width // INPROJ_TN),
        in_specs=[pl.BlockSpec((INPROJ_TM, d), lambda i, j: (i, 0)),
                  pl.BlockSpec((d, INPROJ_TN), lambda i, j: (0, col_block(j))),
                  pl.BlockSpec((1, INPROJ_TN), lambda i, j: (0, col_block(j)))],
        out_specs=pl.BlockSpec((INPROJ_TN // LANES, 1, dil, INPROJ_TM // dil, LANES),
                               lambda i, j: (j, i // tiles, 0, i % tiles, 0)),
        out_shape=jax.ShapeDtypeStruct((width // LANES, bsz, dil, s // dil, LANES), BF16),
        scratch_shapes=[pltpu.VMEM((INPROJ_TN // LANES, INPROJ_TM, LANES), F32)],
        compiler_params=_params(("parallel", "arbitrary")),
        name=f"inproj_{dil}",
    )(hb, w, b)


def _na_bias_tiles(rpb, rows):
    heads, n_dr, n_dc = rpb.shape
    qc = np.arange(GRID_W)[:, None]
    kc = np.arange(GRID_W)[None, :]
    qcs = np.clip(qc - NA_KW // 2, 0, GRID_W - NA_KW)
    v_col = (kc >= qcs) & (kc < qcs + NA_KW)
    dc = np.clip(kc - qc + NA_KW - 1, 0, n_dc - 1)
    onehot = (dc[None] == np.arange(n_dc)[:, None, None]) & v_col[None]
    toep = jnp.einsum('hrd,dqk->hrqk', rpb.astype(F32), jnp.asarray(onehot, F32),
                      precision=lax.Precision.HIGHEST)
    toep = jnp.where(jnp.asarray(v_col), toep, MASK_VALUE)
    masked = jnp.full((heads, 1, GRID_W, GRID_W), MASK_VALUE, F32)
    blocks = jnp.concatenate([toep, masked], axis=1)
    i = np.arange(NA_QROWS)[:, None]
    j = np.arange(NA_KROWS)[None, :]
    sel = []
    for r0, ks in ((0, 0), (2 * NA_QROWS, 2 * NA_QROWS - NA_KH // 2), (rows - NA_QROWS, rows - NA_KROWS)):
        r, krow = r0 + i, ks + j
        start = np.clip(r - NA_KH // 2, 0, rows - NA_KH)
        v_row = (krow >= start) & (krow < start + NA_KH)
        sel.append(np.where(v_row, krow - r + NA_KH - 1, n_dr))
    sel = np.stack(sel)
    pairs = sel.reshape(3, NA_QROWS, NA_KROWS // 2, 2)
    distinct = sorted({tuple(p) for p in pairs.reshape(-1, 2).tolist()})
    plan = [[[distinct.index(tuple(pairs[v, i, c])) for c in range(NA_KROWS // 2)]
             for i in range(NA_QROWS)] for v in range(3)]
    left = jnp.take(blocks, jnp.asarray([p[0] for p in distinct], jnp.int32), axis=1)
    right = jnp.take(blocks, jnp.asarray([p[1] for p in distinct], jnp.int32), axis=1)
    return jnp.concatenate([left, right], axis=-1), plan


def _attend_blocks(nblocks, scores, finish):
    scored, weighted = {}, {}
    for t in range(nblocks + 2):
        if t < nblocks:
            scored[t] = scores(t)
        if 0 <= t - 1 < nblocks:
            s, vw = scored.pop(t - 1)
            m = jnp.max(s, axis=-1, keepdims=True)
            p = jnp.exp(s - m)
            weighted[t - 1] = (p.astype(BF16), vw, m, jnp.sum(p, axis=-1, keepdims=True))
        if 0 <= t - 2 < nblocks:
            p, vw, m, l = weighted.pop(t - 2)
            finish(t - 2, jnp.dot(p, vw, preferred_element_type=F32) / l, m, l)


def _scores(q, kw, bias):
    return lax.dot_general(q, kw, (((1,), (1,)), ((), ())), preferred_element_type=F32) + bias


def _edge_variant(blk, nblk):
    return jnp.where(blk == 0, 0, jnp.where(blk == nblk - 1, 2, 1))


def _na_kernel(q_ref, k_ref, v_ref, tile_ref, o_ref, tab_ref, *, rows, plan):
    tq = NA_QROWS * GRID_W

    @pl.when(pl.program_id(2) == 0)
    def _():
        for v in range(3):
            for i in range(NA_QROWS):
                for c in range(NA_KROWS // 2):
                    tab_ref[v, 0, i * GRID_W:(i + 1) * GRID_W, c * 2 * GRID_W:(c + 1) * 2 * GRID_W] = (
                        tile_ref[0, plan[v][i][c]])

    def scores(u):
        blk = pl.program_id(2) * NA_NSUB + u
        ks = jnp.clip(blk * NA_QROWS - NA_KH // 2, 0, rows - NA_KROWS)
        off = pl.multiple_of(ks * GRID_W, GRID_W)
        kw = k_ref[0, 0, pl.ds(off, NA_KROWS * GRID_W), :]
        vw = v_ref[0, 0, pl.ds(off, NA_KROWS * GRID_W), :]
        bias = tab_ref[_edge_variant(blk, rows // NA_QROWS), 0]
        return _scores(q_ref[0, 0, u * tq:(u + 1) * tq, :], kw, bias), vw

    def finish(u, o, m, l):
        o_ref[0, 0, u * tq:(u + 1) * tq, :] = o.astype(BF16)

    _attend_blocks(NA_NSUB, scores, finish)


def _neighborhood_attention(zq4, rpb):
    _, bsz, s, _ = zq4.shape
    rows = s // GRID_W
    assert rows % (NA_QROWS * NA_NSUB) == 0 and rows >= NA_KROWS + NA_QROWS and NA_KROWS % 2 == 0
    tiles, plan = _na_bias_tiles(rpb, rows)
    tq = NA_QROWS * GRID_W * NA_NSUB
    return pl.pallas_call(
        functools.partial(_na_kernel, rows=rows, plan=plan),
        grid=(bsz, NA_HEADS, s // tq),
        in_specs=[pl.BlockSpec((1, 1, tq, HEAD_DIM), lambda b, h, i: (QA_H + h, b, i, 0)),
                  pl.BlockSpec((1, 1, s, HEAD_DIM), lambda b, h, i: (KA_H + h, b, 0, 0)),
                  pl.BlockSpec((1, 1, s, HEAD_DIM), lambda b, h, i: (VA_H + h, b, 0, 0)),
                  pl.BlockSpec((1,) + tiles.shape[1:], lambda b, h, i: (h, 0, 0, 0))],
        out_specs=pl.BlockSpec((1, 1, tq, HEAD_DIM), lambda b, h, i: (h, b, i, 0)),
        out_shape=jax.ShapeDtypeStruct((NA_HEADS, bsz, s, HEAD_DIM), BF16),
        scratch_shapes=[pltpu.VMEM((3, 1, NA_QROWS * GRID_W, NA_KROWS * GRID_W), F32)],
        compiler_params=_params(("arbitrary", "arbitrary", "arbitrary")),
        name="na_attn",
    )(zq4, zq4, zq4, tiles)


def _alibi_slopes(n):
    return np.array([2.0 ** (-8.0 * (i + 1) / n) for i in range(n)], dtype=np.float32)


def _dil_bias_tables(slopes, dil):
    qi = np.arange(DIL_TQ)[:, None]
    kj = np.arange(DIL_TK)[None, :]
    dist = np.stack([np.abs(kj - qi + shift) for shift in (0, -DIL_HALF, -2 * DIL_HALF)])
    dist = jnp.asarray(dist, F32)[:, None]
    penalty = jnp.asarray(slopes, F32)[None, :, None, None] * (dist * float(dil))
    return jnp.where(dist <= DIL_HALF, -penalty, MASK_VALUE)


def _dil_kernel(q_ref, k_ref, v_ref, tab_ref, o_ref, lse_ref, *, length, nsub, nres):
    def scores(u):
        res, sub = divmod(u, nsub)
        blk = pl.program_id(3) * nsub + sub
        ks = jnp.clip(blk * DIL_TQ - DIL_HALF, 0, length - DIL_TK)
        off = pl.multiple_of(ks, DIL_HALF)
        kw = k_ref[0, 0, res, pl.ds(off, DIL_TK), :]
        vw = v_ref[0, 0, res, pl.ds(off, DIL_TK), :]
        bias = tab_ref[_edge_variant(blk, length // DIL_TQ), 0]
        return _scores(q_ref[0, 0, res, sub * DIL_TQ:(sub + 1) * DIL_TQ, :], kw, bias), vw

    def finish(u, o, m, l):
        res, sub = divmod(u, nsub)
        rows = slice(sub * DIL_TQ, (sub + 1) * DIL_TQ)
        o_ref[0, 0, res, rows, :] = o.astype(BF16)
        lse_ref[0, 0, res, rows, :] = jnp.broadcast_to(m + jnp.log(l), (DIL_TQ, LANES))

    _attend_blocks(nres * nsub, scores, finish)


def _dilated_group(zq, base, g, slopes):
    window, dil = DIL_GROUPS[g]
    assert (window // 2) // dil == DIL_HALF
    _, bsz, _, length, _ = zq.shape
    nsub = min(DIL_NSUB, length // DIL_TQ)
    nres = min(dil, DIL_NSUB // nsub)
    assert zq.shape[2] == dil and length % (DIL_TQ * nsub) == 0 and length >= DIL_TK and dil % nres == 0
    gh = DIL_HEADS_PER_GROUP
    tabs = _dil_bias_tables(slopes[g * gh:(g + 1) * gh], dil)
    tq = DIL_TQ * nsub
    qspec = pl.BlockSpec((1, 1, nres, tq, HEAD_DIM), lambda b, h, r, n: (base + h, b, r, n, 0))
    kvspec = lambda off: pl.BlockSpec((1, 1, nres, length, HEAD_DIM),
                                      lambda b, h, r, n: (base + off + h, b, r, 0, 0))
    ospec = pl.BlockSpec((1, 1, nres, tq, HEAD_DIM), lambda b, h, r, n: (h, b, r, n, 0))
    return pl.pallas_call(
        functools.partial(_dil_kernel, length=length, nsub=nsub, nres=nres),
        grid=(bsz, gh, dil // nres, length // tq),
        in_specs=[qspec, kvspec(gh), kvspec(2 * gh),
                  pl.BlockSpec((3, 1, DIL_TQ, DIL_TK), lambda b, h, r, n: (0, h, 0, 0))],
        out_specs=[ospec, ospec],
        out_shape=[jax.ShapeDtypeStruct((gh, bsz, dil, length, HEAD_DIM), BF16),
                   jax.ShapeDtypeStruct((gh, bsz, dil, length, HEAD_DIM), F32)],
        compiler_params=_params(("parallel", "parallel", "parallel", "arbitrary")),
        name=f"dil_attn_{dil}",
    )(zq, zq, zq, tabs)


def _mix_kernel(na_ref, o1_ref, o2_ref, o3_ref, l1_ref, l2_ref, l3_ref, hb_ref,
                wpa_ref, wpb_ref, wga_ref, wgb_ref, bga_ref, bgb_ref, m_ref, na_s, dil_s, o_s, l_s):
    @pl.when(pl.program_id(1) == 0)
    def _():
        for h in range(NA_HEADS):
            na_s[:, h * HEAD_DIM:(h + 1) * HEAD_DIM] = na_ref[h]
        for g, (o_ref, l_ref) in enumerate(((o1_ref, l1_ref), (o2_ref, l2_ref), (o3_ref, l3_ref))):
            dil = DIL_GROUPS[g][1]
            for h in range(DIL_HEADS_PER_GROUP):
                for r in range(dil):
                    rows = slice(None) if dil == 1 else pl.ds(r, MIX_TM // dil, stride=dil)
                    o_s[g, h, rows, :] = o_ref[h, 0, r].astype(F32)
                    l_s[g, h, rows, :] = l_ref[h, 0, r]
        for h in range(DIL_HEADS_PER_GROUP):
            ls = [l_s[g, h] for g in range(len(DIL_GROUPS))]
            mx = jnp.maximum(jnp.maximum(ls[0], ls[1]), ls[2])
            es = [jnp.exp(l - mx) for l in ls]
            den = es[0] + es[1] + es[2]
            acc = (es[0] * o_s[0, h] + es[1] * o_s[1, h] + es[2] * o_s[2, h]) / den
            dil_s[:, h * HEAD_DIM:(h + 1) * HEAD_DIM] = acc.astype(BF16)

    hb = hb_ref[...]
    ya = jnp.dot(na_s[...], wpa_ref[...], preferred_element_type=F32)
    yb = jnp.dot(dil_s[...], wpb_ref[...], preferred_element_type=F32)
    ga = jnp.dot(hb, wga_ref[...], preferred_element_type=F32) + bga_ref[...]
    gb = jnp.dot(hb, wgb_ref[...], preferred_element_type=F32) + bgb_ref[...]
    m_ref[...] = (jax.nn.sigmoid(ga) * ya + jax.nn.sigmoid(gb) * yb).astype(BF16)


def _mix(na, dil_o, dil_lse, hb, wpa, wpb, w_all, b_all, s):
    n, d = hb.shape
    tm, tn = MIX_TM, MIX_TN
    tiles = s // tm
    gh = DIL_HEADS_PER_GROUP
    ngroups = len(DIL_GROUPS)
    assert all(tm % (16 * dil) == 0 for _, dil in DIL_GROUPS) and s % tm == 0
    col = lambda k: pl.BlockSpec((k, tn), lambda i, j: (0, j))
    ga0, gb0 = QKV_WIDTH // tn, (QKV_WIDTH + d) // tn
    gate = lambda k, first: pl.BlockSpec((k, tn), lambda i, j: (0, first + j))
    grp = lambda dil: pl.BlockSpec((gh, 1, dil, tm // dil, HEAD_DIM),
                                   lambda i, j: (0, i // tiles, 0, i % tiles, 0))
    groups = [grp(dil) for _, dil in DIL_GROUPS]
    return pl.pallas_call(
        _mix_kernel,
        grid=(n // tm, d // tn),
        in_specs=[pl.BlockSpec((NA_HEADS, tm, HEAD_DIM), lambda i, j: (0, i, 0)), *groups, *groups,
                  pl.BlockSpec((tm, d), lambda i, j: (i, 0)),
                  col(NA_WIDTH), col(DIL_OUT_WIDTH),
                  gate(d, ga0), gate(d, gb0), gate(1, ga0), gate(1, gb0)],
        out_specs=pl.BlockSpec((tm, tn), lambda i, j: (i, j)),
        out_shape=jax.ShapeDtypeStruct((n, d), BF16),
        scratch_shapes=[pltpu.VMEM((tm, NA_WIDTH), BF16), pltpu.VMEM((tm, DIL_OUT_WIDTH), BF16),
                        pltpu.VMEM((ngroups, gh, tm, HEAD_DIM), F32),
                        pltpu.VMEM((ngroups, gh, tm, HEAD_DIM), F32)],
        compiler_params=_params(("parallel", "arbitrary")),
        name="mix",
    )(na, *dil_o, *dil_lse, hb, wpa, wpb, w_all, w_all, b_all, b_all)


def _outproj_kernel(m_ref, h_ref, wo_ref, bo_ref, g_ref, b_ref, wrh_ref, wrl_ref, br_ref,
                    h1_ref, h1p_hbm, lg_ref, hbuf, sem):
    i = pl.program_id(0)
    slot = i % 2

    def out_copies(step, sl):
        return _tile_copies(h1p_hbm, step * OUT_TM, OUT_TM, hbuf.at[sl], sem.at[sl], True)

    mix = jnp.dot(m_ref[...], wo_ref[...], preferred_element_type=F32) + bo_ref[...]
    h1 = _layer_norm(DN_ALPHA * h_ref[...] + mix, g_ref[...], b_ref[...])
    h1_ref[...] = h1

    @pl.when(i >= 2)
    def _():
        for cp in out_copies(i - 2, slot):
            cp.wait()

    _pack_rows(hbuf.at[slot], h1)
    for cp in out_copies(i, slot):
        cp.start()

    @pl.when(i == pl.num_programs(0) - 1)
    def _():
        for cp in out_copies(i, slot):
            cp.wait()

        @pl.when(i >= 1)
        def _():
            for cp in out_copies(i - 1, 1 - slot):
                cp.wait()

    hi = h1.astype(BF16)
    lo = (h1 - hi.astype(F32)).astype(BF16)
    lg = (jnp.dot(hi, wrh_ref[...], preferred_element_type=F32)
          + jnp.dot(lo, wrh_ref[...], preferred_element_type=F32)
          + jnp.dot(hi, wrl_ref[...], preferred_element_type=F32))
    lg_ref[...] = lg + br_ref[...]


def _outproj(m, h, wo, bo, g, b, wr_hi, wr_lo, br):
    n, d = h.shape
    tm = OUT_TM
    row = lambda w: pl.BlockSpec((tm, w), lambda i: (i, 0))
    full = lambda r, c: pl.BlockSpec((r, c), lambda i: (0, 0), pipeline_mode=pl.Buffered(1))
    return pl.pallas_call(
        _outproj_kernel,
        grid=(n // tm,),
        in_specs=[row(d), row(d), full(d, d), full(1, d), full(1, d), full(1, d),
                  full(d, ROUTER_PAD), full(d, ROUTER_PAD), full(1, ROUTER_PAD)],
        out_specs=[row(d), pl.BlockSpec(memory_space=pl.ANY), row(ROUTER_PAD)],
        out_shape=[jax.ShapeDtypeStruct((n, d), F32),
                   jax.ShapeDtypeStruct((n, PACKED, LANES), jnp.uint32),
                   jax.ShapeDtypeStruct((n, ROUTER_PAD), F32)],
        scratch_shapes=[pltpu.VMEM((2, PACKED, tm, LANES), jnp.uint32), pltpu.SemaphoreType.DMA((2,))],
        compiler_params=_params(("arbitrary",)),
        name="outproj",
    )(m, h, wo, bo.reshape(1, d), g.reshape(1, d), b.reshape(1, d), wr_hi, wr_lo, br)


def _first_argmax(vals, lane_f):
    top = jnp.max(vals, axis=-1, keepdims=True)
    idx = jnp.min(jnp.where(vals == top, lane_f, float(LANES)), axis=-1, keepdims=True)
    return top, idx


def _route_kernel(lg_ref, info_ref, gate_ref, cnt_ref, base_s):
    @pl.when(pl.program_id(0) == 0)
    def _():
        base_s[...] = jnp.zeros(base_s.shape, F32)

    lg = lg_ref[...]
    lane = lax.broadcasted_iota(jnp.int32, lg.shape, 1)
    lane_f = lane.astype(F32)
    g_mask = lane < N_GROUPS
    g_top, g_sel = _first_argmax(jnp.where(g_mask, lg, MASK_VALUE), lane_f)
    g_prob = 1.0 / jnp.sum(jnp.where(g_mask, jnp.exp(lg - g_top), 0.0), axis=-1, keepdims=True)
    first = N_GROUPS + g_sel * EXPERTS_PER_GROUP
    e_mask = (lane_f >= first) & (lane_f < first + EXPERTS_PER_GROUP)
    el = jnp.where(e_mask, lg, MASK_VALUE)
    v0, i0 = _first_argmax(el, lane_f)
    v1, i1 = _first_argmax(jnp.where(lane_f == i0, MASK_VALUE, el), lane_f)
    e1 = jnp.exp(v1 - v0)
    w0 = g_prob / (1.0 + e1)
    w1 = g_prob * e1 / (1.0 + e1)

    tm = lg.shape[0]
    tri = (lax.broadcasted_iota(jnp.int32, (tm, tm), 1)
           < lax.broadcasted_iota(jnp.int32, (tm, tm), 0)).astype(BF16)
    ranks = []
    for idx in (i0, i1):
        onehot = lane_f == idx - N_GROUPS
        before = jnp.dot(tri, onehot.astype(BF16), preferred_element_type=F32) + base_s[...]
        ranks.append(jnp.sum(jnp.where(onehot, before, 0.0), axis=-1, keepdims=True))
        base_s[...] = base_s[...] + jnp.sum(onehot.astype(F32), axis=0, keepdims=True)

    info = jnp.where(lane == 0, i0 - N_GROUPS,
                     jnp.where(lane == 1, i1 - N_GROUPS,
                               jnp.where(lane == 2, ranks[0], jnp.where(lane == 3, ranks[1], 0.0))))
    info_ref[...] = info.astype(jnp.int32)
    gate_ref[...] = jnp.where(lane == 0, w0, jnp.where(lane == 1, w1, 0.0))
    cnt_ref[...] = base_s[...]


def _route(logits, n):
    tm = ROUTE_TM
    row = pl.BlockSpec((tm, ROUTER_PAD), lambda i: (i, 0))
    one = pl.BlockSpec((1, ROUTER_PAD), lambda i: (0, 0))
    info, gate, counts = pl.pallas_call(
        _route_kernel,
        grid=(n // tm,),
        in_specs=[row],
        out_specs=[row, row, one],
        out_shape=[jax.ShapeDtypeStruct((n, ROUTER_PAD), jnp.int32),
                   jax.ShapeDtypeStruct((n, ROUTER_PAD), F32),
                   jax.ShapeDtypeStruct((1, ROUTER_PAD), F32)],
        scratch_shapes=[pltpu.VMEM((1, ROUTER_PAD), F32)],
        compiler_params=_params(("arbitrary",)),
        name="route",
    )(logits)
    gate = gate[:, :TOP_K]
    a = n * TOP_K
    e_flat = info[:, :TOP_K].reshape(a)
    rank = info[:, TOP_K:2 * TOP_K].reshape(a)
    counts = counts[0, :N_EXPERTS].astype(jnp.int32)
    pcounts = (counts + MOE_TB - 1) // MOE_TB * MOE_TB
    pends = jnp.cumsum(pcounts)
    pstarts = pends - pcounts
    dest = (pstarts[e_flat] + rank).astype(jnp.int32)
    nb = a // MOE_TB + N_EXPERTS
    slots = nb * MOE_TB
    tok = jnp.arange(a, dtype=jnp.int32) // TOP_K
    slot_tok = jnp.zeros((slots,), jnp.int32).at[dest].set(tok)
    block_start = jnp.arange(nb, dtype=jnp.int32) * MOE_TB
    block_e = jnp.minimum(jnp.sum((pends[None, :] <= block_start[:, None]).astype(jnp.int32), axis=1),
                          N_EXPERTS - 1)
    n_active = (pends[-1] // MOE_TB).astype(jnp.int32).reshape(1)
    ids = jnp.arange(N_EXPERTS, dtype=jnp.int32)
    later = (counts[None, :] > 0) & (ids[None, :] > ids[:, None])
    nxt = jnp.min(jnp.where(later, ids[None, :], N_EXPERTS), axis=1)
    block_next = jnp.where(nxt == N_EXPERTS, -1, nxt)[block_e].astype(jnp.int32)
    return dest, gate, slot_tok, block_e, block_next, n_active


def _cast_weights(src, dst):
    rows, cols = src.shape
    step = CAST_VREGS * 8 * LANES // cols

    def body(r, c):
        sl = pl.ds(pl.multiple_of(r * step, step), step)
        dst[sl, :] = src[sl, :].astype(BF16)
        return c

    lax.fori_loop(0, rows // step, body, 0, unroll=2)


def _expert_kernel(be_ref, nx_ref, nact_ref, tok_ref, h1p_hbm, wg_hbm, wu_hbm, wd_hbm, ys_hbm,
                   xbuf, ybuf, wg_f, wu_f, wd_f, wg_s, wu_s, wd_s, gsem, osem, wsem):
    i = pl.program_id(0)
    slot = i % 2
    xslot = i % GATHER_SLOTS
    nact = nact_ref[0]

    def gather(step, inline=False):
        sl = step % GATHER_SLOTS
        _start_row_gather(tok_ref, step * MOE_TB, 1, MOE_TB, h1p_hbm, xbuf.at[sl], gsem.at[sl], False,
                          inline)

    def out_copies(step, sl):
        return _tile_copies(ys_hbm, step * MOE_TB, MOE_TB, ybuf.at[sl], osem.at[sl], True)

    def weight_copies(e):
        return [pltpu.make_async_copy(src.at[e], dst, wsem)
                for src, dst in ((wg_hbm, wg_f), (wu_hbm, wu_f), (wd_hbm, wd_f))]

    @pl.when((i == 0) & (nact > 0))
    def _():
        for cp in weight_copies(be_ref[0]):
            cp.start(priority=WEIGHT_DMA_PRIORITY)
        gather(0)

    @pl.when((i == 0) & (nact > 1))
    def _():
        gather(1)

    @pl.when(i >= 2)
    def _():
        for cp in out_copies(i - 2, slot):
            cp.wait()

    @pl.when(i < nact)
    def _():
        expert = be_ref[i]

        @pl.when((i == 0) | (expert != be_ref[jnp.maximum(i - 1, 0)]))
        def _():
            for cp in weight_copies(expert):
                cp.wait()
            _cast_weights(wg_f, wg_s)
            _cast_weights(wu_f, wu_s)
            _cast_weights(wd_f, wd_s)

            @pl.when(nx_ref[i] >= 0)
            def _():
                for cp in weight_copies(nx_ref[i]):
                    cp.start(priority=WEIGHT_DMA_PRIORITY)

        _wait_row_gather(xbuf.at[xslot], gsem.at[xslot])

        def block(prefetch):
            if prefetch:
                gather(i + 2, inline=True)
            xb = _unpack_rows(xbuf.at[xslot]).astype(BF16)
            gate = jnp.dot(xb, wg_s[...], preferred_element_type=F32)
            up = jnp.dot(xb, wu_s[...], preferred_element_type=F32)
            hid = (jax.nn.silu(gate) * up).astype(BF16)
            _pack_rows(ybuf.at[slot], jnp.dot(hid, wd_s[...], preferred_element_type=F32))

        pl.when(i + 2 < nact)(functools.partial(block, True))
        pl.when(i + 2 >= nact)(functools.partial(block, False))

    @pl.when(i >= nact)
    def _():
        ybuf[slot] = jnp.zeros(ybuf.shape[1:], jnp.uint32)

    for cp in out_copies(i, slot):
        cp.start()

    @pl.when(i == pl.num_programs(0) - 1)
    def _():
        for cp in out_copies(i, slot):
            cp.wait()

        @pl.when(i >= 1)
        def _():
            for cp in out_copies(i - 1, 1 - slot):
                cp.wait()


def _experts(h1p, slot_tok, block_e, block_next, n_active, w_gate, w_up, w_down):
    d = D_MODEL
    slots = slot_tok.shape[0]
    nb = slots // MOE_TB
    any_spec = pl.BlockSpec(memory_space=pl.ANY)
    grid_spec = pltpu.PrefetchScalarGridSpec(
        num_scalar_prefetch=4,
        grid=(nb,),
        in_specs=[any_spec, any_spec, any_spec, any_spec],
        out_specs=any_spec,
        scratch_shapes=[pltpu.VMEM((GATHER_SLOTS, PACKED, MOE_TB, LANES), jnp.uint32),
                        pltpu.VMEM((2, PACKED, MOE_TB, LANES), jnp.uint32),
                        pltpu.VMEM((d, D_EXPERT), F32), pltpu.VMEM((d, D_EXPERT), F32),
                        pltpu.VMEM((D_EXPERT, d), F32),
                        pltpu.VMEM((d, D_EXPERT), BF16), pltpu.VMEM((d, D_EXPERT), BF16),
                        pltpu.VMEM((D_EXPERT, d), BF16),
                        pltpu.SemaphoreType.DMA((GATHER_SLOTS,)), pltpu.SemaphoreType.DMA((2,)),
                        pltpu.SemaphoreType.DMA(())])
    return pl.pallas_call(
        _expert_kernel,
        grid_spec=grid_spec,
        out_shape=jax.ShapeDtypeStruct((slots, PACKED, LANES), jnp.uint32),
        compiler_params=_params(("arbitrary",)),
        name="experts",
    )(block_e, block_next, n_active, slot_tok, h1p, w_gate, w_up, w_down)


def _final_kernel(dest_ref, ys_hbm, h1_ref, gate_ref, g_ref, b_ref, o_ref, ybuf, gsem):
    i = pl.program_id(0)
    steps = pl.num_programs(0)
    slot = i % GATHER_SLOTS

    def fetch(step, inline=False):
        sl = step % GATHER_SLOTS
        for k in range(TOP_K):
            _start_row_gather(dest_ref, step * FIN_TM * TOP_K + k, TOP_K, FIN_TM, ys_hbm,
                              ybuf.at[sl, k], gsem.at[sl], True, inline)

    @pl.when(i == 0)
    def _():
        fetch(0)

    @pl.when((i == 0) & (steps > 1))
    def _():
        fetch(1)

    _wait_row_gather(ybuf.at[slot], gsem.at[slot])

    def combine(prefetch):
        if prefetch:
            fetch(i + 2, inline=True)
        gate = gate_ref[...]
        ffn = (_unpack_rows(ybuf.at[slot, 0]) * gate[:, 0:1]
               + _unpack_rows(ybuf.at[slot, 1]) * gate[:, 1:2])
        o_ref[...] = _layer_norm(DN_ALPHA * h1_ref[...] + ffn, g_ref[...], b_ref[...])

    pl.when(i + 2 < steps)(functools.partial(combine, True))
    pl.when(i + 2 >= steps)(functools.partial(combine, False))


def _final(dest, ys, h1, gate, g, b):
    n, d = h1.shape
    grid_spec = pltpu.PrefetchScalarGridSpec(
        num_scalar_prefetch=1,
        grid=(n // FIN_TM,),
        in_specs=[pl.BlockSpec(memory_space=pl.ANY),
                  pl.BlockSpec((FIN_TM, d), lambda i, ds: (i, 0)),
                  pl.BlockSpec((FIN_TM, TOP_K), lambda i, ds: (i, 0)),
                  pl.BlockSpec((1, d), lambda i, ds: (0, 0)),
                  pl.BlockSpec((1, d), lambda i, ds: (0, 0))],
        out_specs=pl.BlockSpec((FIN_TM, d), lambda i, ds: (i, 0)),
        scratch_shapes=[pltpu.VMEM((GATHER_SLOTS, TOP_K, PACKED, FIN_TM, LANES), jnp.uint32),
                        pltpu.SemaphoreType.DMA((GATHER_SLOTS,))])
    return pl.pallas_call(
        _final_kernel,
        grid_spec=grid_spec,
        out_shape=jax.ShapeDtypeStruct((n, d), F32),
        compiler_params=_params(("arbitrary",)),
        name="final",
    )(dest, ys, h1, gate, g.reshape(1, d), b.reshape(1, d))


def kernel(x, ln0_g, ln0_b, w_in, b_in, rpb, w_proj_a, w_proj_b, w_o, b_o, ln1_g, ln1_b,
           w_router_group, b_router_group, w_router_expert, b_router_expert,
           w_gate, w_up, w_down, ln2_g, ln2_b):
    bsz, s, d = x.shape
    n = bsz * s
    assert d == D_MODEL and w_in.shape[0] == DEPTH
    scale = HEAD_DIM ** -0.5

    col_scale = np.ones((w_in.shape[2],), np.float32)
    col_scale[:NA_WIDTH] = scale
    col_scale[3 * NA_WIDTH:3 * NA_WIDTH + DIL_WIDTH] = scale
    w_all = (w_in[0] * col_scale).astype(BF16)
    b_all = (b_in[0] * col_scale).reshape(1, -1)
    assert DIL_OUT_WIDTH == INPROJ_TN
    na_blocks, ngroups = 3 * NA_WIDTH // INPROJ_TN, len(DIL_GROUPS)
    w_r = jnp.concatenate([w_router_group[0], w_router_expert[0]], axis=1)
    w_r = jnp.pad(w_r, ((0, 0), (0, ROUTER_PAD - w_r.shape[1])))
    w_r_hi = w_r.astype(BF16)
    w_r_lo = (w_r - w_r_hi.astype(F32)).astype(BF16)
    b_r = jnp.pad(jnp.concatenate([b_router_group[0], b_router_expert[0]]),
                  (0, ROUTER_PAD - N_GROUPS - N_EXPERTS)).reshape(1, ROUTER_PAD)
    slopes = _alibi_slopes(DIL_HEADS)

    h, hb = _ln0(x.reshape(n, d), ln0_g, ln0_b)
    zq = [_inproj(hb, w_all, b_all, bsz, s, DIL_GROUPS[0][1], 3 * NA_WIDTH + 3 * DIL_OUT_WIDTH,
                  lambda j: jnp.where(j < na_blocks, j, na_blocks + (j - na_blocks) * ngroups))]
    for g in range(1, ngroups):
        zq.append(_inproj(hb, w_all, b_all, bsz, s, DIL_GROUPS[g][1], 3 * DIL_OUT_WIDTH,
                          lambda j, g=g: na_blocks + j * ngroups + g))
    na = _neighborhood_attention(zq[0].reshape(-1, bsz, s, HEAD_DIM), rpb[0])
    na = na.reshape(NA_HEADS, n, HEAD_DIM)
    dil = [_dilated_group(zq[g], DIL0_H if g == 0 else 0, g, slopes) for g in range(ngroups)]
    m = _mix(na, [o for o, _ in dil], [l for _, l in dil], hb,
             w_proj_a[0].astype(BF16), w_proj_b[0].astype(BF16), w_all, b_all, s)
    h1, h1p, logits = _outproj(m, h, w_o[0].astype(BF16), b_o[0], ln1_g[0], ln1_b[0],
                               w_r_hi, w_r_lo, b_r)
    dest, gate, slot_tok, block_e, block_next, n_active = _route(logits, n)
    ys = _experts(h1p, slot_tok, block_e, block_next, n_active, w_gate[0], w_up[0], w_down[0])
    out = _final(dest, ys, h1, gate, ln2_g[0], ln2_b[0])
    return out.reshape(bsz, s, d)
```

```python
import functools

import numpy as np
import jax
import jax.numpy as jnp
from jax import lax
from jax.experimental import pallas as pl
from jax.experimental.pallas import tpu as pltpu

F32 = jnp.float32
BF16 = jnp.bfloat16

D_MODEL = 2048
HEAD_DIM = 128
GRID_W = 64
NA_HEADS = 8
NA_KH = 8
NA_KW = 16
DIL_GROUPS = ((128, 1), (512, 4), (2048, 16))
DIL_HEADS_PER_GROUP = 4
DIL_HEADS = DIL_HEADS_PER_GROUP * len(DIL_GROUPS)
N_GROUPS = 8
EXPERTS_PER_GROUP = 8
N_EXPERTS = N_GROUPS * EXPERTS_PER_GROUP
TOP_K = 2
D_EXPERT = D_MODEL // 4
LN_EPS = 1e-5
DEPTH = 1
DN_ALPHA = (2 * DEPTH) ** 0.25
NA_WIDTH = NA_HEADS * HEAD_DIM
DIL_WIDTH = DIL_HEADS * HEAD_DIM
DIL_OUT_WIDTH = DIL_HEADS_PER_GROUP * HEAD_DIM
QKV_WIDTH = 3 * NA_WIDTH + 3 * DIL_WIDTH
QKV_HEADS = QKV_WIDTH // HEAD_DIM
LANES = 128
CHUNKS = D_MODEL // LANES
PACKED = CHUNKS // 2
MASK_VALUE = -1e30

QA_H, KA_H, VA_H = 0, NA_HEADS, 2 * NA_HEADS
DIL0_H = 3 * NA_HEADS

LN_TM = 512
INPROJ_TM, INPROJ_TN = 2048, 512
NA_QROWS = 4
NA_KROWS = 12
NA_NSUB = 8
DIL_NSUB = 8
DIL_TQ = 256
DIL_HALF = 64
DIL_TK = DIL_TQ + 2 * DIL_HALF
MIX_TM, MIX_TN = 512, 512
OUT_TM = 512
ROUTER_PAD = 128
ROUTE_TM = 512
MOE_TB = 128
FIN_TM = 128
CAST_VREGS = 32
GATHER_SLOTS = 3
WEIGHT_DMA_PRIORITY = 1
VMEM_LIMIT = 56 * 1024 * 1024


def _params(sem, limit=VMEM_LIMIT):
    return pltpu.CompilerParams(dimension_semantics=sem, vmem_limit_bytes=limit)


def _layer_norm(x, g, b):
    mu = jnp.mean(x, axis=-1, keepdims=True)
    xc = x - mu
    var = jnp.mean(xc * xc, axis=-1, keepdims=True)
    return xc * lax.rsqrt(var + LN_EPS) * g + b


HIGH_HALF = np.uint32(0xFFFF0000)


def _bf16_bits(x):
    return lax.bitcast_convert_type(x.astype(BF16).astype(F32), jnp.uint32)


def _pack_rows(ref, val):
    for c in range(PACKED):
        lo = _bf16_bits(val[:, c * LANES:(c + 1) * LANES])
        hi = _bf16_bits(val[:, (c + PACKED) * LANES:(c + PACKED + 1) * LANES])
        ref[c] = (lo >> 16) | (hi & HIGH_HALF)


def _unpack_rows(ref):
    words = [ref[c] for c in range(PACKED)]
    lo = [lax.bitcast_convert_type(w << 16, F32) for w in words]
    hi = [lax.bitcast_convert_type(w & HIGH_HALF, F32) for w in words]
    return jnp.concatenate(lo + hi, axis=-1)


def _tile_copies(hbm, row0, rows, vmem, sem, to_hbm):
    copies = []
    for c in range(PACKED):
        h, v = hbm.at[pl.ds(row0, rows), c], vmem.at[c]
        copies.append(pltpu.make_async_copy(v, h, sem) if to_hbm else pltpu.make_async_copy(h, v, sem))
    return copies


def _start_row_gather(idx_ref, base, step, count, src_hbm, dst, sem, both_threads, inline=False):
    def start(j, c):
        for half in range(2):
            t = 2 * j + half
            pltpu.make_async_copy(src_hbm.at[idx_ref[base + t * step]], dst.at[:, t], sem).start(
                priority=half if both_threads else 0)
        return c

    if inline:
        for j in range(count // 2):
            start(j, 0)
    else:
        lax.fori_loop(0, count // 2, start, 0, unroll=2)


def _wait_row_gather(dst, sem):
    pltpu.make_async_copy(dst, dst, sem).wait()


def _ln0_kernel(x_ref, g_ref, b_ref, h_ref, hb_ref):
    y = _layer_norm(x_ref[...], g_ref[...], b_ref[...])
    h_ref[...] = y
    hb_ref[...] = y.astype(BF16)


def _ln0(x, g, b):
    n, d = x.shape
    row = pl.BlockSpec((LN_TM, d), lambda i: (i, 0))
    vec = pl.BlockSpec((1, d), lambda i: (0, 0))
    return pl.pallas_call(
        _ln0_kernel,
        grid=(n // LN_TM,),
        in_specs=[row, vec, vec],
        out_specs=[row, row],
        out_shape=[jax.ShapeDtypeStruct((n, d), F32), jax.ShapeDtypeStruct((n, d), BF16)],
        compiler_params=_params(("parallel",)),
        name="ln0",
    )(x, g.reshape(1, d), b.reshape(1, d))


def _inproj_kernel(hb_ref, w_ref, b_ref, o_ref, acc_s, *, dil):
    acc = jnp.dot(hb_ref[...], w_ref[...], preferred_element_type=F32) + b_ref[...]
    if dil == 1:
        for c in range(INPROJ_TN // LANES):
            o_ref[c, 0, 0] = acc[:, c * LANES:(c + 1) * LANES].astype(BF16)
    else:
        for c in range(INPROJ_TN // LANES):
            acc_s[c] = acc[:, c * LANES:(c + 1) * LANES]
        for c in range(INPROJ_TN // LANES):
            for r in range(dil):
                o_ref[c, 0, r] = acc_s[c, pl.ds(r, INPROJ_TM // dil, stride=dil), :].astype(BF16)


def _inproj(hb, w, b, bsz, s, dil, width, col_block):
    n, d = hb.shape
    tiles = s // INPROJ_TM
    assert INPROJ_TM % (16 * dil) == 0 and s % INPROJ_TM == 0 and width % INPROJ_TN == 0
    return pl.pallas_call(
        functools.partial(_inproj_kernel, dil=dil),
        grid=(n // INPROJ_TM, width // INPROJ_TN),
        in_specs=[pl.BlockSpec((INPROJ_TM, d), lambda i, j: (i, 0)),
                  pl.BlockSpec((d, INPROJ_TN), lambda i, j: (0, col_block(j))),
                  pl.BlockSpec((1, INPROJ_TN), lambda i, j: (0, col_block(j)))],
        out_specs=pl.BlockSpec((INPROJ_TN // LANES, 1, dil, INPROJ_TM // dil, LANES),
                               lambda i, j: (j, i // tiles, 0, i % tiles, 0)),
        out_shape=jax.ShapeDtypeStruct((width // LANES, bsz, dil, s // dil, LANES), BF16),
        scratch_shapes=[pltpu.VMEM((INPROJ_TN // LANES, INPROJ_TM, LANES), F32)],
        compiler_params=_params(("parallel", "arbitrary")),
        name=f"inproj_{dil}",
    )(hb, w, b)


def _na_bias_tiles(rpb, rows):
    heads, n_dr, n_dc = rpb.shape
    qc = np.arange(GRID_W)[:, None]
    kc = np.arange(GRID_W)[None, :]
    qcs = np.clip(qc - NA_KW // 2, 0, GRID_W - NA_KW)
    v_col = (kc >= qcs) & (kc < qcs + NA_KW)
    dc = np.clip(kc - qc + NA_KW - 1, 0, n_dc - 1)
    onehot = (dc[None] == np.arange(n_dc)[:, None, None]) & v_col[None]
    toep = jnp.einsum('hrd,dqk->hrqk', rpb.astype(F32), jnp.asarray(onehot, F32),
                      precision=lax.Precision.HIGHEST)
    toep = jnp.where(jnp.asarray(v_col), toep, MASK_VALUE)
    masked = jnp.full((heads, 1, GRID_W, GRID_W), MASK_VALUE, F32)
    blocks = jnp.concatenate([toep, masked], axis=1)
    i = np.arange(NA_QROWS)[:, None]
    j = np.arange(NA_KROWS)[None, :]
    sel = []
    for r0, ks in ((0, 0), (2 * NA_QROWS, 2 * NA_QROWS - NA_KH // 2), (rows - NA_QROWS, rows - NA_KROWS)):
        r, krow = r0 + i, ks + j
        start = np.clip(r - NA_KH // 2, 0, rows - NA_KH)
        v_row = (krow >= start) & (krow < start + NA_KH)
        sel.append(np.where(v_row, krow - r + NA_KH - 1, n_dr))
    sel = np.stack(sel)
    pairs = sel.reshape(3, NA_QROWS, NA_KROWS // 2, 2)
    distinct = sorted({tuple(p) for p in pairs.reshape(-1, 2).tolist()})
    plan = [[[distinct.index(tuple(pairs[v, i, c])) for c in range(NA_KROWS // 2)]
             for i in range(NA_QROWS)] for v in range(3)]
    left = jnp.take(blocks, jnp.asarray([p[0] for p in distinct], jnp.int32), axis=1)
    right = jnp.take(blocks, jnp.asarray([p[1] for p in distinct], jnp.int32), axis=1)
    return jnp.concatenate([left, right], axis=-1), plan


def _attend_blocks(nblocks, scores, finish):
    scored, weighted = {}, {}
    for t in range(nblocks + 2):
        if t < nblocks:
            scored[t] = scores(t)
        if 0 <= t - 1 < nblocks:
            s, vw = scored.pop(t - 1)
            m = jnp.max(s, axis=-1, keepdims=True)
            p = jnp.exp(s - m)
            weighted[t - 1] = (p.astype(BF16), vw, m, jnp.sum(p, axis=-1, keepdims=True))
        if 0 <= t - 2 < nblocks:
            p, vw, m, l = weighted.pop(t - 2)
            finish(t - 2, jnp.dot(p, vw, preferred_element_type=F32) / l, m, l)


def _scores(q, kw, bias):
    return lax.dot_general(q, kw, (((1,), (1,)), ((), ())), preferred_element_type=F32) + bias


def _edge_variant(blk, nblk):
    return jnp.where(blk == 0, 0, jnp.where(blk == nblk - 1, 2, 1))


def _na_kernel(q_ref, k_ref, v_ref, tile_ref, o_ref, tab_ref, *, rows, plan):
    tq = NA_QROWS * GRID_W

    @pl.when(pl.program_id(2) == 0)
    def _():
        for v in range(3):
            for i in range(NA_QROWS):
                for c in range(NA_KROWS // 2):
                    tab_ref[v, 0, i * GRID_W:(i + 1) * GRID_W, c * 2 * GRID_W:(c + 1) * 2 * GRID_W] = (
                        tile_ref[0, plan[v][i][c]])

    def scores(u):
        blk = pl.program_id(2) * NA_NSUB + u
        ks = jnp.clip(blk * NA_QROWS - NA_KH // 2, 0, rows - NA_KROWS)
        off = pl.multiple_of(ks * GRID_W, GRID_W)
        kw = k_ref[0, 0, pl.ds(off, NA_KROWS * GRID_W), :]
        vw = v_ref[0, 0, pl.ds(off, NA_KROWS * GRID_W), :]
        bias = tab_ref[_edge_variant(blk, rows // NA_QROWS), 0]
        return _scores(q_ref[0, 0, u * tq:(u + 1) * tq, :], kw, bias), vw

    def finish(u, o, m, l):
        o_ref[0, 0, u * tq:(u + 1) * tq, :] = o.astype(BF16)

    _attend_blocks(NA_NSUB, scores, finish)


def _neighborhood_attention(zq4, rpb):
    _, bsz, s, _ = zq4.shape
    rows = s // GRID_W
    assert rows % (NA_QROWS * NA_NSUB) == 0 and rows >= NA_KROWS + NA_QROWS and NA_KROWS % 2 == 0
    tiles, plan = _na_bias_tiles(rpb, rows)
    tq = NA_QROWS * GRID_W * NA_NSUB
    return pl.pallas_call(
        functools.partial(_na_kernel, rows=rows, plan=plan),
        grid=(bsz, NA_HEADS, s // tq),
        in_specs=[pl.BlockSpec((1, 1, tq, HEAD_DIM), lambda b, h, i: (QA_H + h, b, i, 0)),
                  pl.BlockSpec((1, 1, s, HEAD_DIM), lambda b, h, i: (KA_H + h, b, 0, 0)),
                  pl.BlockSpec((1, 1, s, HEAD_DIM), lambda b, h, i: (VA_H + h, b, 0, 0)),
                  pl.BlockSpec((1,) + tiles.shape[1:], lambda b, h, i: (h, 0, 0, 0))],
        out_specs=pl.BlockSpec((1, 1, tq, HEAD_DIM), lambda b, h, i: (h, b, i, 0)),
        out_shape=jax.ShapeDtypeStruct((NA_HEADS, bsz, s, HEAD_DIM), BF16),
        scratch_shapes=[pltpu.VMEM((3, 1, NA_QROWS * GRID_W, NA_KROWS * GRID_W), F32)],
        compiler_params=_params(("arbitrary", "arbitrary", "arbitrary")),
        name="na_attn",
    )(zq4, zq4, zq4, tiles)


def _alibi_slopes(n):
    return np.array([2.0 ** (-8.0 * (i + 1) / n) for i in range(n)], dtype=np.float32)


def _dil_bias_tables(slopes, dil):
    qi = np.arange(DIL_TQ)[:, None]
    kj = np.arange(DIL_TK)[None, :]
    dist = np.stack([np.abs(kj - qi + shift) for shift in (0, -DIL_HALF, -2 * DIL_HALF)])
    dist = jnp.asarray(dist, F32)[:, None]
    penalty = jnp.asarray(slopes, F32)[None, :, None, None] * (dist * float(dil))
    return jnp.where(dist <= DIL_HALF, -penalty, MASK_VALUE)


def _dil_kernel(q_ref, k_ref, v_ref, tab_ref, o_ref, lse_ref, *, length, nsub, nres):
    def scores(u):
        res, sub = divmod(u, nsub)
        blk = pl.program_id(3) * nsub + sub
        ks = jnp.clip(blk * DIL_TQ - DIL_HALF, 0, length - DIL_TK)
        off = pl.multiple_of(ks, DIL_HALF)
        kw = k_ref[0, 0, res, pl.ds(off, DIL_TK), :]
        vw = v_ref[0, 0, res, pl.ds(off, DIL_TK), :]
        bias = tab_ref[_edge_variant(blk, length // DIL_TQ), 0]
        return _scores(q_ref[0, 0, res, sub * DIL_TQ:(sub + 1) * DIL_TQ, :], kw, bias), vw

    def finish(u, o, m, l):
        res, sub = divmod(u, nsub)
        rows = slice(sub * DIL_TQ, (sub + 1) * DIL_TQ)
        o_ref[0, 0, res, rows, :] = o.astype(BF16)
        lse_ref[0, 0, res, rows, :] = jnp.broadcast_to(m + jnp.log(l), (DIL_TQ, LANES))

    _attend_blocks(nres * nsub, scores, finish)


def _dilated_group(zq, base, g, slopes):
    window, dil = DIL_GROUPS[g]
    assert (window // 2) // dil == DIL_HALF
    _, bsz, _, length, _ = zq.shape
    nsub = min(DIL_NSUB, length // DIL_TQ)
    nres = min(dil, DIL_NSUB // nsub)
    assert zq.shape[2] == dil and length % (DIL_TQ * nsub) == 0 and length >= DIL_TK and dil % nres == 0
    gh = DIL_HEADS_PER_GROUP
    tabs = _dil_bias_tables(slopes[g * gh:(g + 1) * gh], dil)
    tq = DIL_TQ * nsub
    qspec = pl.BlockSpec((1, 1, nres, tq, HEAD_DIM), lambda b, h, r, n: (base + h, b, r, n, 0))
    kvspec = lambda off: pl.BlockSpec((1, 1, nres, length, HEAD_DIM),
                                      lambda b, h, r, n: (base + off + h, b, r, 0, 0))
    ospec = pl.BlockSpec((1, 1, nres, tq, HEAD_DIM), lambda b, h, r, n: (h, b, r, n, 0))
    return pl.pallas_call(
        functools.partial(_dil_kernel, length=length, nsub=nsub, nres=nres),
        grid=(bsz, gh, dil // nres, length // tq),
        in_specs=[qspec, kvspec(gh), kvspec(2 * gh),
                  pl.BlockSpec((3, 1, DIL_TQ, DIL_TK), lambda b, h, r, n: (0, h, 0, 0))],
        out_specs=[ospec, ospec],
        out_shape=[jax.ShapeDtypeStruct((gh, bsz, dil, length, HEAD_DIM), BF16),
                   jax.ShapeDtypeStruct((gh, bsz, dil, length, HEAD_DIM), F32)],
        compiler_params=_params(("parallel", "parallel", "parallel", "arbitrary")),
        name=f"dil_attn_{dil}",
    )(zq, zq, zq, tabs)


def _mix_kernel(na_ref, o1_ref, o2_ref, o3_ref, l1_ref, l2_ref, l3_ref, hb_ref,
                wpa_ref, wpb_ref, wga_ref, wgb_ref, bga_ref, bgb_ref, m_ref, na_s, dil_s, o_s, l_s):
    @pl.when(pl.program_id(1) == 0)
    def _():
        for h in range(NA_HEADS):
            na_s[:, h * HEAD_DIM:(h + 1) * HEAD_DIM] = na_ref[h]
        for g, (o_ref, l_ref) in enumerate(((o1_ref, l1_ref), (o2_ref, l2_ref), (o3_ref, l3_ref))):
            dil = DIL_GROUPS[g][1]
            for h in range(DIL_HEADS_PER_GROUP):
                for r in range(dil):
                    rows = slice(None) if dil == 1 else pl.ds(r, MIX_TM // dil, stride=dil)
                    o_s[g, h, rows, :] = o_ref[h, 0, r].astype(F32)
                    l_s[g, h, rows, :] = l_ref[h, 0, r]
        for h in range(DIL_HEADS_PER_GROUP):
            ls = [l_s[g, h] for g in range(len(DIL_GROUPS))]
            mx = jnp.maximum(jnp.maximum(ls[0], ls[1]), ls[2])
            es = [jnp.exp(l - mx) for l in ls]
            den = es[0] + es[1] + es[2]
            acc = (es[0] * o_s[0, h] + es[1] * o_s[1, h] + es[2] * o_s[2, h]) / den
            dil_s[:, h * HEAD_DIM:(h + 1) * HEAD_DIM] = acc.astype(BF16)

    hb = hb_ref[...]
    ya = jnp.dot(na_s[...], wpa_ref[...], preferred_element_type=F32)
    yb = jnp.dot(dil_s[...], wpb_ref[...], preferred_element_type=F32)
    ga = jnp.dot(hb, wga_ref[...], preferred_element_type=F32) + bga_ref[...]
    gb = jnp.dot(hb, wgb_ref[...], preferred_element_type=F32) + bgb_ref[...]
    m_ref[...] = (jax.nn.sigmoid(ga) * ya + jax.nn.sigmoid(gb) * yb).astype(BF16)


def _mix(na, dil_o, dil_lse, hb, wpa, wpb, w_all, b_all, s):
    n, d = hb.shape
    tm, tn = MIX_TM, MIX_TN
    tiles = s // tm
    gh = DIL_HEADS_PER_GROUP
    ngroups = len(DIL_GROUPS)
    assert all(tm % (16 * dil) == 0 for _, dil in DIL_GROUPS) and s % tm == 0
    col = lambda k: pl.BlockSpec((k, tn), lambda i, j: (0, j))
    ga0, gb0 = QKV_WIDTH // tn, (QKV_WIDTH + d) // tn
    gate = lambda k, first: pl.BlockSpec((k, tn), lambda i, j: (0, first + j))
    grp = lambda dil: pl.BlockSpec((gh, 1, dil, tm // dil, HEAD_DIM),
                                   lambda i, j: (0, i // tiles, 0, i % tiles, 0))
    groups = [grp(dil) for _, dil in DIL_GROUPS]
    return pl.pallas_call(
        _mix_kernel,
        grid=(n // tm, d // tn),
        in_specs=[pl.BlockSpec((NA_HEADS, tm, HEAD_DIM), lambda i, j: (0, i, 0)), *groups, *groups,
                  pl.BlockSpec((tm, d), lambda i, j: (i, 0)),
                  col(NA_WIDTH), col(DIL_OUT_WIDTH),
                  gate(d, ga0), gate(d, gb0), gate(1, ga0), gate(1, gb0)],
        out_specs=pl.BlockSpec((tm, tn), lambda i, j: (i, j)),
        out_shape=jax.ShapeDtypeStruct((n, d), BF16),
        scratch_shapes=[pltpu.VMEM((tm, NA_WIDTH), BF16), pltpu.VMEM((tm, DIL_OUT_WIDTH), BF16),
                        pltpu.VMEM((ngroups, gh, tm, HEAD_DIM), F32),
                        pltpu.VMEM((ngroups, gh, tm, HEAD_DIM), F32)],
        compiler_params=_params(("parallel", "arbitrary")),
        name="mix",
    )(na, *dil_o, *dil_lse, hb, wpa, wpb, w_all, w_all, b_all, b_all)


def _outproj_kernel(m_ref, h_ref, wo_ref, bo_ref, g_ref, b_ref, wrh_ref, wrl_ref, br_ref,
                    h1_ref, h1p_hbm, lg_ref, hbuf, sem):
    i = pl.program_id(0)
    slot = i % 2

    def out_copies(step, sl):
        return _tile_copies(h1p_hbm, step * OUT_TM, OUT_TM, hbuf.at[sl], sem.at[sl], True)

    mix = jnp.dot(m_ref[...], wo_ref[...], preferred_element_type=F32) + bo_ref[...]
    h1 = _layer_norm(DN_ALPHA * h_ref[...] + mix, g_ref[...], b_ref[...])
    h1_ref[...] = h1

    @pl.when(i >= 2)
    def _():
        for cp in out_copies(i - 2, slot):
            cp.wait()

    _pack_rows(hbuf.at[slot], h1)
    for cp in out_copies(i, slot):
        cp.start()

    @pl.when(i == pl.num_programs(0) - 1)
    def _():
        for cp in out_copies(i, slot):
            cp.wait()

        @pl.when(i >= 1)
        def _():
            for cp in out_copies(i - 1, 1 - slot):
                cp.wait()

    hi = h1.astype(BF16)
    lo = (h1 - hi.astype(F32)).astype(BF16)
    lg = (jnp.dot(hi, wrh_ref[...], preferred_element_type=F32)
          + jnp.dot(lo, wrh_ref[...], preferred_element_type=F32)
          + jnp.dot(hi, wrl_ref[...], preferred_element_type=F32))
    lg_ref[...] = lg + br_ref[...]


def _outproj(m, h, wo, bo, g, b, wr_hi, wr_lo, br):
    n, d = h.shape
    tm = OUT_TM
    row = lambda w: pl.BlockSpec((tm, w), lambda i: (i, 0))
    full = lambda r, c: pl.BlockSpec((r, c), lambda i: (0, 0), pipeline_mode=pl.Buffered(1))
    return pl.pallas_call(
        _outproj_kernel,
        grid=(n // tm,),
        in_specs=[row(d), row(d), full(d, d), full(1, d), full(1, d), full(1, d),
                  full(d, ROUTER_PAD), full(d, ROUTER_PAD), full(1, ROUTER_PAD)],
        out_specs=[row(d), pl.BlockSpec(memory_space=pl.ANY), row(ROUTER_PAD)],
        out_shape=[jax.ShapeDtypeStruct((n, d), F32),
                   jax.ShapeDtypeStruct((n, PACKED, LANES), jnp.uint32),
                   jax.ShapeDtypeStruct((n, ROUTER_PAD), F32)],
        scratch_shapes=[pltpu.VMEM((2, PACKED, tm, LANES), jnp.uint32), pltpu.SemaphoreType.DMA((2,))],
        compiler_params=_params(("arbitrary",)),
        name="outproj",
    )(m, h, wo, bo.reshape(1, d), g.reshape(1, d), b.reshape(1, d), wr_hi, wr_lo, br)


def _first_argmax(vals, lane_f):
    top = jnp.max(vals, axis=-1, keepdims=True)
    idx = jnp.min(jnp.where(vals == top, lane_f, float(LANES)), axis=-1, keepdims=True)
    return top, idx


def _route_kernel(lg_ref, gate_ref, dest_ref, cnt_ref, base_s, start_s):
    phase = pl.program_id(0)

    @pl.when(pl.program_id(1) == 0)
    def _():
        @pl.when(phase == 1)
        def _():
            counts = base_s[...]
            cnt_ref[...] = counts
            blocks = jnp.ceil(counts * (1.0 / MOE_TB))
            before = (lax.broadcasted_iota(jnp.int32, (ROUTER_PAD, ROUTER_PAD), 0)
                      < lax.broadcasted_iota(jnp.int32, (ROUTER_PAD, ROUTER_PAD), 1)).astype(BF16)
            starts = jnp.dot(jnp.broadcast_to(blocks, (8, ROUTER_PAD)).astype(BF16), before,
                             preferred_element_type=F32)
            start_s[...] = starts[0:1] * float(MOE_TB)

        base_s[...] = jnp.zeros(base_s.shape, F32)

    lg = lg_ref[...]
    lane = lax.broadcasted_iota(jnp.int32, lg.shape, 1)
    lane_f = lane.astype(F32)
    g_mask = lane < N_GROUPS
    g_top, g_sel = _first_argmax(jnp.where(g_mask, lg, MASK_VALUE), lane_f)
    g_prob = 1.0 / jnp.sum(jnp.where(g_mask, jnp.exp(lg - g_top), 0.0), axis=-1, keepdims=True)
    first = N_GROUPS + g_sel * EXPERTS_PER_GROUP
    e_mask = (lane_f >= first) & (lane_f < first + EXPERTS_PER_GROUP)
    el = jnp.where(e_mask, lg, MASK_VALUE)
    v0, i0 = _first_argmax(el, lane_f)
    v1, i1 = _first_argmax(jnp.where(lane_f == i0, MASK_VALUE, el), lane_f)
    e1 = jnp.exp(v1 - v0)
    w0 = g_prob / (1.0 + e1)
    w1 = g_prob * e1 / (1.0 + e1)

    onehots = [lane_f == idx - N_GROUPS for idx in (i0, i1)]

    @pl.when(phase == 0)
    def _():
        base_s[...] = base_s[...] + sum(jnp.sum(oh.astype(F32), axis=0, keepdims=True) for oh in onehots)

    @pl.when(phase == 1)
    def _():
        tm = lg.shape[0]
        tri = (lax.broadcasted_iota(jnp.int32, (tm, tm), 1)
               < lax.broadcasted_iota(jnp.int32, (tm, tm), 0)).astype(BF16)
        dests = []
        for onehot in onehots:
            before = jnp.dot(tri, onehot.astype(BF16), preferred_element_type=F32) + base_s[...]
            dests.append(jnp.sum(jnp.where(onehot, before + start_s[...], 0.0), axis=-1, keepdims=True))
            base_s[...] = base_s[...] + jnp.sum(onehot.astype(F32), axis=0, keepdims=True)
        gate_ref[...] = jnp.where(lane == 0, w0, jnp.where(lane == 1, w1, 0.0))
        cols = jnp.where(lane == 0, dests[0], jnp.where(lane == 1, dests[1], 0.0))
        dest_ref[...] = jnp.transpose(cols)[0:8, :].astype(jnp.int32)


def _route(logits, n):
    tm = ROUTE_TM
    one = pl.BlockSpec((1, ROUTER_PAD), lambda p, i: (0, 0))
    gate, dest, counts = pl.pallas_call(
        _route_kernel,
        grid=(2, n // tm),
        in_specs=[pl.BlockSpec((tm, ROUTER_PAD), lambda p, i: (i, 0))],
        out_specs=[pl.BlockSpec((tm, ROUTER_PAD), lambda p, i: (i * p, 0)),
                   pl.BlockSpec((8, tm), lambda p, i: (0, i * p)), one],
        out_shape=[jax.ShapeDtypeStruct((n, ROUTER_PAD), F32),
                   jax.ShapeDtypeStruct((8, n), jnp.int32),
                   jax.ShapeDtypeStruct((1, ROUTER_PAD), F32)],
        scratch_shapes=[pltpu.VMEM((1, ROUTER_PAD), F32), pltpu.VMEM((1, ROUTER_PAD), F32)],
        compiler_params=_params(("arbitrary", "arbitrary")),
        name="route",
    )(logits)
    gate = gate[:, :TOP_K]
    dest = dest[:TOP_K].reshape(TOP_K * n)
    a = n * TOP_K
    counts = counts[0, :N_EXPERTS].astype(jnp.int32)
    pcounts = (counts + MOE_TB - 1) // MOE_TB * MOE_TB
    pends = jnp.cumsum(pcounts)
    nb = a // MOE_TB + N_EXPERTS
    slots = nb * MOE_TB
    tok = jnp.arange(a, dtype=jnp.int32) % n
    slot_tok = jnp.zeros((slots,), jnp.int32).at[dest].set(tok)
    block_start = jnp.arange(nb, dtype=jnp.int32) * MOE_TB
    block_e = jnp.minimum(jnp.sum((pends[None, :] <= block_start[:, None]).astype(jnp.int32), axis=1),
                          N_EXPERTS - 1)
    n_active = (pends[-1] // MOE_TB).astype(jnp.int32).reshape(1)
    ids = jnp.arange(N_EXPERTS, dtype=jnp.int32)
    later = (counts[None, :] > 0) & (ids[None, :] > ids[:, None])
    nxt = jnp.min(jnp.where(later, ids[None, :], N_EXPERTS), axis=1)
    block_next = jnp.where(nxt == N_EXPERTS, -1, nxt)[block_e].astype(jnp.int32)
    return dest, gate, slot_tok, block_e, block_next, n_active


def _cast_weights(src, dst):
    rows, cols = src.shape
    step = CAST_VREGS * 8 * LANES // cols

    def body(r, c):
        sl = pl.ds(pl.multiple_of(r * step, step), step)
        dst[sl, :] = src[sl, :].astype(BF16)
        return c

    lax.fori_loop(0, rows // step, body, 0, unroll=2)


def _expert_kernel(be_ref, nx_ref, nact_ref, tok_ref, h1p_hbm, wg_hbm, wu_hbm, wd_hbm, ys_hbm,
                   xbuf, ybuf, wg_f, wu_f, wd_f, wg_s, wu_s, wd_s, gsem, osem, wsem):
    i = pl.program_id(0)
    slot = i % 2
    xslot = i % GATHER_SLOTS
    nact = nact_ref[0]

    def gather(step, inline=False):
        sl = step % GATHER_SLOTS
        _start_row_gather(tok_ref, step * MOE_TB, 1, MOE_TB, h1p_hbm, xbuf.at[sl], gsem.at[sl], False,
                          inline)

    def out_copies(step, sl):
        return _tile_copies(ys_hbm, step * MOE_TB, MOE_TB, ybuf.at[sl], osem.at[sl], True)

    def weight_copies(e):
        return [pltpu.make_async_copy(src.at[e], dst, wsem)
                for src, dst in ((wg_hbm, wg_f), (wu_hbm, wu_f), (wd_hbm, wd_f))]

    @pl.when((i == 0) & (nact > 0))
    def _():
        for cp in weight_copies(be_ref[0]):
            cp.start(priority=WEIGHT_DMA_PRIORITY)
        gather(0)

    @pl.when((i == 0) & (nact > 1))
    def _():
        gather(1)

    @pl.when(i >= 2)
    def _():
        for cp in out_copies(i - 2, slot):
            cp.wait()

    @pl.when(i < nact)
    def _():
        expert = be_ref[i]

        @pl.when((i == 0) | (expert != be_ref[jnp.maximum(i - 1, 0)]))
        def _():
            for cp in weight_copies(expert):
                cp.wait()
            _cast_weights(wg_f, wg_s)
            _cast_weights(wu_f, wu_s)
            _cast_weights(wd_f, wd_s)

            @pl.when(nx_ref[i] >= 0)
            def _():
                for cp in weight_copies(nx_ref[i]):
                    cp.start(priority=WEIGHT_DMA_PRIORITY)

        _wait_row_gather(xbuf.at[xslot], gsem.at[xslot])

        def block(prefetch):
            if prefetch:
                gather(i + 2, inline=True)
            xb = _unpack_rows(xbuf.at[xslot]).astype(BF16)
            gate = jnp.dot(xb, wg_s[...], preferred_element_type=F32)
            up = jnp.dot(xb, wu_s[...], preferred_element_type=F32)
            hid = (jax.nn.silu(gate) * up).astype(BF16)
            _pack_rows(ybuf.at[slot], jnp.dot(hid, wd_s[...], preferred_element_type=F32))

        pl.when(i + 2 < nact)(functools.partial(block, True))
        pl.when(i + 2 >= nact)(functools.partial(block, False))

    @pl.when(i >= nact)
    def _():
        ybuf[slot] = jnp.zeros(ybuf.shape[1:], jnp.uint32)

    for cp in out_copies(i, slot):
        cp.start()

    @pl.when(i == pl.num_programs(0) - 1)
    def _():
        for cp in out_copies(i, slot):
            cp.wait()

        @pl.when(i >= 1)
        def _():
            for cp in out_copies(i - 1, 1 - slot):
                cp.wait()


def _experts(h1p, slot_tok, block_e, block_next, n_active, w_gate, w_up, w_down):
    d = D_MODEL
    slots = slot_tok.shape[0]
    nb = slots // MOE_TB
    any_spec = pl.BlockSpec(memory_space=pl.ANY)
    grid_spec = pltpu.PrefetchScalarGridSpec(
        num_scalar_prefetch=4,
        grid=(nb,),
        in_specs=[any_spec, any_spec, any_spec, any_spec],
        out_specs=any_spec,
        scratch_shapes=[pltpu.VMEM((GATHER_SLOTS, PACKED, MOE_TB, LANES), jnp.uint32),
                        pltpu.VMEM((2, PACKED, MOE_TB, LANES), jnp.uint32),
                        pltpu.VMEM((d, D_EXPERT), F32), pltpu.VMEM((d, D_EXPERT), F32),
                        pltpu.VMEM((D_EXPERT, d), F32),
                        pltpu.VMEM((d, D_EXPERT), BF16), pltpu.VMEM((d, D_EXPERT), BF16),
                        pltpu.VMEM((D_EXPERT, d), BF16),
                        pltpu.SemaphoreType.DMA((GATHER_SLOTS,)), pltpu.SemaphoreType.DMA((2,)),
                        pltpu.SemaphoreType.DMA(())])
    return pl.pallas_call(
        _expert_kernel,
        grid_spec=grid_spec,
        out_shape=jax.ShapeDtypeStruct((slots, PACKED, LANES), jnp.uint32),
        compiler_params=_params(("arbitrary",)),
        name="experts",
    )(block_e, block_next, n_active, slot_tok, h1p, w_gate, w_up, w_down)


def _final_kernel(dest_ref, ys_hbm, h1_ref, gate_ref, g_ref, b_ref, o_ref, ybuf, gsem):
    i = pl.program_id(0)
    steps = pl.num_programs(0)
    slot = i % GATHER_SLOTS

    def fetch(step, inline=False):
        sl = step % GATHER_SLOTS
        for k in range(TOP_K):
            _start_row_gather(dest_ref, k * steps * FIN_TM + step * FIN_TM, 1, FIN_TM, ys_hbm,
                              ybuf.at[sl, k], gsem.at[sl], True, inline)

    @pl.when(i == 0)
    def _():
        fetch(0)

    @pl.when((i == 0) & (steps > 1))
    def _():
        fetch(1)

    _wait_row_gather(ybuf.at[slot], gsem.at[slot])

    def combine(prefetch):
        if prefetch:
            fetch(i + 2, inline=True)
        gate = gate_ref[...]
        ffn = (_unpack_rows(ybuf.at[slot, 0]) * gate[:, 0:1]
               + _unpack_rows(ybuf.at[slot, 1]) * gate[:, 1:2])
        o_ref[...] = _layer_norm(DN_ALPHA * h1_ref[...] + ffn, g_ref[...], b_ref[...])

    pl.when(i + 2 < steps)(functools.partial(combine, True))
    pl.when(i + 2 >= steps)(functools.partial(combine, False))


def _final(dest, ys, h1, gate, g, b):
    n, d = h1.shape
    grid_spec = pltpu.PrefetchScalarGridSpec(
        num_scalar_prefetch=1,
        grid=(n // FIN_TM,),
        in_specs=[pl.BlockSpec(memory_space=pl.ANY),
                  pl.BlockSpec((FIN_TM, d), lambda i, ds: (i, 0)),
                  pl.BlockSpec((FIN_TM, TOP_K), lambda i, ds: (i, 0)),
                  pl.BlockSpec((1, d), lambda i, ds: (0, 0)),
                  pl.BlockSpec((1, d), lambda i, ds: (0, 0))],
        out_specs=pl.BlockSpec((FIN_TM, d), lambda i, ds: (i, 0)),
        scratch_shapes=[pltpu.VMEM((GATHER_SLOTS, TOP_K, PACKED, FIN_TM, LANES), jnp.uint32),
                        pltpu.SemaphoreType.DMA((GATHER_SLOTS,))])
    return pl.pallas_call(
        _final_kernel,
        grid_spec=grid_spec,
        out_shape=jax.ShapeDtypeStruct((n, d), F32),
        compiler_params=_params(("arbitrary",)),
        name="final",
    )(dest, ys, h1, gate, g.reshape(1, d), b.reshape(1, d))


def kernel(x, ln0_g, ln0_b, w_in, b_in, rpb, w_proj_a, w_proj_b, w_o, b_o, ln1_g, ln1_b,
           w_router_group, b_router_group, w_router_expert, b_router_expert,
           w_gate, w_up, w_down, ln2_g, ln2_b):
    bsz, s, d = x.shape
    n = bsz * s
    assert d == D_MODEL and w_in.shape[0] == DEPTH
    scale = HEAD_DIM ** -0.5

    col_scale = np.ones((w_in.shape[2],), np.float32)
    col_scale[:NA_WIDTH] = scale
    col_scale[3 * NA_WIDTH:3 * NA_WIDTH + DIL_WIDTH] = scale
    w_all = (w_in[0] * col_scale).astype(BF16)
    b_all = (b_in[0] * col_scale).reshape(1, -1)
    assert DIL_OUT_WIDTH == INPROJ_TN
    na_blocks, ngroups = 3 * NA_WIDTH // INPROJ_TN, len(DIL_GROUPS)
    w_r = jnp.concatenate([w_router_group[0], w_router_expert[0]], axis=1)
    w_r = jnp.pad(w_r, ((0, 0), (0, ROUTER_PAD - w_r.shape[1])))
    w_r_hi = w_r.astype(BF16)
    w_r_lo = (w_r - w_r_hi.astype(F32)).astype(BF16)
    b_r = jnp.pad(jnp.concatenate([b_router_group[0], b_router_expert[0]]),
                  (0, ROUTER_PAD - N_GROUPS - N_EXPERTS)).reshape(1, ROUTER_PAD)
    slopes = _alibi_slopes(DIL_HEADS)

    h, hb = _ln0(x.reshape(n, d), ln0_g, ln0_b)
    zq = [_inproj(hb, w_all, b_all, bsz, s, DIL_GROUPS[0][1], 3 * NA_WIDTH + 3 * DIL_OUT_WIDTH,
                  lambda j: jnp.where(j < na_blocks, j, na_blocks + (j - na_blocks) * ngroups))]
    for g in range(1, ngroups):
        zq.append(_inproj(hb, w_all, b_all, bsz, s, DIL_GROUPS[g][1], 3 * DIL_OUT_WIDTH,
                          lambda j, g=g: na_blocks + j * ngroups + g))
    na = _neighborhood_attention(zq[0].reshape(-1, bsz, s, HEAD_DIM), rpb[0])
    na = na.reshape(NA_HEADS, n, HEAD_DIM)
    dil = [_dilated_group(zq[g], DIL0_H if g == 0 else 0, g, slopes) for g in range(ngroups)]
    m = _mix(na, [o for o, _ in dil], [l for _, l in dil], hb,
             w_proj_a[0].astype(BF16), w_proj_b[0].astype(BF16), w_all, b_all, s)
    h1, h1p, logits = _outproj(m, h, w_o[0].astype(BF16), b_o[0], ln1_g[0], ln1_b[0],
                               w_r_hi, w_r_lo, b_r)
    dest, gate, slot_tok, block_e, block_next, n_active = _route(logits, n)
    ys = _experts(h1p, slot_tok, block_e, block_next, n_active, w_gate[0], w_up[0], w_down[0])
    out = _final(dest, ys, h1, gate, ln2_g[0], ln2_b[0])
    return out.reshape(bsz, s, d)
```

```python
import functools

import numpy as np
import jax
import jax.numpy as jnp
from jax import lax
from jax.experimental import pallas as pl
from jax.experimental.pallas import tpu as pltpu

F32 = jnp.float32
BF16 = jnp.bfloat16

D_MODEL = 2048
HEAD_DIM = 128
GRID_W = 64
NA_HEADS = 8
NA_KH = 8
NA_KW = 16
DIL_GROUPS = ((128, 1), (512, 4), (2048, 16))
DIL_HEADS_PER_GROUP = 4
DIL_HEADS = DIL_HEADS_PER_GROUP * len(DIL_GROUPS)
N_GROUPS = 8
EXPERTS_PER_GROUP = 8
N_EXPERTS = N_GROUPS * EXPERTS_PER_GROUP
TOP_K = 2
D_EXPERT = D_MODEL // 4
LN_EPS = 1e-5
DEPTH = 1
DN_ALPHA = (2 * DEPTH) ** 0.25
NA_WIDTH = NA_HEADS * HEAD_DIM
DIL_WIDTH = DIL_HEADS * HEAD_DIM
DIL_OUT_WIDTH = DIL_HEADS_PER_GROUP * HEAD_DIM
QKV_WIDTH = 3 * NA_WIDTH + 3 * DIL_WIDTH
QKV_HEADS = QKV_WIDTH // HEAD_DIM
LANES = 128
CHUNKS = D_MODEL // LANES
PACKED = CHUNKS // 2
MASK_VALUE = -1e30

QA_H, KA_H, VA_H = 0, NA_HEADS, 2 * NA_HEADS
DIL0_H = 3 * NA_HEADS

LN_TM = 512
INPROJ_TM, INPROJ_TN = 2048, 512
NA_QROWS = 4
NA_KROWS = 12
NA_NSUB = 8
DIL_NSUB = 8
DIL_TQ = 256
DIL_HALF = 64
DIL_TK = DIL_TQ + 2 * DIL_HALF
MIX_TM, MIX_TN = 512, 512
OUT_TM = 512
ROUTER_PAD = 128
ROUTE_TM = 512
MOE_TB = 128
FIN_TM = 128
CAST_VREGS = 32
GATHER_SLOTS = 3
WEIGHT_DMA_PRIORITY = 1
VMEM_LIMIT = 56 * 1024 * 1024


def _params(sem, limit=VMEM_LIMIT):
    return pltpu.CompilerParams(dimension_semantics=sem, vmem_limit_bytes=limit)


def _layer_norm(x, g, b):
    mu = jnp.mean(x, axis=-1, keepdims=True)
    xc = x - mu
    var = jnp.mean(xc * xc, axis=-1, keepdims=True)
    return xc * lax.rsqrt(var + LN_EPS) * g + b


HIGH_HALF = np.uint32(0xFFFF0000)


def _bf16_bits(x):
    return lax.bitcast_convert_type(x.astype(BF16).astype(F32), jnp.uint32)


def _pack_rows(ref, val):
    for c in range(PACKED):
        lo = _bf16_bits(val[:, c * LANES:(c + 1) * LANES])
        hi = _bf16_bits(val[:, (c + PACKED) * LANES:(c + PACKED + 1) * LANES])
        ref[c] = (lo >> 16) | (hi & HIGH_HALF)


def _unpack_rows(ref):
    words = [ref[c] for c in range(PACKED)]
    lo = [lax.bitcast_convert_type(w << 16, F32) for w in words]
    hi = [lax.bitcast_convert_type(w & HIGH_HALF, F32) for w in words]
    return jnp.concatenate(lo + hi, axis=-1)


def _tile_copies(hbm, row0, rows, vmem, sem, to_hbm):
    copies = []
    for c in range(PACKED):
        h, v = hbm.at[pl.ds(row0, rows), c], vmem.at[c]
        copies.append(pltpu.make_async_copy(v, h, sem) if to_hbm else pltpu.make_async_copy(h, v, sem))
    return copies


def _start_row_gather(idx_ref, base, step, count, src_hbm, dst, sem, both_threads, inline=False):
    def start(j, c):
        for half in range(2):
            t = 2 * j + half
            pltpu.make_async_copy(src_hbm.at[idx_ref[base + t * step]], dst.at[:, t], sem).start(
                priority=half if both_threads else 0)
        return c

    if inline:
        for j in range(count // 2):
            start(j, 0)
    else:
        lax.fori_loop(0, count // 2, start, 0, unroll=2)


def _wait_row_gather(dst, sem):
    pltpu.make_async_copy(dst, dst, sem).wait()


def _ln0_kernel(x_ref, g_ref, b_ref, h_ref, hb_ref):
    y = _layer_norm(x_ref[...], g_ref[...], b_ref[...])
    h_ref[...] = y
    hb_ref[...] = y.astype(BF16)


def _ln0(x, g, b):
    n, d = x.shape
    row = pl.BlockSpec((LN_TM, d), lambda i: (i, 0))
    vec = pl.BlockSpec((1, d), lambda i: (0, 0))
    return pl.pallas_call(
        _ln0_kernel,
        grid=(n // LN_TM,),
        in_specs=[row, vec, vec],
        out_specs=[row, row],
        out_shape=[jax.ShapeDtypeStruct((n, d), F32), jax.ShapeDtypeStruct((n, d), BF16)],
        compiler_params=_params(("parallel",)),
        name="ln0",
    )(x, g.reshape(1, d), b.reshape(1, d))


def _inproj_kernel(hb_ref, w_ref, b_ref, o_ref, acc_s, *, dil):
    acc = jnp.dot(hb_ref[...], w_ref[...], preferred_element_type=F32) + b_ref[...]
    if dil == 1:
        for c in range(INPROJ_TN // LANES):
            o_ref[c, 0, 0] = acc[:, c * LANES:(c + 1) * LANES].astype(BF16)
    else:
        for c in range(INPROJ_TN // LANES):
            acc_s[c] = acc[:, c * LANES:(c + 1) * LANES]
        for c in range(INPROJ_TN // LANES):
            for r in range(dil):
                o_ref[c, 0, r] = acc_s[c, pl.ds(r, INPROJ_TM // dil, stride=dil), :].astype(BF16)


def _inproj(hb, w, b, bsz, s, dil, width, col_block):
    n, d = hb.shape
    tiles = s // INPROJ_TM
    assert INPROJ_TM % (16 * dil) == 0 and s % INPROJ_TM == 0 and width % INPROJ_TN == 0
    return pl.pallas_call(
        functools.partial(_inproj_kernel, dil=dil),
        grid=(n // INPROJ_TM, width // INPROJ_TN),
        in_specs=[pl.BlockSpec((INPROJ_TM, d), lambda i, j: (i, 0)),
                  pl.BlockSpec((d, INPROJ_TN), lambda i, j: (0, col_block(j))),
                  pl.BlockSpec((1, INPROJ_TN), lambda i, j: (0, col_block(j)))],
        out_specs=pl.BlockSpec((INPROJ_TN // LANES, 1, dil, INPROJ_TM // dil, LANES),
                               lambda i, j: (j, i // tiles, 0, i % tiles, 0)),
        out_shape=jax.ShapeDtypeStruct((width // LANES, bsz, dil, s // dil, LANES), BF16),
        scratch_shapes=[pltpu.VMEM((INPROJ_TN // LANES, INPROJ_TM, LANES), F32)],
        compiler_params=_params(("parallel", "arbitrary")),
        name=f"inproj_{dil}",
    )(hb, w, b)


def _na_bias_tiles(rpb, rows):
    heads, n_dr, n_dc = rpb.shape
    qc = np.arange(GRID_W)[:, None]
    kc = np.arange(GRID_W)[None, :]
    qcs = np.clip(qc - NA_KW // 2, 0, GRID_W - NA_KW)
    v_col = (kc >= qcs) & (kc < qcs + NA_KW)
    dc = np.clip(kc - qc + NA_KW - 1, 0, n_dc - 1)
    onehot = (dc[None] == np.arange(n_dc)[:, None, None]) & v_col[None]
    toep = jnp.einsum('hrd,dqk->hrqk', rpb.astype(F32), jnp.asarray(onehot, F32),
                      precision=lax.Precision.HIGHEST)
    toep = jnp.where(jnp.asarray(v_col), toep, MASK_VALUE)
    masked = jnp.full((heads, 1, GRID_W, GRID_W), MASK_VALUE, F32)
    blocks = jnp.concatenate([toep, masked], axis=1)
    i = np.arange(NA_QROWS)[:, None]
    j = np.arange(NA_KROWS)[None, :]
    sel = []
    for r0, ks in ((0, 0), (2 * NA_QROWS, 2 * NA_QROWS - NA_KH // 2), (rows - NA_QROWS, rows - NA_KROWS)):
        r, krow = r0 + i, ks + j
        start = np.clip(r - NA_KH // 2, 0, rows - NA_KH)
        v_row = (krow >= start) & (krow < start + NA_KH)
        sel.append(np.where(v_row, krow - r + NA_KH - 1, n_dr))
    sel = np.stack(sel)
    pairs = sel.reshape(3, NA_QROWS, NA_KROWS // 2, 2)
    distinct = sorted({tuple(p) for p in pairs.reshape(-1, 2).tolist()})
    plan = [[[distinct.index(tuple(pairs[v, i, c])) for c in range(NA_KROWS // 2)]
             for i in range(NA_QROWS)] for v in range(3)]
    left = jnp.take(blocks, jnp.asarray([p[0] for p in distinct], jnp.int32), axis=1)
    right = jnp.take(blocks, jnp.asarray([p[1] for p in distinct], jnp.int32), axis=1)
    return jnp.concatenate([left, right], axis=-1), plan


def _attend_blocks(nblocks, scores, finish):
    scored, weighted = {}, {}
    for t in range(nblocks + 2):
        if t < nblocks:
            scored[t] = scores(t)
        if 0 <= t - 1 < nblocks:
            s, vw = scored.pop(t - 1)
            m = jnp.max(s, axis=-1, keepdims=True)
            p = jnp.exp(s - m)
            weighted[t - 1] = (p.astype(BF16), vw, m, jnp.sum(p, axis=-1, keepdims=True))
        if 0 <= t - 2 < nblocks:
            p, vw, m, l = weighted.pop(t - 2)
            finish(t - 2, jnp.dot(p, vw, preferred_element_type=F32) / l, m, l)


def _scores(q, kw, bias):
    return lax.dot_general(q, kw, (((1,), (1,)), ((), ())), preferred_element_type=F32) + bias


def _edge_variant(blk, nblk):
    return jnp.where(blk == 0, 0, jnp.where(blk == nblk - 1, 2, 1))


def _na_kernel(q_ref, k_ref, v_ref, tile_ref, o_ref, tab_ref, *, rows, plan):
    tq = NA_QROWS * GRID_W

    @pl.when(pl.program_id(2) == 0)
    def _():
        for v in range(3):
            for i in range(NA_QROWS):
                for c in range(NA_KROWS // 2):
                    tab_ref[v, 0, i * GRID_W:(i + 1) * GRID_W, c * 2 * GRID_W:(c + 1) * 2 * GRID_W] = (
                        tile_ref[0, plan[v][i][c]])

    def scores(u):
        blk = pl.program_id(2) * NA_NSUB + u
        ks = jnp.clip(blk * NA_QROWS - NA_KH // 2, 0, rows - NA_KROWS)
        off = pl.multiple_of(ks * GRID_W, GRID_W)
        kw = k_ref[0, 0, pl.ds(off, NA_KROWS * GRID_W), :]
        vw = v_ref[0, 0, pl.ds(off, NA_KROWS * GRID_W), :]
        bias = tab_ref[_edge_variant(blk, rows // NA_QROWS), 0]
        return _scores(q_ref[0, 0, u * tq:(u + 1) * tq, :], kw, bias), vw

    def finish(u, o, m, l):
        o_ref[0, 0, u * tq:(u + 1) * tq, :] = o.astype(BF16)

    _attend_blocks(NA_NSUB, scores, finish)


def _neighborhood_attention(zq4, rpb):
    _, bsz, s, _ = zq4.shape
    rows = s // GRID_W
    assert rows % (NA_QROWS * NA_NSUB) == 0 and rows >= NA_KROWS + NA_QROWS and NA_KROWS % 2 == 0
    tiles, plan = _na_bias_tiles(rpb, rows)
    tq = NA_QROWS * GRID_W * NA_NSUB
    return pl.pallas_call(
        functools.partial(_na_kernel, rows=rows, plan=plan),
        grid=(bsz, NA_HEADS, s // tq),
        in_specs=[pl.BlockSpec((1, 1, tq, HEAD_DIM), lambda b, h, i: (QA_H + h, b, i, 0)),
                  pl.BlockSpec((1, 1, s, HEAD_DIM), lambda b, h, i: (KA_H + h, b, 0, 0)),
                  pl.BlockSpec((1, 1, s, HEAD_DIM), lambda b, h, i: (VA_H + h, b, 0, 0)),
                  pl.BlockSpec((1,) + tiles.shape[1:], lambda b, h, i: (h, 0, 0, 0))],
        out_specs=pl.BlockSpec((1, 1, tq, HEAD_DIM), lambda b, h, i: (h, b, i, 0)),
        out_shape=jax.ShapeDtypeStruct((NA_HEADS, bsz, s, HEAD_DIM), BF16),
        scratch_shapes=[pltpu.VMEM((3, 1, NA_QROWS * GRID_W, NA_KROWS * GRID_W), F32)],
        compiler_params=_params(("arbitrary", "arbitrary", "arbitrary")),
        name="na_attn",
    )(zq4, zq4, zq4, tiles)


def _alibi_slopes(n):
    return np.array([2.0 ** (-8.0 * (i + 1) / n) for i in range(n)], dtype=np.float32)


def _dil_bias_tables(slopes, dil):
    qi = np.arange(DIL_TQ)[:, None]
    kj = np.arange(DIL_TK)[None, :]
    dist = np.stack([np.abs(kj - qi + shift) for shift in (0, -DIL_HALF, -2 * DIL_HALF)])
    dist = jnp.asarray(dist, F32)[:, None]
    penalty = jnp.asarray(slopes, F32)[None, :, None, None] * (dist * float(dil))
    return jnp.where(dist <= DIL_HALF, -penalty, MASK_VALUE)


def _dil_kernel(q_ref, k_ref, v_ref, tab_ref, o_ref, lse_ref, *, length, nsub, nres):
    def scores(u):
        res, sub = divmod(u, nsub)
        blk = pl.program_id(3) * nsub + sub
        ks = jnp.clip(blk * DIL_TQ - DIL_HALF, 0, length - DIL_TK)
        off = pl.multiple_of(ks, DIL_HALF)
        kw = k_ref[0, 0, res, pl.ds(off, DIL_TK), :]
        vw = v_ref[0, 0, res, pl.ds(off, DIL_TK), :]
        bias = tab_ref[_edge_variant(blk, length // DIL_TQ), 0]
        return _scores(q_ref[0, 0, res, sub * DIL_TQ:(sub + 1) * DIL_TQ, :], kw, bias), vw

    def finish(u, o, m, l):
        res, sub = divmod(u, nsub)
        rows = slice(sub * DIL_TQ, (sub + 1) * DIL_TQ)
        o_ref[0, 0, res, rows, :] = o.astype(BF16)
        lse_ref[0, 0, res, rows, :] = jnp.broadcast_to(m + jnp.log(l), (DIL_TQ, LANES))

    _attend_blocks(nres * nsub, scores, finish)


def _dilated_group(zq, base, g, slopes):
    window, dil = DIL_GROUPS[g]
    assert (window // 2) // dil == DIL_HALF
    _, bsz, _, length, _ = zq.shape
    nsub = min(DIL_NSUB, length // DIL_TQ)
    nres = min(dil, DIL_NSUB // nsub)
    assert zq.shape[2] == dil and length % (DIL_TQ * nsub) == 0 and length >= DIL_TK and dil % nres == 0
    gh = DIL_HEADS_PER_GROUP
    tabs = _dil_bias_tables(slopes[g * gh:(g + 1) * gh], dil)
    tq = DIL_TQ * nsub
    qspec = pl.BlockSpec((1, 1, nres, tq, HEAD_DIM), lambda b, h, r, n: (base + h, b, r, n, 0))
    kvspec = lambda off: pl.BlockSpec((1, 1, nres, length, HEAD_DIM),
                                      lambda b, h, r, n: (base + off + h, b, r, 0, 0))
    ospec = pl.BlockSpec((1, 1, nres, tq, HEAD_DIM), lambda b, h, r, n: (h, b, r, n, 0))
    return pl.pallas_call(
        functools.partial(_dil_kernel, length=length, nsub=nsub, nres=nres),
        grid=(bsz, gh, dil // nres, length // tq),
        in_specs=[qspec, kvspec(gh), kvspec(2 * gh),
                  pl.BlockSpec((3, 1, DIL_TQ, DIL_TK), lambda b, h, r, n: (0, h, 0, 0))],
        out_specs=[ospec, ospec],
        out_shape=[jax.ShapeDtypeStruct((gh, bsz, dil, length, HEAD_DIM), BF16),
                   jax.ShapeDtypeStruct((gh, bsz, dil, length, HEAD_DIM), F32)],
        compiler_params=_params(("parallel", "parallel", "parallel", "arbitrary")),
        name=f"dil_attn_{dil}",
    )(zq, zq, zq, tabs)


def _mix_kernel(na_ref, o1_ref, o2_ref, o3_ref, l1_ref, l2_ref, l3_ref, hb_ref,
                wpa_ref, wpb_ref, wga_ref, wgb_ref, bga_ref, bgb_ref, m_ref, na_s, dil_s, o_s, l_s):
    @pl.when(pl.program_id(1) == 0)
    def _():
        for h in range(NA_HEADS):
            na_s[:, h * HEAD_DIM:(h + 1) * HEAD_DIM] = na_ref[h]
        for g, (o_ref, l_ref) in enumerate(((o1_ref, l1_ref), (o2_ref, l2_ref), (o3_ref, l3_ref))):
            dil = DIL_GROUPS[g][1]
            for h in range(DIL_HEADS_PER_GROUP):
                for r in range(dil):
                    rows = slice(None) if dil == 1 else pl.ds(r, MIX_TM // dil, stride=dil)
                    o_s[g, h, rows, :] = o_ref[h, 0, r].astype(F32)
                    l_s[g, h, rows, :] = l_ref[h, 0, r]
        for h in range(DIL_HEADS_PER_GROUP):
            ls = [l_s[g, h] for g in range(len(DIL_GROUPS))]
            mx = jnp.maximum(jnp.maximum(ls[0], ls[1]), ls[2])
            es = [jnp.exp(l - mx) for l in ls]
            den = es[0] + es[1] + es[2]
            acc = (es[0] * o_s[0, h] + es[1] * o_s[1, h] + es[2] * o_s[2, h]) / den
            dil_s[:, h * HEAD_DIM:(h + 1) * HEAD_DIM] = acc.astype(BF16)

    hb = hb_ref[...]
    ya = jnp.dot(na_s[...], wpa_ref[...], preferred_element_type=F32)
    yb = jnp.dot(dil_s[...], wpb_ref[...], preferred_element_type=F32)
    ga = jnp.dot(hb, wga_ref[...], preferred_element_type=F32) + bga_ref[...]
    gb = jnp.dot(hb, wgb_ref[...], preferred_element_type=F32) + bgb_ref[...]
    m_ref[...] = (jax.nn.sigmoid(ga) * ya + jax.nn.sigmoid(gb) * yb).astype(BF16)


def _mix(na, dil_o, dil_lse, hb, wpa, wpb, w_all, b_all, s):
    n, d = hb.shape
    tm, tn = MIX_TM, MIX_TN
    tiles = s // tm
    gh = DIL_HEADS_PER_GROUP
    ngroups = len(DIL_GROUPS)
    assert all(tm % (16 * dil) == 0 for _, dil in DIL_GROUPS) and s % tm == 0
    col = lambda k: pl.BlockSpec((k, tn), lambda i, j: (0, j))
    ga0, gb0 = QKV_WIDTH // tn, (QKV_WIDTH + d) // tn
    gate = lambda k, first: pl.BlockSpec((k, tn), lambda i, j: (0, first + j))
    grp = lambda dil: pl.BlockSpec((gh, 1, dil, tm // dil, HEAD_DIM),
                                   lambda i, j: (0, i // tiles, 0, i % tiles, 0))
    groups = [grp(dil) for _, dil in DIL_GROUPS]
    return pl.pallas_call(
        _mix_kernel,
        grid=(n // tm, d // tn),
        in_specs=[pl.BlockSpec((NA_HEADS, tm, HEAD_DIM), lambda i, j: (0, i, 0)), *groups, *groups,
                  pl.BlockSpec((tm, d), lambda i, j: (i, 0)),
                  col(NA_WIDTH), col(DIL_OUT_WIDTH),
                  gate(d, ga0), gate(d, gb0), gate(1, ga0), gate(1, gb0)],
        out_specs=pl.BlockSpec((tm, tn), lambda i, j: (i, j)),
        out_shape=jax.ShapeDtypeStruct((n, d), BF16),
        scratch_shapes=[pltpu.VMEM((tm, NA_WIDTH), BF16), pltpu.VMEM((tm, DIL_OUT_WIDTH), BF16),
                        pltpu.VMEM((ngroups, gh, tm, HEAD_DIM), F32),
                        pltpu.VMEM((ngroups, gh, tm, HEAD_DIM), F32)],
        compiler_params=_params(("parallel", "arbitrary")),
        name="mix",
    )(na, *dil_o, *dil_lse, hb, wpa, wpb, w_all, w_all, b_all, b_all)


def _outproj_kernel(m_ref, h_ref, wo_ref, bo_ref, g_ref, b_ref, wrh_ref, wrl_ref, br_ref,
                    h1_ref, h1p_hbm, lg_ref, hbuf, sem):
    i = pl.program_id(0)
    slot = i % 2

    def out_copies(step, sl):
        return _tile_copies(h1p_hbm, step * OUT_TM, OUT_TM, hbuf.at[sl], sem.at[sl], True)

    mix = jnp.dot(m_ref[...], wo_ref[...], preferred_element_type=F32) + bo_ref[...]
    h1 = _layer_norm(DN_ALPHA * h_ref[...] + mix, g_ref[...], b_ref[...])
    h1_ref[...] = h1

    @pl.when(i >= 2)
    def _():
        for cp in out_copies(i - 2, slot):
            cp.wait()

    _pack_rows(hbuf.at[slot], h1)
    for cp in out_copies(i, slot):
        cp.start()

    @pl.when(i == pl.num_programs(0) - 1)
    def _():
        for cp in out_copies(i, slot):
            cp.wait()

        @pl.when(i >= 1)
        def _():
            for cp in out_copies(i - 1, 1 - slot):
                cp.wait()

    hi = h1.astype(BF16)
    lo = (h1 - hi.astype(F32)).astype(BF16)
    lg = (jnp.dot(hi, wrh_ref[...], preferred_element_type=F32)
          + jnp.dot(lo, wrh_ref[...], preferred_element_type=F32)
          + jnp.dot(hi, wrl_ref[...], preferred_element_type=F32))
    lg_ref[...] = lg + br_ref[...]


def _outproj(m, h, wo, bo, g, b, wr_hi, wr_lo, br):
    n, d = h.shape
    tm = OUT_TM
    row = lambda w: pl.BlockSpec((tm, w), lambda i: (i, 0))
    full = lambda r, c: pl.BlockSpec((r, c), lambda i: (0, 0), pipeline_mode=pl.Buffered(1))
    return pl.pallas_call(
        _outproj_kernel,
        grid=(n // tm,),
        in_specs=[row(d), row(d), full(d, d), full(1, d), full(1, d), full(1, d),
                  full(d, ROUTER_PAD), full(d, ROUTER_PAD), full(1, ROUTER_PAD)],
        out_specs=[row(d), pl.BlockSpec(memory_space=pl.ANY), row(ROUTER_PAD)],
        out_shape=[jax.ShapeDtypeStruct((n, d), F32),
                   jax.ShapeDtypeStruct((n, PACKED, LANES), jnp.uint32),
                   jax.ShapeDtypeStruct((n, ROUTER_PAD), F32)],
        scratch_shapes=[pltpu.VMEM((2, PACKED, tm, LANES), jnp.uint32), pltpu.SemaphoreType.DMA((2,))],
        compiler_params=_params(("arbitrary",)),
        name="outproj",
    )(m, h, wo, bo.reshape(1, d), g.reshape(1, d), b.reshape(1, d), wr_hi, wr_lo, br)


def _first_argmax(vals, lane_f):
    top = jnp.max(vals, axis=-1, keepdims=True)
    idx = jnp.min(jnp.where(vals == top, lane_f, float(LANES)), axis=-1, keepdims=True)
    return top, idx


def _route_kernel(lg_ref, gate_ref, dest_ref, cnt_ref, base_s, start_s):
    phase = pl.program_id(0)

    @pl.when(pl.program_id(1) == 0)
    def _():
        @pl.when(phase == 1)
        def _():
            counts = base_s[...]
            cnt_ref[...] = counts
            blocks = jnp.ceil(counts * (1.0 / MOE_TB))
            before = (lax.broadcasted_iota(jnp.int32, (ROUTER_PAD, ROUTER_PAD), 0)
                      < lax.broadcasted_iota(jnp.int32, (ROUTER_PAD, ROUTER_PAD), 1)).astype(BF16)
            starts = jnp.dot(jnp.broadcast_to(blocks, (8, ROUTER_PAD)).astype(BF16), before,
                             preferred_element_type=F32)
            start_s[...] = starts[0:1] * float(MOE_TB)

        base_s[...] = jnp.zeros(base_s.shape, F32)

    lg = lg_ref[...]
    lane = lax.broadcasted_iota(jnp.int32, lg.shape, 1)
    lane_f = lane.astype(F32)
    g_mask = lane < N_GROUPS
    g_top, g_sel = _first_argmax(jnp.where(g_mask, lg, MASK_VALUE), lane_f)
    g_prob = 1.0 / jnp.sum(jnp.where(g_mask, jnp.exp(lg - g_top), 0.0), axis=-1, keepdims=True)
    first = N_GROUPS + g_sel * EXPERTS_PER_GROUP
    e_mask = (lane_f >= first) & (lane_f < first + EXPERTS_PER_GROUP)
    el = jnp.where(e_mask, lg, MASK_VALUE)
    v0, i0 = _first_argmax(el, lane_f)
    v1, i1 = _first_argmax(jnp.where(lane_f == i0, MASK_VALUE, el), lane_f)
    e1 = jnp.exp(v1 - v0)
    w0 = g_prob / (1.0 + e1)
    w1 = g_prob * e1 / (1.0 + e1)

    onehots = [lane_f == idx - N_GROUPS for idx in (i0, i1)]

    @pl.when(phase == 0)
    def _():
        base_s[...] = base_s[...] + sum(jnp.sum(oh.astype(F32), axis=0, keepdims=True) for oh in onehots)

    @pl.when(phase == 1)
    def _():
        tm = lg.shape[0]
        tri = (lax.broadcasted_iota(jnp.int32, (tm, tm), 1)
               < lax.broadcasted_iota(jnp.int32, (tm, tm), 0)).astype(BF16)
        dests = []
        for onehot in onehots:
            before = jnp.dot(tri, onehot.astype(BF16), preferred_element_type=F32) + base_s[...]
            dests.append(jnp.sum(jnp.where(onehot, before + start_s[...], 0.0), axis=-1, keepdims=True))
            base_s[...] = base_s[...] + jnp.sum(onehot.astype(F32), axis=0, keepdims=True)
        gate_ref[...] = jnp.where(lane == 0, w0, jnp.where(lane == 1, w1, 0.0))
        cols = jnp.where(lane == 0, dests[0], jnp.where(lane == 1, dests[1], 0.0))
        dest_ref[...] = jnp.transpose(cols)[0:8, :].astype(jnp.int32)


def _invert_kernel(dest_ref, tok_ref, *, n):
    def clear(s, c):
        tok_ref[s] = 0
        return c

    def place(a, c):
        tok_ref[dest_ref[a]] = jnp.where(a >= n, a - n, a)
        return c

    lax.fori_loop(0, tok_ref.shape[0], clear, 0, unroll=8)
    lax.fori_loop(0, dest_ref.shape[0], place, 0, unroll=8)


def _invert_slots(dest, n, slots):
    assert dest.shape[0] == TOP_K * n and TOP_K == 2
    return pl.pallas_call(
        functools.partial(_invert_kernel, n=n),
        grid_spec=pltpu.PrefetchScalarGridSpec(
            num_scalar_prefetch=1, grid=(1,), in_specs=[],
            out_specs=pl.BlockSpec(memory_space=pltpu.SMEM)),
        out_shape=jax.ShapeDtypeStruct((slots,), jnp.int32),
        compiler_params=_params(("arbitrary",)),
        name="invert_slots",
    )(dest)


def _route(logits, n):
    tm = ROUTE_TM
    one = pl.BlockSpec((1, ROUTER_PAD), lambda p, i: (0, 0))
    gate, dest, counts = pl.pallas_call(
        _route_kernel,
        grid=(2, n // tm),
        in_specs=[pl.BlockSpec((tm, ROUTER_PAD), lambda p, i: (i, 0))],
        out_specs=[pl.BlockSpec((tm, ROUTER_PAD), lambda p, i: (i * p, 0)),
                   pl.BlockSpec((8, tm), lambda p, i: (0, i * p)), one],
        out_shape=[jax.ShapeDtypeStruct((n, ROUTER_PAD), F32),
                   jax.ShapeDtypeStruct((8, n), jnp.int32),
                   jax.ShapeDtypeStruct((1, ROUTER_PAD), F32)],
        scratch_shapes=[pltpu.VMEM((1, ROUTER_PAD), F32), pltpu.VMEM((1, ROUTER_PAD), F32)],
        compiler_params=_params(("arbitrary", "arbitrary")),
        name="route",
    )(logits)
    gate = gate[:, :TOP_K]
    dest = dest[:TOP_K].reshape(TOP_K * n)
    a = n * TOP_K
    counts = counts[0, :N_EXPERTS].astype(jnp.int32)
    pcounts = (counts + MOE_TB - 1) // MOE_TB * MOE_TB
    pends = jnp.cumsum(pcounts)
    nb = a // MOE_TB + N_EXPERTS
    slots = nb * MOE_TB
    slot_tok = _invert_slots(dest, n, slots)
    block_start = jnp.arange(nb, dtype=jnp.int32) * MOE_TB
    block_e = jnp.minimum(jnp.sum((pends[None, :] <= block_start[:, None]).astype(jnp.int32), axis=1),
                          N_EXPERTS - 1)
    n_active = (pends[-1] // MOE_TB).astype(jnp.int32).reshape(1)
    ids = jnp.arange(N_EXPERTS, dtype=jnp.int32)
    later = (counts[None, :] > 0) & (ids[None, :] > ids[:, None])
    nxt = jnp.min(jnp.where(later, ids[None, :], N_EXPERTS), axis=1)
    nxt = jnp.where(nxt == N_EXPERTS, -1, nxt)
    block_next = jnp.sum(jnp.where(block_e[:, None] == ids[None, :], nxt[None, :], 0), axis=1).astype(jnp.int32)
    return dest, gate, slot_tok, block_e, block_next, n_active


def _cast_weights(src, dst):
    rows, cols = src.shape
    step = CAST_VREGS * 8 * LANES // cols

    def body(r, c):
        sl = pl.ds(pl.multiple_of(r * step, step), step)
        dst[sl, :] = src[sl, :].astype(BF16)
        return c

    lax.fori_loop(0, rows // step, body, 0, unroll=2)


def _expert_kernel(be_ref, nx_ref, nact_ref, tok_ref, h1p_hbm, wg_hbm, wu_hbm, wd_hbm, ys_hbm,
                   xbuf, ybuf, wg_f, wu_f, wd_f, wg_s, wu_s, wd_s, gsem, osem, wsem):
    i = pl.program_id(0)
    slot = i % 2
    xslot = i % GATHER_SLOTS
    nact = nact_ref[0]

    def gather(step, inline=False):
        sl = step % GATHER_SLOTS
        _start_row_gather(tok_ref, step * MOE_TB, 1, MOE_TB, h1p_hbm, xbuf.at[sl], gsem.at[sl], False,
                          inline)

    def out_copies(step, sl):
        return _tile_copies(ys_hbm, step * MOE_TB, MOE_TB, ybuf.at[sl], osem.at[sl], True)

    def weight_copies(e):
        return [pltpu.make_async_copy(src.at[e], dst, wsem)
                for src, dst in ((wg_hbm, wg_f), (wu_hbm, wu_f), (wd_hbm, wd_f))]

    @pl.when((i == 0) & (nact > 0))
    def _():
        for cp in weight_copies(be_ref[0]):
            cp.start(priority=WEIGHT_DMA_PRIORITY)
        gather(0)

    @pl.when((i == 0) & (nact > 1))
    def _():
        gather(1)

    @pl.when(i >= 2)
    def _():
        for cp in out_copies(i - 2, slot):
            cp.wait()

    @pl.when(i < nact)
    def _():
        expert = be_ref[i]

        @pl.when((i == 0) | (expert != be_ref[jnp.maximum(i - 1, 0)]))
        def _():
            for cp in weight_copies(expert):
                cp.wait()
            _cast_weights(wg_f, wg_s)
            _cast_weights(wu_f, wu_s)
            _cast_weights(wd_f, wd_s)

            @pl.when(nx_ref[i] >= 0)
            def _():
                for cp in weight_copies(nx_ref[i]):
                    cp.start(priority=WEIGHT_DMA_PRIORITY)

        _wait_row_gather(xbuf.at[xslot], gsem.at[xslot])

        def block(prefetch):
            if prefetch:
                gather(i + 2, inline=True)
            xb = _unpack_rows(xbuf.at[xslot]).astype(BF16)
            gate = jnp.dot(xb, wg_s[...], preferred_element_type=F32)
            up = jnp.dot(xb, wu_s[...], preferred_element_type=F32)
            hid = (jax.nn.silu(gate) * up).astype(BF16)
            _pack_rows(ybuf.at[slot], jnp.dot(hid, wd_s[...], preferred_element_type=F32))

        pl.when(i + 2 < nact)(functools.partial(block, True))
        pl.when(i + 2 >= nact)(functools.partial(block, False))

    @pl.when(i >= nact)
    def _():
        ybuf[slot] = jnp.zeros(ybuf.shape[1:], jnp.uint32)

    for cp in out_copies(i, slot):
        cp.start()

    @pl.when(i == pl.num_programs(0) - 1)
    def _():
        for cp in out_copies(i, slot):
            cp.wait()

        @pl.when(i >= 1)
        def _():
            for cp in out_copies(i - 1, 1 - slot):
                cp.wait()


def _experts(h1p, slot_tok, block_e, block_next, n_active, w_gate, w_up, w_down):
    d = D_MODEL
    slots = slot_tok.shape[0]
    nb = slots // MOE_TB
    any_spec = pl.BlockSpec(memory_space=pl.ANY)
    grid_spec = pltpu.PrefetchScalarGridSpec(
        num_scalar_prefetch=4,
        grid=(nb,),
        in_specs=[any_spec, any_spec, any_spec, any_spec],
        out_specs=any_spec,
        scratch_shapes=[pltpu.VMEM((GATHER_SLOTS, PACKED, MOE_TB, LANES), jnp.uint32),
                        pltpu.VMEM((2, PACKED, MOE_TB, LANES), jnp.uint32),
                        pltpu.VMEM((d, D_EXPERT), F32), pltpu.VMEM((d, D_EXPERT), F32),
                        pltpu.VMEM((D_EXPERT, d), F32),
                        pltpu.VMEM((d, D_EXPERT), BF16), pltpu.VMEM((d, D_EXPERT), BF16),
                        pltpu.VMEM((D_EXPERT, d), BF16),
                        pltpu.SemaphoreType.DMA((GATHER_SLOTS,)), pltpu.SemaphoreType.DMA((2,)),
                        pltpu.SemaphoreType.DMA(())])
    return pl.pallas_call(
        _expert_kernel,
        grid_spec=grid_spec,
        out_shape=jax.ShapeDtypeStruct((slots, PACKED, LANES), jnp.uint32),
        compiler_params=_params(("arbitrary",)),
        name="experts",
    )(block_e, block_next, n_active, slot_tok, h1p, w_gate, w_up, w_down)


def _final_kernel(dest_ref, ys_hbm, h1_ref, gate_ref, g_ref, b_ref, o_ref, ybuf, gsem):
    i = pl.program_id(0)
    steps = pl.num_programs(0)
    slot = i % GATHER_SLOTS

    def fetch(step, inline=False):
        sl = step % GATHER_SLOTS
        for k in range(TOP_K):
            _start_row_gather(dest_ref, k * steps * FIN_TM + step * FIN_TM, 1, FIN_TM, ys_hbm,
                              ybuf.at[sl, k], gsem.at[sl], True, inline)

    @pl.when(i == 0)
    def _():
        fetch(0)

    @pl.when((i == 0) & (steps > 1))
    def _():
        fetch(1)

    _wait_row_gather(ybuf.at[slot], gsem.at[slot])

    def combine(prefetch):
        if prefetch:
            fetch(i + 2, inline=True)
        gate = gate_ref[...]
        ffn = (_unpack_rows(ybuf.at[slot, 0]) * gate[:, 0:1]
               + _unpack_rows(ybuf.at[slot, 1]) * gate[:, 1:2])
        o_ref[...] = _layer_norm(DN_ALPHA * h1_ref[...] + ffn, g_ref[...], b_ref[...])

    pl.when(i + 2 < steps)(functools.partial(combine, True))
    pl.when(i + 2 >= steps)(functools.partial(combine, False))


def _final(dest, ys, h1, gate, g, b):
    n, d = h1.shape
    grid_spec = pltpu.PrefetchScalarGridSpec(
        num_scalar_prefetch=1,
        grid=(n // FIN_TM,),
        in_specs=[pl.BlockSpec(memory_space=pl.ANY),
                  pl.BlockSpec((FIN_TM, d), lambda i, ds: (i, 0)),
                  pl.BlockSpec((FIN_TM, TOP_K), lambda i, ds: (i, 0)),
                  pl.BlockSpec((1, d), lambda i, ds: (0, 0)),
                  pl.BlockSpec((1, d), lambda i, ds: (0, 0))],
        out_specs=pl.BlockSpec((FIN_TM, d), lambda i, ds: (i, 0)),
        scratch_shapes=[pltpu.VMEM((GATHER_SLOTS, TOP_K, PACKED, FIN_TM, LANES), jnp.uint32),
                        pltpu.SemaphoreType.DMA((GATHER_SLOTS,))])
    return pl.pallas_call(
        _final_kernel,
        grid_spec=grid_spec,
        out_shape=jax.ShapeDtypeStruct((n, d), F32),
        compiler_params=_params(("arbitrary",)),
        name="final",
    )(dest, ys, h1, gate, g.reshape(1, d), b.reshape(1, d))


def kernel(x, ln0_g, ln0_b, w_in, b_in, rpb, w_proj_a, w_proj_b, w_o, b_o, ln1_g, ln1_b,
           w_router_group, b_router_group, w_router_expert, b_router_expert,
           w_gate, w_up, w_down, ln2_g, ln2_b):
    bsz, s, d = x.shape
    n = bsz * s
    assert d == D_MODEL and w_in.shape[0] == DEPTH
    scale = HEAD_DIM ** -0.5

    col_scale = np.ones((w_in.shape[2],), np.float32)
    col_scale[:NA_WIDTH] = scale
    col_scale[3 * NA_WIDTH:3 * NA_WIDTH + DIL_WIDTH] = scale
    w_all = (w_in[0] * col_scale).astype(BF16)
    b_all = (b_in[0] * col_scale).reshape(1, -1)
    assert DIL_OUT_WIDTH == INPROJ_TN
    na_blocks, ngroups = 3 * NA_WIDTH // INPROJ_TN, len(DIL_GROUPS)
    w_r = jnp.concatenate([w_router_group[0], w_router_expert[0]], axis=1)
    w_r = jnp.pad(w_r, ((0, 0), (0, ROUTER_PAD - w_r.shape[1])))
    w_r_hi = w_r.astype(BF16)
    w_r_lo = (w_r - w_r_hi.astype(F32)).astype(BF16)
    b_r = jnp.pad(jnp.concatenate([b_router_group[0], b_router_expert[0]]),
                  (0, ROUTER_PAD - N_GROUPS - N_EXPERTS)).reshape(1, ROUTER_PAD)
    slopes = _alibi_slopes(DIL_HEADS)

    h, hb = _ln0(x.reshape(n, d), ln0_g, ln0_b)
    zq = [_inproj(hb, w_all, b_all, bsz, s, DIL_GROUPS[0][1], 3 * NA_WIDTH + 3 * DIL_OUT_WIDTH,
                  lambda j: jnp.where(j < na_blocks, j, na_blocks + (j - na_blocks) * ngroups))]
    for g in range(1, ngroups):
        zq.append(_inproj(hb, w_all, b_all, bsz, s, DIL_GROUPS[g][1], 3 * DIL_OUT_WIDTH,
                          lambda j, g=g: na_blocks + j * ngroups + g))
    na = _neighborhood_attention(zq[0].reshape(-1, bsz, s, HEAD_DIM), rpb[0])
    na = na.reshape(NA_HEADS, n, HEAD_DIM)
    dil = [_dilated_group(zq[g], DIL0_H if g == 0 else 0, g, slopes) for g in range(ngroups)]
    m = _mix(na, [o for o, _ in dil], [l for _, l in dil], hb,
             w_proj_a[0].astype(BF16), w_proj_b[0].astype(BF16), w_all, b_all, s)
    h1, h1p, logits = _outproj(m, h, w_o[0].astype(BF16), b_o[0], ln1_g[0], ln1_b[0],
                               w_r_hi, w_r_lo, b_r)
    dest, gate, slot_tok, block_e, block_next, n_active = _route(logits, n)
    ys = _experts(h1p, slot_tok, block_e, block_next, n_active, w_gate[0], w_up[0], w_down[0])
    out = _final(dest, ys, h1, gate, ln2_g[0], ln2_b[0])
    return out.reshape(bsz, s, d)
```

```python
import functools

import numpy as np
import jax
import jax.numpy as jnp
from jax import lax
from jax.experimental import pallas as pl
from jax.experimental.pallas import tpu as pltpu

F32 = jnp.float32
BF16 = jnp.bfloat16

D_MODEL = 2048
HEAD_DIM = 128
GRID_W = 64
NA_HEADS = 8
NA_KH = 8
NA_KW = 16
DIL_GROUPS = ((128, 1), (512, 4), (2048, 16))
DIL_HEADS_PER_GROUP = 4
DIL_HEADS = DIL_HEADS_PER_GROUP * len(DIL_GROUPS)
N_GROUPS = 8
EXPERTS_PER_GROUP = 8
N_EXPERTS = N_GROUPS * EXPERTS_PER_GROUP
TOP_K = 2
D_EXPERT = D_MODEL // 4
LN_EPS = 1e-5
DEPTH = 1
DN_ALPHA = (2 * DEPTH) ** 0.25
NA_WIDTH = NA_HEADS * HEAD_DIM
DIL_WIDTH = DIL_HEADS * HEAD_DIM
DIL_OUT_WIDTH = DIL_HEADS_PER_GROUP * HEAD_DIM
QKV_WIDTH = 3 * NA_WIDTH + 3 * DIL_WIDTH
QKV_HEADS = QKV_WIDTH // HEAD_DIM
LANES = 128
CHUNKS = D_MODEL // LANES
PACKED = CHUNKS // 2
MASK_VALUE = -1e30

QA_H, KA_H, VA_H = 0, NA_HEADS, 2 * NA_HEADS
DIL0_H = 3 * NA_HEADS

LN_TM = 512
INPROJ_TM, INPROJ_TN = 2048, 512
NA_QROWS = 4
NA_KROWS = 12
NA_NSUB = 8
DIL_NSUB = 8
DIL_TQ = 256
DIL_HALF = 64
DIL_TK = DIL_TQ + 2 * DIL_HALF
MIX_TM, MIX_TN = 512, 512
OUT_TM = 512
ROUTER_PAD = 128
ROUTE_TM = 512
MOE_TB = 128
FIN_TM = 128
CAST_VREGS = 32
GATHER_SLOTS = 3
WEIGHT_DMA_PRIORITY = 1
VMEM_LIMIT = 56 * 1024 * 1024


def _params(sem, limit=VMEM_LIMIT):
    return pltpu.CompilerParams(dimension_semantics=sem, vmem_limit_bytes=limit)


def _layer_norm(x, g, b):
    mu = jnp.mean(x, axis=-1, keepdims=True)
    xc = x - mu
    var = jnp.mean(xc * xc, axis=-1, keepdims=True)
    return xc * lax.rsqrt(var + LN_EPS) * g + b


HIGH_HALF = np.uint32(0xFFFF0000)


def _bf16_bits(x):
    return lax.bitcast_convert_type(x.astype(BF16).astype(F32), jnp.uint32)


def _pack_rows(ref, val):
    for c in range(PACKED):
        lo = _bf16_bits(val[:, c * LANES:(c + 1) * LANES])
        hi = _bf16_bits(val[:, (c + PACKED) * LANES:(c + PACKED + 1) * LANES])
        ref[c] = (lo >> 16) | (hi & HIGH_HALF)


def _unpack_rows(ref):
    words = [ref[c] for c in range(PACKED)]
    lo = [lax.bitcast_convert_type(w << 16, F32) for w in words]
    hi = [lax.bitcast_convert_type(w & HIGH_HALF, F32) for w in words]
    return jnp.concatenate(lo + hi, axis=-1)


def _tile_copies(hbm, row0, rows, vmem, sem, to_hbm):
    copies = []
    for c in range(PACKED):
        h, v = hbm.at[pl.ds(row0, rows), c], vmem.at[c]
        copies.append(pltpu.make_async_copy(v, h, sem) if to_hbm else pltpu.make_async_copy(h, v, sem))
    return copies


def _start_row_gather(idx_ref, base, step, count, src_hbm, dst, sem, both_threads, inline=False):
    def start(j, c):
        for half in range(2):
            t = 2 * j + half
            pltpu.make_async_copy(src_hbm.at[idx_ref[base + t * step]], dst.at[:, t], sem).start(
                priority=half if both_threads else 0)
        return c

    if inline:
        for j in range(count // 2):
            start(j, 0)
    else:
        lax.fori_loop(0, count // 2, start, 0, unroll=2)


def _wait_row_gather(dst, sem):
    pltpu.make_async_copy(dst, dst, sem).wait()


def _ln0_kernel(x_ref, g_ref, b_ref, h_ref, hb_ref):
    y = _layer_norm(x_ref[...], g_ref[...], b_ref[...])
    h_ref[...] = y
    hb_ref[...] = y.astype(BF16)


def _ln0(x, g, b):
    n, d = x.shape
    row = pl.BlockSpec((LN_TM, d), lambda i: (i, 0))
    vec = pl.BlockSpec((1, d), lambda i: (0, 0))
    return pl.pallas_call(
        _ln0_kernel,
        grid=(n // LN_TM,),
        in_specs=[row, vec, vec],
        out_specs=[row, row],
        out_shape=[jax.ShapeDtypeStruct((n, d), F32), jax.ShapeDtypeStruct((n, d), BF16)],
        compiler_params=_params(("parallel",)),
        name="ln0",
    )(x, g.reshape(1, d), b.reshape(1, d))


def _inproj_kernel(hb_ref, w_ref, b_ref, o_ref, acc_s, *, dil):
    acc = jnp.dot(hb_ref[...], w_ref[...], preferred_element_type=F32) + b_ref[...]
    if dil == 1:
        for c in range(INPROJ_TN // LANES):
            o_ref[c, 0, 0] = acc[:, c * LANES:(c + 1) * LANES].astype(BF16)
    else:
        for c in range(INPROJ_TN // LANES):
            acc_s[c] = acc[:, c * LANES:(c + 1) * LANES]
        for c in range(INPROJ_TN // LANES):
            for r in range(dil):
                o_ref[c, 0, r] = acc_s[c, pl.ds(r, INPROJ_TM // dil, stride=dil), :].astype(BF16)


def _inproj(hb, w, b, bsz, s, dil, width, col_block):
    n, d = hb.shape
    tiles = s // INPROJ_TM
    assert INPROJ_TM % (16 * dil) == 0 and s % INPROJ_TM == 0 and width % INPROJ_TN == 0
    return pl.pallas_call(
        functools.partial(_inproj_kernel, dil=dil),
        grid=(n // INPROJ_TM, width // INPROJ_TN),
        in_specs=[pl.BlockSpec((INPROJ_TM, d), lambda i, j: (i, 0)),
                  pl.BlockSpec((d, INPROJ_TN), lambda i, j: (0, col_block(j))),
                  pl.BlockSpec((1, INPROJ_TN), lambda i, j: (0, col_block(j)))],
        out_specs=pl.BlockSpec((INPROJ_TN // LANES, 1, dil, INPROJ_TM // dil, LANES),
                               lambda i, j: (j, i // tiles, 0, i % tiles, 0)),
        out_shape=jax.ShapeDtypeStruct((width // LANES, bsz, dil, s // dil, LANES), BF16),
        scratch_shapes=[pltpu.VMEM((INPROJ_TN // LANES, INPROJ_TM, LANES), F32)],
        compiler_params=_params(("parallel", "arbitrary")),
        name=f"inproj_{dil}",
    )(hb, w, b)


def _na_bias_tiles(rpb, rows):
    heads, n_dr, n_dc = rpb.shape
    qc = np.arange(GRID_W)[:, None]
    kc = np.arange(GRID_W)[None, :]
    qcs = np.clip(qc - NA_KW // 2, 0, GRID_W - NA_KW)
    v_col = (kc >= qcs) & (kc < qcs + NA_KW)
    dc = np.clip(kc - qc + NA_KW - 1, 0, n_dc - 1)
    onehot = (dc[None] == np.arange(n_dc)[:, None, None]) & v_col[None]
    toep = jnp.einsum('hrd,dqk->hrqk', rpb.astype(F32), jnp.asarray(onehot, F32),
                      precision=lax.Precision.HIGHEST)
    toep = jnp.where(jnp.asarray(v_col), toep, MASK_VALUE)
    masked = jnp.full((heads, 1, GRID_W, GRID_W), MASK_VALUE, F32)
    blocks = jnp.concatenate([toep, masked], axis=1)
    i = np.arange(NA_QROWS)[:, None]
    j = np.arange(NA_KROWS)[None, :]
    sel = []
    for r0, ks in ((0, 0), (2 * NA_QROWS, 2 * NA_QROWS - NA_KH // 2), (rows - NA_QROWS, rows - NA_KROWS)):
        r, krow = r0 + i, ks + j
        start = np.clip(r - NA_KH // 2, 0, rows - NA_KH)
        v_row = (krow >= start) & (krow < start + NA_KH)
        sel.append(np.where(v_row, krow - r + NA_KH - 1, n_dr))
    sel = np.stack(sel)
    pairs = sel.reshape(3, NA_QROWS, NA_KROWS // 2, 2)
    distinct = sorted({tuple(p) for p in pairs.reshape(-1, 2).tolist()})
    plan = [[[distinct.index(tuple(pairs[v, i, c])) for c in range(NA_KROWS // 2)]
             for i in range(NA_QROWS)] for v in range(3)]
    left = jnp.take(blocks, jnp.asarray([p[0] for p in distinct], jnp.int32), axis=1)
    right = jnp.take(blocks, jnp.asarray([p[1] for p in distinct], jnp.int32), axis=1)
    return jnp.concatenate([left, right], axis=-1), plan


def _attend_blocks(nblocks, scores, finish):
    scored, weighted = {}, {}
    for t in range(nblocks + 2):
        if t < nblocks:
            scored[t] = scores(t)
        if 0 <= t - 1 < nblocks:
            s, vw = scored.pop(t - 1)
            m = jnp.max(s, axis=-1, keepdims=True)
            p = jnp.exp(s - m)
            weighted[t - 1] = (p.astype(BF16), vw, m, jnp.sum(p, axis=-1, keepdims=True))
        if 0 <= t - 2 < nblocks:
            p, vw, m, l = weighted.pop(t - 2)
            finish(t - 2, jnp.dot(p, vw, preferred_element_type=F32) / l, m, l)


def _scores(q, kw, bias):
    return lax.dot_general(q, kw, (((1,), (1,)), ((), ())), preferred_element_type=F32) + bias


def _edge_variant(blk, nblk):
    return jnp.where(blk == 0, 0, jnp.where(blk == nblk - 1, 2, 1))


def _na_kernel(q_ref, k_ref, v_ref, tile_ref, o_ref, tab_ref, *, rows, plan):
    tq = NA_QROWS * GRID_W

    @pl.when(pl.program_id(2) == 0)
    def _():
        for v in range(3):
            for i in range(NA_QROWS):
                for c in range(NA_KROWS // 2):
                    tab_ref[v, 0, i * GRID_W:(i + 1) * GRID_W, c * 2 * GRID_W:(c + 1) * 2 * GRID_W] = (
                        tile_ref[0, plan[v][i][c]])

    def scores(u):
        blk = pl.program_id(2) * NA_NSUB + u
        ks = jnp.clip(blk * NA_QROWS - NA_KH // 2, 0, rows - NA_KROWS)
        off = pl.multiple_of(ks * GRID_W, GRID_W)
        kw = k_ref[0, 0, pl.ds(off, NA_KROWS * GRID_W), :]
        vw = v_ref[0, 0, pl.ds(off, NA_KROWS * GRID_W), :]
        bias = tab_ref[_edge_variant(blk, rows // NA_QROWS), 0]
        return _scores(q_ref[0, 0, u * tq:(u + 1) * tq, :], kw, bias), vw

    def finish(u, o, m, l):
        o_ref[0, 0, u * tq:(u + 1) * tq, :] = o.astype(BF16)

    _attend_blocks(NA_NSUB, scores, finish)


def _neighborhood_attention(zq4, rpb):
    _, bsz, s, _ = zq4.shape
    rows = s // GRID_W
    assert rows % (NA_QROWS * NA_NSUB) == 0 and rows >= NA_KROWS + NA_QROWS and NA_KROWS % 2 == 0
    tiles, plan = _na_bias_tiles(rpb, rows)
    tq = NA_QROWS * GRID_W * NA_NSUB
    return pl.pallas_call(
        functools.partial(_na_kernel, rows=rows, plan=plan),
        grid=(bsz, NA_HEADS, s // tq),
        in_specs=[pl.BlockSpec((1, 1, tq, HEAD_DIM), lambda b, h, i: (QA_H + h, b, i, 0)),
                  pl.BlockSpec((1, 1, s, HEAD_DIM), lambda b, h, i: (KA_H + h, b, 0, 0)),
                  pl.BlockSpec((1, 1, s, HEAD_DIM), lambda b, h, i: (VA_H + h, b, 0, 0)),
                  pl.BlockSpec((1,) + tiles.shape[1:], lambda b, h, i: (h, 0, 0, 0))],
        out_specs=pl.BlockSpec((1, 1, tq, HEAD_DIM), lambda b, h, i: (h, b, i, 0)),
        out_shape=jax.ShapeDtypeStruct((NA_HEADS, bsz, s, HEAD_DIM), BF16),
        scratch_shapes=[pltpu.VMEM((3, 1, NA_QROWS * GRID_W, NA_KROWS * GRID_W), F32)],
        compiler_params=_params(("arbitrary", "arbitrary", "arbitrary")),
        name="na_attn",
    )(zq4, zq4, zq4, tiles)


def _alibi_slopes(n):
    return np.array([2.0 ** (-8.0 * (i + 1) / n) for i in range(n)], dtype=np.float32)


def _dil_bias_tables(slopes, dil):
    qi = np.arange(DIL_TQ)[:, None]
    kj = np.arange(DIL_TK)[None, :]
    dist = np.stack([np.abs(kj - qi + shift) for shift in (0, -DIL_HALF, -2 * DIL_HALF)])
    dist = jnp.asarray(dist, F32)[:, None]
    penalty = jnp.asarray(slopes, F32)[None, :, None, None] * (dist * float(dil))
    return jnp.where(dist <= DIL_HALF, -penalty, MASK_VALUE)


def _dil_kernel(q_ref, k_ref, v_ref, tab_ref, o_ref, lse_ref, *, length, nsub, nres):
    def scores(u):
        res, sub = divmod(u, nsub)
        blk = pl.program_id(3) * nsub + sub
        ks = jnp.clip(blk * DIL_TQ - DIL_HALF, 0, length - DIL_TK)
        off = pl.multiple_of(ks, DIL_HALF)
        kw = k_ref[0, 0, res, pl.ds(off, DIL_TK), :]
        vw = v_ref[0, 0, res, pl.ds(off, DIL_TK), :]
        bias = tab_ref[_edge_variant(blk, length // DIL_TQ), 0]
        return _scores(q_ref[0, 0, res, sub * DIL_TQ:(sub + 1) * DIL_TQ, :], kw, bias), vw

    def finish(u, o, m, l):
        res, sub = divmod(u, nsub)
        rows = slice(sub * DIL_TQ, (sub + 1) * DIL_TQ)
        o_ref[0, 0, res, rows, :] = o.astype(BF16)
        lse_ref[0, 0, res, rows, :] = jnp.broadcast_to(m + jnp.log(l), (DIL_TQ, LANES))

    _attend_blocks(nres * nsub, scores, finish)


def _dilated_group(zq, base, g, slopes):
    window, dil = DIL_GROUPS[g]
    assert (window // 2) // dil == DIL_HALF
    _, bsz, _, length, _ = zq.shape
    nsub = min(DIL_NSUB, length // DIL_TQ)
    nres = min(dil, DIL_NSUB // nsub)
    assert zq.shape[2] == dil and length % (DIL_TQ * nsub) == 0 and length >= DIL_TK and dil % nres == 0
    gh = DIL_HEADS_PER_GROUP
    tabs = _dil_bias_tables(slopes[g * gh:(g + 1) * gh], dil)
    tq = DIL_TQ * nsub
    qspec = pl.BlockSpec((1, 1, nres, tq, HEAD_DIM), lambda b, h, r, n: (base + h, b, r, n, 0))
    kvspec = lambda off: pl.BlockSpec((1, 1, nres, length, HEAD_DIM),
                                      lambda b, h, r, n: (base + off + h, b, r, 0, 0))
    ospec = pl.BlockSpec((1, 1, nres, tq, HEAD_DIM), lambda b, h, r, n: (h, b, r, n, 0))
    return pl.pallas_call(
        functools.partial(_dil_kernel, length=length, nsub=nsub, nres=nres),
        grid=(bsz, gh, dil // nres, length // tq),
        in_specs=[qspec, kvspec(gh), kvspec(2 * gh),
                  pl.BlockSpec((3, 1, DIL_TQ, DIL_TK), lambda b, h, r, n: (0, h, 0, 0))],
        out_specs=[ospec, ospec],
        out_shape=[jax.ShapeDtypeStruct((gh, bsz, dil, length, HEAD_DIM), BF16),
                   jax.ShapeDtypeStruct((gh, bsz, dil, length, HEAD_DIM), F32)],
        compiler_params=_params(("parallel", "parallel", "parallel", "arbitrary")),
        name=f"dil_attn_{dil}",
    )(zq, zq, zq, tabs)


def _mix_kernel(na_ref, o1_ref, o2_ref, o3_ref, l1_ref, l2_ref, l3_ref, hb_ref,
                wpa_ref, wpb_ref, wga_ref, wgb_ref, bga_ref, bgb_ref, m_ref, na_s, dil_s, o_s, l_s):
    @pl.when(pl.program_id(1) == 0)
    def _():
        for h in range(NA_HEADS):
            na_s[:, h * HEAD_DIM:(h + 1) * HEAD_DIM] = na_ref[h]
        for g, (o_ref, l_ref) in enumerate(((o1_ref, l1_ref), (o2_ref, l2_ref), (o3_ref, l3_ref))):
            dil = DIL_GROUPS[g][1]
            for h in range(DIL_HEADS_PER_GROUP):
                for r in range(dil):
                    rows = slice(None) if dil == 1 else pl.ds(r, MIX_TM // dil, stride=dil)
                    o_s[g, h, rows, :] = o_ref[h, 0, r].astype(F32)
                    l_s[g, h, rows, :] = l_ref[h, 0, r]
        for h in range(DIL_HEADS_PER_GROUP):
            ls = [l_s[g, h] for g in range(len(DIL_GROUPS))]
            mx = jnp.maximum(jnp.maximum(ls[0], ls[1]), ls[2])
            es = [jnp.exp(l - mx) for l in ls]
            den = es[0] + es[1] + es[2]
            acc = (es[0] * o_s[0, h] + es[1] * o_s[1, h] + es[2] * o_s[2, h]) / den
            dil_s[:, h * HEAD_DIM:(h + 1) * HEAD_DIM] = acc.astype(BF16)

    hb = hb_ref[...]
    ya = jnp.dot(na_s[...], wpa_ref[...], preferred_element_type=F32)
    yb = jnp.dot(dil_s[...], wpb_ref[...], preferred_element_type=F32)
    ga = jnp.dot(hb, wga_ref[...], preferred_element_type=F32) + bga_ref[...]
    gb = jnp.dot(hb, wgb_ref[...], preferred_element_type=F32) + bgb_ref[...]
    m_ref[...] = (jax.nn.sigmoid(ga) * ya + jax.nn.sigmoid(gb) * yb).astype(BF16)


def _mix(na, dil_o, dil_lse, hb, wpa, wpb, w_all, b_all, s):
    n, d = hb.shape
    tm, tn = MIX_TM, MIX_TN
    tiles = s // tm
    gh = DIL_HEADS_PER_GROUP
    ngroups = len(DIL_GROUPS)
    assert all(tm % (16 * dil) == 0 for _, dil in DIL_GROUPS) and s % tm == 0
    col = lambda k: pl.BlockSpec((k, tn), lambda i, j: (0, j))
    ga0, gb0 = QKV_WIDTH // tn, (QKV_WIDTH + d) // tn
    gate = lambda k, first: pl.BlockSpec((k, tn), lambda i, j: (0, first + j))
    grp = lambda dil: pl.BlockSpec((gh, 1, dil, tm // dil, HEAD_DIM),
                                   lambda i, j: (0, i // tiles, 0, i % tiles, 0))
    groups = [grp(dil) for _, dil in DIL_GROUPS]
    return pl.pallas_call(
        _mix_kernel,
        grid=(n // tm, d // tn),
        in_specs=[pl.BlockSpec((NA_HEADS, tm, HEAD_DIM), lambda i, j: (0, i, 0)), *groups, *groups,
                  pl.BlockSpec((tm, d), lambda i, j: (i, 0)),
                  col(NA_WIDTH), col(DIL_OUT_WIDTH),
                  gate(d, ga0), gate(d, gb0), gate(1, ga0), gate(1, gb0)],
        out_specs=pl.BlockSpec((tm, tn), lambda i, j: (i, j)),
        out_shape=jax.ShapeDtypeStruct((n, d), BF16),
        scratch_shapes=[pltpu.VMEM((tm, NA_WIDTH), BF16), pltpu.VMEM((tm, DIL_OUT_WIDTH), BF16),
                        pltpu.VMEM((ngroups, gh, tm, HEAD_DIM), F32),
                        pltpu.VMEM((ngroups, gh, tm, HEAD_DIM), F32)],
        compiler_params=_params(("parallel", "arbitrary")),
        name="mix",
    )(na, *dil_o, *dil_lse, hb, wpa, wpb, w_all, w_all, b_all, b_all)


def _outproj_kernel(m_ref, h_ref, wo_ref, bo_ref, g_ref, b_ref, wrh_ref, wrl_ref, br_ref,
                    h1_ref, h1p_hbm, lg_ref, hbuf, sem):
    i = pl.program_id(0)
    slot = i % 2

    def out_copies(step, sl):
        return _tile_copies(h1p_hbm, step * OUT_TM, OUT_TM, hbuf.at[sl], sem.at[sl], True)

    mix = jnp.dot(m_ref[...], wo_ref[...], preferred_element_type=F32) + bo_ref[...]
    h1 = _layer_norm(DN_ALPHA * h_ref[...] + mix, g_ref[...], b_ref[...])
    h1_ref[...] = h1

    @pl.when(i >= 2)
    def _():
        for cp in out_copies(i - 2, slot):
            cp.wait()

    _pack_rows(hbuf.at[slot], h1)
    for cp in out_copies(i, slot):
        cp.start()

    @pl.when(i == pl.num_programs(0) - 1)
    def _():
        for cp in out_copies(i, slot):
            cp.wait()

        @pl.when(i >= 1)
        def _():
            for cp in out_copies(i - 1, 1 - slot):
                cp.wait()

    hi = h1.astype(BF16)
    lo = (h1 - hi.astype(F32)).astype(BF16)
    lg = (jnp.dot(hi, wrh_ref[...], preferred_element_type=F32)
          + jnp.dot(lo, wrh_ref[...], preferred_element_type=F32)
          + jnp.dot(hi, wrl_ref[...], preferred_element_type=F32))
    lg_ref[...] = lg + br_ref[...]


def _outproj(m, h, wo, bo, g, b, wr_hi, wr_lo, br):
    n, d = h.shape
    tm = OUT_TM
    row = lambda w: pl.BlockSpec((tm, w), lambda i: (i, 0))
    full = lambda r, c: pl.BlockSpec((r, c), lambda i: (0, 0), pipeline_mode=pl.Buffered(1))
    return pl.pallas_call(
        _outproj_kernel,
        grid=(n // tm,),
        in_specs=[row(d), row(d), full(d, d), full(1, d), full(1, d), full(1, d),
                  full(d, ROUTER_PAD), full(d, ROUTER_PAD), full(1, ROUTER_PAD)],
        out_specs=[row(d), pl.BlockSpec(memory_space=pl.ANY), row(ROUTER_PAD)],
        out_shape=[jax.ShapeDtypeStruct((n, d), F32),
                   jax.ShapeDtypeStruct((n, PACKED, LANES), jnp.uint32),
                   jax.ShapeDtypeStruct((n, ROUTER_PAD), F32)],
        scratch_shapes=[pltpu.VMEM((2, PACKED, tm, LANES), jnp.uint32), pltpu.SemaphoreType.DMA((2,))],
        compiler_params=_params(("arbitrary",)),
        name="outproj",
    )(m, h, wo, bo.reshape(1, d), g.reshape(1, d), b.reshape(1, d), wr_hi, wr_lo, br)


def _first_argmax(vals, lane_f):
    top = jnp.max(vals, axis=-1, keepdims=True)
    idx = jnp.min(jnp.where(vals == top, lane_f, float(LANES)), axis=-1, keepdims=True)
    return top, idx


def _route_kernel(lg_ref, gate_ref, dest_ref, cnt_ref, base_s, start_s):
    phase = pl.program_id(0)

    @pl.when(pl.program_id(1) == 0)
    def _():
        @pl.when(phase == 1)
        def _():
            counts = base_s[...]
            cnt_ref[...] = counts
            blocks = jnp.ceil(counts * (1.0 / MOE_TB))
            before = (lax.broadcasted_iota(jnp.int32, (ROUTER_PAD, ROUTER_PAD), 0)
                      < lax.broadcasted_iota(jnp.int32, (ROUTER_PAD, ROUTER_PAD), 1)).astype(BF16)
            starts = jnp.dot(jnp.broadcast_to(blocks, (8, ROUTER_PAD)).astype(BF16), before,
                             preferred_element_type=F32)
            start_s[...] = starts[0:1] * float(MOE_TB)

        base_s[...] = jnp.zeros(base_s.shape, F32)

    lg = lg_ref[...]
    lane = lax.broadcasted_iota(jnp.int32, lg.shape, 1)
    lane_f = lane.astype(F32)
    g_mask = lane < N_GROUPS
    g_top, g_sel = _first_argmax(jnp.where(g_mask, lg, MASK_VALUE), lane_f)
    g_prob = 1.0 / jnp.sum(jnp.where(g_mask, jnp.exp(lg - g_top), 0.0), axis=-1, keepdims=True)
    first = N_GROUPS + g_sel * EXPERTS_PER_GROUP
    e_mask = (lane_f >= first) & (lane_f < first + EXPERTS_PER_GROUP)
    el = jnp.where(e_mask, lg, MASK_VALUE)
    v0, i0 = _first_argmax(el, lane_f)
    v1, i1 = _first_argmax(jnp.where(lane_f == i0, MASK_VALUE, el), lane_f)
    e1 = jnp.exp(v1 - v0)
    w0 = g_prob / (1.0 + e1)
    w1 = g_prob * e1 / (1.0 + e1)

    onehots = [lane_f == idx - N_GROUPS for idx in (i0, i1)]

    @pl.when(phase == 0)
    def _():
        base_s[...] = base_s[...] + sum(jnp.sum(oh.astype(F32), axis=0, keepdims=True) for oh in onehots)

    @pl.when(phase == 1)
    def _():
        tm = lg.shape[0]
        tri = (lax.broadcasted_iota(jnp.int32, (tm, tm), 1)
               < lax.broadcasted_iota(jnp.int32, (tm, tm), 0)).astype(BF16)
        dests = []
        for onehot in onehots:
            before = jnp.dot(tri, onehot.astype(BF16), preferred_element_type=F32) + base_s[...]
            dests.append(jnp.sum(jnp.where(onehot, before + start_s[...], 0.0), axis=-1, keepdims=True))
            base_s[...] = base_s[...] + jnp.sum(onehot.astype(F32), axis=0, keepdims=True)
        gate_ref[...] = jnp.where(lane == 0, w0, jnp.where(lane == 1, w1, 0.0))
        cols = jnp.where(lane == 0, dests[0], jnp.where(lane == 1, dests[1], 0.0))
        dest_ref[...] = jnp.transpose(cols)[0:8, :].astype(jnp.int32)


def _invert_kernel(dest_ref, lo_ref, hi_ref, tok_ref, *, n):
    def clear_range(e, c):
        def clear(s, c2):
            tok_ref[s] = 0
            return c2

        lax.fori_loop(lo_ref[e], hi_ref[e], clear, 0)
        return c

    lax.fori_loop(0, lo_ref.shape[0], clear_range, 0)
    for k in range(TOP_K):
        def place(t, c, k=k):
            tok_ref[dest_ref[k * n + t]] = t
            return c

        lax.fori_loop(0, n, place, 0, unroll=8)


def _invert_slots(dest, unused_lo, unused_hi, n, slots):
    assert dest.shape[0] == TOP_K * n
    return pl.pallas_call(
        functools.partial(_invert_kernel, n=n),
        grid_spec=pltpu.PrefetchScalarGridSpec(
            num_scalar_prefetch=3, grid=(1,), in_specs=[],
            out_specs=pl.BlockSpec(memory_space=pltpu.SMEM)),
        out_shape=jax.ShapeDtypeStruct((slots,), jnp.int32),
        compiler_params=_params(("arbitrary",)),
        name="invert_slots",
    )(dest, unused_lo, unused_hi)


def _route(logits, n):
    tm = ROUTE_TM
    one = pl.BlockSpec((1, ROUTER_PAD), lambda p, i: (0, 0))
    gate, dest, counts = pl.pallas_call(
        _route_kernel,
        grid=(2, n // tm),
        in_specs=[pl.BlockSpec((tm, ROUTER_PAD), lambda p, i: (i, 0))],
        out_specs=[pl.BlockSpec((tm, ROUTER_PAD), lambda p, i: (i * p, 0)),
                   pl.BlockSpec((8, tm), lambda p, i: (0, i * p)), one],
        out_shape=[jax.ShapeDtypeStruct((n, ROUTER_PAD), F32),
                   jax.ShapeDtypeStruct((8, n), jnp.int32),
                   jax.ShapeDtypeStruct((1, ROUTER_PAD), F32)],
        scratch_shapes=[pltpu.VMEM((1, ROUTER_PAD), F32), pltpu.VMEM((1, ROUTER_PAD), F32)],
        compiler_params=_params(("arbitrary", "arbitrary")),
        name="route",
    )(logits)
    gate = gate[:, :TOP_K]
    dest = dest[:TOP_K].reshape(TOP_K * n)
    a = n * TOP_K
    counts = counts[0, :N_EXPERTS].astype(jnp.int32)
    pcounts = (counts + MOE_TB - 1) // MOE_TB * MOE_TB
    pends = jnp.cumsum(pcounts)
    nb = a // MOE_TB + N_EXPERTS
    slots = nb * MOE_TB
    unused_lo = jnp.concatenate([pends - pcounts + counts, pends[-1:]])
    unused_hi = jnp.concatenate([pends, jnp.full((1,), slots, jnp.int32)])
    slot_tok = _invert_slots(dest, unused_lo, unused_hi, n, slots)
    block_start = jnp.arange(nb, dtype=jnp.int32) * MOE_TB
    block_e = jnp.minimum(jnp.sum((pends[None, :] <= block_start[:, None]).astype(jnp.int32), axis=1),
                          N_EXPERTS - 1)
    n_active = (pends[-1] // MOE_TB).astype(jnp.int32).reshape(1)
    ids = jnp.arange(N_EXPERTS, dtype=jnp.int32)
    later = (counts[None, :] > 0) & (ids[None, :] > ids[:, None])
    nxt = jnp.min(jnp.where(later, ids[None, :], N_EXPERTS), axis=1)
    nxt = jnp.where(nxt == N_EXPERTS, -1, nxt)
    block_next = jnp.sum(jnp.where(block_e[:, None] == ids[None, :], nxt[None, :], 0), axis=1).astype(jnp.int32)
    return dest, gate, slot_tok, block_e, block_next, n_active


def _cast_weights(src, dst):
    rows, cols = src.shape
    step = CAST_VREGS * 8 * LANES // cols

    def body(r, c):
        sl = pl.ds(pl.multiple_of(r * step, step), step)
        dst[sl, :] = src[sl, :].astype(BF16)
        return c

    lax.fori_loop(0, rows // step, body, 0, unroll=2)


def _expert_kernel(be_ref, nx_ref, nact_ref, tok_ref, h1p_hbm, wg_hbm, wu_hbm, wd_hbm, ys_hbm,
                   xbuf, ybuf, wg_f, wu_f, wd_f, wg_s, wu_s, wd_s, gsem, osem, wsem):
    i = pl.program_id(0)
    slot = i % 2
    xslot = i % GATHER_SLOTS
    nact = nact_ref[0]

    def gather(step, inline=False):
        sl = step % GATHER_SLOTS
        _start_row_gather(tok_ref, step * MOE_TB, 1, MOE_TB, h1p_hbm, xbuf.at[sl], gsem.at[sl], False,
                          inline)

    def out_copies(step, sl):
        return _tile_copies(ys_hbm, step * MOE_TB, MOE_TB, ybuf.at[sl], osem.at[sl], True)

    def weight_copies(e):
        return [pltpu.make_async_copy(src.at[e], dst, wsem)
                for src, dst in ((wg_hbm, wg_f), (wu_hbm, wu_f), (wd_hbm, wd_f))]

    @pl.when((i == 0) & (nact > 0))
    def _():
        for cp in weight_copies(be_ref[0]):
            cp.start(priority=WEIGHT_DMA_PRIORITY)
        gather(0)

    @pl.when((i == 0) & (nact > 1))
    def _():
        gather(1)

    @pl.when(i >= 2)
    def _():
        for cp in out_copies(i - 2, slot):
            cp.wait()

    @pl.when(i < nact)
    def _():
        expert = be_ref[i]

        @pl.when((i == 0) | (expert != be_ref[jnp.maximum(i - 1, 0)]))
        def _():
            for cp in weight_copies(expert):
                cp.wait()
            _cast_weights(wg_f, wg_s)
            _cast_weights(wu_f, wu_s)
            _cast_weights(wd_f, wd_s)

            @pl.when(nx_ref[i] >= 0)
            def _():
                for cp in weight_copies(nx_ref[i]):
                    cp.start(priority=WEIGHT_DMA_PRIORITY)

        _wait_row_gather(xbuf.at[xslot], gsem.at[xslot])

        def block(prefetch):
            if prefetch:
                gather(i + 2, inline=True)
            xb = _unpack_rows(xbuf.at[xslot]).astype(BF16)
            gate = jnp.dot(xb, wg_s[...], preferred_element_type=F32)
            up = jnp.dot(xb, wu_s[...], preferred_element_type=F32)
            hid = (jax.nn.silu(gate) * up).astype(BF16)
            _pack_rows(ybuf.at[slot], jnp.dot(hid, wd_s[...], preferred_element_type=F32))

        pl.when(i + 2 < nact)(functools.partial(block, True))
        pl.when(i + 2 >= nact)(functools.partial(block, False))

    @pl.when(i >= nact)
    def _():
        ybuf[slot] = jnp.zeros(ybuf.shape[1:], jnp.uint32)

    for cp in out_copies(i, slot):
        cp.start()

    @pl.when(i == pl.num_programs(0) - 1)
    def _():
        for cp in out_copies(i, slot):
            cp.wait()

        @pl.when(i >= 1)
        def _():
            for cp in out_copies(i - 1, 1 - slot):
                cp.wait()


def _experts(h1p, slot_tok, block_e, block_next, n_active, w_gate, w_up, w_down):
    d = D_MODEL
    slots = slot_tok.shape[0]
    nb = slots // MOE_TB
    any_spec = pl.BlockSpec(memory_space=pl.ANY)
    grid_spec = pltpu.PrefetchScalarGridSpec(
        num_scalar_prefetch=4,
        grid=(nb,),
        in_specs=[any_spec, any_spec, any_spec, any_spec],
        out_specs=any_spec,
        scratch_shapes=[pltpu.VMEM((GATHER_SLOTS, PACKED, MOE_TB, LANES), jnp.uint32),
                        pltpu.VMEM((2, PACKED, MOE_TB, LANES), jnp.uint32),
                        pltpu.VMEM((d, D_EXPERT), F32), pltpu.VMEM((d, D_EXPERT), F32),
                        pltpu.VMEM((D_EXPERT, d), F32),
                        pltpu.VMEM((d, D_EXPERT), BF16), pltpu.VMEM((d, D_EXPERT), BF16),
                        pltpu.VMEM((D_EXPERT, d), BF16),
                        pltpu.SemaphoreType.DMA((GATHER_SLOTS,)), pltpu.SemaphoreType.DMA((2,)),
                        pltpu.SemaphoreType.DMA(())])
    return pl.pallas_call(
        _expert_kernel,
        grid_spec=grid_spec,
        out_shape=jax.ShapeDtypeStruct((slots, PACKED, LANES), jnp.uint32),
        compiler_params=_params(("arbitrary",)),
        name="experts",
    )(block_e, block_next, n_active, slot_tok, h1p, w_gate, w_up, w_down)


def _final_kernel(dest_ref, ys_hbm, h1_ref, gate_ref, g_ref, b_ref, o_ref, ybuf, gsem):
    i = pl.program_id(0)
    steps = pl.num_programs(0)
    slot = i % GATHER_SLOTS

    def fetch(step, inline=False):
        sl = step % GATHER_SLOTS
        for k in range(TOP_K):
            _start_row_gather(dest_ref, k * steps * FIN_TM + step * FIN_TM, 1, FIN_TM, ys_hbm,
                              ybuf.at[sl, k], gsem.at[sl], True, inline)

    @pl.when(i == 0)
    def _():
        fetch(0)

    @pl.when((i == 0) & (steps > 1))
    def _():
        fetch(1)

    _wait_row_gather(ybuf.at[slot], gsem.at[slot])

    def combine(prefetch):
        if prefetch:
            fetch(i + 2, inline=True)
        gate = gate_ref[...]
        ffn = (_unpack_rows(ybuf.at[slot, 0]) * gate[:, 0:1]
               + _unpack_rows(ybuf.at[slot, 1]) * gate[:, 1:2])
        o_ref[...] = _layer_norm(DN_ALPHA * h1_ref[...] + ffn, g_ref[...], b_ref[...])

    pl.when(i + 2 < steps)(functools.partial(combine, True))
    pl.when(i + 2 >= steps)(functools.partial(combine, False))


def _final(dest, ys, h1, gate, g, b):
    n, d = h1.shape
    grid_spec = pltpu.PrefetchScalarGridSpec(
        num_scalar_prefetch=1,
        grid=(n // FIN_TM,),
        in_specs=[pl.BlockSpec(memory_space=pl.ANY),
                  pl.BlockSpec((FIN_TM, d), lambda i, ds: (i, 0)),
                  pl.BlockSpec((FIN_TM, TOP_K), lambda i, ds: (i, 0)),
                  pl.BlockSpec((1, d), lambda i, ds: (0, 0)),
                  pl.BlockSpec((1, d), lambda i, ds: (0, 0))],
        out_specs=pl.BlockSpec((FIN_TM, d), lambda i, ds: (i, 0)),
        scratch_shapes=[pltpu.VMEM((GATHER_SLOTS, TOP_K, PACKED, FIN_TM, LANES), jnp.uint32),
                        pltpu.SemaphoreType.DMA((GATHER_SLOTS,))])
    return pl.pallas_call(
        _final_kernel,
        grid_spec=grid_spec,
        out_shape=jax.ShapeDtypeStruct((n, d), F32),
        compiler_params=_params(("arbitrary",)),
        name="final",
    )(dest, ys, h1, gate, g.reshape(1, d), b.reshape(1, d))


def kernel(x, ln0_g, ln0_b, w_in, b_in, rpb, w_proj_a, w_proj_b, w_o, b_o, ln1_g, ln1_b,
           w_router_group, b_router_group, w_router_expert, b_router_expert,
           w_gate, w_up, w_down, ln2_g, ln2_b):
    bsz, s, d = x.shape
    n = bsz * s
    assert d == D_MODEL and w_in.shape[0] == DEPTH
    scale = HEAD_DIM ** -0.5

    col_scale = np.ones((w_in.shape[2],), np.float32)
    col_scale[:NA_WIDTH] = scale
    col_scale[3 * NA_WIDTH:3 * NA_WIDTH + DIL_WIDTH] = scale
    w_all = (w_in[0] * col_scale).astype(BF16)
    b_all = (b_in[0] * col_scale).reshape(1, -1)
    assert DIL_OUT_WIDTH == INPROJ_TN
    na_blocks, ngroups = 3 * NA_WIDTH // INPROJ_TN, len(DIL_GROUPS)
    w_r = jnp.concatenate([w_router_group[0], w_router_expert[0]], axis=1)
    w_r = jnp.pad(w_r, ((0, 0), (0, ROUTER_PAD - w_r.shape[1])))
    w_r_hi = w_r.astype(BF16)
    w_r_lo = (w_r - w_r_hi.astype(F32)).astype(BF16)
    b_r = jnp.pad(jnp.concatenate([b_router_group[0], b_router_expert[0]]),
                  (0, ROUTER_PAD - N_GROUPS - N_EXPERTS)).reshape(1, ROUTER_PAD)
    slopes = _alibi_slopes(DIL_HEADS)

    h, hb = _ln0(x.reshape(n, d), ln0_g, ln0_b)
    zq = [_inproj(hb, w_all, b_all, bsz, s, DIL_GROUPS[0][1], 3 * NA_WIDTH + 3 * DIL_OUT_WIDTH,
                  lambda j: jnp.where(j < na_blocks, j, na_blocks + (j - na_blocks) * ngroups))]
    for g in range(1, ngroups):
        zq.append(_inproj(hb, w_all, b_all, bsz, s, DIL_GROUPS[g][1], 3 * DIL_OUT_WIDTH,
                          lambda j, g=g: na_blocks + j * ngroups + g))
    na = _neighborhood_attention(zq[0].reshape(-1, bsz, s, HEAD_DIM), rpb[0])
    na = na.reshape(NA_HEADS, n, HEAD_DIM)
    dil = [_dilated_group(zq[g], DIL0_H if g == 0 else 0, g, slopes) for g in range(ngroups)]
    m = _mix(na, [o for o, _ in dil], [l for _, l in dil], hb,
             w_proj_a[0].astype(BF16), w_proj_b[0].astype(BF16), w_all, b_all, s)
    h1, h1p, logits = _outproj(m, h, w_o[0].astype(BF16), b_o[0], ln1_g[0], ln1_b[0],
                               w_r_hi, w_r_lo, b_r)
    dest, gate, slot_tok, block_e, block_next, n_active = _route(logits, n)
    ys = _experts(h1p, slot_tok, block_e, block_next, n_active, w_gate[0], w_up[0], w_down[0])
    out = _final(dest, ys, h1, gate, ln2_g[0], ln2_b[0])
    return out.reshape(bsz, s, d)
```

```python
import functools

import numpy as np
import jax
import jax.numpy as jnp
from jax import lax
from jax.experimental import pallas as pl
from jax.experimental.pallas import tpu as pltpu

F32 = jnp.float32
BF16 = jnp.bfloat16

D_MODEL = 2048
HEAD_DIM = 128
GRID_W = 64
NA_HEADS = 8
NA_KH = 8
NA_KW = 16
DIL_GROUPS = ((128, 1), (512, 4), (2048, 16))
DIL_HEADS_PER_GROUP = 4
DIL_HEADS = DIL_HEADS_PER_GROUP * len(DIL_GROUPS)
N_GROUPS = 8
EXPERTS_PER_GROUP = 8
N_EXPERTS = N_GROUPS * EXPERTS_PER_GROUP
TOP_K = 2
D_EXPERT = D_MODEL // 4
LN_EPS = 1e-5
DEPTH = 1
DN_ALPHA = (2 * DEPTH) ** 0.25
NA_WIDTH = NA_HEADS * HEAD_DIM
DIL_WIDTH = DIL_HEADS * HEAD_DIM
DIL_OUT_WIDTH = DIL_HEADS_PER_GROUP * HEAD_DIM
QKV_WIDTH = 3 * NA_WIDTH + 3 * DIL_WIDTH
QKV_HEADS = QKV_WIDTH // HEAD_DIM
LANES = 128
CHUNKS = D_MODEL // LANES
PACKED = CHUNKS // 2
MASK_VALUE = -1e30

QA_H, KA_H, VA_H = 0, NA_HEADS, 2 * NA_HEADS
DIL0_H = 3 * NA_HEADS

LN_TM = 512
INPROJ_TM, INPROJ_TN = 2048, 512
NA_QROWS = 4
NA_KROWS = 12
NA_NSUB = 8
DIL_NSUB = 8
DIL_TQ = 256
DIL_HALF = 64
DIL_TK = DIL_TQ + 2 * DIL_HALF
MIX_TM, MIX_TN = 512, 512
OUT_TM = 512
ROUTER_PAD = 128
ROUTE_TM = 512
MOE_TB = 128
FIN_TM = 128
CAST_VREGS = 32
GATHER_SLOTS = 3
WEIGHT_DMA_PRIORITY = 1
VMEM_LIMIT = 56 * 1024 * 1024


def _params(sem, limit=VMEM_LIMIT):
    return pltpu.CompilerParams(dimension_semantics=sem, vmem_limit_bytes=limit)


def _layer_norm(x, g, b):
    mu = jnp.mean(x, axis=-1, keepdims=True)
    xc = x - mu
    var = jnp.mean(xc * xc, axis=-1, keepdims=True)
    return xc * lax.rsqrt(var + LN_EPS) * g + b


HIGH_HALF = np.uint32(0xFFFF0000)


def _bf16_bits(x):
    return lax.bitcast_convert_type(x.astype(BF16).astype(F32), jnp.uint32)


def _pack_rows(ref, val):
    for c in range(PACKED):
        lo = _bf16_bits(val[:, c * LANES:(c + 1) * LANES])
        hi = _bf16_bits(val[:, (c + PACKED) * LANES:(c + PACKED + 1) * LANES])
        ref[c] = (lo >> 16) | (hi & HIGH_HALF)


def _unpack_rows(ref):
    words = [ref[c] for c in range(PACKED)]
    lo = [lax.bitcast_convert_type(w << 16, F32) for w in words]
    hi = [lax.bitcast_convert_type(w & HIGH_HALF, F32) for w in words]
    return jnp.concatenate(lo + hi, axis=-1)


def _tile_copies(hbm, row0, rows, vmem, sem, to_hbm):
    copies = []
    for c in range(PACKED):
        h, v = hbm.at[pl.ds(row0, rows), c], vmem.at[c]
        copies.append(pltpu.make_async_copy(v, h, sem) if to_hbm else pltpu.make_async_copy(h, v, sem))
    return copies


def _start_row_gather(idx_ref, base, step, count, src_hbm, dst, sem, both_threads, inline=False):
    def start(j, c):
        for half in range(2):
            t = 2 * j + half
            pltpu.make_async_copy(src_hbm.at[idx_ref[base + t * step]], dst.at[:, t], sem).start(
                priority=half if both_threads else 0)
        return c

    if inline:
        for j in range(count // 2):
            start(j, 0)
    else:
        lax.fori_loop(0, count // 2, start, 0, unroll=2)


def _wait_row_gather(dst, sem):
    pltpu.make_async_copy(dst, dst, sem).wait()


def _ln0_kernel(x_ref, g_ref, b_ref, hb_ref):
    hb_ref[...] = _layer_norm(x_ref[...], g_ref[...], b_ref[...]).astype(BF16)


def _ln0(x, g, b):
    n, d = x.shape
    row = pl.BlockSpec((LN_TM, d), lambda i: (i, 0))
    vec = pl.BlockSpec((1, d), lambda i: (0, 0))
    return pl.pallas_call(
        _ln0_kernel,
        grid=(n // LN_TM,),
        in_specs=[row, vec, vec],
        out_specs=row,
        out_shape=jax.ShapeDtypeStruct((n, d), BF16),
        compiler_params=_params(("parallel",)),
        name="ln0",
    )(x, g.reshape(1, d), b.reshape(1, d))


def _inproj_kernel(hb_ref, w_ref, b_ref, o_ref, acc_s, *, dil):
    acc = jnp.dot(hb_ref[...], w_ref[...], preferred_element_type=F32) + b_ref[...]
    if dil == 1:
        for c in range(INPROJ_TN // LANES):
            o_ref[c, 0, 0] = acc[:, c * LANES:(c + 1) * LANES].astype(BF16)
    else:
        for c in range(INPROJ_TN // LANES):
            acc_s[c] = acc[:, c * LANES:(c + 1) * LANES]
        for c in range(INPROJ_TN // LANES):
            for r in range(dil):
                o_ref[c, 0, r] = acc_s[c, pl.ds(r, INPROJ_TM // dil, stride=dil), :].astype(BF16)


def _inproj(hb, w, b, bsz, s, dil, width, col_block):
    n, d = hb.shape
    tiles = s // INPROJ_TM
    assert INPROJ_TM % (16 * dil) == 0 and s % INPROJ_TM == 0 and width % INPROJ_TN == 0
    return pl.pallas_call(
        functools.partial(_inproj_kernel, dil=dil),
        grid=(n // INPROJ_TM, width // INPROJ_TN),
        in_specs=[pl.BlockSpec((INPROJ_TM, d), lambda i, j: (i, 0)),
                  pl.BlockSpec((d, INPROJ_TN), lambda i, j: (0, col_block(j))),
                  pl.BlockSpec((1, INPROJ_TN), lambda i, j: (0, col_block(j)))],
        out_specs=pl.BlockSpec((INPROJ_TN // LANES, 1, dil, INPROJ_TM // dil, LANES),
                               lambda i, j: (j, i // tiles, 0, i % tiles, 0)),
        out_shape=jax.ShapeDtypeStruct((width // LANES, bsz, dil, s // dil, LANES), BF16),
        scratch_shapes=[pltpu.VMEM((INPROJ_TN // LANES, INPROJ_TM, LANES), F32)],
        compiler_params=_params(("parallel", "arbitrary")),
        name=f"inproj_{dil}",
    )(hb, w, b)


def _na_bias_tiles(rpb, rows):
    heads, n_dr, n_dc = rpb.shape
    qc = np.arange(GRID_W)[:, None]
    kc = np.arange(GRID_W)[None, :]
    qcs = np.clip(qc - NA_KW // 2, 0, GRID_W - NA_KW)
    v_col = (kc >= qcs) & (kc < qcs + NA_KW)
    dc = np.clip(kc - qc + NA_KW - 1, 0, n_dc - 1)
    onehot = (dc[None] == np.arange(n_dc)[:, None, None]) & v_col[None]
    toep = jnp.einsum('hrd,dqk->hrqk', rpb.astype(F32), jnp.asarray(onehot, F32),
                      precision=lax.Precision.HIGHEST)
    toep = jnp.where(jnp.asarray(v_col), toep, MASK_VALUE)
    masked = jnp.full((heads, 1, GRID_W, GRID_W), MASK_VALUE, F32)
    blocks = jnp.concatenate([toep, masked], axis=1)
    i = np.arange(NA_QROWS)[:, None]
    j = np.arange(NA_KROWS)[None, :]
    sel = []
    for r0, ks in ((0, 0), (2 * NA_QROWS, 2 * NA_QROWS - NA_KH // 2), (rows - NA_QROWS, rows - NA_KROWS)):
        r, krow = r0 + i, ks + j
        start = np.clip(r - NA_KH // 2, 0, rows - NA_KH)
        v_row = (krow >= start) & (krow < start + NA_KH)
        sel.append(np.where(v_row, krow - r + NA_KH - 1, n_dr))
    sel = np.stack(sel)
    pairs = sel.reshape(3, NA_QROWS, NA_KROWS // 2, 2)
    distinct = sorted({tuple(p) for p in pairs.reshape(-1, 2).tolist()})
    plan = [[[distinct.index(tuple(pairs[v, i, c])) for c in range(NA_KROWS // 2)]
             for i in range(NA_QROWS)] for v in range(3)]
    left = jnp.take(blocks, jnp.asarray([p[0] for p in distinct], jnp.int32), axis=1)
    right = jnp.take(blocks, jnp.asarray([p[1] for p in distinct], jnp.int32), axis=1)
    return jnp.concatenate([left, right], axis=-1), plan


def _attend_blocks(nblocks, scores, finish):
    scored, weighted = {}, {}
    for t in range(nblocks + 2):
        if t < nblocks:
            scored[t] = scores(t)
        if 0 <= t - 1 < nblocks:
            s, vw = scored.pop(t - 1)
            m = jnp.max(s, axis=-1, keepdims=True)
            p = jnp.exp(s - m)
            weighted[t - 1] = (p.astype(BF16), vw, m, jnp.sum(p, axis=-1, keepdims=True))
        if 0 <= t - 2 < nblocks:
            p, vw, m, l = weighted.pop(t - 2)
            finish(t - 2, jnp.dot(p, vw, preferred_element_type=F32) / l, m, l)


def _scores(q, kw, bias):
    return lax.dot_general(q, kw, (((1,), (1,)), ((), ())), preferred_element_type=F32) + bias


def _edge_variant(blk, nblk):
    return jnp.where(blk == 0, 0, jnp.where(blk == nblk - 1, 2, 1))


def _na_kernel(q_ref, k_ref, v_ref, tile_ref, o_ref, tab_ref, *, rows, plan):
    tq = NA_QROWS * GRID_W

    @pl.when(pl.program_id(2) == 0)
    def _():
        for v in range(3):
            for i in range(NA_QROWS):
                for c in range(NA_KROWS // 2):
                    tab_ref[v, 0, i * GRID_W:(i + 1) * GRID_W, c * 2 * GRID_W:(c + 1) * 2 * GRID_W] = (
                        tile_ref[0, plan[v][i][c]])

    def scores(u):
        blk = pl.program_id(2) * NA_NSUB + u
        ks = jnp.clip(blk * NA_QROWS - NA_KH // 2, 0, rows - NA_KROWS)
        off = pl.multiple_of(ks * GRID_W, GRID_W)
        kw = k_ref[0, 0, pl.ds(off, NA_KROWS * GRID_W), :]
        vw = v_ref[0, 0, pl.ds(off, NA_KROWS * GRID_W), :]
        bias = tab_ref[_edge_variant(blk, rows // NA_QROWS), 0]
        return _scores(q_ref[0, 0, u * tq:(u + 1) * tq, :], kw, bias), vw

    def finish(u, o, m, l):
        o_ref[0, 0, u * tq:(u + 1) * tq, :] = o.astype(BF16)

    _attend_blocks(NA_NSUB, scores, finish)


def _neighborhood_attention(zq4, rpb):
    _, bsz, s, _ = zq4.shape
    rows = s // GRID_W
    assert rows % (NA_QROWS * NA_NSUB) == 0 and rows >= NA_KROWS + NA_QROWS and NA_KROWS % 2 == 0
    tiles, plan = _na_bias_tiles(rpb, rows)
    tq = NA_QROWS * GRID_W * NA_NSUB
    return pl.pallas_call(
        functools.partial(_na_kernel, rows=rows, plan=plan),
        grid=(bsz, NA_HEADS, s // tq),
        in_specs=[pl.BlockSpec((1, 1, tq, HEAD_DIM), lambda b, h, i: (QA_H + h, b, i, 0)),
                  pl.BlockSpec((1, 1, s, HEAD_DIM), lambda b, h, i: (KA_H + h, b, 0, 0)),
                  pl.BlockSpec((1, 1, s, HEAD_DIM), lambda b, h, i: (VA_H + h, b, 0, 0)),
                  pl.BlockSpec((1,) + tiles.shape[1:], lambda b, h, i: (h, 0, 0, 0))],
        out_specs=pl.BlockSpec((1, 1, tq, HEAD_DIM), lambda b, h, i: (h, b, i, 0)),
        out_shape=jax.ShapeDtypeStruct((NA_HEADS, bsz, s, HEAD_DIM), BF16),
        scratch_shapes=[pltpu.VMEM((3, 1, NA_QROWS * GRID_W, NA_KROWS * GRID_W), F32)],
        compiler_params=_params(("arbitrary", "arbitrary", "arbitrary")),
        name="na_attn",
    )(zq4, zq4, zq4, tiles)


def _alibi_slopes(n):
    return np.array([2.0 ** (-8.0 * (i + 1) / n) for i in range(n)], dtype=np.float32)


def _dil_bias_tables(slopes, dil):
    qi = np.arange(DIL_TQ)[:, None]
    kj = np.arange(DIL_TK)[None, :]
    dist = np.stack([np.abs(kj - qi + shift) for shift in (0, -DIL_HALF, -2 * DIL_HALF)])
    dist = jnp.asarray(dist, F32)[:, None]
    penalty = jnp.asarray(slopes, F32)[None, :, None, None] * (dist * float(dil))
    return jnp.where(dist <= DIL_HALF, -penalty, MASK_VALUE)


def _dil_kernel(q_ref, k_ref, v_ref, tab_ref, o_ref, lse_ref, *, length, nsub, nres):
    def scores(u):
        res, sub = divmod(u, nsub)
        blk = pl.program_id(3) * nsub + sub
        ks = jnp.clip(blk * DIL_TQ - DIL_HALF, 0, length - DIL_TK)
        off = pl.multiple_of(ks, DIL_HALF)
        kw = k_ref[0, 0, res, pl.ds(off, DIL_TK), :]
        vw = v_ref[0, 0, res, pl.ds(off, DIL_TK), :]
        bias = tab_ref[_edge_variant(blk, length // DIL_TQ), 0]
        return _scores(q_ref[0, 0, res, sub * DIL_TQ:(sub + 1) * DIL_TQ, :], kw, bias), vw

    def finish(u, o, m, l):
        res, sub = divmod(u, nsub)
        rows = slice(sub * DIL_TQ, (sub + 1) * DIL_TQ)
        o_ref[0, 0, res, rows, :] = o.astype(BF16)
        lse_ref[0, 0, res, rows, :] = jnp.broadcast_to(m + jnp.log(l), (DIL_TQ, LANES))

    _attend_blocks(nres * nsub, scores, finish)


def _dilated_group(zq, base, g, slopes):
    window, dil = DIL_GROUPS[g]
    assert (window // 2) // dil == DIL_HALF
    _, bsz, _, length, _ = zq.shape
    nsub = min(DIL_NSUB, length // DIL_TQ)
    nres = min(dil, DIL_NSUB // nsub)
    assert zq.shape[2] == dil and length % (DIL_TQ * nsub) == 0 and length >= DIL_TK and dil % nres == 0
    gh = DIL_HEADS_PER_GROUP
    tabs = _dil_bias_tables(slopes[g * gh:(g + 1) * gh], dil)
    tq = DIL_TQ * nsub
    qspec = pl.BlockSpec((1, 1, nres, tq, HEAD_DIM), lambda b, h, r, n: (base + h, b, r, n, 0))
    kvspec = lambda off: pl.BlockSpec((1, 1, nres, length, HEAD_DIM),
                                      lambda b, h, r, n: (base + off + h, b, r, 0, 0))
    ospec = pl.BlockSpec((1, 1, nres, tq, HEAD_DIM), lambda b, h, r, n: (h, b, r, n, 0))
    return pl.pallas_call(
        functools.partial(_dil_kernel, length=length, nsub=nsub, nres=nres),
        grid=(bsz, gh, dil // nres, length // tq),
        in_specs=[qspec, kvspec(gh), kvspec(2 * gh),
                  pl.BlockSpec((3, 1, DIL_TQ, DIL_TK), lambda b, h, r, n: (0, h, 0, 0))],
        out_specs=[ospec, ospec],
        out_shape=[jax.ShapeDtypeStruct((gh, bsz, dil, length, HEAD_DIM), BF16),
                   jax.ShapeDtypeStruct((gh, bsz, dil, length, HEAD_DIM), F32)],
        compiler_params=_params(("parallel", "parallel", "parallel", "arbitrary")),
        name=f"dil_attn_{dil}",
    )(zq, zq, zq, tabs)


def _mix_kernel(na_ref, o1_ref, o2_ref, o3_ref, l1_ref, l2_ref, l3_ref, hb_ref,
                wpa_ref, wpb_ref, wga_ref, wgb_ref, bga_ref, bgb_ref, m_ref, na_s, dil_s, o_s, l_s):
    @pl.when(pl.program_id(1) == 0)
    def _():
        for h in range(NA_HEADS):
            na_s[:, h * HEAD_DIM:(h + 1) * HEAD_DIM] = na_ref[h]
        for g, (o_ref, l_ref) in enumerate(((o1_ref, l1_ref), (o2_ref, l2_ref), (o3_ref, l3_ref))):
            dil = DIL_GROUPS[g][1]
            for h in range(DIL_HEADS_PER_GROUP):
                for r in range(dil):
                    rows = slice(None) if dil == 1 else pl.ds(r, MIX_TM // dil, stride=dil)
                    o_s[g, h, rows, :] = o_ref[h, 0, r].astype(F32)
                    l_s[g, h, rows, :] = l_ref[h, 0, r]
        for h in range(DIL_HEADS_PER_GROUP):
            ls = [l_s[g, h] for g in range(len(DIL_GROUPS))]
            mx = jnp.maximum(jnp.maximum(ls[0], ls[1]), ls[2])
            es = [jnp.exp(l - mx) for l in ls]
            den = es[0] + es[1] + es[2]
            acc = (es[0] * o_s[0, h] + es[1] * o_s[1, h] + es[2] * o_s[2, h]) / den
            dil_s[:, h * HEAD_DIM:(h + 1) * HEAD_DIM] = acc.astype(BF16)

    hb = hb_ref[...]
    ya = jnp.dot(na_s[...], wpa_ref[...], preferred_element_type=F32)
    yb = jnp.dot(dil_s[...], wpb_ref[...], preferred_element_type=F32)
    ga = jnp.dot(hb, wga_ref[...], preferred_element_type=F32) + bga_ref[...]
    gb = jnp.dot(hb, wgb_ref[...], preferred_element_type=F32) + bgb_ref[...]
    m_ref[...] = (jax.nn.sigmoid(ga) * ya + jax.nn.sigmoid(gb) * yb).astype(BF16)


def _mix(na, dil_o, dil_lse, hb, wpa, wpb, w_all, b_all, s):
    n, d = hb.shape
    tm, tn = MIX_TM, MIX_TN
    tiles = s // tm
    gh = DIL_HEADS_PER_GROUP
    ngroups = len(DIL_GROUPS)
    assert all(tm % (16 * dil) == 0 for _, dil in DIL_GROUPS) and s % tm == 0
    col = lambda k: pl.BlockSpec((k, tn), lambda i, j: (0, j))
    ga0, gb0 = QKV_WIDTH // tn, (QKV_WIDTH + d) // tn
    gate = lambda k, first: pl.BlockSpec((k, tn), lambda i, j: (0, first + j))
    grp = lambda dil: pl.BlockSpec((gh, 1, dil, tm // dil, HEAD_DIM),
                                   lambda i, j: (0, i // tiles, 0, i % tiles, 0))
    groups = [grp(dil) for _, dil in DIL_GROUPS]
    return pl.pallas_call(
        _mix_kernel,
        grid=(n // tm, d // tn),
        in_specs=[pl.BlockSpec((NA_HEADS, tm, HEAD_DIM), lambda i, j: (0, i, 0)), *groups, *groups,
                  pl.BlockSpec((tm, d), lambda i, j: (i, 0)),
                  col(NA_WIDTH), col(DIL_OUT_WIDTH),
                  gate(d, ga0), gate(d, gb0), gate(1, ga0), gate(1, gb0)],
        out_specs=pl.BlockSpec((tm, tn), lambda i, j: (i, j)),
        out_shape=jax.ShapeDtypeStruct((n, d), BF16),
        scratch_shapes=[pltpu.VMEM((tm, NA_WIDTH), BF16), pltpu.VMEM((tm, DIL_OUT_WIDTH), BF16),
                        pltpu.VMEM((ngroups, gh, tm, HEAD_DIM), F32),
                        pltpu.VMEM((ngroups, gh, tm, HEAD_DIM), F32)],
        compiler_params=_params(("parallel", "arbitrary")),
        name="mix",
    )(na, *dil_o, *dil_lse, hb, wpa, wpb, w_all, w_all, b_all, b_all)


def _outproj_kernel(m_ref, x_ref, g0_ref, b0_ref, wo_ref, bo_ref, g_ref, b_ref, wrh_ref, wrl_ref, br_ref,
                    h1_ref, h1p_hbm, lg_ref, hbuf, sem):
    i = pl.program_id(0)
    slot = i % 2

    def out_copies(step, sl):
        return _tile_copies(h1p_hbm, step * OUT_TM, OUT_TM, hbuf.at[sl], sem.at[sl], True)

    mix = jnp.dot(m_ref[...], wo_ref[...], preferred_element_type=F32) + bo_ref[...]
    h = _layer_norm(x_ref[...], g0_ref[...], b0_ref[...])
    h1 = _layer_norm(DN_ALPHA * h + mix, g_ref[...], b_ref[...])
    h1_ref[...] = h1

    @pl.when(i >= 2)
    def _():
        for cp in out_copies(i - 2, slot):
            cp.wait()

    _pack_rows(hbuf.at[slot], h1)
    for cp in out_copies(i, slot):
        cp.start()

    @pl.when(i == pl.num_programs(0) - 1)
    def _():
        for cp in out_copies(i, slot):
            cp.wait()

        @pl.when(i >= 1)
        def _():
            for cp in out_copies(i - 1, 1 - slot):
                cp.wait()

    hi = h1.astype(BF16)
    lo = (h1 - hi.astype(F32)).astype(BF16)
    lg = (jnp.dot(hi, wrh_ref[...], preferred_element_type=F32)
          + jnp.dot(lo, wrh_ref[...], preferred_element_type=F32)
          + jnp.dot(hi, wrl_ref[...], preferred_element_type=F32))
    lg_ref[...] = lg + br_ref[...]


def _outproj(m, x, g0, b0, wo, bo, g, b, wr_hi, wr_lo, br):
    n, d = x.shape
    tm = OUT_TM
    row = lambda w: pl.BlockSpec((tm, w), lambda i: (i, 0))
    full = lambda r, c: pl.BlockSpec((r, c), lambda i: (0, 0), pipeline_mode=pl.Buffered(1))
    return pl.pallas_call(
        _outproj_kernel,
        grid=(n // tm,),
        in_specs=[row(d), row(d), full(1, d), full(1, d), full(d, d), full(1, d), full(1, d), full(1, d),
                  full(d, ROUTER_PAD), full(d, ROUTER_PAD), full(1, ROUTER_PAD)],
        out_specs=[row(d), pl.BlockSpec(memory_space=pl.ANY), row(ROUTER_PAD)],
        out_shape=[jax.ShapeDtypeStruct((n, d), F32),
                   jax.ShapeDtypeStruct((n, PACKED, LANES), jnp.uint32),
                   jax.ShapeDtypeStruct((n, ROUTER_PAD), F32)],
        scratch_shapes=[pltpu.VMEM((2, PACKED, tm, LANES), jnp.uint32), pltpu.SemaphoreType.DMA((2,))],
        compiler_params=_params(("arbitrary",)),
        name="outproj",
    )(m, x, g0.reshape(1, d), b0.reshape(1, d), wo, bo.reshape(1, d), g.reshape(1, d), b.reshape(1, d),
      wr_hi, wr_lo, br)


def _first_argmax(vals, lane_f):
    top = jnp.max(vals, axis=-1, keepdims=True)
    idx = jnp.min(jnp.where(vals == top, lane_f, float(LANES)), axis=-1, keepdims=True)
    return top, idx


def _route_kernel(lg_ref, gate_ref, dest_ref, cnt_ref, base_s, start_s):
    phase = pl.program_id(0)

    @pl.when(pl.program_id(1) == 0)
    def _():
        @pl.when(phase == 1)
        def _():
            counts = base_s[...]
            cnt_ref[...] = counts
            blocks = jnp.ceil(counts * (1.0 / MOE_TB))
            before = (lax.broadcasted_iota(jnp.int32, (ROUTER_PAD, ROUTER_PAD), 0)
                      < lax.broadcasted_iota(jnp.int32, (ROUTER_PAD, ROUTER_PAD), 1)).astype(BF16)
            starts = jnp.dot(jnp.broadcast_to(blocks, (8, ROUTER_PAD)).astype(BF16), before,
                             preferred_element_type=F32)
            start_s[...] = starts[0:1] * float(MOE_TB)

        base_s[...] = jnp.zeros(base_s.shape, F32)

    lg = lg_ref[...]
    lane = lax.broadcasted_iota(jnp.int32, lg.shape, 1)
    lane_f = lane.astype(F32)
    g_mask = lane < N_GROUPS
    g_top, g_sel = _first_argmax(jnp.where(g_mask, lg, MASK_VALUE), lane_f)
    g_prob = 1.0 / jnp.sum(jnp.where(g_mask, jnp.exp(lg - g_top), 0.0), axis=-1, keepdims=True)
    first = N_GROUPS + g_sel * EXPERTS_PER_GROUP
    e_mask = (lane_f >= first) & (lane_f < first + EXPERTS_PER_GROUP)
    el = jnp.where(e_mask, lg, MASK_VALUE)
    v0, i0 = _first_argmax(el, lane_f)
    v1, i1 = _first_argmax(jnp.where(lane_f == i0, MASK_VALUE, el), lane_f)
    e1 = jnp.exp(v1 - v0)
    w0 = g_prob / (1.0 + e1)
    w1 = g_prob * e1 / (1.0 + e1)

    onehots = [lane_f == idx - N_GROUPS for idx in (i0, i1)]

    @pl.when(phase == 0)
    def _():
        base_s[...] = base_s[...] + sum(jnp.sum(oh.astype(F32), axis=0, keepdims=True) for oh in onehots)

    @pl.when(phase == 1)
    def _():
        tm = lg.shape[0]
        tri = (lax.broadcasted_iota(jnp.int32, (tm, tm), 1)
               < lax.broadcasted_iota(jnp.int32, (tm, tm), 0)).astype(BF16)
        dests = []
        for onehot in onehots:
            before = jnp.dot(tri, onehot.astype(BF16), preferred_element_type=F32) + base_s[...]
            dests.append(jnp.sum(jnp.where(onehot, before + start_s[...], 0.0), axis=-1, keepdims=True))
            base_s[...] = base_s[...] + jnp.sum(onehot.astype(F32), axis=0, keepdims=True)
        gate_ref[...] = jnp.where(lane == 0, w0, jnp.where(lane == 1, w1, 0.0))
        cols = jnp.where(lane == 0, dests[0], jnp.where(lane == 1, dests[1], 0.0))
        dest_ref[...] = jnp.transpose(cols)[0:8, :].astype(jnp.int32)


def _invert_kernel(dest_ref, lo_ref, hi_ref, tok_ref, *, n):
    def clear_range(e, c):
        def clear(s, c2):
            tok_ref[s] = 0
            return c2

        lax.fori_loop(lo_ref[e], hi_ref[e], clear, 0)
        return c

    lax.fori_loop(0, lo_ref.shape[0], clear_range, 0)
    for k in range(TOP_K):
        def place(t, c, k=k):
            tok_ref[dest_ref[k * n + t]] = t
            return c

        lax.fori_loop(0, n, place, 0, unroll=8)


def _invert_slots(dest, unused_lo, unused_hi, n, slots):
    assert dest.shape[0] == TOP_K * n
    return pl.pallas_call(
        functools.partial(_invert_kernel, n=n),
        grid_spec=pltpu.PrefetchScalarGridSpec(
            num_scalar_prefetch=3, grid=(1,), in_specs=[],
            out_specs=pl.BlockSpec(memory_space=pltpu.SMEM)),
        out_shape=jax.ShapeDtypeStruct((slots,), jnp.int32),
        compiler_params=_params(("arbitrary",)),
        name="invert_slots",
    )(dest, unused_lo, unused_hi)


def _route(logits, n):
    tm = ROUTE_TM
    one = pl.BlockSpec((1, ROUTER_PAD), lambda p, i: (0, 0))
    gate, dest, counts = pl.pallas_call(
        _route_kernel,
        grid=(2, n // tm),
        in_specs=[pl.BlockSpec((tm, ROUTER_PAD), lambda p, i: (i, 0))],
        out_specs=[pl.BlockSpec((tm, ROUTER_PAD), lambda p, i: (i * p, 0)),
                   pl.BlockSpec((8, tm), lambda p, i: (0, i * p)), one],
        out_shape=[jax.ShapeDtypeStruct((n, ROUTER_PAD), F32),
                   jax.ShapeDtypeStruct((8, n), jnp.int32),
                   jax.ShapeDtypeStruct((1, ROUTER_PAD), F32)],
        scratch_shapes=[pltpu.VMEM((1, ROUTER_PAD), F32), pltpu.VMEM((1, ROUTER_PAD), F32)],
        compiler_params=_params(("arbitrary", "arbitrary")),
        name="route",
    )(logits)
    gate = gate[:, :TOP_K]
    dest = dest[:TOP_K].reshape(TOP_K * n)
    a = n * TOP_K
    counts = counts[0, :N_EXPERTS].astype(jnp.int32)
    pcounts = (counts + MOE_TB - 1) // MOE_TB * MOE_TB
    pends = jnp.cumsum(pcounts)
    nb = a // MOE_TB + N_EXPERTS
    slots = nb * MOE_TB
    unused_lo = jnp.concatenate([pends - pcounts + counts, pends[-1:]])
    unused_hi = jnp.concatenate([pends, jnp.full((1,), slots, jnp.int32)])
    slot_tok = _invert_slots(dest, unused_lo, unused_hi, n, slots)
    block_start = jnp.arange(nb, dtype=jnp.int32) * MOE_TB
    block_e = jnp.minimum(jnp.sum((pends[None, :] <= block_start[:, None]).astype(jnp.int32), axis=1),
                          N_EXPERTS - 1)
    n_active = (pends[-1] // MOE_TB).astype(jnp.int32).reshape(1)
    ids = jnp.arange(N_EXPERTS, dtype=jnp.int32)
    later = (counts[None, :] > 0) & (ids[None, :] > ids[:, None])
    nxt = jnp.min(jnp.where(later, ids[None, :], N_EXPERTS), axis=1)
    nxt = jnp.where(nxt == N_EXPERTS, -1, nxt)
    block_next = jnp.sum(jnp.where(block_e[:, None] == ids[None, :], nxt[None, :], 0), axis=1).astype(jnp.int32)
    return dest, gate, slot_tok, block_e, block_next, n_active


def _cast_weights(src, dst):
    rows, cols = src.shape
    step = CAST_VREGS * 8 * LANES // cols

    def body(r, c):
        sl = pl.ds(pl.multiple_of(r * step, step), step)
        dst[sl, :] = src[sl, :].astype(BF16)
        return c

    lax.fori_loop(0, rows // step, body, 0, unroll=2)


def _expert_kernel(be_ref, nx_ref, nact_ref, tok_ref, h1p_hbm, wg_hbm, wu_hbm, wd_hbm, ys_hbm,
                   xbuf, ybuf, wg_f, wu_f, wd_f, wg_s, wu_s, wd_s, gsem, osem, wsem):
    i = pl.program_id(0)
    slot = i % 2
    xslot = i % GATHER_SLOTS
    nact = nact_ref[0]

    def gather(step, inline=False):
        sl = step % GATHER_SLOTS
        _start_row_gather(tok_ref, step * MOE_TB, 1, MOE_TB, h1p_hbm, xbuf.at[sl], gsem.at[sl], False,
                          inline)

    def out_copies(step, sl):
        return _tile_copies(ys_hbm, step * MOE_TB, MOE_TB, ybuf.at[sl], osem.at[sl], True)

    def weight_copies(e):
        return [pltpu.make_async_copy(src.at[e], dst, wsem)
                for src, dst in ((wg_hbm, wg_f), (wu_hbm, wu_f), (wd_hbm, wd_f))]

    @pl.when((i == 0) & (nact > 0))
    def _():
        for cp in weight_copies(be_ref[0]):
            cp.start(priority=WEIGHT_DMA_PRIORITY)
        gather(0)

    @pl.when((i == 0) & (nact > 1))
    def _():
        gather(1)

    @pl.when(i >= 2)
    def _():
        for cp in out_copies(i - 2, slot):
            cp.wait()

    @pl.when(i < nact)
    def _():
        expert = be_ref[i]

        @pl.when((i == 0) | (expert != be_ref[jnp.maximum(i - 1, 0)]))
        def _():
            for cp in weight_copies(expert):
                cp.wait()
            _cast_weights(wg_f, wg_s)
            _cast_weights(wu_f, wu_s)
            _cast_weights(wd_f, wd_s)

            @pl.when(nx_ref[i] >= 0)
            def _():
                for cp in weight_copies(nx_ref[i]):
                    cp.start(priority=WEIGHT_DMA_PRIORITY)

        _wait_row_gather(xbuf.at[xslot], gsem.at[xslot])

        def block(prefetch):
            if prefetch:
                gather(i + 2, inline=True)
            xb = _unpack_rows(xbuf.at[xslot]).astype(BF16)
            gate = jnp.dot(xb, wg_s[...], preferred_element_type=F32)
            up = jnp.dot(xb, wu_s[...], preferred_element_type=F32)
            hid = (jax.nn.silu(gate) * up).astype(BF16)
            _pack_rows(ybuf.at[slot], jnp.dot(hid, wd_s[...], preferred_element_type=F32))

        pl.when(i + 2 < nact)(functools.partial(block, True))
        pl.when(i + 2 >= nact)(functools.partial(block, False))

    @pl.when(i >= nact)
    def _():
        ybuf[slot] = jnp.zeros(ybuf.shape[1:], jnp.uint32)

    for cp in out_copies(i, slot):
        cp.start()

    @pl.when(i == pl.num_programs(0) - 1)
    def _():
        for cp in out_copies(i, slot):
            cp.wait()

        @pl.when(i >= 1)
        def _():
            for cp in out_copies(i - 1, 1 - slot):
                cp.wait()


def _experts(h1p, slot_tok, block_e, block_next, n_active, w_gate, w_up, w_down):
    d = D_MODEL
    slots = slot_tok.shape[0]
    nb = slots // MOE_TB
    any_spec = pl.BlockSpec(memory_space=pl.ANY)
    grid_spec = pltpu.PrefetchScalarGridSpec(
        num_scalar_prefetch=4,
        grid=(nb,),
        in_specs=[any_spec, any_spec, any_spec, any_spec],
        out_specs=any_spec,
        scratch_shapes=[pltpu.VMEM((GATHER_SLOTS, PACKED, MOE_TB, LANES), jnp.uint32),
                        pltpu.VMEM((2, PACKED, MOE_TB, LANES), jnp.uint32),
                        pltpu.VMEM((d, D_EXPERT), F32), pltpu.VMEM((d, D_EXPERT), F32),
                        pltpu.VMEM((D_EXPERT, d), F32),
                        pltpu.VMEM((d, D_EXPERT), BF16), pltpu.VMEM((d, D_EXPERT), BF16),
                        pltpu.VMEM((D_EXPERT, d), BF16),
                        pltpu.SemaphoreType.DMA((GATHER_SLOTS,)), pltpu.SemaphoreType.DMA((2,)),
                        pltpu.SemaphoreType.DMA(())])
    return pl.pallas_call(
        _expert_kernel,
        grid_spec=grid_spec,
        out_shape=jax.ShapeDtypeStruct((slots, PACKED, LANES), jnp.uint32),
        compiler_params=_params(("arbitrary",)),
        name="experts",
    )(block_e, block_next, n_active, slot_tok, h1p, w_gate, w_up, w_down)


def _final_kernel(dest_ref, ys_hbm, h1_ref, gate_ref, g_ref, b_ref, o_ref, ybuf, gsem):
    i = pl.program_id(0)
    steps = pl.num_programs(0)
    slot = i % GATHER_SLOTS

    def fetch(step, inline=False):
        sl = step % GATHER_SLOTS
        for k in range(TOP_K):
            _start_row_gather(dest_ref, k * steps * FIN_TM + step * FIN_TM, 1, FIN_TM, ys_hbm,
                              ybuf.at[sl, k], gsem.at[sl], True, inline)

    @pl.when(i == 0)
    def _():
        fetch(0)

    @pl.when((i == 0) & (steps > 1))
    def _():
        fetch(1)

    _wait_row_gather(ybuf.at[slot], gsem.at[slot])

    def combine(prefetch):
        if prefetch:
            fetch(i + 2, inline=True)
        gate = gate_ref[...]
        ffn = (_unpack_rows(ybuf.at[slot, 0]) * gate[:, 0:1]
               + _unpack_rows(ybuf.at[slot, 1]) * gate[:, 1:2])
        o_ref[...] = _layer_norm(DN_ALPHA * h1_ref[...] + ffn, g_ref[...], b_ref[...])

    pl.when(i + 2 < steps)(functools.partial(combine, True))
    pl.when(i + 2 >= steps)(functools.partial(combine, False))


def _final(dest, ys, h1, gate, g, b):
    n, d = h1.shape
    grid_spec = pltpu.PrefetchScalarGridSpec(
        num_scalar_prefetch=1,
        grid=(n // FIN_TM,),
        in_specs=[pl.BlockSpec(memory_space=pl.ANY),
                  pl.BlockSpec((FIN_TM, d), lambda i, ds: (i, 0)),
                  pl.BlockSpec((FIN_TM, TOP_K), lambda i, ds: (i, 0)),
                  pl.BlockSpec((1, d), lambda i, ds: (0, 0)),
                  pl.BlockSpec((1, d), lambda i, ds: (0, 0))],
        out_specs=pl.BlockSpec((FIN_TM, d), lambda i, ds: (i, 0)),
        scratch_shapes=[pltpu.VMEM((GATHER_SLOTS, TOP_K, PACKED, FIN_TM, LANES), jnp.uint32),
                        pltpu.SemaphoreType.DMA((GATHER_SLOTS,))])
    return pl.pallas_call(
        _final_kernel,
        grid_spec=grid_spec,
        out_shape=jax.ShapeDtypeStruct((n, d), F32),
        compiler_params=_params(("arbitrary",)),
        name="final",
    )(dest, ys, h1, gate, g.reshape(1, d), b.reshape(1, d))


def kernel(x, ln0_g, ln0_b, w_in, b_in, rpb, w_proj_a, w_proj_b, w_o, b_o, ln1_g, ln1_b,
           w_router_group, b_router_group, w_router_expert, b_router_expert,
           w_gate, w_up, w_down, ln2_g, ln2_b):
    bsz, s, d = x.shape
    n = bsz * s
    assert d == D_MODEL and w_in.shape[0] == DEPTH
    scale = HEAD_DIM ** -0.5

    col_scale = np.ones((w_in.shape[2],), np.float32)
    col_scale[:NA_WIDTH] = scale
    col_scale[3 * NA_WIDTH:3 * NA_WIDTH + DIL_WIDTH] = scale
    w_all = (w_in[0] * col_scale).astype(BF16)
    b_all = (b_in[0] * col_scale).reshape(1, -1)
    assert DIL_OUT_WIDTH == INPROJ_TN
    na_blocks, ngroups = 3 * NA_WIDTH // INPROJ_TN, len(DIL_GROUPS)
    w_r = jnp.concatenate([w_router_group[0], w_router_expert[0]], axis=1)
    w_r = jnp.pad(w_r, ((0, 0), (0, ROUTER_PAD - w_r.shape[1])))
    w_r_hi = w_r.astype(BF16)
    w_r_lo = (w_r - w_r_hi.astype(F32)).astype(BF16)
    b_r = jnp.pad(jnp.concatenate([b_router_group[0], b_router_expert[0]]),
                  (0, ROUTER_PAD - N_GROUPS - N_EXPERTS)).reshape(1, ROUTER_PAD)
    slopes = _alibi_slopes(DIL_HEADS)

    x2 = x.reshape(n, d)
    hb = _ln0(x2, ln0_g, ln0_b)
    zq = [_inproj(hb, w_all, b_all, bsz, s, DIL_GROUPS[0][1], 3 * NA_WIDTH + 3 * DIL_OUT_WIDTH,
                  lambda j: jnp.where(j < na_blocks, j, na_blocks + (j - na_blocks) * ngroups))]
    for g in range(1, ngroups):
        zq.append(_inproj(hb, w_all, b_all, bsz, s, DIL_GROUPS[g][1], 3 * DIL_OUT_WIDTH,
                          lambda j, g=g: na_blocks + j * ngroups + g))
    na = _neighborhood_attention(zq[0].reshape(-1, bsz, s, HEAD_DIM), rpb[0])
    na = na.reshape(NA_HEADS, n, HEAD_DIM)
    dil = [_dilated_group(zq[g], DIL0_H if g == 0 else 0, g, slopes) for g in range(ngroups)]
    m = _mix(na, [o for o, _ in dil], [l for _, l in dil], hb,
             w_proj_a[0].astype(BF16), w_proj_b[0].astype(BF16), w_all, b_all, s)
    h1, h1p, logits = _outproj(m, x2, ln0_g, ln0_b, w_o[0].astype(BF16), b_o[0], ln1_g[0], ln1_b[0],
                               w_r_hi, w_r_lo, b_r)
    dest, gate, slot_tok, block_e, block_next, n_active = _route(logits, n)
    ys = _experts(h1p, slot_tok, block_e, block_next, n_active, w_gate[0], w_up[0], w_down[0])
    out = _final(dest, ys, h1, gate, ln2_g[0], ln2_b[0])
    return out.reshape(bsz, s, d)
```

```python
import functools

import numpy as np
import jax
import jax.numpy as jnp
from jax import lax
from jax.experimental import pallas as pl
from jax.experimental.pallas import tpu as pltpu

F32 = jnp.float32
BF16 = jnp.bfloat16

D_MODEL = 2048
HEAD_DIM = 128
GRID_W = 64
NA_HEADS = 8
NA_KH = 8
NA_KW = 16
DIL_GROUPS = ((128, 1), (512, 4), (2048, 16))
DIL_HEADS_PER_GROUP = 4
DIL_HEADS = DIL_HEADS_PER_GROUP * len(DIL_GROUPS)
N_GROUPS = 8
EXPERTS_PER_GROUP = 8
N_EXPERTS = N_GROUPS * EXPERTS_PER_GROUP
TOP_K = 2
D_EXPERT = D_MODEL // 4
LN_EPS = 1e-5
DEPTH = 1
DN_ALPHA = (2 * DEPTH) ** 0.25
NA_WIDTH = NA_HEADS * HEAD_DIM
DIL_WIDTH = DIL_HEADS * HEAD_DIM
DIL_OUT_WIDTH = DIL_HEADS_PER_GROUP * HEAD_DIM
QKV_WIDTH = 3 * NA_WIDTH + 3 * DIL_WIDTH
QKV_HEADS = QKV_WIDTH // HEAD_DIM
LANES = 128
CHUNKS = D_MODEL // LANES
PACKED = CHUNKS // 2
MASK_VALUE = -1e30

QA_H, KA_H, VA_H = 0, NA_HEADS, 2 * NA_HEADS
DIL0_H = 3 * NA_HEADS

LN_TM = 512
INPROJ_TM, INPROJ_TN = 2048, 512
NA_QROWS = 4
NA_KROWS = 12
NA_NSUB = 8
DIL_NSUB = 8
DIL_TQ = 256
DIL_HALF = 64
DIL_TK = DIL_TQ + 2 * DIL_HALF
MIX_TM, MIX_TN = 512, 512
OUT_TM = 512
ROUTER_PAD = 128
ROUTE_TM = 512
MOE_TB = 128
FIN_TM = 256
CAST_VREGS = 32
GATHER_SLOTS = 3
WEIGHT_DMA_PRIORITY = 1
VMEM_LIMIT = 56 * 1024 * 1024


def _params(sem, limit=VMEM_LIMIT):
    return pltpu.CompilerParams(dimension_semantics=sem, vmem_limit_bytes=limit)


def _layer_norm(x, g, b):
    mu = jnp.mean(x, axis=-1, keepdims=True)
    xc = x - mu
    var = jnp.mean(xc * xc, axis=-1, keepdims=True)
    return xc * lax.rsqrt(var + LN_EPS) * g + b


HIGH_HALF = np.uint32(0xFFFF0000)


def _bf16_bits(x):
    return lax.bitcast_convert_type(x.astype(BF16).astype(F32), jnp.uint32)


def _pack_rows(ref, val):
    for c in range(PACKED):
        lo = _bf16_bits(val[:, c * LANES:(c + 1) * LANES])
        hi = _bf16_bits(val[:, (c + PACKED) * LANES:(c + PACKED + 1) * LANES])
        ref[c] = (lo >> 16) | (hi & HIGH_HALF)


def _unpack_rows(ref):
    words = [ref[c] for c in range(PACKED)]
    lo = [lax.bitcast_convert_type(w << 16, F32) for w in words]
    hi = [lax.bitcast_convert_type(w & HIGH_HALF, F32) for w in words]
    return jnp.concatenate(lo + hi, axis=-1)


def _tile_copies(hbm, row0, rows, vmem, sem, to_hbm):
    copies = []
    for c in range(PACKED):
        h, v = hbm.at[pl.ds(row0, rows), c], vmem.at[c]
        copies.append(pltpu.make_async_copy(v, h, sem) if to_hbm else pltpu.make_async_copy(h, v, sem))
    return copies


def _start_row_gather(idx_ref, base, step, count, src_hbm, dst, sem, both_threads, inline=False):
    def start(j, c):
        for half in range(2):
            t = 2 * j + half
            pltpu.make_async_copy(src_hbm.at[idx_ref[base + t * step]], dst.at[:, t], sem).start(
                priority=half if both_threads else 0)
        return c

    if inline:
        for j in range(count // 2):
            start(j, 0)
    else:
        lax.fori_loop(0, count // 2, start, 0, unroll=2)


def _wait_row_gather(dst, sem):
    pltpu.make_async_copy(dst, dst, sem).wait()


def _ln0_kernel(x_ref, g_ref, b_ref, hb_ref):
    hb_ref[...] = _layer_norm(x_ref[...], g_ref[...], b_ref[...]).astype(BF16)


def _ln0(x, g, b):
    n, d = x.shape
    row = pl.BlockSpec((LN_TM, d), lambda i: (i, 0))
    vec = pl.BlockSpec((1, d), lambda i: (0, 0))
    return pl.pallas_call(
        _ln0_kernel,
        grid=(n // LN_TM,),
        in_specs=[row, vec, vec],
        out_specs=row,
        out_shape=jax.ShapeDtypeStruct((n, d), BF16),
        compiler_params=_params(("parallel",)),
        name="ln0",
    )(x, g.reshape(1, d), b.reshape(1, d))


def _inproj_kernel(hb_ref, w_ref, b_ref, o_ref, acc_s, *, dil):
    acc = jnp.dot(hb_ref[...], w_ref[...], preferred_element_type=F32) + b_ref[...]
    if dil == 1:
        for c in range(INPROJ_TN // LANES):
            o_ref[c, 0, 0] = acc[:, c * LANES:(c + 1) * LANES].astype(BF16)
    else:
        for c in range(INPROJ_TN // LANES):
            acc_s[c] = acc[:, c * LANES:(c + 1) * LANES]
        for c in range(INPROJ_TN // LANES):
            for r in range(dil):
                o_ref[c, 0, r] = acc_s[c, pl.ds(r, INPROJ_TM // dil, stride=dil), :].astype(BF16)


def _inproj(hb, w, b, bsz, s, dil, width, col_block):
    n, d = hb.shape
    tiles = s // INPROJ_TM
    assert INPROJ_TM % (16 * dil) == 0 and s % INPROJ_TM == 0 and width % INPROJ_TN == 0
    return pl.pallas_call(
        functools.partial(_inproj_kernel, dil=dil),
        grid=(n // INPROJ_TM, width // INPROJ_TN),
        in_specs=[pl.BlockSpec((INPROJ_TM, d), lambda i, j: (i, 0)),
                  pl.BlockSpec((d, INPROJ_TN), lambda i, j: (0, col_block(j))),
                  pl.BlockSpec((1, INPROJ_TN), lambda i, j: (0, col_block(j)))],
        out_specs=pl.BlockSpec((INPROJ_TN // LANES, 1, dil, INPROJ_TM // dil, LANES),
                               lambda i, j: (j, i // tiles, 0, i % tiles, 0)),
        out_shape=jax.ShapeDtypeStruct((width // LANES, bsz, dil, s // dil, LANES), BF16),
        scratch_shapes=[pltpu.VMEM((INPROJ_TN // LANES, INPROJ_TM, LANES), F32)],
        compiler_params=_params(("parallel", "arbitrary")),
        name=f"inproj_{dil}",
    )(hb, w, b)


def _na_bias_tiles(rpb, rows):
    heads, n_dr, n_dc = rpb.shape
    qc = np.arange(GRID_W)[:, None]
    kc = np.arange(GRID_W)[None, :]
    qcs = np.clip(qc - NA_KW // 2, 0, GRID_W - NA_KW)
    v_col = (kc >= qcs) & (kc < qcs + NA_KW)
    dc = np.clip(kc - qc + NA_KW - 1, 0, n_dc - 1)
    onehot = (dc[None] == np.arange(n_dc)[:, None, None]) & v_col[None]
    toep = jnp.einsum('hrd,dqk->hrqk', rpb.astype(F32), jnp.asarray(onehot, F32),
                      precision=lax.Precision.HIGHEST)
    toep = jnp.where(jnp.asarray(v_col), toep, MASK_VALUE)
    masked = jnp.full((heads, 1, GRID_W, GRID_W), MASK_VALUE, F32)
    blocks = jnp.concatenate([toep, masked], axis=1)
    i = np.arange(NA_QROWS)[:, None]
    j = np.arange(NA_KROWS)[None, :]
    sel = []
    for r0, ks in ((0, 0), (2 * NA_QROWS, 2 * NA_QROWS - NA_KH // 2), (rows - NA_QROWS, rows - NA_KROWS)):
        r, krow = r0 + i, ks + j
        start = np.clip(r - NA_KH // 2, 0, rows - NA_KH)
        v_row = (krow >= start) & (krow < start + NA_KH)
        sel.append(np.where(v_row, krow - r + NA_KH - 1, n_dr))
    sel = np.stack(sel)
    pairs = sel.reshape(3, NA_QROWS, NA_KROWS // 2, 2)
    distinct = sorted({tuple(p) for p in pairs.reshape(-1, 2).tolist()})
    plan = [[[distinct.index(tuple(pairs[v, i, c])) for c in range(NA_KROWS // 2)]
             for i in range(NA_QROWS)] for v in range(3)]
    left = jnp.take(blocks, jnp.asarray([p[0] for p in distinct], jnp.int32), axis=1)
    right = jnp.take(blocks, jnp.asarray([p[1] for p in distinct], jnp.int32), axis=1)
    return jnp.concatenate([left, right], axis=-1), plan


def _attend_blocks(nblocks, scores, finish):
    scored, weighted = {}, {}
    for t in range(nblocks + 2):
        if t < nblocks:
            scored[t] = scores(t)
        if 0 <= t - 1 < nblocks:
            s, vw = scored.pop(t - 1)
            m = jnp.max(s, axis=-1, keepdims=True)
            p = jnp.exp(s - m)
            weighted[t - 1] = (p.astype(BF16), vw, m, jnp.sum(p, axis=-1, keepdims=True))
        if 0 <= t - 2 < nblocks:
            p, vw, m, l = weighted.pop(t - 2)
            finish(t - 2, jnp.dot(p, vw, preferred_element_type=F32) / l, m, l)


def _scores(q, kw, bias):
    return lax.dot_general(q, kw, (((1,), (1,)), ((), ())), preferred_element_type=F32) + bias


def _edge_variant(blk, nblk):
    return jnp.where(blk == 0, 0, jnp.where(blk == nblk - 1, 2, 1))


def _na_kernel(q_ref, k_ref, v_ref, tile_ref, o_ref, tab_ref, *, rows, plan):
    tq = NA_QROWS * GRID_W

    @pl.when(pl.program_id(2) == 0)
    def _():
        for v in range(3):
            for i in range(NA_QROWS):
                for c in range(NA_KROWS // 2):
                    tab_ref[v, 0, i * GRID_W:(i + 1) * GRID_W, c * 2 * GRID_W:(c + 1) * 2 * GRID_W] = (
                        tile_ref[0, plan[v][i][c]])

    def scores(u):
        blk = pl.program_id(2) * NA_NSUB + u
        ks = jnp.clip(blk * NA_QROWS - NA_KH // 2, 0, rows - NA_KROWS)
        off = pl.multiple_of(ks * GRID_W, GRID_W)
        kw = k_ref[0, 0, pl.ds(off, NA_KROWS * GRID_W), :]
        vw = v_ref[0, 0, pl.ds(off, NA_KROWS * GRID_W), :]
        bias = tab_ref[_edge_variant(blk, rows // NA_QROWS), 0]
        return _scores(q_ref[0, 0, u * tq:(u + 1) * tq, :], kw, bias), vw

    def finish(u, o, m, l):
        o_ref[0, 0, u * tq:(u + 1) * tq, :] = o.astype(BF16)

    _attend_blocks(NA_NSUB, scores, finish)


def _neighborhood_attention(zq4, rpb):
    _, bsz, s, _ = zq4.shape
    rows = s // GRID_W
    assert rows % (NA_QROWS * NA_NSUB) == 0 and rows >= NA_KROWS + NA_QROWS and NA_KROWS % 2 == 0
    tiles, plan = _na_bias_tiles(rpb, rows)
    tq = NA_QROWS * GRID_W * NA_NSUB
    return pl.pallas_call(
        functools.partial(_na_kernel, rows=rows, plan=plan),
        grid=(bsz, NA_HEADS, s // tq),
        in_specs=[pl.BlockSpec((1, 1, tq, HEAD_DIM), lambda b, h, i: (QA_H + h, b, i, 0)),
                  pl.BlockSpec((1, 1, s, HEAD_DIM), lambda b, h, i: (KA_H + h, b, 0, 0)),
                  pl.BlockSpec((1, 1, s, HEAD_DIM), lambda b, h, i: (VA_H + h, b, 0, 0)),
                  pl.BlockSpec((1,) + tiles.shape[1:], lambda b, h, i: (h, 0, 0, 0))],
        out_specs=pl.BlockSpec((1, 1, tq, HEAD_DIM), lambda b, h, i: (h, b, i, 0)),
        out_shape=jax.ShapeDtypeStruct((NA_HEADS, bsz, s, HEAD_DIM), BF16),
        scratch_shapes=[pltpu.VMEM((3, 1, NA_QROWS * GRID_W, NA_KROWS * GRID_W), F32)],
        compiler_params=_params(("arbitrary", "arbitrary", "arbitrary")),
        name="na_attn",
    )(zq4, zq4, zq4, tiles)


def _alibi_slopes(n):
    return np.array([2.0 ** (-8.0 * (i + 1) / n) for i in range(n)], dtype=np.float32)


def _dil_bias_tables(slopes, dil):
    qi = np.arange(DIL_TQ)[:, None]
    kj = np.arange(DIL_TK)[None, :]
    dist = np.stack([np.abs(kj - qi + shift) for shift in (0, -DIL_HALF, -2 * DIL_HALF)])
    dist = jnp.asarray(dist, F32)[:, None]
    penalty = jnp.asarray(slopes, F32)[None, :, None, None] * (dist * float(dil))
    return jnp.where(dist <= DIL_HALF, -penalty, MASK_VALUE)


def _dil_kernel(q_ref, k_ref, v_ref, tab_ref, o_ref, lse_ref, *, length, nsub, nres):
    def scores(u):
        res, sub = divmod(u, nsub)
        blk = pl.program_id(3) * nsub + sub
        ks = jnp.clip(blk * DIL_TQ - DIL_HALF, 0, length - DIL_TK)
        off = pl.multiple_of(ks, DIL_HALF)
        kw = k_ref[0, 0, res, pl.ds(off, DIL_TK), :]
        vw = v_ref[0, 0, res, pl.ds(off, DIL_TK), :]
        bias = tab_ref[_edge_variant(blk, length // DIL_TQ), 0]
        return _scores(q_ref[0, 0, res, sub * DIL_TQ:(sub + 1) * DIL_TQ, :], kw, bias), vw

    def finish(u, o, m, l):
        res, sub = divmod(u, nsub)
        rows = slice(sub * DIL_TQ, (sub + 1) * DIL_TQ)
        o_ref[0, 0, res, rows, :] = o.astype(BF16)
        lse_ref[0, 0, res, rows, :] = jnp.broadcast_to(m + jnp.log(l), (DIL_TQ, LANES))

    _attend_blocks(nres * nsub, scores, finish)


def _dilated_group(zq, base, g, slopes):
    window, dil = DIL_GROUPS[g]
    assert (window // 2) // dil == DIL_HALF
    _, bsz, _, length, _ = zq.shape
    nsub = min(DIL_NSUB, length // DIL_TQ)
    nres = min(dil, DIL_NSUB // nsub)
    assert zq.shape[2] == dil and length % (DIL_TQ * nsub) == 0 and length >= DIL_TK and dil % nres == 0
    gh = DIL_HEADS_PER_GROUP
    tabs = _dil_bias_tables(slopes[g * gh:(g + 1) * gh], dil)
    tq = DIL_TQ * nsub
    qspec = pl.BlockSpec((1, 1, nres, tq, HEAD_DIM), lambda b, h, r, n: (base + h, b, r, n, 0))
    kvspec = lambda off: pl.BlockSpec((1, 1, nres, length, HEAD_DIM),
                                      lambda b, h, r, n: (base + off + h, b, r, 0, 0))
    ospec = pl.BlockSpec((1, 1, nres, tq, HEAD_DIM), lambda b, h, r, n: (h, b, r, n, 0))
    return pl.pallas_call(
        functools.partial(_dil_kernel, length=length, nsub=nsub, nres=nres),
        grid=(bsz, gh, dil // nres, length // tq),
        in_specs=[qspec, kvspec(gh), kvspec(2 * gh),
                  pl.BlockSpec((3, 1, DIL_TQ, DIL_TK), lambda b, h, r, n: (0, h, 0, 0))],
        out_specs=[ospec, ospec],
        out_shape=[jax.ShapeDtypeStruct((gh, bsz, dil, length, HEAD_DIM), BF16),
                   jax.ShapeDtypeStruct((gh, bsz, dil, length, HEAD_DIM), F32)],
        compiler_params=_params(("parallel", "parallel", "parallel", "arbitrary")),
        name=f"dil_attn_{dil}",
    )(zq, zq, zq, tabs)


def _mix_kernel(na_ref, o1_ref, o2_ref, o3_ref, l1_ref, l2_ref, l3_ref, hb_ref,
                wpa_ref, wpb_ref, wga_ref, wgb_ref, bga_ref, bgb_ref, m_ref, na_s, dil_s, o_s, l_s):
    @pl.when(pl.program_id(1) == 0)
    def _():
        for h in range(NA_HEADS):
            na_s[:, h * HEAD_DIM:(h + 1) * HEAD_DIM] = na_ref[h]
        for g, (o_ref, l_ref) in enumerate(((o1_ref, l1_ref), (o2_ref, l2_ref), (o3_ref, l3_ref))):
            dil = DIL_GROUPS[g][1]
            for h in range(DIL_HEADS_PER_GROUP):
                for r in range(dil):
                    rows = slice(None) if dil == 1 else pl.ds(r, MIX_TM // dil, stride=dil)
                    o_s[g, h, rows, :] = o_ref[h, 0, r].astype(F32)
                    l_s[g, h, rows, :] = l_ref[h, 0, r]
        for h in range(DIL_HEADS_PER_GROUP):
            ls = [l_s[g, h] for g in range(len(DIL_GROUPS))]
            mx = jnp.maximum(jnp.maximum(ls[0], ls[1]), ls[2])
            es = [jnp.exp(l - mx) for l in ls]
            den = es[0] + es[1] + es[2]
            acc = (es[0] * o_s[0, h] + es[1] * o_s[1, h] + es[2] * o_s[2, h]) / den
            dil_s[:, h * HEAD_DIM:(h + 1) * HEAD_DIM] = acc.astype(BF16)

    hb = hb_ref[...]
    ya = jnp.dot(na_s[...], wpa_ref[...], preferred_element_type=F32)
    yb = jnp.dot(dil_s[...], wpb_ref[...], preferred_element_type=F32)
    ga = jnp.dot(hb, wga_ref[...], preferred_element_type=F32) + bga_ref[...]
    gb = jnp.dot(hb, wgb_ref[...], preferred_element_type=F32) + bgb_ref[...]
    m_ref[...] = (jax.nn.sigmoid(ga) * ya + jax.nn.sigmoid(gb) * yb).astype(BF16)


def _mix(na, dil_o, dil_lse, hb, wpa, wpb, w_all, b_all, s):
    n, d = hb.shape
    tm, tn = MIX_TM, MIX_TN
    tiles = s // tm
    gh = DIL_HEADS_PER_GROUP
    ngroups = len(DIL_GROUPS)
    assert all(tm % (16 * dil) == 0 for _, dil in DIL_GROUPS) and s % tm == 0
    col = lambda k: pl.BlockSpec((k, tn), lambda i, j: (0, j))
    ga0, gb0 = QKV_WIDTH // tn, (QKV_WIDTH + d) // tn
    gate = lambda k, first: pl.BlockSpec((k, tn), lambda i, j: (0, first + j))
    grp = lambda dil: pl.BlockSpec((gh, 1, dil, tm // dil, HEAD_DIM),
                                   lambda i, j: (0, i // tiles, 0, i % tiles, 0))
    groups = [grp(dil) for _, dil in DIL_GROUPS]
    return pl.pallas_call(
        _mix_kernel,
        grid=(n // tm, d // tn),
        in_specs=[pl.BlockSpec((NA_HEADS, tm, HEAD_DIM), lambda i, j: (0, i, 0)), *groups, *groups,
                  pl.BlockSpec((tm, d), lambda i, j: (i, 0)),
                  col(NA_WIDTH), col(DIL_OUT_WIDTH),
                  gate(d, ga0), gate(d, gb0), gate(1, ga0), gate(1, gb0)],
        out_specs=pl.BlockSpec((tm, tn), lambda i, j: (i, j)),
        out_shape=jax.ShapeDtypeStruct((n, d), BF16),
        scratch_shapes=[pltpu.VMEM((tm, NA_WIDTH), BF16), pltpu.VMEM((tm, DIL_OUT_WIDTH), BF16),
                        pltpu.VMEM((ngroups, gh, tm, HEAD_DIM), F32),
                        pltpu.VMEM((ngroups, gh, tm, HEAD_DIM), F32)],
        compiler_params=_params(("parallel", "arbitrary")),
        name="mix",
    )(na, *dil_o, *dil_lse, hb, wpa, wpb, w_all, w_all, b_all, b_all)


def _outproj_kernel(m_ref, x_ref, g0_ref, b0_ref, wo_ref, bo_ref, g_ref, b_ref, wr_ref, br_ref,
                    h1_ref, h1p_hbm, lg_ref, hbuf, sem):
    i = pl.program_id(0)
    slot = i % 2

    def out_copies(step, sl):
        return _tile_copies(h1p_hbm, step * OUT_TM, OUT_TM, hbuf.at[sl], sem.at[sl], True)

    mix = jnp.dot(m_ref[...], wo_ref[...], preferred_element_type=F32) + bo_ref[...]
    h = _layer_norm(x_ref[...], g0_ref[...], b0_ref[...])
    h1 = _layer_norm(DN_ALPHA * h + mix, g_ref[...], b_ref[...])
    h1_ref[...] = h1

    @pl.when(i >= 2)
    def _():
        for cp in out_copies(i - 2, slot):
            cp.wait()

    _pack_rows(hbuf.at[slot], h1)
    for cp in out_copies(i, slot):
        cp.start()

    @pl.when(i == pl.num_programs(0) - 1)
    def _():
        for cp in out_copies(i, slot):
            cp.wait()

        @pl.when(i >= 1)
        def _():
            for cp in out_copies(i - 1, 1 - slot):
                cp.wait()

    hi = h1.astype(BF16)
    lo = (h1 - hi.astype(F32)).astype(BF16)
    both = jnp.dot(hi, wr_ref[...], preferred_element_type=F32)
    lg = (both[:, :ROUTER_PAD] + both[:, ROUTER_PAD:]
          + jnp.dot(lo, wr_ref[:, :ROUTER_PAD], preferred_element_type=F32))
    lg_ref[...] = lg + br_ref[...]


def _outproj(m, x, g0, b0, wo, bo, g, b, wr, br):
    n, d = x.shape
    tm = OUT_TM
    row = lambda w: pl.BlockSpec((tm, w), lambda i: (i, 0))
    full = lambda r, c: pl.BlockSpec((r, c), lambda i: (0, 0), pipeline_mode=pl.Buffered(1))
    return pl.pallas_call(
        _outproj_kernel,
        grid=(n // tm,),
        in_specs=[row(d), row(d), full(1, d), full(1, d), full(d, d), full(1, d), full(1, d), full(1, d),
                  full(d, 2 * ROUTER_PAD), full(1, ROUTER_PAD)],
        out_specs=[row(d), pl.BlockSpec(memory_space=pl.ANY), row(ROUTER_PAD)],
        out_shape=[jax.ShapeDtypeStruct((n, d), F32),
                   jax.ShapeDtypeStruct((n, PACKED, LANES), jnp.uint32),
                   jax.ShapeDtypeStruct((n, ROUTER_PAD), F32)],
        scratch_shapes=[pltpu.VMEM((2, PACKED, tm, LANES), jnp.uint32), pltpu.SemaphoreType.DMA((2,))],
        compiler_params=_params(("arbitrary",)),
        name="outproj",
    )(m, x, g0.reshape(1, d), b0.reshape(1, d), wo, bo.reshape(1, d), g.reshape(1, d), b.reshape(1, d),
      wr, br)


def _first_argmax(vals, lane_f):
    top = jnp.max(vals, axis=-1, keepdims=True)
    idx = jnp.min(jnp.where(vals == top, lane_f, float(LANES)), axis=-1, keepdims=True)
    return top, idx


def _route_kernel(lg_ref, gate_ref, dest_ref, cnt_ref, base_s, start_s):
    phase = pl.program_id(0)

    @pl.when(pl.program_id(1) == 0)
    def _():
        @pl.when(phase == 1)
        def _():
            counts = base_s[...]
            cnt_ref[...] = counts
            blocks = jnp.ceil(counts * (1.0 / MOE_TB))
            before = (lax.broadcasted_iota(jnp.int32, (ROUTER_PAD, ROUTER_PAD), 0)
                      < lax.broadcasted_iota(jnp.int32, (ROUTER_PAD, ROUTER_PAD), 1)).astype(BF16)
            starts = jnp.dot(jnp.broadcast_to(blocks, (8, ROUTER_PAD)).astype(BF16), before,
                             preferred_element_type=F32)
            start_s[...] = starts[0:1] * float(MOE_TB)

        base_s[...] = jnp.zeros(base_s.shape, F32)

    lg = lg_ref[...]
    lane = lax.broadcasted_iota(jnp.int32, lg.shape, 1)
    lane_f = lane.astype(F32)
    g_mask = lane < N_GROUPS
    g_top, g_sel = _first_argmax(jnp.where(g_mask, lg, MASK_VALUE), lane_f)
    g_prob = 1.0 / jnp.sum(jnp.where(g_mask, jnp.exp(lg - g_top), 0.0), axis=-1, keepdims=True)
    first = N_GROUPS + g_sel * EXPERTS_PER_GROUP
    e_mask = (lane_f >= first) & (lane_f < first + EXPERTS_PER_GROUP)
    el = jnp.where(e_mask, lg, MASK_VALUE)
    v0, i0 = _first_argmax(el, lane_f)
    v1, i1 = _first_argmax(jnp.where(lane_f == i0, MASK_VALUE, el), lane_f)
    e1 = jnp.exp(v1 - v0)
    w0 = g_prob / (1.0 + e1)
    w1 = g_prob * e1 / (1.0 + e1)

    onehots = [lane_f == idx - N_GROUPS for idx in (i0, i1)]

    @pl.when(phase == 0)
    def _():
        base_s[...] = base_s[...] + sum(jnp.sum(oh.astype(F32), axis=0, keepdims=True) for oh in onehots)

    @pl.when(phase == 1)
    def _():
        tm = lg.shape[0]
        tri = (lax.broadcasted_iota(jnp.int32, (tm, tm), 1)
               < lax.broadcasted_iota(jnp.int32, (tm, tm), 0)).astype(BF16)
        dests = []
        for onehot in onehots:
            before = jnp.dot(tri, onehot.astype(BF16), preferred_element_type=F32) + base_s[...]
            dests.append(jnp.sum(jnp.where(onehot, before + start_s[...], 0.0), axis=-1, keepdims=True))
            base_s[...] = base_s[...] + jnp.sum(onehot.astype(F32), axis=0, keepdims=True)
        gate_ref[...] = jnp.where(lane == 0, w0, jnp.where(lane == 1, w1, 0.0))
        cols = jnp.where(lane == 0, dests[0], jnp.where(lane == 1, dests[1], 0.0))
        dest_ref[...] = jnp.transpose(cols)[0:8, :].astype(jnp.int32)


def _invert_kernel(dest_ref, lo_ref, hi_ref, tok_ref, *, n):
    def clear_range(e, c):
        def clear(s, c2):
            tok_ref[s] = 0
            return c2

        lax.fori_loop(lo_ref[e], hi_ref[e], clear, 0)
        return c

    lax.fori_loop(0, lo_ref.shape[0], clear_range, 0)
    for k in range(TOP_K):
        def place(t, c, k=k):
            tok_ref[dest_ref[k * n + t]] = t
            return c

        lax.fori_loop(0, n, place, 0, unroll=8)


def _invert_slots(dest, unused_lo, unused_hi, n, slots):
    assert dest.shape[0] == TOP_K * n
    return pl.pallas_call(
        functools.partial(_invert_kernel, n=n),
        grid_spec=pltpu.PrefetchScalarGridSpec(
            num_scalar_prefetch=3, grid=(1,), in_specs=[],
            out_specs=pl.BlockSpec(memory_space=pltpu.SMEM)),
        out_shape=jax.ShapeDtypeStruct((slots,), jnp.int32),
        compiler_params=_params(("arbitrary",)),
        name="invert_slots",
    )(dest, unused_lo, unused_hi)


def _route(logits, n):
    tm = ROUTE_TM
    one = pl.BlockSpec((1, ROUTER_PAD), lambda p, i: (0, 0))
    gate, dest, counts = pl.pallas_call(
        _route_kernel,
        grid=(2, n // tm),
        in_specs=[pl.BlockSpec((tm, ROUTER_PAD), lambda p, i: (i, 0))],
        out_specs=[pl.BlockSpec((tm, ROUTER_PAD), lambda p, i: (i * p, 0)),
                   pl.BlockSpec((8, tm), lambda p, i: (0, i * p)), one],
        out_shape=[jax.ShapeDtypeStruct((n, ROUTER_PAD), F32),
                   jax.ShapeDtypeStruct((8, n), jnp.int32),
                   jax.ShapeDtypeStruct((1, ROUTER_PAD), F32)],
        scratch_shapes=[pltpu.VMEM((1, ROUTER_PAD), F32), pltpu.VMEM((1, ROUTER_PAD), F32)],
        compiler_params=_params(("arbitrary", "arbitrary")),
        name="route",
    )(logits)
    gate = gate[:, :TOP_K]
    dest = dest[:TOP_K].reshape(TOP_K * n)
    a = n * TOP_K
    counts = counts[0, :N_EXPERTS].astype(jnp.int32)
    pcounts = (counts + MOE_TB - 1) // MOE_TB * MOE_TB
    pends = jnp.cumsum(pcounts)
    nb = a // MOE_TB + N_EXPERTS
    slots = nb * MOE_TB
    unused_lo = jnp.concatenate([pends - pcounts + counts, pends[-1:]])
    unused_hi = jnp.concatenate([pends, jnp.full((1,), slots, jnp.int32)])
    slot_tok = _invert_slots(dest, unused_lo, unused_hi, n, slots)
    block_start = jnp.arange(nb, dtype=jnp.int32) * MOE_TB
    block_e = jnp.minimum(jnp.sum((pends[None, :] <= block_start[:, None]).astype(jnp.int32), axis=1),
                          N_EXPERTS - 1)
    n_active = (pends[-1] // MOE_TB).astype(jnp.int32).reshape(1)
    ids = jnp.arange(N_EXPERTS, dtype=jnp.int32)
    later = (counts[None, :] > 0) & (ids[None, :] > ids[:, None])
    nxt = jnp.min(jnp.where(later, ids[None, :], N_EXPERTS), axis=1)
    nxt = jnp.where(nxt == N_EXPERTS, -1, nxt)
    block_next = jnp.sum(jnp.where(block_e[:, None] == ids[None, :], nxt[None, :], 0), axis=1).astype(jnp.int32)
    return dest, gate, slot_tok, block_e, block_next, n_active


def _cast_weights(src, dst):
    rows, cols = src.shape
    step = CAST_VREGS * 8 * LANES // cols

    def body(r, c):
        sl = pl.ds(pl.multiple_of(r * step, step), step)
        dst[sl, :] = src[sl, :].astype(BF16)
        return c

    lax.fori_loop(0, rows // step, body, 0, unroll=2)


def _expert_kernel(be_ref, nx_ref, nact_ref, tok_ref, h1p_hbm, wg_hbm, wu_hbm, wd_hbm, ys_hbm,
                   xbuf, ybuf, wg_f, wu_f, wd_f, wg_s, wu_s, wd_s, gsem, osem, wsem):
    i = pl.program_id(0)
    slot = i % 2
    xslot = i % GATHER_SLOTS
    nact = nact_ref[0]

    def gather(step, inline=False):
        sl = step % GATHER_SLOTS
        _start_row_gather(tok_ref, step * MOE_TB, 1, MOE_TB, h1p_hbm, xbuf.at[sl], gsem.at[sl], False,
                          inline)

    def out_copies(step, sl):
        return _tile_copies(ys_hbm, step * MOE_TB, MOE_TB, ybuf.at[sl], osem.at[sl], True)

    def weight_copies(e):
        return [pltpu.make_async_copy(src.at[e], dst, wsem)
                for src, dst in ((wg_hbm, wg_f), (wu_hbm, wu_f), (wd_hbm, wd_f))]

    @pl.when((i == 0) & (nact > 0))
    def _():
        for cp in weight_copies(be_ref[0]):
            cp.start(priority=WEIGHT_DMA_PRIORITY)
        gather(0)

    @pl.when((i == 0) & (nact > 1))
    def _():
        gather(1)

    @pl.when(i >= 2)
    def _():
        for cp in out_copies(i - 2, slot):
            cp.wait()

    @pl.when(i < nact)
    def _():
        expert = be_ref[i]

        @pl.when((i == 0) | (expert != be_ref[jnp.maximum(i - 1, 0)]))
        def _():
            for cp in weight_copies(expert):
                cp.wait()
            _cast_weights(wg_f, wg_s)
            _cast_weights(wu_f, wu_s)
            _cast_weights(wd_f, wd_s)

            @pl.when(nx_ref[i] >= 0)
            def _():
                for cp in weight_copies(nx_ref[i]):
                    cp.start(priority=WEIGHT_DMA_PRIORITY)

        _wait_row_gather(xbuf.at[xslot], gsem.at[xslot])

        def block(prefetch):
            if prefetch:
                gather(i + 2, inline=True)
            xb = _unpack_rows(xbuf.at[xslot]).astype(BF16)
            gate = jnp.dot(xb, wg_s[...], preferred_element_type=F32)
            up = jnp.dot(xb, wu_s[...], preferred_element_type=F32)
            hid = (jax.nn.silu(gate) * up).astype(BF16)
            _pack_rows(ybuf.at[slot], jnp.dot(hid, wd_s[...], preferred_element_type=F32))

        pl.when(i + 2 < nact)(functools.partial(block, True))
        pl.when(i + 2 >= nact)(functools.partial(block, False))

    @pl.when(i >= nact)
    def _():
        ybuf[slot] = jnp.zeros(ybuf.shape[1:], jnp.uint32)

    for cp in out_copies(i, slot):
        cp.start()

    @pl.when(i == pl.num_programs(0) - 1)
    def _():
        for cp in out_copies(i, slot):
            cp.wait()

        @pl.when(i >= 1)
        def _():
            for cp in out_copies(i - 1, 1 - slot):
                cp.wait()


def _experts(h1p, slot_tok, block_e, block_next, n_active, w_gate, w_up, w_down):
    d = D_MODEL
    slots = slot_tok.shape[0]
    nb = slots // MOE_TB
    any_spec = pl.BlockSpec(memory_space=pl.ANY)
    grid_spec = pltpu.PrefetchScalarGridSpec(
        num_scalar_prefetch=4,
        grid=(nb,),
        in_specs=[any_spec, any_spec, any_spec, any_spec],
        out_specs=any_spec,
        scratch_shapes=[pltpu.VMEM((GATHER_SLOTS, PACKED, MOE_TB, LANES), jnp.uint32),
                        pltpu.VMEM((2, PACKED, MOE_TB, LANES), jnp.uint32),
                        pltpu.VMEM((d, D_EXPERT), F32), pltpu.VMEM((d, D_EXPERT), F32),
                        pltpu.VMEM((D_EXPERT, d), F32),
                        pltpu.VMEM((d, D_EXPERT), BF16), pltpu.VMEM((d, D_EXPERT), BF16),
                        pltpu.VMEM((D_EXPERT, d), BF16),
                        pltpu.SemaphoreType.DMA((GATHER_SLOTS,)), pltpu.SemaphoreType.DMA((2,)),
                        pltpu.SemaphoreType.DMA(())])
    return pl.pallas_call(
        _expert_kernel,
        grid_spec=grid_spec,
        out_shape=jax.ShapeDtypeStruct((slots, PACKED, LANES), jnp.uint32),
        compiler_params=_params(("arbitrary",)),
        name="experts",
    )(block_e, block_next, n_active, slot_tok, h1p, w_gate, w_up, w_down)


def _final_kernel(dest_ref, ys_hbm, h1_ref, gate_ref, g_ref, b_ref, o_ref, ybuf, gsem):
    i = pl.program_id(0)
    steps = pl.num_programs(0)
    slot = i % GATHER_SLOTS

    def fetch(step, inline=False):
        sl = step % GATHER_SLOTS
        for k in range(TOP_K):
            _start_row_gather(dest_ref, k * steps * FIN_TM + step * FIN_TM, 1, FIN_TM, ys_hbm,
                              ybuf.at[sl, k], gsem.at[sl], True, inline)

    @pl.when(i == 0)
    def _():
        fetch(0)

    @pl.when((i == 0) & (steps > 1))
    def _():
        fetch(1)

    _wait_row_gather(ybuf.at[slot], gsem.at[slot])

    def combine(prefetch):
        if prefetch:
            fetch(i + 2, inline=True)
        gate = gate_ref[...]
        ffn = (_unpack_rows(ybuf.at[slot, 0]) * gate[:, 0:1]
               + _unpack_rows(ybuf.at[slot, 1]) * gate[:, 1:2])
        o_ref[...] = _layer_norm(DN_ALPHA * h1_ref[...] + ffn, g_ref[...], b_ref[...])

    pl.when(i + 2 < steps)(functools.partial(combine, True))
    pl.when(i + 2 >= steps)(functools.partial(combine, False))


def _final(dest, ys, h1, gate, g, b):
    n, d = h1.shape
    grid_spec = pltpu.PrefetchScalarGridSpec(
        num_scalar_prefetch=1,
        grid=(n // FIN_TM,),
        in_specs=[pl.BlockSpec(memory_space=pl.ANY),
                  pl.BlockSpec((FIN_TM, d), lambda i, ds: (i, 0)),
                  pl.BlockSpec((FIN_TM, TOP_K), lambda i, ds: (i, 0)),
                  pl.BlockSpec((1, d), lambda i, ds: (0, 0)),
                  pl.BlockSpec((1, d), lambda i, ds: (0, 0))],
        out_specs=pl.BlockSpec((FIN_TM, d), lambda i, ds: (i, 0)),
        scratch_shapes=[pltpu.VMEM((GATHER_SLOTS, TOP_K, PACKED, FIN_TM, LANES), jnp.uint32),
                        pltpu.SemaphoreType.DMA((GATHER_SLOTS,))])
    return pl.pallas_call(
        _final_kernel,
        grid_spec=grid_spec,
        out_shape=jax.ShapeDtypeStruct((n, d), F32),
        compiler_params=_params(("arbitrary",)),
        name="final",
    )(dest, ys, h1, gate, g.reshape(1, d), b.reshape(1, d))


def kernel(x, ln0_g, ln0_b, w_in, b_in, rpb, w_proj_a, w_proj_b, w_o, b_o, ln1_g, ln1_b,
           w_router_group, b_router_group, w_router_expert, b_router_expert,
           w_gate, w_up, w_down, ln2_g, ln2_b):
    bsz, s, d = x.shape
    n = bsz * s
    assert d == D_MODEL and w_in.shape[0] == DEPTH
    scale = HEAD_DIM ** -0.5

    col_scale = np.ones((w_in.shape[2],), np.float32)
    col_scale[:NA_WIDTH] = scale
    col_scale[3 * NA_WIDTH:3 * NA_WIDTH + DIL_WIDTH] = scale
    w_all = (w_in[0] * col_scale).astype(BF16)
    b_all = (b_in[0] * col_scale).reshape(1, -1)
    assert DIL_OUT_WIDTH == INPROJ_TN
    na_blocks, ngroups = 3 * NA_WIDTH // INPROJ_TN, len(DIL_GROUPS)
    w_r = jnp.concatenate([w_router_group[0], w_router_expert[0]], axis=1)
    w_r = jnp.pad(w_r, ((0, 0), (0, ROUTER_PAD - w_r.shape[1])))
    w_r_hi = w_r.astype(BF16)
    w_r_cat = jnp.concatenate([w_r_hi, (w_r - w_r_hi.astype(F32)).astype(BF16)], axis=1)
    b_r = jnp.pad(jnp.concatenate([b_router_group[0], b_router_expert[0]]),
                  (0, ROUTER_PAD - N_GROUPS - N_EXPERTS)).reshape(1, ROUTER_PAD)
    slopes = _alibi_slopes(DIL_HEADS)

    x2 = x.reshape(n, d)
    hb = _ln0(x2, ln0_g, ln0_b)
    zq = [_inproj(hb, w_all, b_all, bsz, s, DIL_GROUPS[0][1], 3 * NA_WIDTH + 3 * DIL_OUT_WIDTH,
                  lambda j: jnp.where(j < na_blocks, j, na_blocks + (j - na_blocks) * ngroups))]
    for g in range(1, ngroups):
        zq.append(_inproj(hb, w_all, b_all, bsz, s, DIL_GROUPS[g][1], 3 * DIL_OUT_WIDTH,
                          lambda j, g=g: na_blocks + j * ngroups + g))
    na = _neighborhood_attention(zq[0].reshape(-1, bsz, s, HEAD_DIM), rpb[0])
    na = na.reshape(NA_HEADS, n, HEAD_DIM)
    dil = [_dilated_group(zq[g], DIL0_H if g == 0 else 0, g, slopes) for g in range(ngroups)]
    m = _mix(na, [o for o, _ in dil], [l for _, l in dil], hb,
             w_proj_a[0].astype(BF16), w_proj_b[0].astype(BF16), w_all, b_all, s)
    h1, h1p, logits = _outproj(m, x2, ln0_g, ln0_b, w_o[0].astype(BF16), b_o[0], ln1_g[0], ln1_b[0],
                               w_r_cat, b_r)
    dest, gate, slot_tok, block_e, block_next, n_active = _route(logits, n)
    ys = _experts(h1p, slot_tok, block_e, block_next, n_active, w_gate[0], w_up[0], w_down[0])
    out = _final(dest, ys, h1, gate, ln2_g[0], ln2_b[0])
    return out.reshape(bsz, s, d)
```

```python
import functools

import numpy as np
import jax
import jax.numpy as jnp
from jax import lax
from jax.experimental import pallas as pl
from jax.experimental.pallas import tpu as pltpu

F32 = jnp.float32
BF16 = jnp.bfloat16

D_MODEL = 2048
HEAD_DIM = 128
GRID_W = 64
NA_HEADS = 8
NA_KH = 8
NA_KW = 16
DIL_GROUPS = ((128, 1), (512, 4), (2048, 16))
DIL_HEADS_PER_GROUP = 4
DIL_HEADS = DIL_HEADS_PER_GROUP * len(DIL_GROUPS)
N_GROUPS = 8
EXPERTS_PER_GROUP = 8
N_EXPERTS = N_GROUPS * EXPERTS_PER_GROUP
TOP_K = 2
D_EXPERT = D_MODEL // 4
LN_EPS = 1e-5
DEPTH = 1
DN_ALPHA = (2 * DEPTH) ** 0.25
NA_WIDTH = NA_HEADS * HEAD_DIM
DIL_WIDTH = DIL_HEADS * HEAD_DIM
DIL_OUT_WIDTH = DIL_HEADS_PER_GROUP * HEAD_DIM
QKV_WIDTH = 3 * NA_WIDTH + 3 * DIL_WIDTH
QKV_HEADS = QKV_WIDTH // HEAD_DIM
LANES = 128
CHUNKS = D_MODEL // LANES
PACKED = CHUNKS // 2
MASK_VALUE = -1e30

QA_H, KA_H, VA_H = 0, NA_HEADS, 2 * NA_HEADS
DIL0_H = 3 * NA_HEADS

LN_TM = 512
INPROJ_TM, INPROJ_TN = 2048, 512
NA_QROWS = 4
NA_KROWS = 12
NA_NSUB = 8
DIL_NSUB = 8
DIL_TQ = 256
DIL_HALF = 64
DIL_TK = DIL_TQ + 2 * DIL_HALF
MIX_TM, MIX_TN = 512, 512
OUT_TM = 512
ROUTER_PAD = 128
ROUTE_TM = 512
MOE_TB = 128
FIN_TM = 256
CAST_VREGS = 32
GATHER_SLOTS = 3
WEIGHT_DMA_PRIORITY = 1
VMEM_LIMIT = 56 * 1024 * 1024


def _params(sem, limit=VMEM_LIMIT):
    return pltpu.CompilerParams(dimension_semantics=sem, vmem_limit_bytes=limit)


def _layer_norm(x, g, b):
    mu = jnp.mean(x, axis=-1, keepdims=True)
    xc = x - mu
    var = jnp.mean(xc * xc, axis=-1, keepdims=True)
    return xc * lax.rsqrt(var + LN_EPS) * g + b


HIGH_HALF = np.uint32(0xFFFF0000)


def _bf16_bits(x):
    return lax.bitcast_convert_type(x.astype(BF16).astype(F32), jnp.uint32)


def _pack_rows(ref, val):
    for c in range(PACKED):
        lo = _bf16_bits(val[:, c * LANES:(c + 1) * LANES])
        hi = _bf16_bits(val[:, (c + PACKED) * LANES:(c + PACKED + 1) * LANES])
        ref[c] = (lo >> 16) | (hi & HIGH_HALF)


def _unpack_rows(ref):
    words = [ref[c] for c in range(PACKED)]
    lo = [lax.bitcast_convert_type(w << 16, F32) for w in words]
    hi = [lax.bitcast_convert_type(w & HIGH_HALF, F32) for w in words]
    return jnp.concatenate(lo + hi, axis=-1)


def _tile_copies(hbm, row0, rows, vmem, sem, to_hbm):
    copies = []
    for c in range(PACKED):
        h, v = hbm.at[pl.ds(row0, rows), c], vmem.at[c]
        copies.append(pltpu.make_async_copy(v, h, sem) if to_hbm else pltpu.make_async_copy(h, v, sem))
    return copies


def _start_row_gather(idx_ref, base, step, count, src_hbm, dst, sem, both_threads, inline=False):
    def start(j, c):
        for half in range(2):
            t = 2 * j + half
            pltpu.make_async_copy(src_hbm.at[idx_ref[base + t * step]], dst.at[:, t], sem).start(
                priority=half if both_threads else 0)
        return c

    if inline:
        for j in range(count // 2):
            start(j, 0)
    else:
        lax.fori_loop(0, count // 2, start, 0, unroll=2)


def _wait_row_gather(dst, sem):
    pltpu.make_async_copy(dst, dst, sem).wait()


def _ln0_kernel(x_ref, g_ref, b_ref, hb_ref):
    hb_ref[...] = _layer_norm(x_ref[...], g_ref[...], b_ref[...]).astype(BF16)


def _ln0(x, g, b):
    n, d = x.shape
    row = pl.BlockSpec((LN_TM, d), lambda i: (i, 0))
    vec = pl.BlockSpec((1, d), lambda i: (0, 0))
    return pl.pallas_call(
        _ln0_kernel,
        grid=(n // LN_TM,),
        in_specs=[row, vec, vec],
        out_specs=row,
        out_shape=jax.ShapeDtypeStruct((n, d), BF16),
        compiler_params=_params(("parallel",)),
        name="ln0",
    )(x, g.reshape(1, d), b.reshape(1, d))


def _inproj_kernel(hb_ref, w_ref, b_ref, o_ref, acc_s, *, dil):
    acc = jnp.dot(hb_ref[...], w_ref[0], preferred_element_type=F32) + b_ref[...]
    if dil == 1:
        for c in range(INPROJ_TN // LANES):
            o_ref[c, 0, 0] = acc[:, c * LANES:(c + 1) * LANES].astype(BF16)
    else:
        for c in range(INPROJ_TN // LANES):
            acc_s[c] = acc[:, c * LANES:(c + 1) * LANES]
        for c in range(INPROJ_TN // LANES):
            for r in range(dil):
                o_ref[c, 0, r] = acc_s[c, pl.ds(r, INPROJ_TM // dil, stride=dil), :].astype(BF16)


def _inproj(hb, w, b, bsz, s, dil, width, col_block):
    n, d = hb.shape
    tiles = s // INPROJ_TM
    assert INPROJ_TM % (16 * dil) == 0 and s % INPROJ_TM == 0 and width % INPROJ_TN == 0
    return pl.pallas_call(
        functools.partial(_inproj_kernel, dil=dil),
        grid=(n // INPROJ_TM, width // INPROJ_TN),
        in_specs=[pl.BlockSpec((INPROJ_TM, d), lambda i, j: (i, 0)),
                  pl.BlockSpec((1, d, INPROJ_TN), lambda i, j: (col_block(j), 0, 0)),
                  pl.BlockSpec((1, INPROJ_TN), lambda i, j: (0, col_block(j)))],
        out_specs=pl.BlockSpec((INPROJ_TN // LANES, 1, dil, INPROJ_TM // dil, LANES),
                               lambda i, j: (j, i // tiles, 0, i % tiles, 0)),
        out_shape=jax.ShapeDtypeStruct((width // LANES, bsz, dil, s // dil, LANES), BF16),
        scratch_shapes=[pltpu.VMEM((INPROJ_TN // LANES, INPROJ_TM, LANES), F32)],
        compiler_params=_params(("parallel", "arbitrary")),
        name=f"inproj_{dil}",
    )(hb, w, b)


def _na_bias_tiles(rpb, rows):
    heads, n_dr, n_dc = rpb.shape
    qc = np.arange(GRID_W)[:, None]
    kc = np.arange(GRID_W)[None, :]
    qcs = np.clip(qc - NA_KW // 2, 0, GRID_W - NA_KW)
    v_col = (kc >= qcs) & (kc < qcs + NA_KW)
    dc = np.clip(kc - qc + NA_KW - 1, 0, n_dc - 1)
    onehot = (dc[None] == np.arange(n_dc)[:, None, None]) & v_col[None]
    toep = jnp.einsum('hrd,dqk->hrqk', rpb.astype(F32), jnp.asarray(onehot, F32),
                      precision=lax.Precision.HIGHEST)
    toep = jnp.where(jnp.asarray(v_col), toep, MASK_VALUE)
    masked = jnp.full((heads, 1, GRID_W, GRID_W), MASK_VALUE, F32)
    blocks = jnp.concatenate([toep, masked], axis=1)
    i = np.arange(NA_QROWS)[:, None]
    j = np.arange(NA_KROWS)[None, :]
    sel = []
    for r0, ks in ((0, 0), (2 * NA_QROWS, 2 * NA_QROWS - NA_KH // 2), (rows - NA_QROWS, rows - NA_KROWS)):
        r, krow = r0 + i, ks + j
        start = np.clip(r - NA_KH // 2, 0, rows - NA_KH)
        v_row = (krow >= start) & (krow < start + NA_KH)
        sel.append(np.where(v_row, krow - r + NA_KH - 1, n_dr))
    sel = np.stack(sel)
    pairs = sel.reshape(3, NA_QROWS, NA_KROWS // 2, 2)
    distinct = sorted({tuple(p) for p in pairs.reshape(-1, 2).tolist()})
    plan = [[[distinct.index(tuple(pairs[v, i, c])) for c in range(NA_KROWS // 2)]
             for i in range(NA_QROWS)] for v in range(3)]
    left = jnp.take(blocks, jnp.asarray([p[0] for p in distinct], jnp.int32), axis=1)
    right = jnp.take(blocks, jnp.asarray([p[1] for p in distinct], jnp.int32), axis=1)
    return jnp.concatenate([left, right], axis=-1), plan


def _attend_blocks(nblocks, scores, finish):
    scored, weighted = {}, {}
    for t in range(nblocks + 2):
        if t < nblocks:
            scored[t] = scores(t)
        if 0 <= t - 1 < nblocks:
            s, vw = scored.pop(t - 1)
            m = jnp.max(s, axis=-1, keepdims=True)
            p = jnp.exp(s - m)
            weighted[t - 1] = (p.astype(BF16), vw, m, jnp.sum(p, axis=-1, keepdims=True))
        if 0 <= t - 2 < nblocks:
            p, vw, m, l = weighted.pop(t - 2)
            finish(t - 2, jnp.dot(p, vw, preferred_element_type=F32) / l, m, l)


def _scores(q, kw, bias):
    return lax.dot_general(q, kw, (((1,), (1,)), ((), ())), preferred_element_type=F32) + bias


def _edge_variant(blk, nblk):
    return jnp.where(blk == 0, 0, jnp.where(blk == nblk - 1, 2, 1))


def _na_kernel(q_ref, k_ref, v_ref, tile_ref, o_ref, tab_ref, *, rows, plan):
    tq = NA_QROWS * GRID_W

    @pl.when(pl.program_id(2) == 0)
    def _():
        for v in range(3):
            for i in range(NA_QROWS):
                for c in range(NA_KROWS // 2):
                    tab_ref[v, 0, i * GRID_W:(i + 1) * GRID_W, c * 2 * GRID_W:(c + 1) * 2 * GRID_W] = (
                        tile_ref[0, plan[v][i][c]])

    def scores(u):
        blk = pl.program_id(2) * NA_NSUB + u
        ks = jnp.clip(blk * NA_QROWS - NA_KH // 2, 0, rows - NA_KROWS)
        off = pl.multiple_of(ks * GRID_W, GRID_W)
        kw = k_ref[0, 0, pl.ds(off, NA_KROWS * GRID_W), :]
        vw = v_ref[0, 0, pl.ds(off, NA_KROWS * GRID_W), :]
        bias = tab_ref[_edge_variant(blk, rows // NA_QROWS), 0]
        return _scores(q_ref[0, 0, u * tq:(u + 1) * tq, :], kw, bias), vw

    def finish(u, o, m, l):
        o_ref[0, 0, u * tq:(u + 1) * tq, :] = o.astype(BF16)

    _attend_blocks(NA_NSUB, scores, finish)


def _neighborhood_attention(zq4, rpb):
    _, bsz, s, _ = zq4.shape
    rows = s // GRID_W
    assert rows % (NA_QROWS * NA_NSUB) == 0 and rows >= NA_KROWS + NA_QROWS and NA_KROWS % 2 == 0
    tiles, plan = _na_bias_tiles(rpb, rows)
    tq = NA_QROWS * GRID_W * NA_NSUB
    return pl.pallas_call(
        functools.partial(_na_kernel, rows=rows, plan=plan),
        grid=(bsz, NA_HEADS, s // tq),
        in_specs=[pl.BlockSpec((1, 1, tq, HEAD_DIM), lambda b, h, i: (QA_H + h, b, i, 0)),
                  pl.BlockSpec((1, 1, s, HEAD_DIM), lambda b, h, i: (KA_H + h, b, 0, 0)),
                  pl.BlockSpec((1, 1, s, HEAD_DIM), lambda b, h, i: (VA_H + h, b, 0, 0)),
                  pl.BlockSpec((1,) + tiles.shape[1:], lambda b, h, i: (h, 0, 0, 0))],
        out_specs=pl.BlockSpec((1, 1, tq, HEAD_DIM), lambda b, h, i: (h, b, i, 0)),
        out_shape=jax.ShapeDtypeStruct((NA_HEADS, bsz, s, HEAD_DIM), BF16),
        scratch_shapes=[pltpu.VMEM((3, 1, NA_QROWS * GRID_W, NA_KROWS * GRID_W), F32)],
        compiler_params=_params(("arbitrary", "arbitrary", "arbitrary")),
        name="na_attn",
    )(zq4, zq4, zq4, tiles)


def _alibi_slopes(n):
    return np.array([2.0 ** (-8.0 * (i + 1) / n) for i in range(n)], dtype=np.float32)


def _dil_bias_tables(slopes, dil):
    qi = np.arange(DIL_TQ)[:, None]
    kj = np.arange(DIL_TK)[None, :]
    dist = np.stack([np.abs(kj - qi + shift) for shift in (0, -DIL_HALF, -2 * DIL_HALF)])
    dist = jnp.asarray(dist, F32)[:, None]
    penalty = jnp.asarray(slopes, F32)[None, :, None, None] * (dist * float(dil))
    return jnp.where(dist <= DIL_HALF, -penalty, MASK_VALUE)


def _dil_kernel(q_ref, k_ref, v_ref, tab_ref, o_ref, lse_ref, *, length, nsub, nres):
    def scores(u):
        res, sub = divmod(u, nsub)
        blk = pl.program_id(3) * nsub + sub
        ks = jnp.clip(blk * DIL_TQ - DIL_HALF, 0, length - DIL_TK)
        off = pl.multiple_of(ks, DIL_HALF)
        kw = k_ref[0, 0, res, pl.ds(off, DIL_TK), :]
        vw = v_ref[0, 0, res, pl.ds(off, DIL_TK), :]
        bias = tab_ref[_edge_variant(blk, length // DIL_TQ), 0]
        return _scores(q_ref[0, 0, res, sub * DIL_TQ:(sub + 1) * DIL_TQ, :], kw, bias), vw

    def finish(u, o, m, l):
        res, sub = divmod(u, nsub)
        rows = slice(sub * DIL_TQ, (sub + 1) * DIL_TQ)
        o_ref[0, 0, res, rows, :] = o.astype(BF16)
        lse_ref[0, 0, res, rows, :] = jnp.broadcast_to(m + jnp.log(l), (DIL_TQ, LANES))

    _attend_blocks(nres * nsub, scores, finish)


def _dilated_group(zq, base, g, slopes):
    window, dil = DIL_GROUPS[g]
    assert (window // 2) // dil == DIL_HALF
    _, bsz, _, length, _ = zq.shape
    nsub = min(DIL_NSUB, length // DIL_TQ)
    nres = min(dil, DIL_NSUB // nsub)
    assert zq.shape[2] == dil and length % (DIL_TQ * nsub) == 0 and length >= DIL_TK and dil % nres == 0
    gh = DIL_HEADS_PER_GROUP
    tabs = _dil_bias_tables(slopes[g * gh:(g + 1) * gh], dil)
    tq = DIL_TQ * nsub
    qspec = pl.BlockSpec((1, 1, nres, tq, HEAD_DIM), lambda b, h, r, n: (base + h, b, r, n, 0))
    kvspec = lambda off: pl.BlockSpec((1, 1, nres, length, HEAD_DIM),
                                      lambda b, h, r, n: (base + off + h, b, r, 0, 0))
    ospec = pl.BlockSpec((1, 1, nres, tq, HEAD_DIM), lambda b, h, r, n: (h, b, r, n, 0))
    return pl.pallas_call(
        functools.partial(_dil_kernel, length=length, nsub=nsub, nres=nres),
        grid=(bsz, gh, dil // nres, length // tq),
        in_specs=[qspec, kvspec(gh), kvspec(2 * gh),
                  pl.BlockSpec((3, 1, DIL_TQ, DIL_TK), lambda b, h, r, n: (0, h, 0, 0))],
        out_specs=[ospec, ospec],
        out_shape=[jax.ShapeDtypeStruct((gh, bsz, dil, length, HEAD_DIM), BF16),
                   jax.ShapeDtypeStruct((gh, bsz, dil, length, HEAD_DIM), F32)],
        compiler_params=_params(("parallel", "parallel", "parallel", "arbitrary")),
        name=f"dil_attn_{dil}",
    )(zq, zq, zq, tabs)


def _mix_kernel(na_ref, o1_ref, o2_ref, o3_ref, l1_ref, l2_ref, l3_ref, hb_ref,
                wpa_ref, wpb_ref, wga_ref, wgb_ref, bga_ref, bgb_ref, m_ref, na_s, dil_s, o_s, l_s):
    @pl.when(pl.program_id(1) == 0)
    def _():
        for h in range(NA_HEADS):
            na_s[:, h * HEAD_DIM:(h + 1) * HEAD_DIM] = na_ref[h]
        for g, (o_ref, l_ref) in enumerate(((o1_ref, l1_ref), (o2_ref, l2_ref), (o3_ref, l3_ref))):
            dil = DIL_GROUPS[g][1]
            for h in range(DIL_HEADS_PER_GROUP):
                for r in range(dil):
                    rows = slice(None) if dil == 1 else pl.ds(r, MIX_TM // dil, stride=dil)
                    o_s[g, h, rows, :] = o_ref[h, 0, r].astype(F32)
                    l_s[g, h, rows, :] = l_ref[h, 0, r]
        for h in range(DIL_HEADS_PER_GROUP):
            ls = [l_s[g, h] for g in range(len(DIL_GROUPS))]
            mx = jnp.maximum(jnp.maximum(ls[0], ls[1]), ls[2])
            es = [jnp.exp(l - mx) for l in ls]
            den = es[0] + es[1] + es[2]
            acc = (es[0] * o_s[0, h] + es[1] * o_s[1, h] + es[2] * o_s[2, h]) / den
            dil_s[:, h * HEAD_DIM:(h + 1) * HEAD_DIM] = acc.astype(BF16)

    hb = hb_ref[...]
    ya = jnp.dot(na_s[...], wpa_ref[0], preferred_element_type=F32)
    yb = jnp.dot(dil_s[...], wpb_ref[0], preferred_element_type=F32)
    ga = jnp.dot(hb, wga_ref[0], preferred_element_type=F32) + bga_ref[...]
    gb = jnp.dot(hb, wgb_ref[0], preferred_element_type=F32) + bgb_ref[...]
    m_ref[...] = (jax.nn.sigmoid(ga) * ya + jax.nn.sigmoid(gb) * yb).astype(BF16)


def _mix(na, dil_o, dil_lse, hb, wpa, wpb, w_all, b_all, s):
    n, d = hb.shape
    tm, tn = MIX_TM, MIX_TN
    tiles = s // tm
    gh = DIL_HEADS_PER_GROUP
    ngroups = len(DIL_GROUPS)
    assert all(tm % (16 * dil) == 0 for _, dil in DIL_GROUPS) and s % tm == 0 and tn == INPROJ_TN
    col = lambda k: pl.BlockSpec((1, k, tn), lambda i, j: (j, 0, 0))
    ga0, gb0 = QKV_WIDTH // tn, (QKV_WIDTH + d) // tn
    gate = lambda first: pl.BlockSpec((1, d, tn), lambda i, j: (first + j, 0, 0))
    bias = lambda first: pl.BlockSpec((1, tn), lambda i, j: (0, first + j))
    grp = lambda dil: pl.BlockSpec((gh, 1, dil, tm // dil, HEAD_DIM),
                                   lambda i, j: (0, i // tiles, 0, i % tiles, 0))
    groups = [grp(dil) for _, dil in DIL_GROUPS]
    return pl.pallas_call(
        _mix_kernel,
        grid=(n // tm, d // tn),
        in_specs=[pl.BlockSpec((NA_HEADS, tm, HEAD_DIM), lambda i, j: (0, i, 0)), *groups, *groups,
                  pl.BlockSpec((tm, d), lambda i, j: (i, 0)),
                  col(NA_WIDTH), col(DIL_OUT_WIDTH),
                  gate(ga0), gate(gb0), bias(ga0), bias(gb0)],
        out_specs=pl.BlockSpec((tm, tn), lambda i, j: (i, j)),
        out_shape=jax.ShapeDtypeStruct((n, d), BF16),
        scratch_shapes=[pltpu.VMEM((tm, NA_WIDTH), BF16), pltpu.VMEM((tm, DIL_OUT_WIDTH), BF16),
                        pltpu.VMEM((ngroups, gh, tm, HEAD_DIM), F32),
                        pltpu.VMEM((ngroups, gh, tm, HEAD_DIM), F32)],
        compiler_params=_params(("parallel", "arbitrary")),
        name="mix",
    )(na, *dil_o, *dil_lse, hb, wpa, wpb, w_all, w_all, b_all, b_all)


def _outproj_kernel(m_ref, x_ref, g0_ref, b0_ref, wo_ref, bo_ref, g_ref, b_ref, wr_ref, br_ref,
                    h1_ref, h1p_hbm, lg_ref, hbuf, sem):
    i = pl.program_id(0)
    slot = i % 2

    def out_copies(step, sl):
        return _tile_copies(h1p_hbm, step * OUT_TM, OUT_TM, hbuf.at[sl], sem.at[sl], True)

    mix = jnp.dot(m_ref[...], wo_ref[...], preferred_element_type=F32) + bo_ref[...]
    h = _layer_norm(x_ref[...], g0_ref[...], b0_ref[...])
    h1 = _layer_norm(DN_ALPHA * h + mix, g_ref[...], b_ref[...])
    h1_ref[...] = h1

    @pl.when(i >= 2)
    def _():
        for cp in out_copies(i - 2, slot):
            cp.wait()

    _pack_rows(hbuf.at[slot], h1)
    for cp in out_copies(i, slot):
        cp.start()

    @pl.when(i == pl.num_programs(0) - 1)
    def _():
        for cp in out_copies(i, slot):
            cp.wait()

        @pl.when(i >= 1)
        def _():
            for cp in out_copies(i - 1, 1 - slot):
                cp.wait()

    hi = h1.astype(BF16)
    lo = (h1 - hi.astype(F32)).astype(BF16)
    both = jnp.dot(hi, wr_ref[...], preferred_element_type=F32)
    lg = (both[:, :ROUTER_PAD] + both[:, ROUTER_PAD:]
          + jnp.dot(lo, wr_ref[:, :ROUTER_PAD], preferred_element_type=F32))
    lg_ref[...] = lg + br_ref[...]


def _outproj(m, x, g0, b0, wo, bo, g, b, wr, br):
    n, d = x.shape
    tm = OUT_TM
    row = lambda w: pl.BlockSpec((tm, w), lambda i: (i, 0))
    full = lambda r, c: pl.BlockSpec((r, c), lambda i: (0, 0), pipeline_mode=pl.Buffered(1))
    return pl.pallas_call(
        _outproj_kernel,
        grid=(n // tm,),
        in_specs=[row(d), row(d), full(1, d), full(1, d), full(d, d), full(1, d), full(1, d), full(1, d),
                  full(d, 2 * ROUTER_PAD), full(1, ROUTER_PAD)],
        out_specs=[row(d), pl.BlockSpec(memory_space=pl.ANY), row(ROUTER_PAD)],
        out_shape=[jax.ShapeDtypeStruct((n, d), F32),
                   jax.ShapeDtypeStruct((n, PACKED, LANES), jnp.uint32),
                   jax.ShapeDtypeStruct((n, ROUTER_PAD), F32)],
        scratch_shapes=[pltpu.VMEM((2, PACKED, tm, LANES), jnp.uint32), pltpu.SemaphoreType.DMA((2,))],
        compiler_params=_params(("arbitrary",)),
        name="outproj",
    )(m, x, g0.reshape(1, d), b0.reshape(1, d), wo, bo.reshape(1, d), g.reshape(1, d), b.reshape(1, d),
      wr, br)


def _first_argmax(vals, lane_f):
    top = jnp.max(vals, axis=-1, keepdims=True)
    idx = jnp.min(jnp.where(vals == top, lane_f, float(LANES)), axis=-1, keepdims=True)
    return top, idx


def _route_kernel(lg_ref, gate_ref, dest_ref, cnt_ref, base_s, start_s):
    phase = pl.program_id(0)

    @pl.when(pl.program_id(1) == 0)
    def _():
        @pl.when(phase == 1)
        def _():
            counts = base_s[...]
            cnt_ref[...] = counts
            blocks = jnp.ceil(counts * (1.0 / MOE_TB))
            before = (lax.broadcasted_iota(jnp.int32, (ROUTER_PAD, ROUTER_PAD), 0)
                      < lax.broadcasted_iota(jnp.int32, (ROUTER_PAD, ROUTER_PAD), 1)).astype(BF16)
            starts = jnp.dot(jnp.broadcast_to(blocks, (8, ROUTER_PAD)).astype(BF16), before,
                             preferred_element_type=F32)
            start_s[...] = starts[0:1] * float(MOE_TB)

        base_s[...] = jnp.zeros(base_s.shape, F32)

    lg = lg_ref[...]
    lane = lax.broadcasted_iota(jnp.int32, lg.shape, 1)
    lane_f = lane.astype(F32)
    g_mask = lane < N_GROUPS
    g_top, g_sel = _first_argmax(jnp.where(g_mask, lg, MASK_VALUE), lane_f)
    g_prob = 1.0 / jnp.sum(jnp.where(g_mask, jnp.exp(lg - g_top), 0.0), axis=-1, keepdims=True)
    first = N_GROUPS + g_sel * EXPERTS_PER_GROUP
    e_mask = (lane_f >= first) & (lane_f < first + EXPERTS_PER_GROUP)
    el = jnp.where(e_mask, lg, MASK_VALUE)
    v0, i0 = _first_argmax(el, lane_f)
    v1, i1 = _first_argmax(jnp.where(lane_f == i0, MASK_VALUE, el), lane_f)
    e1 = jnp.exp(v1 - v0)
    w0 = g_prob / (1.0 + e1)
    w1 = g_prob * e1 / (1.0 + e1)

    onehots = [lane_f == idx - N_GROUPS for idx in (i0, i1)]

    @pl.when(phase == 0)
    def _():
        base_s[...] = base_s[...] + sum(jnp.sum(oh.astype(F32), axis=0, keepdims=True) for oh in onehots)

    @pl.when(phase == 1)
    def _():
        tm = lg.shape[0]
        tri = (lax.broadcasted_iota(jnp.int32, (tm, tm), 1)
               < lax.broadcasted_iota(jnp.int32, (tm, tm), 0)).astype(BF16)
        dests = []
        for onehot in onehots:
            before = jnp.dot(tri, onehot.astype(BF16), preferred_element_type=F32) + base_s[...]
            dests.append(jnp.sum(jnp.where(onehot, before + start_s[...], 0.0), axis=-1, keepdims=True))
            base_s[...] = base_s[...] + jnp.sum(onehot.astype(F32), axis=0, keepdims=True)
        gate_ref[...] = jnp.where(lane == 0, w0, jnp.where(lane == 1, w1, 0.0))
        cols = jnp.where(lane == 0, dests[0], jnp.where(lane == 1, dests[1], 0.0))
        dest_ref[...] = jnp.transpose(cols)[0:8, :].astype(jnp.int32)


def _invert_kernel(dest_ref, lo_ref, hi_ref, tok_ref, *, n):
    def clear_range(e, c):
        def clear(s, c2):
            tok_ref[s] = 0
            return c2

        lax.fori_loop(lo_ref[e], hi_ref[e], clear, 0)
        return c

    lax.fori_loop(0, lo_ref.shape[0], clear_range, 0)
    for k in range(TOP_K):
        def place(t, c, k=k):
            tok_ref[dest_ref[k * n + t]] = t
            return c

        lax.fori_loop(0, n, place, 0, unroll=8)


def _invert_slots(dest, unused_lo, unused_hi, n, slots):
    assert dest.shape[0] == TOP_K * n
    return pl.pallas_call(
        functools.partial(_invert_kernel, n=n),
        grid_spec=pltpu.PrefetchScalarGridSpec(
            num_scalar_prefetch=3, grid=(1,), in_specs=[],
            out_specs=pl.BlockSpec(memory_space=pltpu.SMEM)),
        out_shape=jax.ShapeDtypeStruct((slots,), jnp.int32),
        compiler_params=_params(("arbitrary",)),
        name="invert_slots",
    )(dest, unused_lo, unused_hi)


def _route(logits, n):
    tm = ROUTE_TM
    one = pl.BlockSpec((1, ROUTER_PAD), lambda p, i: (0, 0))
    gate, dest, counts = pl.pallas_call(
        _route_kernel,
        grid=(2, n // tm),
        in_specs=[pl.BlockSpec((tm, ROUTER_PAD), lambda p, i: (i, 0))],
        out_specs=[pl.BlockSpec((tm, ROUTER_PAD), lambda p, i: (i * p, 0)),
                   pl.BlockSpec((8, tm), lambda p, i: (0, i * p)), one],
        out_shape=[jax.ShapeDtypeStruct((n, ROUTER_PAD), F32),
                   jax.ShapeDtypeStruct((8, n), jnp.int32),
                   jax.ShapeDtypeStruct((1, ROUTER_PAD), F32)],
        scratch_shapes=[pltpu.VMEM((1, ROUTER_PAD), F32), pltpu.VMEM((1, ROUTER_PAD), F32)],
        compiler_params=_params(("arbitrary", "arbitrary")),
        name="route",
    )(logits)
    gate = gate[:, :TOP_K]
    dest = dest[:TOP_K].reshape(TOP_K * n)
    a = n * TOP_K
    counts = counts[0, :N_EXPERTS].astype(jnp.int32)
    pcounts = (counts + MOE_TB - 1) // MOE_TB * MOE_TB
    pends = jnp.cumsum(pcounts)
    nb = a // MOE_TB + N_EXPERTS
    slots = nb * MOE_TB
    unused_lo = jnp.concatenate([pends - pcounts + counts, pends[-1:]])
    unused_hi = jnp.concatenate([pends, jnp.full((1,), slots, jnp.int32)])
    slot_tok = _invert_slots(dest, unused_lo, unused_hi, n, slots)
    block_start = jnp.arange(nb, dtype=jnp.int32) * MOE_TB
    block_e = jnp.minimum(jnp.sum((pends[None, :] <= block_start[:, None]).astype(jnp.int32), axis=1),
                          N_EXPERTS - 1)
    n_active = (pends[-1] // MOE_TB).astype(jnp.int32).reshape(1)
    ids = jnp.arange(N_EXPERTS, dtype=jnp.int32)
    later = (counts[None, :] > 0) & (ids[None, :] > ids[:, None])
    nxt = jnp.min(jnp.where(later, ids[None, :], N_EXPERTS), axis=1)
    nxt = jnp.where(nxt == N_EXPERTS, -1, nxt)
    block_next = jnp.sum(jnp.where(block_e[:, None] == ids[None, :], nxt[None, :], 0), axis=1).astype(jnp.int32)
    return dest, gate, slot_tok, block_e, block_next, n_active


def _cast_weights(src, dst):
    rows, cols = src.shape
    step = CAST_VREGS * 8 * LANES // cols

    def body(r, c):
        sl = pl.ds(pl.multiple_of(r * step, step), step)
        dst[sl, :] = src[sl, :].astype(BF16)
        return c

    lax.fori_loop(0, rows // step, body, 0, unroll=2)


def _expert_kernel(be_ref, nx_ref, nact_ref, tok_ref, h1p_hbm, wg_hbm, wu_hbm, wd_hbm, ys_hbm,
                   xbuf, ybuf, wg_f, wu_f, wd_f, wg_s, wu_s, wd_s, gsem, osem, wsem):
    i = pl.program_id(0)
    slot = i % 2
    xslot = i % GATHER_SLOTS
    nact = nact_ref[0]

    def gather(step, inline=False):
        sl = step % GATHER_SLOTS
        _start_row_gather(tok_ref, step * MOE_TB, 1, MOE_TB, h1p_hbm, xbuf.at[sl], gsem.at[sl], False,
                          inline)

    def out_copies(step, sl):
        return _tile_copies(ys_hbm, step * MOE_TB, MOE_TB, ybuf.at[sl], osem.at[sl], True)

    def weight_copies(e):
        return [pltpu.make_async_copy(src.at[e], dst, wsem)
                for src, dst in ((wg_hbm, wg_f), (wu_hbm, wu_f), (wd_hbm, wd_f))]

    @pl.when((i == 0) & (nact > 0))
    def _():
        for cp in weight_copies(be_ref[0]):
            cp.start(priority=WEIGHT_DMA_PRIORITY)
        gather(0)

    @pl.when((i == 0) & (nact > 1))
    def _():
        gather(1)

    @pl.when(i >= 2)
    def _():
        for cp in out_copies(i - 2, slot):
            cp.wait()

    @pl.when(i < nact)
    def _():
        expert = be_ref[i]

        @pl.when((i == 0) | (expert != be_ref[jnp.maximum(i - 1, 0)]))
        def _():
            for cp in weight_copies(expert):
                cp.wait()
            _cast_weights(wg_f, wg_s)
            _cast_weights(wu_f, wu_s)
            _cast_weights(wd_f, wd_s)

            @pl.when(nx_ref[i] >= 0)
            def _():
                for cp in weight_copies(nx_ref[i]):
                    cp.start(priority=WEIGHT_DMA_PRIORITY)

        _wait_row_gather(xbuf.at[xslot], gsem.at[xslot])

        def block(prefetch):
            if prefetch:
                gather(i + 2, inline=True)
            xb = _unpack_rows(xbuf.at[xslot]).astype(BF16)
            gate = jnp.dot(xb, wg_s[...], preferred_element_type=F32)
            up = jnp.dot(xb, wu_s[...], preferred_element_type=F32)
            hid = (jax.nn.silu(gate) * up).astype(BF16)
            _pack_rows(ybuf.at[slot], jnp.dot(hid, wd_s[...], preferred_element_type=F32))

        pl.when(i + 2 < nact)(functools.partial(block, True))
        pl.when(i + 2 >= nact)(functools.partial(block, False))

    @pl.when(i >= nact)
    def _():
        ybuf[slot] = jnp.zeros(ybuf.shape[1:], jnp.uint32)

    for cp in out_copies(i, slot):
        cp.start()

    @pl.when(i == pl.num_programs(0) - 1)
    def _():
        for cp in out_copies(i, slot):
            cp.wait()

        @pl.when(i >= 1)
        def _():
            for cp in out_copies(i - 1, 1 - slot):
                cp.wait()


def _experts(h1p, slot_tok, block_e, block_next, n_active, w_gate, w_up, w_down):
    d = D_MODEL
    slots = slot_tok.shape[0]
    nb = slots // MOE_TB
    any_spec = pl.BlockSpec(memory_space=pl.ANY)
    grid_spec = pltpu.PrefetchScalarGridSpec(
        num_scalar_prefetch=4,
        grid=(nb,),
        in_specs=[any_spec, any_spec, any_spec, any_spec],
        out_specs=any_spec,
        scratch_shapes=[pltpu.VMEM((GATHER_SLOTS, PACKED, MOE_TB, LANES), jnp.uint32),
                        pltpu.VMEM((2, PACKED, MOE_TB, LANES), jnp.uint32),
                        pltpu.VMEM((d, D_EXPERT), F32), pltpu.VMEM((d, D_EXPERT), F32),
                        pltpu.VMEM((D_EXPERT, d), F32),
                        pltpu.VMEM((d, D_EXPERT), BF16), pltpu.VMEM((d, D_EXPERT), BF16),
                        pltpu.VMEM((D_EXPERT, d), BF16),
                        pltpu.SemaphoreType.DMA((GATHER_SLOTS,)), pltpu.SemaphoreType.DMA((2,)),
                        pltpu.SemaphoreType.DMA(())])
    return pl.pallas_call(
        _expert_kernel,
        grid_spec=grid_spec,
        out_shape=jax.ShapeDtypeStruct((slots, PACKED, LANES), jnp.uint32),
        compiler_params=_params(("arbitrary",)),
        name="experts",
    )(block_e, block_next, n_active, slot_tok, h1p, w_gate, w_up, w_down)


def _final_kernel(dest_ref, ys_hbm, h1_ref, gate_ref, g_ref, b_ref, o_ref, ybuf, gsem):
    i = pl.program_id(0)
    steps = pl.num_programs(0)
    slot = i % GATHER_SLOTS

    def fetch(step, inline=False):
        sl = step % GATHER_SLOTS
        for k in range(TOP_K):
            _start_row_gather(dest_ref, k * steps * FIN_TM + step * FIN_TM, 1, FIN_TM, ys_hbm,
                              ybuf.at[sl, k], gsem.at[sl], True, inline)

    @pl.when(i == 0)
    def _():
        fetch(0)

    @pl.when((i == 0) & (steps > 1))
    def _():
        fetch(1)

    _wait_row_gather(ybuf.at[slot], gsem.at[slot])

    def combine(prefetch):
        if prefetch:
            fetch(i + 2, inline=True)
        gate = gate_ref[...]
        ffn = (_unpack_rows(ybuf.at[slot, 0]) * gate[:, 0:1]
               + _unpack_rows(ybuf.at[slot, 1]) * gate[:, 1:2])
        o_ref[...] = _layer_norm(DN_ALPHA * h1_ref[...] + ffn, g_ref[...], b_ref[...])

    pl.when(i + 2 < steps)(functools.partial(combine, True))
    pl.when(i + 2 >= steps)(functools.partial(combine, False))


def _final(dest, ys, h1, gate, g, b):
    n, d = h1.shape
    grid_spec = pltpu.PrefetchScalarGridSpec(
        num_scalar_prefetch=1,
        grid=(n // FIN_TM,),
        in_specs=[pl.BlockSpec(memory_space=pl.ANY),
                  pl.BlockSpec((FIN_TM, d), lambda i, ds: (i, 0)),
                  pl.BlockSpec((FIN_TM, TOP_K), lambda i, ds: (i, 0)),
                  pl.BlockSpec((1, d), lambda i, ds: (0, 0)),
                  pl.BlockSpec((1, d), lambda i, ds: (0, 0))],
        out_specs=pl.BlockSpec((FIN_TM, d), lambda i, ds: (i, 0)),
        scratch_shapes=[pltpu.VMEM((GATHER_SLOTS, TOP_K, PACKED, FIN_TM, LANES), jnp.uint32),
                        pltpu.SemaphoreType.DMA((GATHER_SLOTS,))])
    return pl.pallas_call(
        _final_kernel,
        grid_spec=grid_spec,
        out_shape=jax.ShapeDtypeStruct((n, d), F32),
        compiler_params=_params(("arbitrary",)),
        name="final",
    )(dest, ys, h1, gate, g.reshape(1, d), b.reshape(1, d))


def kernel(x, ln0_g, ln0_b, w_in, b_in, rpb, w_proj_a, w_proj_b, w_o, b_o, ln1_g, ln1_b,
           w_router_group, b_router_group, w_router_expert, b_router_expert,
           w_gate, w_up, w_down, ln2_g, ln2_b):
    bsz, s, d = x.shape
    n = bsz * s
    assert d == D_MODEL and w_in.shape[0] == DEPTH
    scale = HEAD_DIM ** -0.5

    col_scale = np.ones((w_in.shape[2],), np.float32)
    col_scale[:NA_WIDTH] = scale
    col_scale[3 * NA_WIDTH:3 * NA_WIDTH + DIL_WIDTH] = scale
    def block_major(w, tn):
        k, ncol = w.shape
        return w.astype(BF16).reshape(k, ncol // tn, tn).transpose(1, 0, 2)

    w_all = block_major(w_in[0] * col_scale, INPROJ_TN)
    b_all = (b_in[0] * col_scale).reshape(1, -1)
    assert DIL_OUT_WIDTH == INPROJ_TN
    na_blocks, ngroups = 3 * NA_WIDTH // INPROJ_TN, len(DIL_GROUPS)
    w_r = jnp.concatenate([w_router_group[0], w_router_expert[0]], axis=1)
    w_r = jnp.pad(w_r, ((0, 0), (0, ROUTER_PAD - w_r.shape[1])))
    w_r_hi = w_r.astype(BF16)
    w_r_cat = jnp.concatenate([w_r_hi, (w_r - w_r_hi.astype(F32)).astype(BF16)], axis=1)
    b_r = jnp.pad(jnp.concatenate([b_router_group[0], b_router_expert[0]]),
                  (0, ROUTER_PAD - N_GROUPS - N_EXPERTS)).reshape(1, ROUTER_PAD)
    slopes = _alibi_slopes(DIL_HEADS)

    x2 = x.reshape(n, d)
    hb = _ln0(x2, ln0_g, ln0_b)
    zq = [_inproj(hb, w_all, b_all, bsz, s, DIL_GROUPS[0][1], 3 * NA_WIDTH + 3 * DIL_OUT_WIDTH,
                  lambda j: jnp.where(j < na_blocks, j, na_blocks + (j - na_blocks) * ngroups))]
    for g in range(1, ngroups):
        zq.append(_inproj(hb, w_all, b_all, bsz, s, DIL_GROUPS[g][1], 3 * DIL_OUT_WIDTH,
                          lambda j, g=g: na_blocks + j * ngroups + g))
    na = _neighborhood_attention(zq[0].reshape(-1, bsz, s, HEAD_DIM), rpb[0])
    na = na.reshape(NA_HEADS, n, HEAD_DIM)
    dil = [_dilated_group(zq[g], DIL0_H if g == 0 else 0, g, slopes) for g in range(ngroups)]
    m = _mix(na, [o for o, _ in dil], [l for _, l in dil], hb,
             block_major(w_proj_a[0], MIX_TN), block_major(w_proj_b[0], MIX_TN), w_all, b_all, s)
    h1, h1p, logits = _outproj(m, x2, ln0_g, ln0_b, w_o[0].astype(BF16), b_o[0], ln1_g[0], ln1_b[0],
                               w_r_cat, b_r)
    dest, gate, slot_tok, block_e, block_next, n_active = _route(logits, n)
    ys = _experts(h1p, slot_tok, block_e, block_next, n_active, w_gate[0], w_up[0], w_down[0])
    out = _final(dest, ys, h1, gate, ln2_g[0], ln2_b[0])
    return out.reshape(bsz, s, d)
```

```python
import functools

import numpy as np
import jax
import jax.numpy as jnp
from jax import lax
from jax.experimental import pallas as pl
from jax.experimental.pallas import tpu as pltpu

F32 = jnp.float32
BF16 = jnp.bfloat16

D_MODEL = 2048
HEAD_DIM = 128
GRID_W = 64
NA_HEADS = 8
NA_KH = 8
NA_KW = 16
DIL_GROUPS = ((128, 1), (512, 4), (2048, 16))
DIL_HEADS_PER_GROUP = 4
DIL_HEADS = DIL_HEADS_PER_GROUP * len(DIL_GROUPS)
N_GROUPS = 8
EXPERTS_PER_GROUP = 8
N_EXPERTS = N_GROUPS * EXPERTS_PER_GROUP
TOP_K = 2
D_EXPERT = D_MODEL // 4
LN_EPS = 1e-5
DEPTH = 1
DN_ALPHA = (2 * DEPTH) ** 0.25
NA_WIDTH = NA_HEADS * HEAD_DIM
DIL_WIDTH = DIL_HEADS * HEAD_DIM
DIL_OUT_WIDTH = DIL_HEADS_PER_GROUP * HEAD_DIM
QKV_WIDTH = 3 * NA_WIDTH + 3 * DIL_WIDTH
QKV_HEADS = QKV_WIDTH // HEAD_DIM
LANES = 128
CHUNKS = D_MODEL // LANES
PACKED = CHUNKS // 2
MASK_VALUE = -1e30

QA_H, KA_H, VA_H = 0, NA_HEADS, 2 * NA_HEADS
DIL0_H = 3 * NA_HEADS

LN_TM = 1024
INPROJ_TM, INPROJ_TN = 2048, 512
NA_QROWS = 4
NA_KROWS = 12
NA_NSUB = 8
DIL_NSUB = 8
DIL_TQ = 256
DIL_HALF = 64
DIL_TK = DIL_TQ + 2 * DIL_HALF
MIX_TM, MIX_TN = 512, 512
MIX_CHUNK = 256
OUT_TM = 512
ROUTER_PAD = 128
ROUTE_TM = 1024
MOE_TB = 128
FIN_TM = 256
CAST_VREGS = 32
GATHER_SLOTS = 3
WEIGHT_DMA_PRIORITY = 1
VMEM_LIMIT = 56 * 1024 * 1024


def _params(sem, limit=VMEM_LIMIT):
    return pltpu.CompilerParams(dimension_semantics=sem, vmem_limit_bytes=limit)


def _layer_norm(x, g, b):
    mu = jnp.mean(x, axis=-1, keepdims=True)
    xc = x - mu
    var = jnp.mean(xc * xc, axis=-1, keepdims=True)
    return xc * lax.rsqrt(var + LN_EPS) * g + b


HIGH_HALF = np.uint32(0xFFFF0000)


def _bf16_bits(x):
    return lax.bitcast_convert_type(x.astype(BF16).astype(F32), jnp.uint32)


def _pack_rows(ref, val):
    for c in range(PACKED):
        lo = _bf16_bits(val[:, c * LANES:(c + 1) * LANES])
        hi = _bf16_bits(val[:, (c + PACKED) * LANES:(c + PACKED + 1) * LANES])
        ref[c] = (lo >> 16) | (hi & HIGH_HALF)


def _unpack_rows(ref):
    words = [ref[c] for c in range(PACKED)]
    lo = [lax.bitcast_convert_type(w << 16, F32) for w in words]
    hi = [lax.bitcast_convert_type(w & HIGH_HALF, F32) for w in words]
    return jnp.concatenate(lo + hi, axis=-1)


def _tile_copies(hbm, row0, rows, vmem, sem, to_hbm):
    copies = []
    for c in range(PACKED):
        h, v = hbm.at[pl.ds(row0, rows), c], vmem.at[c]
        copies.append(pltpu.make_async_copy(v, h, sem) if to_hbm else pltpu.make_async_copy(h, v, sem))
    return copies


def _start_row_gather(idx_ref, base, step, count, src_hbm, dst, sem, both_threads, inline=False):
    def start(j, c):
        for half in range(2):
            t = 2 * j + half
            pltpu.make_async_copy(src_hbm.at[idx_ref[base + t * step]], dst.at[:, t], sem).start(
                priority=half if both_threads else 0)
        return c

    if inline:
        for j in range(count // 2):
            start(j, 0)
    else:
        lax.fori_loop(0, count // 2, start, 0, unroll=2)


def _wait_row_gather(dst, sem):
    pltpu.make_async_copy(dst, dst, sem).wait()


def _ln0_kernel(x_ref, g_ref, b_ref, hb_ref):
    hb_ref[...] = _layer_norm(x_ref[...], g_ref[...], b_ref[...]).astype(BF16)


def _ln0(x, g, b):
    n, d = x.shape
    row = pl.BlockSpec((LN_TM, d), lambda i: (i, 0))
    vec = pl.BlockSpec((1, d), lambda i: (0, 0))
    return pl.pallas_call(
        _ln0_kernel,
        grid=(n // LN_TM,),
        in_specs=[row, vec, vec],
        out_specs=row,
        out_shape=jax.ShapeDtypeStruct((n, d), BF16),
        compiler_params=_params(("parallel",)),
        name="ln0",
    )(x, g.reshape(1, d), b.reshape(1, d))


def _inproj_kernel(hb_ref, w_ref, b_ref, o_ref, acc_s, *, dil):
    acc = jnp.dot(hb_ref[...], w_ref[...], preferred_element_type=F32) + b_ref[...]
    if dil == 1:
        for c in range(INPROJ_TN // LANES):
            o_ref[c, 0, 0] = acc[:, c * LANES:(c + 1) * LANES].astype(BF16)
    else:
        for c in range(INPROJ_TN // LANES):
            acc_s[c] = acc[:, c * LANES:(c + 1) * LANES]
        for c in range(INPROJ_TN // LANES):
            for r in range(dil):
                o_ref[c, 0, r] = acc_s[c, pl.ds(r, INPROJ_TM // dil, stride=dil), :].astype(BF16)


def _inproj(hb, w, b, bsz, s, dil, width, col_block):
    n, d = hb.shape
    tiles = s // INPROJ_TM
    assert INPROJ_TM % (16 * dil) == 0 and s % INPROJ_TM == 0 and width % INPROJ_TN == 0
    return pl.pallas_call(
        functools.partial(_inproj_kernel, dil=dil),
        grid=(n // INPROJ_TM, width // INPROJ_TN),
        in_specs=[pl.BlockSpec((INPROJ_TM, d), lambda i, j: (i, 0)),
                  pl.BlockSpec((d, INPROJ_TN), lambda i, j: (0, col_block(j))),
                  pl.BlockSpec((1, INPROJ_TN), lambda i, j: (0, col_block(j)))],
        out_specs=pl.BlockSpec((INPROJ_TN // LANES, 1, dil, INPROJ_TM // dil, LANES),
                               lambda i, j: (j, i // tiles, 0, i % tiles, 0)),
        out_shape=jax.ShapeDtypeStruct((width // LANES, bsz, dil, s // dil, LANES), BF16),
        scratch_shapes=[pltpu.VMEM((INPROJ_TN // LANES, INPROJ_TM, LANES), F32)],
        compiler_params=_params(("parallel", "arbitrary")),
        name=f"inproj_{dil}",
    )(hb, w, b)


def _na_bias_tiles(rpb, rows):
    heads, n_dr, n_dc = rpb.shape
    qc = np.arange(GRID_W)[:, None]
    kc = np.arange(GRID_W)[None, :]
    qcs = np.clip(qc - NA_KW // 2, 0, GRID_W - NA_KW)
    v_col = (kc >= qcs) & (kc < qcs + NA_KW)
    dc = np.clip(kc - qc + NA_KW - 1, 0, n_dc - 1)
    onehot = (dc[None] == np.arange(n_dc)[:, None, None]) & v_col[None]
    toep = jnp.einsum('hrd,dqk->hrqk', rpb.astype(F32), jnp.asarray(onehot, F32),
                      precision=lax.Precision.HIGHEST)
    toep = jnp.where(jnp.asarray(v_col), toep, MASK_VALUE)
    masked = jnp.full((heads, 1, GRID_W, GRID_W), MASK_VALUE, F32)
    blocks = jnp.concatenate([toep, masked], axis=1)
    i = np.arange(NA_QROWS)[:, None]
    j = np.arange(NA_KROWS)[None, :]
    sel = []
    for r0, ks in ((0, 0), (2 * NA_QROWS, 2 * NA_QROWS - NA_KH // 2), (rows - NA_QROWS, rows - NA_KROWS)):
        r, krow = r0 + i, ks + j
        start = np.clip(r - NA_KH // 2, 0, rows - NA_KH)
        v_row = (krow >= start) & (krow < start + NA_KH)
        sel.append(np.where(v_row, krow - r + NA_KH - 1, n_dr))
    sel = np.stack(sel)
    pairs = sel.reshape(3, NA_QROWS, NA_KROWS // 2, 2)
    distinct = sorted({tuple(p) for p in pairs.reshape(-1, 2).tolist()})
    plan = [[[distinct.index(tuple(pairs[v, i, c])) for c in range(NA_KROWS // 2)]
             for i in range(NA_QROWS)] for v in range(3)]
    left = jnp.take(blocks, jnp.asarray([p[0] for p in distinct], jnp.int32), axis=1)
    right = jnp.take(blocks, jnp.asarray([p[1] for p in distinct], jnp.int32), axis=1)
    return jnp.concatenate([left, right], axis=-1), plan


def _attend_blocks(nblocks, scores, finish):
    scored, weighted = {}, {}
    for t in range(nblocks + 2):
        if t < nblocks:
            scored[t] = scores(t)
        if 0 <= t - 1 < nblocks:
            s, vw = scored.pop(t - 1)
            m = jnp.max(s, axis=-1, keepdims=True)
            p = jnp.exp(s - m)
            weighted[t - 1] = (p.astype(BF16), vw, m, jnp.sum(p, axis=-1, keepdims=True))
        if 0 <= t - 2 < nblocks:
            p, vw, m, l = weighted.pop(t - 2)
            finish(t - 2, jnp.dot(p, vw, preferred_element_type=F32) / l, m, l)


def _scores(q, kw, bias):
    return lax.dot_general(q, kw, (((1,), (1,)), ((), ())), preferred_element_type=F32) + bias


def _edge_variant(blk, nblk):
    return jnp.where(blk == 0, 0, jnp.where(blk == nblk - 1, 2, 1))


def _na_kernel(q_ref, k_ref, v_ref, tile_ref, o_ref, tab_ref, *, rows, plan):
    tq = NA_QROWS * GRID_W

    @pl.when(pl.program_id(2) == 0)
    def _():
        for v in range(3):
            for i in range(NA_QROWS):
                for c in range(NA_KROWS // 2):
                    tab_ref[v, 0, i * GRID_W:(i + 1) * GRID_W, c * 2 * GRID_W:(c + 1) * 2 * GRID_W] = (
                        tile_ref[0, plan[v][i][c]])

    def scores(u):
        blk = pl.program_id(2) * NA_NSUB + u
        ks = jnp.clip(blk * NA_QROWS - NA_KH // 2, 0, rows - NA_KROWS)
        off = pl.multiple_of(ks * GRID_W, GRID_W)
        kw = k_ref[0, 0, pl.ds(off, NA_KROWS * GRID_W), :]
        vw = v_ref[0, 0, pl.ds(off, NA_KROWS * GRID_W), :]
        bias = tab_ref[_edge_variant(blk, rows // NA_QROWS), 0]
        return _scores(q_ref[0, 0, u * tq:(u + 1) * tq, :], kw, bias), vw

    def finish(u, o, m, l):
        o_ref[0, 0, u * tq:(u + 1) * tq, :] = o.astype(BF16)

    _attend_blocks(NA_NSUB, scores, finish)


def _neighborhood_attention(zq4, rpb):
    _, bsz, s, _ = zq4.shape
    rows = s // GRID_W
    assert rows % (NA_QROWS * NA_NSUB) == 0 and rows >= NA_KROWS + NA_QROWS and NA_KROWS % 2 == 0
    tiles, plan = _na_bias_tiles(rpb, rows)
    tq = NA_QROWS * GRID_W * NA_NSUB
    return pl.pallas_call(
        functools.partial(_na_kernel, rows=rows, plan=plan),
        grid=(bsz, NA_HEADS, s // tq),
        in_specs=[pl.BlockSpec((1, 1, tq, HEAD_DIM), lambda b, h, i: (QA_H + h, b, i, 0)),
                  pl.BlockSpec((1, 1, s, HEAD_DIM), lambda b, h, i: (KA_H + h, b, 0, 0)),
                  pl.BlockSpec((1, 1, s, HEAD_DIM), lambda b, h, i: (VA_H + h, b, 0, 0)),
                  pl.BlockSpec((1,) + tiles.shape[1:], lambda b, h, i: (h, 0, 0, 0))],
        out_specs=pl.BlockSpec((1, 1, tq, HEAD_DIM), lambda b, h, i: (h, b, i, 0)),
        out_shape=jax.ShapeDtypeStruct((NA_HEADS, bsz, s, HEAD_DIM), BF16),
        scratch_shapes=[pltpu.VMEM((3, 1, NA_QROWS * GRID_W, NA_KROWS * GRID_W), F32)],
        compiler_params=_params(("arbitrary", "arbitrary", "arbitrary")),
        name="na_attn",
    )(zq4, zq4, zq4, tiles)


def _alibi_slopes(n):
    return np.array([2.0 ** (-8.0 * (i + 1) / n) for i in range(n)], dtype=np.float32)


def _dil_bias_tables(slopes, dil):
    qi = np.arange(DIL_TQ)[:, None]
    kj = np.arange(DIL_TK)[None, :]
    dist = np.stack([np.abs(kj - qi + shift) for shift in (0, -DIL_HALF, -2 * DIL_HALF)])
    dist = jnp.asarray(dist, F32)[:, None]
    penalty = jnp.asarray(slopes, F32)[None, :, None, None] * (dist * float(dil))
    return jnp.where(dist <= DIL_HALF, -penalty, MASK_VALUE)


def _dil_kernel(q_ref, k_ref, v_ref, tab_ref, o_ref, lse_ref, *, length, nsub, nres):
    def scores(u):
        res, sub = divmod(u, nsub)
        blk = pl.program_id(3) * nsub + sub
        ks = jnp.clip(blk * DIL_TQ - DIL_HALF, 0, length - DIL_TK)
        off = pl.multiple_of(ks, DIL_HALF)
        kw = k_ref[0, 0, res, pl.ds(off, DIL_TK), :]
        vw = v_ref[0, 0, res, pl.ds(off, DIL_TK), :]
        bias = tab_ref[_edge_variant(blk, length // DIL_TQ), 0]
        return _scores(q_ref[0, 0, res, sub * DIL_TQ:(sub + 1) * DIL_TQ, :], kw, bias), vw

    def finish(u, o, m, l):
        res, sub = divmod(u, nsub)
        rows = slice(sub * DIL_TQ, (sub + 1) * DIL_TQ)
        o_ref[0, 0, res, rows, :] = o.astype(BF16)
        lse_ref[0, 0, res, rows, :] = jnp.broadcast_to(m + jnp.log(l), (DIL_TQ, LANES))

    _attend_blocks(nres * nsub, scores, finish)


def _dilated_group(zq, base, g, slopes):
    window, dil = DIL_GROUPS[g]
    assert (window // 2) // dil == DIL_HALF
    _, bsz, _, length, _ = zq.shape
    nsub = min(DIL_NSUB, length // DIL_TQ)
    nres = min(dil, DIL_NSUB // nsub)
    assert zq.shape[2] == dil and length % (DIL_TQ * nsub) == 0 and length >= DIL_TK and dil % nres == 0
    gh = DIL_HEADS_PER_GROUP
    tabs = _dil_bias_tables(slopes[g * gh:(g + 1) * gh], dil)
    tq = DIL_TQ * nsub
    qspec = pl.BlockSpec((1, 1, nres, tq, HEAD_DIM), lambda b, h, r, n: (base + h, b, r, n, 0))
    kvspec = lambda off: pl.BlockSpec((1, 1, nres, length, HEAD_DIM),
                                      lambda b, h, r, n: (base + off + h, b, r, 0, 0))
    ospec = pl.BlockSpec((1, 1, nres, tq, HEAD_DIM), lambda b, h, r, n: (h, b, r, n, 0))
    return pl.pallas_call(
        functools.partial(_dil_kernel, length=length, nsub=nsub, nres=nres),
        grid=(bsz, gh, dil // nres, length // tq),
        in_specs=[qspec, kvspec(gh), kvspec(2 * gh),
                  pl.BlockSpec((3, 1, DIL_TQ, DIL_TK), lambda b, h, r, n: (0, h, 0, 0))],
        out_specs=[ospec, ospec],
        out_shape=[jax.ShapeDtypeStruct((gh, bsz, dil, length, HEAD_DIM), BF16),
                   jax.ShapeDtypeStruct((gh, bsz, dil, length, HEAD_DIM), F32)],
        compiler_params=_params(("parallel", "parallel", "parallel", "arbitrary")),
        name=f"dil_attn_{dil}",
    )(zq, zq, zq, tabs)


def _mix_kernel(na_ref, o1_ref, o2_ref, o3_ref, l1_ref, l2_ref, l3_ref, hb_ref,
                wpa_ref, wpb_ref, wga_ref, wgb_ref, bga_ref, bgb_ref, m_ref, na_s, dil_s, o_s, l_s):
    @pl.when(pl.program_id(1) == 0)
    def _():
        for h in range(NA_HEADS):
            na_s[:, h * HEAD_DIM:(h + 1) * HEAD_DIM] = na_ref[h]
        for g, (o_ref, l_ref) in enumerate(((o1_ref, l1_ref), (o2_ref, l2_ref), (o3_ref, l3_ref))):
            dil = DIL_GROUPS[g][1]
            for h in range(DIL_HEADS_PER_GROUP):
                for r in range(dil):
                    rows = slice(None) if dil == 1 else pl.ds(r, MIX_TM // dil, stride=dil)
                    o_s[g, h, rows, :] = o_ref[h, 0, r].astype(F32)
                    l_s[g, h, rows, :] = l_ref[h, 0, r]
        for h in range(DIL_HEADS_PER_GROUP):
            ls = [l_s[g, h] for g in range(len(DIL_GROUPS))]
            mx = jnp.maximum(jnp.maximum(ls[0], ls[1]), ls[2])
            es = [jnp.exp(l - mx) for l in ls]
            den = es[0] + es[1] + es[2]
            acc = (es[0] * o_s[0, h] + es[1] * o_s[1, h] + es[2] * o_s[2, h]) / den
            dil_s[:, h * HEAD_DIM:(h + 1) * HEAD_DIM] = acc.astype(BF16)

    hb = hb_ref[...]
    na, dm = na_s[...], dil_s[...]

    def products(c):
        cols = slice(c * MIX_CHUNK, (c + 1) * MIX_CHUNK)
        return (jnp.dot(na, wpa_ref[:, cols], preferred_element_type=F32),
                jnp.dot(dm, wpb_ref[:, cols], preferred_element_type=F32),
                jnp.dot(hb, wga_ref[:, cols], preferred_element_type=F32) + bga_ref[:, cols],
                jnp.dot(hb, wgb_ref[:, cols], preferred_element_type=F32) + bgb_ref[:, cols])

    nchunks = MIX_TN // MIX_CHUNK
    pending = products(0)
    for c in range(nchunks):
        following = products(c + 1) if c + 1 < nchunks else None
        ya, yb, ga, gb = pending
        m_ref[:, c * MIX_CHUNK:(c + 1) * MIX_CHUNK] = (
            jax.nn.sigmoid(ga) * ya + jax.nn.sigmoid(gb) * yb).astype(BF16)
        pending = following


def _mix(na, dil_o, dil_lse, hb, wpa, wpb, w_all, b_all, s):
    n, d = hb.shape
    tm, tn = MIX_TM, MIX_TN
    tiles = s // tm
    gh = DIL_HEADS_PER_GROUP
    ngroups = len(DIL_GROUPS)
    assert all(tm % (16 * dil) == 0 for _, dil in DIL_GROUPS) and s % tm == 0
    col = lambda k: pl.BlockSpec((k, tn), lambda i, j: (0, j))
    ga0, gb0 = QKV_WIDTH // tn, (QKV_WIDTH + d) // tn
    gate = lambda k, first: pl.BlockSpec((k, tn), lambda i, j: (0, first + j))
    grp = lambda dil: pl.BlockSpec((gh, 1, dil, tm // dil, HEAD_DIM),
                                   lambda i, j: (0, i // tiles, 0, i % tiles, 0))
    groups = [grp(dil) for _, dil in DIL_GROUPS]
    return pl.pallas_call(
        _mix_kernel,
        grid=(n // tm, d // tn),
        in_specs=[pl.BlockSpec((NA_HEADS, tm, HEAD_DIM), lambda i, j: (0, i, 0)), *groups, *groups,
                  pl.BlockSpec((tm, d), lambda i, j: (i, 0)),
                  col(NA_WIDTH), col(DIL_OUT_WIDTH),
                  gate(d, ga0), gate(d, gb0), gate(1, ga0), gate(1, gb0)],
        out_specs=pl.BlockSpec((tm, tn), lambda i, j: (i, j)),
        out_shape=jax.ShapeDtypeStruct((n, d), BF16),
        scratch_shapes=[pltpu.VMEM((tm, NA_WIDTH), BF16), pltpu.VMEM((tm, DIL_OUT_WIDTH), BF16),
                        pltpu.VMEM((ngroups, gh, tm, HEAD_DIM), F32),
                        pltpu.VMEM((ngroups, gh, tm, HEAD_DIM), F32)],
        compiler_params=_params(("parallel", "arbitrary")),
        name="mix",
    )(na, *dil_o, *dil_lse, hb, wpa, wpb, w_all, w_all, b_all, b_all)


def _outproj_kernel(m_ref, x_ref, g0_ref, b0_ref, wo_ref, bo_ref, g_ref, b_ref, wr_ref, br_ref,
                    h1_ref, h1p_hbm, lg_ref, hbuf, sem):
    i = pl.program_id(0)
    slot = i % 2

    def out_copies(step, sl):
        return _tile_copies(h1p_hbm, step * OUT_TM, OUT_TM, hbuf.at[sl], sem.at[sl], True)

    mix = jnp.dot(m_ref[...], wo_ref[...], preferred_element_type=F32) + bo_ref[...]
    h = _layer_norm(x_ref[...], g0_ref[...], b0_ref[...])
    h1 = _layer_norm(DN_ALPHA * h + mix, g_ref[...], b_ref[...])
    h1_ref[...] = h1

    @pl.when(i >= 2)
    def _():
        for cp in out_copies(i - 2, slot):
            cp.wait()

    _pack_rows(hbuf.at[slot], h1)
    for cp in out_copies(i, slot):
        cp.start()

    @pl.when(i == pl.num_programs(0) - 1)
    def _():
        for cp in out_copies(i, slot):
            cp.wait()

        @pl.when(i >= 1)
        def _():
            for cp in out_copies(i - 1, 1 - slot):
                cp.wait()

    hi = h1.astype(BF16)
    lo = (h1 - hi.astype(F32)).astype(BF16)
    both = jnp.dot(hi, wr_ref[...], preferred_element_type=F32)
    lg = (both[:, :ROUTER_PAD] + both[:, ROUTER_PAD:]
          + jnp.dot(lo, wr_ref[:, :ROUTER_PAD], preferred_element_type=F32))
    lg_ref[...] = lg + br_ref[...]


def _outproj(m, x, g0, b0, wo, bo, g, b, wr, br):
    n, d = x.shape
    tm = OUT_TM
    row = lambda w: pl.BlockSpec((tm, w), lambda i: (i, 0))
    full = lambda r, c: pl.BlockSpec((r, c), lambda i: (0, 0), pipeline_mode=pl.Buffered(1))
    return pl.pallas_call(
        _outproj_kernel,
        grid=(n // tm,),
        in_specs=[row(d), row(d), full(1, d), full(1, d), full(d, d), full(1, d), full(1, d), full(1, d),
                  full(d, 2 * ROUTER_PAD), full(1, ROUTER_PAD)],
        out_specs=[row(d), pl.BlockSpec(memory_space=pl.ANY), row(ROUTER_PAD)],
        out_shape=[jax.ShapeDtypeStruct((n, d), F32),
                   jax.ShapeDtypeStruct((n, PACKED, LANES), jnp.uint32),
                   jax.ShapeDtypeStruct((n, ROUTER_PAD), F32)],
        scratch_shapes=[pltpu.VMEM((2, PACKED, tm, LANES), jnp.uint32), pltpu.SemaphoreType.DMA((2,))],
        compiler_params=_params(("arbitrary",)),
        name="outproj",
    )(m, x, g0.reshape(1, d), b0.reshape(1, d), wo, bo.reshape(1, d), g.reshape(1, d), b.reshape(1, d),
      wr, br)


def _first_argmax(vals, lane_f):
    top = jnp.max(vals, axis=-1, keepdims=True)
    idx = jnp.min(jnp.where(vals == top, lane_f, float(LANES)), axis=-1, keepdims=True)
    return top, idx


def _route_kernel(lg_ref, gate_ref, dest_ref, cnt_ref, base_s, start_s):
    phase = pl.program_id(0)

    @pl.when(pl.program_id(1) == 0)
    def _():
        @pl.when(phase == 1)
        def _():
            counts = base_s[...]
            cnt_ref[...] = counts
            blocks = jnp.ceil(counts * (1.0 / MOE_TB))
            before = (lax.broadcasted_iota(jnp.int32, (ROUTER_PAD, ROUTER_PAD), 0)
                      < lax.broadcasted_iota(jnp.int32, (ROUTER_PAD, ROUTER_PAD), 1)).astype(BF16)
            starts = jnp.dot(jnp.broadcast_to(blocks, (8, ROUTER_PAD)).astype(BF16), before,
                             preferred_element_type=F32)
            start_s[...] = starts[0:1] * float(MOE_TB)

        base_s[...] = jnp.zeros(base_s.shape, F32)

    lg = lg_ref[...]
    lane = lax.broadcasted_iota(jnp.int32, lg.shape, 1)
    lane_f = lane.astype(F32)
    g_mask = lane < N_GROUPS
    g_top, g_sel = _first_argmax(jnp.where(g_mask, lg, MASK_VALUE), lane_f)
    g_prob = 1.0 / jnp.sum(jnp.where(g_mask, jnp.exp(lg - g_top), 0.0), axis=-1, keepdims=True)
    first = N_GROUPS + g_sel * EXPERTS_PER_GROUP
    e_mask = (lane_f >= first) & (lane_f < first + EXPERTS_PER_GROUP)
    el = jnp.where(e_mask, lg, MASK_VALUE)
    v0, i0 = _first_argmax(el, lane_f)
    v1, i1 = _first_argmax(jnp.where(lane_f == i0, MASK_VALUE, el), lane_f)
    e1 = jnp.exp(v1 - v0)
    w0 = g_prob / (1.0 + e1)
    w1 = g_prob * e1 / (1.0 + e1)

    onehots = [lane_f == idx - N_GROUPS for idx in (i0, i1)]

    @pl.when(phase == 0)
    def _():
        base_s[...] = base_s[...] + sum(jnp.sum(oh.astype(F32), axis=0, keepdims=True) for oh in onehots)

    @pl.when(phase == 1)
    def _():
        tm = lg.shape[0]
        tri = (lax.broadcasted_iota(jnp.int32, (tm, tm), 1)
               < lax.broadcasted_iota(jnp.int32, (tm, tm), 0)).astype(BF16)
        dests = []
        for onehot in onehots:
            before = jnp.dot(tri, onehot.astype(BF16), preferred_element_type=F32) + base_s[...]
            dests.append(jnp.sum(jnp.where(onehot, before + start_s[...], 0.0), axis=-1, keepdims=True))
            base_s[...] = base_s[...] + jnp.sum(onehot.astype(F32), axis=0, keepdims=True)
        gate_ref[...] = jnp.where(lane == 0, w0, jnp.where(lane == 1, w1, 0.0))
        cols = jnp.where(lane == 0, dests[0], jnp.where(lane == 1, dests[1], 0.0))
        dest_ref[...] = jnp.transpose(cols)[0:8, :].astype(jnp.int32)


def _invert_kernel(dest_ref, lo_ref, hi_ref, tok_ref, *, n):
    def clear_range(e, c):
        def clear(s, c2):
            tok_ref[s] = 0
            return c2

        lax.fori_loop(lo_ref[e], hi_ref[e], clear, 0)
        return c

    lax.fori_loop(0, lo_ref.shape[0], clear_range, 0)
    for k in range(TOP_K):
        def place(t, c, k=k):
            tok_ref[dest_ref[k * n + t]] = t
            return c

        lax.fori_loop(0, n, place, 0, unroll=8)


def _invert_slots(dest, unused_lo, unused_hi, n, slots):
    assert dest.shape[0] == TOP_K * n
    return pl.pallas_call(
        functools.partial(_invert_kernel, n=n),
        grid_spec=pltpu.PrefetchScalarGridSpec(
            num_scalar_prefetch=3, grid=(1,), in_specs=[],
            out_specs=pl.BlockSpec(memory_space=pltpu.SMEM)),
        out_shape=jax.ShapeDtypeStruct((slots,), jnp.int32),
        compiler_params=_params(("arbitrary",)),
        name="invert_slots",
    )(dest, unused_lo, unused_hi)


def _route(logits, n):
    tm = ROUTE_TM
    one = pl.BlockSpec((1, ROUTER_PAD), lambda p, i: (0, 0))
    gate, dest, counts = pl.pallas_call(
        _route_kernel,
        grid=(2, n // tm),
        in_specs=[pl.BlockSpec((tm, ROUTER_PAD), lambda p, i: (i, 0))],
        out_specs=[pl.BlockSpec((tm, ROUTER_PAD), lambda p, i: (i * p, 0)),
                   pl.BlockSpec((8, tm), lambda p, i: (0, i * p)), one],
        out_shape=[jax.ShapeDtypeStruct((n, ROUTER_PAD), F32),
                   jax.ShapeDtypeStruct((8, n), jnp.int32),
                   jax.ShapeDtypeStruct((1, ROUTER_PAD), F32)],
        scratch_shapes=[pltpu.VMEM((1, ROUTER_PAD), F32), pltpu.VMEM((1, ROUTER_PAD), F32)],
        compiler_params=_params(("arbitrary", "arbitrary")),
        name="route",
    )(logits)
    gate = gate[:, :TOP_K]
    dest = dest[:TOP_K].reshape(TOP_K * n)
    a = n * TOP_K
    counts = counts[0, :N_EXPERTS].astype(jnp.int32)
    pcounts = (counts + MOE_TB - 1) // MOE_TB * MOE_TB
    pends = jnp.cumsum(pcounts)
    nb = a // MOE_TB + N_EXPERTS
    slots = nb * MOE_TB
    unused_lo = jnp.concatenate([pends - pcounts + counts, pends[-1:]])
    unused_hi = jnp.concatenate([pends, jnp.full((1,), slots, jnp.int32)])
    slot_tok = _invert_slots(dest, unused_lo, unused_hi, n, slots)
    block_start = jnp.arange(nb, dtype=jnp.int32) * MOE_TB
    block_e = jnp.minimum(jnp.sum((pends[None, :] <= block_start[:, None]).astype(jnp.int32), axis=1),
                          N_EXPERTS - 1)
    n_active = (pends[-1] // MOE_TB).astype(jnp.int32).reshape(1)
    ids = jnp.arange(N_EXPERTS, dtype=jnp.int32)
    later = (counts[None, :] > 0) & (ids[None, :] > ids[:, None])
    nxt = jnp.min(jnp.where(later, ids[None, :], N_EXPERTS), axis=1)
    nxt = jnp.where(nxt == N_EXPERTS, -1, nxt)
    block_next = jnp.sum(jnp.where(block_e[:, None] == ids[None, :], nxt[None, :], 0), axis=1).astype(jnp.int32)
    return dest, gate, slot_tok, block_e, block_next, n_active


def _cast_weights(src, dst):
    rows, cols = src.shape
    step = CAST_VREGS * 8 * LANES // cols

    def body(r, c):
        sl = pl.ds(pl.multiple_of(r * step, step), step)
        dst[sl, :] = src[sl, :].astype(BF16)
        return c

    lax.fori_loop(0, rows // step, body, 0, unroll=2)


def _expert_kernel(be_ref, nx_ref, nact_ref, tok_ref, h1p_hbm, wg_hbm, wu_hbm, wd_hbm, ys_hbm,
                   xbuf, ybuf, wg_f, wu_f, wd_f, wg_s, wu_s, wd_s, gsem, osem, wsem):
    i = pl.program_id(0)
    slot = i % 2
    xslot = i % GATHER_SLOTS
    nact = nact_ref[0]

    def gather(step, inline=False):
        sl = step % GATHER_SLOTS
        _start_row_gather(tok_ref, step * MOE_TB, 1, MOE_TB, h1p_hbm, xbuf.at[sl], gsem.at[sl], False,
                          inline)

    def out_copies(step, sl):
        return _tile_copies(ys_hbm, step * MOE_TB, MOE_TB, ybuf.at[sl], osem.at[sl], True)

    def weight_copies(e):
        return [pltpu.make_async_copy(src.at[e], dst, wsem)
                for src, dst in ((wg_hbm, wg_f), (wu_hbm, wu_f), (wd_hbm, wd_f))]

    @pl.when((i == 0) & (nact > 0))
    def _():
        for cp in weight_copies(be_ref[0]):
            cp.start(priority=WEIGHT_DMA_PRIORITY)
        gather(0)

    @pl.when((i == 0) & (nact > 1))
    def _():
        gather(1)

    @pl.when(i >= 2)
    def _():
        for cp in out_copies(i - 2, slot):
            cp.wait()

    @pl.when(i < nact)
    def _():
        expert = be_ref[i]

        @pl.when((i == 0) | (expert != be_ref[jnp.maximum(i - 1, 0)]))
        def _():
            for cp in weight_copies(expert):
                cp.wait()
            _cast_weights(wg_f, wg_s)
            _cast_weights(wu_f, wu_s)
            _cast_weights(wd_f, wd_s)

            @pl.when(nx_ref[i] >= 0)
            def _():
                for cp in weight_copies(nx_ref[i]):
                    cp.start(priority=WEIGHT_DMA_PRIORITY)

        _wait_row_gather(xbuf.at[xslot], gsem.at[xslot])

        def block(prefetch):
            if prefetch:
                gather(i + 2, inline=True)
            xb = _unpack_rows(xbuf.at[xslot]).astype(BF16)
            gate = jnp.dot(xb, wg_s[...], preferred_element_type=F32)
            up = jnp.dot(xb, wu_s[...], preferred_element_type=F32)
            hid = (jax.nn.silu(gate) * up).astype(BF16)
            _pack_rows(ybuf.at[slot], jnp.dot(hid, wd_s[...], preferred_element_type=F32))

        pl.when(i + 2 < nact)(functools.partial(block, True))
        pl.when(i + 2 >= nact)(functools.partial(block, False))

    @pl.when(i >= nact)
    def _():
        ybuf[slot] = jnp.zeros(ybuf.shape[1:], jnp.uint32)

    for cp in out_copies(i, slot):
        cp.start()

    @pl.when(i == pl.num_programs(0) - 1)
    def _():
        for cp in out_copies(i, slot):
            cp.wait()

        @pl.when(i >= 1)
        def _():
            for cp in out_copies(i - 1, 1 - slot):
                cp.wait()


def _experts(h1p, slot_tok, block_e, block_next, n_active, w_gate, w_up, w_down):
    d = D_MODEL
    slots = slot_tok.shape[0]
    nb = slots // MOE_TB
    any_spec = pl.BlockSpec(memory_space=pl.ANY)
    grid_spec = pltpu.PrefetchScalarGridSpec(
        num_scalar_prefetch=4,
        grid=(nb,),
        in_specs=[any_spec, any_spec, any_spec, any_spec],
        out_specs=any_spec,
        scratch_shapes=[pltpu.VMEM((GATHER_SLOTS, PACKED, MOE_TB, LANES), jnp.uint32),
                        pltpu.VMEM((2, PACKED, MOE_TB, LANES), jnp.uint32),
                        pltpu.VMEM((d, D_EXPERT), F32), pltpu.VMEM((d, D_EXPERT), F32),
                        pltpu.VMEM((D_EXPERT, d), F32),
                        pltpu.VMEM((d, D_EXPERT), BF16), pltpu.VMEM((d, D_EXPERT), BF16),
                        pltpu.VMEM((D_EXPERT, d), BF16),
                        pltpu.SemaphoreType.DMA((GATHER_SLOTS,)), pltpu.SemaphoreType.DMA((2,)),
                        pltpu.SemaphoreType.DMA(())])
    return pl.pallas_call(
        _expert_kernel,
        grid_spec=grid_spec,
        out_shape=jax.ShapeDtypeStruct((slots, PACKED, LANES), jnp.uint32),
        compiler_params=_params(("arbitrary",)),
        name="experts",
    )(block_e, block_next, n_active, slot_tok, h1p, w_gate, w_up, w_down)


def _final_kernel(dest_ref, ys_hbm, h1_ref, gate_ref, g_ref, b_ref, o_ref, ybuf, gsem):
    i = pl.program_id(0)
    steps = pl.num_programs(0)
    slot = i % GATHER_SLOTS

    def fetch(step, inline=False):
        sl = step % GATHER_SLOTS
        for k in range(TOP_K):
            _start_row_gather(dest_ref, k * steps * FIN_TM + step * FIN_TM, 1, FIN_TM, ys_hbm,
                              ybuf.at[sl, k], gsem.at[sl], True, inline)

    @pl.when(i == 0)
    def _():
        fetch(0)

    @pl.when((i == 0) & (steps > 1))
    def _():
        fetch(1)

    _wait_row_gather(ybuf.at[slot], gsem.at[slot])

    def combine(prefetch):
        if prefetch:
            fetch(i + 2, inline=True)
        gate = gate_ref[...]
        ffn = (_unpack_rows(ybuf.at[slot, 0]) * gate[:, 0:1]
               + _unpack_rows(ybuf.at[slot, 1]) * gate[:, 1:2])
        o_ref[...] = _layer_norm(DN_ALPHA * h1_ref[...] + ffn, g_ref[...], b_ref[...])

    pl.when(i + 2 < steps)(functools.partial(combine, True))
    pl.when(i + 2 >= steps)(functools.partial(combine, False))


def _final(dest, ys, h1, gate, g, b):
    n, d = h1.shape
    grid_spec = pltpu.PrefetchScalarGridSpec(
        num_scalar_prefetch=1,
        grid=(n // FIN_TM,),
        in_specs=[pl.BlockSpec(memory_space=pl.ANY),
                  pl.BlockSpec((FIN_TM, d), lambda i, ds: (i, 0)),
                  pl.BlockSpec((FIN_TM, TOP_K), lambda i, ds: (i, 0)),
                  pl.BlockSpec((1, d), lambda i, ds: (0, 0)),
                  pl.BlockSpec((1, d), lambda i, ds: (0, 0))],
        out_specs=pl.BlockSpec((FIN_TM, d), lambda i, ds: (i, 0)),
        scratch_shapes=[pltpu.VMEM((GATHER_SLOTS, TOP_K, PACKED, FIN_TM, LANES), jnp.uint32),
                        pltpu.SemaphoreType.DMA((GATHER_SLOTS,))])
    return pl.pallas_call(
        _final_kernel,
        grid_spec=grid_spec,
        out_shape=jax.ShapeDtypeStruct((n, d), F32),
        compiler_params=_params(("arbitrary",)),
        name="final",
    )(dest, ys, h1, gate, g.reshape(1, d), b.reshape(1, d))


def kernel(x, ln0_g, ln0_b, w_in, b_in, rpb, w_proj_a, w_proj_b, w_o, b_o, ln1_g, ln1_b,
           w_router_group, b_router_group, w_router_expert, b_router_expert,
           w_gate, w_up, w_down, ln2_g, ln2_b):
    bsz, s, d = x.shape
    n = bsz * s
    assert d == D_MODEL and w_in.shape[0] == DEPTH
    scale = HEAD_DIM ** -0.5

    col_scale = np.ones((w_in.shape[2],), np.float32)
    col_scale[:NA_WIDTH] = scale
    col_scale[3 * NA_WIDTH:3 * NA_WIDTH + DIL_WIDTH] = scale
    w_all = (w_in[0] * col_scale).astype(BF16)
    b_all = (b_in[0] * col_scale).reshape(1, -1)
    assert DIL_OUT_WIDTH == INPROJ_TN
    na_blocks, ngroups = 3 * NA_WIDTH // INPROJ_TN, len(DIL_GROUPS)
    w_r = jnp.concatenate([w_router_group[0], w_router_expert[0]], axis=1)
    w_r = jnp.pad(w_r, ((0, 0), (0, ROUTER_PAD - w_r.shape[1])))
    w_r_hi = w_r.astype(BF16)
    w_r_cat = jnp.concatenate([w_r_hi, (w_r - w_r_hi.astype(F32)).astype(BF16)], axis=1)
    b_r = jnp.pad(jnp.concatenate([b_router_group[0], b_router_expert[0]]),
                  (0, ROUTER_PAD - N_GROUPS - N_EXPERTS)).reshape(1, ROUTER_PAD)
    slopes = _alibi_slopes(DIL_HEADS)

    x2 = x.reshape(n, d)
    hb = _ln0(x2, ln0_g, ln0_b)
    zq = [_inproj(hb, w_all, b_all, bsz, s, DIL_GROUPS[0][1], 3 * NA_WIDTH + 3 * DIL_OUT_WIDTH,
                  lambda j: jnp.where(j < na_blocks, j, na_blocks + (j - na_blocks) * ngroups))]
    for g in range(1, ngroups):
        zq.append(_inproj(hb, w_all, b_all, bsz, s, DIL_GROUPS[g][1], 3 * DIL_OUT_WIDTH,
                          lambda j, g=g: na_blocks + j * ngroups + g))
    na = _neighborhood_attention(zq[0].reshape(-1, bsz, s, HEAD_DIM), rpb[0])
    na = na.reshape(NA_HEADS, n, HEAD_DIM)
    dil = [_dilated_group(zq[g], DIL0_H if g == 0 else 0, g, slopes) for g in range(ngroups)]
    m = _mix(na, [o for o, _ in dil], [l for _, l in dil], hb,
             w_proj_a[0].astype(BF16), w_proj_b[0].astype(BF16), w_all, b_all, s)
    h1, h1p, logits = _outproj(m, x2, ln0_g, ln0_b, w_o[0].astype(BF16), b_o[0], ln1_g[0], ln1_b[0],
                               w_r_cat, b_r)
    dest, gate, slot_tok, block_e, block_next, n_active = _route(logits, n)
    ys = _experts(h1p, slot_tok, block_e, block_next, n_active, w_gate[0], w_up[0], w_down[0])
    out = _final(dest, ys, h1, gate, ln2_g[0], ln2_b[0])
    return out.reshape(bsz, s, d)
```

```python
import functools

import numpy as np
import jax
import jax.numpy as jnp
from jax import lax
from jax.experimental import pallas as pl
from jax.experimental.pallas import tpu as pltpu

F32 = jnp.float32
BF16 = jnp.bfloat16

D_MODEL = 2048
HEAD_DIM = 128
GRID_W = 64
NA_HEADS = 8
NA_KH = 8
NA_KW = 16
DIL_GROUPS = ((128, 1), (512, 4), (2048, 16))
DIL_HEADS_PER_GROUP = 4
DIL_HEADS = DIL_HEADS_PER_GROUP * len(DIL_GROUPS)
N_GROUPS = 8
EXPERTS_PER_GROUP = 8
N_EXPERTS = N_GROUPS * EXPERTS_PER_GROUP
TOP_K = 2
D_EXPERT = D_MODEL // 4
LN_EPS = 1e-5
DEPTH = 1
DN_ALPHA = (2 * DEPTH) ** 0.25
NA_WIDTH = NA_HEADS * HEAD_DIM
DIL_WIDTH = DIL_HEADS * HEAD_DIM
DIL_OUT_WIDTH = DIL_HEADS_PER_GROUP * HEAD_DIM
QKV_WIDTH = 3 * NA_WIDTH + 3 * DIL_WIDTH
QKV_HEADS = QKV_WIDTH // HEAD_DIM
LANES = 128
CHUNKS = D_MODEL // LANES
PACKED = CHUNKS // 2
MASK_VALUE = -1e30

QA_H, KA_H, VA_H = 0, NA_HEADS, 2 * NA_HEADS
DIL0_H = 3 * NA_HEADS

LN_TM = 1024
INPROJ_TM, INPROJ_TN = 2048, 512
NA_QROWS = 4
NA_KROWS = 12
NA_NSUB = 8
DIL_NSUB = 8
DIL_TQ = 256
DIL_HALF = 64
DIL_TK = DIL_TQ + 2 * DIL_HALF
MIX_TM, MIX_TN = 512, 512
MIX_CHUNK = 256
OUT_TM = 512
ROUTER_PAD = 128
ROUTE_TM = 1024
MOE_TB = 128
FIN_TM = 256
CAST_VREGS = 32
GATHER_SLOTS = 3
WEIGHT_DMA_PRIORITY = 1
VMEM_LIMIT = 56 * 1024 * 1024


def _params(sem, limit=VMEM_LIMIT):
    return pltpu.CompilerParams(dimension_semantics=sem, vmem_limit_bytes=limit)


def _layer_norm(x, g, b):
    mu = jnp.mean(x, axis=-1, keepdims=True)
    xc = x - mu
    var = jnp.mean(xc * xc, axis=-1, keepdims=True)
    return xc * lax.rsqrt(var + LN_EPS) * g + b


HIGH_HALF = np.uint32(0xFFFF0000)


def _bf16_bits(x):
    return lax.bitcast_convert_type(x.astype(BF16).astype(F32), jnp.uint32)


def _pack_rows(ref, val):
    for c in range(PACKED):
        lo = _bf16_bits(val[:, c * LANES:(c + 1) * LANES])
        hi = _bf16_bits(val[:, (c + PACKED) * LANES:(c + PACKED + 1) * LANES])
        ref[c] = (lo >> 16) | (hi & HIGH_HALF)


def _unpack_rows(ref):
    words = [ref[c] for c in range(PACKED)]
    lo = [lax.bitcast_convert_type(w << 16, F32) for w in words]
    hi = [lax.bitcast_convert_type(w & HIGH_HALF, F32) for w in words]
    return jnp.concatenate(lo + hi, axis=-1)


def _tile_copies(hbm, row0, rows, vmem, sem, to_hbm, chunk_major=False):
    copies = []
    for c in range(PACKED):
        h = hbm.at[c, pl.ds(row0, rows)] if chunk_major else hbm.at[pl.ds(row0, rows), c]
        v = vmem.at[c]
        copies.append(pltpu.make_async_copy(v, h, sem) if to_hbm else pltpu.make_async_copy(h, v, sem))
    return copies


def _start_row_gather(idx_ref, base, step, count, src_hbm, dst, sem, both_threads, inline=False,
                      chunk_major=False):
    def start(j, c):
        for half in range(2):
            t = 2 * j + half
            row = idx_ref[base + t * step]
            src = src_hbm.at[:, row] if chunk_major else src_hbm.at[row]
            pltpu.make_async_copy(src, dst.at[:, t], sem).start(priority=half if both_threads else 0)
        return c

    if inline:
        for j in range(count // 2):
            start(j, 0)
    else:
        lax.fori_loop(0, count // 2, start, 0, unroll=2)


def _wait_row_gather(dst, sem):
    pltpu.make_async_copy(dst, dst, sem).wait()


def _ln0_kernel(x_ref, g_ref, b_ref, hb_ref):
    hb_ref[...] = _layer_norm(x_ref[...], g_ref[...], b_ref[...]).astype(BF16)


def _ln0(x, g, b):
    n, d = x.shape
    row = pl.BlockSpec((LN_TM, d), lambda i: (i, 0))
    vec = pl.BlockSpec((1, d), lambda i: (0, 0))
    return pl.pallas_call(
        _ln0_kernel,
        grid=(n // LN_TM,),
        in_specs=[row, vec, vec],
        out_specs=row,
        out_shape=jax.ShapeDtypeStruct((n, d), BF16),
        compiler_params=_params(("parallel",)),
        name="ln0",
    )(x, g.reshape(1, d), b.reshape(1, d))


def _inproj_kernel(hb_ref, w_ref, b_ref, o_ref, acc_s, *, dil):
    acc = jnp.dot(hb_ref[...], w_ref[...], preferred_element_type=F32) + b_ref[...]
    if dil == 1:
        for c in range(INPROJ_TN // LANES):
            o_ref[c, 0, 0] = acc[:, c * LANES:(c + 1) * LANES].astype(BF16)
    else:
        for c in range(INPROJ_TN // LANES):
            acc_s[c] = acc[:, c * LANES:(c + 1) * LANES]
        for c in range(INPROJ_TN // LANES):
            for r in range(dil):
                o_ref[c, 0, r] = acc_s[c, pl.ds(r, INPROJ_TM // dil, stride=dil), :].astype(BF16)


def _inproj(hb, w, b, bsz, s, dil, width, col_block):
    n, d = hb.shape
    tiles = s // INPROJ_TM
    assert INPROJ_TM % (16 * dil) == 0 and s % INPROJ_TM == 0 and width % INPROJ_TN == 0
    return pl.pallas_call(
        functools.partial(_inproj_kernel, dil=dil),
        grid=(n // INPROJ_TM, width // INPROJ_TN),
        in_specs=[pl.BlockSpec((INPROJ_TM, d), lambda i, j: (i, 0)),
                  pl.BlockSpec((d, INPROJ_TN), lambda i, j: (0, col_block(j))),
                  pl.BlockSpec((1, INPROJ_TN), lambda i, j: (0, col_block(j)))],
        out_specs=pl.BlockSpec((INPROJ_TN // LANES, 1, dil, INPROJ_TM // dil, LANES),
                               lambda i, j: (j, i // tiles, 0, i % tiles, 0)),
        out_shape=jax.ShapeDtypeStruct((width // LANES, bsz, dil, s // dil, LANES), BF16),
        scratch_shapes=[pltpu.VMEM((INPROJ_TN // LANES, INPROJ_TM, LANES), F32)],
        compiler_params=_params(("parallel", "arbitrary")),
        name=f"inproj_{dil}",
    )(hb, w, b)


def _na_bias_tiles(rpb, rows):
    heads, n_dr, n_dc = rpb.shape
    qc = np.arange(GRID_W)[:, None]
    kc = np.arange(GRID_W)[None, :]
    qcs = np.clip(qc - NA_KW // 2, 0, GRID_W - NA_KW)
    v_col = (kc >= qcs) & (kc < qcs + NA_KW)
    dc = np.clip(kc - qc + NA_KW - 1, 0, n_dc - 1)
    onehot = (dc[None] == np.arange(n_dc)[:, None, None]) & v_col[None]
    toep = jnp.einsum('hrd,dqk->hrqk', rpb.astype(F32), jnp.asarray(onehot, F32),
                      precision=lax.Precision.HIGHEST)
    toep = jnp.where(jnp.asarray(v_col), toep, MASK_VALUE)
    masked = jnp.full((heads, 1, GRID_W, GRID_W), MASK_VALUE, F32)
    blocks = jnp.concatenate([toep, masked], axis=1)
    i = np.arange(NA_QROWS)[:, None]
    j = np.arange(NA_KROWS)[None, :]
    sel = []
    for r0, ks in ((0, 0), (2 * NA_QROWS, 2 * NA_QROWS - NA_KH // 2), (rows - NA_QROWS, rows - NA_KROWS)):
        r, krow = r0 + i, ks + j
        start = np.clip(r - NA_KH // 2, 0, rows - NA_KH)
        v_row = (krow >= start) & (krow < start + NA_KH)
        sel.append(np.where(v_row, krow - r + NA_KH - 1, n_dr))
    sel = np.stack(sel)
    pairs = sel.reshape(3, NA_QROWS, NA_KROWS // 2, 2)
    distinct = sorted({tuple(p) for p in pairs.reshape(-1, 2).tolist()})
    plan = [[[distinct.index(tuple(pairs[v, i, c])) for c in range(NA_KROWS // 2)]
             for i in range(NA_QROWS)] for v in range(3)]
    left = jnp.take(blocks, jnp.asarray([p[0] for p in distinct], jnp.int32), axis=1)
    right = jnp.take(blocks, jnp.asarray([p[1] for p in distinct], jnp.int32), axis=1)
    return jnp.concatenate([left, right], axis=-1), plan


def _attend_blocks(nblocks, scores, finish):
    scored, weighted = {}, {}
    for t in range(nblocks + 2):
        if t < nblocks:
            scored[t] = scores(t)
        if 0 <= t - 1 < nblocks:
            s, vw = scored.pop(t - 1)
            m = jnp.max(s, axis=-1, keepdims=True)
            p = jnp.exp(s - m)
            weighted[t - 1] = (p.astype(BF16), vw, m, jnp.sum(p, axis=-1, keepdims=True))
        if 0 <= t - 2 < nblocks:
            p, vw, m, l = weighted.pop(t - 2)
            finish(t - 2, jnp.dot(p, vw, preferred_element_type=F32) / l, m, l)


def _scores(q, kw, bias):
    return lax.dot_general(q, kw, (((1,), (1,)), ((), ())), preferred_element_type=F32) + bias


def _edge_variant(blk, nblk):
    return jnp.where(blk == 0, 0, jnp.where(blk == nblk - 1, 2, 1))


def _na_kernel(q_ref, k_ref, v_ref, tile_ref, o_ref, tab_ref, *, rows, plan):
    tq = NA_QROWS * GRID_W

    @pl.when(pl.program_id(2) == 0)
    def _():
        for v in range(3):
            for i in range(NA_QROWS):
                for c in range(NA_KROWS // 2):
                    tab_ref[v, 0, i * GRID_W:(i + 1) * GRID_W, c * 2 * GRID_W:(c + 1) * 2 * GRID_W] = (
                        tile_ref[0, plan[v][i][c]])

    def scores(u):
        blk = pl.program_id(2) * NA_NSUB + u
        ks = jnp.clip(blk * NA_QROWS - NA_KH // 2, 0, rows - NA_KROWS)
        off = pl.multiple_of(ks * GRID_W, GRID_W)
        kw = k_ref[0, 0, pl.ds(off, NA_KROWS * GRID_W), :]
        vw = v_ref[0, 0, pl.ds(off, NA_KROWS * GRID_W), :]
        bias = tab_ref[_edge_variant(blk, rows // NA_QROWS), 0]
        return _scores(q_ref[0, 0, u * tq:(u + 1) * tq, :], kw, bias), vw

    def finish(u, o, m, l):
        o_ref[0, 0, u * tq:(u + 1) * tq, :] = o.astype(BF16)

    _attend_blocks(NA_NSUB, scores, finish)


def _neighborhood_attention(zq4, rpb):
    _, bsz, s, _ = zq4.shape
    rows = s // GRID_W
    assert rows % (NA_QROWS * NA_NSUB) == 0 and rows >= NA_KROWS + NA_QROWS and NA_KROWS % 2 == 0
    tiles, plan = _na_bias_tiles(rpb, rows)
    tq = NA_QROWS * GRID_W * NA_NSUB
    return pl.pallas_call(
        functools.partial(_na_kernel, rows=rows, plan=plan),
        grid=(bsz, NA_HEADS, s // tq),
        in_specs=[pl.BlockSpec((1, 1, tq, HEAD_DIM), lambda b, h, i: (QA_H + h, b, i, 0)),
                  pl.BlockSpec((1, 1, s, HEAD_DIM), lambda b, h, i: (KA_H + h, b, 0, 0)),
                  pl.BlockSpec((1, 1, s, HEAD_DIM), lambda b, h, i: (VA_H + h, b, 0, 0)),
                  pl.BlockSpec((1,) + tiles.shape[1:], lambda b, h, i: (h, 0, 0, 0))],
        out_specs=pl.BlockSpec((1, 1, tq, HEAD_DIM), lambda b, h, i: (h, b, i, 0)),
        out_shape=jax.ShapeDtypeStruct((NA_HEADS, bsz, s, HEAD_DIM), BF16),
        scratch_shapes=[pltpu.VMEM((3, 1, NA_QROWS * GRID_W, NA_KROWS * GRID_W), F32)],
        compiler_params=_params(("arbitrary", "arbitrary", "arbitrary")),
        name="na_attn",
    )(zq4, zq4, zq4, tiles)


def _alibi_slopes(n):
    return np.array([2.0 ** (-8.0 * (i + 1) / n) for i in range(n)], dtype=np.float32)


def _dil_bias_tables(slopes, dil):
    qi = np.arange(DIL_TQ)[:, None]
    kj = np.arange(DIL_TK)[None, :]
    dist = np.stack([np.abs(kj - qi + shift) for shift in (0, -DIL_HALF, -2 * DIL_HALF)])
    dist = jnp.asarray(dist, F32)[:, None]
    penalty = jnp.asarray(slopes, F32)[None, :, None, None] * (dist * float(dil))
    return jnp.where(dist <= DIL_HALF, -penalty, MASK_VALUE)


def _dil_kernel(q_ref, k_ref, v_ref, tab_ref, o_ref, lse_ref, *, length, nsub, nres):
    def scores(u):
        res, sub = divmod(u, nsub)
        blk = pl.program_id(3) * nsub + sub
        ks = jnp.clip(blk * DIL_TQ - DIL_HALF, 0, length - DIL_TK)
        off = pl.multiple_of(ks, DIL_HALF)
        kw = k_ref[0, 0, res, pl.ds(off, DIL_TK), :]
        vw = v_ref[0, 0, res, pl.ds(off, DIL_TK), :]
        bias = tab_ref[_edge_variant(blk, length // DIL_TQ), 0]
        return _scores(q_ref[0, 0, res, sub * DIL_TQ:(sub + 1) * DIL_TQ, :], kw, bias), vw

    def finish(u, o, m, l):
        res, sub = divmod(u, nsub)
        rows = slice(sub * DIL_TQ, (sub + 1) * DIL_TQ)
        o_ref[0, 0, res, rows, :] = o.astype(BF16)
        lse_ref[0, 0, res, rows, :] = jnp.broadcast_to(m + jnp.log(l), (DIL_TQ, LANES))

    _attend_blocks(nres * nsub, scores, finish)


def _dilated_group(zq, base, g, slopes):
    window, dil = DIL_GROUPS[g]
    assert (window // 2) // dil == DIL_HALF
    _, bsz, _, length, _ = zq.shape
    nsub = min(DIL_NSUB, length // DIL_TQ)
    nres = min(dil, DIL_NSUB // nsub)
    assert zq.shape[2] == dil and length % (DIL_TQ * nsub) == 0 and length >= DIL_TK and dil % nres == 0
    gh = DIL_HEADS_PER_GROUP
    tabs = _dil_bias_tables(slopes[g * gh:(g + 1) * gh], dil)
    tq = DIL_TQ * nsub
    qspec = pl.BlockSpec((1, 1, nres, tq, HEAD_DIM), lambda b, h, r, n: (base + h, b, r, n, 0))
    kvspec = lambda off: pl.BlockSpec((1, 1, nres, length, HEAD_DIM),
                                      lambda b, h, r, n: (base + off + h, b, r, 0, 0))
    ospec = pl.BlockSpec((1, 1, nres, tq, HEAD_DIM), lambda b, h, r, n: (h, b, r, n, 0))
    return pl.pallas_call(
        functools.partial(_dil_kernel, length=length, nsub=nsub, nres=nres),
        grid=(bsz, gh, dil // nres, length // tq),
        in_specs=[qspec, kvspec(gh), kvspec(2 * gh),
                  pl.BlockSpec((3, 1, DIL_TQ, DIL_TK), lambda b, h, r, n: (0, h, 0, 0))],
        out_specs=[ospec, ospec],
        out_shape=[jax.ShapeDtypeStruct((gh, bsz, dil, length, HEAD_DIM), BF16),
                   jax.ShapeDtypeStruct((gh, bsz, dil, length, HEAD_DIM), F32)],
        compiler_params=_params(("parallel", "parallel", "parallel", "arbitrary")),
        name=f"dil_attn_{dil}",
    )(zq, zq, zq, tabs)


def _mix_kernel(na_ref, o1_ref, o2_ref, o3_ref, l1_ref, l2_ref, l3_ref, hb_ref,
                wpa_ref, wpb_ref, wga_ref, wgb_ref, bga_ref, bgb_ref, m_ref, na_s, dil_s, o_s, l_s):
    @pl.when(pl.program_id(1) == 0)
    def _():
        for h in range(NA_HEADS):
            na_s[:, h * HEAD_DIM:(h + 1) * HEAD_DIM] = na_ref[h]
        for g, (o_ref, l_ref) in enumerate(((o1_ref, l1_ref), (o2_ref, l2_ref), (o3_ref, l3_ref))):
            dil = DIL_GROUPS[g][1]
            for h in range(DIL_HEADS_PER_GROUP):
                for r in range(dil):
                    rows = slice(None) if dil == 1 else pl.ds(r, MIX_TM // dil, stride=dil)
                    o_s[g, h, rows, :] = o_ref[h, 0, r].astype(F32)
                    l_s[g, h, rows, :] = l_ref[h, 0, r]
        for h in range(DIL_HEADS_PER_GROUP):
            ls = [l_s[g, h] for g in range(len(DIL_GROUPS))]
            mx = jnp.maximum(jnp.maximum(ls[0], ls[1]), ls[2])
            es = [jnp.exp(l - mx) for l in ls]
            den = es[0] + es[1] + es[2]
            acc = (es[0] * o_s[0, h] + es[1] * o_s[1, h] + es[2] * o_s[2, h]) / den
            dil_s[:, h * HEAD_DIM:(h + 1) * HEAD_DIM] = acc.astype(BF16)

    hb = hb_ref[...]
    na, dm = na_s[...], dil_s[...]

    def products(c):
        cols = slice(c * MIX_CHUNK, (c + 1) * MIX_CHUNK)
        return (jnp.dot(na, wpa_ref[:, cols], preferred_element_type=F32),
                jnp.dot(dm, wpb_ref[:, cols], preferred_element_type=F32),
                jnp.dot(hb, wga_ref[:, cols], preferred_element_type=F32) + bga_ref[:, cols],
                jnp.dot(hb, wgb_ref[:, cols], preferred_element_type=F32) + bgb_ref[:, cols])

    nchunks = MIX_TN // MIX_CHUNK
    pending = products(0)
    for c in range(nchunks):
        following = products(c + 1) if c + 1 < nchunks else None
        ya, yb, ga, gb = pending
        m_ref[:, c * MIX_CHUNK:(c + 1) * MIX_CHUNK] = (
            jax.nn.sigmoid(ga) * ya + jax.nn.sigmoid(gb) * yb).astype(BF16)
        pending = following


def _mix(na, dil_o, dil_lse, hb, wpa, wpb, w_all, b_all, s):
    n, d = hb.shape
    tm, tn = MIX_TM, MIX_TN
    tiles = s // tm
    gh = DIL_HEADS_PER_GROUP
    ngroups = len(DIL_GROUPS)
    assert all(tm % (16 * dil) == 0 for _, dil in DIL_GROUPS) and s % tm == 0
    col = lambda k: pl.BlockSpec((k, tn), lambda i, j: (0, j))
    ga0, gb0 = QKV_WIDTH // tn, (QKV_WIDTH + d) // tn
    gate = lambda k, first: pl.BlockSpec((k, tn), lambda i, j: (0, first + j))
    grp = lambda dil: pl.BlockSpec((gh, 1, dil, tm // dil, HEAD_DIM),
                                   lambda i, j: (0, i // tiles, 0, i % tiles, 0))
    groups = [grp(dil) for _, dil in DIL_GROUPS]
    return pl.pallas_call(
        _mix_kernel,
        grid=(n // tm, d // tn),
        in_specs=[pl.BlockSpec((NA_HEADS, tm, HEAD_DIM), lambda i, j: (0, i, 0)), *groups, *groups,
                  pl.BlockSpec((tm, d), lambda i, j: (i, 0)),
                  col(NA_WIDTH), col(DIL_OUT_WIDTH),
                  gate(d, ga0), gate(d, gb0), gate(1, ga0), gate(1, gb0)],
        out_specs=pl.BlockSpec((tm, tn), lambda i, j: (i, j)),
        out_shape=jax.ShapeDtypeStruct((n, d), BF16),
        scratch_shapes=[pltpu.VMEM((tm, NA_WIDTH), BF16), pltpu.VMEM((tm, DIL_OUT_WIDTH), BF16),
                        pltpu.VMEM((ngroups, gh, tm, HEAD_DIM), F32),
                        pltpu.VMEM((ngroups, gh, tm, HEAD_DIM), F32)],
        compiler_params=_params(("parallel", "arbitrary")),
        name="mix",
    )(na, *dil_o, *dil_lse, hb, wpa, wpb, w_all, w_all, b_all, b_all)


def _outproj_kernel(m_ref, x_ref, g0_ref, b0_ref, wo_ref, bo_ref, g_ref, b_ref, wr_ref, br_ref,
                    h1_ref, h1p_hbm, lg_ref, hbuf, sem):
    i = pl.program_id(0)
    slot = i % 2

    def out_copies(step, sl):
        return _tile_copies(h1p_hbm, step * OUT_TM, OUT_TM, hbuf.at[sl], sem.at[sl], True)

    mix = jnp.dot(m_ref[...], wo_ref[...], preferred_element_type=F32) + bo_ref[...]
    h = _layer_norm(x_ref[...], g0_ref[...], b0_ref[...])
    h1 = _layer_norm(DN_ALPHA * h + mix, g_ref[...], b_ref[...])
    h1_ref[...] = h1

    @pl.when(i >= 2)
    def _():
        for cp in out_copies(i - 2, slot):
            cp.wait()

    _pack_rows(hbuf.at[slot], h1)
    for cp in out_copies(i, slot):
        cp.start()

    @pl.when(i == pl.num_programs(0) - 1)
    def _():
        for cp in out_copies(i, slot):
            cp.wait()

        @pl.when(i >= 1)
        def _():
            for cp in out_copies(i - 1, 1 - slot):
                cp.wait()

    hi = h1.astype(BF16)
    lo = (h1 - hi.astype(F32)).astype(BF16)
    both = jnp.dot(hi, wr_ref[...], preferred_element_type=F32)
    lg = (both[:, :ROUTER_PAD] + both[:, ROUTER_PAD:]
          + jnp.dot(lo, wr_ref[:, :ROUTER_PAD], preferred_element_type=F32))
    lg_ref[...] = lg + br_ref[...]


def _outproj(m, x, g0, b0, wo, bo, g, b, wr, br):
    n, d = x.shape
    tm = OUT_TM
    row = lambda w: pl.BlockSpec((tm, w), lambda i: (i, 0))
    full = lambda r, c: pl.BlockSpec((r, c), lambda i: (0, 0), pipeline_mode=pl.Buffered(1))
    return pl.pallas_call(
        _outproj_kernel,
        grid=(n // tm,),
        in_specs=[row(d), row(d), full(1, d), full(1, d), full(d, d), full(1, d), full(1, d), full(1, d),
                  full(d, 2 * ROUTER_PAD), full(1, ROUTER_PAD)],
        out_specs=[row(d), pl.BlockSpec(memory_space=pl.ANY), row(ROUTER_PAD)],
        out_shape=[jax.ShapeDtypeStruct((n, d), F32),
                   jax.ShapeDtypeStruct((n, PACKED, LANES), jnp.uint32),
                   jax.ShapeDtypeStruct((n, ROUTER_PAD), F32)],
        scratch_shapes=[pltpu.VMEM((2, PACKED, tm, LANES), jnp.uint32), pltpu.SemaphoreType.DMA((2,))],
        compiler_params=_params(("arbitrary",)),
        name="outproj",
    )(m, x, g0.reshape(1, d), b0.reshape(1, d), wo, bo.reshape(1, d), g.reshape(1, d), b.reshape(1, d),
      wr, br)


def _first_argmax(vals, lane_f):
    top = jnp.max(vals, axis=-1, keepdims=True)
    idx = jnp.min(jnp.where(vals == top, lane_f, float(LANES)), axis=-1, keepdims=True)
    return top, idx


def _route_kernel(lg_ref, gate_ref, dest_ref, cnt_ref, base_s, start_s):
    phase = pl.program_id(0)

    @pl.when(pl.program_id(1) == 0)
    def _():
        @pl.when(phase == 1)
        def _():
            counts = base_s[...]
            cnt_ref[...] = counts
            blocks = jnp.ceil(counts * (1.0 / MOE_TB))
            before = (lax.broadcasted_iota(jnp.int32, (ROUTER_PAD, ROUTER_PAD), 0)
                      < lax.broadcasted_iota(jnp.int32, (ROUTER_PAD, ROUTER_PAD), 1)).astype(BF16)
            starts = jnp.dot(jnp.broadcast_to(blocks, (8, ROUTER_PAD)).astype(BF16), before,
                             preferred_element_type=F32)
            start_s[...] = starts[0:1] * float(MOE_TB)

        base_s[...] = jnp.zeros(base_s.shape, F32)

    lg = lg_ref[...]
    lane = lax.broadcasted_iota(jnp.int32, lg.shape, 1)
    lane_f = lane.astype(F32)
    g_mask = lane < N_GROUPS
    g_top, g_sel = _first_argmax(jnp.where(g_mask, lg, MASK_VALUE), lane_f)
    g_prob = 1.0 / jnp.sum(jnp.where(g_mask, jnp.exp(lg - g_top), 0.0), axis=-1, keepdims=True)
    first = N_GROUPS + g_sel * EXPERTS_PER_GROUP
    e_mask = (lane_f >= first) & (lane_f < first + EXPERTS_PER_GROUP)
    el = jnp.where(e_mask, lg, MASK_VALUE)
    v0, i0 = _first_argmax(el, lane_f)
    v1, i1 = _first_argmax(jnp.where(lane_f == i0, MASK_VALUE, el), lane_f)
    e1 = jnp.exp(v1 - v0)
    w0 = g_prob / (1.0 + e1)
    w1 = g_prob * e1 / (1.0 + e1)

    onehots = [lane_f == idx - N_GROUPS for idx in (i0, i1)]

    @pl.when(phase == 0)
    def _():
        base_s[...] = base_s[...] + sum(jnp.sum(oh.astype(F32), axis=0, keepdims=True) for oh in onehots)

    @pl.when(phase == 1)
    def _():
        tm = lg.shape[0]
        tri = (lax.broadcasted_iota(jnp.int32, (tm, tm), 1)
               < lax.broadcasted_iota(jnp.int32, (tm, tm), 0)).astype(BF16)
        dests = []
        for onehot in onehots:
            before = jnp.dot(tri, onehot.astype(BF16), preferred_element_type=F32) + base_s[...]
            dests.append(jnp.sum(jnp.where(onehot, before + start_s[...], 0.0), axis=-1, keepdims=True))
            base_s[...] = base_s[...] + jnp.sum(onehot.astype(F32), axis=0, keepdims=True)
        gate_ref[...] = jnp.where(lane == 0, w0, jnp.where(lane == 1, w1, 0.0))
        cols = jnp.where(lane == 0, dests[0], jnp.where(lane == 1, dests[1], 0.0))
        dest_ref[...] = jnp.transpose(cols)[0:8, :].astype(jnp.int32)


def _invert_kernel(dest_ref, lo_ref, hi_ref, tok_ref, *, n):
    def clear_range(e, c):
        def clear(s, c2):
            tok_ref[s] = 0
            return c2

        lax.fori_loop(lo_ref[e], hi_ref[e], clear, 0)
        return c

    lax.fori_loop(0, lo_ref.shape[0], clear_range, 0)
    for k in range(TOP_K):
        def place(t, c, k=k):
            tok_ref[dest_ref[k * n + t]] = t
            return c

        lax.fori_loop(0, n, place, 0, unroll=8)


def _invert_slots(dest, unused_lo, unused_hi, n, slots):
    assert dest.shape[0] == TOP_K * n
    return pl.pallas_call(
        functools.partial(_invert_kernel, n=n),
        grid_spec=pltpu.PrefetchScalarGridSpec(
            num_scalar_prefetch=3, grid=(1,), in_specs=[],
            out_specs=pl.BlockSpec(memory_space=pltpu.SMEM)),
        out_shape=jax.ShapeDtypeStruct((slots,), jnp.int32),
        compiler_params=_params(("arbitrary",)),
        name="invert_slots",
    )(dest, unused_lo, unused_hi)


def _route(logits, n):
    tm = ROUTE_TM
    one = pl.BlockSpec((1, ROUTER_PAD), lambda p, i: (0, 0))
    gate, dest, counts = pl.pallas_call(
        _route_kernel,
        grid=(2, n // tm),
        in_specs=[pl.BlockSpec((tm, ROUTER_PAD), lambda p, i: (i, 0))],
        out_specs=[pl.BlockSpec((tm, ROUTER_PAD), lambda p, i: (i * p, 0)),
                   pl.BlockSpec((8, tm), lambda p, i: (0, i * p)), one],
        out_shape=[jax.ShapeDtypeStruct((n, ROUTER_PAD), F32),
                   jax.ShapeDtypeStruct((8, n), jnp.int32),
                   jax.ShapeDtypeStruct((1, ROUTER_PAD), F32)],
        scratch_shapes=[pltpu.VMEM((1, ROUTER_PAD), F32), pltpu.VMEM((1, ROUTER_PAD), F32)],
        compiler_params=_params(("arbitrary", "arbitrary")),
        name="route",
    )(logits)
    gate = gate[:, :TOP_K]
    dest = dest[:TOP_K].reshape(TOP_K * n)
    a = n * TOP_K
    counts = counts[0, :N_EXPERTS].astype(jnp.int32)
    pcounts = (counts + MOE_TB - 1) // MOE_TB * MOE_TB
    pends = jnp.cumsum(pcounts)
    nb = a // MOE_TB + N_EXPERTS
    slots = nb * MOE_TB
    unused_lo = jnp.concatenate([pends - pcounts + counts, pends[-1:]])
    unused_hi = jnp.concatenate([pends, jnp.full((1,), slots, jnp.int32)])
    slot_tok = _invert_slots(dest, unused_lo, unused_hi, n, slots)
    block_start = jnp.arange(nb, dtype=jnp.int32) * MOE_TB
    block_e = jnp.minimum(jnp.sum((pends[None, :] <= block_start[:, None]).astype(jnp.int32), axis=1),
                          N_EXPERTS - 1)
    n_active = (pends[-1] // MOE_TB).astype(jnp.int32).reshape(1)
    ids = jnp.arange(N_EXPERTS, dtype=jnp.int32)
    later = (counts[None, :] > 0) & (ids[None, :] > ids[:, None])
    nxt = jnp.min(jnp.where(later, ids[None, :], N_EXPERTS), axis=1)
    nxt = jnp.where(nxt == N_EXPERTS, -1, nxt)
    block_next = jnp.sum(jnp.where(block_e[:, None] == ids[None, :], nxt[None, :], 0), axis=1).astype(jnp.int32)
    return dest, gate, slot_tok, block_e, block_next, n_active


def _cast_weights(src, dst):
    rows, cols = src.shape
    step = CAST_VREGS * 8 * LANES // cols

    def body(r, c):
        sl = pl.ds(pl.multiple_of(r * step, step), step)
        dst[sl, :] = src[sl, :].astype(BF16)
        return c

    lax.fori_loop(0, rows // step, body, 0, unroll=2)


def _expert_kernel(be_ref, nx_ref, nact_ref, tok_ref, h1p_hbm, wg_hbm, wu_hbm, wd_hbm, ys_hbm,
                   xbuf, ybuf, wg_f, wu_f, wd_f, wg_s, wu_s, wd_s, gsem, osem, wsem):
    i = pl.program_id(0)
    slot = i % 2
    xslot = i % GATHER_SLOTS
    nact = nact_ref[0]

    def gather(step, inline=False):
        sl = step % GATHER_SLOTS
        _start_row_gather(tok_ref, step * MOE_TB, 1, MOE_TB, h1p_hbm, xbuf.at[sl], gsem.at[sl], False,
                          inline)

    def out_copies(step, sl):
        return _tile_copies(ys_hbm, step * MOE_TB, MOE_TB, ybuf.at[sl], osem.at[sl], True, chunk_major=True)

    def weight_copies(e):
        return [pltpu.make_async_copy(src.at[e], dst, wsem)
                for src, dst in ((wg_hbm, wg_f), (wu_hbm, wu_f), (wd_hbm, wd_f))]

    @pl.when((i == 0) & (nact > 0))
    def _():
        for cp in weight_copies(be_ref[0]):
            cp.start(priority=WEIGHT_DMA_PRIORITY)
        gather(0)

    @pl.when((i == 0) & (nact > 1))
    def _():
        gather(1)

    @pl.when(i >= 2)
    def _():
        for cp in out_copies(i - 2, slot):
            cp.wait()

    @pl.when(i < nact)
    def _():
        expert = be_ref[i]

        @pl.when((i == 0) | (expert != be_ref[jnp.maximum(i - 1, 0)]))
        def _():
            for cp in weight_copies(expert):
                cp.wait()
            _cast_weights(wg_f, wg_s)
            _cast_weights(wu_f, wu_s)
            _cast_weights(wd_f, wd_s)

            @pl.when(nx_ref[i] >= 0)
            def _():
                for cp in weight_copies(nx_ref[i]):
                    cp.start(priority=WEIGHT_DMA_PRIORITY)

        _wait_row_gather(xbuf.at[xslot], gsem.at[xslot])

        def block(prefetch):
            if prefetch:
                gather(i + 2, inline=True)
            xb = _unpack_rows(xbuf.at[xslot]).astype(BF16)
            gate = jnp.dot(xb, wg_s[...], preferred_element_type=F32)
            up = jnp.dot(xb, wu_s[...], preferred_element_type=F32)
            hid = (jax.nn.silu(gate) * up).astype(BF16)
            _pack_rows(ybuf.at[slot], jnp.dot(hid, wd_s[...], preferred_element_type=F32))

        pl.when(i + 2 < nact)(functools.partial(block, True))
        pl.when(i + 2 >= nact)(functools.partial(block, False))

    @pl.when(i >= nact)
    def _():
        ybuf[slot] = jnp.zeros(ybuf.shape[1:], jnp.uint32)

    for cp in out_copies(i, slot):
        cp.start()

    @pl.when(i == pl.num_programs(0) - 1)
    def _():
        for cp in out_copies(i, slot):
            cp.wait()

        @pl.when(i >= 1)
        def _():
            for cp in out_copies(i - 1, 1 - slot):
                cp.wait()


def _experts(h1p, slot_tok, block_e, block_next, n_active, w_gate, w_up, w_down):
    d = D_MODEL
    slots = slot_tok.shape[0]
    nb = slots // MOE_TB
    any_spec = pl.BlockSpec(memory_space=pl.ANY)
    grid_spec = pltpu.PrefetchScalarGridSpec(
        num_scalar_prefetch=4,
        grid=(nb,),
        in_specs=[any_spec, any_spec, any_spec, any_spec],
        out_specs=any_spec,
        scratch_shapes=[pltpu.VMEM((GATHER_SLOTS, PACKED, MOE_TB, LANES), jnp.uint32),
                        pltpu.VMEM((2, PACKED, MOE_TB, LANES), jnp.uint32),
                        pltpu.VMEM((d, D_EXPERT), F32), pltpu.VMEM((d, D_EXPERT), F32),
                        pltpu.VMEM((D_EXPERT, d), F32),
                        pltpu.VMEM((d, D_EXPERT), BF16), pltpu.VMEM((d, D_EXPERT), BF16),
                        pltpu.VMEM((D_EXPERT, d), BF16),
                        pltpu.SemaphoreType.DMA((GATHER_SLOTS,)), pltpu.SemaphoreType.DMA((2,)),
                        pltpu.SemaphoreType.DMA(())])
    return pl.pallas_call(
        _expert_kernel,
        grid_spec=grid_spec,
        out_shape=jax.ShapeDtypeStruct((PACKED, slots, LANES), jnp.uint32),
        compiler_params=_params(("arbitrary",)),
        name="experts",
    )(block_e, block_next, n_active, slot_tok, h1p, w_gate, w_up, w_down)


def _final_kernel(dest_ref, ys_hbm, h1_ref, gate_ref, g_ref, b_ref, o_ref, ybuf, gsem):
    i = pl.program_id(0)
    steps = pl.num_programs(0)
    slot = i % GATHER_SLOTS

    def fetch(step, inline=False):
        sl = step % GATHER_SLOTS
        for k in range(TOP_K):
            _start_row_gather(dest_ref, k * steps * FIN_TM + step * FIN_TM, 1, FIN_TM, ys_hbm,
                              ybuf.at[sl, k], gsem.at[sl], True, inline, chunk_major=True)

    @pl.when(i == 0)
    def _():
        fetch(0)

    @pl.when((i == 0) & (steps > 1))
    def _():
        fetch(1)

    _wait_row_gather(ybuf.at[slot], gsem.at[slot])

    def combine(prefetch):
        if prefetch:
            fetch(i + 2, inline=True)
        gate = gate_ref[...]
        ffn = (_unpack_rows(ybuf.at[slot, 0]) * gate[:, 0:1]
               + _unpack_rows(ybuf.at[slot, 1]) * gate[:, 1:2])
        o_ref[...] = _layer_norm(DN_ALPHA * h1_ref[...] + ffn, g_ref[...], b_ref[...])

    pl.when(i + 2 < steps)(functools.partial(combine, True))
    pl.when(i + 2 >= steps)(functools.partial(combine, False))


def _final(dest, ys, h1, gate, g, b):
    n, d = h1.shape
    grid_spec = pltpu.PrefetchScalarGridSpec(
        num_scalar_prefetch=1,
        grid=(n // FIN_TM,),
        in_specs=[pl.BlockSpec(memory_space=pl.ANY),
                  pl.BlockSpec((FIN_TM, d), lambda i, ds: (i, 0)),
                  pl.BlockSpec((FIN_TM, TOP_K), lambda i, ds: (i, 0)),
                  pl.BlockSpec((1, d), lambda i, ds: (0, 0)),
                  pl.BlockSpec((1, d), lambda i, ds: (0, 0))],
        out_specs=pl.BlockSpec((FIN_TM, d), lambda i, ds: (i, 0)),
        scratch_shapes=[pltpu.VMEM((GATHER_SLOTS, TOP_K, PACKED, FIN_TM, LANES), jnp.uint32),
                        pltpu.SemaphoreType.DMA((GATHER_SLOTS,))])
    return pl.pallas_call(
        _final_kernel,
        grid_spec=grid_spec,
        out_shape=jax.ShapeDtypeStruct((n, d), F32),
        compiler_params=_params(("arbitrary",)),
        name="final",
    )(dest, ys, h1, gate, g.reshape(1, d), b.reshape(1, d))


def kernel(x, ln0_g, ln0_b, w_in, b_in, rpb, w_proj_a, w_proj_b, w_o, b_o, ln1_g, ln1_b,
           w_router_group, b_router_group, w_router_expert, b_router_expert,
           w_gate, w_up, w_down, ln2_g, ln2_b):
    bsz, s, d = x.shape
    n = bsz * s
    assert d == D_MODEL and w_in.shape[0] == DEPTH
    scale = HEAD_DIM ** -0.5

    col_scale = np.ones((w_in.shape[2],), np.float32)
    col_scale[:NA_WIDTH] = scale
    col_scale[3 * NA_WIDTH:3 * NA_WIDTH + DIL_WIDTH] = scale
    w_all = (w_in[0] * col_scale).astype(BF16)
    b_all = (b_in[0] * col_scale).reshape(1, -1)
    assert DIL_OUT_WIDTH == INPROJ_TN
    na_blocks, ngroups = 3 * NA_WIDTH // INPROJ_TN, len(DIL_GROUPS)
    w_r = jnp.concatenate([w_router_group[0], w_router_expert[0]], axis=1)
    w_r = jnp.pad(w_r, ((0, 0), (0, ROUTER_PAD - w_r.shape[1])))
    w_r_hi = w_r.astype(BF16)
    w_r_cat = jnp.concatenate([w_r_hi, (w_r - w_r_hi.astype(F32)).astype(BF16)], axis=1)
    b_r = jnp.pad(jnp.concatenate([b_router_group[0], b_router_expert[0]]),
                  (0, ROUTER_PAD - N_GROUPS - N_EXPERTS)).reshape(1, ROUTER_PAD)
    slopes = _alibi_slopes(DIL_HEADS)

    x2 = x.reshape(n, d)
    hb = _ln0(x2, ln0_g, ln0_b)
    zq = [_inproj(hb, w_all, b_all, bsz, s, DIL_GROUPS[0][1], 3 * NA_WIDTH + 3 * DIL_OUT_WIDTH,
                  lambda j: jnp.where(j < na_blocks, j, na_blocks + (j - na_blocks) * ngroups))]
    for g in range(1, ngroups):
        zq.append(_inproj(hb, w_all, b_all, bsz, s, DIL_GROUPS[g][1], 3 * DIL_OUT_WIDTH,
                          lambda j, g=g: na_blocks + j * ngroups + g))
    na = _neighborhood_attention(zq[0].reshape(-1, bsz, s, HEAD_DIM), rpb[0])
    na = na.reshape(NA_HEADS, n, HEAD_DIM)
    dil = [_dilated_group(zq[g], DIL0_H if g == 0 else 0, g, slopes) for g in range(ngroups)]
    m = _mix(na, [o for o, _ in dil], [l for _, l in dil], hb,
             w_proj_a[0].astype(BF16), w_proj_b[0].astype(BF16), w_all, b_all, s)
    h1, h1p, logits = _outproj(m, x2, ln0_g, ln0_b, w_o[0].astype(BF16), b_o[0], ln1_g[0], ln1_b[0],
                               w_r_cat, b_r)
    dest, gate, slot_tok, block_e, block_next, n_active = _route(logits, n)
    ys = _experts(h1p, slot_tok, block_e, block_next, n_active, w_gate[0], w_up[0], w_down[0])
    out = _final(dest, ys, h1, gate, ln2_g[0], ln2_b[0])
    return out.reshape(bsz, s, d)
```

```python
import functools

import numpy as np
import jax
import jax.numpy as jnp
from jax import lax
from jax.experimental import pallas as pl
from jax.experimental.pallas import tpu as pltpu

F32 = jnp.float32
BF16 = jnp.bfloat16

D_MODEL = 2048
HEAD_DIM = 128
GRID_W = 64
NA_HEADS = 8
NA_KH = 8
NA_KW = 16
DIL_GROUPS = ((128, 1), (512, 4), (2048, 16))
DIL_HEADS_PER_GROUP = 4
DIL_HEADS = DIL_HEADS_PER_GROUP * len(DIL_GROUPS)
N_GROUPS = 8
EXPERTS_PER_GROUP = 8
N_EXPERTS = N_GROUPS * EXPERTS_PER_GROUP
TOP_K = 2
D_EXPERT = D_MODEL // 4
LN_EPS = 1e-5
DEPTH = 1
DN_ALPHA = (2 * DEPTH) ** 0.25
NA_WIDTH = NA_HEADS * HEAD_DIM
DIL_WIDTH = DIL_HEADS * HEAD_DIM
DIL_OUT_WIDTH = DIL_HEADS_PER_GROUP * HEAD_DIM
QKV_WIDTH = 3 * NA_WIDTH + 3 * DIL_WIDTH
QKV_HEADS = QKV_WIDTH // HEAD_DIM
LANES = 128
CHUNKS = D_MODEL // LANES
PACKED = CHUNKS // 2
MASK_VALUE = -1e30

QA_H, KA_H, VA_H = 0, NA_HEADS, 2 * NA_HEADS
DIL0_H = 3 * NA_HEADS

LN_TM = 1024
INPROJ_TM, INPROJ_TN = 2048, 512
NA_QROWS = 4
NA_KROWS = 12
NA_NSUB = 16
DIL_NSUB = 16
DIL_TQ = 256
DIL_HALF = 64
DIL_TK = DIL_TQ + 2 * DIL_HALF
MIX_TM, MIX_TN = 512, 512
MIX_CHUNK = 256
OUT_TM = 512
ROUTER_PAD = 128
ROUTE_TM = 1024
MOE_TB = 128
FIN_TM = 256
CAST_VREGS = 32
GATHER_SLOTS = 3
WEIGHT_DMA_PRIORITY = 1
VMEM_LIMIT = 56 * 1024 * 1024


def _params(sem, limit=VMEM_LIMIT):
    return pltpu.CompilerParams(dimension_semantics=sem, vmem_limit_bytes=limit)


def _layer_norm(x, g, b):
    mu = jnp.mean(x, axis=-1, keepdims=True)
    xc = x - mu
    var = jnp.mean(xc * xc, axis=-1, keepdims=True)
    return xc * lax.rsqrt(var + LN_EPS) * g + b


HIGH_HALF = np.uint32(0xFFFF0000)


def _bf16_bits(x):
    return lax.bitcast_convert_type(x.astype(BF16).astype(F32), jnp.uint32)


def _pack_rows(ref, val):
    for c in range(PACKED):
        lo = _bf16_bits(val[:, c * LANES:(c + 1) * LANES])
        hi = _bf16_bits(val[:, (c + PACKED) * LANES:(c + PACKED + 1) * LANES])
        ref[c] = (lo >> 16) | (hi & HIGH_HALF)


def _unpack_rows(ref):
    words = [ref[c] for c in range(PACKED)]
    lo = [lax.bitcast_convert_type(w << 16, F32) for w in words]
    hi = [lax.bitcast_convert_type(w & HIGH_HALF, F32) for w in words]
    return jnp.concatenate(lo + hi, axis=-1)


def _tile_copies(hbm, row0, rows, vmem, sem, to_hbm, chunk_major=False):
    copies = []
    for c in range(PACKED):
        h = hbm.at[c, pl.ds(row0, rows)] if chunk_major else hbm.at[pl.ds(row0, rows), c]
        v = vmem.at[c]
        copies.append(pltpu.make_async_copy(v, h, sem) if to_hbm else pltpu.make_async_copy(h, v, sem))
    return copies


def _start_row_gather(idx_ref, base, step, count, src_hbm, dst, sem, both_threads, inline=False,
                      chunk_major=False):
    def start(j, c):
        for half in range(2):
            t = 2 * j + half
            row = idx_ref[base + t * step]
            src = src_hbm.at[:, row] if chunk_major else src_hbm.at[row]
            pltpu.make_async_copy(src, dst.at[:, t], sem).start(priority=half if both_threads else 0)
        return c

    if inline:
        for j in range(count // 2):
            start(j, 0)
    else:
        lax.fori_loop(0, count // 2, start, 0, unroll=2)


def _wait_row_gather(dst, sem):
    pltpu.make_async_copy(dst, dst, sem).wait()


def _ln0_kernel(x_ref, g_ref, b_ref, hb_ref):
    hb_ref[...] = _layer_norm(x_ref[...], g_ref[...], b_ref[...]).astype(BF16)


def _ln0(x, g, b):
    n, d = x.shape
    row = pl.BlockSpec((LN_TM, d), lambda i: (i, 0))
    vec = pl.BlockSpec((1, d), lambda i: (0, 0))
    return pl.pallas_call(
        _ln0_kernel,
        grid=(n // LN_TM,),
        in_specs=[row, vec, vec],
        out_specs=row,
        out_shape=jax.ShapeDtypeStruct((n, d), BF16),
        compiler_params=_params(("parallel",)),
        name="ln0",
    )(x, g.reshape(1, d), b.reshape(1, d))


def _inproj_kernel(hb_ref, w_ref, b_ref, o_ref, acc_s, *, dil):
    acc = jnp.dot(hb_ref[...], w_ref[...], preferred_element_type=F32) + b_ref[...]
    if dil == 1:
        for c in range(INPROJ_TN // LANES):
            o_ref[c, 0, 0] = acc[:, c * LANES:(c + 1) * LANES].astype(BF16)
    else:
        for c in range(INPROJ_TN // LANES):
            acc_s[c] = acc[:, c * LANES:(c + 1) * LANES]
        for c in range(INPROJ_TN // LANES):
            for r in range(dil):
                o_ref[c, 0, r] = acc_s[c, pl.ds(r, INPROJ_TM // dil, stride=dil), :].astype(BF16)


def _inproj(hb, w, b, bsz, s, dil, width, col_block):
    n, d = hb.shape
    tiles = s // INPROJ_TM
    assert INPROJ_TM % (16 * dil) == 0 and s % INPROJ_TM == 0 and width % INPROJ_TN == 0
    return pl.pallas_call(
        functools.partial(_inproj_kernel, dil=dil),
        grid=(n // INPROJ_TM, width // INPROJ_TN),
        in_specs=[pl.BlockSpec((INPROJ_TM, d), lambda i, j: (i, 0)),
                  pl.BlockSpec((d, INPROJ_TN), lambda i, j: (0, col_block(j))),
                  pl.BlockSpec((1, INPROJ_TN), lambda i, j: (0, col_block(j)))],
        out_specs=pl.BlockSpec((INPROJ_TN // LANES, 1, dil, INPROJ_TM // dil, LANES),
                               lambda i, j: (j, i // tiles, 0, i % tiles, 0)),
        out_shape=jax.ShapeDtypeStruct((width // LANES, bsz, dil, s // dil, LANES), BF16),
        scratch_shapes=[pltpu.VMEM((INPROJ_TN // LANES, INPROJ_TM, LANES), F32)],
        compiler_params=_params(("parallel", "arbitrary")),
        name=f"inproj_{dil}",
    )(hb, w, b)


def _na_bias_tiles(rpb, rows):
    heads, n_dr, n_dc = rpb.shape
    qc = np.arange(GRID_W)[:, None]
    kc = np.arange(GRID_W)[None, :]
    qcs = np.clip(qc - NA_KW // 2, 0, GRID_W - NA_KW)
    v_col = (kc >= qcs) & (kc < qcs + NA_KW)
    dc = np.clip(kc - qc + NA_KW - 1, 0, n_dc - 1)
    onehot = (dc[None] == np.arange(n_dc)[:, None, None]) & v_col[None]
    toep = jnp.einsum('hrd,dqk->hrqk', rpb.astype(F32), jnp.asarray(onehot, F32),
                      precision=lax.Precision.HIGHEST)
    toep = jnp.where(jnp.asarray(v_col), toep, MASK_VALUE)
    masked = jnp.full((heads, 1, GRID_W, GRID_W), MASK_VALUE, F32)
    blocks = jnp.concatenate([toep, masked], axis=1)
    i = np.arange(NA_QROWS)[:, None]
    j = np.arange(NA_KROWS)[None, :]
    sel = []
    for r0, ks in ((0, 0), (2 * NA_QROWS, 2 * NA_QROWS - NA_KH // 2), (rows - NA_QROWS, rows - NA_KROWS)):
        r, krow = r0 + i, ks + j
        start = np.clip(r - NA_KH // 2, 0, rows - NA_KH)
        v_row = (krow >= start) & (krow < start + NA_KH)
        sel.append(np.where(v_row, krow - r + NA_KH - 1, n_dr))
    sel = np.stack(sel)
    pairs = sel.reshape(3, NA_QROWS, NA_KROWS // 2, 2)
    distinct = sorted({tuple(p) for p in pairs.reshape(-1, 2).tolist()})
    plan = [[[distinct.index(tuple(pairs[v, i, c])) for c in range(NA_KROWS // 2)]
             for i in range(NA_QROWS)] for v in range(3)]
    left = jnp.take(blocks, jnp.asarray([p[0] for p in distinct], jnp.int32), axis=1)
    right = jnp.take(blocks, jnp.asarray([p[1] for p in distinct], jnp.int32), axis=1)
    return jnp.concatenate([left, right], axis=-1), plan


def _attend_blocks(nblocks, scores, finish):
    scored, weighted = {}, {}
    for t in range(nblocks + 2):
        if t < nblocks:
            scored[t] = scores(t)
        if 0 <= t - 1 < nblocks:
            s, vw = scored.pop(t - 1)
            m = jnp.max(s, axis=-1, keepdims=True)
            p = jnp.exp(s - m)
            weighted[t - 1] = (p.astype(BF16), vw, m, jnp.sum(p, axis=-1, keepdims=True))
        if 0 <= t - 2 < nblocks:
            p, vw, m, l = weighted.pop(t - 2)
            finish(t - 2, jnp.dot(p, vw, preferred_element_type=F32) / l, m, l)


def _scores(q, kw, bias):
    return lax.dot_general(q, kw, (((1,), (1,)), ((), ())), preferred_element_type=F32) + bias


def _edge_variant(blk, nblk):
    return jnp.where(blk == 0, 0, jnp.where(blk == nblk - 1, 2, 1))


def _na_kernel(q_ref, k_ref, v_ref, tile_ref, o_ref, tab_ref, *, rows, plan):
    tq = NA_QROWS * GRID_W

    @pl.when(pl.program_id(2) == 0)
    def _():
        for v in range(3):
            for i in range(NA_QROWS):
                for c in range(NA_KROWS // 2):
                    tab_ref[v, 0, i * GRID_W:(i + 1) * GRID_W, c * 2 * GRID_W:(c + 1) * 2 * GRID_W] = (
                        tile_ref[0, plan[v][i][c]])

    def scores(u):
        blk = pl.program_id(2) * NA_NSUB + u
        ks = jnp.clip(blk * NA_QROWS - NA_KH // 2, 0, rows - NA_KROWS)
        off = pl.multiple_of(ks * GRID_W, GRID_W)
        kw = k_ref[0, 0, pl.ds(off, NA_KROWS * GRID_W), :]
        vw = v_ref[0, 0, pl.ds(off, NA_KROWS * GRID_W), :]
        bias = tab_ref[_edge_variant(blk, rows // NA_QROWS), 0]
        return _scores(q_ref[0, 0, u * tq:(u + 1) * tq, :], kw, bias), vw

    def finish(u, o, m, l):
        o_ref[0, 0, u * tq:(u + 1) * tq, :] = o.astype(BF16)

    _attend_blocks(NA_NSUB, scores, finish)


def _neighborhood_attention(zq4, rpb):
    _, bsz, s, _ = zq4.shape
    rows = s // GRID_W
    assert rows % (NA_QROWS * NA_NSUB) == 0 and rows >= NA_KROWS + NA_QROWS and NA_KROWS % 2 == 0
    tiles, plan = _na_bias_tiles(rpb, rows)
    tq = NA_QROWS * GRID_W * NA_NSUB
    return pl.pallas_call(
        functools.partial(_na_kernel, rows=rows, plan=plan),
        grid=(bsz, NA_HEADS, s // tq),
        in_specs=[pl.BlockSpec((1, 1, tq, HEAD_DIM), lambda b, h, i: (QA_H + h, b, i, 0)),
                  pl.BlockSpec((1, 1, s, HEAD_DIM), lambda b, h, i: (KA_H + h, b, 0, 0)),
                  pl.BlockSpec((1, 1, s, HEAD_DIM), lambda b, h, i: (VA_H + h, b, 0, 0)),
                  pl.BlockSpec((1,) + tiles.shape[1:], lambda b, h, i: (h, 0, 0, 0))],
        out_specs=pl.BlockSpec((1, 1, tq, HEAD_DIM), lambda b, h, i: (h, b, i, 0)),
        out_shape=jax.ShapeDtypeStruct((NA_HEADS, bsz, s, HEAD_DIM), BF16),
        scratch_shapes=[pltpu.VMEM((3, 1, NA_QROWS * GRID_W, NA_KROWS * GRID_W), F32)],
        compiler_params=_params(("arbitrary", "arbitrary", "arbitrary")),
        name="na_attn",
    )(zq4, zq4, zq4, tiles)


def _alibi_slopes(n):
    return np.array([2.0 ** (-8.0 * (i + 1) / n) for i in range(n)], dtype=np.float32)


def _dil_bias_tables(slopes, dil):
    qi = np.arange(DIL_TQ)[:, None]
    kj = np.arange(DIL_TK)[None, :]
    dist = np.stack([np.abs(kj - qi + shift) for shift in (0, -DIL_HALF, -2 * DIL_HALF)])
    dist = jnp.asarray(dist, F32)[:, None]
    penalty = jnp.asarray(slopes, F32)[None, :, None, None] * (dist * float(dil))
    return jnp.where(dist <= DIL_HALF, -penalty, MASK_VALUE)


def _dil_kernel(q_ref, k_ref, v_ref, tab_ref, o_ref, lse_ref, *, length, nsub, nres):
    def scores(u):
        res, sub = divmod(u, nsub)
        blk = pl.program_id(3) * nsub + sub
        ks = jnp.clip(blk * DIL_TQ - DIL_HALF, 0, length - DIL_TK)
        off = pl.multiple_of(ks, DIL_HALF)
        kw = k_ref[0, 0, res, pl.ds(off, DIL_TK), :]
        vw = v_ref[0, 0, res, pl.ds(off, DIL_TK), :]
        bias = tab_ref[_edge_variant(blk, length // DIL_TQ), 0]
        return _scores(q_ref[0, 0, res, sub * DIL_TQ:(sub + 1) * DIL_TQ, :], kw, bias), vw

    def finish(u, o, m, l):
        res, sub = divmod(u, nsub)
        rows = slice(sub * DIL_TQ, (sub + 1) * DIL_TQ)
        o_ref[0, 0, res, rows, :] = o.astype(BF16)
        lse_ref[0, 0, res, rows, :] = jnp.broadcast_to(m + jnp.log(l), (DIL_TQ, LANES))

    _attend_blocks(nres * nsub, scores, finish)


def _dilated_group(zq, base, g, slopes):
    window, dil = DIL_GROUPS[g]
    assert (window // 2) // dil == DIL_HALF
    _, bsz, _, length, _ = zq.shape
    nsub = min(DIL_NSUB, length // DIL_TQ)
    nres = min(dil, DIL_NSUB // nsub)
    assert zq.shape[2] == dil and length % (DIL_TQ * nsub) == 0 and length >= DIL_TK and dil % nres == 0
    gh = DIL_HEADS_PER_GROUP
    tabs = _dil_bias_tables(slopes[g * gh:(g + 1) * gh], dil)
    tq = DIL_TQ * nsub
    qspec = pl.BlockSpec((1, 1, nres, tq, HEAD_DIM), lambda b, h, r, n: (base + h, b, r, n, 0))
    kvspec = lambda off: pl.BlockSpec((1, 1, nres, length, HEAD_DIM),
                                      lambda b, h, r, n: (base + off + h, b, r, 0, 0))
    ospec = pl.BlockSpec((1, 1, nres, tq, HEAD_DIM), lambda b, h, r, n: (h, b, r, n, 0))
    return pl.pallas_call(
        functools.partial(_dil_kernel, length=length, nsub=nsub, nres=nres),
        grid=(bsz, gh, dil // nres, length // tq),
        in_specs=[qspec, kvspec(gh), kvspec(2 * gh),
                  pl.BlockSpec((3, 1, DIL_TQ, DIL_TK), lambda b, h, r, n: (0, h, 0, 0))],
        out_specs=[ospec, ospec],
        out_shape=[jax.ShapeDtypeStruct((gh, bsz, dil, length, HEAD_DIM), BF16),
                   jax.ShapeDtypeStruct((gh, bsz, dil, length, HEAD_DIM), F32)],
        compiler_params=_params(("parallel", "parallel", "parallel", "arbitrary")),
        name=f"dil_attn_{dil}",
    )(zq, zq, zq, tabs)


def _mix_kernel(na_ref, o1_ref, o2_ref, o3_ref, l1_ref, l2_ref, l3_ref, hb_ref,
                wpa_ref, wpb_ref, wga_ref, wgb_ref, bga_ref, bgb_ref, m_ref, na_s, dil_s, o_s, l_s):
    @pl.when(pl.program_id(1) == 0)
    def _():
        for h in range(NA_HEADS):
            na_s[:, h * HEAD_DIM:(h + 1) * HEAD_DIM] = na_ref[h]
        for g, (o_ref, l_ref) in enumerate(((o1_ref, l1_ref), (o2_ref, l2_ref), (o3_ref, l3_ref))):
            dil = DIL_GROUPS[g][1]
            for h in range(DIL_HEADS_PER_GROUP):
                for r in range(dil):
                    rows = slice(None) if dil == 1 else pl.ds(r, MIX_TM // dil, stride=dil)
                    o_s[g, h, rows, :] = o_ref[h, 0, r].astype(F32)
                    l_s[g, h, rows, :] = l_ref[h, 0, r]
        for h in range(DIL_HEADS_PER_GROUP):
            ls = [l_s[g, h] for g in range(len(DIL_GROUPS))]
            mx = jnp.maximum(jnp.maximum(ls[0], ls[1]), ls[2])
            es = [jnp.exp(l - mx) for l in ls]
            den = es[0] + es[1] + es[2]
            acc = (es[0] * o_s[0, h] + es[1] * o_s[1, h] + es[2] * o_s[2, h]) / den
            dil_s[:, h * HEAD_DIM:(h + 1) * HEAD_DIM] = acc.astype(BF16)

    hb = hb_ref[...]
    na, dm = na_s[...], dil_s[...]

    def products(c):
        cols = slice(c * MIX_CHUNK, (c + 1) * MIX_CHUNK)
        return (jnp.dot(na, wpa_ref[:, cols], preferred_element_type=F32),
                jnp.dot(dm, wpb_ref[:, cols], preferred_element_type=F32),
                jnp.dot(hb, wga_ref[:, cols], preferred_element_type=F32) + bga_ref[:, cols],
                jnp.dot(hb, wgb_ref[:, cols], preferred_element_type=F32) + bgb_ref[:, cols])

    nchunks = MIX_TN // MIX_CHUNK
    pending = products(0)
    for c in range(nchunks):
        following = products(c + 1) if c + 1 < nchunks else None
        ya, yb, ga, gb = pending
        m_ref[:, c * MIX_CHUNK:(c + 1) * MIX_CHUNK] = (
            jax.nn.sigmoid(ga) * ya + jax.nn.sigmoid(gb) * yb).astype(BF16)
        pending = following


def _mix(na, dil_o, dil_lse, hb, wpa, wpb, w_all, b_all, s):
    n, d = hb.shape
    tm, tn = MIX_TM, MIX_TN
    tiles = s // tm
    gh = DIL_HEADS_PER_GROUP
    ngroups = len(DIL_GROUPS)
    assert all(tm % (16 * dil) == 0 for _, dil in DIL_GROUPS) and s % tm == 0
    col = lambda k: pl.BlockSpec((k, tn), lambda i, j: (0, j))
    ga0, gb0 = QKV_WIDTH // tn, (QKV_WIDTH + d) // tn
    gate = lambda k, first: pl.BlockSpec((k, tn), lambda i, j: (0, first + j))
    grp = lambda dil: pl.BlockSpec((gh, 1, dil, tm // dil, HEAD_DIM),
                                   lambda i, j: (0, i // tiles, 0, i % tiles, 0))
    groups = [grp(dil) for _, dil in DIL_GROUPS]
    return pl.pallas_call(
        _mix_kernel,
        grid=(n // tm, d // tn),
        in_specs=[pl.BlockSpec((NA_HEADS, tm, HEAD_DIM), lambda i, j: (0, i, 0)), *groups, *groups,
                  pl.BlockSpec((tm, d), lambda i, j: (i, 0)),
                  col(NA_WIDTH), col(DIL_OUT_WIDTH),
                  gate(d, ga0), gate(d, gb0), gate(1, ga0), gate(1, gb0)],
        out_specs=pl.BlockSpec((tm, tn), lambda i, j: (i, j)),
        out_shape=jax.ShapeDtypeStruct((n, d), BF16),
        scratch_shapes=[pltpu.VMEM((tm, NA_WIDTH), BF16), pltpu.VMEM((tm, DIL_OUT_WIDTH), BF16),
                        pltpu.VMEM((ngroups, gh, tm, HEAD_DIM), F32),
                        pltpu.VMEM((ngroups, gh, tm, HEAD_DIM), F32)],
        compiler_params=_params(("parallel", "arbitrary")),
        name="mix",
    )(na, *dil_o, *dil_lse, hb, wpa, wpb, w_all, w_all, b_all, b_all)


def _outproj_kernel(m_ref, x_ref, g0_ref, b0_ref, wo_ref, bo_ref, g_ref, b_ref, wr_ref, br_ref,
                    h1_ref, h1p_hbm, lg_ref, hbuf, sem):
    i = pl.program_id(0)
    slot = i % 2

    def out_copies(step, sl):
        return _tile_copies(h1p_hbm, step * OUT_TM, OUT_TM, hbuf.at[sl], sem.at[sl], True)

    mix = jnp.dot(m_ref[...], wo_ref[...], preferred_element_type=F32) + bo_ref[...]
    h = _layer_norm(x_ref[...], g0_ref[...], b0_ref[...])
    h1 = _layer_norm(DN_ALPHA * h + mix, g_ref[...], b_ref[...])
    h1_ref[...] = h1

    @pl.when(i >= 2)
    def _():
        for cp in out_copies(i - 2, slot):
            cp.wait()

    _pack_rows(hbuf.at[slot], h1)
    for cp in out_copies(i, slot):
        cp.start()

    @pl.when(i == pl.num_programs(0) - 1)
    def _():
        for cp in out_copies(i, slot):
            cp.wait()

        @pl.when(i >= 1)
        def _():
            for cp in out_copies(i - 1, 1 - slot):
                cp.wait()

    hi = h1.astype(BF16)
    lo = (h1 - hi.astype(F32)).astype(BF16)
    both = jnp.dot(hi, wr_ref[...], preferred_element_type=F32)
    lg = (both[:, :ROUTER_PAD] + both[:, ROUTER_PAD:]
          + jnp.dot(lo, wr_ref[:, :ROUTER_PAD], preferred_element_type=F32))
    lg_ref[...] = lg + br_ref[...]


def _outproj(m, x, g0, b0, wo, bo, g, b, wr, br):
    n, d = x.shape
    tm = OUT_TM
    row = lambda w: pl.BlockSpec((tm, w), lambda i: (i, 0))
    full = lambda r, c: pl.BlockSpec((r, c), lambda i: (0, 0), pipeline_mode=pl.Buffered(1))
    return pl.pallas_call(
        _outproj_kernel,
        grid=(n // tm,),
        in_specs=[row(d), row(d), full(1, d), full(1, d), full(d, d), full(1, d), full(1, d), full(1, d),
                  full(d, 2 * ROUTER_PAD), full(1, ROUTER_PAD)],
        out_specs=[row(d), pl.BlockSpec(memory_space=pl.ANY), row(ROUTER_PAD)],
        out_shape=[jax.ShapeDtypeStruct((n, d), F32),
                   jax.ShapeDtypeStruct((n, PACKED, LANES), jnp.uint32),
                   jax.ShapeDtypeStruct((n, ROUTER_PAD), F32)],
        scratch_shapes=[pltpu.VMEM((2, PACKED, tm, LANES), jnp.uint32), pltpu.SemaphoreType.DMA((2,))],
        compiler_params=_params(("arbitrary",)),
        name="outproj",
    )(m, x, g0.reshape(1, d), b0.reshape(1, d), wo, bo.reshape(1, d), g.reshape(1, d), b.reshape(1, d),
      wr, br)


def _first_argmax(vals, lane_f):
    top = jnp.max(vals, axis=-1, keepdims=True)
    idx = jnp.min(jnp.where(vals == top, lane_f, float(LANES)), axis=-1, keepdims=True)
    return top, idx


def _route_kernel(lg_ref, gate_ref, dest_ref, cnt_ref, base_s, start_s):
    phase = pl.program_id(0)

    @pl.when(pl.program_id(1) == 0)
    def _():
        @pl.when(phase == 1)
        def _():
            counts = base_s[...]
            cnt_ref[...] = counts
            blocks = jnp.ceil(counts * (1.0 / MOE_TB))
            before = (lax.broadcasted_iota(jnp.int32, (ROUTER_PAD, ROUTER_PAD), 0)
                      < lax.broadcasted_iota(jnp.int32, (ROUTER_PAD, ROUTER_PAD), 1)).astype(BF16)
            starts = jnp.dot(jnp.broadcast_to(blocks, (8, ROUTER_PAD)).astype(BF16), before,
                             preferred_element_type=F32)
            start_s[...] = starts[0:1] * float(MOE_TB)

        base_s[...] = jnp.zeros(base_s.shape, F32)

    lg = lg_ref[...]
    lane = lax.broadcasted_iota(jnp.int32, lg.shape, 1)
    lane_f = lane.astype(F32)
    g_mask = lane < N_GROUPS
    g_top, g_sel = _first_argmax(jnp.where(g_mask, lg, MASK_VALUE), lane_f)
    g_prob = 1.0 / jnp.sum(jnp.where(g_mask, jnp.exp(lg - g_top), 0.0), axis=-1, keepdims=True)
    first = N_GROUPS + g_sel * EXPERTS_PER_GROUP
    e_mask = (lane_f >= first) & (lane_f < first + EXPERTS_PER_GROUP)
    el = jnp.where(e_mask, lg, MASK_VALUE)
    v0, i0 = _first_argmax(el, lane_f)
    v1, i1 = _first_argmax(jnp.where(lane_f == i0, MASK_VALUE, el), lane_f)
    e1 = jnp.exp(v1 - v0)
    w0 = g_prob / (1.0 + e1)
    w1 = g_prob * e1 / (1.0 + e1)

    onehots = [lane_f == idx - N_GROUPS for idx in (i0, i1)]

    @pl.when(phase == 0)
    def _():
        base_s[...] = base_s[...] + sum(jnp.sum(oh.astype(F32), axis=0, keepdims=True) for oh in onehots)

    @pl.when(phase == 1)
    def _():
        tm = lg.shape[0]
        tri = (lax.broadcasted_iota(jnp.int32, (tm, tm), 1)
               < lax.broadcasted_iota(jnp.int32, (tm, tm), 0)).astype(BF16)
        dests = []
        for onehot in onehots:
            before = jnp.dot(tri, onehot.astype(BF16), preferred_element_type=F32) + base_s[...]
            dests.append(jnp.sum(jnp.where(onehot, before + start_s[...], 0.0), axis=-1, keepdims=True))
            base_s[...] = base_s[...] + jnp.sum(onehot.astype(F32), axis=0, keepdims=True)
        gate_ref[...] = jnp.where(lane == 0, w0, jnp.where(lane == 1, w1, 0.0))
        cols = jnp.where(lane == 0, dests[0], jnp.where(lane == 1, dests[1], 0.0))
        dest_ref[...] = jnp.transpose(cols)[0:8, :].astype(jnp.int32)


def _invert_kernel(dest_ref, lo_ref, hi_ref, tok_ref, *, n):
    def clear_range(e, c):
        def clear(s, c2):
            tok_ref[s] = 0
            return c2

        lax.fori_loop(lo_ref[e], hi_ref[e], clear, 0)
        return c

    lax.fori_loop(0, lo_ref.shape[0], clear_range, 0)
    def place(t, c):
        for k in range(TOP_K):
            tok_ref[dest_ref[k * n + t]] = t
        return c

    lax.fori_loop(0, n, place, 0, unroll=8)


def _invert_slots(dest, unused_lo, unused_hi, n, slots):
    assert dest.shape[0] == TOP_K * n
    return pl.pallas_call(
        functools.partial(_invert_kernel, n=n),
        grid_spec=pltpu.PrefetchScalarGridSpec(
            num_scalar_prefetch=3, grid=(1,), in_specs=[],
            out_specs=pl.BlockSpec(memory_space=pltpu.SMEM)),
        out_shape=jax.ShapeDtypeStruct((slots,), jnp.int32),
        compiler_params=_params(("arbitrary",)),
        name="invert_slots",
    )(dest, unused_lo, unused_hi)


def _route(logits, n):
    tm = ROUTE_TM
    one = pl.BlockSpec((1, ROUTER_PAD), lambda p, i: (0, 0))
    gate, dest, counts = pl.pallas_call(
        _route_kernel,
        grid=(2, n // tm),
        in_specs=[pl.BlockSpec((tm, ROUTER_PAD), lambda p, i: (i, 0))],
        out_specs=[pl.BlockSpec((tm, ROUTER_PAD), lambda p, i: (i * p, 0)),
                   pl.BlockSpec((8, tm), lambda p, i: (0, i * p)), one],
        out_shape=[jax.ShapeDtypeStruct((n, ROUTER_PAD), F32),
                   jax.ShapeDtypeStruct((8, n), jnp.int32),
                   jax.ShapeDtypeStruct((1, ROUTER_PAD), F32)],
        scratch_shapes=[pltpu.VMEM((1, ROUTER_PAD), F32), pltpu.VMEM((1, ROUTER_PAD), F32)],
        compiler_params=_params(("arbitrary", "arbitrary")),
        name="route",
    )(logits)
    gate = gate[:, :TOP_K]
    dest = dest[:TOP_K].reshape(TOP_K * n)
    a = n * TOP_K
    counts = counts[0, :N_EXPERTS].astype(jnp.int32)
    pcounts = (counts + MOE_TB - 1) // MOE_TB * MOE_TB
    pends = jnp.cumsum(pcounts)
    nb = a // MOE_TB + N_EXPERTS
    slots = nb * MOE_TB
    unused_lo = jnp.concatenate([pends - pcounts + counts, pends[-1:]])
    unused_hi = jnp.concatenate([pends, jnp.full((1,), slots, jnp.int32)])
    slot_tok = _invert_slots(dest, unused_lo, unused_hi, n, slots)
    block_start = jnp.arange(nb, dtype=jnp.int32) * MOE_TB
    block_e = jnp.minimum(jnp.sum((pends[None, :] <= block_start[:, None]).astype(jnp.int32), axis=1),
                          N_EXPERTS - 1)
    n_active = (pends[-1] // MOE_TB).astype(jnp.int32).reshape(1)
    ids = jnp.arange(N_EXPERTS, dtype=jnp.int32)
    later = (counts[None, :] > 0) & (ids[None, :] > ids[:, None])
    nxt = jnp.min(jnp.where(later, ids[None, :], N_EXPERTS), axis=1)
    nxt = jnp.where(nxt == N_EXPERTS, -1, nxt)
    block_next = jnp.sum(jnp.where(block_e[:, None] == ids[None, :], nxt[None, :], 0), axis=1).astype(jnp.int32)
    return dest, gate, slot_tok, block_e, block_next, n_active


def _cast_weights(src, dst):
    rows, cols = src.shape
    step = CAST_VREGS * 8 * LANES // cols

    def body(r, c):
        sl = pl.ds(pl.multiple_of(r * step, step), step)
        dst[sl, :] = src[sl, :].astype(BF16)
        return c

    lax.fori_loop(0, rows // step, body, 0, unroll=2)


def _expert_kernel(be_ref, nx_ref, nact_ref, tok_ref, h1p_hbm, wg_hbm, wu_hbm, wd_hbm, ys_hbm,
                   xbuf, ybuf, wg_f, wu_f, wd_f, wg_s, wu_s, wd_s, gsem, osem, wsem):
    i = pl.program_id(0)
    slot = i % 2
    xslot = i % GATHER_SLOTS
    nact = nact_ref[0]

    def gather(step, inline=False):
        sl = step % GATHER_SLOTS
        _start_row_gather(tok_ref, step * MOE_TB, 1, MOE_TB, h1p_hbm, xbuf.at[sl], gsem.at[sl], False,
                          inline)

    def out_copies(step, sl):
        return _tile_copies(ys_hbm, step * MOE_TB, MOE_TB, ybuf.at[sl], osem.at[sl], True, chunk_major=True)

    def weight_copies(e):
        return [pltpu.make_async_copy(src.at[e], dst, wsem)
                for src, dst in ((wg_hbm, wg_f), (wu_hbm, wu_f), (wd_hbm, wd_f))]

    @pl.when((i == 0) & (nact > 0))
    def _():
        for cp in weight_copies(be_ref[0]):
            cp.start(priority=WEIGHT_DMA_PRIORITY)
        gather(0)

    @pl.when((i == 0) & (nact > 1))
    def _():
        gather(1)

    @pl.when(i >= 2)
    def _():
        for cp in out_copies(i - 2, slot):
            cp.wait()

    @pl.when(i < nact)
    def _():
        expert = be_ref[i]

        @pl.when((i == 0) | (expert != be_ref[jnp.maximum(i - 1, 0)]))
        def _():
            for cp in weight_copies(expert):
                cp.wait()
            _cast_weights(wg_f, wg_s)
            _cast_weights(wu_f, wu_s)
            _cast_weights(wd_f, wd_s)

            @pl.when(nx_ref[i] >= 0)
            def _():
                for cp in weight_copies(nx_ref[i]):
                    cp.start(priority=WEIGHT_DMA_PRIORITY)

        _wait_row_gather(xbuf.at[xslot], gsem.at[xslot])

        def block(prefetch):
            if prefetch:
                gather(i + 2, inline=True)
            xb = _unpack_rows(xbuf.at[xslot]).astype(BF16)
            gate = jnp.dot(xb, wg_s[...], preferred_element_type=F32)
            up = jnp.dot(xb, wu_s[...], preferred_element_type=F32)
            hid = (jax.nn.silu(gate) * up).astype(BF16)
            _pack_rows(ybuf.at[slot], jnp.dot(hid, wd_s[...], preferred_element_type=F32))

        pl.when(i + 2 < nact)(functools.partial(block, True))
        pl.when(i + 2 >= nact)(functools.partial(block, False))

    @pl.when(i >= nact)
    def _():
        ybuf[slot] = jnp.zeros(ybuf.shape[1:], jnp.uint32)

    for cp in out_copies(i, slot):
        cp.start()

    @pl.when(i == pl.num_programs(0) - 1)
    def _():
        for cp in out_copies(i, slot):
            cp.wait()

        @pl.when(i >= 1)
        def _():
            for cp in out_copies(i - 1, 1 - slot):
                cp.wait()


def _experts(h1p, slot_tok, block_e, block_next, n_active, w_gate, w_up, w_down):
    d = D_MODEL
    slots = slot_tok.shape[0]
    nb = slots // MOE_TB
    any_spec = pl.BlockSpec(memory_space=pl.ANY)
    grid_spec = pltpu.PrefetchScalarGridSpec(
        num_scalar_prefetch=4,
        grid=(nb,),
        in_specs=[any_spec, any_spec, any_spec, any_spec],
        out_specs=any_spec,
        scratch_shapes=[pltpu.VMEM((GATHER_SLOTS, PACKED, MOE_TB, LANES), jnp.uint32),
                        pltpu.VMEM((2, PACKED, MOE_TB, LANES), jnp.uint32),
                        pltpu.VMEM((d, D_EXPERT), F32), pltpu.VMEM((d, D_EXPERT), F32),
                        pltpu.VMEM((D_EXPERT, d), F32),
                        pltpu.VMEM((d, D_EXPERT), BF16), pltpu.VMEM((d, D_EXPERT), BF16),
                        pltpu.VMEM((D_EXPERT, d), BF16),
                        pltpu.SemaphoreType.DMA((GATHER_SLOTS,)), pltpu.SemaphoreType.DMA((2,)),
                        pltpu.SemaphoreType.DMA(())])
    return pl.pallas_call(
        _expert_kernel,
        grid_spec=grid_spec,
        out_shape=jax.ShapeDtypeStruct((PACKED, slots, LANES), jnp.uint32),
        compiler_params=_params(("arbitrary",)),
        name="experts",
    )(block_e, block_next, n_active, slot_tok, h1p, w_gate, w_up, w_down)


def _final_kernel(dest_ref, ys_hbm, h1_ref, gate_ref, g_ref, b_ref, o_ref, ybuf, gsem):
    i = pl.program_id(0)
    steps = pl.num_programs(0)
    slot = i % GATHER_SLOTS

    def fetch(step, inline=False):
        sl = step % GATHER_SLOTS
        for k in range(TOP_K):
            _start_row_gather(dest_ref, k * steps * FIN_TM + step * FIN_TM, 1, FIN_TM, ys_hbm,
                              ybuf.at[sl, k], gsem.at[sl], True, inline, chunk_major=True)

    @pl.when(i == 0)
    def _():
        fetch(0)

    @pl.when((i == 0) & (steps > 1))
    def _():
        fetch(1)

    _wait_row_gather(ybuf.at[slot], gsem.at[slot])

    def combine(prefetch):
        if prefetch:
            fetch(i + 2, inline=True)
        gate = gate_ref[...]
        ffn = (_unpack_rows(ybuf.at[slot, 0]) * gate[:, 0:1]
               + _unpack_rows(ybuf.at[slot, 1]) * gate[:, 1:2])
        o_ref[...] = _layer_norm(DN_ALPHA * h1_ref[...] + ffn, g_ref[...], b_ref[...])

    pl.when(i + 2 < steps)(functools.partial(combine, True))
    pl.when(i + 2 >= steps)(functools.partial(combine, False))


def _final(dest, ys, h1, gate, g, b):
    n, d = h1.shape
    grid_spec = pltpu.PrefetchScalarGridSpec(
        num_scalar_prefetch=1,
        grid=(n // FIN_TM,),
        in_specs=[pl.BlockSpec(memory_space=pl.ANY),
                  pl.BlockSpec((FIN_TM, d), lambda i, ds: (i, 0)),
                  pl.BlockSpec((FIN_TM, TOP_K), lambda i, ds: (i, 0)),
                  pl.BlockSpec((1, d), lambda i, ds: (0, 0)),
                  pl.BlockSpec((1, d), lambda i, ds: (0, 0))],
        out_specs=pl.BlockSpec((FIN_TM, d), lambda i, ds: (i, 0)),
        scratch_shapes=[pltpu.VMEM((GATHER_SLOTS, TOP_K, PACKED, FIN_TM, LANES), jnp.uint32),
                        pltpu.SemaphoreType.DMA((GATHER_SLOTS,))])
    return pl.pallas_call(
        _final_kernel,
        grid_spec=grid_spec,
        out_shape=jax.ShapeDtypeStruct((n, d), F32),
        compiler_params=_params(("arbitrary",)),
        name="final",
    )(dest, ys, h1, gate, g.reshape(1, d), b.reshape(1, d))


def kernel(x, ln0_g, ln0_b, w_in, b_in, rpb, w_proj_a, w_proj_b, w_o, b_o, ln1_g, ln1_b,
           w_router_group, b_router_group, w_router_expert, b_router_expert,
           w_gate, w_up, w_down, ln2_g, ln2_b):
    bsz, s, d = x.shape
    n = bsz * s
    assert d == D_MODEL and w_in.shape[0] == DEPTH
    scale = HEAD_DIM ** -0.5

    col_scale = np.ones((w_in.shape[2],), np.float32)
    col_scale[:NA_WIDTH] = scale
    col_scale[3 * NA_WIDTH:3 * NA_WIDTH + DIL_WIDTH] = scale
    w_all = (w_in[0] * col_scale).astype(BF16)
    b_all = (b_in[0] * col_scale).reshape(1, -1)
    assert DIL_OUT_WIDTH == INPROJ_TN
    na_blocks, ngroups = 3 * NA_WIDTH // INPROJ_TN, len(DIL_GROUPS)
    w_r = jnp.concatenate([w_router_group[0], w_router_expert[0]], axis=1)
    w_r = jnp.pad(w_r, ((0, 0), (0, ROUTER_PAD - w_r.shape[1])))
    w_r_hi = w_r.astype(BF16)
    w_r_cat = jnp.concatenate([w_r_hi, (w_r - w_r_hi.astype(F32)).astype(BF16)], axis=1)
    b_r = jnp.pad(jnp.concatenate([b_router_group[0], b_router_expert[0]]),
                  (0, ROUTER_PAD - N_GROUPS - N_EXPERTS)).reshape(1, ROUTER_PAD)
    slopes = _alibi_slopes(DIL_HEADS)

    x2 = x.reshape(n, d)
    hb = _ln0(x2, ln0_g, ln0_b)
    zq = [_inproj(hb, w_all, b_all, bsz, s, DIL_GROUPS[0][1], 3 * NA_WIDTH + 3 * DIL_OUT_WIDTH,
                  lambda j: jnp.where(j < na_blocks, j, na_blocks + (j - na_blocks) * ngroups))]
    for g in range(1, ngroups):
        zq.append(_inproj(hb, w_all, b_all, bsz, s, DIL_GROUPS[g][1], 3 * DIL_OUT_WIDTH,
                          lambda j, g=g: na_blocks + j * ngroups + g))
    na = _neighborhood_attention(zq[0].reshape(-1, bsz, s, HEAD_DIM), rpb[0])
    na = na.reshape(NA_HEADS, n, HEAD_DIM)
    dil = [_dilated_group(zq[g], DIL0_H if g == 0 else 0, g, slopes) for g in range(ngroups)]
    m = _mix(na, [o for o, _ in dil], [l for _, l in dil], hb,
             w_proj_a[0].astype(BF16), w_proj_b[0].astype(BF16), w_all, b_all, s)
    h1, h1p, logits = _outproj(m, x2, ln0_g, ln0_b, w_o[0].astype(BF16), b_o[0], ln1_g[0], ln1_b[0],
                               w_r_cat, b_r)
    dest, gate, slot_tok, block_e, block_next, n_active = _route(logits, n)
    ys = _experts(h1p, slot_tok, block_e, block_next, n_active, w_gate[0], w_up[0], w_down[0])
    out = _final(dest, ys, h1, gate, ln2_g[0], ln2_b[0])
    return out.reshape(bsz, s, d)
```

```python
import functools

import numpy as np
import jax
import jax.numpy as jnp
from jax import lax
from jax.experimental import pallas as pl
from jax.experimental.pallas import tpu as pltpu

F32 = jnp.float32
BF16 = jnp.bfloat16

D_MODEL = 2048
HEAD_DIM = 128
GRID_W = 64
NA_HEADS = 8
NA_KH = 8
NA_KW = 16
DIL_GROUPS = ((128, 1), (512, 4), (2048, 16))
DIL_HEADS_PER_GROUP = 4
DIL_HEADS = DIL_HEADS_PER_GROUP * len(DIL_GROUPS)
N_GROUPS = 8
EXPERTS_PER_GROUP = 8
N_EXPERTS = N_GROUPS * EXPERTS_PER_GROUP
TOP_K = 2
D_EXPERT = D_MODEL // 4
LN_EPS = 1e-5
DEPTH = 1
DN_ALPHA = (2 * DEPTH) ** 0.25
NA_WIDTH = NA_HEADS * HEAD_DIM
DIL_WIDTH = DIL_HEADS * HEAD_DIM
DIL_OUT_WIDTH = DIL_HEADS_PER_GROUP * HEAD_DIM
QKV_WIDTH = 3 * NA_WIDTH + 3 * DIL_WIDTH
QKV_HEADS = QKV_WIDTH // HEAD_DIM
LANES = 128
CHUNKS = D_MODEL // LANES
PACKED = CHUNKS // 2
MASK_VALUE = -1e30

QA_H, KA_H, VA_H = 0, NA_HEADS, 2 * NA_HEADS
DIL0_H = 3 * NA_HEADS

LN_TM = 1024
INPROJ_TM, INPROJ_TN = 2048, 512
SPLIT_STRIDE = 4
NA_QROWS = 4
NA_KROWS = 12
NA_NSUB = 16
DIL_NSUB = 16
DIL_TQ = 256
DIL_HALF = 64
DIL_TK = DIL_TQ + 2 * DIL_HALF
MIX_TM, MIX_TN = 512, 512
MIX_CHUNK = 256
OUT_TM = 512
ROUTER_PAD = 128
ROUTE_TM = 1024
MOE_TB = 128
FIN_TM = 256
CAST_VREGS = 32
GATHER_SLOTS = 3
WEIGHT_DMA_PRIORITY = 1
VMEM_LIMIT = 56 * 1024 * 1024


def _params(sem, limit=VMEM_LIMIT):
    return pltpu.CompilerParams(dimension_semantics=sem, vmem_limit_bytes=limit)


def _layer_norm(x, g, b):
    mu = jnp.mean(x, axis=-1, keepdims=True)
    xc = x - mu
    var = jnp.mean(xc * xc, axis=-1, keepdims=True)
    return xc * lax.rsqrt(var + LN_EPS) * g + b


HIGH_HALF = np.uint32(0xFFFF0000)


def _bf16_bits(x):
    return lax.bitcast_convert_type(x.astype(BF16).astype(F32), jnp.uint32)


def _pack_rows(ref, val):
    for c in range(PACKED):
        lo = _bf16_bits(val[:, c * LANES:(c + 1) * LANES])
        hi = _bf16_bits(val[:, (c + PACKED) * LANES:(c + PACKED + 1) * LANES])
        ref[c] = (lo >> 16) | (hi & HIGH_HALF)


def _unpack_rows(ref):
    words = [ref[c] for c in range(PACKED)]
    lo = [lax.bitcast_convert_type(w << 16, F32) for w in words]
    hi = [lax.bitcast_convert_type(w & HIGH_HALF, F32) for w in words]
    return jnp.concatenate(lo + hi, axis=-1)


def _tile_copies(hbm, row0, rows, vmem, sem, to_hbm, chunk_major=False):
    copies = []
    for c in range(PACKED):
        h = hbm.at[c, pl.ds(row0, rows)] if chunk_major else hbm.at[pl.ds(row0, rows), c]
        v = vmem.at[c]
        copies.append(pltpu.make_async_copy(v, h, sem) if to_hbm else pltpu.make_async_copy(h, v, sem))
    return copies


def _start_row_gather(idx_ref, base, step, count, src_hbm, dst, sem, both_threads, inline=False,
                      chunk_major=False):
    def start(j, c):
        for half in range(2):
            t = 2 * j + half
            row = idx_ref[base + t * step]
            src = src_hbm.at[:, row] if chunk_major else src_hbm.at[row]
            pltpu.make_async_copy(src, dst.at[:, t], sem).start(priority=half if both_threads else 0)
        return c

    if inline:
        for j in range(count // 2):
            start(j, 0)
    else:
        lax.fori_loop(0, count // 2, start, 0, unroll=2)


def _wait_row_gather(dst, sem):
    pltpu.make_async_copy(dst, dst, sem).wait()


def _ln0_kernel(x_ref, g_ref, b_ref, hb_ref):
    hb_ref[...] = _layer_norm(x_ref[...], g_ref[...], b_ref[...]).astype(BF16)


def _ln0(x, g, b):
    n, d = x.shape
    row = pl.BlockSpec((LN_TM, d), lambda i: (i, 0))
    vec = pl.BlockSpec((1, d), lambda i: (0, 0))
    return pl.pallas_call(
        _ln0_kernel,
        grid=(n // LN_TM,),
        in_specs=[row, vec, vec],
        out_specs=row,
        out_shape=jax.ShapeDtypeStruct((n, d), BF16),
        compiler_params=_params(("parallel",)),
        name="ln0",
    )(x, g.reshape(1, d), b.reshape(1, d))


def _inproj_kernel(hb_ref, w_ref, b_ref, o_ref, acc_s, mid_s, *, dil):
    acc = jnp.dot(hb_ref[...], w_ref[...], preferred_element_type=F32) + b_ref[...]
    if dil == 1:
        for c in range(INPROJ_TN // LANES):
            o_ref[c, 0, 0] = acc[:, c * LANES:(c + 1) * LANES].astype(BF16)
        return
    for c in range(INPROJ_TN // LANES):
        acc_s[c] = acc[:, c * LANES:(c + 1) * LANES]
    if dil <= SPLIT_STRIDE:
        for c in range(INPROJ_TN // LANES):
            for r in range(dil):
                o_ref[c, 0, r] = acc_s[c, pl.ds(r, INPROJ_TM // dil, stride=dil), :].astype(BF16)
        return
    outer = dil // SPLIT_STRIDE
    for c in range(INPROJ_TN // LANES):
        for a in range(SPLIT_STRIDE):
            mid_s[c, a] = acc_s[c, pl.ds(a, INPROJ_TM // SPLIT_STRIDE, stride=SPLIT_STRIDE), :]
    for c in range(INPROJ_TN // LANES):
        for a in range(SPLIT_STRIDE):
            for b in range(outer):
                o_ref[c, 0, b * SPLIT_STRIDE + a] = (
                    mid_s[c, a, pl.ds(b, INPROJ_TM // dil, stride=outer), :].astype(BF16))


def _inproj(hb, w, b, bsz, s, dil, width, col_block):
    n, d = hb.shape
    tiles = s // INPROJ_TM
    assert INPROJ_TM % (16 * dil) == 0 and s % INPROJ_TM == 0 and width % INPROJ_TN == 0
    assert dil <= SPLIT_STRIDE or dil % SPLIT_STRIDE == 0
    mid_rows = INPROJ_TM // SPLIT_STRIDE if dil > SPLIT_STRIDE else 8
    return pl.pallas_call(
        functools.partial(_inproj_kernel, dil=dil),
        grid=(n // INPROJ_TM, width // INPROJ_TN),
        in_specs=[pl.BlockSpec((INPROJ_TM, d), lambda i, j: (i, 0)),
                  pl.BlockSpec((d, INPROJ_TN), lambda i, j: (0, col_block(j))),
                  pl.BlockSpec((1, INPROJ_TN), lambda i, j: (0, col_block(j)))],
        out_specs=pl.BlockSpec((INPROJ_TN // LANES, 1, dil, INPROJ_TM // dil, LANES),
                               lambda i, j: (j, i // tiles, 0, i % tiles, 0)),
        out_shape=jax.ShapeDtypeStruct((width // LANES, bsz, dil, s // dil, LANES), BF16),
        scratch_shapes=[pltpu.VMEM((INPROJ_TN // LANES, INPROJ_TM, LANES), F32),
                        pltpu.VMEM((INPROJ_TN // LANES, SPLIT_STRIDE, mid_rows, LANES), F32)],
        compiler_params=_params(("parallel", "arbitrary")),
        name=f"inproj_{dil}",
    )(hb, w, b)


def _na_bias_tiles(rpb, rows):
    heads, n_dr, n_dc = rpb.shape
    qc = np.arange(GRID_W)[:, None]
    kc = np.arange(GRID_W)[None, :]
    qcs = np.clip(qc - NA_KW // 2, 0, GRID_W - NA_KW)
    v_col = (kc >= qcs) & (kc < qcs + NA_KW)
    dc = np.clip(kc - qc + NA_KW - 1, 0, n_dc - 1)
    onehot = (dc[None] == np.arange(n_dc)[:, None, None]) & v_col[None]
    toep = jnp.einsum('hrd,dqk->hrqk', rpb.astype(F32), jnp.asarray(onehot, F32),
                      precision=lax.Precision.HIGHEST)
    toep = jnp.where(jnp.asarray(v_col), toep, MASK_VALUE)
    masked = jnp.full((heads, 1, GRID_W, GRID_W), MASK_VALUE, F32)
    blocks = jnp.concatenate([toep, masked], axis=1)
    i = np.arange(NA_QROWS)[:, None]
    j = np.arange(NA_KROWS)[None, :]
    sel = []
    for r0, ks in ((0, 0), (2 * NA_QROWS, 2 * NA_QROWS - NA_KH // 2), (rows - NA_QROWS, rows - NA_KROWS)):
        r, krow = r0 + i, ks + j
        start = np.clip(r - NA_KH // 2, 0, rows - NA_KH)
        v_row = (krow >= start) & (krow < start + NA_KH)
        sel.append(np.where(v_row, krow - r + NA_KH - 1, n_dr))
    sel = np.stack(sel)
    pairs = sel.reshape(3, NA_QROWS, NA_KROWS // 2, 2)
    distinct = sorted({tuple(p) for p in pairs.reshape(-1, 2).tolist()})
    plan = [[[distinct.index(tuple(pairs[v, i, c])) for c in range(NA_KROWS // 2)]
             for i in range(NA_QROWS)] for v in range(3)]
    left = jnp.take(blocks, jnp.asarray([p[0] for p in distinct], jnp.int32), axis=1)
    right = jnp.take(blocks, jnp.asarray([p[1] for p in distinct], jnp.int32), axis=1)
    return jnp.concatenate([left, right], axis=-1), plan


def _attend_blocks(nblocks, scores, finish):
    scored, weighted = {}, {}
    for t in range(nblocks + 2):
        if t < nblocks:
            scored[t] = scores(t)
        if 0 <= t - 1 < nblocks:
            s, vw = scored.pop(t - 1)
            m = jnp.max(s, axis=-1, keepdims=True)
            p = jnp.exp(s - m)
            weighted[t - 1] = (p.astype(BF16), vw, m, jnp.sum(p, axis=-1, keepdims=True))
        if 0 <= t - 2 < nblocks:
            p, vw, m, l = weighted.pop(t - 2)
            finish(t - 2, jnp.dot(p, vw, preferred_element_type=F32) / l, m, l)


def _scores(q, kw, bias):
    return lax.dot_general(q, kw, (((1,), (1,)), ((), ())), preferred_element_type=F32) + bias


def _edge_variant(blk, nblk):
    return jnp.where(blk == 0, 0, jnp.where(blk == nblk - 1, 2, 1))


def _na_kernel(q_ref, k_ref, v_ref, tile_ref, o_ref, tab_ref, *, rows, plan):
    tq = NA_QROWS * GRID_W

    @pl.when(pl.program_id(2) == 0)
    def _():
        for v in range(3):
            for i in range(NA_QROWS):
                for c in range(NA_KROWS // 2):
                    tab_ref[v, 0, i * GRID_W:(i + 1) * GRID_W, c * 2 * GRID_W:(c + 1) * 2 * GRID_W] = (
                        tile_ref[0, plan[v][i][c]])

    def scores(u):
        blk = pl.program_id(2) * NA_NSUB + u
        ks = jnp.clip(blk * NA_QROWS - NA_KH // 2, 0, rows - NA_KROWS)
        off = pl.multiple_of(ks * GRID_W, GRID_W)
        kw = k_ref[0, 0, pl.ds(off, NA_KROWS * GRID_W), :]
        vw = v_ref[0, 0, pl.ds(off, NA_KROWS * GRID_W), :]
        bias = tab_ref[_edge_variant(blk, rows // NA_QROWS), 0]
        return _scores(q_ref[0, 0, u * tq:(u + 1) * tq, :], kw, bias), vw

    def finish(u, o, m, l):
        o_ref[0, 0, u * tq:(u + 1) * tq, :] = o.astype(BF16)

    _attend_blocks(NA_NSUB, scores, finish)


def _neighborhood_attention(zq4, rpb):
    _, bsz, s, _ = zq4.shape
    rows = s // GRID_W
    assert rows % (NA_QROWS * NA_NSUB) == 0 and rows >= NA_KROWS + NA_QROWS and NA_KROWS % 2 == 0
    tiles, plan = _na_bias_tiles(rpb, rows)
    tq = NA_QROWS * GRID_W * NA_NSUB
    return pl.pallas_call(
        functools.partial(_na_kernel, rows=rows, plan=plan),
        grid=(bsz, NA_HEADS, s // tq),
        in_specs=[pl.BlockSpec((1, 1, tq, HEAD_DIM), lambda b, h, i: (QA_H + h, b, i, 0)),
                  pl.BlockSpec((1, 1, s, HEAD_DIM), lambda b, h, i: (KA_H + h, b, 0, 0)),
                  pl.BlockSpec((1, 1, s, HEAD_DIM), lambda b, h, i: (VA_H + h, b, 0, 0)),
                  pl.BlockSpec((1,) + tiles.shape[1:], lambda b, h, i: (h, 0, 0, 0))],
        out_specs=pl.BlockSpec((1, 1, tq, HEAD_DIM), lambda b, h, i: (h, b, i, 0)),
        out_shape=jax.ShapeDtypeStruct((NA_HEADS, bsz, s, HEAD_DIM), BF16),
        scratch_shapes=[pltpu.VMEM((3, 1, NA_QROWS * GRID_W, NA_KROWS * GRID_W), F32)],
        compiler_params=_params(("arbitrary", "arbitrary", "arbitrary")),
        name="na_attn",
    )(zq4, zq4, zq4, tiles)


def _alibi_slopes(n):
    return np.array([2.0 ** (-8.0 * (i + 1) / n) for i in range(n)], dtype=np.float32)


def _dil_bias_tables(slopes, dil):
    qi = np.arange(DIL_TQ)[:, None]
    kj = np.arange(DIL_TK)[None, :]
    dist = np.stack([np.abs(kj - qi + shift) for shift in (0, -DIL_HALF, -2 * DIL_HALF)])
    dist = jnp.asarray(dist, F32)[:, None]
    penalty = jnp.asarray(slopes, F32)[None, :, None, None] * (dist * float(dil))
    return jnp.where(dist <= DIL_HALF, -penalty, MASK_VALUE)


def _dil_kernel(q_ref, k_ref, v_ref, tab_ref, o_ref, lse_ref, *, length, nsub, nres):
    def scores(u):
        res, sub = divmod(u, nsub)
        blk = pl.program_id(3) * nsub + sub
        ks = jnp.clip(blk * DIL_TQ - DIL_HALF, 0, length - DIL_TK)
        off = pl.multiple_of(ks, DIL_HALF)
        kw = k_ref[0, 0, res, pl.ds(off, DIL_TK), :]
        vw = v_ref[0, 0, res, pl.ds(off, DIL_TK), :]
        bias = tab_ref[_edge_variant(blk, length // DIL_TQ), 0]
        return _scores(q_ref[0, 0, res, sub * DIL_TQ:(sub + 1) * DIL_TQ, :], kw, bias), vw

    def finish(u, o, m, l):
        res, sub = divmod(u, nsub)
        rows = slice(sub * DIL_TQ, (sub + 1) * DIL_TQ)
        o_ref[0, 0, res, rows, :] = o.astype(BF16)
        lse_ref[0, 0, res, rows, :] = jnp.broadcast_to(m + jnp.log(l), (DIL_TQ, LANES))

    _attend_blocks(nres * nsub, scores, finish)


def _dilated_group(zq, base, g, slopes):
    window, dil = DIL_GROUPS[g]
    assert (window // 2) // dil == DIL_HALF
    _, bsz, _, length, _ = zq.shape
    nsub = min(DIL_NSUB, length // DIL_TQ)
    nres = min(dil, DIL_NSUB // nsub)
    assert zq.shape[2] == dil and length % (DIL_TQ * nsub) == 0 and length >= DIL_TK and dil % nres == 0
    gh = DIL_HEADS_PER_GROUP
    tabs = _dil_bias_tables(slopes[g * gh:(g + 1) * gh], dil)
    tq = DIL_TQ * nsub
    qspec = pl.BlockSpec((1, 1, nres, tq, HEAD_DIM), lambda b, h, r, n: (base + h, b, r, n, 0))
    kvspec = lambda off: pl.BlockSpec((1, 1, nres, length, HEAD_DIM),
                                      lambda b, h, r, n: (base + off + h, b, r, 0, 0))
    ospec = pl.BlockSpec((1, 1, nres, tq, HEAD_DIM), lambda b, h, r, n: (h, b, r, n, 0))
    return pl.pallas_call(
        functools.partial(_dil_kernel, length=length, nsub=nsub, nres=nres),
        grid=(bsz, gh, dil // nres, length // tq),
        in_specs=[qspec, kvspec(gh), kvspec(2 * gh),
                  pl.BlockSpec((3, 1, DIL_TQ, DIL_TK), lambda b, h, r, n: (0, h, 0, 0))],
        out_specs=[ospec, ospec],
        out_shape=[jax.ShapeDtypeStruct((gh, bsz, dil, length, HEAD_DIM), BF16),
                   jax.ShapeDtypeStruct((gh, bsz, dil, length, HEAD_DIM), F32)],
        compiler_params=_params(("parallel", "parallel", "parallel", "arbitrary")),
        name=f"dil_attn_{dil}",
    )(zq, zq, zq, tabs)


def _mix_kernel(na_ref, o1_ref, o2_ref, o3_ref, l1_ref, l2_ref, l3_ref, hb_ref,
                wpa_ref, wpb_ref, wga_ref, wgb_ref, bga_ref, bgb_ref, m_ref, na_s, dil_s, o_s, l_s):
    @pl.when(pl.program_id(1) == 0)
    def _():
        for h in range(NA_HEADS):
            na_s[:, h * HEAD_DIM:(h + 1) * HEAD_DIM] = na_ref[h]
        for g, (o_ref, l_ref) in enumerate(((o1_ref, l1_ref), (o2_ref, l2_ref), (o3_ref, l3_ref))):
            dil = DIL_GROUPS[g][1]
            for h in range(DIL_HEADS_PER_GROUP):
                for r in range(dil):
                    rows = slice(None) if dil == 1 else pl.ds(r, MIX_TM // dil, stride=dil)
                    o_s[g, h, rows, :] = o_ref[h, 0, r].astype(F32)
                    l_s[g, h, rows, :] = l_ref[h, 0, r]
        for h in range(DIL_HEADS_PER_GROUP):
            ls = [l_s[g, h] for g in range(len(DIL_GROUPS))]
            mx = jnp.maximum(jnp.maximum(ls[0], ls[1]), ls[2])
            es = [jnp.exp(l - mx) for l in ls]
            den = es[0] + es[1] + es[2]
            acc = (es[0] * o_s[0, h] + es[1] * o_s[1, h] + es[2] * o_s[2, h]) / den
            dil_s[:, h * HEAD_DIM:(h + 1) * HEAD_DIM] = acc.astype(BF16)

    hb = hb_ref[...]
    na, dm = na_s[...], dil_s[...]

    def products(c):
        cols = slice(c * MIX_CHUNK, (c + 1) * MIX_CHUNK)
        return (jnp.dot(na, wpa_ref[:, cols], preferred_element_type=F32),
                jnp.dot(dm, wpb_ref[:, cols], preferred_element_type=F32),
                jnp.dot(hb, wga_ref[:, cols], preferred_element_type=F32) + bga_ref[:, cols],
                jnp.dot(hb, wgb_ref[:, cols], preferred_element_type=F32) + bgb_ref[:, cols])

    nchunks = MIX_TN // MIX_CHUNK
    pending = products(0)
    for c in range(nchunks):
        following = products(c + 1) if c + 1 < nchunks else None
        ya, yb, ga, gb = pending
        m_ref[:, c * MIX_CHUNK:(c + 1) * MIX_CHUNK] = (
            jax.nn.sigmoid(ga) * ya + jax.nn.sigmoid(gb) * yb).astype(BF16)
        pending = following


def _mix(na, dil_o, dil_lse, hb, wpa, wpb, w_all, b_all, s):
    n, d = hb.shape
    tm, tn = MIX_TM, MIX_TN
    tiles = s // tm
    gh = DIL_HEADS_PER_GROUP
    ngroups = len(DIL_GROUPS)
    assert all(tm % (16 * dil) == 0 for _, dil in DIL_GROUPS) and s % tm == 0
    col = lambda k: pl.BlockSpec((k, tn), lambda i, j: (0, j))
    ga0, gb0 = QKV_WIDTH // tn, (QKV_WIDTH + d) // tn
    gate = lambda k, first: pl.BlockSpec((k, tn), lambda i, j: (0, first + j))
    grp = lambda dil: pl.BlockSpec((gh, 1, dil, tm // dil, HEAD_DIM),
                                   lambda i, j: (0, i // tiles, 0, i % tiles, 0))
    groups = [grp(dil) for _, dil in DIL_GROUPS]
    return pl.pallas_call(
        _mix_kernel,
        grid=(n // tm, d // tn),
        in_specs=[pl.BlockSpec((NA_HEADS, tm, HEAD_DIM), lambda i, j: (0, i, 0)), *groups, *groups,
                  pl.BlockSpec((tm, d), lambda i, j: (i, 0)),
                  col(NA_WIDTH), col(DIL_OUT_WIDTH),
                  gate(d, ga0), gate(d, gb0), gate(1, ga0), gate(1, gb0)],
        out_specs=pl.BlockSpec((tm, tn), lambda i, j: (i, j)),
        out_shape=jax.ShapeDtypeStruct((n, d), BF16),
        scratch_shapes=[pltpu.VMEM((tm, NA_WIDTH), BF16), pltpu.VMEM((tm, DIL_OUT_WIDTH), BF16),
                        pltpu.VMEM((ngroups, gh, tm, HEAD_DIM), F32),
                        pltpu.VMEM((ngroups, gh, tm, HEAD_DIM), F32)],
        compiler_params=_params(("parallel", "arbitrary")),
        name="mix",
    )(na, *dil_o, *dil_lse, hb, wpa, wpb, w_all, w_all, b_all, b_all)


def _outproj_kernel(m_ref, x_ref, g0_ref, b0_ref, wo_ref, bo_ref, g_ref, b_ref, wr_ref, br_ref,
                    h1_ref, h1p_hbm, lg_ref, hbuf, sem):
    i = pl.program_id(0)
    slot = i % 2

    def out_copies(step, sl):
        return _tile_copies(h1p_hbm, step * OUT_TM, OUT_TM, hbuf.at[sl], sem.at[sl], True)

    mix = jnp.dot(m_ref[...], wo_ref[...], preferred_element_type=F32) + bo_ref[...]
    h = _layer_norm(x_ref[...], g0_ref[...], b0_ref[...])
    h1 = _layer_norm(DN_ALPHA * h + mix, g_ref[...], b_ref[...])
    h1_ref[...] = h1

    @pl.when(i >= 2)
    def _():
        for cp in out_copies(i - 2, slot):
            cp.wait()

    _pack_rows(hbuf.at[slot], h1)
    for cp in out_copies(i, slot):
        cp.start()

    @pl.when(i == pl.num_programs(0) - 1)
    def _():
        for cp in out_copies(i, slot):
            cp.wait()

        @pl.when(i >= 1)
        def _():
            for cp in out_copies(i - 1, 1 - slot):
                cp.wait()

    hi = h1.astype(BF16)
    lo = (h1 - hi.astype(F32)).astype(BF16)
    both = jnp.dot(hi, wr_ref[...], preferred_element_type=F32)
    lg = (both[:, :ROUTER_PAD] + both[:, ROUTER_PAD:]
          + jnp.dot(lo, wr_ref[:, :ROUTER_PAD], preferred_element_type=F32))
    lg_ref[...] = lg + br_ref[...]


def _outproj(m, x, g0, b0, wo, bo, g, b, wr, br):
    n, d = x.shape
    tm = OUT_TM
    row = lambda w: pl.BlockSpec((tm, w), lambda i: (i, 0))
    full = lambda r, c: pl.BlockSpec((r, c), lambda i: (0, 0), pipeline_mode=pl.Buffered(1))
    return pl.pallas_call(
        _outproj_kernel,
        grid=(n // tm,),
        in_specs=[row(d), row(d), full(1, d), full(1, d), full(d, d), full(1, d), full(1, d), full(1, d),
                  full(d, 2 * ROUTER_PAD), full(1, ROUTER_PAD)],
        out_specs=[row(d), pl.BlockSpec(memory_space=pl.ANY), row(ROUTER_PAD)],
        out_shape=[jax.ShapeDtypeStruct((n, d), F32),
                   jax.ShapeDtypeStruct((n, PACKED, LANES), jnp.uint32),
                   jax.ShapeDtypeStruct((n, ROUTER_PAD), F32)],
        scratch_shapes=[pltpu.VMEM((2, PACKED, tm, LANES), jnp.uint32), pltpu.SemaphoreType.DMA((2,))],
        compiler_params=_params(("arbitrary",)),
        name="outproj",
    )(m, x, g0.reshape(1, d), b0.reshape(1, d), wo, bo.reshape(1, d), g.reshape(1, d), b.reshape(1, d),
      wr, br)


def _first_argmax(vals, lane_f):
    top = jnp.max(vals, axis=-1, keepdims=True)
    idx = jnp.min(jnp.where(vals == top, lane_f, float(LANES)), axis=-1, keepdims=True)
    return top, idx


def _route_kernel(lg_ref, gate_ref, dest_ref, cnt_ref, base_s, start_s):
    phase = pl.program_id(0)

    @pl.when(pl.program_id(1) == 0)
    def _():
        @pl.when(phase == 1)
        def _():
            counts = base_s[...]
            cnt_ref[...] = counts
            blocks = jnp.ceil(counts * (1.0 / MOE_TB))
            before = (lax.broadcasted_iota(jnp.int32, (ROUTER_PAD, ROUTER_PAD), 0)
                      < lax.broadcasted_iota(jnp.int32, (ROUTER_PAD, ROUTER_PAD), 1)).astype(BF16)
            starts = jnp.dot(jnp.broadcast_to(blocks, (8, ROUTER_PAD)).astype(BF16), before,
                             preferred_element_type=F32)
            start_s[...] = starts[0:1] * float(MOE_TB)

        base_s[...] = jnp.zeros(base_s.shape, F32)

    lg = lg_ref[...]
    lane = lax.broadcasted_iota(jnp.int32, lg.shape, 1)
    lane_f = lane.astype(F32)
    g_mask = lane < N_GROUPS
    g_top, g_sel = _first_argmax(jnp.where(g_mask, lg, MASK_VALUE), lane_f)
    g_prob = 1.0 / jnp.sum(jnp.where(g_mask, jnp.exp(lg - g_top), 0.0), axis=-1, keepdims=True)
    first = N_GROUPS + g_sel * EXPERTS_PER_GROUP
    e_mask = (lane_f >= first) & (lane_f < first + EXPERTS_PER_GROUP)
    el = jnp.where(e_mask, lg, MASK_VALUE)
    v0, i0 = _first_argmax(el, lane_f)
    v1, i1 = _first_argmax(jnp.where(lane_f == i0, MASK_VALUE, el), lane_f)
    e1 = jnp.exp(v1 - v0)
    w0 = g_prob / (1.0 + e1)
    w1 = g_prob * e1 / (1.0 + e1)

    onehots = [lane_f == idx - N_GROUPS for idx in (i0, i1)]

    @pl.when(phase == 0)
    def _():
        base_s[...] = base_s[...] + sum(jnp.sum(oh.astype(F32), axis=0, keepdims=True) for oh in onehots)

    @pl.when(phase == 1)
    def _():
        tm = lg.shape[0]
        tri = (lax.broadcasted_iota(jnp.int32, (tm, tm), 1)
               < lax.broadcasted_iota(jnp.int32, (tm, tm), 0)).astype(BF16)
        dests = []
        for onehot in onehots:
            before = jnp.dot(tri, onehot.astype(BF16), preferred_element_type=F32) + base_s[...]
            dests.append(jnp.sum(jnp.where(onehot, before + start_s[...], 0.0), axis=-1, keepdims=True))
            base_s[...] = base_s[...] + jnp.sum(onehot.astype(F32), axis=0, keepdims=True)
        gate_ref[...] = jnp.where(lane == 0, w0, jnp.where(lane == 1, w1, 0.0))
        cols = jnp.where(lane == 0, dests[0], jnp.where(lane == 1, dests[1], 0.0))
        dest_ref[...] = jnp.transpose(cols)[0:8, :].astype(jnp.int32)


def _invert_kernel(dest_ref, lo_ref, hi_ref, tok_ref, *, n):
    def clear_range(e, c):
        def clear(s, c2):
            tok_ref[s] = 0
            return c2

        lax.fori_loop(lo_ref[e], hi_ref[e], clear, 0)
        return c

    lax.fori_loop(0, lo_ref.shape[0], clear_range, 0)
    def place(t, c):
        for k in range(TOP_K):
            tok_ref[dest_ref[k * n + t]] = t
        return c

    lax.fori_loop(0, n, place, 0, unroll=8)


def _invert_slots(dest, unused_lo, unused_hi, n, slots):
    assert dest.shape[0] == TOP_K * n
    return pl.pallas_call(
        functools.partial(_invert_kernel, n=n),
        grid_spec=pltpu.PrefetchScalarGridSpec(
            num_scalar_prefetch=3, grid=(1,), in_specs=[],
            out_specs=pl.BlockSpec(memory_space=pltpu.SMEM)),
        out_shape=jax.ShapeDtypeStruct((slots,), jnp.int32),
        compiler_params=_params(("arbitrary",)),
        name="invert_slots",
    )(dest, unused_lo, unused_hi)


def _route(logits, n):
    tm = ROUTE_TM
    one = pl.BlockSpec((1, ROUTER_PAD), lambda p, i: (0, 0))
    gate, dest, counts = pl.pallas_call(
        _route_kernel,
        grid=(2, n // tm),
        in_specs=[pl.BlockSpec((tm, ROUTER_PAD), lambda p, i: (i, 0))],
        out_specs=[pl.BlockSpec((tm, ROUTER_PAD), lambda p, i: (i * p, 0)),
                   pl.BlockSpec((8, tm), lambda p, i: (0, i * p)), one],
        out_shape=[jax.ShapeDtypeStruct((n, ROUTER_PAD), F32),
                   jax.ShapeDtypeStruct((8, n), jnp.int32),
                   jax.ShapeDtypeStruct((1, ROUTER_PAD), F32)],
        scratch_shapes=[pltpu.VMEM((1, ROUTER_PAD), F32), pltpu.VMEM((1, ROUTER_PAD), F32)],
        compiler_params=_params(("arbitrary", "arbitrary")),
        name="route",
    )(logits)
    gate = gate[:, :TOP_K]
    dest = dest[:TOP_K].reshape(TOP_K * n)
    a = n * TOP_K
    counts = counts[0, :N_EXPERTS].astype(jnp.int32)
    pcounts = (counts + MOE_TB - 1) // MOE_TB * MOE_TB
    pends = jnp.cumsum(pcounts)
    nb = a // MOE_TB + N_EXPERTS
    slots = nb * MOE_TB
    unused_lo = jnp.concatenate([pends - pcounts + counts, pends[-1:]])
    unused_hi = jnp.concatenate([pends, jnp.full((1,), slots, jnp.int32)])
    slot_tok = _invert_slots(dest, unused_lo, unused_hi, n, slots)
    block_start = jnp.arange(nb, dtype=jnp.int32) * MOE_TB
    block_e = jnp.minimum(jnp.sum((pends[None, :] <= block_start[:, None]).astype(jnp.int32), axis=1),
                          N_EXPERTS - 1)
    n_active = (pends[-1] // MOE_TB).astype(jnp.int32).reshape(1)
    ids = jnp.arange(N_EXPERTS, dtype=jnp.int32)
    later = (counts[None, :] > 0) & (ids[None, :] > ids[:, None])
    nxt = jnp.min(jnp.where(later, ids[None, :], N_EXPERTS), axis=1)
    nxt = jnp.where(nxt == N_EXPERTS, -1, nxt)
    block_next = jnp.sum(jnp.where(block_e[:, None] == ids[None, :], nxt[None, :], 0), axis=1).astype(jnp.int32)
    return dest, gate, slot_tok, block_e, block_next, n_active


def _cast_weights(src, dst):
    rows, cols = src.shape
    step = CAST_VREGS * 8 * LANES // cols

    def body(r, c):
        sl = pl.ds(pl.multiple_of(r * step, step), step)
        dst[sl, :] = src[sl, :].astype(BF16)
        return c

    lax.fori_loop(0, rows // step, body, 0, unroll=2)


def _expert_kernel(be_ref, nx_ref, nact_ref, tok_ref, h1p_hbm, wg_hbm, wu_hbm, wd_hbm, ys_hbm,
                   xbuf, ybuf, wg_f, wu_f, wd_f, wg_s, wu_s, wd_s, gsem, osem, wsem):
    i = pl.program_id(0)
    slot = i % 2
    xslot = i % GATHER_SLOTS
    nact = nact_ref[0]

    def gather(step, inline=False):
        sl = step % GATHER_SLOTS
        _start_row_gather(tok_ref, step * MOE_TB, 1, MOE_TB, h1p_hbm, xbuf.at[sl], gsem.at[sl], False,
                          inline)

    def out_copies(step, sl):
        return _tile_copies(ys_hbm, step * MOE_TB, MOE_TB, ybuf.at[sl], osem.at[sl], True, chunk_major=True)

    def weight_copies(e):
        return [pltpu.make_async_copy(src.at[e], dst, wsem)
                for src, dst in ((wg_hbm, wg_f), (wu_hbm, wu_f), (wd_hbm, wd_f))]

    @pl.when((i == 0) & (nact > 0))
    def _():
        for cp in weight_copies(be_ref[0]):
            cp.start(priority=WEIGHT_DMA_PRIORITY)
        gather(0)

    @pl.when((i == 0) & (nact > 1))
    def _():
        gather(1)

    @pl.when(i >= 2)
    def _():
        for cp in out_copies(i - 2, slot):
            cp.wait()

    @pl.when(i < nact)
    def _():
        expert = be_ref[i]

        @pl.when((i == 0) | (expert != be_ref[jnp.maximum(i - 1, 0)]))
        def _():
            for cp in weight_copies(expert):
                cp.wait()
            _cast_weights(wg_f, wg_s)
            _cast_weights(wu_f, wu_s)
            _cast_weights(wd_f, wd_s)

            @pl.when(nx_ref[i] >= 0)
            def _():
                for cp in weight_copies(nx_ref[i]):
                    cp.start(priority=WEIGHT_DMA_PRIORITY)

        _wait_row_gather(xbuf.at[xslot], gsem.at[xslot])

        def block(prefetch):
            if prefetch:
                gather(i + 2, inline=True)
            xb = _unpack_rows(xbuf.at[xslot]).astype(BF16)
            gate = jnp.dot(xb, wg_s[...], preferred_element_type=F32)
            up = jnp.dot(xb, wu_s[...], preferred_element_type=F32)
            hid = (jax.nn.silu(gate) * up).astype(BF16)
            _pack_rows(ybuf.at[slot], jnp.dot(hid, wd_s[...], preferred_element_type=F32))

        pl.when(i + 2 < nact)(functools.partial(block, True))
        pl.when(i + 2 >= nact)(functools.partial(block, False))

    @pl.when(i >= nact)
    def _():
        ybuf[slot] = jnp.zeros(ybuf.shape[1:], jnp.uint32)

    for cp in out_copies(i, slot):
        cp.start()

    @pl.when(i == pl.num_programs(0) - 1)
    def _():
        for cp in out_copies(i, slot):
            cp.wait()

        @pl.when(i >= 1)
        def _():
            for cp in out_copies(i - 1, 1 - slot):
                cp.wait()


def _experts(h1p, slot_tok, block_e, block_next, n_active, w_gate, w_up, w_down):
    d = D_MODEL
    slots = slot_tok.shape[0]
    nb = slots // MOE_TB
    any_spec = pl.BlockSpec(memory_space=pl.ANY)
    grid_spec = pltpu.PrefetchScalarGridSpec(
        num_scalar_prefetch=4,
        grid=(nb,),
        in_specs=[any_spec, any_spec, any_spec, any_spec],
        out_specs=any_spec,
        scratch_shapes=[pltpu.VMEM((GATHER_SLOTS, PACKED, MOE_TB, LANES), jnp.uint32),
                        pltpu.VMEM((2, PACKED, MOE_TB, LANES), jnp.uint32),
                        pltpu.VMEM((d, D_EXPERT), F32), pltpu.VMEM((d, D_EXPERT), F32),
                        pltpu.VMEM((D_EXPERT, d), F32),
                        pltpu.VMEM((d, D_EXPERT), BF16), pltpu.VMEM((d, D_EXPERT), BF16),
                        pltpu.VMEM((D_EXPERT, d), BF16),
                        pltpu.SemaphoreType.DMA((GATHER_SLOTS,)), pltpu.SemaphoreType.DMA((2,)),
                        pltpu.SemaphoreType.DMA(())])
    return pl.pallas_call(
        _expert_kernel,
        grid_spec=grid_spec,
        out_shape=jax.ShapeDtypeStruct((PACKED, slots, LANES), jnp.uint32),
        compiler_params=_params(("arbitrary",)),
        name="experts",
    )(block_e, block_next, n_active, slot_tok, h1p, w_gate, w_up, w_down)


def _final_kernel(dest_ref, ys_hbm, h1_ref, gate_ref, g_ref, b_ref, o_ref, ybuf, gsem):
    i = pl.program_id(0)
    steps = pl.num_programs(0)
    slot = i % GATHER_SLOTS

    def fetch(step, inline=False):
        sl = step % GATHER_SLOTS
        for k in range(TOP_K):
            _start_row_gather(dest_ref, k * steps * FIN_TM + step * FIN_TM, 1, FIN_TM, ys_hbm,
                              ybuf.at[sl, k], gsem.at[sl], True, inline, chunk_major=True)

    @pl.when(i == 0)
    def _():
        fetch(0)

    @pl.when((i == 0) & (steps > 1))
    def _():
        fetch(1)

    _wait_row_gather(ybuf.at[slot], gsem.at[slot])

    def combine(prefetch):
        if prefetch:
            fetch(i + 2, inline=True)
        gate = gate_ref[...]
        ffn = (_unpack_rows(ybuf.at[slot, 0]) * gate[:, 0:1]
               + _unpack_rows(ybuf.at[slot, 1]) * gate[:, 1:2])
        o_ref[...] = _layer_norm(DN_ALPHA * h1_ref[...] + ffn, g_ref[...], b_ref[...])

    pl.when(i + 2 < steps)(functools.partial(combine, True))
    pl.when(i + 2 >= steps)(functools.partial(combine, False))


def _final(dest, ys, h1, gate, g, b):
    n, d = h1.shape
    grid_spec = pltpu.PrefetchScalarGridSpec(
        num_scalar_prefetch=1,
        grid=(n // FIN_TM,),
        in_specs=[pl.BlockSpec(memory_space=pl.ANY),
                  pl.BlockSpec((FIN_TM, d), lambda i, ds: (i, 0)),
                  pl.BlockSpec((FIN_TM, TOP_K), lambda i, ds: (i, 0)),
                  pl.BlockSpec((1, d), lambda i, ds: (0, 0)),
                  pl.BlockSpec((1, d), lambda i, ds: (0, 0))],
        out_specs=pl.BlockSpec((FIN_TM, d), lambda i, ds: (i, 0)),
        scratch_shapes=[pltpu.VMEM((GATHER_SLOTS, TOP_K, PACKED, FIN_TM, LANES), jnp.uint32),
                        pltpu.SemaphoreType.DMA((GATHER_SLOTS,))])
    return pl.pallas_call(
        _final_kernel,
        grid_spec=grid_spec,
        out_shape=jax.ShapeDtypeStruct((n, d), F32),
        compiler_params=_params(("arbitrary",)),
        name="final",
    )(dest, ys, h1, gate, g.reshape(1, d), b.reshape(1, d))


def kernel(x, ln0_g, ln0_b, w_in, b_in, rpb, w_proj_a, w_proj_b, w_o, b_o, ln1_g, ln1_b,
           w_router_group, b_router_group, w_router_expert, b_router_expert,
           w_gate, w_up, w_down, ln2_g, ln2_b):
    bsz, s, d = x.shape
    n = bsz * s
    assert d == D_MODEL and w_in.shape[0] == DEPTH
    scale = HEAD_DIM ** -0.5

    col_scale = np.ones((w_in.shape[2],), np.float32)
    col_scale[:NA_WIDTH] = scale
    col_scale[3 * NA_WIDTH:3 * NA_WIDTH + DIL_WIDTH] = scale
    w_all = (w_in[0] * col_scale).astype(BF16)
    b_all = (b_in[0] * col_scale).reshape(1, -1)
    assert DIL_OUT_WIDTH == INPROJ_TN
    na_blocks, ngroups = 3 * NA_WIDTH // INPROJ_TN, len(DIL_GROUPS)
    w_r = jnp.concatenate([w_router_group[0], w_router_expert[0]], axis=1)
    w_r = jnp.pad(w_r, ((0, 0), (0, ROUTER_PAD - w_r.shape[1])))
    w_r_hi = w_r.astype(BF16)
    w_r_cat = jnp.concatenate([w_r_hi, (w_r - w_r_hi.astype(F32)).astype(BF16)], axis=1)
    b_r = jnp.pad(jnp.concatenate([b_router_group[0], b_router_expert[0]]),
                  (0, ROUTER_PAD - N_GROUPS - N_EXPERTS)).reshape(1, ROUTER_PAD)
    slopes = _alibi_slopes(DIL_HEADS)

    x2 = x.reshape(n, d)
    hb = _ln0(x2, ln0_g, ln0_b)
    zq = [_inproj(hb, w_all, b_all, bsz, s, DIL_GROUPS[0][1], 3 * NA_WIDTH + 3 * DIL_OUT_WIDTH,
                  lambda j: jnp.where(j < na_blocks, j, na_blocks + (j - na_blocks) * ngroups))]
    for g in range(1, ngroups):
        zq.append(_inproj(hb, w_all, b_all, bsz, s, DIL_GROUPS[g][1], 3 * DIL_OUT_WIDTH,
                          lambda j, g=g: na_blocks + j * ngroups + g))
    na = _neighborhood_attention(zq[0].reshape(-1, bsz, s, HEAD_DIM), rpb[0])
    na = na.reshape(NA_HEADS, n, HEAD_DIM)
    dil = [_dilated_group(zq[g], DIL0_H if g == 0 else 0, g, slopes) for g in range(ngroups)]
    m = _mix(na, [o for o, _ in dil], [l for _, l in dil], hb,
             w_proj_a[0].astype(BF16), w_proj_b[0].astype(BF16), w_all, b_all, s)
    h1, h1p, logits = _outproj(m, x2, ln0_g, ln0_b, w_o[0].astype(BF16), b_o[0], ln1_g[0], ln1_b[0],
                               w_r_cat, b_r)
    dest, gate, slot_tok, block_e, block_next, n_active = _route(logits, n)
    ys = _experts(h1p, slot_tok, block_e, block_next, n_active, w_gate[0], w_up[0], w_down[0])
    out = _final(dest, ys, h1, gate, ln2_g[0], ln2_b[0])
    return out.reshape(bsz, s, d)
```

```python
import functools

import numpy as np
import jax
import jax.numpy as jnp
from jax import lax
from jax.experimental import pallas as pl
from jax.experimental.pallas import tpu as pltpu

F32 = jnp.float32
BF16 = jnp.bfloat16

D_MODEL = 2048
HEAD_DIM = 128
GRID_W = 64
NA_HEADS = 8
NA_KH = 8
NA_KW = 16
DIL_GROUPS = ((128, 1), (512, 4), (2048, 16))
DIL_HEADS_PER_GROUP = 4
DIL_HEADS = DIL_HEADS_PER_GROUP * len(DIL_GROUPS)
N_GROUPS = 8
EXPERTS_PER_GROUP = 8
N_EXPERTS = N_GROUPS * EXPERTS_PER_GROUP
TOP_K = 2
D_EXPERT = D_MODEL // 4
LN_EPS = 1e-5
DEPTH = 1
DN_ALPHA = (2 * DEPTH) ** 0.25
NA_WIDTH = NA_HEADS * HEAD_DIM
DIL_WIDTH = DIL_HEADS * HEAD_DIM
DIL_OUT_WIDTH = DIL_HEADS_PER_GROUP * HEAD_DIM
QKV_WIDTH = 3 * NA_WIDTH + 3 * DIL_WIDTH
QKV_HEADS = QKV_WIDTH // HEAD_DIM
LANES = 128
CHUNKS = D_MODEL // LANES
PACKED = CHUNKS // 2
MASK_VALUE = -1e30

QA_H, KA_H, VA_H = 0, NA_HEADS, 2 * NA_HEADS
DIL0_H = 3 * NA_HEADS

LN_TM = 1024
INPROJ_TM, INPROJ_TN = 2048, 512
SPLIT_STRIDE = 4
NA_QROWS = 4
NA_KROWS = 12
NA_NSUB = 16
DIL_NSUB = 16
DIL_TQ = 256
DIL_HALF = 64
DIL_TK = DIL_TQ + 2 * DIL_HALF
MIX_TM, MIX_TN = 512, 512
MIX_CHUNK = 256
OUT_TM = 512
ROUTER_PAD = 128
ROUTE_TM = 1024
MOE_TB = 128
FIN_TM = 256
CAST_VREGS = 32
GATHER_SLOTS = 3
WEIGHT_DMA_PRIORITY = 1
VMEM_LIMIT = 56 * 1024 * 1024


def _params(sem, limit=VMEM_LIMIT):
    return pltpu.CompilerParams(dimension_semantics=sem, vmem_limit_bytes=limit)


def _layer_norm(x, g, b):
    mu = jnp.mean(x, axis=-1, keepdims=True)
    xc = x - mu
    var = jnp.mean(xc * xc, axis=-1, keepdims=True)
    return xc * lax.rsqrt(var + LN_EPS) * g + b


HIGH_HALF = np.uint32(0xFFFF0000)


def _bf16_bits(x):
    return lax.bitcast_convert_type(x.astype(BF16).astype(F32), jnp.uint32)


def _pack_rows(ref, val):
    for c in range(PACKED):
        lo = _bf16_bits(val[:, c * LANES:(c + 1) * LANES])
        hi = _bf16_bits(val[:, (c + PACKED) * LANES:(c + PACKED + 1) * LANES])
        ref[c] = (lo >> 16) | (hi & HIGH_HALF)


def _unpack_rows(ref):
    words = [ref[c] for c in range(PACKED)]
    lo = [lax.bitcast_convert_type(w << 16, F32) for w in words]
    hi = [lax.bitcast_convert_type(w & HIGH_HALF, F32) for w in words]
    return jnp.concatenate(lo + hi, axis=-1)


def _tile_copies(hbm, row0, rows, vmem, sem, to_hbm, chunk_major=False):
    copies = []
    for c in range(PACKED):
        h = hbm.at[c, pl.ds(row0, rows)] if chunk_major else hbm.at[pl.ds(row0, rows), c]
        v = vmem.at[c]
        copies.append(pltpu.make_async_copy(v, h, sem) if to_hbm else pltpu.make_async_copy(h, v, sem))
    return copies


def _start_row_gather(idx_ref, base, step, count, src_hbm, dst, sem, both_threads, inline=False,
                      chunk_major=False):
    def start(j, c):
        for half in range(2):
            t = 2 * j + half
            row = idx_ref[base + t * step]
            src = src_hbm.at[:, row] if chunk_major else src_hbm.at[row]
            pltpu.make_async_copy(src, dst.at[:, t], sem).start(priority=half if both_threads else 0)
        return c

    if inline:
        for j in range(count // 2):
            start(j, 0)
    else:
        lax.fori_loop(0, count // 2, start, 0, unroll=2)


def _wait_row_gather(dst, sem):
    pltpu.make_async_copy(dst, dst, sem).wait()


def _ln0_kernel(x_ref, g_ref, b_ref, hb_ref):
    hb_ref[...] = _layer_norm(x_ref[...], g_ref[...], b_ref[...]).astype(BF16)


def _ln0(x, g, b):
    n, d = x.shape
    row = pl.BlockSpec((LN_TM, d), lambda i: (i, 0))
    vec = pl.BlockSpec((1, d), lambda i: (0, 0))
    return pl.pallas_call(
        _ln0_kernel,
        grid=(n // LN_TM,),
        in_specs=[row, vec, vec],
        out_specs=row,
        out_shape=jax.ShapeDtypeStruct((n, d), BF16),
        compiler_params=_params(("parallel",)),
        name="ln0",
    )(x, g.reshape(1, d), b.reshape(1, d))


def _inproj_kernel(hb_ref, w_ref, b_ref, o_ref, acc_s, mid_s, *, dil):
    acc = jnp.dot(hb_ref[...], w_ref[...], preferred_element_type=F32) + b_ref[...]
    if dil == 1:
        for c in range(INPROJ_TN // LANES):
            o_ref[c, 0, 0] = acc[:, c * LANES:(c + 1) * LANES].astype(BF16)
        return
    for c in range(INPROJ_TN // LANES):
        acc_s[c] = acc[:, c * LANES:(c + 1) * LANES]
    if dil <= SPLIT_STRIDE:
        for c in range(INPROJ_TN // LANES):
            for r in range(dil):
                o_ref[c, 0, r] = acc_s[c, pl.ds(r, INPROJ_TM // dil, stride=dil), :].astype(BF16)
        return
    outer = dil // SPLIT_STRIDE
    for c in range(INPROJ_TN // LANES):
        for a in range(SPLIT_STRIDE):
            mid_s[c, a] = acc_s[c, pl.ds(a, INPROJ_TM // SPLIT_STRIDE, stride=SPLIT_STRIDE), :]
    for c in range(INPROJ_TN // LANES):
        for a in range(SPLIT_STRIDE):
            for b in range(outer):
                o_ref[c, 0, b * SPLIT_STRIDE + a] = (
                    mid_s[c, a, pl.ds(b, INPROJ_TM // dil, stride=outer), :].astype(BF16))


def _inproj(hb, w, b, bsz, s, dil, width, col_block):
    n, d = hb.shape
    tiles = s // INPROJ_TM
    assert INPROJ_TM % (16 * dil) == 0 and s % INPROJ_TM == 0 and width % INPROJ_TN == 0
    assert dil <= SPLIT_STRIDE or dil % SPLIT_STRIDE == 0
    mid_rows = INPROJ_TM // SPLIT_STRIDE if dil > SPLIT_STRIDE else 8
    return pl.pallas_call(
        functools.partial(_inproj_kernel, dil=dil),
        grid=(n // INPROJ_TM, width // INPROJ_TN),
        in_specs=[pl.BlockSpec((INPROJ_TM, d), lambda i, j: (i, 0)),
                  pl.BlockSpec((d, INPROJ_TN), lambda i, j: (0, col_block(j))),
                  pl.BlockSpec((1, INPROJ_TN), lambda i, j: (0, col_block(j)))],
        out_specs=pl.BlockSpec((INPROJ_TN // LANES, 1, dil, INPROJ_TM // dil, LANES),
                               lambda i, j: (j, i // tiles, 0, i % tiles, 0)),
        out_shape=jax.ShapeDtypeStruct((width // LANES, bsz, dil, s // dil, LANES), BF16),
        scratch_shapes=[pltpu.VMEM((INPROJ_TN // LANES, INPROJ_TM, LANES), F32),
                        pltpu.VMEM((INPROJ_TN // LANES, SPLIT_STRIDE, mid_rows, LANES), F32)],
        compiler_params=_params(("parallel", "arbitrary")),
        name=f"inproj_{dil}",
    )(hb, w, b)


def _na_bias_tiles(rpb, rows):
    heads, n_dr, n_dc = rpb.shape
    qc = np.arange(GRID_W)[:, None]
    kc = np.arange(GRID_W)[None, :]
    qcs = np.clip(qc - NA_KW // 2, 0, GRID_W - NA_KW)
    v_col = (kc >= qcs) & (kc < qcs + NA_KW)
    dc = np.clip(kc - qc + NA_KW - 1, 0, n_dc - 1)
    onehot = (dc[None] == np.arange(n_dc)[:, None, None]) & v_col[None]
    toep = jnp.einsum('hrd,dqk->hrqk', rpb.astype(F32), jnp.asarray(onehot, F32),
                      precision=lax.Precision.HIGHEST)
    toep = jnp.where(jnp.asarray(v_col), toep, MASK_VALUE)
    masked = jnp.full((heads, 1, GRID_W, GRID_W), MASK_VALUE, F32)
    blocks = jnp.concatenate([toep, masked], axis=1)
    i = np.arange(NA_QROWS)[:, None]
    j = np.arange(NA_KROWS)[None, :]
    sel = []
    for r0, ks in ((0, 0), (2 * NA_QROWS, 2 * NA_QROWS - NA_KH // 2), (rows - NA_QROWS, rows - NA_KROWS)):
        r, krow = r0 + i, ks + j
        start = np.clip(r - NA_KH // 2, 0, rows - NA_KH)
        v_row = (krow >= start) & (krow < start + NA_KH)
        sel.append(np.where(v_row, krow - r + NA_KH - 1, n_dr))
    sel = np.stack(sel)
    pairs = sel.reshape(3, NA_QROWS, NA_KROWS // 2, 2)
    distinct = sorted({tuple(p) for p in pairs.reshape(-1, 2).tolist()})
    plan = [[[distinct.index(tuple(pairs[v, i, c])) for c in range(NA_KROWS // 2)]
             for i in range(NA_QROWS)] for v in range(3)]
    left = jnp.take(blocks, jnp.asarray([p[0] for p in distinct], jnp.int32), axis=1)
    right = jnp.take(blocks, jnp.asarray([p[1] for p in distinct], jnp.int32), axis=1)
    return jnp.concatenate([left, right], axis=-1), plan


def _attend_blocks(nblocks, scores, finish):
    scored, weighted = {}, {}
    for t in range(nblocks + 2):
        if t < nblocks:
            scored[t] = scores(t)
        if 0 <= t - 1 < nblocks:
            s, vw = scored.pop(t - 1)
            m = jnp.max(s, axis=-1, keepdims=True)
            p = jnp.exp(s - m)
            weighted[t - 1] = (p.astype(BF16), vw, m, jnp.sum(p, axis=-1, keepdims=True))
        if 0 <= t - 2 < nblocks:
            p, vw, m, l = weighted.pop(t - 2)
            finish(t - 2, jnp.dot(p, vw, preferred_element_type=F32) / l, m, l)


def _scores(q, kw, bias):
    return lax.dot_general(q, kw, (((1,), (1,)), ((), ())), preferred_element_type=F32) + bias


def _edge_variant(blk, nblk):
    return jnp.where(blk == 0, 0, jnp.where(blk == nblk - 1, 2, 1))


def _na_kernel(q_ref, k_ref, v_ref, tile_ref, o_ref, tab_ref, *, rows, plan):
    tq = NA_QROWS * GRID_W

    @pl.when(pl.program_id(2) == 0)
    def _():
        for v in range(3):
            for i in range(NA_QROWS):
                for c in range(NA_KROWS // 2):
                    tab_ref[v, 0, i * GRID_W:(i + 1) * GRID_W, c * 2 * GRID_W:(c + 1) * 2 * GRID_W] = (
                        tile_ref[0, plan[v][i][c]])

    def scores(u):
        blk = pl.program_id(2) * NA_NSUB + u
        ks = jnp.clip(blk * NA_QROWS - NA_KH // 2, 0, rows - NA_KROWS)
        off = pl.multiple_of(ks * GRID_W, GRID_W)
        kw = k_ref[0, 0, pl.ds(off, NA_KROWS * GRID_W), :]
        vw = v_ref[0, 0, pl.ds(off, NA_KROWS * GRID_W), :]
        bias = tab_ref[_edge_variant(blk, rows // NA_QROWS), 0]
        return _scores(q_ref[0, 0, u * tq:(u + 1) * tq, :], kw, bias), vw

    def finish(u, o, m, l):
        o_ref[0, 0, u * tq:(u + 1) * tq, :] = o.astype(BF16)

    _attend_blocks(NA_NSUB, scores, finish)


def _neighborhood_attention(zq4, rpb):
    _, bsz, s, _ = zq4.shape
    rows = s // GRID_W
    assert rows % (NA_QROWS * NA_NSUB) == 0 and rows >= NA_KROWS + NA_QROWS and NA_KROWS % 2 == 0
    tiles, plan = _na_bias_tiles(rpb, rows)
    tq = NA_QROWS * GRID_W * NA_NSUB
    return pl.pallas_call(
        functools.partial(_na_kernel, rows=rows, plan=plan),
        grid=(bsz, NA_HEADS, s // tq),
        in_specs=[pl.BlockSpec((1, 1, tq, HEAD_DIM), lambda b, h, i: (QA_H + h, b, i, 0)),
                  pl.BlockSpec((1, 1, s, HEAD_DIM), lambda b, h, i: (KA_H + h, b, 0, 0)),
                  pl.BlockSpec((1, 1, s, HEAD_DIM), lambda b, h, i: (VA_H + h, b, 0, 0)),
                  pl.BlockSpec((1,) + tiles.shape[1:], lambda b, h, i: (h, 0, 0, 0))],
        out_specs=pl.BlockSpec((1, 1, tq, HEAD_DIM), lambda b, h, i: (h, b, i, 0)),
        out_shape=jax.ShapeDtypeStruct((NA_HEADS, bsz, s, HEAD_DIM), BF16),
        scratch_shapes=[pltpu.VMEM((3, 1, NA_QROWS * GRID_W, NA_KROWS * GRID_W), F32)],
        compiler_params=_params(("arbitrary", "arbitrary", "arbitrary")),
        name="na_attn",
    )(zq4, zq4, zq4, tiles)


def _alibi_slopes(n):
    return np.array([2.0 ** (-8.0 * (i + 1) / n) for i in range(n)], dtype=np.float32)


def _dil_bias_tables(slopes, dil):
    qi = np.arange(DIL_TQ)[:, None]
    kj = np.arange(DIL_TK)[None, :]
    dist = np.stack([np.abs(kj - qi + shift) for shift in (0, -DIL_HALF, -2 * DIL_HALF)])
    dist = jnp.asarray(dist, F32)[:, None]
    penalty = jnp.asarray(slopes, F32)[None, :, None, None] * (dist * float(dil))
    return jnp.where(dist <= DIL_HALF, -penalty, MASK_VALUE)


def _dil_kernel(q_ref, k_ref, v_ref, tab_ref, o_ref, lse_ref, *, length, nsub, nres):
    def scores(u):
        res, sub = divmod(u, nsub)
        blk = pl.program_id(3) * nsub + sub
        ks = jnp.clip(blk * DIL_TQ - DIL_HALF, 0, length - DIL_TK)
        off = pl.multiple_of(ks, DIL_HALF)
        kw = k_ref[0, 0, res, pl.ds(off, DIL_TK), :]
        vw = v_ref[0, 0, res, pl.ds(off, DIL_TK), :]
        bias = tab_ref[_edge_variant(blk, length // DIL_TQ), 0]
        return _scores(q_ref[0, 0, res, sub * DIL_TQ:(sub + 1) * DIL_TQ, :], kw, bias), vw

    def finish(u, o, m, l):
        res, sub = divmod(u, nsub)
        rows = slice(sub * DIL_TQ, (sub + 1) * DIL_TQ)
        o_ref[0, 0, res, rows, :] = o.astype(BF16)
        lse_ref[0, 0, res, rows, :] = jnp.broadcast_to(m + jnp.log(l), (DIL_TQ, LANES))

    _attend_blocks(nres * nsub, scores, finish)


def _dilated_group(zq, base, g, slopes):
    window, dil = DIL_GROUPS[g]
    assert (window // 2) // dil == DIL_HALF
    _, bsz, _, length, _ = zq.shape
    nsub = min(DIL_NSUB, length // DIL_TQ)
    nres = min(dil, DIL_NSUB // nsub)
    assert zq.shape[2] == dil and length % (DIL_TQ * nsub) == 0 and length >= DIL_TK and dil % nres == 0
    gh = DIL_HEADS_PER_GROUP
    tabs = _dil_bias_tables(slopes[g * gh:(g + 1) * gh], dil)
    tq = DIL_TQ * nsub
    qspec = pl.BlockSpec((1, 1, nres, tq, HEAD_DIM), lambda b, h, r, n: (base + h, b, r, n, 0))
    kvspec = lambda off: pl.BlockSpec((1, 1, nres, length, HEAD_DIM),
                                      lambda b, h, r, n: (base + off + h, b, r, 0, 0))
    ospec = pl.BlockSpec((1, 1, nres, tq, HEAD_DIM), lambda b, h, r, n: (h, b, r, n, 0))
    return pl.pallas_call(
        functools.partial(_dil_kernel, length=length, nsub=nsub, nres=nres),
        grid=(bsz, gh, dil // nres, length // tq),
        in_specs=[qspec, kvspec(gh), kvspec(2 * gh),
                  pl.BlockSpec((3, 1, DIL_TQ, DIL_TK), lambda b, h, r, n: (0, h, 0, 0))],
        out_specs=[ospec, ospec],
        out_shape=[jax.ShapeDtypeStruct((gh, bsz, dil, length, HEAD_DIM), BF16),
                   jax.ShapeDtypeStruct((gh, bsz, dil, length, HEAD_DIM), F32)],
        compiler_params=_params(("parallel", "parallel", "parallel", "arbitrary")),
        name=f"dil_attn_{dil}",
    )(zq, zq, zq, tabs)


def _mix_kernel(na_ref, o1_ref, o2_ref, o3_ref, l1_ref, l2_ref, l3_ref, hb_ref,
                wpa_ref, wpb_ref, wga_ref, wgb_ref, bga_ref, bgb_ref, m_ref, na_s, dil_s, o_s, l_s, mid_s):
    @pl.when(pl.program_id(1) == 0)
    def _():
        for h in range(NA_HEADS):
            na_s[:, h * HEAD_DIM:(h + 1) * HEAD_DIM] = na_ref[h]
        for g, (o_ref, l_ref) in enumerate(((o1_ref, l1_ref), (o2_ref, l2_ref), (o3_ref, l3_ref))):
            dil = DIL_GROUPS[g][1]
            for h in range(DIL_HEADS_PER_GROUP):
                for which, (src, dst) in enumerate(((o_ref, o_s), (l_ref, l_s))):
                    if dil <= SPLIT_STRIDE:
                        for r in range(dil):
                            rows = slice(None) if dil == 1 else pl.ds(r, MIX_TM // dil, stride=dil)
                            dst[g, h, rows, :] = src[h, 0, r].astype(F32)
                        continue
                    outer = dil // SPLIT_STRIDE
                    for r in range(dil):
                        a, b = r % SPLIT_STRIDE, r // SPLIT_STRIDE
                        mid_s[h, which, a, pl.ds(b, MIX_TM // dil, stride=outer), :] = src[h, 0, r].astype(F32)
                    for a in range(SPLIT_STRIDE):
                        dst[g, h, pl.ds(a, MIX_TM // SPLIT_STRIDE, stride=SPLIT_STRIDE), :] = mid_s[h, which, a]
        for h in range(DIL_HEADS_PER_GROUP):
            ls = [l_s[g, h] for g in range(len(DIL_GROUPS))]
            mx = jnp.maximum(jnp.maximum(ls[0], ls[1]), ls[2])
            es = [jnp.exp(l - mx) for l in ls]
            den = es[0] + es[1] + es[2]
            acc = (es[0] * o_s[0, h] + es[1] * o_s[1, h] + es[2] * o_s[2, h]) / den
            dil_s[:, h * HEAD_DIM:(h + 1) * HEAD_DIM] = acc.astype(BF16)

    hb = hb_ref[...]
    na, dm = na_s[...], dil_s[...]

    def products(c):
        cols = slice(c * MIX_CHUNK, (c + 1) * MIX_CHUNK)
        return (jnp.dot(na, wpa_ref[:, cols], preferred_element_type=F32),
                jnp.dot(dm, wpb_ref[:, cols], preferred_element_type=F32),
                jnp.dot(hb, wga_ref[:, cols], preferred_element_type=F32) + bga_ref[:, cols],
                jnp.dot(hb, wgb_ref[:, cols], preferred_element_type=F32) + bgb_ref[:, cols])

    nchunks = MIX_TN // MIX_CHUNK
    pending = products(0)
    for c in range(nchunks):
        following = products(c + 1) if c + 1 < nchunks else None
        ya, yb, ga, gb = pending
        m_ref[:, c * MIX_CHUNK:(c + 1) * MIX_CHUNK] = (
            jax.nn.sigmoid(ga) * ya + jax.nn.sigmoid(gb) * yb).astype(BF16)
        pending = following


def _mix(na, dil_o, dil_lse, hb, wpa, wpb, w_all, b_all, s):
    n, d = hb.shape
    tm, tn = MIX_TM, MIX_TN
    tiles = s // tm
    gh = DIL_HEADS_PER_GROUP
    ngroups = len(DIL_GROUPS)
    assert all(tm % (16 * dil) == 0 for _, dil in DIL_GROUPS) and s % tm == 0
    col = lambda k: pl.BlockSpec((k, tn), lambda i, j: (0, j))
    ga0, gb0 = QKV_WIDTH // tn, (QKV_WIDTH + d) // tn
    gate = lambda k, first: pl.BlockSpec((k, tn), lambda i, j: (0, first + j))
    grp = lambda dil: pl.BlockSpec((gh, 1, dil, tm // dil, HEAD_DIM),
                                   lambda i, j: (0, i // tiles, 0, i % tiles, 0))
    groups = [grp(dil) for _, dil in DIL_GROUPS]
    return pl.pallas_call(
        _mix_kernel,
        grid=(n // tm, d // tn),
        in_specs=[pl.BlockSpec((NA_HEADS, tm, HEAD_DIM), lambda i, j: (0, i, 0)), *groups, *groups,
                  pl.BlockSpec((tm, d), lambda i, j: (i, 0)),
                  col(NA_WIDTH), col(DIL_OUT_WIDTH),
                  gate(d, ga0), gate(d, gb0), gate(1, ga0), gate(1, gb0)],
        out_specs=pl.BlockSpec((tm, tn), lambda i, j: (i, j)),
        out_shape=jax.ShapeDtypeStruct((n, d), BF16),
        scratch_shapes=[pltpu.VMEM((tm, NA_WIDTH), BF16), pltpu.VMEM((tm, DIL_OUT_WIDTH), BF16),
                        pltpu.VMEM((ngroups, gh, tm, HEAD_DIM), F32),
                        pltpu.VMEM((ngroups, gh, tm, HEAD_DIM), F32),
                        pltpu.VMEM((gh, 2, SPLIT_STRIDE, tm // SPLIT_STRIDE, HEAD_DIM), F32)],
        compiler_params=_params(("parallel", "arbitrary")),
        name="mix",
    )(na, *dil_o, *dil_lse, hb, wpa, wpb, w_all, w_all, b_all, b_all)


def _outproj_kernel(m_ref, x_ref, g0_ref, b0_ref, wo_ref, bo_ref, g_ref, b_ref, wr_ref, br_ref,
                    h1_ref, h1p_hbm, lg_ref, hbuf, sem):
    i = pl.program_id(0)
    slot = i % 2

    def out_copies(step, sl):
        return _tile_copies(h1p_hbm, step * OUT_TM, OUT_TM, hbuf.at[sl], sem.at[sl], True)

    mix = jnp.dot(m_ref[...], wo_ref[...], preferred_element_type=F32) + bo_ref[...]
    h = _layer_norm(x_ref[...], g0_ref[...], b0_ref[...])
    h1 = _layer_norm(DN_ALPHA * h + mix, g_ref[...], b_ref[...])
    h1_ref[...] = h1

    @pl.when(i >= 2)
    def _():
        for cp in out_copies(i - 2, slot):
            cp.wait()

    _pack_rows(hbuf.at[slot], h1)
    for cp in out_copies(i, slot):
        cp.start()

    @pl.when(i == pl.num_programs(0) - 1)
    def _():
        for cp in out_copies(i, slot):
            cp.wait()

        @pl.when(i >= 1)
        def _():
            for cp in out_copies(i - 1, 1 - slot):
                cp.wait()

    hi = h1.astype(BF16)
    lo = (h1 - hi.astype(F32)).astype(BF16)
    both = jnp.dot(hi, wr_ref[...], preferred_element_type=F32)
    lg = (both[:, :ROUTER_PAD] + both[:, ROUTER_PAD:]
          + jnp.dot(lo, wr_ref[:, :ROUTER_PAD], preferred_element_type=F32))
    lg_ref[...] = lg + br_ref[...]


def _outproj(m, x, g0, b0, wo, bo, g, b, wr, br):
    n, d = x.shape
    tm = OUT_TM
    row = lambda w: pl.BlockSpec((tm, w), lambda i: (i, 0))
    full = lambda r, c: pl.BlockSpec((r, c), lambda i: (0, 0), pipeline_mode=pl.Buffered(1))
    return pl.pallas_call(
        _outproj_kernel,
        grid=(n // tm,),
        in_specs=[row(d), row(d), full(1, d), full(1, d), full(d, d), full(1, d), full(1, d), full(1, d),
                  full(d, 2 * ROUTER_PAD), full(1, ROUTER_PAD)],
        out_specs=[row(d), pl.BlockSpec(memory_space=pl.ANY), row(ROUTER_PAD)],
        out_shape=[jax.ShapeDtypeStruct((n, d), F32),
                   jax.ShapeDtypeStruct((n, PACKED, LANES), jnp.uint32),
                   jax.ShapeDtypeStruct((n, ROUTER_PAD), F32)],
        scratch_shapes=[pltpu.VMEM((2, PACKED, tm, LANES), jnp.uint32), pltpu.SemaphoreType.DMA((2,))],
        compiler_params=_params(("arbitrary",)),
        name="outproj",
    )(m, x, g0.reshape(1, d), b0.reshape(1, d), wo, bo.reshape(1, d), g.reshape(1, d), b.reshape(1, d),
      wr, br)


def _first_argmax(vals, lane_f):
    top = jnp.max(vals, axis=-1, keepdims=True)
    idx = jnp.min(jnp.where(vals == top, lane_f, float(LANES)), axis=-1, keepdims=True)
    return top, idx


def _route_kernel(lg_ref, gate_ref, dest_ref, cnt_ref, base_s, start_s):
    phase = pl.program_id(0)

    @pl.when(pl.program_id(1) == 0)
    def _():
        @pl.when(phase == 1)
        def _():
            counts = base_s[...]
            cnt_ref[...] = counts
            blocks = jnp.ceil(counts * (1.0 / MOE_TB))
            before = (lax.broadcasted_iota(jnp.int32, (ROUTER_PAD, ROUTER_PAD), 0)
                      < lax.broadcasted_iota(jnp.int32, (ROUTER_PAD, ROUTER_PAD), 1)).astype(BF16)
            starts = jnp.dot(jnp.broadcast_to(blocks, (8, ROUTER_PAD)).astype(BF16), before,
                             preferred_element_type=F32)
            start_s[...] = starts[0:1] * float(MOE_TB)

        base_s[...] = jnp.zeros(base_s.shape, F32)

    lg = lg_ref[...]
    lane = lax.broadcasted_iota(jnp.int32, lg.shape, 1)
    lane_f = lane.astype(F32)
    g_mask = lane < N_GROUPS
    g_top, g_sel = _first_argmax(jnp.where(g_mask, lg, MASK_VALUE), lane_f)
    g_prob = 1.0 / jnp.sum(jnp.where(g_mask, jnp.exp(lg - g_top), 0.0), axis=-1, keepdims=True)
    first = N_GROUPS + g_sel * EXPERTS_PER_GROUP
    e_mask = (lane_f >= first) & (lane_f < first + EXPERTS_PER_GROUP)
    el = jnp.where(e_mask, lg, MASK_VALUE)
    v0, i0 = _first_argmax(el, lane_f)
    v1, i1 = _first_argmax(jnp.where(lane_f == i0, MASK_VALUE, el), lane_f)
    e1 = jnp.exp(v1 - v0)
    w0 = g_prob / (1.0 + e1)
    w1 = g_prob * e1 / (1.0 + e1)

    onehots = [lane_f == idx - N_GROUPS for idx in (i0, i1)]

    @pl.when(phase == 0)
    def _():
        base_s[...] = base_s[...] + sum(jnp.sum(oh.astype(F32), axis=0, keepdims=True) for oh in onehots)

    @pl.when(phase == 1)
    def _():
        tm = lg.shape[0]
        tri = (lax.broadcasted_iota(jnp.int32, (tm, tm), 1)
               < lax.broadcasted_iota(jnp.int32, (tm, tm), 0)).astype(BF16)
        dests = []
        for onehot in onehots:
            before = jnp.dot(tri, onehot.astype(BF16), preferred_element_type=F32) + base_s[...]
            dests.append(jnp.sum(jnp.where(onehot, before + start_s[...], 0.0), axis=-1, keepdims=True))
            base_s[...] = base_s[...] + jnp.sum(onehot.astype(F32), axis=0, keepdims=True)
        gate_ref[...] = jnp.where(lane == 0, w0, jnp.where(lane == 1, w1, 0.0))
        cols = jnp.where(lane == 0, dests[0], jnp.where(lane == 1, dests[1], 0.0))
        dest_ref[...] = jnp.transpose(cols)[0:8, :].astype(jnp.int32)


def _invert_kernel(dest_ref, lo_ref, hi_ref, tok_ref, *, n):
    def clear_range(e, c):
        def clear(s, c2):
            tok_ref[s] = 0
            return c2

        lax.fori_loop(lo_ref[e], hi_ref[e], clear, 0)
        return c

    lax.fori_loop(0, lo_ref.shape[0], clear_range, 0)
    def place(t, c):
        for k in range(TOP_K):
            tok_ref[dest_ref[k * n + t]] = t
        return c

    lax.fori_loop(0, n, place, 0, unroll=8)


def _invert_slots(dest, unused_lo, unused_hi, n, slots):
    assert dest.shape[0] == TOP_K * n
    return pl.pallas_call(
        functools.partial(_invert_kernel, n=n),
        grid_spec=pltpu.PrefetchScalarGridSpec(
            num_scalar_prefetch=3, grid=(1,), in_specs=[],
            out_specs=pl.BlockSpec(memory_space=pltpu.SMEM)),
        out_shape=jax.ShapeDtypeStruct((slots,), jnp.int32),
        compiler_params=_params(("arbitrary",)),
        name="invert_slots",
    )(dest, unused_lo, unused_hi)


def _route(logits, n):
    tm = ROUTE_TM
    one = pl.BlockSpec((1, ROUTER_PAD), lambda p, i: (0, 0))
    gate, dest, counts = pl.pallas_call(
        _route_kernel,
        grid=(2, n // tm),
        in_specs=[pl.BlockSpec((tm, ROUTER_PAD), lambda p, i: (i, 0))],
        out_specs=[pl.BlockSpec((tm, ROUTER_PAD), lambda p, i: (i * p, 0)),
                   pl.BlockSpec((8, tm), lambda p, i: (0, i * p)), one],
        out_shape=[jax.ShapeDtypeStruct((n, ROUTER_PAD), F32),
                   jax.ShapeDtypeStruct((8, n), jnp.int32),
                   jax.ShapeDtypeStruct((1, ROUTER_PAD), F32)],
        scratch_shapes=[pltpu.VMEM((1, ROUTER_PAD), F32), pltpu.VMEM((1, ROUTER_PAD), F32)],
        compiler_params=_params(("arbitrary", "arbitrary")),
        name="route",
    )(logits)
    gate = gate[:, :TOP_K]
    dest = dest[:TOP_K].reshape(TOP_K * n)
    a = n * TOP_K
    counts = counts[0, :N_EXPERTS].astype(jnp.int32)
    pcounts = (counts + MOE_TB - 1) // MOE_TB * MOE_TB
    pends = jnp.cumsum(pcounts)
    nb = a // MOE_TB + N_EXPERTS
    slots = nb * MOE_TB
    unused_lo = jnp.concatenate([pends - pcounts + counts, pends[-1:]])
    unused_hi = jnp.concatenate([pends, jnp.full((1,), slots, jnp.int32)])
    slot_tok = _invert_slots(dest, unused_lo, unused_hi, n, slots)
    block_start = jnp.arange(nb, dtype=jnp.int32) * MOE_TB
    block_e = jnp.minimum(jnp.sum((pends[None, :] <= block_start[:, None]).astype(jnp.int32), axis=1),
                          N_EXPERTS - 1)
    n_active = (pends[-1] // MOE_TB).astype(jnp.int32).reshape(1)
    ids = jnp.arange(N_EXPERTS, dtype=jnp.int32)
    later = (counts[None, :] > 0) & (ids[None, :] > ids[:, None])
    nxt = jnp.min(jnp.where(later, ids[None, :], N_EXPERTS), axis=1)
    nxt = jnp.where(nxt == N_EXPERTS, -1, nxt)
    block_next = jnp.sum(jnp.where(block_e[:, None] == ids[None, :], nxt[None, :], 0), axis=1).astype(jnp.int32)
    return dest, gate, slot_tok, block_e, block_next, n_active


def _cast_weights(src, dst):
    rows, cols = src.shape
    step = CAST_VREGS * 8 * LANES // cols

    def body(r, c):
        sl = pl.ds(pl.multiple_of(r * step, step), step)
        dst[sl, :] = src[sl, :].astype(BF16)
        return c

    lax.fori_loop(0, rows // step, body, 0, unroll=2)


def _expert_kernel(be_ref, nx_ref, nact_ref, tok_ref, h1p_hbm, wg_hbm, wu_hbm, wd_hbm, ys_hbm,
                   xbuf, ybuf, wg_f, wu_f, wd_f, wg_s, wu_s, wd_s, gsem, osem, wsem):
    i = pl.program_id(0)
    slot = i % 2
    xslot = i % GATHER_SLOTS
    nact = nact_ref[0]

    def gather(step, inline=False):
        sl = step % GATHER_SLOTS
        _start_row_gather(tok_ref, step * MOE_TB, 1, MOE_TB, h1p_hbm, xbuf.at[sl], gsem.at[sl], False,
                          inline)

    def out_copies(step, sl):
        return _tile_copies(ys_hbm, step * MOE_TB, MOE_TB, ybuf.at[sl], osem.at[sl], True, chunk_major=True)

    def weight_copies(e):
        return [pltpu.make_async_copy(src.at[e], dst, wsem)
                for src, dst in ((wg_hbm, wg_f), (wu_hbm, wu_f), (wd_hbm, wd_f))]

    @pl.when((i == 0) & (nact > 0))
    def _():
        for cp in weight_copies(be_ref[0]):
            cp.start(priority=WEIGHT_DMA_PRIORITY)
        gather(0)

    @pl.when((i == 0) & (nact > 1))
    def _():
        gather(1)

    @pl.when(i >= 2)
    def _():
        for cp in out_copies(i - 2, slot):
            cp.wait()

    @pl.when(i < nact)
    def _():
        expert = be_ref[i]

        @pl.when((i == 0) | (expert != be_ref[jnp.maximum(i - 1, 0)]))
        def _():
            for cp in weight_copies(expert):
                cp.wait()
            _cast_weights(wg_f, wg_s)
            _cast_weights(wu_f, wu_s)
            _cast_weights(wd_f, wd_s)

            @pl.when(nx_ref[i] >= 0)
            def _():
                for cp in weight_copies(nx_ref[i]):
                    cp.start(priority=WEIGHT_DMA_PRIORITY)

        _wait_row_gather(xbuf.at[xslot], gsem.at[xslot])

        def block(prefetch):
            if prefetch:
                gather(i + 2, inline=True)
            xb = _unpack_rows(xbuf.at[xslot]).astype(BF16)
            gate = jnp.dot(xb, wg_s[...], preferred_element_type=F32)
            up = jnp.dot(xb, wu_s[...], preferred_element_type=F32)
            hid = (jax.nn.silu(gate) * up).astype(BF16)
            _pack_rows(ybuf.at[slot], jnp.dot(hid, wd_s[...], preferred_element_type=F32))

        pl.when(i + 2 < nact)(functools.partial(block, True))
        pl.when(i + 2 >= nact)(functools.partial(block, False))

    @pl.when(i >= nact)
    def _():
        ybuf[slot] = jnp.zeros(ybuf.shape[1:], jnp.uint32)

    for cp in out_copies(i, slot):
        cp.start()

    @pl.when(i == pl.num_programs(0) - 1)
    def _():
        for cp in out_copies(i, slot):
            cp.wait()

        @pl.when(i >= 1)
        def _():
            for cp in out_copies(i - 1, 1 - slot):
                cp.wait()


def _experts(h1p, slot_tok, block_e, block_next, n_active, w_gate, w_up, w_down):
    d = D_MODEL
    slots = slot_tok.shape[0]
    nb = slots // MOE_TB
    any_spec = pl.BlockSpec(memory_space=pl.ANY)
    grid_spec = pltpu.PrefetchScalarGridSpec(
        num_scalar_prefetch=4,
        grid=(nb,),
        in_specs=[any_spec, any_spec, any_spec, any_spec],
        out_specs=any_spec,
        scratch_shapes=[pltpu.VMEM((GATHER_SLOTS, PACKED, MOE_TB, LANES), jnp.uint32),
                        pltpu.VMEM((2, PACKED, MOE_TB, LANES), jnp.uint32),
                        pltpu.VMEM((d, D_EXPERT), F32), pltpu.VMEM((d, D_EXPERT), F32),
                        pltpu.VMEM((D_EXPERT, d), F32),
                        pltpu.VMEM((d, D_EXPERT), BF16), pltpu.VMEM((d, D_EXPERT), BF16),
                        pltpu.VMEM((D_EXPERT, d), BF16),
                        pltpu.SemaphoreType.DMA((GATHER_SLOTS,)), pltpu.SemaphoreType.DMA((2,)),
                        pltpu.SemaphoreType.DMA(())])
    return pl.pallas_call(
        _expert_kernel,
        grid_spec=grid_spec,
        out_shape=jax.ShapeDtypeStruct((PACKED, slots, LANES), jnp.uint32),
        compiler_params=_params(("arbitrary",)),
        name="experts",
    )(block_e, block_next, n_active, slot_tok, h1p, w_gate, w_up, w_down)


def _final_kernel(dest_ref, ys_hbm, h1_ref, gate_ref, g_ref, b_ref, o_ref, ybuf, gsem):
    i = pl.program_id(0)
    steps = pl.num_programs(0)
    slot = i % GATHER_SLOTS

    def fetch(step, inline=False):
        sl = step % GATHER_SLOTS
        for k in range(TOP_K):
            _start_row_gather(dest_ref, k * steps * FIN_TM + step * FIN_TM, 1, FIN_TM, ys_hbm,
                              ybuf.at[sl, k], gsem.at[sl], True, inline, chunk_major=True)

    @pl.when(i == 0)
    def _():
        fetch(0)

    @pl.when((i == 0) & (steps > 1))
    def _():
        fetch(1)

    _wait_row_gather(ybuf.at[slot], gsem.at[slot])

    def combine(prefetch):
        if prefetch:
            fetch(i + 2, inline=True)
        gate = gate_ref[...]
        ffn = (_unpack_rows(ybuf.at[slot, 0]) * gate[:, 0:1]
               + _unpack_rows(ybuf.at[slot, 1]) * gate[:, 1:2])
        o_ref[...] = _layer_norm(DN_ALPHA * h1_ref[...] + ffn, g_ref[...], b_ref[...])

    pl.when(i + 2 < steps)(functools.partial(combine, True))
    pl.when(i + 2 >= steps)(functools.partial(combine, False))


def _final(dest, ys, h1, gate, g, b):
    n, d = h1.shape
    grid_spec = pltpu.PrefetchScalarGridSpec(
        num_scalar_prefetch=1,
        grid=(n // FIN_TM,),
        in_specs=[pl.BlockSpec(memory_space=pl.ANY),
                  pl.BlockSpec((FIN_TM, d), lambda i, ds: (i, 0)),
                  pl.BlockSpec((FIN_TM, TOP_K), lambda i, ds: (i, 0)),
                  pl.BlockSpec((1, d), lambda i, ds: (0, 0)),
                  pl.BlockSpec((1, d), lambda i, ds: (0, 0))],
        out_specs=pl.BlockSpec((FIN_TM, d), lambda i, ds: (i, 0)),
        scratch_shapes=[pltpu.VMEM((GATHER_SLOTS, TOP_K, PACKED, FIN_TM, LANES), jnp.uint32),
                        pltpu.SemaphoreType.DMA((GATHER_SLOTS,))])
    return pl.pallas_call(
        _final_kernel,
        grid_spec=grid_spec,
        out_shape=jax.ShapeDtypeStruct((n, d), F32),
        compiler_params=_params(("arbitrary",)),
        name="final",
    )(dest, ys, h1, gate, g.reshape(1, d), b.reshape(1, d))


def kernel(x, ln0_g, ln0_b, w_in, b_in, rpb, w_proj_a, w_proj_b, w_o, b_o, ln1_g, ln1_b,
           w_router_group, b_router_group, w_router_expert, b_router_expert,
           w_gate, w_up, w_down, ln2_g, ln2_b):
    bsz, s, d = x.shape
    n = bsz * s
    assert d == D_MODEL and w_in.shape[0] == DEPTH
    scale = HEAD_DIM ** -0.5

    col_scale = np.ones((w_in.shape[2],), np.float32)
    col_scale[:NA_WIDTH] = scale
    col_scale[3 * NA_WIDTH:3 * NA_WIDTH + DIL_WIDTH] = scale
    w_all = (w_in[0] * col_scale).astype(BF16)
    b_all = (b_in[0] * col_scale).reshape(1, -1)
    assert DIL_OUT_WIDTH == INPROJ_TN
    na_blocks, ngroups = 3 * NA_WIDTH // INPROJ_TN, len(DIL_GROUPS)
    w_r = jnp.concatenate([w_router_group[0], w_router_expert[0]], axis=1)
    w_r = jnp.pad(w_r, ((0, 0), (0, ROUTER_PAD - w_r.shape[1])))
    w_r_hi = w_r.astype(BF16)
    w_r_cat = jnp.concatenate([w_r_hi, (w_r - w_r_hi.astype(F32)).astype(BF16)], axis=1)
    b_r = jnp.pad(jnp.concatenate([b_router_group[0], b_router_expert[0]]),
                  (0, ROUTER_PAD - N_GROUPS - N_EXPERTS)).reshape(1, ROUTER_PAD)
    slopes = _alibi_slopes(DIL_HEADS)

    x2 = x.reshape(n, d)
    hb = _ln0(x2, ln0_g, ln0_b)
    zq = [_inproj(hb, w_all, b_all, bsz, s, DIL_GROUPS[0][1], 3 * NA_WIDTH + 3 * DIL_OUT_WIDTH,
                  lambda j: jnp.where(j < na_blocks, j, na_blocks + (j - na_blocks) * ngroups))]
    for g in range(1, ngroups):
        zq.append(_inproj(hb, w_all, b_all, bsz, s, DIL_GROUPS[g][1], 3 * DIL_OUT_WIDTH,
                          lambda j, g=g: na_blocks + j * ngroups + g))
    na = _neighborhood_attention(zq[0].reshape(-1, bsz, s, HEAD_DIM), rpb[0])
    na = na.reshape(NA_HEADS, n, HEAD_DIM)
    dil = [_dilated_group(zq[g], DIL0_H if g == 0 else 0, g, slopes) for g in range(ngroups)]
    m = _mix(na, [o for o, _ in dil], [l for _, l in dil], hb,
             w_proj_a[0].astype(BF16), w_proj_b[0].astype(BF16), w_all, b_all, s)
    h1, h1p, logits = _outproj(m, x2, ln0_g, ln0_b, w_o[0].astype(BF16), b_o[0], ln1_g[0], ln1_b[0],
                               w_r_cat, b_r)
    dest, gate, slot_tok, block_e, block_next, n_active = _route(logits, n)
    ys = _experts(h1p, slot_tok, block_e, block_next, n_active, w_gate[0], w_up[0], w_down[0])
    out = _final(dest, ys, h1, gate, ln2_g[0], ln2_b[0])
    return out.reshape(bsz, s, d)
```

```python
import functools

import numpy as np
import jax
import jax.numpy as jnp
from jax import lax
from jax.experimental import pallas as pl
from jax.experimental.pallas import tpu as pltpu

F32 = jnp.float32
BF16 = jnp.bfloat16

D_MODEL = 2048
HEAD_DIM = 128
GRID_W = 64
NA_HEADS = 8
NA_KH = 8
NA_KW = 16
DIL_GROUPS = ((128, 1), (512, 4), (2048, 16))
DIL_HEADS_PER_GROUP = 4
DIL_HEADS = DIL_HEADS_PER_GROUP * len(DIL_GROUPS)
N_GROUPS = 8
EXPERTS_PER_GROUP = 8
N_EXPERTS = N_GROUPS * EXPERTS_PER_GROUP
TOP_K = 2
D_EXPERT = D_MODEL // 4
LN_EPS = 1e-5
DEPTH = 1
DN_ALPHA = (2 * DEPTH) ** 0.25
NA_WIDTH = NA_HEADS * HEAD_DIM
DIL_WIDTH = DIL_HEADS * HEAD_DIM
DIL_OUT_WIDTH = DIL_HEADS_PER_GROUP * HEAD_DIM
QKV_WIDTH = 3 * NA_WIDTH + 3 * DIL_WIDTH
QKV_HEADS = QKV_WIDTH // HEAD_DIM
LANES = 128
CHUNKS = D_MODEL // LANES
PACKED = CHUNKS // 2
MASK_VALUE = -1e30

QA_H, KA_H, VA_H = 0, NA_HEADS, 2 * NA_HEADS
DIL0_H = 3 * NA_HEADS

LN_TM = 1024
INPROJ_TM, INPROJ_TN = 2048, 512
SPLIT_STRIDE = 4
NA_QROWS = 4
NA_KROWS = 12
NA_NSUB = 16
DIL_NSUB = 16
DIL_TQ = 256
DIL_HALF = 64
DIL_TK = DIL_TQ + 2 * DIL_HALF
MIX_TM, MIX_TN = 512, 512
MIX_CHUNK = 256
OUT_TM = 512
ROUTER_PAD = 128
ROUTE_TM = 1024
MOE_TB = 128
FIN_TM = 512
CAST_VREGS = 32
GATHER_SLOTS = 3
WEIGHT_DMA_PRIORITY = 1
VMEM_LIMIT = 56 * 1024 * 1024


def _params(sem, limit=VMEM_LIMIT):
    return pltpu.CompilerParams(dimension_semantics=sem, vmem_limit_bytes=limit)


def _layer_norm(x, g, b):
    mu = jnp.mean(x, axis=-1, keepdims=True)
    xc = x - mu
    var = jnp.mean(xc * xc, axis=-1, keepdims=True)
    return xc * lax.rsqrt(var + LN_EPS) * g + b


HIGH_HALF = np.uint32(0xFFFF0000)


def _bf16_bits(x):
    return lax.bitcast_convert_type(x.astype(BF16).astype(F32), jnp.uint32)


def _pack_rows(ref, val):
    for c in range(PACKED):
        lo = _bf16_bits(val[:, c * LANES:(c + 1) * LANES])
        hi = _bf16_bits(val[:, (c + PACKED) * LANES:(c + PACKED + 1) * LANES])
        ref[c] = (lo >> 16) | (hi & HIGH_HALF)


def _unpack_rows(ref):
    words = [ref[c] for c in range(PACKED)]
    lo = [lax.bitcast_convert_type(w << 16, F32) for w in words]
    hi = [lax.bitcast_convert_type(w & HIGH_HALF, F32) for w in words]
    return jnp.concatenate(lo + hi, axis=-1)


def _tile_copies(hbm, row0, rows, vmem, sem, to_hbm, chunk_major=False):
    copies = []
    for c in range(PACKED):
        h = hbm.at[c, pl.ds(row0, rows)] if chunk_major else hbm.at[pl.ds(row0, rows), c]
        v = vmem.at[c]
        copies.append(pltpu.make_async_copy(v, h, sem) if to_hbm else pltpu.make_async_copy(h, v, sem))
    return copies


def _start_row_gather(idx_ref, base, step, count, src_hbm, dst, sem, both_threads, inline=False,
                      chunk_major=False):
    def start(j, c):
        for half in range(2):
            t = 2 * j + half
            row = idx_ref[base + t * step]
            src = src_hbm.at[:, row] if chunk_major else src_hbm.at[row]
            pltpu.make_async_copy(src, dst.at[:, t], sem).start(priority=half if both_threads else 0)
        return c

    if inline:
        for j in range(count // 2):
            start(j, 0)
    else:
        lax.fori_loop(0, count // 2, start, 0, unroll=2)


def _wait_row_gather(dst, sem):
    pltpu.make_async_copy(dst, dst, sem).wait()


def _ln0_kernel(x_ref, g_ref, b_ref, hb_ref):
    hb_ref[...] = _layer_norm(x_ref[...], g_ref[...], b_ref[...]).astype(BF16)


def _ln0(x, g, b):
    n, d = x.shape
    row = pl.BlockSpec((LN_TM, d), lambda i: (i, 0))
    vec = pl.BlockSpec((1, d), lambda i: (0, 0))
    return pl.pallas_call(
        _ln0_kernel,
        grid=(n // LN_TM,),
        in_specs=[row, vec, vec],
        out_specs=row,
        out_shape=jax.ShapeDtypeStruct((n, d), BF16),
        compiler_params=_params(("parallel",)),
        name="ln0",
    )(x, g.reshape(1, d), b.reshape(1, d))


def _inproj_kernel(hb_ref, w_ref, b_ref, o_ref, acc_s, mid_s, *, dil):
    acc = jnp.dot(hb_ref[...], w_ref[...], preferred_element_type=F32) + b_ref[...]
    if dil == 1:
        for c in range(INPROJ_TN // LANES):
            o_ref[c, 0, 0] = acc[:, c * LANES:(c + 1) * LANES].astype(BF16)
        return
    for c in range(INPROJ_TN // LANES):
        acc_s[c] = acc[:, c * LANES:(c + 1) * LANES]
    if dil <= SPLIT_STRIDE:
        for c in range(INPROJ_TN // LANES):
            for r in range(dil):
                o_ref[c, 0, r] = acc_s[c, pl.ds(r, INPROJ_TM // dil, stride=dil), :].astype(BF16)
        return
    outer = dil // SPLIT_STRIDE
    for c in range(INPROJ_TN // LANES):
        for a in range(SPLIT_STRIDE):
            mid_s[c, a] = acc_s[c, pl.ds(a, INPROJ_TM // SPLIT_STRIDE, stride=SPLIT_STRIDE), :]
    for c in range(INPROJ_TN // LANES):
        for a in range(SPLIT_STRIDE):
            for b in range(outer):
                o_ref[c, 0, b * SPLIT_STRIDE + a] = (
                    mid_s[c, a, pl.ds(b, INPROJ_TM // dil, stride=outer), :].astype(BF16))


def _inproj(hb, w, b, bsz, s, dil, width, col_block):
    n, d = hb.shape
    tiles = s // INPROJ_TM
    assert INPROJ_TM % (16 * dil) == 0 and s % INPROJ_TM == 0 and width % INPROJ_TN == 0
    assert dil <= SPLIT_STRIDE or dil % SPLIT_STRIDE == 0
    mid_rows = INPROJ_TM // SPLIT_STRIDE if dil > SPLIT_STRIDE else 8
    return pl.pallas_call(
        functools.partial(_inproj_kernel, dil=dil),
        grid=(n // INPROJ_TM, width // INPROJ_TN),
        in_specs=[pl.BlockSpec((INPROJ_TM, d), lambda i, j: (i, 0)),
                  pl.BlockSpec((d, INPROJ_TN), lambda i, j: (0, col_block(j))),
                  pl.BlockSpec((1, INPROJ_TN), lambda i, j: (0, col_block(j)))],
        out_specs=pl.BlockSpec((INPROJ_TN // LANES, 1, dil, INPROJ_TM // dil, LANES),
                               lambda i, j: (j, i // tiles, 0, i % tiles, 0)),
        out_shape=jax.ShapeDtypeStruct((width // LANES, bsz, dil, s // dil, LANES), BF16),
        scratch_shapes=[pltpu.VMEM((INPROJ_TN // LANES, INPROJ_TM, LANES), F32),
                        pltpu.VMEM((INPROJ_TN // LANES, SPLIT_STRIDE, mid_rows, LANES), F32)],
        compiler_params=_params(("parallel", "arbitrary")),
        name=f"inproj_{dil}",
    )(hb, w, b)


def _na_bias_tiles(rpb, rows):
    heads, n_dr, n_dc = rpb.shape
    qc = np.arange(GRID_W)[:, None]
    kc = np.arange(GRID_W)[None, :]
    qcs = np.clip(qc - NA_KW // 2, 0, GRID_W - NA_KW)
    v_col = (kc >= qcs) & (kc < qcs + NA_KW)
    dc = np.clip(kc - qc + NA_KW - 1, 0, n_dc - 1)
    onehot = (dc[None] == np.arange(n_dc)[:, None, None]) & v_col[None]
    toep = jnp.einsum('hrd,dqk->hrqk', rpb.astype(F32), jnp.asarray(onehot, F32),
                      precision=lax.Precision.HIGHEST)
    toep = jnp.where(jnp.asarray(v_col), toep, MASK_VALUE)
    masked = jnp.full((heads, 1, GRID_W, GRID_W), MASK_VALUE, F32)
    blocks = jnp.concatenate([toep, masked], axis=1)
    i = np.arange(NA_QROWS)[:, None]
    j = np.arange(NA_KROWS)[None, :]
    sel = []
    for r0, ks in ((0, 0), (2 * NA_QROWS, 2 * NA_QROWS - NA_KH // 2), (rows - NA_QROWS, rows - NA_KROWS)):
        r, krow = r0 + i, ks + j
        start = np.clip(r - NA_KH // 2, 0, rows - NA_KH)
        v_row = (krow >= start) & (krow < start + NA_KH)
        sel.append(np.where(v_row, krow - r + NA_KH - 1, n_dr))
    sel = np.stack(sel)
    pairs = sel.reshape(3, NA_QROWS, NA_KROWS // 2, 2)
    distinct = sorted({tuple(p) for p in pairs.reshape(-1, 2).tolist()})
    plan = [[[distinct.index(tuple(pairs[v, i, c])) for c in range(NA_KROWS // 2)]
             for i in range(NA_QROWS)] for v in range(3)]
    left = jnp.take(blocks, jnp.asarray([p[0] for p in distinct], jnp.int32), axis=1)
    right = jnp.take(blocks, jnp.asarray([p[1] for p in distinct], jnp.int32), axis=1)
    return jnp.concatenate([left, right], axis=-1), plan


def _attend_blocks(nblocks, scores, finish):
    scored, weighted = {}, {}
    for t in range(nblocks + 2):
        if t < nblocks:
            scored[t] = scores(t)
        if 0 <= t - 1 < nblocks:
            s, vw = scored.pop(t - 1)
            m = jnp.max(s, axis=-1, keepdims=True)
            p = jnp.exp(s - m)
            weighted[t - 1] = (p.astype(BF16), vw, m, jnp.sum(p, axis=-1, keepdims=True))
        if 0 <= t - 2 < nblocks:
            p, vw, m, l = weighted.pop(t - 2)
            finish(t - 2, jnp.dot(p, vw, preferred_element_type=F32) / l, m, l)


def _scores(q, kw, bias):
    return lax.dot_general(q, kw, (((1,), (1,)), ((), ())), preferred_element_type=F32) + bias


def _edge_variant(blk, nblk):
    return jnp.where(blk == 0, 0, jnp.where(blk == nblk - 1, 2, 1))


def _na_kernel(q_ref, k_ref, v_ref, tile_ref, o_ref, tab_ref, *, rows, plan):
    tq = NA_QROWS * GRID_W

    @pl.when(pl.program_id(2) == 0)
    def _():
        for v in range(3):
            for i in range(NA_QROWS):
                for c in range(NA_KROWS // 2):
                    tab_ref[v, 0, i * GRID_W:(i + 1) * GRID_W, c * 2 * GRID_W:(c + 1) * 2 * GRID_W] = (
                        tile_ref[0, plan[v][i][c]])

    def scores(u):
        blk = pl.program_id(2) * NA_NSUB + u
        ks = jnp.clip(blk * NA_QROWS - NA_KH // 2, 0, rows - NA_KROWS)
        off = pl.multiple_of(ks * GRID_W, GRID_W)
        kw = k_ref[0, 0, pl.ds(off, NA_KROWS * GRID_W), :]
        vw = v_ref[0, 0, pl.ds(off, NA_KROWS * GRID_W), :]
        bias = tab_ref[_edge_variant(blk, rows // NA_QROWS), 0]
        return _scores(q_ref[0, 0, u * tq:(u + 1) * tq, :], kw, bias), vw

    def finish(u, o, m, l):
        o_ref[0, 0, u * tq:(u + 1) * tq, :] = o.astype(BF16)

    _attend_blocks(NA_NSUB, scores, finish)


def _neighborhood_attention(zq4, rpb):
    _, bsz, s, _ = zq4.shape
    rows = s // GRID_W
    assert rows % (NA_QROWS * NA_NSUB) == 0 and rows >= NA_KROWS + NA_QROWS and NA_KROWS % 2 == 0
    tiles, plan = _na_bias_tiles(rpb, rows)
    tq = NA_QROWS * GRID_W * NA_NSUB
    return pl.pallas_call(
        functools.partial(_na_kernel, rows=rows, plan=plan),
        grid=(bsz, NA_HEADS, s // tq),
        in_specs=[pl.BlockSpec((1, 1, tq, HEAD_DIM), lambda b, h, i: (QA_H + h, b, i, 0)),
                  pl.BlockSpec((1, 1, s, HEAD_DIM), lambda b, h, i: (KA_H + h, b, 0, 0)),
                  pl.BlockSpec((1, 1, s, HEAD_DIM), lambda b, h, i: (VA_H + h, b, 0, 0)),
                  pl.BlockSpec((1,) + tiles.shape[1:], lambda b, h, i: (h, 0, 0, 0))],
        out_specs=pl.BlockSpec((1, 1, tq, HEAD_DIM), lambda b, h, i: (h, b, i, 0)),
        out_shape=jax.ShapeDtypeStruct((NA_HEADS, bsz, s, HEAD_DIM), BF16),
        scratch_shapes=[pltpu.VMEM((3, 1, NA_QROWS * GRID_W, NA_KROWS * GRID_W), F32)],
        compiler_params=_params(("arbitrary", "arbitrary", "arbitrary")),
        name="na_attn",
    )(zq4, zq4, zq4, tiles)


def _alibi_slopes(n):
    return np.array([2.0 ** (-8.0 * (i + 1) / n) for i in range(n)], dtype=np.float32)


def _dil_bias_tables(slopes, dil):
    qi = np.arange(DIL_TQ)[:, None]
    kj = np.arange(DIL_TK)[None, :]
    dist = np.stack([np.abs(kj - qi + shift) for shift in (0, -DIL_HALF, -2 * DIL_HALF)])
    dist = jnp.asarray(dist, F32)[:, None]
    penalty = jnp.asarray(slopes, F32)[None, :, None, None] * (dist * float(dil))
    return jnp.where(dist <= DIL_HALF, -penalty, MASK_VALUE)


def _dil_kernel(q_ref, k_ref, v_ref, tab_ref, o_ref, lse_ref, *, length, nsub, nres):
    def scores(u):
        res, sub = divmod(u, nsub)
        blk = pl.program_id(3) * nsub + sub
        ks = jnp.clip(blk * DIL_TQ - DIL_HALF, 0, length - DIL_TK)
        off = pl.multiple_of(ks, DIL_HALF)
        kw = k_ref[0, 0, res, pl.ds(off, DIL_TK), :]
        vw = v_ref[0, 0, res, pl.ds(off, DIL_TK), :]
        bias = tab_ref[_edge_variant(blk, length // DIL_TQ), 0]
        return _scores(q_ref[0, 0, res, sub * DIL_TQ:(sub + 1) * DIL_TQ, :], kw, bias), vw

    def finish(u, o, m, l):
        res, sub = divmod(u, nsub)
        rows = slice(sub * DIL_TQ, (sub + 1) * DIL_TQ)
        o_ref[0, 0, res, rows, :] = o.astype(BF16)
        lse_ref[0, 0, res, rows, :] = jnp.broadcast_to(m + jnp.log(l), (DIL_TQ, LANES))

    _attend_blocks(nres * nsub, scores, finish)


def _dilated_group(zq, base, g, slopes):
    window, dil = DIL_GROUPS[g]
    assert (window // 2) // dil == DIL_HALF
    _, bsz, _, length, _ = zq.shape
    nsub = min(DIL_NSUB, length // DIL_TQ)
    nres = min(dil, DIL_NSUB // nsub)
    assert zq.shape[2] == dil and length % (DIL_TQ * nsub) == 0 and length >= DIL_TK and dil % nres == 0
    gh = DIL_HEADS_PER_GROUP
    tabs = _dil_bias_tables(slopes[g * gh:(g + 1) * gh], dil)
    tq = DIL_TQ * nsub
    qspec = pl.BlockSpec((1, 1, nres, tq, HEAD_DIM), lambda b, h, r, n: (base + h, b, r, n, 0))
    kvspec = lambda off: pl.BlockSpec((1, 1, nres, length, HEAD_DIM),
                                      lambda b, h, r, n: (base + off + h, b, r, 0, 0))
    ospec = pl.BlockSpec((1, 1, nres, tq, HEAD_DIM), lambda b, h, r, n: (h, b, r, n, 0))
    return pl.pallas_call(
        functools.partial(_dil_kernel, length=length, nsub=nsub, nres=nres),
        grid=(bsz, gh, dil // nres, length // tq),
        in_specs=[qspec, kvspec(gh), kvspec(2 * gh),
                  pl.BlockSpec((3, 1, DIL_TQ, DIL_TK), lambda b, h, r, n: (0, h, 0, 0))],
        out_specs=[ospec, ospec],
        out_shape=[jax.ShapeDtypeStruct((gh, bsz, dil, length, HEAD_DIM), BF16),
                   jax.ShapeDtypeStruct((gh, bsz, dil, length, HEAD_DIM), F32)],
        compiler_params=_params(("parallel", "parallel", "parallel", "arbitrary")),
        name=f"dil_attn_{dil}",
    )(zq, zq, zq, tabs)


def _mix_kernel(na_ref, o1_ref, o2_ref, o3_ref, l1_ref, l2_ref, l3_ref, hb_ref,
                wpa_ref, wpb_ref, wga_ref, wgb_ref, bga_ref, bgb_ref, m_ref, na_s, dil_s, o_s, l_s, mid_s):
    @pl.when(pl.program_id(1) == 0)
    def _():
        for h in range(NA_HEADS):
            na_s[:, h * HEAD_DIM:(h + 1) * HEAD_DIM] = na_ref[h]
        for g, (o_ref, l_ref) in enumerate(((o1_ref, l1_ref), (o2_ref, l2_ref), (o3_ref, l3_ref))):
            dil = DIL_GROUPS[g][1]
            for h in range(DIL_HEADS_PER_GROUP):
                for which, (src, dst) in enumerate(((o_ref, o_s), (l_ref, l_s))):
                    if dil <= SPLIT_STRIDE:
                        for r in range(dil):
                            rows = slice(None) if dil == 1 else pl.ds(r, MIX_TM // dil, stride=dil)
                            dst[g, h, rows, :] = src[h, 0, r].astype(F32)
                        continue
                    outer = dil // SPLIT_STRIDE
                    for r in range(dil):
                        a, b = r % SPLIT_STRIDE, r // SPLIT_STRIDE
                        mid_s[h, which, a, pl.ds(b, MIX_TM // dil, stride=outer), :] = src[h, 0, r].astype(F32)
                    for a in range(SPLIT_STRIDE):
                        dst[g, h, pl.ds(a, MIX_TM // SPLIT_STRIDE, stride=SPLIT_STRIDE), :] = mid_s[h, which, a]
        for h in range(DIL_HEADS_PER_GROUP):
            ls = [l_s[g, h] for g in range(len(DIL_GROUPS))]
            mx = jnp.maximum(jnp.maximum(ls[0], ls[1]), ls[2])
            es = [jnp.exp(l - mx) for l in ls]
            den = es[0] + es[1] + es[2]
            acc = (es[0] * o_s[0, h] + es[1] * o_s[1, h] + es[2] * o_s[2, h]) / den
            dil_s[:, h * HEAD_DIM:(h + 1) * HEAD_DIM] = acc.astype(BF16)

    hb = hb_ref[...]
    na, dm = na_s[...], dil_s[...]

    def products(c):
        cols = slice(c * MIX_CHUNK, (c + 1) * MIX_CHUNK)
        return (jnp.dot(na, wpa_ref[:, cols], preferred_element_type=F32),
                jnp.dot(dm, wpb_ref[:, cols], preferred_element_type=F32),
                jnp.dot(hb, wga_ref[:, cols], preferred_element_type=F32) + bga_ref[:, cols],
                jnp.dot(hb, wgb_ref[:, cols], preferred_element_type=F32) + bgb_ref[:, cols])

    nchunks = MIX_TN // MIX_CHUNK
    pending = products(0)
    for c in range(nchunks):
        following = products(c + 1) if c + 1 < nchunks else None
        ya, yb, ga, gb = pending
        m_ref[:, c * MIX_CHUNK:(c + 1) * MIX_CHUNK] = (
            jax.nn.sigmoid(ga) * ya + jax.nn.sigmoid(gb) * yb).astype(BF16)
        pending = following


def _mix(na, dil_o, dil_lse, hb, wpa, wpb, w_all, b_all, s):
    n, d = hb.shape
    tm, tn = MIX_TM, MIX_TN
    tiles = s // tm
    gh = DIL_HEADS_PER_GROUP
    ngroups = len(DIL_GROUPS)
    assert all(tm % (16 * dil) == 0 for _, dil in DIL_GROUPS) and s % tm == 0
    col = lambda k: pl.BlockSpec((k, tn), lambda i, j: (0, j))
    ga0, gb0 = QKV_WIDTH // tn, (QKV_WIDTH + d) // tn
    gate = lambda k, first: pl.BlockSpec((k, tn), lambda i, j: (0, first + j))
    grp = lambda dil: pl.BlockSpec((gh, 1, dil, tm // dil, HEAD_DIM),
                                   lambda i, j: (0, i // tiles, 0, i % tiles, 0))
    groups = [grp(dil) for _, dil in DIL_GROUPS]
    return pl.pallas_call(
        _mix_kernel,
        grid=(n // tm, d // tn),
        in_specs=[pl.BlockSpec((NA_HEADS, tm, HEAD_DIM), lambda i, j: (0, i, 0)), *groups, *groups,
                  pl.BlockSpec((tm, d), lambda i, j: (i, 0)),
                  col(NA_WIDTH), col(DIL_OUT_WIDTH),
                  gate(d, ga0), gate(d, gb0), gate(1, ga0), gate(1, gb0)],
        out_specs=pl.BlockSpec((tm, tn), lambda i, j: (i, j)),
        out_shape=jax.ShapeDtypeStruct((n, d), BF16),
        scratch_shapes=[pltpu.VMEM((tm, NA_WIDTH), BF16), pltpu.VMEM((tm, DIL_OUT_WIDTH), BF16),
                        pltpu.VMEM((ngroups, gh, tm, HEAD_DIM), F32),
                        pltpu.VMEM((ngroups, gh, tm, HEAD_DIM), F32),
                        pltpu.VMEM((gh, 2, SPLIT_STRIDE, tm // SPLIT_STRIDE, HEAD_DIM), F32)],
        compiler_params=_params(("parallel", "arbitrary")),
        name="mix",
    )(na, *dil_o, *dil_lse, hb, wpa, wpb, w_all, w_all, b_all, b_all)


def _outproj_kernel(m_ref, x_ref, g0_ref, b0_ref, wo_ref, bo_ref, g_ref, b_ref, wr_ref, br_ref,
                    h1_ref, h1p_hbm, lg_ref, hbuf, sem):
    i = pl.program_id(0)
    slot = i % 2

    def out_copies(step, sl):
        return _tile_copies(h1p_hbm, step * OUT_TM, OUT_TM, hbuf.at[sl], sem.at[sl], True)

    mix = jnp.dot(m_ref[...], wo_ref[...], preferred_element_type=F32) + bo_ref[...]
    h = _layer_norm(x_ref[...], g0_ref[...], b0_ref[...])
    h1 = _layer_norm(DN_ALPHA * h + mix, g_ref[...], b_ref[...])
    h1_ref[...] = h1

    @pl.when(i >= 2)
    def _():
        for cp in out_copies(i - 2, slot):
            cp.wait()

    _pack_rows(hbuf.at[slot], h1)
    for cp in out_copies(i, slot):
        cp.start()

    @pl.when(i == pl.num_programs(0) - 1)
    def _():
        for cp in out_copies(i, slot):
            cp.wait()

        @pl.when(i >= 1)
        def _():
            for cp in out_copies(i - 1, 1 - slot):
                cp.wait()

    hi = h1.astype(BF16)
    lo = (h1 - hi.astype(F32)).astype(BF16)
    both = jnp.dot(hi, wr_ref[...], preferred_element_type=F32)
    lg = (both[:, :ROUTER_PAD] + both[:, ROUTER_PAD:]
          + jnp.dot(lo, wr_ref[:, :ROUTER_PAD], preferred_element_type=F32))
    lg_ref[...] = lg + br_ref[...]


def _outproj(m, x, g0, b0, wo, bo, g, b, wr, br):
    n, d = x.shape
    tm = OUT_TM
    row = lambda w: pl.BlockSpec((tm, w), lambda i: (i, 0))
    full = lambda r, c: pl.BlockSpec((r, c), lambda i: (0, 0), pipeline_mode=pl.Buffered(1))
    return pl.pallas_call(
        _outproj_kernel,
        grid=(n // tm,),
        in_specs=[row(d), row(d), full(1, d), full(1, d), full(d, d), full(1, d), full(1, d), full(1, d),
                  full(d, 2 * ROUTER_PAD), full(1, ROUTER_PAD)],
        out_specs=[row(d), pl.BlockSpec(memory_space=pl.ANY), row(ROUTER_PAD)],
        out_shape=[jax.ShapeDtypeStruct((n, d), F32),
                   jax.ShapeDtypeStruct((n, PACKED, LANES), jnp.uint32),
                   jax.ShapeDtypeStruct((n, ROUTER_PAD), F32)],
        scratch_shapes=[pltpu.VMEM((2, PACKED, tm, LANES), jnp.uint32), pltpu.SemaphoreType.DMA((2,))],
        compiler_params=_params(("arbitrary",)),
        name="outproj",
    )(m, x, g0.reshape(1, d), b0.reshape(1, d), wo, bo.reshape(1, d), g.reshape(1, d), b.reshape(1, d),
      wr, br)


def _first_argmax(vals, lane_f):
    top = jnp.max(vals, axis=-1, keepdims=True)
    idx = jnp.min(jnp.where(vals == top, lane_f, float(LANES)), axis=-1, keepdims=True)
    return top, idx


def _route_kernel(lg_ref, gate_ref, dest_ref, cnt_ref, base_s, start_s):
    phase = pl.program_id(0)

    @pl.when(pl.program_id(1) == 0)
    def _():
        @pl.when(phase == 1)
        def _():
            counts = base_s[...]
            cnt_ref[...] = counts
            blocks = jnp.ceil(counts * (1.0 / MOE_TB))
            before = (lax.broadcasted_iota(jnp.int32, (ROUTER_PAD, ROUTER_PAD), 0)
                      < lax.broadcasted_iota(jnp.int32, (ROUTER_PAD, ROUTER_PAD), 1)).astype(BF16)
            starts = jnp.dot(jnp.broadcast_to(blocks, (8, ROUTER_PAD)).astype(BF16), before,
                             preferred_element_type=F32)
            start_s[...] = starts[0:1] * float(MOE_TB)

        base_s[...] = jnp.zeros(base_s.shape, F32)

    lg = lg_ref[...]
    lane = lax.broadcasted_iota(jnp.int32, lg.shape, 1)
    lane_f = lane.astype(F32)
    g_mask = lane < N_GROUPS
    g_top, g_sel = _first_argmax(jnp.where(g_mask, lg, MASK_VALUE), lane_f)
    g_prob = 1.0 / jnp.sum(jnp.where(g_mask, jnp.exp(lg - g_top), 0.0), axis=-1, keepdims=True)
    first = N_GROUPS + g_sel * EXPERTS_PER_GROUP
    e_mask = (lane_f >= first) & (lane_f < first + EXPERTS_PER_GROUP)
    el = jnp.where(e_mask, lg, MASK_VALUE)
    v0, i0 = _first_argmax(el, lane_f)
    v1, i1 = _first_argmax(jnp.where(lane_f == i0, MASK_VALUE, el), lane_f)
    e1 = jnp.exp(v1 - v0)
    w0 = g_prob / (1.0 + e1)
    w1 = g_prob * e1 / (1.0 + e1)

    onehots = [lane_f == idx - N_GROUPS for idx in (i0, i1)]

    @pl.when(phase == 0)
    def _():
        base_s[...] = base_s[...] + sum(jnp.sum(oh.astype(F32), axis=0, keepdims=True) for oh in onehots)

    @pl.when(phase == 1)
    def _():
        tm = lg.shape[0]
        tri = (lax.broadcasted_iota(jnp.int32, (tm, tm), 1)
               < lax.broadcasted_iota(jnp.int32, (tm, tm), 0)).astype(BF16)
        dests = []
        for onehot in onehots:
            before = jnp.dot(tri, onehot.astype(BF16), preferred_element_type=F32) + base_s[...]
            dests.append(jnp.sum(jnp.where(onehot, before + start_s[...], 0.0), axis=-1, keepdims=True))
            base_s[...] = base_s[...] + jnp.sum(onehot.astype(F32), axis=0, keepdims=True)
        gate_ref[...] = jnp.where(lane == 0, w0, jnp.where(lane == 1, w1, 0.0))
        cols = jnp.where(lane == 0, dests[0], jnp.where(lane == 1, dests[1], 0.0))
        dest_ref[...] = jnp.transpose(cols)[0:8, :].astype(jnp.int32)


def _invert_kernel(dest_ref, lo_ref, hi_ref, tok_ref, *, n):
    def clear_range(e, c):
        def clear(s, c2):
            tok_ref[s] = 0
            return c2

        lax.fori_loop(lo_ref[e], hi_ref[e], clear, 0)
        return c

    lax.fori_loop(0, lo_ref.shape[0], clear_range, 0)
    def place(t, c):
        for k in range(TOP_K):
            tok_ref[dest_ref[k * n + t]] = t
        return c

    lax.fori_loop(0, n, place, 0, unroll=8)


def _invert_slots(dest, unused_lo, unused_hi, n, slots):
    assert dest.shape[0] == TOP_K * n
    return pl.pallas_call(
        functools.partial(_invert_kernel, n=n),
        grid_spec=pltpu.PrefetchScalarGridSpec(
            num_scalar_prefetch=3, grid=(1,), in_specs=[],
            out_specs=pl.BlockSpec(memory_space=pltpu.SMEM)),
        out_shape=jax.ShapeDtypeStruct((slots,), jnp.int32),
        compiler_params=_params(("arbitrary",)),
        name="invert_slots",
    )(dest, unused_lo, unused_hi)


def _route(logits, n):
    tm = ROUTE_TM
    one = pl.BlockSpec((1, ROUTER_PAD), lambda p, i: (0, 0))
    gate, dest, counts = pl.pallas_call(
        _route_kernel,
        grid=(2, n // tm),
        in_specs=[pl.BlockSpec((tm, ROUTER_PAD), lambda p, i: (i, 0))],
        out_specs=[pl.BlockSpec((tm, ROUTER_PAD), lambda p, i: (i * p, 0)),
                   pl.BlockSpec((8, tm), lambda p, i: (0, i * p)), one],
        out_shape=[jax.ShapeDtypeStruct((n, ROUTER_PAD), F32),
                   jax.ShapeDtypeStruct((8, n), jnp.int32),
                   jax.ShapeDtypeStruct((1, ROUTER_PAD), F32)],
        scratch_shapes=[pltpu.VMEM((1, ROUTER_PAD), F32), pltpu.VMEM((1, ROUTER_PAD), F32)],
        compiler_params=_params(("arbitrary", "arbitrary")),
        name="route",
    )(logits)
    gate = gate[:, :TOP_K]
    dest = dest[:TOP_K].reshape(TOP_K * n)
    a = n * TOP_K
    counts = counts[0, :N_EXPERTS].astype(jnp.int32)
    pcounts = (counts + MOE_TB - 1) // MOE_TB * MOE_TB
    pends = jnp.cumsum(pcounts)
    nb = a // MOE_TB + N_EXPERTS
    slots = nb * MOE_TB
    unused_lo = jnp.concatenate([pends - pcounts + counts, pends[-1:]])
    unused_hi = jnp.concatenate([pends, jnp.full((1,), slots, jnp.int32)])
    slot_tok = _invert_slots(dest, unused_lo, unused_hi, n, slots)
    block_start = jnp.arange(nb, dtype=jnp.int32) * MOE_TB
    block_e = jnp.minimum(jnp.sum((pends[None, :] <= block_start[:, None]).astype(jnp.int32), axis=1),
                          N_EXPERTS - 1)
    n_active = (pends[-1] // MOE_TB).astype(jnp.int32).reshape(1)
    ids = jnp.arange(N_EXPERTS, dtype=jnp.int32)
    later = (counts[None, :] > 0) & (ids[None, :] > ids[:, None])
    nxt = jnp.min(jnp.where(later, ids[None, :], N_EXPERTS), axis=1)
    nxt = jnp.where(nxt == N_EXPERTS, -1, nxt)
    block_next = jnp.sum(jnp.where(block_e[:, None] == ids[None, :], nxt[None, :], 0), axis=1).astype(jnp.int32)
    return dest, gate, slot_tok, block_e, block_next, n_active


def _cast_weights(src, dst):
    rows, cols = src.shape
    step = CAST_VREGS * 8 * LANES // cols

    def body(r, c):
        sl = pl.ds(pl.multiple_of(r * step, step), step)
        dst[sl, :] = src[sl, :].astype(BF16)
        return c

    lax.fori_loop(0, rows // step, body, 0, unroll=2)


def _expert_kernel(be_ref, nx_ref, nact_ref, tok_ref, h1p_hbm, wg_hbm, wu_hbm, wd_hbm, ys_hbm,
                   xbuf, ybuf, wg_f, wu_f, wd_f, wg_s, wu_s, wd_s, gsem, osem, wsem):
    i = pl.program_id(0)
    slot = i % 2
    xslot = i % GATHER_SLOTS
    nact = nact_ref[0]

    def gather(step, inline=False):
        sl = step % GATHER_SLOTS
        _start_row_gather(tok_ref, step * MOE_TB, 1, MOE_TB, h1p_hbm, xbuf.at[sl], gsem.at[sl], False,
                          inline)

    def out_copies(step, sl):
        return _tile_copies(ys_hbm, step * MOE_TB, MOE_TB, ybuf.at[sl], osem.at[sl], True, chunk_major=True)

    def weight_copies(e):
        return [pltpu.make_async_copy(src.at[e], dst, wsem)
                for src, dst in ((wg_hbm, wg_f), (wu_hbm, wu_f), (wd_hbm, wd_f))]

    @pl.when((i == 0) & (nact > 0))
    def _():
        for cp in weight_copies(be_ref[0]):
            cp.start(priority=WEIGHT_DMA_PRIORITY)
        gather(0)

    @pl.when((i == 0) & (nact > 1))
    def _():
        gather(1)

    @pl.when(i >= 2)
    def _():
        for cp in out_copies(i - 2, slot):
            cp.wait()

    @pl.when(i < nact)
    def _():
        expert = be_ref[i]

        @pl.when((i == 0) | (expert != be_ref[jnp.maximum(i - 1, 0)]))
        def _():
            for cp in weight_copies(expert):
                cp.wait()
            _cast_weights(wg_f, wg_s)
            _cast_weights(wu_f, wu_s)
            _cast_weights(wd_f, wd_s)

            @pl.when(nx_ref[i] >= 0)
            def _():
                for cp in weight_copies(nx_ref[i]):
                    cp.start(priority=WEIGHT_DMA_PRIORITY)

        _wait_row_gather(xbuf.at[xslot], gsem.at[xslot])

        def block(prefetch):
            if prefetch:
                gather(i + 2, inline=True)
            xb = _unpack_rows(xbuf.at[xslot]).astype(BF16)
            gate = jnp.dot(xb, wg_s[...], preferred_element_type=F32)
            up = jnp.dot(xb, wu_s[...], preferred_element_type=F32)
            hid = (jax.nn.silu(gate) * up).astype(BF16)
            _pack_rows(ybuf.at[slot], jnp.dot(hid, wd_s[...], preferred_element_type=F32))

        pl.when(i + 2 < nact)(functools.partial(block, True))
        pl.when(i + 2 >= nact)(functools.partial(block, False))

    @pl.when(i >= nact)
    def _():
        ybuf[slot] = jnp.zeros(ybuf.shape[1:], jnp.uint32)

    for cp in out_copies(i, slot):
        cp.start()

    @pl.when(i == pl.num_programs(0) - 1)
    def _():
        for cp in out_copies(i, slot):
            cp.wait()

        @pl.when(i >= 1)
        def _():
            for cp in out_copies(i - 1, 1 - slot):
                cp.wait()


def _experts(h1p, slot_tok, block_e, block_next, n_active, w_gate, w_up, w_down):
    d = D_MODEL
    slots = slot_tok.shape[0]
    nb = slots // MOE_TB
    any_spec = pl.BlockSpec(memory_space=pl.ANY)
    grid_spec = pltpu.PrefetchScalarGridSpec(
        num_scalar_prefetch=4,
        grid=(nb,),
        in_specs=[any_spec, any_spec, any_spec, any_spec],
        out_specs=any_spec,
        scratch_shapes=[pltpu.VMEM((GATHER_SLOTS, PACKED, MOE_TB, LANES), jnp.uint32),
                        pltpu.VMEM((2, PACKED, MOE_TB, LANES), jnp.uint32),
                        pltpu.VMEM((d, D_EXPERT), F32), pltpu.VMEM((d, D_EXPERT), F32),
                        pltpu.VMEM((D_EXPERT, d), F32),
                        pltpu.VMEM((d, D_EXPERT), BF16), pltpu.VMEM((d, D_EXPERT), BF16),
                        pltpu.VMEM((D_EXPERT, d), BF16),
                        pltpu.SemaphoreType.DMA((GATHER_SLOTS,)), pltpu.SemaphoreType.DMA((2,)),
                        pltpu.SemaphoreType.DMA(())])
    return pl.pallas_call(
        _expert_kernel,
        grid_spec=grid_spec,
        out_shape=jax.ShapeDtypeStruct((PACKED, slots, LANES), jnp.uint32),
        compiler_params=_params(("arbitrary",)),
        name="experts",
    )(block_e, block_next, n_active, slot_tok, h1p, w_gate, w_up, w_down)


def _final_kernel(dest_ref, ys_hbm, h1_ref, gate_ref, g_ref, b_ref, o_ref, ybuf, gsem):
    i = pl.program_id(0)
    steps = pl.num_programs(0)
    slot = i % GATHER_SLOTS

    def fetch(step, inline=False):
        sl = step % GATHER_SLOTS
        for k in range(TOP_K):
            _start_row_gather(dest_ref, k * steps * FIN_TM + step * FIN_TM, 1, FIN_TM, ys_hbm,
                              ybuf.at[sl, k], gsem.at[sl], True, inline, chunk_major=True)

    @pl.when(i == 0)
    def _():
        fetch(0)

    @pl.when((i == 0) & (steps > 1))
    def _():
        fetch(1)

    _wait_row_gather(ybuf.at[slot], gsem.at[slot])

    def combine(prefetch):
        if prefetch:
            fetch(i + 2, inline=True)
        gate = gate_ref[...]
        ffn = (_unpack_rows(ybuf.at[slot, 0]) * gate[:, 0:1]
               + _unpack_rows(ybuf.at[slot, 1]) * gate[:, 1:2])
        o_ref[...] = _layer_norm(DN_ALPHA * h1_ref[...] + ffn, g_ref[...], b_ref[...])

    pl.when(i + 2 < steps)(functools.partial(combine, True))
    pl.when(i + 2 >= steps)(functools.partial(combine, False))


def _final(dest, ys, h1, gate, g, b):
    n, d = h1.shape
    grid_spec = pltpu.PrefetchScalarGridSpec(
        num_scalar_prefetch=1,
        grid=(n // FIN_TM,),
        in_specs=[pl.BlockSpec(memory_space=pl.ANY),
                  pl.BlockSpec((FIN_TM, d), lambda i, ds: (i, 0)),
                  pl.BlockSpec((FIN_TM, TOP_K), lambda i, ds: (i, 0)),
                  pl.BlockSpec((1, d), lambda i, ds: (0, 0)),
                  pl.BlockSpec((1, d), lambda i, ds: (0, 0))],
        out_specs=pl.BlockSpec((FIN_TM, d), lambda i, ds: (i, 0)),
        scratch_shapes=[pltpu.VMEM((GATHER_SLOTS, TOP_K, PACKED, FIN_TM, LANES), jnp.uint32),
                        pltpu.SemaphoreType.DMA((GATHER_SLOTS,))])
    return pl.pallas_call(
        _final_kernel,
        grid_spec=grid_spec,
        out_shape=jax.ShapeDtypeStruct((n, d), F32),
        compiler_params=_params(("arbitrary",)),
        name="final",
    )(dest, ys, h1, gate, g.reshape(1, d), b.reshape(1, d))


def kernel(x, ln0_g, ln0_b, w_in, b_in, rpb, w_proj_a, w_proj_b, w_o, b_o, ln1_g, ln1_b,
           w_router_group, b_router_group, w_router_expert, b_router_expert,
           w_gate, w_up, w_down, ln2_g, ln2_b):
    bsz, s, d = x.shape
    n = bsz * s
    assert d == D_MODEL and w_in.shape[0] == DEPTH
    scale = HEAD_DIM ** -0.5

    col_scale = np.ones((w_in.shape[2],), np.float32)
    col_scale[:NA_WIDTH] = scale
    col_scale[3 * NA_WIDTH:3 * NA_WIDTH + DIL_WIDTH] = scale
    w_all = (w_in[0] * col_scale).astype(BF16)
    b_all = (b_in[0] * col_scale).reshape(1, -1)
    assert DIL_OUT_WIDTH == INPROJ_TN
    na_blocks, ngroups = 3 * NA_WIDTH // INPROJ_TN, len(DIL_GROUPS)
    w_r = jnp.concatenate([w_router_group[0], w_router_expert[0]], axis=1)
    w_r = jnp.pad(w_r, ((0, 0), (0, ROUTER_PAD - w_r.shape[1])))
    w_r_hi = w_r.astype(BF16)
    w_r_cat = jnp.concatenate([w_r_hi, (w_r - w_r_hi.astype(F32)).astype(BF16)], axis=1)
    b_r = jnp.pad(jnp.concatenate([b_router_group[0], b_router_expert[0]]),
                  (0, ROUTER_PAD - N_GROUPS - N_EXPERTS)).reshape(1, ROUTER_PAD)
    slopes = _alibi_slopes(DIL_HEADS)

    x2 = x.reshape(n, d)
    hb = _ln0(x2, ln0_g, ln0_b)
    zq = [_inproj(hb, w_all, b_all, bsz, s, DIL_GROUPS[0][1], 3 * NA_WIDTH + 3 * DIL_OUT_WIDTH,
                  lambda j: jnp.where(j < na_blocks, j, na_blocks + (j - na_blocks) * ngroups))]
    for g in range(1, ngroups):
        zq.append(_inproj(hb, w_all, b_all, bsz, s, DIL_GROUPS[g][1], 3 * DIL_OUT_WIDTH,
                          lambda j, g=g: na_blocks + j * ngroups + g))
    na = _neighborhood_attention(zq[0].reshape(-1, bsz, s, HEAD_DIM), rpb[0])
    na = na.reshape(NA_HEADS, n, HEAD_DIM)
    dil = [_dilated_group(zq[g], DIL0_H if g == 0 else 0, g, slopes) for g in range(ngroups)]
    m = _mix(na, [o for o, _ in dil], [l for _, l in dil], hb,
             w_proj_a[0].astype(BF16), w_proj_b[0].astype(BF16), w_all, b_all, s)
    h1, h1p, logits = _outproj(m, x2, ln0_g, ln0_b, w_o[0].astype(BF16), b_o[0], ln1_g[0], ln1_b[0],
                               w_r_cat, b_r)
    dest, gate, slot_tok, block_e, block_next, n_active = _route(logits, n)
    ys = _experts(h1p, slot_tok, block_e, block_next, n_active, w_gate[0], w_up[0], w_down[0])
    out = _final(dest, ys, h1, gate, ln2_g[0], ln2_b[0])
    return out.reshape(bsz, s, d)
```
